```python
import math
import jax, jax.numpy as jnp
from jax import lax
import numpy as np

D_MODEL = 2048
BATCH = 8
SEQ = 2048
DEPTH = 1

GRID_W = 64
NA_HEADS = 8
NA_HEAD_DIM = 128
NA_WIN_ROWS = 8
NA_WIN_COLS = 16
NA_WIDTH = NA_HEADS * NA_HEAD_DIM
MLA_HEADS = 8
MLA_Q_RANK = 512
MLA_KV_RANK = 512
MLA_NOPE_DIM = 128
MLA_ROPE_DIM = 64
MLA_V_DIM = 128
MLA_QK_DIM = MLA_NOPE_DIM + MLA_ROPE_DIM
MLA_WIDTH = MLA_HEADS * MLA_V_DIM
ROPE_THETA = 10000.0
Q_BLOCK = 128
D_FF = 5632
PL_DIM = 256
NORM_EPS = 1e-6
NEG_INF = -1e30
IN_SIZES = (NA_WIDTH, NA_WIDTH, NA_WIDTH, MLA_Q_RANK, MLA_KV_RANK, MLA_ROPE_DIM, D_MODEL, D_MODEL)
N_IN = NA_WIDTH * 3 + MLA_Q_RANK + MLA_KV_RANK + MLA_ROPE_DIM + 2 * D_MODEL

kernel_name = "hybrid_na2d_mla_macaron_encoder"


def rmsnorm(x, g):
    xf = x.astype(jnp.float32)
    y = xf * lax.rsqrt(jnp.mean(xf * xf, axis=-1, keepdims=True) + NORM_EPS)
    return (y * g.astype(jnp.float32)).astype(x.dtype)


def swiglu(x, w_gate, w_up, w_down):
    return (jax.nn.silu(x @ w_gate) * (x @ w_up)) @ w_down


def split_points():
    pts, acc = [], 0
    for s in IN_SIZES[:-1]:
        acc += s
        pts.append(acc)
    return pts


def rope_tables(seq_len, dim):
    pos = jnp.arange(seq_len, dtype=jnp.float32)
    inv_freq = 1.0 / (ROPE_THETA ** (jnp.arange(0, dim, 2, dtype=jnp.float32) / dim))
    ang = pos[:, None] * inv_freq[None, :]
    return jnp.cos(ang), jnp.sin(ang)


def apply_rope(x, cos, sin):
    half = x.shape[-1] // 2
    x1, x2 = x[..., :half], x[..., half:]
    cos = cos.astype(x.dtype)
    sin = sin.astype(x.dtype)
    return jnp.concatenate([x1 * cos - x2 * sin, x2 * cos + x1 * sin], axis=-1)


def neighbourhood_attention(q, k, v, rpb):
    B, S, _ = q.shape
    rows = S // GRID_W
    kh = min(NA_WIN_ROWS, rows)
    kw = NA_WIN_COLS
    grid = lambda t: t.reshape(B, rows, GRID_W, NA_HEADS, NA_HEAD_DIM)
    qg, kg, vg = grid(q), grid(k), grid(v)
    cols = np.arange(GRID_W)
    col_start = np.clip(cols - kw // 2, 0, GRID_W - kw)
    col_mask = (cols[None, :] >= col_start[:, None]) & (cols[None, :] < col_start[:, None] + kw)
    dc_idx = np.clip(cols[None, :] - cols[:, None], -(kw - 1), kw - 1) + (kw - 1)
    col_mask = jnp.asarray(col_mask)[:, None, :]
    dc_idx = jnp.asarray(dc_idx)
    scale = NA_HEAD_DIM ** -0.5

    def row_block(args):
        q_row, r = args
        rs = jnp.clip(r - kh // 2, 0, rows - kh)
        k_rows = lax.dynamic_slice_in_dim(kg, rs, kh, axis=1)
        v_rows = lax.dynamic_slice_in_dim(vg, rs, kh, axis=1)
        dr = rs + jnp.arange(kh) - r
        bias = rpb[:, dr + (NA_WIN_ROWS - 1)][:, :, dc_idx]
        bias = bias.transpose(0, 2, 1, 3).astype(jnp.float32)
        s = jnp.einsum('bqhd,bikhd->bhqik', q_row, k_rows).astype(jnp.float32) * scale + bias
        s = jnp.where(col_mask, s, NEG_INF)
        pr = jax.nn.softmax(s.reshape(B, NA_HEADS, GRID_W, kh * GRID_W), axis=-1)
        pr = pr.reshape(s.shape).astype(v.dtype)
        return jnp.einsum('bhqik,bikhd->bqhd', pr, v_rows)

    o = lax.map(row_block, (qg.swapaxes(0, 1), jnp.arange(rows)))
    return o.swapaxes(0, 1).reshape(B, S, NA_WIDTH)


def mla_attention(q_lat, kv_lat, k_rope_in, q_a_norm, w_uq, kv_a_norm, w_ukv):
    B, S, _ = q_lat.shape
    cq = rmsnorm(q_lat, q_a_norm)
    q = (cq @ w_uq).reshape(B, S, MLA_HEADS, MLA_QK_DIM)
    q_nope, q_rope = q[..., :MLA_NOPE_DIM], q[..., MLA_NOPE_DIM:]
    ckv = rmsnorm(kv_lat, kv_a_norm)
    kv = (ckv @ w_ukv).reshape(B, S, MLA_HEADS, MLA_NOPE_DIM + MLA_V_DIM)
    k_nope, v = kv[..., :MLA_NOPE_DIM], kv[..., MLA_NOPE_DIM:]
    cos, sin = rope_tables(S, MLA_ROPE_DIM)
    q_rope = apply_rope(q_rope, cos[:, None, :], sin[:, None, :])
    k_rope = apply_rope(k_rope_in, cos, sin)
    scale = MLA_QK_DIM ** -0.5
    nb = S // Q_BLOCK
    to_blocks = lambda t: t.reshape(B, nb, Q_BLOCK, *t.shape[2:]).swapaxes(0, 1)

    def q_block(args):
        qn, qr = args
        s = jnp.einsum('bqhd,bkhd->bhqk', qn, k_nope) + jnp.einsum('bqhr,bkr->bhqk', qr, k_rope)
        pr = jax.nn.softmax(s.astype(jnp.float32) * scale, axis=-1).astype(v.dtype)
        return jnp.einsum('bhqk,bkhd->bqhd', pr, v)

    o = lax.map(q_block, (to_blocks(q_nope), to_blocks(q_rope)))
    return o.swapaxes(0, 1).reshape(B, S, MLA_WIDTH)


def _fwd_setup_inputs(seed: int = 0) -> dict:
    key = jax.random.key(seed)
    ks = iter(jax.random.split(key, 32))
    f32 = jnp.float32
    w = lambda shape, fan_in: jax.random.normal(next(ks), shape, f32) * (fan_in ** -0.5)
    gain = lambda shape: 1.0 + 0.01 * jax.random.normal(next(ks), shape, f32)
    L = DEPTH
    return {
        "x": jax.random.normal(next(ks), (BATCH, SEQ, D_MODEL), f32),
        "p": jax.random.normal(next(ks), (DEPTH, BATCH, SEQ, PL_DIM), f32),
        "ffn1_norm": gain((L, D_MODEL)),
        "ffn1_w_gate": w((L, D_MODEL, D_FF), D_MODEL),
        "ffn1_w_up": w((L, D_MODEL, D_FF), D_MODEL),
        "ffn1_w_down": w((L, D_FF, D_MODEL), D_FF),
        "mix_norm": gain((L, D_MODEL)),
        "w_in": w((L, D_MODEL, N_IN), D_MODEL),
        "q_a_norm": gain((L, MLA_Q_RANK)),
        "w_uq": w((L, MLA_Q_RANK, MLA_HEADS * MLA_QK_DIM), MLA_Q_RANK),
        "kv_a_norm": gain((L, MLA_KV_RANK)),
        "w_ukv": w((L, MLA_KV_RANK, MLA_HEADS * (MLA_NOPE_DIM + MLA_V_DIM)), MLA_KV_RANK),
        "na_rpb": 0.02 * jax.random.normal(next(ks), (L, NA_HEADS, 2 * NA_WIN_ROWS - 1, 2 * NA_WIN_COLS - 1), f32),
        "w_branch_a": w((L, NA_WIDTH, D_MODEL), NA_WIDTH),
        "w_branch_b": w((L, MLA_WIDTH, D_MODEL), MLA_WIDTH),
        "w_out": w((L, D_MODEL, D_MODEL), D_MODEL),
        "ffn2_norm": gain((L, D_MODEL)),
        "ffn2_w_gate": w((L, D_MODEL, D_FF), D_MODEL),
        "ffn2_w_up": w((L, D_MODEL, D_FF), D_MODEL),
        "ffn2_w_down": w((L, D_FF, D_MODEL), D_FF),
        "pl_norm": gain((L, D_MODEL)),
        "w_pl": w((L, PL_DIM, D_MODEL), PL_DIM),
        "w_pl_gate": w((L, D_MODEL, D_MODEL), D_MODEL),
        "final_norm": gain((D_MODEL,)),
    }


def _fwd_reference(x, p, ffn1_norm, ffn1_w_gate, ffn1_w_up, ffn1_w_down, mix_norm, w_in,
              q_a_norm, w_uq, kv_a_norm, w_ukv, na_rpb, w_branch_a, w_branch_b, w_out,
              ffn2_norm, ffn2_w_gate, ffn2_w_up, ffn2_w_down, pl_norm, w_pl, w_pl_gate,
              final_norm):
    pts = split_points()
    h = x
    for i in range(DEPTH):
        h = h + 0.5 * swiglu(rmsnorm(h, ffn1_norm[i]), ffn1_w_gate[i], ffn1_w_up[i], ffn1_w_down[i])
        u = rmsnorm(h, mix_norm[i])
        z = u @ w_in[i]
        na_q, na_k, na_v, q_lat, kv_lat, k_rope, gate_a, gate_b = jnp.split(z, pts, axis=-1)
        y_a = neighbourhood_attention(na_q, na_k, na_v, na_rpb[i]) @ w_branch_a[i]
        y_b = mla_attention(q_lat, kv_lat, k_rope, q_a_norm[i], w_uq[i], kv_a_norm[i], w_ukv[i]) @ w_branch_b[i]
        merged = jax.nn.sigmoid(gate_a) * y_a + jax.nn.sigmoid(gate_b) * y_b
        h = h + merged @ w_out[i]
        h = h + 0.5 * swiglu(rmsnorm(h, ffn2_norm[i]), ffn2_w_gate[i], ffn2_w_up[i], ffn2_w_down[i])
        pl_gate = jax.nn.sigmoid(rmsnorm(h, pl_norm[i]) @ w_pl_gate[i])
        h = h + pl_gate * (p[i] @ w_pl[i])
    return rmsnorm(h, final_norm)


import jax as _jax
import jax.numpy as _jnp

TWIN_FORMAT = 'train_step'
FWD_PARAMS = ['x', 'p', 'ffn1_norm', 'ffn1_w_gate', 'ffn1_w_up', 'ffn1_w_down', 'mix_norm', 'w_in', 'q_a_norm', 'w_uq', 'kv_a_norm', 'w_ukv', 'na_rpb', 'w_branch_a', 'w_branch_b', 'w_out', 'ffn2_norm', 'ffn2_w_gate', 'ffn2_w_up', 'ffn2_w_down', 'pl_norm', 'w_pl', 'w_pl_gate', 'final_norm']
TWIN_WEIGHTS = ['ffn1_norm', 'ffn1_w_gate', 'ffn1_w_up', 'ffn1_w_down', 'mix_norm', 'w_in', 'q_a_norm', 'w_uq', 'kv_a_norm', 'w_ukv', 'na_rpb', 'w_branch_a', 'w_branch_b', 'w_out', 'ffn2_norm', 'ffn2_w_gate', 'ffn2_w_up', 'ffn2_w_down', 'pl_norm', 'w_pl', 'w_pl_gate', 'final_norm']
TWIN_DIFF_INPUT = 'x'
TWIN_INPUTS = ['x', 'p', 'ffn1_norm', 'ffn1_w_gate', 'ffn1_w_up', 'ffn1_w_down', 'mix_norm', 'w_in', 'q_a_norm', 'w_uq', 'kv_a_norm', 'w_ukv', 'na_rpb', 'w_branch_a', 'w_branch_b', 'w_out', 'ffn2_norm', 'ffn2_w_gate', 'ffn2_w_up', 'ffn2_w_down', 'pl_norm', 'w_pl', 'w_pl_gate', 'final_norm', 'loss_target', 'm_ffn1_norm', 'm_ffn1_w_gate', 'm_ffn1_w_up', 'm_ffn1_w_down', 'm_mix_norm', 'm_w_in', 'm_q_a_norm', 'm_w_uq', 'm_kv_a_norm', 'm_w_ukv', 'm_na_rpb', 'm_w_branch_a', 'm_w_branch_b', 'm_w_out', 'm_ffn2_norm', 'm_ffn2_w_gate', 'm_ffn2_w_up', 'm_ffn2_w_down', 'm_pl_norm', 'm_w_pl', 'm_w_pl_gate', 'm_final_norm', 'v_ffn1_norm', 'v_ffn1_w_gate', 'v_ffn1_w_up', 'v_ffn1_w_down', 'v_mix_norm', 'v_w_in', 'v_q_a_norm', 'v_w_uq', 'v_kv_a_norm', 'v_w_ukv', 'v_na_rpb', 'v_w_branch_a', 'v_w_branch_b', 'v_w_out', 'v_ffn2_norm', 'v_ffn2_w_gate', 'v_ffn2_w_up', 'v_ffn2_w_down', 'v_pl_norm', 'v_w_pl', 'v_w_pl_gate', 'v_final_norm']
TWIN_OUTPUTS = ['loss', 'grad_x', 'grad_ffn1_norm', 'grad_ffn1_w_gate', 'grad_ffn1_w_up', 'grad_ffn1_w_down', 'grad_mix_norm', 'grad_w_in', 'grad_q_a_norm', 'grad_w_uq', 'grad_kv_a_norm', 'grad_w_ukv', 'grad_na_rpb', 'grad_w_branch_a', 'grad_w_branch_b', 'grad_w_out', 'grad_ffn2_norm', 'grad_ffn2_w_gate', 'grad_ffn2_w_up', 'grad_ffn2_w_down', 'grad_pl_norm', 'grad_w_pl', 'grad_w_pl_gate', 'grad_final_norm', 'delta_ffn1_norm', 'delta_ffn1_w_gate', 'delta_ffn1_w_up', 'delta_ffn1_w_down', 'delta_mix_norm', 'delta_w_in', 'delta_q_a_norm', 'delta_w_uq', 'delta_kv_a_norm', 'delta_w_ukv', 'delta_na_rpb', 'delta_w_branch_a', 'delta_w_branch_b', 'delta_w_out', 'delta_ffn2_norm', 'delta_ffn2_w_gate', 'delta_ffn2_w_up', 'delta_ffn2_w_down', 'delta_pl_norm', 'delta_w_pl', 'delta_w_pl_gate', 'delta_final_norm', 'new_m_ffn1_norm', 'new_m_ffn1_w_gate', 'new_m_ffn1_w_up', 'new_m_ffn1_w_down', 'new_m_mix_norm', 'new_m_w_in', 'new_m_q_a_norm', 'new_m_w_uq', 'new_m_kv_a_norm', 'new_m_w_ukv', 'new_m_na_rpb', 'new_m_w_branch_a', 'new_m_w_branch_b', 'new_m_w_out', 'new_m_ffn2_norm', 'new_m_ffn2_w_gate', 'new_m_ffn2_w_up', 'new_m_ffn2_w_down', 'new_m_pl_norm', 'new_m_w_pl', 'new_m_w_pl_gate', 'new_m_final_norm', 'new_v_ffn1_norm', 'new_v_ffn1_w_gate', 'new_v_ffn1_w_up', 'new_v_ffn1_w_down', 'new_v_mix_norm', 'new_v_w_in', 'new_v_q_a_norm', 'new_v_w_uq', 'new_v_kv_a_norm', 'new_v_w_ukv', 'new_v_na_rpb', 'new_v_w_branch_a', 'new_v_w_branch_b', 'new_v_w_out', 'new_v_ffn2_norm', 'new_v_ffn2_w_gate', 'new_v_ffn2_w_up', 'new_v_ffn2_w_down', 'new_v_pl_norm', 'new_v_w_pl', 'new_v_w_pl_gate', 'new_v_final_norm']
TWIN_LEAF_KINDS = {'loss': 'loss', 'grad_x': 'grad_x', 'grad_ffn1_norm': 'grad_w', 'grad_ffn1_w_gate': 'grad_w', 'grad_ffn1_w_up': 'grad_w', 'grad_ffn1_w_down': 'grad_w', 'grad_mix_norm': 'grad_w', 'grad_w_in': 'grad_w', 'grad_q_a_norm': 'grad_w', 'grad_w_uq': 'grad_w', 'grad_kv_a_norm': 'grad_w', 'grad_w_ukv': 'grad_w', 'grad_na_rpb': 'grad_w', 'grad_w_branch_a': 'grad_w', 'grad_w_branch_b': 'grad_w', 'grad_w_out': 'grad_w', 'grad_ffn2_norm': 'grad_w', 'grad_ffn2_w_gate': 'grad_w', 'grad_ffn2_w_up': 'grad_w', 'grad_ffn2_w_down': 'grad_w', 'grad_pl_norm': 'grad_w', 'grad_w_pl': 'grad_w', 'grad_w_pl_gate': 'grad_w', 'grad_final_norm': 'grad_w', 'delta_ffn1_norm': 'delta_w', 'delta_ffn1_w_gate': 'delta_w', 'delta_ffn1_w_up': 'delta_w', 'delta_ffn1_w_down': 'delta_w', 'delta_mix_norm': 'delta_w', 'delta_w_in': 'delta_w', 'delta_q_a_norm': 'delta_w', 'delta_w_uq': 'delta_w', 'delta_kv_a_norm': 'delta_w', 'delta_w_ukv': 'delta_w', 'delta_na_rpb': 'delta_w', 'delta_w_branch_a': 'delta_w', 'delta_w_branch_b': 'delta_w', 'delta_w_out': 'delta_w', 'delta_ffn2_norm': 'delta_w', 'delta_ffn2_w_gate': 'delta_w', 'delta_ffn2_w_up': 'delta_w', 'delta_ffn2_w_down': 'delta_w', 'delta_pl_norm': 'delta_w', 'delta_w_pl': 'delta_w', 'delta_w_pl_gate': 'delta_w', 'delta_final_norm': 'delta_w', 'new_m_ffn1_norm': 'new_m', 'new_m_ffn1_w_gate': 'new_m', 'new_m_ffn1_w_up': 'new_m', 'new_m_ffn1_w_down': 'new_m', 'new_m_mix_norm': 'new_m', 'new_m_w_in': 'new_m', 'new_m_q_a_norm': 'new_m', 'new_m_w_uq': 'new_m', 'new_m_kv_a_norm': 'new_m', 'new_m_w_ukv': 'new_m', 'new_m_na_rpb': 'new_m', 'new_m_w_branch_a': 'new_m', 'new_m_w_branch_b': 'new_m', 'new_m_w_out': 'new_m', 'new_m_ffn2_norm': 'new_m', 'new_m_ffn2_w_gate': 'new_m', 'new_m_ffn2_w_up': 'new_m', 'new_m_ffn2_w_down': 'new_m', 'new_m_pl_norm': 'new_m', 'new_m_w_pl': 'new_m', 'new_m_w_pl_gate': 'new_m', 'new_m_final_norm': 'new_m', 'new_v_ffn1_norm': 'new_v', 'new_v_ffn1_w_gate': 'new_v', 'new_v_ffn1_w_up': 'new_v', 'new_v_ffn1_w_down': 'new_v', 'new_v_mix_norm': 'new_v', 'new_v_w_in': 'new_v', 'new_v_q_a_norm': 'new_v', 'new_v_w_uq': 'new_v', 'new_v_kv_a_norm': 'new_v', 'new_v_w_ukv': 'new_v', 'new_v_na_rpb': 'new_v', 'new_v_w_branch_a': 'new_v', 'new_v_w_branch_b': 'new_v', 'new_v_w_out': 'new_v', 'new_v_ffn2_norm': 'new_v', 'new_v_ffn2_w_gate': 'new_v', 'new_v_ffn2_w_up': 'new_v', 'new_v_ffn2_w_down': 'new_v', 'new_v_pl_norm': 'new_v', 'new_v_w_pl': 'new_v', 'new_v_w_pl_gate': 'new_v', 'new_v_final_norm': 'new_v'}


def _forward(args):
    return _fwd_reference(*[args[k] for k in FWD_PARAMS])


def _output_shape():
    out = _jax.eval_shape(lambda: _forward(_fwd_setup_inputs(0)))
    return out.shape, out.dtype

N_MICROBATCH = 1
ADAM_LR = 0.001
ADAM_B1 = 0.9
ADAM_B2 = 0.999
ADAM_EPS = 1e-08
ADAM_WD = 0.01
ADAM_STEP = 10
PER_EXAMPLE_BATCH_AXIS = {'x': 0, 'p': 1, 'loss_target': 0}
SHARED_INPUTS = []
_WEIGHT_DTYPES = {'ffn1_norm': _jnp.float32, 'ffn1_w_gate': _jnp.float32, 'ffn1_w_up': _jnp.float32, 'ffn1_w_down': _jnp.float32, 'mix_norm': _jnp.float32, 'w_in': _jnp.float32, 'q_a_norm': _jnp.float32, 'w_uq': _jnp.float32, 'kv_a_norm': _jnp.float32, 'w_ukv': _jnp.float32, 'na_rpb': _jnp.float32, 'w_branch_a': _jnp.float32, 'w_branch_b': _jnp.float32, 'w_out': _jnp.float32, 'ffn2_norm': _jnp.float32, 'ffn2_w_gate': _jnp.float32, 'ffn2_w_up': _jnp.float32, 'ffn2_w_down': _jnp.float32, 'pl_norm': _jnp.float32, 'w_pl': _jnp.float32, 'w_pl_gate': _jnp.float32, 'final_norm': _jnp.float32}
MOMENT_SCALE = {'ffn1_norm': 2.473439e-02, 'ffn1_w_gate': 1.068546e-02, 'ffn1_w_up': 1.034019e-02, 'ffn1_w_down': 1.715501e-02, 'mix_norm': 1.199110e-02, 'w_in': 5.883776e-03, 'q_a_norm': 6.604991e-03, 'w_uq': 3.788579e-03, 'kv_a_norm': 8.935538e-03, 'w_ukv': 4.323934e-03, 'na_rpb': 4.324021e-03, 'w_branch_a': 6.047335e-03, 'w_branch_b': 3.318470e-03, 'w_out': 6.940047e-03, 'ffn2_norm': 2.292965e-02, 'ffn2_w_gate': 9.761860e-03, 'ffn2_w_up': 9.443878e-03, 'ffn2_w_down': 1.565634e-02, 'pl_norm': 1.107401e-02, 'w_pl': 2.792999e-02, 'w_pl_gate': 1.090416e-02, 'final_norm': 7.994831e+00}


def _to_microbatches(a, axis):
    t = _jnp.moveaxis(a, axis, 0)
    t = t.reshape((N_MICROBATCH, t.shape[0] // N_MICROBATCH) + t.shape[1:])
    return _jnp.moveaxis(t, 1, axis + 1)


def setup_inputs(seed: int = 0) -> dict:
    inp = _fwd_setup_inputs(seed)
    key = _jax.random.fold_in(_jax.random.key(seed), 7919)
    shape, _ = _output_shape()
    out = dict(inp)
    out["loss_target"] = _jax.random.normal(_jax.random.fold_in(key, 0), shape, _jnp.float32)
    for i, name in enumerate(TWIN_WEIGHTS):
        w = inp[name].astype(_jnp.float32)
        if MOMENT_SCALE is None:
            s = _jnp.sqrt(_jnp.mean(_jnp.square(w)) + 1e-30)
        else:
            s = MOMENT_SCALE[name]
        km, kv = _jax.random.split(_jax.random.fold_in(key, i + 1))
        out[name] = w
        out["m_" + name] = s * _jax.random.normal(km, w.shape, _jnp.float32)
        out["v_" + name] = (s * s) * _jax.random.uniform(kv, w.shape, _jnp.float32, 0.5, 1.5)
    if N_MICROBATCH > 1:
        for name, axis in PER_EXAMPLE_BATCH_AXIS.items():
            out[name] = _to_microbatches(out[name], axis)
    return {'x': out['x'], 'p': out['p'], 'ffn1_norm': out['ffn1_norm'], 'ffn1_w_gate': out['ffn1_w_gate'], 'ffn1_w_up': out['ffn1_w_up'], 'ffn1_w_down': out['ffn1_w_down'], 'mix_norm': out['mix_norm'], 'w_in': out['w_in'], 'q_a_norm': out['q_a_norm'], 'w_uq': out['w_uq'], 'kv_a_norm': out['kv_a_norm'], 'w_ukv': out['w_ukv'], 'na_rpb': out['na_rpb'], 'w_branch_a': out['w_branch_a'], 'w_branch_b': out['w_branch_b'], 'w_out': out['w_out'], 'ffn2_norm': out['ffn2_norm'], 'ffn2_w_gate': out['ffn2_w_gate'], 'ffn2_w_up': out['ffn2_w_up'], 'ffn2_w_down': out['ffn2_w_down'], 'pl_norm': out['pl_norm'], 'w_pl': out['w_pl'], 'w_pl_gate': out['w_pl_gate'], 'final_norm': out['final_norm'], 'loss_target': out['loss_target'], 'm_ffn1_norm': out['m_ffn1_norm'], 'm_ffn1_w_gate': out['m_ffn1_w_gate'], 'm_ffn1_w_up': out['m_ffn1_w_up'], 'm_ffn1_w_down': out['m_ffn1_w_down'], 'm_mix_norm': out['m_mix_norm'], 'm_w_in': out['m_w_in'], 'm_q_a_norm': out['m_q_a_norm'], 'm_w_uq': out['m_w_uq'], 'm_kv_a_norm': out['m_kv_a_norm'], 'm_w_ukv': out['m_w_ukv'], 'm_na_rpb': out['m_na_rpb'], 'm_w_branch_a': out['m_w_branch_a'], 'm_w_branch_b': out['m_w_branch_b'], 'm_w_out': out['m_w_out'], 'm_ffn2_norm': out['m_ffn2_norm'], 'm_ffn2_w_gate': out['m_ffn2_w_gate'], 'm_ffn2_w_up': out['m_ffn2_w_up'], 'm_ffn2_w_down': out['m_ffn2_w_down'], 'm_pl_norm': out['m_pl_norm'], 'm_w_pl': out['m_w_pl'], 'm_w_pl_gate': out['m_w_pl_gate'], 'm_final_norm': out['m_final_norm'], 'v_ffn1_norm': out['v_ffn1_norm'], 'v_ffn1_w_gate': out['v_ffn1_w_gate'], 'v_ffn1_w_up': out['v_ffn1_w_up'], 'v_ffn1_w_down': out['v_ffn1_w_down'], 'v_mix_norm': out['v_mix_norm'], 'v_w_in': out['v_w_in'], 'v_q_a_norm': out['v_q_a_norm'], 'v_w_uq': out['v_w_uq'], 'v_kv_a_norm': out['v_kv_a_norm'], 'v_w_ukv': out['v_w_ukv'], 'v_na_rpb': out['v_na_rpb'], 'v_w_branch_a': out['v_w_branch_a'], 'v_w_branch_b': out['v_w_branch_b'], 'v_w_out': out['v_w_out'], 'v_ffn2_norm': out['v_ffn2_norm'], 'v_ffn2_w_gate': out['v_ffn2_w_gate'], 'v_ffn2_w_up': out['v_ffn2_w_up'], 'v_ffn2_w_down': out['v_ffn2_w_down'], 'v_pl_norm': out['v_pl_norm'], 'v_w_pl': out['v_w_pl'], 'v_w_pl_gate': out['v_w_pl_gate'], 'v_final_norm': out['v_final_norm']}


def _loss(weights, diff, rest, loss_target):
    with _jax.named_scope("forward"):
        args = {**rest, TWIN_DIFF_INPUT: diff, **{k: w.astype(_WEIGHT_DTYPES[k]) for k, w in weights.items()}}
        y = _forward(args)
    with _jax.named_scope("loss_head"):
        err = _jnp.square(y.astype(_jnp.float32) - loss_target)
        return 0.5 * _jnp.sum(_jnp.mean(err, axis=-1)) if err.ndim else 0.5 * err


def _adamw(w, g, m, v):
    m = ADAM_B1 * m + (1.0 - ADAM_B1) * g
    v = ADAM_B2 * v + (1.0 - ADAM_B2) * _jnp.square(g)
    m_hat = m / (1.0 - ADAM_B1 ** ADAM_STEP)
    v_hat = v / (1.0 - ADAM_B2 ** ADAM_STEP)
    delta = -ADAM_LR * (m_hat / (_jnp.sqrt(v_hat) + ADAM_EPS) + ADAM_WD * w)
    return delta, m, v


def reference(x, p, ffn1_norm, ffn1_w_gate, ffn1_w_up, ffn1_w_down, mix_norm, w_in, q_a_norm, w_uq, kv_a_norm, w_ukv, na_rpb, w_branch_a, w_branch_b, w_out, ffn2_norm, ffn2_w_gate, ffn2_w_up, ffn2_w_down, pl_norm, w_pl, w_pl_gate, final_norm, loss_target, m_ffn1_norm, m_ffn1_w_gate, m_ffn1_w_up, m_ffn1_w_down, m_mix_norm, m_w_in, m_q_a_norm, m_w_uq, m_kv_a_norm, m_w_ukv, m_na_rpb, m_w_branch_a, m_w_branch_b, m_w_out, m_ffn2_norm, m_ffn2_w_gate, m_ffn2_w_up, m_ffn2_w_down, m_pl_norm, m_w_pl, m_w_pl_gate, m_final_norm, v_ffn1_norm, v_ffn1_w_gate, v_ffn1_w_up, v_ffn1_w_down, v_mix_norm, v_w_in, v_q_a_norm, v_w_uq, v_kv_a_norm, v_w_ukv, v_na_rpb, v_w_branch_a, v_w_branch_b, v_w_out, v_ffn2_norm, v_ffn2_w_gate, v_ffn2_w_up, v_ffn2_w_down, v_pl_norm, v_w_pl, v_w_pl_gate, v_final_norm):
    given = dict(x=x, p=p, ffn1_norm=ffn1_norm, ffn1_w_gate=ffn1_w_gate, ffn1_w_up=ffn1_w_up, ffn1_w_down=ffn1_w_down, mix_norm=mix_norm, w_in=w_in, q_a_norm=q_a_norm, w_uq=w_uq, kv_a_norm=kv_a_norm, w_ukv=w_ukv, na_rpb=na_rpb, w_branch_a=w_branch_a, w_branch_b=w_branch_b, w_out=w_out, ffn2_norm=ffn2_norm, ffn2_w_gate=ffn2_w_gate, ffn2_w_up=ffn2_w_up, ffn2_w_down=ffn2_w_down, pl_norm=pl_norm, w_pl=w_pl, w_pl_gate=w_pl_gate, final_norm=final_norm, loss_target=loss_target, m_ffn1_norm=m_ffn1_norm, m_ffn1_w_gate=m_ffn1_w_gate, m_ffn1_w_up=m_ffn1_w_up, m_ffn1_w_down=m_ffn1_w_down, m_mix_norm=m_mix_norm, m_w_in=m_w_in, m_q_a_norm=m_q_a_norm, m_w_uq=m_w_uq, m_kv_a_norm=m_kv_a_norm, m_w_ukv=m_w_ukv, m_na_rpb=m_na_rpb, m_w_branch_a=m_w_branch_a, m_w_branch_b=m_w_branch_b, m_w_out=m_w_out, m_ffn2_norm=m_ffn2_norm, m_ffn2_w_gate=m_ffn2_w_gate, m_ffn2_w_up=m_ffn2_w_up, m_ffn2_w_down=m_ffn2_w_down, m_pl_norm=m_pl_norm, m_w_pl=m_w_pl, m_w_pl_gate=m_w_pl_gate, m_final_norm=m_final_norm, v_ffn1_norm=v_ffn1_norm, v_ffn1_w_gate=v_ffn1_w_gate, v_ffn1_w_up=v_ffn1_w_up, v_ffn1_w_down=v_ffn1_w_down, v_mix_norm=v_mix_norm, v_w_in=v_w_in, v_q_a_norm=v_q_a_norm, v_w_uq=v_w_uq, v_kv_a_norm=v_kv_a_norm, v_w_ukv=v_w_ukv, v_na_rpb=v_na_rpb, v_w_branch_a=v_w_branch_a, v_w_branch_b=v_w_branch_b, v_w_out=v_w_out, v_ffn2_norm=v_ffn2_norm, v_ffn2_w_gate=v_ffn2_w_gate, v_ffn2_w_up=v_ffn2_w_up, v_ffn2_w_down=v_ffn2_w_down, v_pl_norm=v_pl_norm, v_w_pl=v_w_pl, v_w_pl_gate=v_w_pl_gate, v_final_norm=v_final_norm)
    weights = {n: given[n] for n in TWIN_WEIGHTS}
    shared = {n: given[n] for n in SHARED_INPUTS}
    per_example = {n: given[n] for n in ['x', 'p']}
    grad_fn = _jax.value_and_grad(_loss, argnums=(0, 1))

    def one_microbatch(ex, loss_target):
        ex = dict(ex)
        diff = ex.pop(TWIN_DIFF_INPUT)
        return grad_fn(weights, diff, {**shared, **ex}, loss_target)

    if N_MICROBATCH == 1:
        loss, (grad_w, grad_x) = one_microbatch(per_example, given["loss_target"])
    else:
        def body(carry, xs):
            loss_sum, grad_sum = carry
            l_k, (gw_k, gx_k) = one_microbatch(xs[0], xs[1])
            with _jax.named_scope("update"):
                return (loss_sum + l_k, _jax.tree.map(_jnp.add, grad_sum, gw_k)), gx_k

        init = (_jnp.zeros((), _jnp.float32), _jax.tree.map(_jnp.zeros_like, weights))
        (loss, grad_w), grad_x = _jax.lax.scan(body, init, (per_example, given["loss_target"]))
    with _jax.named_scope("update"):
        delta_w, new_m, new_v = {}, {}, {}
        for n in TWIN_WEIGHTS:
            delta_w[n], new_m[n], new_v[n] = _adamw(weights[n], grad_w[n], given["m_" + n], given["v_" + n])
    return (loss, grad_x, *[grad_w[n] for n in TWIN_WEIGHTS], *[delta_w[n] for n in TWIN_WEIGHTS],
            *[new_m[n] for n in TWIN_WEIGHTS], *[new_v[n] for n in TWIN_WEIGHTS])
```

```python
import functools

import numpy as np
import jax
import jax.numpy as jnp
from jax import lax
from jax.experimental import pallas as pl
from jax.experimental.pallas import tpu as pltpu

F32 = jnp.float32
BF16 = jnp.bfloat16

VMEM_LIMIT_V7X = 56 * 1024 * 1024
VMEM_BUDGET_V7X = 40 * 1024 * 1024
LANES = 128

GRID_W = 64
NA_WIN_ROWS = 8
NA_WIN_COLS = 16
HEAD_DIM = 128
MLA_NOPE = 128
MLA_ROPE = 64
MLA_QK = MLA_NOPE + MLA_ROPE
ROPE_THETA = 10000.0
NORM_EPS = 1e-6
NEG_INF = -1e30
N_CHIPS = 4

ADAM_LR = 0.001
ADAM_B1 = 0.9
ADAM_B2 = 0.999
ADAM_EPS = 1e-08
ADAM_WD = 0.01
ADAM_STEP = 10

MESH = pl.DeviceIdType.MESH
ANY = pl.BlockSpec(memory_space=pl.ANY)


def _params(sem=None):
    return pltpu.CompilerParams(dimension_semantics=sem, vmem_limit_bytes=VMEM_LIMIT_V7X)


def _pick(n, target, align):
    best = None
    t = align
    while t <= min(n, target):
        if n % t == 0:
            best = t
        t += align
    return n if best is None else best


def mm(a, b, *, name, ta=False, tb=False, out_dtype=F32, res=None, alpha=1.0, b_stack=False, out_stack=False,
       exact=False):
    K, M = (a.shape if ta else a.shape[::-1])
    if b_stack:
        nst = b.shape[0]
        if tb:
            N, kb = b.shape[1], b.shape[2]
            Kb, nb = nst * kb, None
        else:
            Kb, nb = b.shape[1], b.shape[2]
            N = nst * nb
    else:
        N, Kb = (b.shape if tb else b.shape[::-1])
    assert K == Kb, (a.shape, b.shape, ta, tb)
    if out_stack:
        assert N % N_CHIPS == 0
    n_unit = N // N_CHIPS if out_stack else (nb if (b_stack and not tb) else N)
    k_unit = kb if (b_stack and tb) else K
    tn = _pick(n_unit, 512, LANES) if n_unit % 512 == 0 or n_unit <= 512 else _pick(n_unit, 1536, LANES)
    tk = _pick(k_unit, 2048, LANES)
    tm = _pick(M, 1024, LANES if ta else 16)
    isz = lambda t: jnp.dtype(t.dtype).itemsize
    osz = jnp.dtype(out_dtype).itemsize

    def vmem(tm_):
        return (2 * tm_ * tk * isz(a) + 2 * tk * tn * isz(b) + 2 * tm_ * tn * osz + tm_ * tn * 4
                + (2 * tm_ * tn * isz(res) if res is not None else 0))

    while vmem(tm) > VMEM_BUDGET_V7X and tm % 2 == 0 and (tm // 2) % (LANES if ta else 16) == 0:
        tm //= 2
    nk = K // tk
    gm, gn = M // tm, N // tn

    a_spec = pl.BlockSpec((tk, tm), lambda i, j, k: (k, i)) if ta else pl.BlockSpec((tm, tk), lambda i, j, k: (i, k))
    if b_stack and not tb:
        q = nb // tn
        b_spec = pl.BlockSpec((None, tk, tn), lambda i, j, k: (j // q, k, j % q))
    elif b_stack and tb:
        q = kb // tk
        b_spec = pl.BlockSpec((None, tn, tk), lambda i, j, k: (k // q, j, k % q))
    elif tb:
        b_spec = pl.BlockSpec((tn, tk), lambda i, j, k: (j, k))
    else:
        b_spec = pl.BlockSpec((tk, tn), lambda i, j, k: (k, j))
    if out_stack:
        qo = (N // N_CHIPS) // tn
        o_spec = pl.BlockSpec((None, tm, tn), lambda i, j, k: (j // qo, i, j % qo))
        o_shape = jax.ShapeDtypeStruct((N_CHIPS, M, N // N_CHIPS), out_dtype)
    else:
        o_spec = pl.BlockSpec((tm, tn), lambda i, j, k: (i, j))
        o_shape = jax.ShapeDtypeStruct((M, N), out_dtype)
    dims = (((0 if ta else 1,), (1 if tb else 0,)), ((), ()))
    has_res = res is not None

    def body(*refs):
        if has_res:
            a_ref, b_ref, r_ref, o_ref, acc_ref = refs
        else:
            a_ref, b_ref, o_ref, acc_ref = refs
            r_ref = None
        k = pl.program_id(2)
        if exact:
            part = lax.dot_general(a_ref[...], b_ref[...], dims, preferred_element_type=F32,
                                   precision=lax.Precision.HIGHEST)
        else:
            part = lax.dot_general(a_ref[...].astype(BF16), b_ref[...].astype(BF16), dims,
                                   preferred_element_type=F32)

        def finish(total):
            if alpha != 1.0:
                total = total * alpha
            if has_res:
                total = total + r_ref[...].astype(F32)
            o_ref[...] = total.astype(out_dtype)

        if nk == 1:
            finish(part)
        else:
            @pl.when(k == 0)
            def _():
                acc_ref[...] = part

            @pl.when(jnp.logical_and(k > 0, k < nk - 1))
            def _():
                acc_ref[...] += part

            @pl.when(k == nk - 1)
            def _():
                finish(acc_ref[...] + part)

    in_specs = [a_spec, b_spec]
    args = [a, b]
    if has_res:
        in_specs.append(pl.BlockSpec((tm, tn), lambda i, j, k: (i, j)))
        args.append(res)
    return pl.pallas_call(
        body, name=name, grid=(gm, gn, nk), in_specs=in_specs, out_specs=o_spec, out_shape=o_shape,
        scratch_shapes=[pltpu.VMEM((tm, tn) if nk > 1 else (8, LANES), F32)],
        compiler_params=_params(("parallel", "parallel", "arbitrary")),
    )(*args)


def rowwise(fn, rows, consts, outs, accs=(), *, tm, name, tn=None):
    rows = [r if isinstance(r, tuple) else (r, r.shape[1], 0) for r in rows]
    S = rows[0][0].shape[0]
    tm = _pick(S, tm, 16)
    nrow, ncon, nout = len(rows), len(consts), len(outs)
    if tn is None:
        grid = (S // tm,)
        in_specs = [pl.BlockSpec((tm, w), functools.partial(lambda i, cb: (i, cb), cb=cb)) for _, w, cb in rows]
        in_specs += [pl.BlockSpec(c.shape, lambda i: (0, 0)) for c in consts]
        out_specs = [pl.BlockSpec((tm, n), lambda i: (i, 0)) for n, _ in outs]
        out_specs += [pl.BlockSpec(s, lambda i: (0, 0)) for s in accs]
        sem = ("arbitrary",)
    else:
        assert not accs
        N = rows[0][1]
        grid = (S // tm, N // tn)
        in_specs = [pl.BlockSpec((tm, tn), lambda i, j: (i, j)) for _ in rows]
        in_specs += [pl.BlockSpec(c.shape, lambda i, j: (0, 0)) for c in consts]
        out_specs = [pl.BlockSpec((tm, tn), lambda i, j: (i, j)) for _ in outs]
        sem = ("parallel", "parallel")
    out_shape = [jax.ShapeDtypeStruct((S, n), dt) for n, dt in outs]
    out_shape += [jax.ShapeDtypeStruct(s, F32) for s in accs]

    def body(*refs):
        vals = fn(*[r[...] for r in refs[:nrow + ncon]])
        if not isinstance(vals, (tuple, list)):
            vals = (vals,)
        o_refs = refs[nrow + ncon:]
        for o_ref, v in zip(o_refs[:nout], vals[:nout]):
            o_ref[...] = v.astype(o_ref.dtype)
        if accs:
            first = pl.program_id(0) == 0

            def accumulate(a_ref, v):
                @pl.when(first)
                def _():
                    a_ref[...] = v

                @pl.when(jnp.logical_not(first))
                def _():
                    a_ref[...] += v

            for a_ref, v in zip(o_refs[nout:], vals[nout:]):
                accumulate(a_ref, v.astype(F32))

    res = pl.pallas_call(
        body, name=name, grid=grid, in_specs=in_specs, out_specs=out_specs, out_shape=out_shape,
        compiler_params=_params(sem),
    )(*[r[0] for r in rows], *consts)
    return res


def _rstd(x):
    return lax.rsqrt(jnp.mean(x * x, axis=-1, keepdims=True) + NORM_EPS)


def norm_fwd(x, g, *, name, tm=256):
    w = x[1] if isinstance(x, tuple) else x.shape[1]

    def fn(xb, gb):
        return (xb * _rstd(xb)) * gb

    return rowwise(fn, [x], [g], [(w, BF16)], tm=tm, name=name)[0]


def norm_bwd(x, g, dn, *, name, res=None, want_f32=True, bf16_alpha=None, tm=256):
    w = x[1] if isinstance(x, tuple) else x.shape[1]
    has_res = res is not None

    def fn(*blocks):
        if has_res:
            xb, dnb, rb, gb = blocks
        else:
            xb, dnb, gb = blocks
        r = _rstd(xb)
        xh = xb * r
        dxh = dnb * gb
        dx = r * (dxh - xh * jnp.mean(dxh * xh, axis=-1, keepdims=True))
        if has_res:
            dx = dx + rb
        out = []
        if want_f32:
            out.append(dx)
        if bf16_alpha is not None:
            out.append(dx * bf16_alpha if bf16_alpha != 1.0 else dx)
        out.append(jnp.sum(dnb * xh, axis=0, keepdims=True))
        return tuple(out)

    outs = ([(w, F32)] if want_f32 else []) + ([(w, BF16)] if bf16_alpha is not None else [])
    rows = [x, dn] + ([res] if has_res else [])
    return rowwise(fn, rows, [g], outs, accs=[(1, w)], tm=tm, name=name)


def _sig(x):
    return jax.nn.sigmoid(x)


def swiglu_fwd(g, u, *, name):
    return rowwise(lambda gb, ub: gb * _sig(gb) * ub, [g, u], [], [(g.shape[1], BF16)], tm=256, name=name,
                   tn=_pick(g.shape[1], 1536, LANES))[0]


def swiglu_bwd(g, u, da, *, name):
    def fn(gb, ub, dab):
        s = _sig(gb)
        return dab * ub * (s + gb * s * (1.0 - s)), dab * (gb * s)

    n = g.shape[1]
    return rowwise(fn, [g, u, da], [], [(n, BF16), (n, BF16)], tm=256, name=name, tn=_pick(n, 1536, LANES))


def rope(x, cos, sin_signed, *, name, out_dtype):
    w = x[1] if isinstance(x, tuple) else x.shape[1]
    half = MLA_ROPE // 2

    def fn(xb, cb, sb):
        lane = lax.broadcasted_iota(jnp.int32, cb.shape, 1)
        outs = []
        for hb in range(w // LANES):
            blk = xb[:, hb * LANES:(hb + 1) * LANES]
            partner = jnp.where(lane < half, pltpu.roll(blk, LANES - half, 1), pltpu.roll(blk, half, 1))
            outs.append(blk * cb + partner * sb)
        return outs[0] if len(outs) == 1 else jnp.concatenate(outs, axis=1)

    return rowwise(fn, [x, cos, sin_signed], [], [(w, out_dtype)], tm=256, name=name)[0]


def _na_tables():
    cols = np.arange(GRID_W)
    kw = NA_WIN_COLS
    dc = np.clip(cols[None, :] - cols[:, None], -(kw - 1), kw - 1) + (kw - 1)
    onehot = np.zeros((LANES, GRID_W * GRID_W), np.float32)
    onehot[dc.reshape(-1), np.arange(GRID_W * GRID_W)] = 1.0
    col_start = np.clip(cols - kw // 2, 0, GRID_W - kw)
    mask = (cols[None, :] >= col_start[:, None]) & (cols[None, :] < col_start[:, None] + kw)
    return onehot, np.where(mask, 0.0, NEG_INF).astype(np.float32)


def na_bias(rpb):
    H = rpb.shape[0]
    nr, kh = 2 * NA_WIN_ROWS - 1, NA_WIN_ROWS
    onehot, maskb = _na_tables()
    rp = jnp.pad(rpb.reshape(H * nr, 2 * NA_WIN_COLS - 1), ((0, 0), (0, LANES - (2 * NA_WIN_COLS - 1))))
    t1 = mm(rp, jnp.asarray(onehot), name="na_bias_table", exact=True).reshape(H, nr, GRID_W, GRID_W)
    t1 = t1 + jnp.asarray(maskb)[None, None]
    per_t = [jnp.stack([t1[:, i - t + kh - 1] for i in range(kh)], axis=2) for t in range(kh)]
    return jnp.stack(per_t, axis=1).reshape(H, kh, GRID_W, kh * GRID_W)


def na_rpb_grad(db):
    H = db.shape[0]
    nr, kh = 2 * NA_WIN_ROWS - 1, NA_WIN_ROWS
    onehot, _ = _na_tables()
    db = db.reshape(H, kh, GRID_W, kh, GRID_W)
    per_dr = []
    for dri in range(nr):
        terms = [db[:, t, :, dri - (kh - 1) + t, :] for t in range(kh) if 0 <= dri - (kh - 1) + t < kh]
        per_dr.append(functools.reduce(jnp.add, terms))
    dt1 = jnp.stack(per_dr, axis=1).reshape(H * nr, GRID_W * GRID_W)
    g = mm(dt1, jnp.asarray(onehot), name="na_rpb_grad", tb=True, exact=True)
    return g[:, :2 * NA_WIN_COLS - 1].reshape(H, nr, 2 * NA_WIN_COLS - 1)


def _na_first_row(r, rows):
    return jnp.clip(r - NA_WIN_ROWS // 2, 0, rows - NA_WIN_ROWS)


def _na_scores(q_ref, k_ref, b_ref, start):
    q = q_ref[...].astype(BF16)
    k = k_ref[pl.ds(start, NA_WIN_ROWS * GRID_W), :].astype(BF16)
    s = lax.dot_general(q, k, (((1,), (1,)), ((), ())), preferred_element_type=F32)
    s = s * (HEAD_DIM ** -0.5) + b_ref[...]
    m = jnp.max(s, axis=-1, keepdims=True)
    e = jnp.exp(s - m)
    return q, k, e / jnp.sum(e, axis=-1, keepdims=True)


def na_fwd(z, bias, H, S):
    rows = S // GRID_W
    nkeys = NA_WIN_ROWS * GRID_W

    def body(q_ref, k_ref, v_ref, b_ref, o_ref):
        r = pl.program_id(1)
        start = pl.multiple_of(_na_first_row(r, rows) * GRID_W, GRID_W)
        _, _, p = _na_scores(q_ref, k_ref, b_ref, start)
        v = v_ref[pl.ds(start, nkeys), :].astype(BF16)
        o_ref[...] = jnp.dot(p.astype(BF16), v, preferred_element_type=F32).astype(o_ref.dtype)

    return pl.pallas_call(
        body, name="na_fwd", grid=(H, rows),
        in_specs=[pl.BlockSpec((GRID_W, HEAD_DIM), lambda h, r: (r, h)),
                  pl.BlockSpec((S, HEAD_DIM), lambda h, r: (0, H + h)),
                  pl.BlockSpec((S, HEAD_DIM), lambda h, r: (0, 2 * H + h)),
                  pl.BlockSpec((None, None, GRID_W, nkeys), lambda h, r: (h, r - _na_first_row(r, rows), 0, 0))],
        out_specs=pl.BlockSpec((GRID_W, HEAD_DIM), lambda h, r: (r, h)),
        out_shape=jax.ShapeDtypeStruct((S, H * HEAD_DIM), BF16),
        compiler_params=_params(("parallel", "arbitrary")),
    )(z, z, z, bias)


def na_bwd(z, bias, do, H, S):
    rows = S // GRID_W
    nkeys = NA_WIN_ROWS * GRID_W
    tn_dims = (((0,), (0,)), ((), ()))

    def body(q_ref, k_ref, v_ref, b_ref, do_ref, dq_ref, dk_ref, dv_ref, db_ref):
        r = pl.program_id(1)
        start = pl.multiple_of(_na_first_row(r, rows) * GRID_W, GRID_W)
        q, k, p = _na_scores(q_ref, k_ref, b_ref, start)
        v = v_ref[pl.ds(start, nkeys), :].astype(BF16)
        dob = do_ref[...].astype(BF16)
        dp = lax.dot_general(dob, v, (((1,), (1,)), ((), ())), preferred_element_type=F32)
        ds = p * (dp - jnp.sum(dp * p, axis=-1, keepdims=True))
        dsb = (ds * (HEAD_DIM ** -0.5)).astype(BF16)
        dq_ref[...] = jnp.dot(dsb, k, preferred_element_type=F32).astype(dq_ref.dtype)

        @pl.when(r == 0)
        def _():
            dk_ref[...] = jnp.zeros_like(dk_ref)
            dv_ref[...] = jnp.zeros_like(dv_ref)

        dk_ref[pl.ds(start, nkeys), :] += lax.dot_general(dsb, q, tn_dims, preferred_element_type=F32)
        dv_ref[pl.ds(start, nkeys), :] += lax.dot_general(p.astype(BF16), dob, tn_dims, preferred_element_type=F32)

        fresh = jnp.logical_or(r <= NA_WIN_ROWS // 2, r > rows - NA_WIN_ROWS // 2)

        @pl.when(fresh)
        def _():
            db_ref[...] = ds

        @pl.when(jnp.logical_not(fresh))
        def _():
            db_ref[...] += ds

    W = H * HEAD_DIM
    return pl.pallas_call(
        body, name="na_bwd", grid=(H, rows),
        in_specs=[pl.BlockSpec((GRID_W, HEAD_DIM), lambda h, r: (r, h)),
                  pl.BlockSpec((S, HEAD_DIM), lambda h, r: (0, H + h)),
                  pl.BlockSpec((S, HEAD_DIM), lambda h, r: (0, 2 * H + h)),
                  pl.BlockSpec((None, None, GRID_W, nkeys), lambda h, r: (h, r - _na_first_row(r, rows), 0, 0)),
                  pl.BlockSpec((GRID_W, HEAD_DIM), lambda h, r: (r, h))],
        out_specs=[pl.BlockSpec((GRID_W, HEAD_DIM), lambda h, r: (r, h)),
                   pl.BlockSpec((S, HEAD_DIM), lambda h, r: (0, h)),
                   pl.BlockSpec((S, HEAD_DIM), lambda h, r: (0, h)),
                   pl.BlockSpec((None, None, GRID_W, nkeys), lambda h, r: (h, r - _na_first_row(r, rows), 0, 0))],
        out_shape=[jax.ShapeDtypeStruct((S, W), BF16), jax.ShapeDtypeStruct((S, W), F32),
                   jax.ShapeDtypeStruct((S, W), F32), jax.ShapeDtypeStruct((H, NA_WIN_ROWS, GRID_W, nkeys), F32)],
        compiler_params=_params(("arbitrary", "arbitrary")),
    )(z, z, z, bias, do)


def _mla_scores(qn_ref, qr_ref, kn_ref, kr_ref):
    nt = (((1,), (1,)), ((), ()))
    s = lax.dot_general(qn_ref[...], kn_ref[...], nt, preferred_element_type=F32)
    s = s + lax.dot_general(qr_ref[...], kr_ref[...], nt, preferred_element_type=F32)
    return s * (MLA_QK ** -0.5)


def mla_fwd(qn, qr, kn, v, kr, H, S):
    tq = _pick(S, 256, 16)

    def body(qn_ref, qr_ref, kn_ref, v_ref, kr_ref, o_ref, lse_ref):
        s = _mla_scores(qn_ref, qr_ref, kn_ref, kr_ref)
        m = jnp.max(s, axis=-1, keepdims=True)
        e = jnp.exp(s - m)
        l = jnp.sum(e, axis=-1, keepdims=True)
        o_ref[...] = jnp.dot((e / l).astype(BF16), v_ref[...], preferred_element_type=F32).astype(o_ref.dtype)
        lse_ref[...] = jnp.broadcast_to(m + jnp.log(l), lse_ref.shape)

    qspec = pl.BlockSpec((tq, HEAD_DIM), lambda h, i: (i, h))
    kspec = pl.BlockSpec((S, HEAD_DIM), lambda h, i: (0, h))
    return pl.pallas_call(
        body, name="mla_fwd", grid=(H, S // tq),
        in_specs=[qspec, qspec, kspec, kspec, pl.BlockSpec((S, LANES), lambda h, i: (0, 0))],
        out_specs=[qspec, qspec],
        out_shape=[jax.ShapeDtypeStruct((S, H * HEAD_DIM), BF16), jax.ShapeDtypeStruct((S, H * LANES), F32)],
        compiler_params=_params(("parallel", "arbitrary")),
    )(qn, qr, kn, v, kr)


def mla_bwd(qn, qr, kn, v, kr, lse, do, H, S):
    tq = _pick(S, 256, 16)
    nt = (((1,), (1,)), ((), ()))
    tn_dims = (((0,), (0,)), ((), ()))

    def body(qn_ref, qr_ref, kn_ref, v_ref, kr_ref, lse_ref, do_ref, dqn_ref, dqr_ref, dkn_ref, dv_ref, dkr_ref):
        h, i = pl.program_id(0), pl.program_id(1)
        s = _mla_scores(qn_ref, qr_ref, kn_ref, kr_ref)
        p = jnp.exp(s - lse_ref[:, 0:1])
        dob = do_ref[...].astype(BF16)
        dp = lax.dot_general(dob, v_ref[...], nt, preferred_element_type=F32)
        ds = p * (dp - jnp.sum(dp * p, axis=-1, keepdims=True))
        dsb = (ds * (MLA_QK ** -0.5)).astype(BF16)
        dqn_ref[...] = jnp.dot(dsb, kn_ref[...], preferred_element_type=F32).astype(dqn_ref.dtype)
        dqr_ref[...] = jnp.dot(dsb, kr_ref[...], preferred_element_type=F32).astype(dqr_ref.dtype)

        @pl.when(i == 0)
        def _():
            dkn_ref[...] = jnp.zeros_like(dkn_ref)
            dv_ref[...] = jnp.zeros_like(dv_ref)

        @pl.when(jnp.logical_and(i == 0, h == 0))
        def _():
            dkr_ref[...] = jnp.zeros_like(dkr_ref)

        dkn_ref[...] += lax.dot_general(dsb, qn_ref[...], tn_dims, preferred_element_type=F32)
        dkr_ref[...] += lax.dot_general(dsb, qr_ref[...], tn_dims, preferred_element_type=F32)
        dv_ref[...] += lax.dot_general(p.astype(BF16), dob, tn_dims, preferred_element_type=F32)

    qspec = pl.BlockSpec((tq, HEAD_DIM), lambda h, i: (i, h))
    kspec = pl.BlockSpec((S, HEAD_DIM), lambda h, i: (0, h))
    rspec = pl.BlockSpec((S, LANES), lambda h, i: (0, 0))
    W = H * HEAD_DIM
    return pl.pallas_call(
        body, name="mla_bwd", grid=(H, S // tq),
        in_specs=[qspec, qspec, kspec, kspec, rspec, qspec, qspec],
        out_specs=[qspec, qspec, kspec, kspec, rspec],
        out_shape=[jax.ShapeDtypeStruct((S, W), BF16), jax.ShapeDtypeStruct((S, W), F32),
                   jax.ShapeDtypeStruct((S, W), F32), jax.ShapeDtypeStruct((S, W), F32),
                   jax.ShapeDtypeStruct((S, LANES), F32)],
        compiler_params=_params(("arbitrary", "arbitrary")),
    )(qn, qr, kn, v, kr, lse, do)


def _place():
    return lax.axis_index("x"), lax.axis_index("y"), lax.axis_index("c")


def _other_chips(x, y):
    return [(1 - x, y), (x, 1 - y), (1 - x, 1 - y)]


def _remote(src, dst, send_sem, recv_sem, to):
    return pltpu.make_async_remote_copy(src_ref=src, dst_ref=dst, send_sem=send_sem, recv_sem=recv_sem,
                                        device_id=to, device_id_type=MESH)


def gather_weights(shards):
    n = len(shards)

    def body(*refs):
        ins, outs = refs[:n], refs[n:2 * n]
        send, recv, local = refs[2 * n:]
        x, y, c = _place()
        me = 2 * x + y
        chips = _other_chips(x, y)
        sibling = (x, y, 1 - c)
        own = [pltpu.make_async_copy(ins[w], outs[w].at[me], local.at[w]) for w in range(n)]
        for cp in own:
            cp.start()
        first = []
        for w in range(n):
            for k, (px, py) in enumerate(chips):
                cp = _remote(ins[w].at[c], outs[w].at[me, c], send.at[6 * w + k], recv.at[6 * w + k], (px, py, c))
                cp.start()
                first.append(cp)
        passed = []
        for w in range(n):
            for k, (px, py) in enumerate(chips):
                blk = outs[w].at[2 * px + py, c]
                _remote(blk, blk, send.at[6 * w + k], recv.at[6 * w + k], (px, py, c)).wait_recv()
                cp = _remote(blk, blk, send.at[6 * w + 3 + k], recv.at[6 * w + 3 + k], sibling)
                cp.start()
                passed.append(cp)
        for w in range(n):
            for k, (px, py) in enumerate(chips):
                blk = outs[w].at[2 * px + py, 1 - c]
                _remote(blk, blk, send.at[6 * w + 3 + k], recv.at[6 * w + 3 + k], sibling).wait_recv()
        for cp in first + passed:
            cp.wait_send()
        for cp in own:
            cp.wait()

    return pl.pallas_call(
        body, name="gather_weights", in_specs=[ANY] * n, out_specs=[ANY] * n,
        out_shape=[jax.ShapeDtypeStruct((N_CHIPS,) + s.shape, s.dtype) for s in shards],
        scratch_shapes=[pltpu.SemaphoreType.DMA((6 * n,)), pltpu.SemaphoreType.DMA((6 * n,)),
                        pltpu.SemaphoreType.DMA((n,))],
    )(*shards)


def pair_exchange(grads):
    n = len(grads)

    def body(*refs):
        ins, outs = refs[:n], refs[n:2 * n]
        send, recv = refs[2 * n:]
        x, y, c = _place()
        cps = []
        for w in range(n):
            cp = _remote(ins[w].at[:, 1 - c], outs[w], send.at[w], recv.at[w], (x, y, 1 - c))
            cp.start()
            cps.append(cp)
        for cp in cps:
            cp.wait()

    return pl.pallas_call(
        body, name="grad_pair_exchange", in_specs=[ANY] * n, out_specs=[ANY] * n,
        out_shape=[jax.ShapeDtypeStruct((g.shape[0],) + g.shape[2:], g.dtype) for g in grads],
        scratch_shapes=[pltpu.SemaphoreType.DMA((n,)), pltpu.SemaphoreType.DMA((n,))],
    )(*grads)


def chip_scatter(sums):
    n = len(sums)

    def body(*refs):
        ins, outs = refs[:n], refs[n:2 * n]
        send, recv = refs[2 * n:]
        x, y, c = _place()
        cps = []
        for w in range(n):
            for k, (px, py) in enumerate(_other_chips(x, y)):
                cp = _remote(ins[w].at[2 * px + py], outs[w].at[k], send.at[3 * w + k], recv.at[3 * w + k], (px, py, c))
                cp.start()
                cps.append(cp)
        for cp in cps:
            cp.wait()

    return pl.pallas_call(
        body, name="grad_chip_scatter", in_specs=[ANY] * n, out_specs=[ANY] * n,
        out_shape=[jax.ShapeDtypeStruct((3,) + s.shape[1:], s.dtype) for s in sums],
        scratch_shapes=[pltpu.SemaphoreType.DMA((3 * n,)), pltpu.SemaphoreType.DMA((3 * n,))],
    )(*sums)


def half_exchange(halves):
    n = len(halves)

    def body(*refs):
        ins, outs = refs[:n], refs[n:2 * n]
        send, recv, local = refs[2 * n:]
        x, y, c = _place()
        own = [pltpu.make_async_copy(ins[w], outs[w].at[c], local.at[w]) for w in range(n)]
        for cp in own:
            cp.start()
        cps = []
        for w in range(n):
            cp = _remote(ins[w], outs[w].at[c], send.at[w], recv.at[w], (x, y, 1 - c))
            cp.start()
            cps.append(cp)
        for w in range(n):
            cps[w].wait_send()
            _remote(ins[w], outs[w].at[1 - c], send.at[w], recv.at[w], (x, y, 1 - c)).wait_recv()
        for cp in own:
            cp.wait()

    return pl.pallas_call(
        body, name="grad_half_exchange", in_specs=[ANY] * n, out_specs=[ANY] * n,
        out_shape=[jax.ShapeDtypeStruct((2,) + h.shape, h.dtype) for h in halves],
        scratch_shapes=[pltpu.SemaphoreType.DMA((n,)), pltpu.SemaphoreType.DMA((n,)), pltpu.SemaphoreType.DMA((n,))],
    )(*halves)


def gather_small(v):
    def body(v_ref, o_ref, send, recv, local):
        x, y, c = _place()
        me = 4 * x + 2 * y + c
        own = pltpu.make_async_copy(v_ref, o_ref.at[me], local)
        own.start()
        cps = []
        for k in range(1, 8):
            fx, fy, fc = (k >> 2) & 1, (k >> 1) & 1, k & 1
            to = (x ^ fx if fx else x, y ^ fy if fy else y, c ^ fc if fc else c)
            cp = _remote(v_ref, o_ref.at[me], send.at[k - 1], recv.at[k - 1], to)
            cp.start()
            cps.append(cp)
        for k in range(1, 8):
            fx, fy, fc = (k >> 2) & 1, (k >> 1) & 1, k & 1
            px, py, pc = (x ^ fx if fx else x, y ^ fy if fy else y, c ^ fc if fc else c)
            cps[k - 1].wait_send()
            _remote(v_ref, o_ref.at[4 * px + 2 * py + pc], send.at[k - 1], recv.at[k - 1], (px, py, pc)).wait_recv()
        own.wait()

    return pl.pallas_call(
        body, name="gather_small_grads", in_specs=[ANY], out_specs=ANY,
        out_shape=jax.ShapeDtypeStruct((8,) + v.shape, v.dtype),
        scratch_shapes=[pltpu.SemaphoreType.DMA((7,)), pltpu.SemaphoreType.DMA((7,)), pltpu.SemaphoreType.DMA],
    )(v)


def _row_tile(rows, cols, nbuf_bytes):
    tm = _pick(rows, 512, 16)
    while tm * cols * nbuf_bytes * 2 > VMEM_BUDGET_V7X and tm % 32 == 0:
        tm //= 2
    return tm


def pair_sum(g, r, c_idx):
    _, _, rows, cols = g.shape
    tm = _row_tile(rows, cols, 2 + 2 + 2)
    nb = rows // tm

    def body(c_ref, g_ref, r_ref, o_ref):
        o_ref[...] = (g_ref[...].astype(F32) + r_ref[...].astype(F32)).astype(o_ref.dtype)

    gs = pltpu.PrefetchScalarGridSpec(
        num_scalar_prefetch=1, grid=(N_CHIPS, nb),
        in_specs=[pl.BlockSpec((None, None, tm, cols), lambda j, i, c_ref: (j, c_ref[0], i, 0)),
                  pl.BlockSpec((None, tm, cols), lambda j, i, c_ref: (j, i, 0))],
        out_specs=pl.BlockSpec((None, tm, cols), lambda j, i, c_ref: (j, i, 0)))
    return pl.pallas_call(body, name="grad_pair_sum", grid_spec=gs,
                          out_shape=jax.ShapeDtypeStruct(r.shape, BF16),
                          compiler_params=_params(("arbitrary", "arbitrary")))(c_idx, g, r)


def chip_sum(s, r, j_idx):
    _, rows, cols = s.shape
    tm = _row_tile(rows, cols, 2 + 3 * 2 + 4)
    nb = rows // tm

    def body(j_ref, s_ref, r_ref, o_ref):
        t = s_ref[...].astype(F32)
        for k in range(3):
            t = t + r_ref[k].astype(F32)
        o_ref[...] = t

    gs = pltpu.PrefetchScalarGridSpec(
        num_scalar_prefetch=1, grid=(nb,),
        in_specs=[pl.BlockSpec((None, tm, cols), lambda i, j_ref: (j_ref[0], i, 0)),
                  pl.BlockSpec((3, tm, cols), lambda i, j_ref: (0, i, 0))],
        out_specs=pl.BlockSpec((tm, cols), lambda i, j_ref: (i, 0)))
    return pl.pallas_call(body, name="grad_chip_sum", grid_spec=gs,
                          out_shape=jax.ShapeDtypeStruct((rows, cols), F32),
                          compiler_params=_params(("arbitrary",)))(j_idx, s, r)


def adamw(w, g, m, v, *, name):
    rows, cols = w.shape
    tm = _row_tile(rows, cols, 7 * 4)

    def fn(wb, gb, mb, vb):
        m2 = ADAM_B1 * mb + (1.0 - ADAM_B1) * gb
        v2 = ADAM_B2 * vb + (1.0 - ADAM_B2) * (gb * gb)
        m_hat = m2 / (1.0 - ADAM_B1 ** ADAM_STEP)
        v_hat = v2 / (1.0 - ADAM_B2 ** ADAM_STEP)
        delta = -ADAM_LR * (m_hat / (jnp.sqrt(v_hat) + ADAM_EPS) + ADAM_WD * wb)
        return delta, m2, v2

    return rowwise(fn, [w, g, m, v], [], [(cols, F32)] * 3, tm=tm, name=name)


def sum_devices(a):
    def body(a_ref, o_ref):
        t = a_ref[0]
        for k in range(1, 8):
            t = t + a_ref[k]
        o_ref[...] = t

    return pl.pallas_call(body, name="sum_small_grads", out_shape=jax.ShapeDtypeStruct(a.shape[1:], a.dtype))(a)


def _halves(w2d):
    r, c = w2d.shape
    return w2d.reshape(2, r // 2, c)


def kernel(x, p, ffn1_norm, ffn1_w_gate, ffn1_w_up, ffn1_w_down, mix_norm, w_in, q_a_norm, w_uq, kv_a_norm, w_ukv, na_rpb, w_branch_a, w_branch_b, w_out, ffn2_norm, ffn2_w_gate, ffn2_w_up, ffn2_w_down, pl_norm, w_pl, w_pl_gate, final_norm, loss_target, m_ffn1_norm, m_ffn1_w_gate, m_ffn1_w_up, m_ffn1_w_down, m_mix_norm, m_w_in, m_q_a_norm, m_w_uq, m_kv_a_norm, m_w_ukv, m_na_rpb, m_w_branch_a, m_w_branch_b, m_w_out, m_ffn2_norm, m_ffn2_w_gate, m_ffn2_w_up, m_ffn2_w_down, m_pl_norm, m_w_pl, m_w_pl_gate, m_final_norm, v_ffn1_norm, v_ffn1_w_gate, v_ffn1_w_up, v_ffn1_w_down, v_mix_norm, v_w_in, v_q_a_norm, v_w_uq, v_kv_a_norm, v_w_ukv, v_na_rpb, v_w_branch_a, v_w_branch_b, v_w_out, v_ffn2_norm, v_ffn2_w_gate, v_ffn2_w_up, v_ffn2_w_down, v_pl_norm, v_w_pl, v_w_pl_gate, v_final_norm):
    big = ["ffn1_w_gate", "ffn1_w_up", "ffn1_w_down", "w_in", "w_uq", "w_ukv", "w_branch_a", "w_branch_b", "w_out",
           "ffn2_w_gate", "ffn2_w_up", "ffn2_w_down", "w_pl", "w_pl_gate"]
    col_sharded = {"ffn1_w_gate", "ffn1_w_up", "w_in", "w_uq", "w_ukv", "w_branch_a", "w_branch_b", "ffn2_w_gate",
                   "ffn2_w_up", "w_pl"}
    small = ["ffn1_norm", "mix_norm", "q_a_norm", "kv_a_norm", "na_rpb", "ffn2_norm", "pl_norm", "final_norm"]
    order = ["ffn1_norm", "ffn1_w_gate", "ffn1_w_up", "ffn1_w_down", "mix_norm", "w_in", "q_a_norm", "w_uq",
             "kv_a_norm", "w_ukv", "na_rpb", "w_branch_a", "w_branch_b", "w_out", "ffn2_norm", "ffn2_w_gate",
             "ffn2_w_up", "ffn2_w_down", "pl_norm", "w_pl", "w_pl_gate", "final_norm"]
    env = dict(locals())
    W = {n: env[n] for n in order}
    Mo = {n: env["m_" + n] for n in order}
    Vo = {n: env["v_" + n] for n in order}

    xs = x[0]
    S, D = xs.shape
    tgt = loss_target[0]
    ps = p[0, 0]
    NAW = w_branch_a.shape[1]
    MLAW = w_branch_b.shape[1]
    NH, MH = NAW // HEAD_DIM, MLAW // HEAD_DIM
    QR, KVR = w_uq.shape[1], w_ukv.shape[1]
    F = ffn1_w_down.shape[1] * N_CHIPS
    cx, cy, cc = _place()
    c_idx = jnp.reshape(cc, (1,)).astype(jnp.int32)
    j_idx = jnp.reshape(2 * cx + cy, (1,)).astype(jnp.int32)

    shards = [_halves(W[n][0].astype(BF16)) for n in big]
    gathered = dict(zip(big, gather_weights(shards)))

    def stacked(n):
        g = gathered[n]
        return g.reshape(N_CHIPS, 2 * g.shape[2], g.shape[3])

    def plain(n):
        g = gathered[n]
        if n in col_sharded:
            st = stacked(n)
            return st.transpose(1, 0, 2).reshape(st.shape[1], N_CHIPS * st.shape[2])
        return g.reshape(N_CHIPS * 2 * g.shape[2], g.shape[3])

    n_front = 3 * NAW + QR + KVR
    win = plain("w_in")
    win_l = jnp.concatenate([win[:, :n_front], win[:, n_front + MLA_ROPE:], win[:, n_front:n_front + MLA_ROPE],
                             jnp.zeros((D, LANES - MLA_ROPE), BF16)], axis=1)
    ZW = win_l.shape[1]
    off_ga = n_front
    off_kr = n_front + 2 * D
    wuq = plain("w_uq").reshape(QR, MH, MLA_QK)
    wuq_n = wuq[:, :, :MLA_NOPE].reshape(QR, MH * MLA_NOPE)
    wuq_r = jnp.pad(wuq[:, :, MLA_NOPE:], ((0, 0), (0, 0), (0, LANES - MLA_ROPE))).reshape(QR, MH * LANES)
    wukv = plain("w_ukv").reshape(KVR, MH, 2, HEAD_DIM)
    wuk = wukv[:, :, 0].reshape(KVR, MH * HEAD_DIM)
    wuv = wukv[:, :, 1].reshape(KVR, MH * HEAD_DIM)

    pos = jnp.arange(S, dtype=F32)
    inv_freq = 1.0 / (ROPE_THETA ** (jnp.arange(0, MLA_ROPE, 2, dtype=F32) / MLA_ROPE))
    ang = pos[:, None] * inv_freq[None, :]
    zpad = jnp.zeros((S, LANES - MLA_ROPE), F32)
    cos_t = jnp.concatenate([jnp.cos(ang), jnp.cos(ang), zpad], axis=1)
    sin_t = jnp.concatenate([-jnp.sin(ang), jnp.sin(ang), zpad], axis=1)

    def ffn_fwd(h, norm_g, tag, pre):
        n = norm_fwd(h, norm_g, name=f"{tag}_norm")
        g = mm(n, stacked(pre + "_w_gate"), name=f"{tag}_gate", b_stack=True)
        u = mm(n, stacked(pre + "_w_up"), name=f"{tag}_up", b_stack=True)
        a = swiglu_fwd(g, u, name=f"{tag}_act")
        h_out = mm(a, plain(pre + "_w_down"), name=f"{tag}_down", res=h, alpha=0.5)
        return h_out, (n, g, u, a)

    def ffn_bwd(h, norm_g, saved, dh, dh_half, tag, pre, last):
        n, g, u, a = saved
        gw_down = mm(a, dh_half, name=f"{tag}_dw_down", ta=True, out_dtype=BF16)
        da = mm(dh_half, plain(pre + "_w_down"), name=f"{tag}_da", tb=True)
        dg, du = swiglu_bwd(g, u, da, name=f"{tag}_dact")
        gw_gate = mm(n, dg, name=f"{tag}_dw_gate", ta=True, out_dtype=BF16, out_stack=True)
        gw_up = mm(n, du, name=f"{tag}_dw_up", ta=True, out_dtype=BF16, out_stack=True)
        dn = mm(dg, stacked(pre + "_w_gate"), name=f"{tag}_dn_gate", tb=True, b_stack=True)
        dn = mm(du, stacked(pre + "_w_up"), name=f"{tag}_dn_up", tb=True, b_stack=True, res=dn)
        outs = norm_bwd(h, norm_g, dn, name=f"{tag}_dnorm", res=dh, bf16_alpha=None if last else 1.0)
        return outs, gw_gate, gw_up, gw_down

    h1, ffn1_saved = ffn_fwd(xs, ffn1_norm, "ffn1", "ffn1")
    u_mix = norm_fwd(h1, mix_norm, name="mix_norm")
    z = mm(u_mix, win_l, name="mix_in")
    bias = na_bias(na_rpb[0])
    o_a = na_fwd(z, bias, NH, S)
    c_q = norm_fwd((z, QR, 3 * NAW // QR), q_a_norm, name="q_a_norm")
    c_kv = norm_fwd((z, KVR, (3 * NAW + QR) // KVR), kv_a_norm, name="kv_a_norm")
    q_n = mm(c_q, wuq_n, name="mla_q_nope", out_dtype=BF16)
    q_r = rope(mm(c_q, wuq_r, name="mla_q_rope"), cos_t, sin_t, name="rope_q", out_dtype=BF16)
    k_n = mm(c_kv, wuk, name="mla_k_nope", out_dtype=BF16)
    v_m = mm(c_kv, wuv, name="mla_v", out_dtype=BF16)
    k_r = rope((z, LANES, off_kr // LANES), cos_t, sin_t, name="rope_k", out_dtype=BF16)
    o_b, lse = mla_fwd(q_n, q_r, k_n, v_m, k_r, MH, S)
    y_a = mm(o_a, stacked("w_branch_a"), name="branch_a", b_stack=True)
    y_b = mm(o_b, stacked("w_branch_b"), name="branch_b", b_stack=True)
    z_ga, z_gb = (z, D, off_ga // D), (z, D, off_ga // D + 1)
    merged = rowwise(lambda ga, gb, ya, yb: _sig(ga) * ya + _sig(gb) * yb, [z_ga, z_gb, y_a, y_b], [], [(D, BF16)],
                     tm=256, name="merge")[0]
    h2 = mm(merged, plain("w_out"), name="mix_out", res=h1)
    h3, ffn2_saved = ffn_fwd(h2, ffn2_norm, "ffn2", "ffn2")
    n4 = norm_fwd(h3, pl_norm, name="pl_norm")
    pg_pre = mm(n4, plain("w_pl_gate"), name="pl_gate")
    pe = mm(ps, stacked("w_pl"), name="pl_embed", b_stack=True)

    def tail(h3b, pgb, peb, tb_, fg):
        pg = _sig(pgb)
        h4 = h3b + pg * peb
        r = _rstd(h4)
        xh = h4 * r
        err = xh * fg - tb_
        loss_rows = jnp.mean(err * err, axis=-1, keepdims=True)
        dy = err * (1.0 / D)
        dxh = dy * fg
        dh4 = r * (dxh - xh * jnp.mean(dxh * xh, axis=-1, keepdims=True))
        loss_part = jnp.broadcast_to(0.5 * jnp.sum(loss_rows, axis=0, keepdims=True), (1, LANES))
        return (dh4, dh4 * peb * pg * (1.0 - pg), dh4 * pg, loss_part, jnp.sum(dy * xh, axis=0, keepdims=True))

    dh4, dpg_pre, dpe, loss_part, g_final = rowwise(
        tail, [h3, pg_pre, pe, tgt], [final_norm.reshape(1, D)], [(D, F32), (D, BF16), (D, BF16)],
        accs=[(1, LANES), (1, D)], tm=128, name="loss_tail")
    loss = lax.psum(loss_part[0, 0], ("x", "y", "c"))

    G = {}
    G["w_pl"] = mm(ps, dpe, name="pl_dw_embed", ta=True, out_dtype=BF16, out_stack=True)
    G["w_pl_gate"] = mm(n4, dpg_pre, name="pl_dw_gate", ta=True, out_dtype=BF16)
    dn4 = mm(dpg_pre, plain("w_pl_gate"), name="pl_dn", tb=True)
    dh3, dh3_half, g_pl = norm_bwd(h3, pl_norm, dn4, name="pl_dnorm", res=dh4, bf16_alpha=0.5)
    (dh2, dh2_b, g_ffn2), G["ffn2_w_gate"], G["ffn2_w_up"], G["ffn2_w_down"] = ffn_bwd(
        h2, ffn2_norm, ffn2_saved, dh3, dh3_half, "ffn2", "ffn2", last=False)

    G["w_out"] = mm(merged, dh2_b, name="mix_dw_out", ta=True, out_dtype=BF16)
    dmerged = mm(dh2_b, plain("w_out"), name="mix_dmerged", tb=True)

    def merge_bwd(ga, gb, ya, yb, dm):
        sa, sb = _sig(ga), _sig(gb)
        return dm * sa, dm * sb, dm * ya * sa * (1.0 - sa), dm * yb * sb * (1.0 - sb)

    dy_a, dy_b, dga, dgb = rowwise(merge_bwd, [z_ga, z_gb, y_a, y_b, dmerged], [], [(D, BF16)] * 4, tm=256,
                                   name="merge_bwd")
    G["w_branch_a"] = mm(o_a, dy_a, name="branch_a_dw", ta=True, out_dtype=BF16, out_stack=True)
    G["w_branch_b"] = mm(o_b, dy_b, name="branch_b_dw", ta=True, out_dtype=BF16, out_stack=True)
    do_a = mm(dy_a, stacked("w_branch_a"), name="branch_a_dx", tb=True, b_stack=True)
    do_b = mm(dy_b, stacked("w_branch_b"), name="branch_b_dx", tb=True, b_stack=True)
    dq_na, dk_na, dv_na, dbias = na_bwd(z, bias, do_a, NH, S)
    g_rpb = na_rpb_grad(dbias)
    dq_n, dq_rr, dk_n, dv_m, dk_rr = mla_bwd(q_n, q_r, k_n, v_m, k_r, lse, do_b, MH, S)
    dq_r = rope(dq_rr, cos_t, -sin_t, name="rope_q_bwd", out_dtype=BF16)
    dk_r = rope(dk_rr, cos_t, -sin_t, name="rope_k_bwd", out_dtype=BF16)
    gw_uq_n = mm(c_q, dq_n, name="mla_dw_q_nope", ta=True, out_dtype=BF16)
    gw_uq_r = mm(c_q, dq_r, name="mla_dw_q_rope", ta=True, out_dtype=BF16)
    dc_q = mm(dq_n, wuq_n, name="mla_dcq_nope", tb=True)
    dc_q = mm(dq_r, wuq_r, name="mla_dcq_rope", tb=True, res=dc_q)
    gw_uk = mm(c_kv, dk_n, name="mla_dw_k", ta=True, out_dtype=BF16)
    gw_uv = mm(c_kv, dv_m, name="mla_dw_v", ta=True, out_dtype=BF16)
    dc_kv = mm(dk_n, wuk, name="mla_dckv_k", tb=True)
    dc_kv = mm(dv_m, wuv, name="mla_dckv_v", tb=True, res=dc_kv)
    dq_lat, g_qa = norm_bwd((z, QR, 3 * NAW // QR), q_a_norm, dc_q, name="q_a_dnorm", want_f32=False, bf16_alpha=1.0)
    dkv_lat, g_kva = norm_bwd((z, KVR, (3 * NAW + QR) // KVR), kv_a_norm, dc_kv, name="kv_a_dnorm", want_f32=False,
                              bf16_alpha=1.0)
    dz = jnp.concatenate([dq_na, dk_na.astype(BF16), dv_na.astype(BF16), dq_lat, dkv_lat, dga, dgb, dk_r], axis=1)
    gw_in_l = mm(u_mix, dz, name="mix_dw_in", ta=True, out_dtype=BF16)
    du_mix = mm(dz, win_l, name="mix_du", tb=True)
    dh1, dh1_half, g_mix = norm_bwd(h1, mix_norm, du_mix, name="mix_dnorm", res=dh2, bf16_alpha=0.5)
    (grad_x, g_ffn1), G["ffn1_w_gate"], G["ffn1_w_up"], G["ffn1_w_down"] = ffn_bwd(
        xs, ffn1_norm, ffn1_saved, dh1, dh1_half, "ffn1", "ffn1", last=True)

    def to_stack(g2d):
        k, n = g2d.shape
        return g2d.reshape(k, N_CHIPS, n // N_CHIPS).transpose(1, 0, 2)

    gw_in = jnp.concatenate([gw_in_l[:, :n_front], gw_in_l[:, off_kr:off_kr + MLA_ROPE], gw_in_l[:, n_front:off_kr]],
                            axis=1)
    G["w_in"] = to_stack(gw_in)
    gw_uq = jnp.concatenate([gw_uq_n.reshape(QR, MH, MLA_NOPE), gw_uq_r.reshape(QR, MH, LANES)[:, :, :MLA_ROPE]],
                            axis=2).reshape(QR, MH * MLA_QK)
    G["w_uq"] = to_stack(gw_uq)
    gw_ukv = jnp.stack([gw_uk.reshape(KVR, MH, HEAD_DIM), gw_uv.reshape(KVR, MH, HEAD_DIM)], axis=2)
    G["w_ukv"] = to_stack(gw_ukv.reshape(KVR, MH * 2 * HEAD_DIM))

    def four(n):
        g = G[n]
        if g.ndim == 2:
            return g.reshape(N_CHIPS, 2, g.shape[0] // (2 * N_CHIPS), g.shape[1])
        return g.reshape(N_CHIPS, 2, g.shape[1] // 2, g.shape[2])

    g4 = [four(n) for n in big]
    from_sibling = pair_exchange(g4)
    sums = [pair_sum(a, b, c_idx) for a, b in zip(g4, from_sibling)]
    from_chips = chip_scatter(sums)
    halves = [chip_sum(a, b, j_idx) for a, b in zip(sums, from_chips)]
    full = half_exchange(halves)
    grads = {n: f.reshape(W[n].shape) for n, f in zip(big, full)}

    small_g = {"ffn1_norm": g_ffn1, "mix_norm": g_mix, "q_a_norm": g_qa, "kv_a_norm": g_kva, "na_rpb": g_rpb,
               "ffn2_norm": g_ffn2, "pl_norm": g_pl, "final_norm": g_final}
    sizes = [int(np.prod(W[n].shape)) for n in small]
    total = sum(sizes)
    padded = -(-total // (8 * LANES)) * (8 * LANES)

    def pack(parts):
        flat = jnp.concatenate([jnp.reshape(parts[n], (-1,)).astype(F32) for n in small]
                               + [jnp.zeros((padded - total,), F32)])
        return flat.reshape(padded // LANES, LANES)

    def unpack(a):
        flat, out, o = a.reshape(-1), {}, 0
        for n, sz in zip(small, sizes):
            out[n] = flat[o:o + sz].reshape(W[n].shape)
            o += sz
        return out

    g_small = sum_devices(gather_small(pack(small_g)))
    d_small, m_small, v_small = adamw(pack(W), g_small, pack(Mo), pack(Vo), name="adamw_small")
    grads.update(unpack(g_small))
    delta, new_m, new_v = unpack(d_small), unpack(m_small), unpack(v_small)

    for n in big:
        shp = W[n].shape
        two_d = lambda a: a.reshape(shp[1], shp[2])
        d_, m_, v_ = adamw(two_d(W[n]), two_d(grads[n]), two_d(Mo[n]), two_d(Vo[n]), name="adamw_" + n)
        delta[n], new_m[n], new_v[n] = d_.reshape(shp), m_.reshape(shp), v_.reshape(shp)

    return (loss, grad_x[None], *[grads[n] for n in order], *[delta[n] for n in order],
            *[new_m[n] for n in order], *[new_v[n] for n in order])
```

```python
import functools

import numpy as np
import jax
import jax.numpy as jnp
from jax import lax
from jax.experimental import pallas as pl
from jax.experimental.pallas import tpu as pltpu

F32 = jnp.float32
BF16 = jnp.bfloat16

VMEM_LIMIT_V7X = 56 * 1024 * 1024
VMEM_BUDGET_V7X = 40 * 1024 * 1024
LANES = 128

GRID_W = 64
NA_WIN_ROWS = 8
NA_WIN_COLS = 16
HEAD_DIM = 128
MLA_NOPE = 128
MLA_ROPE = 64
MLA_QK = MLA_NOPE + MLA_ROPE
ROPE_THETA = 10000.0
NORM_EPS = 1e-6
NEG_INF = -1e30
N_CHIPS = 4

ADAM_LR = 0.001
ADAM_B1 = 0.9
ADAM_B2 = 0.999
ADAM_EPS = 1e-08
ADAM_WD = 0.01
ADAM_STEP = 10

MESH = pl.DeviceIdType.MESH
ANY = pl.BlockSpec(memory_space=pl.ANY)


def _params(sem=None):
    return pltpu.CompilerParams(dimension_semantics=sem, vmem_limit_bytes=VMEM_LIMIT_V7X)


def _pick(n, target, align):
    best = None
    t = align
    while t <= min(n, target):
        if n % t == 0:
            best = t
        t += align
    return n if best is None else best


def mm(a, b, *, name, ta=False, tb=False, out_dtype=F32, res=None, alpha=1.0, b_stack=False, out_stack=False,
       exact=False):
    K, M = (a.shape if ta else a.shape[::-1])
    if b_stack:
        nst = b.shape[0]
        if tb:
            N, kb = b.shape[1], b.shape[2]
            Kb, nb = nst * kb, None
        else:
            Kb, nb = b.shape[1], b.shape[2]
            N = nst * nb
    else:
        N, Kb = (b.shape if tb else b.shape[::-1])
    assert K == Kb, (a.shape, b.shape, ta, tb)
    if out_stack:
        assert N % N_CHIPS == 0
    n_unit = N // N_CHIPS if out_stack else (nb if (b_stack and not tb) else N)
    k_unit = kb if (b_stack and tb) else K
    tn = _pick(n_unit, 512, LANES) if n_unit % 512 == 0 or n_unit <= 512 else _pick(n_unit, 1536, LANES)
    tk = _pick(k_unit, 2048, LANES)
    tm = _pick(M, 1024, LANES if ta else 16)
    isz = lambda t: jnp.dtype(t.dtype).itemsize
    osz = jnp.dtype(out_dtype).itemsize

    def vmem(tm_):
        return (2 * tm_ * tk * isz(a) + 2 * tk * tn * isz(b) + 2 * tm_ * tn * osz + tm_ * tn * 4
                + (2 * tm_ * tn * isz(res) if res is not None else 0))

    while vmem(tm) > VMEM_BUDGET_V7X and tm % 2 == 0 and (tm // 2) % (LANES if ta else 16) == 0:
        tm //= 2
    nk = K // tk
    gm, gn = M // tm, N // tn

    a_spec = pl.BlockSpec((tk, tm), lambda i, j, k: (k, i)) if ta else pl.BlockSpec((tm, tk), lambda i, j, k: (i, k))
    if b_stack and not tb:
        q = nb // tn
        b_spec = pl.BlockSpec((None, tk, tn), lambda i, j, k: (j // q, k, j % q))
    elif b_stack and tb:
        q = kb // tk
        b_spec = pl.BlockSpec((None, tn, tk), lambda i, j, k: (k // q, j, k % q))
    elif tb:
        b_spec = pl.BlockSpec((tn, tk), lambda i, j, k: (j, k))
    else:
        b_spec = pl.BlockSpec((tk, tn), lambda i, j, k: (k, j))
    if out_stack:
        qo = (N // N_CHIPS) // tn
        o_spec = pl.BlockSpec((None, tm, tn), lambda i, j, k: (j // qo, i, j % qo))
        o_shape = jax.ShapeDtypeStruct((N_CHIPS, M, N // N_CHIPS), out_dtype)
    else:
        o_spec = pl.BlockSpec((tm, tn), lambda i, j, k: (i, j))
        o_shape = jax.ShapeDtypeStruct((M, N), out_dtype)
    dims = (((0 if ta else 1,), (1 if tb else 0,)), ((), ()))
    has_res = res is not None

    def body(*refs):
        if has_res:
            a_ref, b_ref, r_ref, o_ref, acc_ref = refs
        else:
            a_ref, b_ref, o_ref, acc_ref = refs
            r_ref = None
        k = pl.program_id(2)
        if exact:
            part = lax.dot_general(a_ref[...], b_ref[...], dims, preferred_element_type=F32,
                                   precision=lax.Precision.HIGHEST)
        else:
            part = lax.dot_general(a_ref[...].astype(BF16), b_ref[...].astype(BF16), dims,
                                   preferred_element_type=F32)

        def finish(total):
            if alpha != 1.0:
                total = total * alpha
            if has_res:
                total = total + r_ref[...].astype(F32)
            o_ref[...] = total.astype(out_dtype)

        if nk == 1:
            finish(part)
        else:
            @pl.when(k == 0)
            def _():
                acc_ref[...] = part

            @pl.when(jnp.logical_and(k > 0, k < nk - 1))
            def _():
                acc_ref[...] += part

            @pl.when(k == nk - 1)
            def _():
                finish(acc_ref[...] + part)

    in_specs = [a_spec, b_spec]
    args = [a, b]
    if has_res:
        in_specs.append(pl.BlockSpec((tm, tn), lambda i, j, k: (i, j)))
        args.append(res)
    return pl.pallas_call(
        body, name=name, grid=(gm, gn, nk), in_specs=in_specs, out_specs=o_spec, out_shape=o_shape,
        scratch_shapes=[pltpu.VMEM((tm, tn) if nk > 1 else (8, LANES), F32)],
        compiler_params=_params(("parallel", "parallel", "arbitrary")),
    )(*args)


def rowwise(fn, rows, consts, outs, accs=(), *, tm, name, tn=None):
    rows = [r if isinstance(r, tuple) else (r, r.shape[1], 0) for r in rows]
    S = rows[0][0].shape[0]
    tm = _pick(S, tm, 16)
    nrow, ncon, nout = len(rows), len(consts), len(outs)
    if tn is None:
        grid = (S // tm,)
        in_specs = [pl.BlockSpec((tm, w), functools.partial(lambda i, cb: (i, cb), cb=cb)) for _, w, cb in rows]
        in_specs += [pl.BlockSpec(c.shape, lambda i: (0, 0)) for c in consts]
        out_specs = [pl.BlockSpec((tm, n), lambda i: (i, 0)) for n, _ in outs]
        out_specs += [pl.BlockSpec(s, lambda i: (0, 0)) for s in accs]
        sem = ("arbitrary",)
    else:
        assert not accs
        N = rows[0][1]
        grid = (S // tm, N // tn)
        in_specs = [pl.BlockSpec((tm, tn), lambda i, j: (i, j)) for _ in rows]
        in_specs += [pl.BlockSpec(c.shape, lambda i, j: (0, 0)) for c in consts]
        out_specs = [pl.BlockSpec((tm, tn), lambda i, j: (i, j)) for _ in outs]
        sem = ("parallel", "parallel")
    out_shape = [jax.ShapeDtypeStruct((S, n), dt) for n, dt in outs]
    out_shape += [jax.ShapeDtypeStruct(s, F32) for s in accs]

    def body(*refs):
        vals = fn(*[r[...] for r in refs[:nrow + ncon]])
        if not isinstance(vals, (tuple, list)):
            vals = (vals,)
        o_refs = refs[nrow + ncon:]
        for o_ref, v in zip(o_refs[:nout], vals[:nout]):
            o_ref[...] = v.astype(o_ref.dtype)
        if accs:
            first = pl.program_id(0) == 0

            def accumulate(a_ref, v):
                @pl.when(first)
                def _():
                    a_ref[...] = v

                @pl.when(jnp.logical_not(first))
                def _():
                    a_ref[...] += v

            for a_ref, v in zip(o_refs[nout:], vals[nout:]):
                accumulate(a_ref, v.astype(F32))

    res = pl.pallas_call(
        body, name=name, grid=grid, in_specs=in_specs, out_specs=out_specs, out_shape=out_shape,
        compiler_params=_params(sem),
    )(*[r[0] for r in rows], *consts)
    return res


def _rstd(x):
    return lax.rsqrt(jnp.mean(x * x, axis=-1, keepdims=True) + NORM_EPS)


def norm_fwd(x, g, *, name, tm=256):
    w = x[1] if isinstance(x, tuple) else x.shape[1]

    def fn(xb, gb):
        return (xb * _rstd(xb)) * gb

    return rowwise(fn, [x], [g], [(w, BF16)], tm=tm, name=name)[0]


def norm_bwd(x, g, dn, *, name, res=None, want_f32=True, bf16_alpha=None, tm=256):
    w = x[1] if isinstance(x, tuple) else x.shape[1]
    has_res = res is not None

    def fn(*blocks):
        if has_res:
            xb, dnb, rb, gb = blocks
        else:
            xb, dnb, gb = blocks
        r = _rstd(xb)
        xh = xb * r
        dxh = dnb * gb
        dx = r * (dxh - xh * jnp.mean(dxh * xh, axis=-1, keepdims=True))
        if has_res:
            dx = dx + rb
        out = []
        if want_f32:
            out.append(dx)
        if bf16_alpha is not None:
            out.append(dx * bf16_alpha if bf16_alpha != 1.0 else dx)
        out.append(jnp.sum(dnb * xh, axis=0, keepdims=True))
        return tuple(out)

    outs = ([(w, F32)] if want_f32 else []) + ([(w, BF16)] if bf16_alpha is not None else [])
    rows = [x, dn] + ([res] if has_res else [])
    return rowwise(fn, rows, [g], outs, accs=[(1, w)], tm=tm, name=name)


def _sig(x):
    return jax.nn.sigmoid(x)


def swiglu_fwd(g, u, *, name):
    return rowwise(lambda gb, ub: gb * _sig(gb) * ub, [g, u], [], [(g.shape[1], BF16)], tm=256, name=name,
                   tn=_pick(g.shape[1], 1536, LANES))[0]


def swiglu_bwd(g, u, da, *, name):
    def fn(gb, ub, dab):
        s = _sig(gb)
        return dab * ub * (s + gb * s * (1.0 - s)), dab * (gb * s)

    n = g.shape[1]
    return rowwise(fn, [g, u, da], [], [(n, BF16), (n, BF16)], tm=256, name=name, tn=_pick(n, 1536, LANES))


def rope(x, cos, sin_signed, *, name, out_dtype):
    w = x[1] if isinstance(x, tuple) else x.shape[1]
    half = MLA_ROPE // 2

    def fn(xb, cb, sb):
        lane = lax.broadcasted_iota(jnp.int32, cb.shape, 1)
        outs = []
        for hb in range(w // LANES):
            blk = xb[:, hb * LANES:(hb + 1) * LANES]
            partner = jnp.where(lane < half, pltpu.roll(blk, LANES - half, 1), pltpu.roll(blk, half, 1))
            outs.append(blk * cb + partner * sb)
        return outs[0] if len(outs) == 1 else jnp.concatenate(outs, axis=1)

    return rowwise(fn, [x, cos, sin_signed], [], [(w, out_dtype)], tm=256, name=name)[0]


def _na_tables():
    cols = np.arange(GRID_W)
    kw = NA_WIN_COLS
    dc = np.clip(cols[None, :] - cols[:, None], -(kw - 1), kw - 1) + (kw - 1)
    onehot = np.zeros((LANES, GRID_W * GRID_W), np.float32)
    onehot[dc.reshape(-1), np.arange(GRID_W * GRID_W)] = 1.0
    col_start = np.clip(cols - kw // 2, 0, GRID_W - kw)
    mask = (cols[None, :] >= col_start[:, None]) & (cols[None, :] < col_start[:, None] + kw)
    return onehot, np.where(mask, 0.0, NEG_INF).astype(np.float32)


def na_bias(rpb):
    H = rpb.shape[0]
    nr, kh = 2 * NA_WIN_ROWS - 1, NA_WIN_ROWS
    onehot, maskb = _na_tables()
    rp = jnp.pad(rpb.reshape(H * nr, 2 * NA_WIN_COLS - 1), ((0, 0), (0, LANES - (2 * NA_WIN_COLS - 1))))
    t1 = mm(rp, jnp.asarray(onehot), name="na_bias_table", exact=True).reshape(H, nr, GRID_W, GRID_W)
    t1 = t1 + jnp.asarray(maskb)[None, None]
    per_t = [jnp.stack([t1[:, i - t + kh - 1] for i in range(kh)], axis=2) for t in range(kh)]
    return jnp.stack(per_t, axis=1).reshape(H, kh, GRID_W, kh * GRID_W)


def na_rpb_grad(db):
    H = db.shape[0]
    nr, kh = 2 * NA_WIN_ROWS - 1, NA_WIN_ROWS
    onehot, _ = _na_tables()
    db = db.reshape(H, kh, GRID_W, kh, GRID_W)
    per_dr = []
    for dri in range(nr):
        terms = [db[:, t, :, dri - (kh - 1) + t, :] for t in range(kh) if 0 <= dri - (kh - 1) + t < kh]
        per_dr.append(functools.reduce(jnp.add, terms))
    dt1 = jnp.stack(per_dr, axis=1).reshape(H * nr, GRID_W * GRID_W)
    g = mm(dt1, jnp.asarray(onehot), name="na_rpb_grad", tb=True, exact=True)
    return g[:, :2 * NA_WIN_COLS - 1].reshape(H, nr, 2 * NA_WIN_COLS - 1)


def _na_first_row(r, rows):
    return jnp.clip(r - NA_WIN_ROWS // 2, 0, rows - NA_WIN_ROWS)


def _na_scores(q_ref, k_ref, b_ref, start):
    q = q_ref[...].astype(BF16)
    k = k_ref[pl.ds(start, NA_WIN_ROWS * GRID_W), :].astype(BF16)
    s = lax.dot_general(q, k, (((1,), (1,)), ((), ())), preferred_element_type=F32)
    s = s * (HEAD_DIM ** -0.5) + b_ref[...]
    m = jnp.max(s, axis=-1, keepdims=True)
    e = jnp.exp(s - m)
    return q, k, e / jnp.sum(e, axis=-1, keepdims=True)


def na_fwd(z, bias, H, S):
    rows = S // GRID_W
    nkeys = NA_WIN_ROWS * GRID_W

    def body(q_ref, k_ref, v_ref, b_ref, o_ref):
        r = pl.program_id(1)
        start = pl.multiple_of(_na_first_row(r, rows) * GRID_W, GRID_W)
        _, _, p = _na_scores(q_ref, k_ref, b_ref, start)
        v = v_ref[pl.ds(start, nkeys), :].astype(BF16)
        o_ref[...] = jnp.dot(p.astype(BF16), v, preferred_element_type=F32).astype(o_ref.dtype)

    return pl.pallas_call(
        body, name="na_fwd", grid=(H, rows),
        in_specs=[pl.BlockSpec((GRID_W, HEAD_DIM), lambda h, r: (r, h)),
                  pl.BlockSpec((S, HEAD_DIM), lambda h, r: (0, H + h)),
                  pl.BlockSpec((S, HEAD_DIM), lambda h, r: (0, 2 * H + h)),
                  pl.BlockSpec((None, None, GRID_W, nkeys), lambda h, r: (h, r - _na_first_row(r, rows), 0, 0))],
        out_specs=pl.BlockSpec((GRID_W, HEAD_DIM), lambda h, r: (r, h)),
        out_shape=jax.ShapeDtypeStruct((S, H * HEAD_DIM), BF16),
        compiler_params=_params(("parallel", "arbitrary")),
    )(z, z, z, bias)


def na_bwd(z, bias, do, H, S):
    rows = S // GRID_W
    nkeys = NA_WIN_ROWS * GRID_W
    tn_dims = (((0,), (0,)), ((), ()))

    def body(q_ref, k_ref, v_ref, b_ref, do_ref, dq_ref, dk_ref, dv_ref, db_ref):
        r = pl.program_id(1)
        start = pl.multiple_of(_na_first_row(r, rows) * GRID_W, GRID_W)
        q, k, p = _na_scores(q_ref, k_ref, b_ref, start)
        v = v_ref[pl.ds(start, nkeys), :].astype(BF16)
        dob = do_ref[...].astype(BF16)
        dp = lax.dot_general(dob, v, (((1,), (1,)), ((), ())), preferred_element_type=F32)
        ds = p * (dp - jnp.sum(dp * p, axis=-1, keepdims=True))
        dsb = (ds * (HEAD_DIM ** -0.5)).astype(BF16)
        dq_ref[...] = jnp.dot(dsb, k, preferred_element_type=F32).astype(dq_ref.dtype)

        @pl.when(r == 0)
        def _():
            dk_ref[...] = jnp.zeros_like(dk_ref)
            dv_ref[...] = jnp.zeros_like(dv_ref)

        dk_ref[pl.ds(start, nkeys), :] += lax.dot_general(dsb, q, tn_dims, preferred_element_type=F32)
        dv_ref[pl.ds(start, nkeys), :] += lax.dot_general(p.astype(BF16), dob, tn_dims, preferred_element_type=F32)

        fresh = jnp.logical_or(r <= NA_WIN_ROWS // 2, r > rows - NA_WIN_ROWS // 2)

        @pl.when(fresh)
        def _():
            db_ref[...] = ds

        @pl.when(jnp.logical_not(fresh))
        def _():
            db_ref[...] += ds

    W = H * HEAD_DIM
    return pl.pallas_call(
        body, name="na_bwd", grid=(H, rows),
        in_specs=[pl.BlockSpec((GRID_W, HEAD_DIM), lambda h, r: (r, h)),
                  pl.BlockSpec((S, HEAD_DIM), lambda h, r: (0, H + h)),
                  pl.BlockSpec((S, HEAD_DIM), lambda h, r: (0, 2 * H + h)),
                  pl.BlockSpec((None, None, GRID_W, nkeys), lambda h, r: (h, r - _na_first_row(r, rows), 0, 0)),
                  pl.BlockSpec((GRID_W, HEAD_DIM), lambda h, r: (r, h))],
        out_specs=[pl.BlockSpec((GRID_W, HEAD_DIM), lambda h, r: (r, h)),
                   pl.BlockSpec((S, HEAD_DIM), lambda h, r: (0, h)),
                   pl.BlockSpec((S, HEAD_DIM), lambda h, r: (0, h)),
                   pl.BlockSpec((None, None, GRID_W, nkeys), lambda h, r: (h, r - _na_first_row(r, rows), 0, 0))],
        out_shape=[jax.ShapeDtypeStruct((S, W), BF16), jax.ShapeDtypeStruct((S, W), F32),
                   jax.ShapeDtypeStruct((S, W), F32), jax.ShapeDtypeStruct((H, NA_WIN_ROWS, GRID_W, nkeys), F32)],
        compiler_params=_params(("arbitrary", "arbitrary")),
    )(z, z, z, bias, do)


def _mla_scores(qn_ref, qr_ref, kn_ref, kr_ref):
    nt = (((1,), (1,)), ((), ()))
    s = lax.dot_general(qn_ref[...], kn_ref[...], nt, preferred_element_type=F32)
    s = s + lax.dot_general(qr_ref[...], kr_ref[...], nt, preferred_element_type=F32)
    return s * (MLA_QK ** -0.5)


def mla_fwd(qn, qr, kn, v, kr, H, S):
    tq = _pick(S, 256, 16)

    def body(qn_ref, qr_ref, kn_ref, v_ref, kr_ref, o_ref, lse_ref):
        s = _mla_scores(qn_ref, qr_ref, kn_ref, kr_ref)
        m = jnp.max(s, axis=-1, keepdims=True)
        e = jnp.exp(s - m)
        l = jnp.sum(e, axis=-1, keepdims=True)
        o_ref[...] = jnp.dot((e / l).astype(BF16), v_ref[...], preferred_element_type=F32).astype(o_ref.dtype)
        lse_ref[...] = jnp.broadcast_to(m + jnp.log(l), lse_ref.shape)

    qspec = pl.BlockSpec((tq, HEAD_DIM), lambda h, i: (i, h))
    kspec = pl.BlockSpec((S, HEAD_DIM), lambda h, i: (0, h))
    return pl.pallas_call(
        body, name="mla_fwd", grid=(H, S // tq),
        in_specs=[qspec, qspec, kspec, kspec, pl.BlockSpec((S, LANES), lambda h, i: (0, 0))],
        out_specs=[qspec, qspec],
        out_shape=[jax.ShapeDtypeStruct((S, H * HEAD_DIM), BF16), jax.ShapeDtypeStruct((S, H * LANES), F32)],
        compiler_params=_params(("parallel", "arbitrary")),
    )(qn, qr, kn, v, kr)


def mla_bwd(qn, qr, kn, v, kr, lse, do, H, S):
    tq = _pick(S, 256, 16)
    nt = (((1,), (1,)), ((), ()))
    tn_dims = (((0,), (0,)), ((), ()))

    def body(qn_ref, qr_ref, kn_ref, v_ref, kr_ref, lse_ref, do_ref, dqn_ref, dqr_ref, dkn_ref, dv_ref, dkr_ref):
        h, i = pl.program_id(0), pl.program_id(1)
        s = _mla_scores(qn_ref, qr_ref, kn_ref, kr_ref)
        p = jnp.exp(s - lse_ref[:, 0:1])
        dob = do_ref[...].astype(BF16)
        dp = lax.dot_general(dob, v_ref[...], nt, preferred_element_type=F32)
        ds = p * (dp - jnp.sum(dp * p, axis=-1, keepdims=True))
        dsb = (ds * (MLA_QK ** -0.5)).astype(BF16)
        dqn_ref[...] = jnp.dot(dsb, kn_ref[...], preferred_element_type=F32).astype(dqn_ref.dtype)
        dqr_ref[...] = jnp.dot(dsb, kr_ref[...], preferred_element_type=F32).astype(dqr_ref.dtype)

        @pl.when(i == 0)
        def _():
            dkn_ref[...] = jnp.zeros_like(dkn_ref)
            dv_ref[...] = jnp.zeros_like(dv_ref)

        @pl.when(jnp.logical_and(i == 0, h == 0))
        def _():
            dkr_ref[...] = jnp.zeros_like(dkr_ref)

        dkn_ref[...] += lax.dot_general(dsb, qn_ref[...], tn_dims, preferred_element_type=F32)
        dkr_ref[...] += lax.dot_general(dsb, qr_ref[...], tn_dims, preferred_element_type=F32)
        dv_ref[...] += lax.dot_general(p.astype(BF16), dob, tn_dims, preferred_element_type=F32)

    qspec = pl.BlockSpec((tq, HEAD_DIM), lambda h, i: (i, h))
    kspec = pl.BlockSpec((S, HEAD_DIM), lambda h, i: (0, h))
    rspec = pl.BlockSpec((S, LANES), lambda h, i: (0, 0))
    W = H * HEAD_DIM
    return pl.pallas_call(
        body, name="mla_bwd", grid=(H, S // tq),
        in_specs=[qspec, qspec, kspec, kspec, rspec, qspec, qspec],
        out_specs=[qspec, qspec, kspec, kspec, rspec],
        out_shape=[jax.ShapeDtypeStruct((S, W), BF16), jax.ShapeDtypeStruct((S, W), F32),
                   jax.ShapeDtypeStruct((S, W), F32), jax.ShapeDtypeStruct((S, W), F32),
                   jax.ShapeDtypeStruct((S, LANES), F32)],
        compiler_params=_params(("arbitrary", "arbitrary")),
    )(qn, qr, kn, v, kr, lse, do)


def _place():
    return lax.axis_index("x"), lax.axis_index("y"), lax.axis_index("c")


def _other_chips(x, y):
    return [(1 - x, y), (x, 1 - y), (1 - x, 1 - y)]


def _remote(src, dst, send_sem, recv_sem, to):
    return pltpu.make_async_remote_copy(src_ref=src, dst_ref=dst, send_sem=send_sem, recv_sem=recv_sem,
                                        device_id=to, device_id_type=MESH)


def gather_weights(shards):
    n = len(shards)

    def body(*refs):
        ins, outs = refs[:n], refs[n:2 * n]
        send, recv = refs[2 * n:]
        x, y, c = _place()
        me = 2 * x + y
        chips = _other_chips(x, y)
        sibling = (x, y, 1 - c)
        first = []
        for w in range(n):
            for k, (px, py) in enumerate(chips):
                cp = _remote(ins[w].at[c], outs[w].at[me, c], send.at[6 * w + k], recv.at[6 * w + k], (px, py, c))
                cp.start()
                first.append(cp)
        passed = []
        for w in range(n):
            for k, (px, py) in enumerate(chips):
                blk = outs[w].at[2 * px + py, c]
                _remote(blk, blk, send.at[6 * w + k], recv.at[6 * w + k], (px, py, c)).wait_recv()
                cp = _remote(blk, blk, send.at[6 * w + 3 + k], recv.at[6 * w + 3 + k], sibling)
                cp.start()
                passed.append(cp)
        for w in range(n):
            for k, (px, py) in enumerate(chips):
                blk = outs[w].at[2 * px + py, 1 - c]
                _remote(blk, blk, send.at[6 * w + 3 + k], recv.at[6 * w + 3 + k], sibling).wait_recv()
        for cp in first + passed:
            cp.wait_send()

    return pl.pallas_call(
        body, name="gather_weights", in_specs=[ANY] * n, out_specs=[ANY] * n,
        out_shape=[jax.ShapeDtypeStruct((N_CHIPS,) + s.shape, s.dtype) for s in shards],
        scratch_shapes=[pltpu.SemaphoreType.DMA((6 * n,)), pltpu.SemaphoreType.DMA((6 * n,))],
    )(*shards)


def pair_exchange(grads):
    n = len(grads)

    def body(*refs):
        ins, outs = refs[:n], refs[n:2 * n]
        send, recv = refs[2 * n:]
        x, y, c = _place()
        cps = []
        for w in range(n):
            cp = _remote(ins[w].at[:, 1 - c], outs[w], send.at[w], recv.at[w], (x, y, 1 - c))
            cp.start()
            cps.append(cp)
        for cp in cps:
            cp.wait()

    return pl.pallas_call(
        body, name="grad_pair_exchange", in_specs=[ANY] * n, out_specs=[ANY] * n,
        out_shape=[jax.ShapeDtypeStruct((g.shape[0],) + g.shape[2:], g.dtype) for g in grads],
        scratch_shapes=[pltpu.SemaphoreType.DMA((n,)), pltpu.SemaphoreType.DMA((n,))],
    )(*grads)


def chip_scatter(sums):
    n = len(sums)

    def body(*refs):
        ins, outs = refs[:n], refs[n:2 * n]
        send, recv = refs[2 * n:]
        x, y, c = _place()
        cps = []
        for w in range(n):
            for k, (px, py) in enumerate(_other_chips(x, y)):
                cp = _remote(ins[w].at[2 * px + py], outs[w].at[k], send.at[3 * w + k], recv.at[3 * w + k], (px, py, c))
                cp.start()
                cps.append(cp)
        for cp in cps:
            cp.wait()

    return pl.pallas_call(
        body, name="grad_chip_scatter", in_specs=[ANY] * n, out_specs=[ANY] * n,
        out_shape=[jax.ShapeDtypeStruct((3,) + s.shape[1:], s.dtype) for s in sums],
        scratch_shapes=[pltpu.SemaphoreType.DMA((3 * n,)), pltpu.SemaphoreType.DMA((3 * n,))],
    )(*sums)


def half_exchange(halves):
    n = len(halves)

    def body(*refs):
        ins, outs = refs[:n], refs[n:2 * n]
        send, recv = refs[2 * n:]
        x, y, c = _place()
        cps = []
        for w in range(n):
            cp = _remote(ins[w], outs[w], send.at[w], recv.at[w], (x, y, 1 - c))
            cp.start()
            cps.append(cp)
        for cp in cps:
            cp.wait()

    return pl.pallas_call(
        body, name="grad_half_exchange", in_specs=[ANY] * n, out_specs=[ANY] * n,
        out_shape=[jax.ShapeDtypeStruct(h.shape, h.dtype) for h in halves],
        scratch_shapes=[pltpu.SemaphoreType.DMA((n,)), pltpu.SemaphoreType.DMA((n,))],
    )(*halves)


def gather_small(v):
    def body(v_ref, o_ref, send, recv, local):
        x, y, c = _place()
        me = 4 * x + 2 * y + c
        own = pltpu.make_async_copy(v_ref, o_ref.at[me], local)
        own.start()
        cps = []
        for k in range(1, 8):
            fx, fy, fc = (k >> 2) & 1, (k >> 1) & 1, k & 1
            to = (x ^ fx if fx else x, y ^ fy if fy else y, c ^ fc if fc else c)
            cp = _remote(v_ref, o_ref.at[me], send.at[k - 1], recv.at[k - 1], to)
            cp.start()
            cps.append(cp)
        for k in range(1, 8):
            fx, fy, fc = (k >> 2) & 1, (k >> 1) & 1, k & 1
            px, py, pc = (x ^ fx if fx else x, y ^ fy if fy else y, c ^ fc if fc else c)
            cps[k - 1].wait_send()
            _remote(v_ref, o_ref.at[4 * px + 2 * py + pc], send.at[k - 1], recv.at[k - 1], (px, py, pc)).wait_recv()
        own.wait()

    return pl.pallas_call(
        body, name="gather_small_grads", in_specs=[ANY], out_specs=ANY,
        out_shape=jax.ShapeDtypeStruct((8,) + v.shape, v.dtype),
        scratch_shapes=[pltpu.SemaphoreType.DMA((7,)), pltpu.SemaphoreType.DMA((7,)), pltpu.SemaphoreType.DMA],
    )(v)


def _row_tile(rows, cols, nbuf_bytes):
    tm = _pick(rows, 512, 16)
    while tm * cols * nbuf_bytes * 2 > VMEM_BUDGET_V7X and tm % 32 == 0:
        tm //= 2
    return tm


def pair_sum(g, r, c_idx):
    _, _, rows, cols = g.shape
    tm = _row_tile(rows, cols, 2 + 2 + 2)
    nb = rows // tm

    def body(c_ref, g_ref, r_ref, o_ref):
        o_ref[...] = (g_ref[...].astype(F32) + r_ref[...].astype(F32)).astype(o_ref.dtype)

    gs = pltpu.PrefetchScalarGridSpec(
        num_scalar_prefetch=1, grid=(N_CHIPS, nb),
        in_specs=[pl.BlockSpec((None, None, tm, cols), lambda j, i, c_ref: (j, c_ref[0], i, 0)),
                  pl.BlockSpec((None, tm, cols), lambda j, i, c_ref: (j, i, 0))],
        out_specs=pl.BlockSpec((None, tm, cols), lambda j, i, c_ref: (j, i, 0)))
    return pl.pallas_call(body, name="grad_pair_sum", grid_spec=gs,
                          out_shape=jax.ShapeDtypeStruct(r.shape, BF16),
                          compiler_params=_params(("arbitrary", "arbitrary")))(c_idx, g, r)


def chip_sum(s, r, j_idx):
    _, rows, cols = s.shape
    tm = _row_tile(rows, cols, 2 + 3 * 2 + 4)
    nb = rows // tm

    def body(j_ref, s_ref, r_ref, o_ref):
        t = s_ref[...].astype(F32)
        for k in range(3):
            t = t + r_ref[k].astype(F32)
        o_ref[...] = t

    gs = pltpu.PrefetchScalarGridSpec(
        num_scalar_prefetch=1, grid=(nb,),
        in_specs=[pl.BlockSpec((None, tm, cols), lambda i, j_ref: (j_ref[0], i, 0)),
                  pl.BlockSpec((3, tm, cols), lambda i, j_ref: (0, i, 0))],
        out_specs=pl.BlockSpec((tm, cols), lambda i, j_ref: (i, 0)))
    return pl.pallas_call(body, name="grad_chip_sum", grid_spec=gs,
                          out_shape=jax.ShapeDtypeStruct((rows, cols), F32),
                          compiler_params=_params(("arbitrary",)))(j_idx, s, r)


def adamw(w, g, m, v, *, name):
    rows, cols = w.shape
    tm = _row_tile(rows, cols, 7 * 4)

    return rowwise(_adamw_math, [w, g, m, v], [], [(cols, F32)] * 3, tm=tm, name=name)


def _adamw_math(wb, gb, mb, vb):
    m2 = ADAM_B1 * mb + (1.0 - ADAM_B1) * gb
    v2 = ADAM_B2 * vb + (1.0 - ADAM_B2) * (gb * gb)
    m_hat = m2 / (1.0 - ADAM_B1 ** ADAM_STEP)
    v_hat = v2 / (1.0 - ADAM_B2 ** ADAM_STEP)
    delta = -ADAM_LR * (m_hat / (jnp.sqrt(v_hat) + ADAM_EPS) + ADAM_WD * wb)
    return delta, m2, v2


def adamw_shard(w, g_own, g_sib, m, v, c_idx, *, name):
    rows, cols = g_own.shape
    tm = _row_tile(rows, cols, 9 * 4)
    nb = rows // tm

    def body(c_ref, w_ref, go_ref, gs_ref, m_ref, v_ref, g_out, d_out, m_out, v_out):
        gb = jnp.where(pl.program_id(0) == c_ref[0], go_ref[...], gs_ref[...])
        delta, m2, v2 = _adamw_math(w_ref[...], gb, m_ref[...], v_ref[...])
        g_out[...] = gb
        d_out[...] = delta
        m_out[...] = m2
        v_out[...] = v2

    full = pl.BlockSpec((tm, cols), lambda h, i, c_ref: (h * nb + i, 0))
    half = pl.BlockSpec((tm, cols), lambda h, i, c_ref: (i, 0))
    gs = pltpu.PrefetchScalarGridSpec(num_scalar_prefetch=1, grid=(2, nb), in_specs=[full, half, half, full, full],
                                      out_specs=[full] * 4)
    return pl.pallas_call(body, name=name, grid_spec=gs, out_shape=[jax.ShapeDtypeStruct(w.shape, F32)] * 4,
                          compiler_params=_params(("arbitrary", "arbitrary")))(c_idx, w, g_own, g_sib, m, v)


def sum_devices(a):
    def body(a_ref, o_ref):
        t = a_ref[0]
        for k in range(1, 8):
            t = t + a_ref[k]
        o_ref[...] = t

    return pl.pallas_call(body, name="sum_small_grads", out_shape=jax.ShapeDtypeStruct(a.shape[1:], a.dtype))(a)


def _halves(w2d):
    r, c = w2d.shape
    return w2d.reshape(2, r // 2, c)


def kernel(x, p, ffn1_norm, ffn1_w_gate, ffn1_w_up, ffn1_w_down, mix_norm, w_in, q_a_norm, w_uq, kv_a_norm, w_ukv, na_rpb, w_branch_a, w_branch_b, w_out, ffn2_norm, ffn2_w_gate, ffn2_w_up, ffn2_w_down, pl_norm, w_pl, w_pl_gate, final_norm, loss_target, m_ffn1_norm, m_ffn1_w_gate, m_ffn1_w_up, m_ffn1_w_down, m_mix_norm, m_w_in, m_q_a_norm, m_w_uq, m_kv_a_norm, m_w_ukv, m_na_rpb, m_w_branch_a, m_w_branch_b, m_w_out, m_ffn2_norm, m_ffn2_w_gate, m_ffn2_w_up, m_ffn2_w_down, m_pl_norm, m_w_pl, m_w_pl_gate, m_final_norm, v_ffn1_norm, v_ffn1_w_gate, v_ffn1_w_up, v_ffn1_w_down, v_mix_norm, v_w_in, v_q_a_norm, v_w_uq, v_kv_a_norm, v_w_ukv, v_na_rpb, v_w_branch_a, v_w_branch_b, v_w_out, v_ffn2_norm, v_ffn2_w_gate, v_ffn2_w_up, v_ffn2_w_down, v_pl_norm, v_w_pl, v_w_pl_gate, v_final_norm):
    big = ["ffn1_w_gate", "ffn1_w_up", "ffn1_w_down", "w_in", "w_uq", "w_ukv", "w_branch_a", "w_branch_b", "w_out",
           "ffn2_w_gate", "ffn2_w_up", "ffn2_w_down", "w_pl", "w_pl_gate"]
    col_sharded = {"ffn1_w_gate", "ffn1_w_up", "w_in", "w_uq", "w_ukv", "w_branch_a", "w_branch_b", "ffn2_w_gate",
                   "ffn2_w_up", "w_pl"}
    small = ["ffn1_norm", "mix_norm", "q_a_norm", "kv_a_norm", "na_rpb", "ffn2_norm", "pl_norm", "final_norm"]
    order = ["ffn1_norm", "ffn1_w_gate", "ffn1_w_up", "ffn1_w_down", "mix_norm", "w_in", "q_a_norm", "w_uq",
             "kv_a_norm", "w_ukv", "na_rpb", "w_branch_a", "w_branch_b", "w_out", "ffn2_norm", "ffn2_w_gate",
             "ffn2_w_up", "ffn2_w_down", "pl_norm", "w_pl", "w_pl_gate", "final_norm"]
    env = dict(locals())
    W = {n: env[n] for n in order}
    Mo = {n: env["m_" + n] for n in order}
    Vo = {n: env["v_" + n] for n in order}

    xs = x[0]
    S, D = xs.shape
    tgt = loss_target[0]
    ps = p[0, 0]
    NAW = w_branch_a.shape[1]
    MLAW = w_branch_b.shape[1]
    NH, MH = NAW // HEAD_DIM, MLAW // HEAD_DIM
    QR, KVR = w_uq.shape[1], w_ukv.shape[1]
    F = ffn1_w_down.shape[1] * N_CHIPS
    cx, cy, cc = _place()
    c_idx = jnp.reshape(cc, (1,)).astype(jnp.int32)
    j_idx = jnp.reshape(2 * cx + cy, (1,)).astype(jnp.int32)

    shards = [_halves(W[n][0].astype(BF16)) for n in big]
    me_chip = 2 * cx + cy
    gathered = {n: lax.dynamic_update_slice(g, s[None], (me_chip, 0, 0, 0))
                for n, g, s in zip(big, gather_weights(shards), shards)}

    def stacked(n):
        g = gathered[n]
        return g.reshape(N_CHIPS, 2 * g.shape[2], g.shape[3])

    def plain(n):
        g = gathered[n]
        if n in col_sharded:
            st = stacked(n)
            return st.transpose(1, 0, 2).reshape(st.shape[1], N_CHIPS * st.shape[2])
        return g.reshape(N_CHIPS * 2 * g.shape[2], g.shape[3])

    n_front = 3 * NAW + QR + KVR
    win = plain("w_in")
    win_l = jnp.concatenate([win[:, :n_front], win[:, n_front + MLA_ROPE:], win[:, n_front:n_front + MLA_ROPE],
                             jnp.zeros((D, LANES - MLA_ROPE), BF16)], axis=1)
    ZW = win_l.shape[1]
    off_ga = n_front
    off_kr = n_front + 2 * D
    wuq = plain("w_uq").reshape(QR, MH, MLA_QK)
    wuq_n = wuq[:, :, :MLA_NOPE].reshape(QR, MH * MLA_NOPE)
    wuq_r = jnp.pad(wuq[:, :, MLA_NOPE:], ((0, 0), (0, 0), (0, LANES - MLA_ROPE))).reshape(QR, MH * LANES)
    wukv = plain("w_ukv").reshape(KVR, MH, 2, HEAD_DIM)
    wuk = wukv[:, :, 0].reshape(KVR, MH * HEAD_DIM)
    wuv = wukv[:, :, 1].reshape(KVR, MH * HEAD_DIM)

    pos = jnp.arange(S, dtype=F32)
    inv_freq = 1.0 / (ROPE_THETA ** (jnp.arange(0, MLA_ROPE, 2, dtype=F32) / MLA_ROPE))
    ang = pos[:, None] * inv_freq[None, :]
    zpad = jnp.zeros((S, LANES - MLA_ROPE), F32)
    cos_t = jnp.concatenate([jnp.cos(ang), jnp.cos(ang), zpad], axis=1)
    sin_t = jnp.concatenate([-jnp.sin(ang), jnp.sin(ang), zpad], axis=1)

    def ffn_fwd(h, norm_g, tag, pre):
        n = norm_fwd(h, norm_g, name=f"{tag}_norm")
        g = mm(n, stacked(pre + "_w_gate"), name=f"{tag}_gate", b_stack=True)
        u = mm(n, stacked(pre + "_w_up"), name=f"{tag}_up", b_stack=True)
        a = swiglu_fwd(g, u, name=f"{tag}_act")
        h_out = mm(a, plain(pre + "_w_down"), name=f"{tag}_down", res=h, alpha=0.5)
        return h_out, (n, g, u, a)

    def ffn_bwd(h, norm_g, saved, dh, dh_half, tag, pre, last):
        n, g, u, a = saved
        gw_down = mm(a, dh_half, name=f"{tag}_dw_down", ta=True, out_dtype=BF16)
        da = mm(dh_half, plain(pre + "_w_down"), name=f"{tag}_da", tb=True)
        dg, du = swiglu_bwd(g, u, da, name=f"{tag}_dact")
        gw_gate = mm(n, dg, name=f"{tag}_dw_gate", ta=True, out_dtype=BF16, out_stack=True)
        gw_up = mm(n, du, name=f"{tag}_dw_up", ta=True, out_dtype=BF16, out_stack=True)
        dn = mm(dg, stacked(pre + "_w_gate"), name=f"{tag}_dn_gate", tb=True, b_stack=True)
        dn = mm(du, stacked(pre + "_w_up"), name=f"{tag}_dn_up", tb=True, b_stack=True, res=dn)
        outs = norm_bwd(h, norm_g, dn, name=f"{tag}_dnorm", res=dh, bf16_alpha=None if last else 1.0)
        return outs, gw_gate, gw_up, gw_down

    h1, ffn1_saved = ffn_fwd(xs, ffn1_norm, "ffn1", "ffn1")
    u_mix = norm_fwd(h1, mix_norm, name="mix_norm")
    z = mm(u_mix, win_l, name="mix_in")
    bias = na_bias(na_rpb[0])
    o_a = na_fwd(z, bias, NH, S)
    c_q = norm_fwd((z, QR, 3 * NAW // QR), q_a_norm, name="q_a_norm")
    c_kv = norm_fwd((z, KVR, (3 * NAW + QR) // KVR), kv_a_norm, name="kv_a_norm")
    q_n = mm(c_q, wuq_n, name="mla_q_nope", out_dtype=BF16)
    q_r = rope(mm(c_q, wuq_r, name="mla_q_rope"), cos_t, sin_t, name="rope_q", out_dtype=BF16)
    k_n = mm(c_kv, wuk, name="mla_k_nope", out_dtype=BF16)
    v_m = mm(c_kv, wuv, name="mla_v", out_dtype=BF16)
    k_r = rope((z, LANES, off_kr // LANES), cos_t, sin_t, name="rope_k", out_dtype=BF16)
    o_b, lse = mla_fwd(q_n, q_r, k_n, v_m, k_r, MH, S)
    y_a = mm(o_a, stacked("w_branch_a"), name="branch_a", b_stack=True)
    y_b = mm(o_b, stacked("w_branch_b"), name="branch_b", b_stack=True)
    z_ga, z_gb = (z, D, off_ga // D), (z, D, off_ga // D + 1)
    merged = rowwise(lambda ga, gb, ya, yb: _sig(ga) * ya + _sig(gb) * yb, [z_ga, z_gb, y_a, y_b], [], [(D, BF16)],
                     tm=256, name="merge")[0]
    h2 = mm(merged, plain("w_out"), name="mix_out", res=h1)
    h3, ffn2_saved = ffn_fwd(h2, ffn2_norm, "ffn2", "ffn2")
    n4 = norm_fwd(h3, pl_norm, name="pl_norm")
    pg_pre = mm(n4, plain("w_pl_gate"), name="pl_gate")
    pe = mm(ps, stacked("w_pl"), name="pl_embed", b_stack=True)

    def tail(h3b, pgb, peb, tb_, fg):
        pg = _sig(pgb)
        h4 = h3b + pg * peb
        r = _rstd(h4)
        xh = h4 * r
        err = xh * fg - tb_
        loss_rows = jnp.mean(err * err, axis=-1, keepdims=True)
        dy = err * (1.0 / D)
        dxh = dy * fg
        dh4 = r * (dxh - xh * jnp.mean(dxh * xh, axis=-1, keepdims=True))
        loss_part = jnp.broadcast_to(0.5 * jnp.sum(loss_rows, axis=0, keepdims=True), (1, LANES))
        return (dh4, dh4 * peb * pg * (1.0 - pg), dh4 * pg, loss_part, jnp.sum(dy * xh, axis=0, keepdims=True))

    dh4, dpg_pre, dpe, loss_part, g_final = rowwise(
        tail, [h3, pg_pre, pe, tgt], [final_norm.reshape(1, D)], [(D, F32), (D, BF16), (D, BF16)],
        accs=[(1, LANES), (1, D)], tm=128, name="loss_tail")
    loss = lax.psum(loss_part[0, 0], ("x", "y", "c"))

    G = {}
    G["w_pl"] = mm(ps, dpe, name="pl_dw_embed", ta=True, out_dtype=BF16, out_stack=True)
    G["w_pl_gate"] = mm(n4, dpg_pre, name="pl_dw_gate", ta=True, out_dtype=BF16)
    dn4 = mm(dpg_pre, plain("w_pl_gate"), name="pl_dn", tb=True)
    dh3, dh3_half, g_pl = norm_bwd(h3, pl_norm, dn4, name="pl_dnorm", res=dh4, bf16_alpha=0.5)
    (dh2, dh2_b, g_ffn2), G["ffn2_w_gate"], G["ffn2_w_up"], G["ffn2_w_down"] = ffn_bwd(
        h2, ffn2_norm, ffn2_saved, dh3, dh3_half, "ffn2", "ffn2", last=False)

    G["w_out"] = mm(merged, dh2_b, name="mix_dw_out", ta=True, out_dtype=BF16)
    dmerged = mm(dh2_b, plain("w_out"), name="mix_dmerged", tb=True)

    def merge_bwd(ga, gb, ya, yb, dm):
        sa, sb = _sig(ga), _sig(gb)
        return dm * sa, dm * sb, dm * ya * sa * (1.0 - sa), dm * yb * sb * (1.0 - sb)

    dy_a, dy_b, dga, dgb = rowwise(merge_bwd, [z_ga, z_gb, y_a, y_b, dmerged], [], [(D, BF16)] * 4, tm=256,
                                   name="merge_bwd")
    G["w_branch_a"] = mm(o_a, dy_a, name="branch_a_dw", ta=True, out_dtype=BF16, out_stack=True)
    G["w_branch_b"] = mm(o_b, dy_b, name="branch_b_dw", ta=True, out_dtype=BF16, out_stack=True)
    do_a = mm(dy_a, stacked("w_branch_a"), name="branch_a_dx", tb=True, b_stack=True)
    do_b = mm(dy_b, stacked("w_branch_b"), name="branch_b_dx", tb=True, b_stack=True)
    dq_na, dk_na, dv_na, dbias = na_bwd(z, bias, do_a, NH, S)
    g_rpb = na_rpb_grad(dbias)
    dq_n, dq_rr, dk_n, dv_m, dk_rr = mla_bwd(q_n, q_r, k_n, v_m, k_r, lse, do_b, MH, S)
    dq_r = rope(dq_rr, cos_t, -sin_t, name="rope_q_bwd", out_dtype=BF16)
    dk_r = rope(dk_rr, cos_t, -sin_t, name="rope_k_bwd", out_dtype=BF16)
    gw_uq_n = mm(c_q, dq_n, name="mla_dw_q_nope", ta=True, out_dtype=BF16)
    gw_uq_r = mm(c_q, dq_r, name="mla_dw_q_rope", ta=True, out_dtype=BF16)
    dc_q = mm(dq_n, wuq_n, name="mla_dcq_nope", tb=True)
    dc_q = mm(dq_r, wuq_r, name="mla_dcq_rope", tb=True, res=dc_q)
    gw_uk = mm(c_kv, dk_n, name="mla_dw_k", ta=True, out_dtype=BF16)
    gw_uv = mm(c_kv, dv_m, name="mla_dw_v", ta=True, out_dtype=BF16)
    dc_kv = mm(dk_n, wuk, name="mla_dckv_k", tb=True)
    dc_kv = mm(dv_m, wuv, name="mla_dckv_v", tb=True, res=dc_kv)
    dq_lat, g_qa = norm_bwd((z, QR, 3 * NAW // QR), q_a_norm, dc_q, name="q_a_dnorm", want_f32=False, bf16_alpha=1.0)
    dkv_lat, g_kva = norm_bwd((z, KVR, (3 * NAW + QR) // KVR), kv_a_norm, dc_kv, name="kv_a_dnorm", want_f32=False,
                              bf16_alpha=1.0)
    dz = jnp.concatenate([dq_na, dk_na.astype(BF16), dv_na.astype(BF16), dq_lat, dkv_lat, dga, dgb, dk_r], axis=1)
    gw_in_l = mm(u_mix, dz, name="mix_dw_in", ta=True, out_dtype=BF16)
    du_mix = mm(dz, win_l, name="mix_du", tb=True)
    dh1, dh1_half, g_mix = norm_bwd(h1, mix_norm, du_mix, name="mix_dnorm", res=dh2, bf16_alpha=0.5)
    (grad_x, g_ffn1), G["ffn1_w_gate"], G["ffn1_w_up"], G["ffn1_w_down"] = ffn_bwd(
        xs, ffn1_norm, ffn1_saved, dh1, dh1_half, "ffn1", "ffn1", last=True)

    def to_stack(g2d):
        k, n = g2d.shape
        return g2d.reshape(k, N_CHIPS, n // N_CHIPS).transpose(1, 0, 2)

    gw_in = jnp.concatenate([gw_in_l[:, :n_front], gw_in_l[:, off_kr:off_kr + MLA_ROPE], gw_in_l[:, n_front:off_kr]],
                            axis=1)
    G["w_in"] = to_stack(gw_in)
    gw_uq = jnp.concatenate([gw_uq_n.reshape(QR, MH, MLA_NOPE), gw_uq_r.reshape(QR, MH, LANES)[:, :, :MLA_ROPE]],
                            axis=2).reshape(QR, MH * MLA_QK)
    G["w_uq"] = to_stack(gw_uq)
    gw_ukv = jnp.stack([gw_uk.reshape(KVR, MH, HEAD_DIM), gw_uv.reshape(KVR, MH, HEAD_DIM)], axis=2)
    G["w_ukv"] = to_stack(gw_ukv.reshape(KVR, MH * 2 * HEAD_DIM))

    def four(n):
        g = G[n]
        if g.ndim == 2:
            return g.reshape(N_CHIPS, 2, g.shape[0] // (2 * N_CHIPS), g.shape[1])
        return g.reshape(N_CHIPS, 2, g.shape[1] // 2, g.shape[2])

    g4 = [four(n) for n in big]
    from_sibling = pair_exchange(g4)
    sums = [pair_sum(a, b, c_idx) for a, b in zip(g4, from_sibling)]
    from_chips = chip_scatter(sums)
    halves = [chip_sum(a, b, j_idx) for a, b in zip(sums, from_chips)]
    sibling_halves = dict(zip(big, half_exchange(halves)))
    own_halves = dict(zip(big, halves))
    grads = {}

    small_g = {"ffn1_norm": g_ffn1, "mix_norm": g_mix, "q_a_norm": g_qa, "kv_a_norm": g_kva, "na_rpb": g_rpb,
               "ffn2_norm": g_ffn2, "pl_norm": g_pl, "final_norm": g_final}
    sizes = [int(np.prod(W[n].shape)) for n in small]
    total = sum(sizes)
    padded = -(-total // (8 * LANES)) * (8 * LANES)

    def pack(parts):
        flat = jnp.concatenate([jnp.reshape(parts[n], (-1,)).astype(F32) for n in small]
                               + [jnp.zeros((padded - total,), F32)])
        return flat.reshape(padded // LANES, LANES)

    def unpack(a):
        flat, out, o = a.reshape(-1), {}, 0
        for n, sz in zip(small, sizes):
            out[n] = flat[o:o + sz].reshape(W[n].shape)
            o += sz
        return out

    g_small = sum_devices(gather_small(pack(small_g)))
    d_small, m_small, v_small = adamw(pack(W), g_small, pack(Mo), pack(Vo), name="adamw_small")
    grads.update(unpack(g_small))
    delta, new_m, new_v = unpack(d_small), unpack(m_small), unpack(v_small)

    for n in big:
        shp = W[n].shape
        two_d = lambda a: a.reshape(shp[1], shp[2])
        g_, d_, m_, v_ = adamw_shard(two_d(W[n]), own_halves[n], sibling_halves[n], two_d(Mo[n]), two_d(Vo[n]),
                                     c_idx, name="adamw_" + n)
        grads[n], delta[n], new_m[n], new_v[n] = g_.reshape(shp), d_.reshape(shp), m_.reshape(shp), v_.reshape(shp)

    return (loss, grad_x[None], *[grads[n] for n in order], *[delta[n] for n in order],
            *[new_m[n] for n in order], *[new_v[n] for n in order])
```

```python
import functools

import numpy as np
import jax
import jax.numpy as jnp
from jax import lax
from jax.experimental import pallas as pl
from jax.experimental.pallas import tpu as pltpu

F32 = jnp.float32
BF16 = jnp.bfloat16

VMEM_LIMIT_V7X = 56 * 1024 * 1024
VMEM_BUDGET_V7X = 40 * 1024 * 1024
LANES = 128

GRID_W = 64
NA_WIN_ROWS = 8
NA_WIN_COLS = 16
HEAD_DIM = 128
MLA_NOPE = 128
MLA_ROPE = 64
MLA_QK = MLA_NOPE + MLA_ROPE
ROPE_THETA = 10000.0
NORM_EPS = 1e-6
NEG_INF = -1e30
N_CHIPS = 4

ADAM_LR = 0.001
ADAM_B1 = 0.9
ADAM_B2 = 0.999
ADAM_EPS = 1e-08
ADAM_WD = 0.01
ADAM_STEP = 10

MESH = pl.DeviceIdType.MESH
ANY = pl.BlockSpec(memory_space=pl.ANY)


def _params(sem=None):
    return pltpu.CompilerParams(dimension_semantics=sem, vmem_limit_bytes=VMEM_LIMIT_V7X)


def _pick(n, target, align):
    best = None
    t = align
    while t <= min(n, target):
        if n % t == 0:
            best = t
        t += align
    return n if best is None else best


def mm(a, b, *, name, ta=False, tb=False, out_dtype=F32, res=None, alpha=1.0, b_stack=False, out_stack=False,
       exact=False):
    K, M = (a.shape if ta else a.shape[::-1])
    if b_stack:
        nst = b.shape[0]
        if tb:
            N, kb = b.shape[1], b.shape[2]
            Kb, nb = nst * kb, None
        else:
            Kb, nb = b.shape[1], b.shape[2]
            N = nst * nb
    else:
        N, Kb = (b.shape if tb else b.shape[::-1])
    assert K == Kb, (a.shape, b.shape, ta, tb)
    if out_stack:
        assert N % N_CHIPS == 0
    n_unit = N // N_CHIPS if out_stack else (nb if (b_stack and not tb) else N)
    k_unit = kb if (b_stack and tb) else K
    tn = _pick(n_unit, 512, LANES) if n_unit % 512 == 0 or n_unit <= 512 else _pick(n_unit, 1536, LANES)
    tk = _pick(k_unit, 2048, LANES)
    tm = _pick(M, 1024, LANES if ta else 16)
    isz = lambda t: jnp.dtype(t.dtype).itemsize
    osz = jnp.dtype(out_dtype).itemsize

    def vmem(tm_):
        return (2 * tm_ * tk * isz(a) + 2 * tk * tn * isz(b) + 2 * tm_ * tn * osz + tm_ * tn * 4
                + (2 * tm_ * tn * isz(res) if res is not None else 0))

    while vmem(tm) > VMEM_BUDGET_V7X and tm % 2 == 0 and (tm // 2) % (LANES if ta else 16) == 0:
        tm //= 2
    nk = K // tk
    gm, gn = M // tm, N // tn

    a_spec = pl.BlockSpec((tk, tm), lambda i, j, k: (k, i)) if ta else pl.BlockSpec((tm, tk), lambda i, j, k: (i, k))
    if b_stack and not tb:
        q = nb // tn
        b_spec = pl.BlockSpec((None, tk, tn), lambda i, j, k: (j // q, k, j % q))
    elif b_stack and tb:
        q = kb // tk
        b_spec = pl.BlockSpec((None, tn, tk), lambda i, j, k: (k // q, j, k % q))
    elif tb:
        b_spec = pl.BlockSpec((tn, tk), lambda i, j, k: (j, k))
    else:
        b_spec = pl.BlockSpec((tk, tn), lambda i, j, k: (k, j))
    if out_stack:
        qo = (N // N_CHIPS) // tn
        o_spec = pl.BlockSpec((None, tm, tn), lambda i, j, k: (j // qo, i, j % qo))
        o_shape = jax.ShapeDtypeStruct((N_CHIPS, M, N // N_CHIPS), out_dtype)
    else:
        o_spec = pl.BlockSpec((tm, tn), lambda i, j, k: (i, j))
        o_shape = jax.ShapeDtypeStruct((M, N), out_dtype)
    dims = (((0 if ta else 1,), (1 if tb else 0,)), ((), ()))
    has_res = res is not None

    def body(*refs):
        if has_res:
            a_ref, b_ref, r_ref, o_ref, acc_ref = refs
        else:
            a_ref, b_ref, o_ref, acc_ref = refs
            r_ref = None
        k = pl.program_id(2)
        if exact:
            part = lax.dot_general(a_ref[...], b_ref[...], dims, preferred_element_type=F32,
                                   precision=lax.Precision.HIGHEST)
        else:
            part = lax.dot_general(a_ref[...].astype(BF16), b_ref[...].astype(BF16), dims,
                                   preferred_element_type=F32)

        def finish(total):
            if alpha != 1.0:
                total = total * alpha
            if has_res:
                total = total + r_ref[...].astype(F32)
            o_ref[...] = total.astype(out_dtype)

        if nk == 1:
            finish(part)
        else:
            @pl.when(k == 0)
            def _():
                acc_ref[...] = part

            @pl.when(jnp.logical_and(k > 0, k < nk - 1))
            def _():
                acc_ref[...] += part

            @pl.when(k == nk - 1)
            def _():
                finish(acc_ref[...] + part)

    in_specs = [a_spec, b_spec]
    args = [a, b]
    if has_res:
        in_specs.append(pl.BlockSpec((tm, tn), lambda i, j, k: (i, j)))
        args.append(res)
    return pl.pallas_call(
        body, name=name, grid=(gm, gn, nk), in_specs=in_specs, out_specs=o_spec, out_shape=o_shape,
        scratch_shapes=[pltpu.VMEM((tm, tn) if nk > 1 else (8, LANES), F32)],
        compiler_params=_params(("parallel", "parallel", "arbitrary")),
    )(*args)


def rowwise(fn, rows, consts, outs, accs=(), *, tm, name, tn=None):
    rows = [r if isinstance(r, tuple) else (r, r.shape[1], 0) for r in rows]
    S = rows[0][0].shape[0]
    tm = _pick(S, tm, 16)
    nrow, ncon, nout = len(rows), len(consts), len(outs)
    if tn is None:
        grid = (S // tm,)
        in_specs = [pl.BlockSpec((tm, w), functools.partial(lambda i, cb: (i, cb), cb=cb)) for _, w, cb in rows]
        in_specs += [pl.BlockSpec(c.shape, lambda i: (0, 0)) for c in consts]
        out_specs = [pl.BlockSpec((tm, n), lambda i: (i, 0)) for n, _ in outs]
        out_specs += [pl.BlockSpec(s, lambda i: (0, 0)) for s in accs]
        sem = ("arbitrary",)
    else:
        assert not accs
        N = rows[0][1]
        grid = (S // tm, N // tn)
        in_specs = [pl.BlockSpec((tm, tn), lambda i, j: (i, j)) for _ in rows]
        in_specs += [pl.BlockSpec(c.shape, lambda i, j: (0, 0)) for c in consts]
        out_specs = [pl.BlockSpec((tm, tn), lambda i, j: (i, j)) for _ in outs]
        sem = ("parallel", "parallel")
    out_shape = [jax.ShapeDtypeStruct((S, n), dt) for n, dt in outs]
    out_shape += [jax.ShapeDtypeStruct(s, F32) for s in accs]

    def body(*refs):
        vals = fn(*[r[...] for r in refs[:nrow + ncon]])
        if not isinstance(vals, (tuple, list)):
            vals = (vals,)
        o_refs = refs[nrow + ncon:]
        for o_ref, v in zip(o_refs[:nout], vals[:nout]):
            o_ref[...] = v.astype(o_ref.dtype)
        if accs:
            first = pl.program_id(0) == 0

            def accumulate(a_ref, v):
                @pl.when(first)
                def _():
                    a_ref[...] = v

                @pl.when(jnp.logical_not(first))
                def _():
                    a_ref[...] += v

            for a_ref, v in zip(o_refs[nout:], vals[nout:]):
                accumulate(a_ref, v.astype(F32))

    res = pl.pallas_call(
        body, name=name, grid=grid, in_specs=in_specs, out_specs=out_specs, out_shape=out_shape,
        compiler_params=_params(sem),
    )(*[r[0] for r in rows], *consts)
    return res


def _rstd(x):
    return lax.rsqrt(jnp.mean(x * x, axis=-1, keepdims=True) + NORM_EPS)


def norm_fwd(x, g, *, name, tm=256):
    w = x[1] if isinstance(x, tuple) else x.shape[1]

    def fn(xb, gb):
        return (xb * _rstd(xb)) * gb

    return rowwise(fn, [x], [g], [(w, BF16)], tm=tm, name=name)[0]


def norm_bwd(x, g, dn, *, name, res=None, want_f32=True, bf16_alpha=None, tm=256):
    w = x[1] if isinstance(x, tuple) else x.shape[1]
    has_res = res is not None

    def fn(*blocks):
        if has_res:
            xb, dnb, rb, gb = blocks
        else:
            xb, dnb, gb = blocks
        r = _rstd(xb)
        xh = xb * r
        dxh = dnb * gb
        dx = r * (dxh - xh * jnp.mean(dxh * xh, axis=-1, keepdims=True))
        if has_res:
            dx = dx + rb
        out = []
        if want_f32:
            out.append(dx)
        if bf16_alpha is not None:
            out.append(dx * bf16_alpha if bf16_alpha != 1.0 else dx)
        out.append(jnp.sum(dnb * xh, axis=0, keepdims=True))
        return tuple(out)

    outs = ([(w, F32)] if want_f32 else []) + ([(w, BF16)] if bf16_alpha is not None else [])
    rows = [x, dn] + ([res] if has_res else [])
    return rowwise(fn, rows, [g], outs, accs=[(1, w)], tm=tm, name=name)


def _sig(x):
    return jax.nn.sigmoid(x)


def swiglu_fwd(g, u, *, name):
    return rowwise(lambda gb, ub: gb * _sig(gb) * ub, [g, u], [], [(g.shape[1], BF16)], tm=256, name=name,
                   tn=_pick(g.shape[1], 1536, LANES))[0]


def swiglu_bwd(g, u, da, *, name):
    def fn(gb, ub, dab):
        s = _sig(gb)
        return dab * ub * (s + gb * s * (1.0 - s)), dab * (gb * s)

    n = g.shape[1]
    return rowwise(fn, [g, u, da], [], [(n, BF16), (n, BF16)], tm=256, name=name, tn=_pick(n, 1536, LANES))


def rope(x, cos, sin_signed, *, name, out_dtype):
    w = x[1] if isinstance(x, tuple) else x.shape[1]
    half = MLA_ROPE // 2

    def fn(xb, cb, sb):
        lane = lax.broadcasted_iota(jnp.int32, cb.shape, 1)
        outs = []
        for hb in range(w // LANES):
            blk = xb[:, hb * LANES:(hb + 1) * LANES]
            partner = jnp.where(lane < half, pltpu.roll(blk, LANES - half, 1), pltpu.roll(blk, half, 1))
            outs.append(blk * cb + partner * sb)
        return outs[0] if len(outs) == 1 else jnp.concatenate(outs, axis=1)

    return rowwise(fn, [x, cos, sin_signed], [], [(w, out_dtype)], tm=256, name=name)[0]


def _na_tables():
    cols = np.arange(GRID_W)
    kw = NA_WIN_COLS
    dc = np.clip(cols[None, :] - cols[:, None], -(kw - 1), kw - 1) + (kw - 1)
    onehot = np.zeros((LANES, GRID_W * GRID_W), np.float32)
    onehot[dc.reshape(-1), np.arange(GRID_W * GRID_W)] = 1.0
    col_start = np.clip(cols - kw // 2, 0, GRID_W - kw)
    mask = (cols[None, :] >= col_start[:, None]) & (cols[None, :] < col_start[:, None] + kw)
    return onehot, np.where(mask, 0.0, NEG_INF).astype(np.float32)


def na_bias(rpb):
    H = rpb.shape[0]
    nr, kh = 2 * NA_WIN_ROWS - 1, NA_WIN_ROWS
    onehot, maskb = _na_tables()
    rp = jnp.pad(rpb.reshape(H * nr, 2 * NA_WIN_COLS - 1), ((0, 0), (0, LANES - (2 * NA_WIN_COLS - 1))))
    t1 = mm(rp, jnp.asarray(onehot), name="na_bias_table", exact=True).reshape(H, nr, GRID_W, GRID_W)
    t1 = t1 + jnp.asarray(maskb)[None, None]
    per_t = [jnp.stack([t1[:, i - t + kh - 1] for i in range(kh)], axis=2) for t in range(kh)]
    return jnp.stack(per_t, axis=1).reshape(H, kh, GRID_W, kh * GRID_W)


def na_rpb_grad(db):
    H = db.shape[0]
    nr, kh = 2 * NA_WIN_ROWS - 1, NA_WIN_ROWS
    onehot, _ = _na_tables()
    db = db.reshape(H, kh, GRID_W, kh, GRID_W)
    per_dr = []
    for dri in range(nr):
        terms = [db[:, t, :, dri - (kh - 1) + t, :] for t in range(kh) if 0 <= dri - (kh - 1) + t < kh]
        per_dr.append(functools.reduce(jnp.add, terms))
    dt1 = jnp.stack(per_dr, axis=1).reshape(H * nr, GRID_W * GRID_W)
    g = mm(dt1, jnp.asarray(onehot), name="na_rpb_grad", tb=True, exact=True)
    return g[:, :2 * NA_WIN_COLS - 1].reshape(H, nr, 2 * NA_WIN_COLS - 1)


def _na_first_row(r, rows):
    return jnp.clip(r - NA_WIN_ROWS // 2, 0, rows - NA_WIN_ROWS)


def _na_scores(q_ref, k_ref, b_ref, start):
    q = q_ref[...].astype(BF16)
    k = k_ref[pl.ds(start, NA_WIN_ROWS * GRID_W), :].astype(BF16)
    s = lax.dot_general(q, k, (((1,), (1,)), ((), ())), preferred_element_type=F32)
    s = s * (HEAD_DIM ** -0.5) + b_ref[...]
    m = jnp.max(s, axis=-1, keepdims=True)
    e = jnp.exp(s - m)
    return q, k, e / jnp.sum(e, axis=-1, keepdims=True)


def na_fwd(z, bias, H, S):
    rows = S // GRID_W
    nkeys = NA_WIN_ROWS * GRID_W

    def body(q_ref, k_ref, v_ref, b_ref, o_ref):
        r = pl.program_id(1)
        start = pl.multiple_of(_na_first_row(r, rows) * GRID_W, GRID_W)
        _, _, p = _na_scores(q_ref, k_ref, b_ref, start)
        v = v_ref[pl.ds(start, nkeys), :].astype(BF16)
        o_ref[...] = jnp.dot(p.astype(BF16), v, preferred_element_type=F32).astype(o_ref.dtype)

    return pl.pallas_call(
        body, name="na_fwd", grid=(H, rows),
        in_specs=[pl.BlockSpec((GRID_W, HEAD_DIM), lambda h, r: (r, h)),
                  pl.BlockSpec((S, HEAD_DIM), lambda h, r: (0, H + h)),
                  pl.BlockSpec((S, HEAD_DIM), lambda h, r: (0, 2 * H + h)),
                  pl.BlockSpec((None, None, GRID_W, nkeys), lambda h, r: (h, r - _na_first_row(r, rows), 0, 0))],
        out_specs=pl.BlockSpec((GRID_W, HEAD_DIM), lambda h, r: (r, h)),
        out_shape=jax.ShapeDtypeStruct((S, H * HEAD_DIM), BF16),
        compiler_params=_params(("parallel", "arbitrary")),
    )(z, z, z, bias)


def na_bwd(z, bias, do, H, S):
    rows = S // GRID_W
    nkeys = NA_WIN_ROWS * GRID_W
    tn_dims = (((0,), (0,)), ((), ()))

    def body(q_ref, k_ref, v_ref, b_ref, do_ref, dq_ref, dk_ref, dv_ref, db_ref):
        r = pl.program_id(1)
        start = pl.multiple_of(_na_first_row(r, rows) * GRID_W, GRID_W)
        q, k, p = _na_scores(q_ref, k_ref, b_ref, start)
        v = v_ref[pl.ds(start, nkeys), :].astype(BF16)
        dob = do_ref[...].astype(BF16)
        dp = lax.dot_general(dob, v, (((1,), (1,)), ((), ())), preferred_element_type=F32)
        ds = p * (dp - jnp.sum(dp * p, axis=-1, keepdims=True))
        dsb = (ds * (HEAD_DIM ** -0.5)).astype(BF16)
        dq_ref[...] = jnp.dot(dsb, k, preferred_element_type=F32).astype(dq_ref.dtype)

        @pl.when(r == 0)
        def _():
            dk_ref[...] = jnp.zeros_like(dk_ref)
            dv_ref[...] = jnp.zeros_like(dv_ref)

        dk_ref[pl.ds(start, nkeys), :] += lax.dot_general(dsb, q, tn_dims, preferred_element_type=F32)
        dv_ref[pl.ds(start, nkeys), :] += lax.dot_general(p.astype(BF16), dob, tn_dims, preferred_element_type=F32)

        fresh = jnp.logical_or(r <= NA_WIN_ROWS // 2, r > rows - NA_WIN_ROWS // 2)

        @pl.when(fresh)
        def _():
            db_ref[...] = ds

        @pl.when(jnp.logical_not(fresh))
        def _():
            db_ref[...] += ds

    W = H * HEAD_DIM
    return pl.pallas_call(
        body, name="na_bwd", grid=(H, rows),
        in_specs=[pl.BlockSpec((GRID_W, HEAD_DIM), lambda h, r: (r, h)),
                  pl.BlockSpec((S, HEAD_DIM), lambda h, r: (0, H + h)),
                  pl.BlockSpec((S, HEAD_DIM), lambda h, r: (0, 2 * H + h)),
                  pl.BlockSpec((None, None, GRID_W, nkeys), lambda h, r: (h, r - _na_first_row(r, rows), 0, 0)),
                  pl.BlockSpec((GRID_W, HEAD_DIM), lambda h, r: (r, h))],
        out_specs=[pl.BlockSpec((GRID_W, HEAD_DIM), lambda h, r: (r, h)),
                   pl.BlockSpec((S, HEAD_DIM), lambda h, r: (0, h)),
                   pl.BlockSpec((S, HEAD_DIM), lambda h, r: (0, h)),
                   pl.BlockSpec((None, None, GRID_W, nkeys), lambda h, r: (h, r - _na_first_row(r, rows), 0, 0))],
        out_shape=[jax.ShapeDtypeStruct((S, W), BF16), jax.ShapeDtypeStruct((S, W), F32),
                   jax.ShapeDtypeStruct((S, W), F32), jax.ShapeDtypeStruct((H, NA_WIN_ROWS, GRID_W, nkeys), F32)],
        compiler_params=_params(("arbitrary", "arbitrary")),
    )(z, z, z, bias, do)


def _mla_scores(qn_ref, qr_ref, kn_ref, kr_ref):
    nt = (((1,), (1,)), ((), ()))
    s = lax.dot_general(qn_ref[...], kn_ref[...], nt, preferred_element_type=F32)
    s = s + lax.dot_general(qr_ref[...], kr_ref[...], nt, preferred_element_type=F32)
    return s * (MLA_QK ** -0.5)


def mla_fwd(qn, qr, kn, v, kr, H, S):
    tq = _pick(S, 256, 16)

    def body(qn_ref, qr_ref, kn_ref, v_ref, kr_ref, o_ref, lse_ref):
        s = _mla_scores(qn_ref, qr_ref, kn_ref, kr_ref)
        m = jnp.max(s, axis=-1, keepdims=True)
        e = jnp.exp(s - m)
        l = jnp.sum(e, axis=-1, keepdims=True)
        o_ref[...] = jnp.dot((e / l).astype(BF16), v_ref[...], preferred_element_type=F32).astype(o_ref.dtype)
        lse_ref[...] = jnp.broadcast_to(m + jnp.log(l), lse_ref.shape)

    qspec = pl.BlockSpec((tq, HEAD_DIM), lambda h, i: (i, h))
    kspec = pl.BlockSpec((S, HEAD_DIM), lambda h, i: (0, h))
    return pl.pallas_call(
        body, name="mla_fwd", grid=(H, S // tq),
        in_specs=[qspec, qspec, kspec, kspec, pl.BlockSpec((S, LANES), lambda h, i: (0, 0))],
        out_specs=[qspec, qspec],
        out_shape=[jax.ShapeDtypeStruct((S, H * HEAD_DIM), BF16), jax.ShapeDtypeStruct((S, H * LANES), F32)],
        compiler_params=_params(("parallel", "arbitrary")),
    )(qn, qr, kn, v, kr)


def mla_bwd(qn, qr, kn, v, kr, lse, do, H, S):
    tq = _pick(S, 256, 16)
    nt = (((1,), (1,)), ((), ()))
    tn_dims = (((0,), (0,)), ((), ()))

    def body(qn_ref, qr_ref, kn_ref, v_ref, kr_ref, lse_ref, do_ref, dqn_ref, dqr_ref, dkn_ref, dv_ref, dkr_ref):
        h, i = pl.program_id(0), pl.program_id(1)
        s = _mla_scores(qn_ref, qr_ref, kn_ref, kr_ref)
        p = jnp.exp(s - lse_ref[:, 0:1])
        dob = do_ref[...].astype(BF16)
        dp = lax.dot_general(dob, v_ref[...], nt, preferred_element_type=F32)
        ds = p * (dp - jnp.sum(dp * p, axis=-1, keepdims=True))
        dsb = (ds * (MLA_QK ** -0.5)).astype(BF16)
        dqn_ref[...] = jnp.dot(dsb, kn_ref[...], preferred_element_type=F32).astype(dqn_ref.dtype)
        dqr_ref[...] = jnp.dot(dsb, kr_ref[...], preferred_element_type=F32).astype(dqr_ref.dtype)

        @pl.when(i == 0)
        def _():
            dkn_ref[...] = jnp.zeros_like(dkn_ref)
            dv_ref[...] = jnp.zeros_like(dv_ref)

        @pl.when(jnp.logical_and(i == 0, h == 0))
        def _():
            dkr_ref[...] = jnp.zeros_like(dkr_ref)

        dkn_ref[...] += lax.dot_general(dsb, qn_ref[...], tn_dims, preferred_element_type=F32)
        dkr_ref[...] += lax.dot_general(dsb, qr_ref[...], tn_dims, preferred_element_type=F32)
        dv_ref[...] += lax.dot_general(p.astype(BF16), dob, tn_dims, preferred_element_type=F32)

    qspec = pl.BlockSpec((tq, HEAD_DIM), lambda h, i: (i, h))
    kspec = pl.BlockSpec((S, HEAD_DIM), lambda h, i: (0, h))
    rspec = pl.BlockSpec((S, LANES), lambda h, i: (0, 0))
    W = H * HEAD_DIM
    return pl.pallas_call(
        body, name="mla_bwd", grid=(H, S // tq),
        in_specs=[qspec, qspec, kspec, kspec, rspec, qspec, qspec],
        out_specs=[qspec, qspec, kspec, kspec, rspec],
        out_shape=[jax.ShapeDtypeStruct((S, W), BF16), jax.ShapeDtypeStruct((S, W), F32),
                   jax.ShapeDtypeStruct((S, W), F32), jax.ShapeDtypeStruct((S, W), F32),
                   jax.ShapeDtypeStruct((S, LANES), F32)],
        compiler_params=_params(("arbitrary", "arbitrary")),
    )(qn, qr, kn, v, kr, lse, do)


def _place():
    return lax.axis_index("x"), lax.axis_index("y"), lax.axis_index("c")


def _other_chips(x, y):
    return [(1 - x, y), (x, 1 - y), (1 - x, 1 - y)]


def _remote(src, dst, send_sem, recv_sem, to):
    return pltpu.make_async_remote_copy(src_ref=src, dst_ref=dst, send_sem=send_sem, recv_sem=recv_sem,
                                        device_id=to, device_id_type=MESH)


def gather_weights(shards):
    n = len(shards)

    def body(*refs):
        ins, outs = refs[:n], refs[n:2 * n]
        send, recv = refs[2 * n:]
        x, y, c = _place()
        me = 2 * x + y
        chips = _other_chips(x, y)
        sibling = (x, y, 1 - c)
        first = []
        for w in range(n):
            for k, (px, py) in enumerate(chips):
                cp = _remote(ins[w].at[c], outs[w].at[me, c], send.at[6 * w + k], recv.at[6 * w + k], (px, py, c))
                cp.start()
                first.append(cp)
        passed = []
        for w in range(n):
            for k, (px, py) in enumerate(chips):
                blk = outs[w].at[2 * px + py, c]
                _remote(blk, blk, send.at[6 * w + k], recv.at[6 * w + k], (px, py, c)).wait_recv()
                cp = _remote(blk, blk, send.at[6 * w + 3 + k], recv.at[6 * w + 3 + k], sibling)
                cp.start()
                passed.append(cp)
        for w in range(n):
            for k, (px, py) in enumerate(chips):
                blk = outs[w].at[2 * px + py, 1 - c]
                _remote(blk, blk, send.at[6 * w + 3 + k], recv.at[6 * w + 3 + k], sibling).wait_recv()
        for cp in first + passed:
            cp.wait_send()

    return pl.pallas_call(
        body, name="gather_weights", in_specs=[ANY] * n, out_specs=[ANY] * n,
        out_shape=[jax.ShapeDtypeStruct((N_CHIPS,) + s.shape, s.dtype) for s in shards],
        scratch_shapes=[pltpu.SemaphoreType.DMA((6 * n,)), pltpu.SemaphoreType.DMA((6 * n,))],
    )(*shards)


HBM = pl.BlockSpec(memory_space=pltpu.HBM)
SEM = pl.BlockSpec(memory_space=pltpu.SEMAPHORE)
EFFECT = pltpu.SideEffectType.DATAFLOW_SIDE_EFFECTING


def _in_hbm(a):
    return pltpu.with_memory_space_constraint(a, pltpu.HBM)


def gather_start(shards, landings, groups):
    n, ng = len(shards), len(groups)

    def body(*refs):
        ins, lands = refs[:n], refs[n:2 * n]
        sems = refs[2 * n:2 * n + 2 * ng]
        x, y, c = _place()
        me = 2 * x + y
        for g, members in enumerate(groups):
            for i, w in enumerate(members):
                for k, (px, py) in enumerate(_other_chips(x, y)):
                    _remote(ins[w].at[c], lands[w].at[me, c], sems[2 * g].at[3 * i + k], sems[2 * g + 1].at[3 * i + k],
                            (px, py, c)).start()

    sem_shapes = []
    for members in groups:
        sem_shapes += [pltpu.SemaphoreType.DMA((3 * len(members),))] * 2
    bufs = list(shards) + list(landings)
    outs = pl.pallas_call(
        body, name="gather_start",
        out_shape=tuple(sem_shapes) + tuple(pltpu.HBM(b.shape, b.dtype) for b in bufs),
        in_specs=[HBM] * (2 * n), out_specs=tuple([SEM] * (2 * ng) + [HBM] * (2 * n)),
        input_output_aliases={i: 2 * ng + i for i in range(2 * n)},
        compiler_params=pltpu.CompilerParams(has_side_effects=EFFECT),
    )(*[_in_hbm(b) for b in bufs])
    sems = [(outs[2 * g], outs[2 * g + 1]) for g in range(ng)]
    return sems, outs[2 * ng:2 * ng + n], outs[2 * ng + n:]


def gather_wait(sems, shards, landings, after, tag):
    n = len(shards)
    send, recv = sems

    def body(*refs):
        ins, lands = refs[:n], refs[n:2 * n]
        send_sem, recv_sem = refs[2 * n], refs[2 * n + 1]
        x, y, c = _place()
        me = 2 * x + y
        for w in range(n):
            for k, (px, py) in enumerate(_other_chips(x, y)):
                cp = _remote(ins[w].at[c], lands[w].at[2 * px + py, c], send_sem.at[3 * w + k], recv_sem.at[3 * w + k],
                             (px, py, c))
                cp.wait_send()
                cp.wait_recv()

    bufs = list(shards) + list(landings)
    outs = pl.pallas_call(
        body, name="gather_wait_" + tag, out_shape=tuple(pltpu.HBM(b.shape, b.dtype) for b in bufs),
        in_specs=[HBM] * (2 * n) + [SEM, SEM, ANY], out_specs=tuple([HBM] * (2 * n)),
        input_output_aliases={i: i for i in range(2 * n)},
        compiler_params=pltpu.CompilerParams(has_side_effects=EFFECT),
    )(*bufs, send, recv, after)
    return outs[n:]


def gather_forward(landings, tag):
    n = len(landings)

    def body(*refs):
        ins, outs = refs[:n], refs[n:2 * n]
        send, recv = refs[2 * n:]
        x, y, c = _place()
        sibling = (x, y, 1 - c)
        cps = []
        for w in range(n):
            for k, (px, py) in enumerate(_other_chips(x, y)):
                j = 2 * px + py
                cp = _remote(ins[w].at[j, c], outs[w].at[j, c], send.at[3 * w + k], recv.at[3 * w + k], sibling)
                cp.start()
                cps.append(cp)
        for w in range(n):
            for k, (px, py) in enumerate(_other_chips(x, y)):
                blk = outs[w].at[2 * px + py, 1 - c]
                _remote(blk, blk, send.at[3 * w + k], recv.at[3 * w + k], sibling).wait_recv()
        for cp in cps:
            cp.wait_send()

    return pl.pallas_call(
        body, name="gather_forward_" + tag, in_specs=[ANY] * n, out_specs=[ANY] * n,
        out_shape=[jax.ShapeDtypeStruct(a.shape, a.dtype) for a in landings],
        input_output_aliases={i: i for i in range(n)},
        scratch_shapes=[pltpu.SemaphoreType.DMA((3 * n,)), pltpu.SemaphoreType.DMA((3 * n,))],
    )(*landings)


def pair_exchange(grads):
    n = len(grads)

    def body(*refs):
        ins, outs = refs[:n], refs[n:2 * n]
        send, recv = refs[2 * n:]
        x, y, c = _place()
        cps = []
        for w in range(n):
            cp = _remote(ins[w].at[:, 1 - c], outs[w], send.at[w], recv.at[w], (x, y, 1 - c))
            cp.start()
            cps.append(cp)
        for cp in cps:
            cp.wait()

    return pl.pallas_call(
        body, name="grad_pair_exchange", in_specs=[ANY] * n, out_specs=[ANY] * n,
        out_shape=[jax.ShapeDtypeStruct((g.shape[0],) + g.shape[2:], g.dtype) for g in grads],
        scratch_shapes=[pltpu.SemaphoreType.DMA((n,)), pltpu.SemaphoreType.DMA((n,))],
    )(*grads)


def chip_scatter(sums):
    n = len(sums)

    def body(*refs):
        ins, outs = refs[:n], refs[n:2 * n]
        send, recv = refs[2 * n:]
        x, y, c = _place()
        cps = []
        for w in range(n):
            for k, (px, py) in enumerate(_other_chips(x, y)):
                cp = _remote(ins[w].at[2 * px + py], outs[w].at[k], send.at[3 * w + k], recv.at[3 * w + k], (px, py, c))
                cp.start()
                cps.append(cp)
        for cp in cps:
            cp.wait()

    return pl.pallas_call(
        body, name="grad_chip_scatter", in_specs=[ANY] * n, out_specs=[ANY] * n,
        out_shape=[jax.ShapeDtypeStruct((3,) + s.shape[1:], s.dtype) for s in sums],
        scratch_shapes=[pltpu.SemaphoreType.DMA((3 * n,)), pltpu.SemaphoreType.DMA((3 * n,))],
    )(*sums)


def half_exchange(halves):
    n = len(halves)

    def body(*refs):
        ins, outs = refs[:n], refs[n:2 * n]
        send, recv = refs[2 * n:]
        x, y, c = _place()
        cps = []
        for w in range(n):
            cp = _remote(ins[w], outs[w], send.at[w], recv.at[w], (x, y, 1 - c))
            cp.start()
            cps.append(cp)
        for cp in cps:
            cp.wait()

    return pl.pallas_call(
        body, name="grad_half_exchange", in_specs=[ANY] * n, out_specs=[ANY] * n,
        out_shape=[jax.ShapeDtypeStruct(h.shape, h.dtype) for h in halves],
        scratch_shapes=[pltpu.SemaphoreType.DMA((n,)), pltpu.SemaphoreType.DMA((n,))],
    )(*halves)


def gather_small(v):
    def body(v_ref, o_ref, send, recv, local):
        x, y, c = _place()
        me = 4 * x + 2 * y + c
        own = pltpu.make_async_copy(v_ref, o_ref.at[me], local)
        own.start()
        cps = []
        for k in range(1, 8):
            fx, fy, fc = (k >> 2) & 1, (k >> 1) & 1, k & 1
            to = (x ^ fx if fx else x, y ^ fy if fy else y, c ^ fc if fc else c)
            cp = _remote(v_ref, o_ref.at[me], send.at[k - 1], recv.at[k - 1], to)
            cp.start()
            cps.append(cp)
        for k in range(1, 8):
            fx, fy, fc = (k >> 2) & 1, (k >> 1) & 1, k & 1
            px, py, pc = (x ^ fx if fx else x, y ^ fy if fy else y, c ^ fc if fc else c)
            cps[k - 1].wait_send()
            _remote(v_ref, o_ref.at[4 * px + 2 * py + pc], send.at[k - 1], recv.at[k - 1], (px, py, pc)).wait_recv()
        own.wait()

    return pl.pallas_call(
        body, name="gather_small_grads", in_specs=[ANY], out_specs=ANY,
        out_shape=jax.ShapeDtypeStruct((8,) + v.shape, v.dtype),
        scratch_shapes=[pltpu.SemaphoreType.DMA((7,)), pltpu.SemaphoreType.DMA((7,)), pltpu.SemaphoreType.DMA],
    )(v)


def _row_tile(rows, cols, nbuf_bytes):
    tm = _pick(rows, 512, 16)
    while tm * cols * nbuf_bytes * 2 > VMEM_BUDGET_V7X and tm % 32 == 0:
        tm //= 2
    return tm


def pair_sum(g, r, c_idx):
    _, _, rows, cols = g.shape
    tm = _row_tile(rows, cols, 2 + 2 + 2)
    nb = rows // tm

    def body(c_ref, g_ref, r_ref, o_ref):
        o_ref[...] = (g_ref[...].astype(F32) + r_ref[...].astype(F32)).astype(o_ref.dtype)

    gs = pltpu.PrefetchScalarGridSpec(
        num_scalar_prefetch=1, grid=(N_CHIPS, nb),
        in_specs=[pl.BlockSpec((None, None, tm, cols), lambda j, i, c_ref: (j, c_ref[0], i, 0)),
                  pl.BlockSpec((None, tm, cols), lambda j, i, c_ref: (j, i, 0))],
        out_specs=pl.BlockSpec((None, tm, cols), lambda j, i, c_ref: (j, i, 0)))
    return pl.pallas_call(body, name="grad_pair_sum", grid_spec=gs,
                          out_shape=jax.ShapeDtypeStruct(r.shape, BF16),
                          compiler_params=_params(("arbitrary", "arbitrary")))(c_idx, g, r)


def chip_sum(s, r, j_idx):
    _, rows, cols = s.shape
    tm = _row_tile(rows, cols, 2 + 3 * 2 + 4)
    nb = rows // tm

    def body(j_ref, s_ref, r_ref, o_ref):
        t = s_ref[...].astype(F32)
        for k in range(3):
            t = t + r_ref[k].astype(F32)
        o_ref[...] = t

    gs = pltpu.PrefetchScalarGridSpec(
        num_scalar_prefetch=1, grid=(nb,),
        in_specs=[pl.BlockSpec((None, tm, cols), lambda i, j_ref: (j_ref[0], i, 0)),
                  pl.BlockSpec((3, tm, cols), lambda i, j_ref: (0, i, 0))],
        out_specs=pl.BlockSpec((tm, cols), lambda i, j_ref: (i, 0)))
    return pl.pallas_call(body, name="grad_chip_sum", grid_spec=gs,
                          out_shape=jax.ShapeDtypeStruct((rows, cols), F32),
                          compiler_params=_params(("arbitrary",)))(j_idx, s, r)


def adamw(w, g, m, v, *, name):
    rows, cols = w.shape
    tm = _row_tile(rows, cols, 7 * 4)

    return rowwise(_adamw_math, [w, g, m, v], [], [(cols, F32)] * 3, tm=tm, name=name)


def _adamw_math(wb, gb, mb, vb):
    m2 = ADAM_B1 * mb + (1.0 - ADAM_B1) * gb
    v2 = ADAM_B2 * vb + (1.0 - ADAM_B2) * (gb * gb)
    m_hat = m2 / (1.0 - ADAM_B1 ** ADAM_STEP)
    v_hat = v2 / (1.0 - ADAM_B2 ** ADAM_STEP)
    delta = -ADAM_LR * (m_hat / (jnp.sqrt(v_hat) + ADAM_EPS) + ADAM_WD * wb)
    return delta, m2, v2


def adamw_shard(w, g_own, g_sib, m, v, c_idx, *, name):
    rows, cols = g_own.shape
    tm = _row_tile(rows, cols, 9 * 4)
    nb = rows // tm

    def body(c_ref, w_ref, go_ref, gs_ref, m_ref, v_ref, g_out, d_out, m_out, v_out):
        gb = jnp.where(pl.program_id(0) == c_ref[0], go_ref[...], gs_ref[...])
        delta, m2, v2 = _adamw_math(w_ref[...], gb, m_ref[...], v_ref[...])
        g_out[...] = gb
        d_out[...] = delta
        m_out[...] = m2
        v_out[...] = v2

    full = pl.BlockSpec((tm, cols), lambda h, i, c_ref: (h * nb + i, 0))
    half = pl.BlockSpec((tm, cols), lambda h, i, c_ref: (i, 0))
    gs = pltpu.PrefetchScalarGridSpec(num_scalar_prefetch=1, grid=(2, nb), in_specs=[full, half, half, full, full],
                                      out_specs=[full] * 4)
    return pl.pallas_call(body, name=name, grid_spec=gs, out_shape=[jax.ShapeDtypeStruct(w.shape, F32)] * 4,
                          compiler_params=_params(("arbitrary", "arbitrary")))(c_idx, w, g_own, g_sib, m, v)


def sum_devices(a):
    def body(a_ref, o_ref):
        t = a_ref[0]
        for k in range(1, 8):
            t = t + a_ref[k]
        o_ref[...] = t

    return pl.pallas_call(body, name="sum_small_grads", out_shape=jax.ShapeDtypeStruct(a.shape[1:], a.dtype))(a)


def _halves(w2d):
    r, c = w2d.shape
    return w2d.reshape(2, r // 2, c)


def kernel(x, p, ffn1_norm, ffn1_w_gate, ffn1_w_up, ffn1_w_down, mix_norm, w_in, q_a_norm, w_uq, kv_a_norm, w_ukv, na_rpb, w_branch_a, w_branch_b, w_out, ffn2_norm, ffn2_w_gate, ffn2_w_up, ffn2_w_down, pl_norm, w_pl, w_pl_gate, final_norm, loss_target, m_ffn1_norm, m_ffn1_w_gate, m_ffn1_w_up, m_ffn1_w_down, m_mix_norm, m_w_in, m_q_a_norm, m_w_uq, m_kv_a_norm, m_w_ukv, m_na_rpb, m_w_branch_a, m_w_branch_b, m_w_out, m_ffn2_norm, m_ffn2_w_gate, m_ffn2_w_up, m_ffn2_w_down, m_pl_norm, m_w_pl, m_w_pl_gate, m_final_norm, v_ffn1_norm, v_ffn1_w_gate, v_ffn1_w_up, v_ffn1_w_down, v_mix_norm, v_w_in, v_q_a_norm, v_w_uq, v_kv_a_norm, v_w_ukv, v_na_rpb, v_w_branch_a, v_w_branch_b, v_w_out, v_ffn2_norm, v_ffn2_w_gate, v_ffn2_w_up, v_ffn2_w_down, v_pl_norm, v_w_pl, v_w_pl_gate, v_final_norm):
    big = ["ffn1_w_gate", "ffn1_w_up", "ffn1_w_down", "w_in", "w_uq", "w_ukv", "w_branch_a", "w_branch_b", "w_out",
           "ffn2_w_gate", "ffn2_w_up", "ffn2_w_down", "w_pl", "w_pl_gate"]
    col_sharded = {"ffn1_w_gate", "ffn1_w_up", "w_in", "w_uq", "w_ukv", "w_branch_a", "w_branch_b", "ffn2_w_gate",
                   "ffn2_w_up", "w_pl"}
    small = ["ffn1_norm", "mix_norm", "q_a_norm", "kv_a_norm", "na_rpb", "ffn2_norm", "pl_norm", "final_norm"]
    order = ["ffn1_norm", "ffn1_w_gate", "ffn1_w_up", "ffn1_w_down", "mix_norm", "w_in", "q_a_norm", "w_uq",
             "kv_a_norm", "w_ukv", "na_rpb", "w_branch_a", "w_branch_b", "w_out", "ffn2_norm", "ffn2_w_gate",
             "ffn2_w_up", "ffn2_w_down", "pl_norm", "w_pl", "w_pl_gate", "final_norm"]
    env = dict(locals())
    W = {n: env[n] for n in order}
    Mo = {n: env["m_" + n] for n in order}
    Vo = {n: env["v_" + n] for n in order}

    xs = x[0]
    S, D = xs.shape
    tgt = loss_target[0]
    ps = p[0, 0]
    NAW = w_branch_a.shape[1]
    MLAW = w_branch_b.shape[1]
    NH, MH = NAW // HEAD_DIM, MLAW // HEAD_DIM
    QR, KVR = w_uq.shape[1], w_ukv.shape[1]
    F = ffn1_w_down.shape[1] * N_CHIPS
    cx, cy, cc = _place()
    c_idx = jnp.reshape(cc, (1,)).astype(jnp.int32)
    j_idx = jnp.reshape(2 * cx + cy, (1,)).astype(jnp.int32)

    shards = [_halves(W[n][0].astype(BF16)) for n in big]
    me_chip = 2 * cx + cy
    landings = [lax.dynamic_update_slice(lax.empty((N_CHIPS,) + s.shape, BF16), s[None], (me_chip, 0, 0, 0))
                for s in shards]
    groups = [["ffn1_w_gate"], ["ffn1_w_up"], ["ffn1_w_down"], ["w_in"],
              ["w_uq", "w_ukv", "w_branch_a", "w_branch_b", "w_out"],
              ["ffn2_w_gate", "ffn2_w_up", "ffn2_w_down"], ["w_pl", "w_pl_gate"]]
    group_idx = [[big.index(n) for n in members] for members in groups]
    group_sems, shards_thru, landings_thru = gather_start(shards, landings, group_idx)
    gathered = {}

    def arrive(n, after):
        g = [n in members for members in groups].index(True)
        idx = group_idx[g]
        got = gather_wait(group_sems[g], [shards_thru[i] for i in idx], [landings_thru[i] for i in idx], after,
                          str(g))
        gathered.update(zip(groups[g], gather_forward(got, str(g))))

    def stacked(n, after=None):
        if n not in gathered:
            arrive(n, after)
        g = gathered[n]
        return g.reshape(N_CHIPS, 2 * g.shape[2], g.shape[3])

    def plain(n, after=None):
        if n in col_sharded:
            st = stacked(n, after)
            return st.transpose(1, 0, 2).reshape(st.shape[1], N_CHIPS * st.shape[2])
        if n not in gathered:
            arrive(n, after)
        g = gathered[n]
        return g.reshape(N_CHIPS * 2 * g.shape[2], g.shape[3])

    n_front = 3 * NAW + QR + KVR
    off_ga = n_front
    off_kr = n_front + 2 * D

    pos = jnp.arange(S, dtype=F32)
    inv_freq = 1.0 / (ROPE_THETA ** (jnp.arange(0, MLA_ROPE, 2, dtype=F32) / MLA_ROPE))
    ang = pos[:, None] * inv_freq[None, :]
    zpad = jnp.zeros((S, LANES - MLA_ROPE), F32)
    cos_t = jnp.concatenate([jnp.cos(ang), jnp.cos(ang), zpad], axis=1)
    sin_t = jnp.concatenate([-jnp.sin(ang), jnp.sin(ang), zpad], axis=1)

    def ffn_fwd(h, norm_g, tag, pre):
        n = norm_fwd(h, norm_g, name=f"{tag}_norm")
        g = mm(n, stacked(pre + "_w_gate", n), name=f"{tag}_gate", b_stack=True)
        u = mm(n, stacked(pre + "_w_up", g), name=f"{tag}_up", b_stack=True)
        a = swiglu_fwd(g, u, name=f"{tag}_act")
        h_out = mm(a, plain(pre + "_w_down", a), name=f"{tag}_down", res=h, alpha=0.5)
        return h_out, (n, g, u, a)

    def ffn_bwd(h, norm_g, saved, dh, dh_half, tag, pre, last):
        n, g, u, a = saved
        gw_down = mm(a, dh_half, name=f"{tag}_dw_down", ta=True, out_dtype=BF16)
        da = mm(dh_half, plain(pre + "_w_down"), name=f"{tag}_da", tb=True)
        dg, du = swiglu_bwd(g, u, da, name=f"{tag}_dact")
        gw_gate = mm(n, dg, name=f"{tag}_dw_gate", ta=True, out_dtype=BF16, out_stack=True)
        gw_up = mm(n, du, name=f"{tag}_dw_up", ta=True, out_dtype=BF16, out_stack=True)
        dn = mm(dg, stacked(pre + "_w_gate"), name=f"{tag}_dn_gate", tb=True, b_stack=True)
        dn = mm(du, stacked(pre + "_w_up"), name=f"{tag}_dn_up", tb=True, b_stack=True, res=dn)
        outs = norm_bwd(h, norm_g, dn, name=f"{tag}_dnorm", res=dh, bf16_alpha=None if last else 1.0)
        return outs, gw_gate, gw_up, gw_down

    h1, ffn1_saved = ffn_fwd(xs, ffn1_norm, "ffn1", "ffn1")
    u_mix = norm_fwd(h1, mix_norm, name="mix_norm")
    win = plain("w_in", u_mix)
    win_l = jnp.concatenate([win[:, :n_front], win[:, n_front + MLA_ROPE:], win[:, n_front:n_front + MLA_ROPE],
                             jnp.zeros((D, LANES - MLA_ROPE), BF16)], axis=1)
    z = mm(u_mix, win_l, name="mix_in")
    bias = na_bias(na_rpb[0])
    o_a = na_fwd(z, bias, NH, S)
    c_q = norm_fwd((z, QR, 3 * NAW // QR), q_a_norm, name="q_a_norm")
    c_kv = norm_fwd((z, KVR, (3 * NAW + QR) // KVR), kv_a_norm, name="kv_a_norm")
    wuq = plain("w_uq", c_kv).reshape(QR, MH, MLA_QK)
    wuq_n = wuq[:, :, :MLA_NOPE].reshape(QR, MH * MLA_NOPE)
    wuq_r = jnp.pad(wuq[:, :, MLA_NOPE:], ((0, 0), (0, 0), (0, LANES - MLA_ROPE))).reshape(QR, MH * LANES)
    wukv = plain("w_ukv").reshape(KVR, MH, 2, HEAD_DIM)
    wuk = wukv[:, :, 0].reshape(KVR, MH * HEAD_DIM)
    wuv = wukv[:, :, 1].reshape(KVR, MH * HEAD_DIM)
    q_n = mm(c_q, wuq_n, name="mla_q_nope", out_dtype=BF16)
    q_r = rope(mm(c_q, wuq_r, name="mla_q_rope"), cos_t, sin_t, name="rope_q", out_dtype=BF16)
    k_n = mm(c_kv, wuk, name="mla_k_nope", out_dtype=BF16)
    v_m = mm(c_kv, wuv, name="mla_v", out_dtype=BF16)
    k_r = rope((z, LANES, off_kr // LANES), cos_t, sin_t, name="rope_k", out_dtype=BF16)
    o_b, lse = mla_fwd(q_n, q_r, k_n, v_m, k_r, MH, S)
    y_a = mm(o_a, stacked("w_branch_a"), name="branch_a", b_stack=True)
    y_b = mm(o_b, stacked("w_branch_b"), name="branch_b", b_stack=True)
    z_ga, z_gb = (z, D, off_ga // D), (z, D, off_ga // D + 1)
    merged = rowwise(lambda ga, gb, ya, yb: _sig(ga) * ya + _sig(gb) * yb, [z_ga, z_gb, y_a, y_b], [], [(D, BF16)],
                     tm=256, name="merge")[0]
    h2 = mm(merged, plain("w_out"), name="mix_out", res=h1)
    h3, ffn2_saved = ffn_fwd(h2, ffn2_norm, "ffn2", "ffn2")
    n4 = norm_fwd(h3, pl_norm, name="pl_norm")
    pg_pre = mm(n4, plain("w_pl_gate", n4), name="pl_gate")
    pe = mm(ps, stacked("w_pl"), name="pl_embed", b_stack=True)

    def tail(h3b, pgb, peb, tb_, fg):
        pg = _sig(pgb)
        h4 = h3b + pg * peb
        r = _rstd(h4)
        xh = h4 * r
        err = xh * fg - tb_
        loss_rows = jnp.mean(err * err, axis=-1, keepdims=True)
        dy = err * (1.0 / D)
        dxh = dy * fg
        dh4 = r * (dxh - xh * jnp.mean(dxh * xh, axis=-1, keepdims=True))
        loss_part = jnp.broadcast_to(0.5 * jnp.sum(loss_rows, axis=0, keepdims=True), (1, LANES))
        return (dh4, dh4 * peb * pg * (1.0 - pg), dh4 * pg, loss_part, jnp.sum(dy * xh, axis=0, keepdims=True))

    dh4, dpg_pre, dpe, loss_part, g_final = rowwise(
        tail, [h3, pg_pre, pe, tgt], [final_norm.reshape(1, D)], [(D, F32), (D, BF16), (D, BF16)],
        accs=[(1, LANES), (1, D)], tm=128, name="loss_tail")
    loss = lax.psum(loss_part[0, 0], ("x", "y", "c"))

    G = {}
    G["w_pl"] = mm(ps, dpe, name="pl_dw_embed", ta=True, out_dtype=BF16, out_stack=True)
    G["w_pl_gate"] = mm(n4, dpg_pre, name="pl_dw_gate", ta=True, out_dtype=BF16)
    dn4 = mm(dpg_pre, plain("w_pl_gate"), name="pl_dn", tb=True)
    dh3, dh3_half, g_pl = norm_bwd(h3, pl_norm, dn4, name="pl_dnorm", res=dh4, bf16_alpha=0.5)
    (dh2, dh2_b, g_ffn2), G["ffn2_w_gate"], G["ffn2_w_up"], G["ffn2_w_down"] = ffn_bwd(
        h2, ffn2_norm, ffn2_saved, dh3, dh3_half, "ffn2", "ffn2", last=False)

    G["w_out"] = mm(merged, dh2_b, name="mix_dw_out", ta=True, out_dtype=BF16)
    dmerged = mm(dh2_b, plain("w_out"), name="mix_dmerged", tb=True)

    def merge_bwd(ga, gb, ya, yb, dm):
        sa, sb = _sig(ga), _sig(gb)
        return dm * sa, dm * sb, dm * ya * sa * (1.0 - sa), dm * yb * sb * (1.0 - sb)

    dy_a, dy_b, dga, dgb = rowwise(merge_bwd, [z_ga, z_gb, y_a, y_b, dmerged], [], [(D, BF16)] * 4, tm=256,
                                   name="merge_bwd")
    G["w_branch_a"] = mm(o_a, dy_a, name="branch_a_dw", ta=True, out_dtype=BF16, out_stack=True)
    G["w_branch_b"] = mm(o_b, dy_b, name="branch_b_dw", ta=True, out_dtype=BF16, out_stack=True)
    do_a = mm(dy_a, stacked("w_branch_a"), name="branch_a_dx", tb=True, b_stack=True)
    do_b = mm(dy_b, stacked("w_branch_b"), name="branch_b_dx", tb=True, b_stack=True)
    dq_na, dk_na, dv_na, dbias = na_bwd(z, bias, do_a, NH, S)
    g_rpb = na_rpb_grad(dbias)
    dq_n, dq_rr, dk_n, dv_m, dk_rr = mla_bwd(q_n, q_r, k_n, v_m, k_r, lse, do_b, MH, S)
    dq_r = rope(dq_rr, cos_t, -sin_t, name="rope_q_bwd", out_dtype=BF16)
    dk_r = rope(dk_rr, cos_t, -sin_t, name="rope_k_bwd", out_dtype=BF16)
    gw_uq_n = mm(c_q, dq_n, name="mla_dw_q_nope", ta=True, out_dtype=BF16)
    gw_uq_r = mm(c_q, dq_r, name="mla_dw_q_rope", ta=True, out_dtype=BF16)
    dc_q = mm(dq_n, wuq_n, name="mla_dcq_nope", tb=True)
    dc_q = mm(dq_r, wuq_r, name="mla_dcq_rope", tb=True, res=dc_q)
    gw_uk = mm(c_kv, dk_n, name="mla_dw_k", ta=True, out_dtype=BF16)
    gw_uv = mm(c_kv, dv_m, name="mla_dw_v", ta=True, out_dtype=BF16)
    dc_kv = mm(dk_n, wuk, name="mla_dckv_k", tb=True)
    dc_kv = mm(dv_m, wuv, name="mla_dckv_v", tb=True, res=dc_kv)
    dq_lat, g_qa = norm_bwd((z, QR, 3 * NAW // QR), q_a_norm, dc_q, name="q_a_dnorm", want_f32=False, bf16_alpha=1.0)
    dkv_lat, g_kva = norm_bwd((z, KVR, (3 * NAW + QR) // KVR), kv_a_norm, dc_kv, name="kv_a_dnorm", want_f32=False,
                              bf16_alpha=1.0)
    dz = jnp.concatenate([dq_na, dk_na.astype(BF16), dv_na.astype(BF16), dq_lat, dkv_lat, dga, dgb, dk_r], axis=1)
    gw_in_l = mm(u_mix, dz, name="mix_dw_in", ta=True, out_dtype=BF16)
    du_mix = mm(dz, win_l, name="mix_du", tb=True)
    dh1, dh1_half, g_mix = norm_bwd(h1, mix_norm, du_mix, name="mix_dnorm", res=dh2, bf16_alpha=0.5)
    (grad_x, g_ffn1), G["ffn1_w_gate"], G["ffn1_w_up"], G["ffn1_w_down"] = ffn_bwd(
        xs, ffn1_norm, ffn1_saved, dh1, dh1_half, "ffn1", "ffn1", last=True)

    def to_stack(g2d):
        k, n = g2d.shape
        return g2d.reshape(k, N_CHIPS, n // N_CHIPS).transpose(1, 0, 2)

    gw_in = jnp.concatenate([gw_in_l[:, :n_front], gw_in_l[:, off_kr:off_kr + MLA_ROPE], gw_in_l[:, n_front:off_kr]],
                            axis=1)
    G["w_in"] = to_stack(gw_in)
    gw_uq = jnp.concatenate([gw_uq_n.reshape(QR, MH, MLA_NOPE), gw_uq_r.reshape(QR, MH, LANES)[:, :, :MLA_ROPE]],
                            axis=2).reshape(QR, MH * MLA_QK)
    G["w_uq"] = to_stack(gw_uq)
    gw_ukv = jnp.stack([gw_uk.reshape(KVR, MH, HEAD_DIM), gw_uv.reshape(KVR, MH, HEAD_DIM)], axis=2)
    G["w_ukv"] = to_stack(gw_ukv.reshape(KVR, MH * 2 * HEAD_DIM))

    def four(n):
        g = G[n]
        if g.ndim == 2:
            return g.reshape(N_CHIPS, 2, g.shape[0] // (2 * N_CHIPS), g.shape[1])
        return g.reshape(N_CHIPS, 2, g.shape[1] // 2, g.shape[2])

    g4 = [four(n) for n in big]
    from_sibling = pair_exchange(g4)
    sums = [pair_sum(a, b, c_idx) for a, b in zip(g4, from_sibling)]
    from_chips = chip_scatter(sums)
    halves = [chip_sum(a, b, j_idx) for a, b in zip(sums, from_chips)]
    sibling_halves = dict(zip(big, half_exchange(halves)))
    own_halves = dict(zip(big, halves))
    grads = {}

    small_g = {"ffn1_norm": g_ffn1, "mix_norm": g_mix, "q_a_norm": g_qa, "kv_a_norm": g_kva, "na_rpb": g_rpb,
               "ffn2_norm": g_ffn2, "pl_norm": g_pl, "final_norm": g_final}
    sizes = [int(np.prod(W[n].shape)) for n in small]
    total = sum(sizes)
    padded = -(-total // (8 * LANES)) * (8 * LANES)

    def pack(parts):
        flat = jnp.concatenate([jnp.reshape(parts[n], (-1,)).astype(F32) for n in small]
                               + [jnp.zeros((padded - total,), F32)])
        return flat.reshape(padded // LANES, LANES)

    def unpack(a):
        flat, out, o = a.reshape(-1), {}, 0
        for n, sz in zip(small, sizes):
            out[n] = flat[o:o + sz].reshape(W[n].shape)
            o += sz
        return out

    g_small = sum_devices(gather_small(pack(small_g)))
    d_small, m_small, v_small = adamw(pack(W), g_small, pack(Mo), pack(Vo), name="adamw_small")
    grads.update(unpack(g_small))
    delta, new_m, new_v = unpack(d_small), unpack(m_small), unpack(v_small)

    for n in big:
        shp = W[n].shape
        two_d = lambda a: a.reshape(shp[1], shp[2])
        g_, d_, m_, v_ = adamw_shard(two_d(W[n]), own_halves[n], sibling_halves[n], two_d(Mo[n]), two_d(Vo[n]),
                                     c_idx, name="adamw_" + n)
        grads[n], delta[n], new_m[n], new_v[n] = g_.reshape(shp), d_.reshape(shp), m_.reshape(shp), v_.reshape(shp)

    return (loss, grad_x[None], *[grads[n] for n in order], *[delta[n] for n in order],
            *[new_m[n] for n in order], *[new_v[n] for n in order])
```

```python
import functools

import numpy as np
import jax
import jax.numpy as jnp
from jax import lax
from jax.experimental import pallas as pl
from jax.experimental.pallas import tpu as pltpu

F32 = jnp.float32
BF16 = jnp.bfloat16

VMEM_LIMIT_V7X = 56 * 1024 * 1024
VMEM_BUDGET_V7X = 40 * 1024 * 1024
LANES = 128

GRID_W = 64
NA_WIN_ROWS = 8
NA_WIN_COLS = 16
HEAD_DIM = 128
MLA_NOPE = 128
MLA_ROPE = 64
MLA_QK = MLA_NOPE + MLA_ROPE
ROPE_THETA = 10000.0
NORM_EPS = 1e-6
NEG_INF = -1e30
N_CHIPS = 4

ADAM_LR = 0.001
ADAM_B1 = 0.9
ADAM_B2 = 0.999
ADAM_EPS = 1e-08
ADAM_WD = 0.01
ADAM_STEP = 10

MESH = pl.DeviceIdType.MESH
ANY = pl.BlockSpec(memory_space=pl.ANY)


def _params(sem=None):
    return pltpu.CompilerParams(dimension_semantics=sem, vmem_limit_bytes=VMEM_LIMIT_V7X)


def _pick(n, target, align):
    best = None
    t = align
    while t <= min(n, target):
        if n % t == 0:
            best = t
        t += align
    return n if best is None else best


def mm(a, b, *, name, ta=False, tb=False, out_dtype=F32, res=None, alpha=1.0, b_stack=False, out_stack=False,
       exact=False):
    K, M = (a.shape if ta else a.shape[::-1])
    if b_stack:
        nst = b.shape[0]
        if tb:
            N, kb = b.shape[1], b.shape[2]
            Kb, nb = nst * kb, None
        else:
            Kb, nb = b.shape[1], b.shape[2]
            N = nst * nb
    else:
        N, Kb = (b.shape if tb else b.shape[::-1])
    assert K == Kb, (a.shape, b.shape, ta, tb)
    if out_stack:
        assert N % N_CHIPS == 0
    n_unit = N // N_CHIPS if out_stack else (nb if (b_stack and not tb) else N)
    k_unit = kb if (b_stack and tb) else K
    tn = _pick(n_unit, 512, LANES) if n_unit % 512 == 0 or n_unit <= 512 else _pick(n_unit, 1536, LANES)
    tk = _pick(k_unit, 2048, LANES)
    tm = _pick(M, 1024, LANES if ta else 16)
    isz = lambda t: jnp.dtype(t.dtype).itemsize
    osz = jnp.dtype(out_dtype).itemsize

    def vmem(tm_):
        return (2 * tm_ * tk * isz(a) + 2 * tk * tn * isz(b) + 2 * tm_ * tn * osz + tm_ * tn * 4
                + (2 * tm_ * tn * isz(res) if res is not None else 0))

    while vmem(tm) > VMEM_BUDGET_V7X and tm % 2 == 0 and (tm // 2) % (LANES if ta else 16) == 0:
        tm //= 2
    nk = K // tk
    gm, gn = M // tm, N // tn

    a_spec = pl.BlockSpec((tk, tm), lambda i, j, k: (k, i)) if ta else pl.BlockSpec((tm, tk), lambda i, j, k: (i, k))
    if b_stack and not tb:
        q = nb // tn
        b_spec = pl.BlockSpec((None, tk, tn), lambda i, j, k: (j // q, k, j % q))
    elif b_stack and tb:
        q = kb // tk
        b_spec = pl.BlockSpec((None, tn, tk), lambda i, j, k: (k // q, j, k % q))
    elif tb:
        b_spec = pl.BlockSpec((tn, tk), lambda i, j, k: (j, k))
    else:
        b_spec = pl.BlockSpec((tk, tn), lambda i, j, k: (k, j))
    if out_stack:
        qo = (N // N_CHIPS) // tn
        o_spec = pl.BlockSpec((None, tm, tn), lambda i, j, k: (j // qo, i, j % qo))
        o_shape = jax.ShapeDtypeStruct((N_CHIPS, M, N // N_CHIPS), out_dtype)
    else:
        o_spec = pl.BlockSpec((tm, tn), lambda i, j, k: (i, j))
        o_shape = jax.ShapeDtypeStruct((M, N), out_dtype)
    dims = (((0 if ta else 1,), (1 if tb else 0,)), ((), ()))
    has_res = res is not None

    def body(*refs):
        if has_res:
            a_ref, b_ref, r_ref, o_ref, acc_ref = refs
        else:
            a_ref, b_ref, o_ref, acc_ref = refs
            r_ref = None
        k = pl.program_id(2)
        if exact:
            part = lax.dot_general(a_ref[...], b_ref[...], dims, preferred_element_type=F32,
                                   precision=lax.Precision.HIGHEST)
        else:
            part = lax.dot_general(a_ref[...].astype(BF16), b_ref[...].astype(BF16), dims,
                                   preferred_element_type=F32)

        def finish(total):
            if alpha != 1.0:
                total = total * alpha
            if has_res:
                total = total + r_ref[...].astype(F32)
            o_ref[...] = total.astype(out_dtype)

        if nk == 1:
            finish(part)
        else:
            @pl.when(k == 0)
            def _():
                acc_ref[...] = part

            @pl.when(jnp.logical_and(k > 0, k < nk - 1))
            def _():
                acc_ref[...] += part

            @pl.when(k == nk - 1)
            def _():
                finish(acc_ref[...] + part)

    in_specs = [a_spec, b_spec]
    args = [a, b]
    if has_res:
        in_specs.append(pl.BlockSpec((tm, tn), lambda i, j, k: (i, j)))
        args.append(res)
    return pl.pallas_call(
        body, name=name, grid=(gm, gn, nk), in_specs=in_specs, out_specs=o_spec, out_shape=o_shape,
        scratch_shapes=[pltpu.VMEM((tm, tn) if nk > 1 else (8, LANES), F32)],
        compiler_params=_params(("parallel", "parallel", "arbitrary")),
    )(*args)


def rowwise(fn, rows, consts, outs, accs=(), *, tm, name, tn=None):
    rows = [r if isinstance(r, tuple) else (r, r.shape[1], 0) for r in rows]
    S = rows[0][0].shape[0]
    tm = _pick(S, tm, 16)
    nrow, ncon, nout = len(rows), len(consts), len(outs)
    if tn is None:
        grid = (S // tm,)
        in_specs = [pl.BlockSpec((tm, w), functools.partial(lambda i, cb: (i, cb), cb=cb)) for _, w, cb in rows]
        in_specs += [pl.BlockSpec(c.shape, lambda i: (0, 0)) for c in consts]
        out_specs = [pl.BlockSpec((tm, n), lambda i: (i, 0)) for n, _ in outs]
        out_specs += [pl.BlockSpec(s, lambda i: (0, 0)) for s in accs]
        sem = ("arbitrary",)
    else:
        assert not accs
        N = rows[0][1]
        grid = (S // tm, N // tn)
        in_specs = [pl.BlockSpec((tm, tn), lambda i, j: (i, j)) for _ in rows]
        in_specs += [pl.BlockSpec(c.shape, lambda i, j: (0, 0)) for c in consts]
        out_specs = [pl.BlockSpec((tm, tn), lambda i, j: (i, j)) for _ in outs]
        sem = ("parallel", "parallel")
    out_shape = [jax.ShapeDtypeStruct((S, n), dt) for n, dt in outs]
    out_shape += [jax.ShapeDtypeStruct(s, F32) for s in accs]

    def body(*refs):
        vals = fn(*[r[...] for r in refs[:nrow + ncon]])
        if not isinstance(vals, (tuple, list)):
            vals = (vals,)
        o_refs = refs[nrow + ncon:]
        for o_ref, v in zip(o_refs[:nout], vals[:nout]):
            o_ref[...] = v.astype(o_ref.dtype)
        if accs:
            first = pl.program_id(0) == 0

            def accumulate(a_ref, v):
                @pl.when(first)
                def _():
                    a_ref[...] = v

                @pl.when(jnp.logical_not(first))
                def _():
                    a_ref[...] += v

            for a_ref, v in zip(o_refs[nout:], vals[nout:]):
                accumulate(a_ref, v.astype(F32))

    res = pl.pallas_call(
        body, name=name, grid=grid, in_specs=in_specs, out_specs=out_specs, out_shape=out_shape,
        compiler_params=_params(sem),
    )(*[r[0] for r in rows], *consts)
    return res


def _rstd(x):
    return lax.rsqrt(jnp.mean(x * x, axis=-1, keepdims=True) + NORM_EPS)


def norm_fwd(x, g, *, name, tm=256):
    w = x[1] if isinstance(x, tuple) else x.shape[1]

    def fn(xb, gb):
        return (xb * _rstd(xb)) * gb

    return rowwise(fn, [x], [g], [(w, BF16)], tm=tm, name=name)[0]


def norm_bwd(x, g, dn, *, name, res=None, want_f32=True, bf16_alpha=None, tm=256):
    w = x[1] if isinstance(x, tuple) else x.shape[1]
    has_res = res is not None

    def fn(*blocks):
        if has_res:
            xb, dnb, rb, gb = blocks
        else:
            xb, dnb, gb = blocks
        r = _rstd(xb)
        xh = xb * r
        dxh = dnb * gb
        dx = r * (dxh - xh * jnp.mean(dxh * xh, axis=-1, keepdims=True))
        if has_res:
            dx = dx + rb
        out = []
        if want_f32:
            out.append(dx)
        if bf16_alpha is not None:
            out.append(dx * bf16_alpha if bf16_alpha != 1.0 else dx)
        out.append(jnp.sum(dnb * xh, axis=0, keepdims=True))
        return tuple(out)

    outs = ([(w, F32)] if want_f32 else []) + ([(w, BF16)] if bf16_alpha is not None else [])
    rows = [x, dn] + ([res] if has_res else [])
    return rowwise(fn, rows, [g], outs, accs=[(1, w)], tm=tm, name=name)


def _sig(x):
    return jax.nn.sigmoid(x)


def swiglu_fwd(g, u, *, name):
    return rowwise(lambda gb, ub: gb * _sig(gb) * ub, [g, u], [], [(g.shape[1], BF16)], tm=256, name=name,
                   tn=_pick(g.shape[1], 1536, LANES))[0]


def swiglu_bwd(g, u, da, *, name):
    def fn(gb, ub, dab):
        s = _sig(gb)
        return dab * ub * (s + gb * s * (1.0 - s)), dab * (gb * s)

    n = g.shape[1]
    return rowwise(fn, [g, u, da], [], [(n, BF16), (n, BF16)], tm=256, name=name, tn=_pick(n, 1536, LANES))


def rope(x, cos, sin_signed, *, name, out_dtype):
    w = x[1] if isinstance(x, tuple) else x.shape[1]
    half = MLA_ROPE // 2

    def fn(xb, cb, sb):
        lane = lax.broadcasted_iota(jnp.int32, cb.shape, 1)
        outs = []
        for hb in range(w // LANES):
            blk = xb[:, hb * LANES:(hb + 1) * LANES]
            partner = jnp.where(lane < half, pltpu.roll(blk, LANES - half, 1), pltpu.roll(blk, half, 1))
            outs.append(blk * cb + partner * sb)
        return outs[0] if len(outs) == 1 else jnp.concatenate(outs, axis=1)

    return rowwise(fn, [x, cos, sin_signed], [], [(w, out_dtype)], tm=256, name=name)[0]


def _na_tables():
    cols = np.arange(GRID_W)
    kw = NA_WIN_COLS
    dc = np.clip(cols[None, :] - cols[:, None], -(kw - 1), kw - 1) + (kw - 1)
    onehot = np.zeros((LANES, GRID_W * GRID_W), np.float32)
    onehot[dc.reshape(-1), np.arange(GRID_W * GRID_W)] = 1.0
    col_start = np.clip(cols - kw // 2, 0, GRID_W - kw)
    mask = (cols[None, :] >= col_start[:, None]) & (cols[None, :] < col_start[:, None] + kw)
    return onehot, np.where(mask, 0.0, NEG_INF).astype(np.float32)


def na_bias(rpb):
    H = rpb.shape[0]
    nr, kh = 2 * NA_WIN_ROWS - 1, NA_WIN_ROWS
    onehot, maskb = _na_tables()
    rp = jnp.pad(rpb.reshape(H * nr, 2 * NA_WIN_COLS - 1), ((0, 0), (0, LANES - (2 * NA_WIN_COLS - 1))))
    t1 = mm(rp, jnp.asarray(onehot), name="na_bias_table", exact=True).reshape(H, nr, GRID_W, GRID_W)
    t1 = t1 + jnp.asarray(maskb)[None, None]
    per_t = [jnp.stack([t1[:, i - t + kh - 1] for i in range(kh)], axis=2) for t in range(kh)]
    return jnp.stack(per_t, axis=1).reshape(H, kh, GRID_W, kh * GRID_W)


def na_rpb_grad(db):
    H = db.shape[0]
    nr, kh = 2 * NA_WIN_ROWS - 1, NA_WIN_ROWS
    onehot, _ = _na_tables()
    db = db.reshape(H, kh, GRID_W, kh, GRID_W)
    per_dr = []
    for dri in range(nr):
        terms = [db[:, t, :, dri - (kh - 1) + t, :] for t in range(kh) if 0 <= dri - (kh - 1) + t < kh]
        per_dr.append(functools.reduce(jnp.add, terms))
    dt1 = jnp.stack(per_dr, axis=1).reshape(H * nr, GRID_W * GRID_W)
    g = mm(dt1, jnp.asarray(onehot), name="na_rpb_grad", tb=True, exact=True)
    return g[:, :2 * NA_WIN_COLS - 1].reshape(H, nr, 2 * NA_WIN_COLS - 1)


def _na_first_row(r, rows):
    return jnp.clip(r - NA_WIN_ROWS // 2, 0, rows - NA_WIN_ROWS)


def _na_scores(q_ref, k_ref, b_ref, start):
    q = q_ref[...].astype(BF16)
    k = k_ref[pl.ds(start, NA_WIN_ROWS * GRID_W), :].astype(BF16)
    s = lax.dot_general(q, k, (((1,), (1,)), ((), ())), preferred_element_type=F32)
    s = s * (HEAD_DIM ** -0.5) + b_ref[...]
    m = jnp.max(s, axis=-1, keepdims=True)
    e = jnp.exp(s - m)
    return q, k, e / jnp.sum(e, axis=-1, keepdims=True)


def na_fwd(z, bias, H, S):
    rows = S // GRID_W
    nkeys = NA_WIN_ROWS * GRID_W

    def body(q_ref, k_ref, v_ref, b_ref, o_ref):
        r = pl.program_id(1)
        start = pl.multiple_of(_na_first_row(r, rows) * GRID_W, GRID_W)
        _, _, p = _na_scores(q_ref, k_ref, b_ref, start)
        v = v_ref[pl.ds(start, nkeys), :].astype(BF16)
        o_ref[...] = jnp.dot(p.astype(BF16), v, preferred_element_type=F32).astype(o_ref.dtype)

    return pl.pallas_call(
        body, name="na_fwd", grid=(H, rows),
        in_specs=[pl.BlockSpec((GRID_W, HEAD_DIM), lambda h, r: (r, h)),
                  pl.BlockSpec((S, HEAD_DIM), lambda h, r: (0, H + h)),
                  pl.BlockSpec((S, HEAD_DIM), lambda h, r: (0, 2 * H + h)),
                  pl.BlockSpec((None, None, GRID_W, nkeys), lambda h, r: (h, r - _na_first_row(r, rows), 0, 0))],
        out_specs=pl.BlockSpec((GRID_W, HEAD_DIM), lambda h, r: (r, h)),
        out_shape=jax.ShapeDtypeStruct((S, H * HEAD_DIM), BF16),
        compiler_params=_params(("parallel", "arbitrary")),
    )(z, z, z, bias)


def na_bwd(z, bias, do, H, S):
    rows = S // GRID_W
    nkeys = NA_WIN_ROWS * GRID_W
    tn_dims = (((0,), (0,)), ((), ()))

    def body(q_ref, k_ref, v_ref, b_ref, do_ref, dq_ref, dk_ref, dv_ref, db_ref):
        r = pl.program_id(1)
        start = pl.multiple_of(_na_first_row(r, rows) * GRID_W, GRID_W)
        q, k, p = _na_scores(q_ref, k_ref, b_ref, start)
        v = v_ref[pl.ds(start, nkeys), :].astype(BF16)
        dob = do_ref[...].astype(BF16)
        dp = lax.dot_general(dob, v, (((1,), (1,)), ((), ())), preferred_element_type=F32)
        ds = p * (dp - jnp.sum(dp * p, axis=-1, keepdims=True))
        dsb = (ds * (HEAD_DIM ** -0.5)).astype(BF16)
        dq_ref[...] = jnp.dot(dsb, k, preferred_element_type=F32).astype(dq_ref.dtype)

        @pl.when(r == 0)
        def _():
            dk_ref[...] = jnp.zeros_like(dk_ref)
            dv_ref[...] = jnp.zeros_like(dv_ref)

        dk_ref[pl.ds(start, nkeys), :] += lax.dot_general(dsb, q, tn_dims, preferred_element_type=F32)
        dv_ref[pl.ds(start, nkeys), :] += lax.dot_general(p.astype(BF16), dob, tn_dims, preferred_element_type=F32)

        fresh = jnp.logical_or(r <= NA_WIN_ROWS // 2, r > rows - NA_WIN_ROWS // 2)

        @pl.when(fresh)
        def _():
            db_ref[...] = ds

        @pl.when(jnp.logical_not(fresh))
        def _():
            db_ref[...] += ds

    W = H * HEAD_DIM
    return pl.pallas_call(
        body, name="na_bwd", grid=(H, rows),
        in_specs=[pl.BlockSpec((GRID_W, HEAD_DIM), lambda h, r: (r, h)),
                  pl.BlockSpec((S, HEAD_DIM), lambda h, r: (0, H + h)),
                  pl.BlockSpec((S, HEAD_DIM), lambda h, r: (0, 2 * H + h)),
                  pl.BlockSpec((None, None, GRID_W, nkeys), lambda h, r: (h, r - _na_first_row(r, rows), 0, 0)),
                  pl.BlockSpec((GRID_W, HEAD_DIM), lambda h, r: (r, h))],
        out_specs=[pl.BlockSpec((GRID_W, HEAD_DIM), lambda h, r: (r, h)),
                   pl.BlockSpec((S, HEAD_DIM), lambda h, r: (0, h)),
                   pl.BlockSpec((S, HEAD_DIM), lambda h, r: (0, h)),
                   pl.BlockSpec((None, None, GRID_W, nkeys), lambda h, r: (h, r - _na_first_row(r, rows), 0, 0))],
        out_shape=[jax.ShapeDtypeStruct((S, W), BF16), jax.ShapeDtypeStruct((S, W), F32),
                   jax.ShapeDtypeStruct((S, W), F32), jax.ShapeDtypeStruct((H, NA_WIN_ROWS, GRID_W, nkeys), F32)],
        compiler_params=_params(("arbitrary", "arbitrary")),
    )(z, z, z, bias, do)


def _mla_scores(qn_ref, qr_ref, kn_ref, kr_ref):
    nt = (((1,), (1,)), ((), ()))
    s = lax.dot_general(qn_ref[...], kn_ref[...], nt, preferred_element_type=F32)
    s = s + lax.dot_general(qr_ref[...], kr_ref[...], nt, preferred_element_type=F32)
    return s * (MLA_QK ** -0.5)


def mla_fwd(qn, qr, kn, v, kr, H, S):
    tq = _pick(S, 256, 16)

    def body(qn_ref, qr_ref, kn_ref, v_ref, kr_ref, o_ref, lse_ref):
        s = _mla_scores(qn_ref, qr_ref, kn_ref, kr_ref)
        m = jnp.max(s, axis=-1, keepdims=True)
        e = jnp.exp(s - m)
        l = jnp.sum(e, axis=-1, keepdims=True)
        o_ref[...] = jnp.dot((e / l).astype(BF16), v_ref[...], preferred_element_type=F32).astype(o_ref.dtype)
        lse_ref[...] = jnp.broadcast_to(m + jnp.log(l), lse_ref.shape)

    qspec = pl.BlockSpec((tq, HEAD_DIM), lambda h, i: (i, h))
    kspec = pl.BlockSpec((S, HEAD_DIM), lambda h, i: (0, h))
    return pl.pallas_call(
        body, name="mla_fwd", grid=(H, S // tq),
        in_specs=[qspec, qspec, kspec, kspec, pl.BlockSpec((S, LANES), lambda h, i: (0, 0))],
        out_specs=[qspec, qspec],
        out_shape=[jax.ShapeDtypeStruct((S, H * HEAD_DIM), BF16), jax.ShapeDtypeStruct((S, H * LANES), F32)],
        compiler_params=_params(("parallel", "arbitrary")),
    )(qn, qr, kn, v, kr)


def mla_bwd(qn, qr, kn, v, kr, lse, do, H, S):
    tq = _pick(S, 256, 16)
    nt = (((1,), (1,)), ((), ()))
    tn_dims = (((0,), (0,)), ((), ()))

    def body(qn_ref, qr_ref, kn_ref, v_ref, kr_ref, lse_ref, do_ref, dqn_ref, dqr_ref, dkn_ref, dv_ref, dkr_ref):
        h, i = pl.program_id(0), pl.program_id(1)
        s = _mla_scores(qn_ref, qr_ref, kn_ref, kr_ref)
        p = jnp.exp(s - lse_ref[:, 0:1])
        dob = do_ref[...].astype(BF16)
        dp = lax.dot_general(dob, v_ref[...], nt, preferred_element_type=F32)
        ds = p * (dp - jnp.sum(dp * p, axis=-1, keepdims=True))
        dsb = (ds * (MLA_QK ** -0.5)).astype(BF16)
        dqn_ref[...] = jnp.dot(dsb, kn_ref[...], preferred_element_type=F32).astype(dqn_ref.dtype)
        dqr_ref[...] = jnp.dot(dsb, kr_ref[...], preferred_element_type=F32).astype(dqr_ref.dtype)

        @pl.when(i == 0)
        def _():
            dkn_ref[...] = jnp.zeros_like(dkn_ref)
            dv_ref[...] = jnp.zeros_like(dv_ref)

        @pl.when(jnp.logical_and(i == 0, h == 0))
        def _():
            dkr_ref[...] = jnp.zeros_like(dkr_ref)

        dkn_ref[...] += lax.dot_general(dsb, qn_ref[...], tn_dims, preferred_element_type=F32)
        dkr_ref[...] += lax.dot_general(dsb, qr_ref[...], tn_dims, preferred_element_type=F32)
        dv_ref[...] += lax.dot_general(p.astype(BF16), dob, tn_dims, preferred_element_type=F32)

    qspec = pl.BlockSpec((tq, HEAD_DIM), lambda h, i: (i, h))
    kspec = pl.BlockSpec((S, HEAD_DIM), lambda h, i: (0, h))
    rspec = pl.BlockSpec((S, LANES), lambda h, i: (0, 0))
    W = H * HEAD_DIM
    return pl.pallas_call(
        body, name="mla_bwd", grid=(H, S // tq),
        in_specs=[qspec, qspec, kspec, kspec, rspec, qspec, qspec],
        out_specs=[qspec, qspec, kspec, kspec, rspec],
        out_shape=[jax.ShapeDtypeStruct((S, W), BF16), jax.ShapeDtypeStruct((S, W), F32),
                   jax.ShapeDtypeStruct((S, W), F32), jax.ShapeDtypeStruct((S, W), F32),
                   jax.ShapeDtypeStruct((S, LANES), F32)],
        compiler_params=_params(("arbitrary", "arbitrary")),
    )(qn, qr, kn, v, kr, lse, do)


def _place():
    return lax.axis_index("x"), lax.axis_index("y"), lax.axis_index("c")


def _other_chips(x, y):
    return [(1 - x, y), (x, 1 - y), (1 - x, 1 - y)]


def _remote(src, dst, send_sem, recv_sem, to):
    return pltpu.make_async_remote_copy(src_ref=src, dst_ref=dst, send_sem=send_sem, recv_sem=recv_sem,
                                        device_id=to, device_id_type=MESH)


HBM = pl.BlockSpec(memory_space=pltpu.HBM)
SEM = pl.BlockSpec(memory_space=pltpu.SEMAPHORE)
EFFECT = pltpu.SideEffectType.DATAFLOW_SIDE_EFFECTING


def _in_hbm(a):
    return pltpu.with_memory_space_constraint(a, pltpu.HBM)


def gather_start(shards, landings, groups):
    n, ng = len(shards), len(groups)

    def body(*refs):
        ins, lands = refs[:n], refs[n:2 * n]
        sems = refs[2 * n:2 * n + 2 * ng]
        x, y, c = _place()
        me = 2 * x + y
        for g, members in enumerate(groups):
            for i, w in enumerate(members):
                for k, (px, py) in enumerate(_other_chips(x, y)):
                    _remote(ins[w].at[c], lands[w].at[me, c], sems[2 * g].at[3 * i + k], sems[2 * g + 1].at[3 * i + k],
                            (px, py, c)).start()

    sem_shapes = []
    for members in groups:
        sem_shapes += [pltpu.SemaphoreType.DMA((3 * len(members),))] * 2
    bufs = list(shards) + list(landings)
    outs = pl.pallas_call(
        body, name="gather_start",
        out_shape=tuple(sem_shapes) + tuple(pltpu.HBM(b.shape, b.dtype) for b in bufs),
        in_specs=[HBM] * (2 * n), out_specs=tuple([SEM] * (2 * ng) + [HBM] * (2 * n)),
        input_output_aliases={i: 2 * ng + i for i in range(2 * n)},
        compiler_params=pltpu.CompilerParams(has_side_effects=EFFECT),
    )(*[_in_hbm(b) for b in bufs])
    sems = [(outs[2 * g], outs[2 * g + 1]) for g in range(ng)]
    return sems, outs[2 * ng:2 * ng + n], outs[2 * ng + n:]


def gather_wait(sems, shards, landings, after, tag):
    n = len(shards)
    send, recv = sems

    def body(*refs):
        ins, lands = refs[:n], refs[n:2 * n]
        send_sem, recv_sem = refs[2 * n], refs[2 * n + 1]
        x, y, c = _place()
        me = 2 * x + y
        for w in range(n):
            for k, (px, py) in enumerate(_other_chips(x, y)):
                cp = _remote(ins[w].at[c], lands[w].at[2 * px + py, c], send_sem.at[3 * w + k], recv_sem.at[3 * w + k],
                             (px, py, c))
                cp.wait_send()
                cp.wait_recv()

    bufs = list(shards) + list(landings)
    outs = pl.pallas_call(
        body, name="gather_wait_" + tag, out_shape=tuple(pltpu.HBM(b.shape, b.dtype) for b in bufs),
        in_specs=[HBM] * (2 * n) + [SEM, SEM, ANY], out_specs=tuple([HBM] * (2 * n)),
        input_output_aliases={i: i for i in range(2 * n)},
        compiler_params=pltpu.CompilerParams(has_side_effects=EFFECT),
    )(*bufs, send, recv, after)
    return outs[n:]


def gather_forward(landings, tag):
    n = len(landings)

    def body(*refs):
        ins, outs = refs[:n], refs[n:2 * n]
        send, recv = refs[2 * n:]
        x, y, c = _place()
        sibling = (x, y, 1 - c)
        cps = []
        for w in range(n):
            for k, (px, py) in enumerate(_other_chips(x, y)):
                j = 2 * px + py
                cp = _remote(ins[w].at[j, c], outs[w].at[j, c], send.at[3 * w + k], recv.at[3 * w + k], sibling)
                cp.start()
                cps.append(cp)
        for w in range(n):
            for k, (px, py) in enumerate(_other_chips(x, y)):
                blk = outs[w].at[2 * px + py, 1 - c]
                _remote(blk, blk, send.at[3 * w + k], recv.at[3 * w + k], sibling).wait_recv()
        for cp in cps:
            cp.wait_send()

    return pl.pallas_call(
        body, name="gather_forward_" + tag, in_specs=[ANY] * n, out_specs=[ANY] * n,
        out_shape=[jax.ShapeDtypeStruct(a.shape, a.dtype) for a in landings],
        input_output_aliases={i: i for i in range(n)},
        scratch_shapes=[pltpu.SemaphoreType.DMA((3 * n,)), pltpu.SemaphoreType.DMA((3 * n,))],
    )(*landings)


def pair_exchange(grads, tag):
    n = len(grads)

    def body(*refs):
        ins, outs = refs[:n], refs[n:2 * n]
        send, recv = refs[2 * n:]
        x, y, c = _place()
        cps = []
        for w in range(n):
            cp = _remote(ins[w].at[:, 1 - c], outs[w], send.at[w], recv.at[w], (x, y, 1 - c))
            cp.start()
            cps.append(cp)
        for cp in cps:
            cp.wait()

    return pl.pallas_call(
        body, name="grad_pair_exchange_" + tag, in_specs=[ANY] * n, out_specs=[ANY] * n,
        out_shape=[jax.ShapeDtypeStruct((g.shape[0],) + g.shape[2:], g.dtype) for g in grads],
        scratch_shapes=[pltpu.SemaphoreType.DMA((n,)), pltpu.SemaphoreType.DMA((n,))],
    )(*grads)


def scatter_start(sums, landings, tag):
    n = len(sums)

    def body(*refs):
        ins, lands = refs[:n], refs[n:2 * n]
        send, recv = refs[2 * n], refs[2 * n + 1]
        x, y, c = _place()
        for w in range(n):
            for k, (px, py) in enumerate(_other_chips(x, y)):
                _remote(ins[w].at[2 * px + py], lands[w].at[k], send.at[3 * w + k], recv.at[3 * w + k], (px, py, c)).start()

    bufs = list(sums) + list(landings)
    outs = pl.pallas_call(
        body, name="scatter_start_" + tag,
        out_shape=(pltpu.SemaphoreType.DMA((3 * n,)),) * 2 + tuple(pltpu.HBM(b.shape, b.dtype) for b in bufs),
        in_specs=[HBM] * (2 * n), out_specs=tuple([SEM, SEM] + [HBM] * (2 * n)),
        input_output_aliases={i: 2 + i for i in range(2 * n)},
        compiler_params=pltpu.CompilerParams(has_side_effects=EFFECT),
    )(*[_in_hbm(b) for b in bufs])
    return (outs[0], outs[1]), outs[2:2 + n], outs[2 + n:]


def scatter_wait(sems, sums, landings, after, tag):
    n = len(sums)

    def body(*refs):
        ins, lands = refs[:n], refs[n:2 * n]
        send, recv = refs[2 * n], refs[2 * n + 1]
        x, y, c = _place()
        for w in range(n):
            for k, (px, py) in enumerate(_other_chips(x, y)):
                cp = _remote(ins[w].at[2 * px + py], lands[w].at[k], send.at[3 * w + k], recv.at[3 * w + k], (px, py, c))
                cp.wait_send()
                cp.wait_recv()

    bufs = list(sums) + list(landings)
    outs = pl.pallas_call(
        body, name="scatter_wait_" + tag, out_shape=tuple(pltpu.HBM(b.shape, b.dtype) for b in bufs),
        in_specs=[HBM] * (2 * n) + [SEM, SEM, ANY], out_specs=tuple([HBM] * (2 * n)),
        input_output_aliases={i: i for i in range(2 * n)},
        compiler_params=pltpu.CompilerParams(has_side_effects=EFFECT),
    )(*bufs, sems[0], sems[1], after)
    return outs[:n], outs[n:]


def half_exchange(halves, tag):
    n = len(halves)

    def body(*refs):
        ins, outs = refs[:n], refs[n:2 * n]
        send, recv = refs[2 * n:]
        x, y, c = _place()
        cps = []
        for w in range(n):
            cp = _remote(ins[w], outs[w], send.at[w], recv.at[w], (x, y, 1 - c))
            cp.start()
            cps.append(cp)
        for cp in cps:
            cp.wait()

    return pl.pallas_call(
        body, name="grad_half_exchange_" + tag, in_specs=[ANY] * n, out_specs=[ANY] * n,
        out_shape=[jax.ShapeDtypeStruct(h.shape, h.dtype) for h in halves],
        scratch_shapes=[pltpu.SemaphoreType.DMA((n,)), pltpu.SemaphoreType.DMA((n,))],
    )(*halves)


def gather_small(v):
    def body(v_ref, o_ref, send, recv, local):
        x, y, c = _place()
        me = 4 * x + 2 * y + c
        own = pltpu.make_async_copy(v_ref, o_ref.at[me], local)
        own.start()
        cps = []
        for k in range(1, 8):
            fx, fy, fc = (k >> 2) & 1, (k >> 1) & 1, k & 1
            to = (x ^ fx if fx else x, y ^ fy if fy else y, c ^ fc if fc else c)
            cp = _remote(v_ref, o_ref.at[me], send.at[k - 1], recv.at[k - 1], to)
            cp.start()
            cps.append(cp)
        for k in range(1, 8):
            fx, fy, fc = (k >> 2) & 1, (k >> 1) & 1, k & 1
            px, py, pc = (x ^ fx if fx else x, y ^ fy if fy else y, c ^ fc if fc else c)
            cps[k - 1].wait_send()
            _remote(v_ref, o_ref.at[4 * px + 2 * py + pc], send.at[k - 1], recv.at[k - 1], (px, py, pc)).wait_recv()
        own.wait()

    return pl.pallas_call(
        body, name="gather_small_grads", in_specs=[ANY], out_specs=ANY,
        out_shape=jax.ShapeDtypeStruct((8,) + v.shape, v.dtype),
        scratch_shapes=[pltpu.SemaphoreType.DMA((7,)), pltpu.SemaphoreType.DMA((7,)), pltpu.SemaphoreType.DMA],
    )(v)


def _row_tile(rows, cols, nbuf_bytes):
    tm = _pick(rows, 512, 16)
    while tm * cols * nbuf_bytes * 2 > VMEM_BUDGET_V7X and tm % 32 == 0:
        tm //= 2
    return tm


def pair_sum(g, r, c_idx, tag):
    _, _, rows, cols = g.shape
    tm = _row_tile(rows, cols, 2 + 2 + 2)
    nb = rows // tm

    def body(c_ref, g_ref, r_ref, o_ref):
        o_ref[...] = (g_ref[...].astype(F32) + r_ref[...].astype(F32)).astype(o_ref.dtype)

    gs = pltpu.PrefetchScalarGridSpec(
        num_scalar_prefetch=1, grid=(N_CHIPS, nb),
        in_specs=[pl.BlockSpec((None, None, tm, cols), lambda j, i, c_ref: (j, c_ref[0], i, 0)),
                  pl.BlockSpec((None, tm, cols), lambda j, i, c_ref: (j, i, 0))],
        out_specs=pl.BlockSpec((None, tm, cols), lambda j, i, c_ref: (j, i, 0)))
    return pl.pallas_call(body, name="grad_pair_sum_" + tag, grid_spec=gs,
                          out_shape=jax.ShapeDtypeStruct(r.shape, BF16),
                          compiler_params=_params(("arbitrary", "arbitrary")))(c_idx, g, r)


def chip_sum(s, r, j_idx, tag):
    _, rows, cols = s.shape
    tm = _row_tile(rows, cols, 2 + 3 * 2 + 4)
    nb = rows // tm

    def body(j_ref, s_ref, r_ref, o_ref):
        t = s_ref[...].astype(F32)
        for k in range(3):
            t = t + r_ref[k].astype(F32)
        o_ref[...] = t

    gs = pltpu.PrefetchScalarGridSpec(
        num_scalar_prefetch=1, grid=(nb,),
        in_specs=[pl.BlockSpec((None, tm, cols), lambda i, j_ref: (j_ref[0], i, 0)),
                  pl.BlockSpec((3, tm, cols), lambda i, j_ref: (0, i, 0))],
        out_specs=pl.BlockSpec((tm, cols), lambda i, j_ref: (i, 0)))
    return pl.pallas_call(body, name="grad_chip_sum_" + tag, grid_spec=gs,
                          out_shape=jax.ShapeDtypeStruct((rows, cols), F32),
                          compiler_params=_params(("arbitrary",)))(j_idx, s, r)


def adamw(w, g, m, v, *, name):
    rows, cols = w.shape
    tm = _row_tile(rows, cols, 7 * 4)

    return rowwise(_adamw_math, [w, g, m, v], [], [(cols, F32)] * 3, tm=tm, name=name)


def _adamw_math(wb, gb, mb, vb):
    m2 = ADAM_B1 * mb + (1.0 - ADAM_B1) * gb
    v2 = ADAM_B2 * vb + (1.0 - ADAM_B2) * (gb * gb)
    m_hat = m2 / (1.0 - ADAM_B1 ** ADAM_STEP)
    v_hat = v2 / (1.0 - ADAM_B2 ** ADAM_STEP)
    delta = -ADAM_LR * (m_hat / (jnp.sqrt(v_hat) + ADAM_EPS) + ADAM_WD * wb)
    return delta, m2, v2


def adamw_shard(w, g_own, g_sib, m, v, c_idx, *, name):
    rows, cols = g_own.shape
    tm = _row_tile(rows, cols, 9 * 4)
    nb = rows // tm

    def body(c_ref, w_ref, go_ref, gs_ref, m_ref, v_ref, g_out, d_out, m_out, v_out):
        gb = jnp.where(pl.program_id(0) == c_ref[0], go_ref[...], gs_ref[...])
        delta, m2, v2 = _adamw_math(w_ref[...], gb, m_ref[...], v_ref[...])
        g_out[...] = gb
        d_out[...] = delta
        m_out[...] = m2
        v_out[...] = v2

    full = pl.BlockSpec((tm, cols), lambda h, i, c_ref: (h * nb + i, 0))
    half = pl.BlockSpec((tm, cols), lambda h, i, c_ref: (i, 0))
    gs = pltpu.PrefetchScalarGridSpec(num_scalar_prefetch=1, grid=(2, nb), in_specs=[full, half, half, full, full],
                                      out_specs=[full] * 4)
    return pl.pallas_call(body, name=name, grid_spec=gs, out_shape=[jax.ShapeDtypeStruct(w.shape, F32)] * 4,
                          compiler_params=_params(("arbitrary", "arbitrary")))(c_idx, w, g_own, g_sib, m, v)


def sum_devices(a):
    def body(a_ref, o_ref):
        t = a_ref[0]
        for k in range(1, 8):
            t = t + a_ref[k]
        o_ref[...] = t

    return pl.pallas_call(body, name="sum_small_grads", out_shape=jax.ShapeDtypeStruct(a.shape[1:], a.dtype))(a)


def _halves(w2d):
    r, c = w2d.shape
    return w2d.reshape(2, r // 2, c)


def kernel(x, p, ffn1_norm, ffn1_w_gate, ffn1_w_up, ffn1_w_down, mix_norm, w_in, q_a_norm, w_uq, kv_a_norm, w_ukv, na_rpb, w_branch_a, w_branch_b, w_out, ffn2_norm, ffn2_w_gate, ffn2_w_up, ffn2_w_down, pl_norm, w_pl, w_pl_gate, final_norm, loss_target, m_ffn1_norm, m_ffn1_w_gate, m_ffn1_w_up, m_ffn1_w_down, m_mix_norm, m_w_in, m_q_a_norm, m_w_uq, m_kv_a_norm, m_w_ukv, m_na_rpb, m_w_branch_a, m_w_branch_b, m_w_out, m_ffn2_norm, m_ffn2_w_gate, m_ffn2_w_up, m_ffn2_w_down, m_pl_norm, m_w_pl, m_w_pl_gate, m_final_norm, v_ffn1_norm, v_ffn1_w_gate, v_ffn1_w_up, v_ffn1_w_down, v_mix_norm, v_w_in, v_q_a_norm, v_w_uq, v_kv_a_norm, v_w_ukv, v_na_rpb, v_w_branch_a, v_w_branch_b, v_w_out, v_ffn2_norm, v_ffn2_w_gate, v_ffn2_w_up, v_ffn2_w_down, v_pl_norm, v_w_pl, v_w_pl_gate, v_final_norm):
    big = ["ffn1_w_gate", "ffn1_w_up", "ffn1_w_down", "w_in", "w_uq", "w_ukv", "w_branch_a", "w_branch_b", "w_out",
           "ffn2_w_gate", "ffn2_w_up", "ffn2_w_down", "w_pl", "w_pl_gate"]
    col_sharded = {"ffn1_w_gate", "ffn1_w_up", "w_in", "w_uq", "w_ukv", "w_branch_a", "w_branch_b", "ffn2_w_gate",
                   "ffn2_w_up", "w_pl"}
    small = ["ffn1_norm", "mix_norm", "q_a_norm", "kv_a_norm", "na_rpb", "ffn2_norm", "pl_norm", "final_norm"]
    order = ["ffn1_norm", "ffn1_w_gate", "ffn1_w_up", "ffn1_w_down", "mix_norm", "w_in", "q_a_norm", "w_uq",
             "kv_a_norm", "w_ukv", "na_rpb", "w_branch_a", "w_branch_b", "w_out", "ffn2_norm", "ffn2_w_gate",
             "ffn2_w_up", "ffn2_w_down", "pl_norm", "w_pl", "w_pl_gate", "final_norm"]
    env = dict(locals())
    W = {n: env[n] for n in order}
    Mo = {n: env["m_" + n] for n in order}
    Vo = {n: env["v_" + n] for n in order}

    xs = x[0]
    S, D = xs.shape
    tgt = loss_target[0]
    ps = p[0, 0]
    NAW = w_branch_a.shape[1]
    MLAW = w_branch_b.shape[1]
    NH, MH = NAW // HEAD_DIM, MLAW // HEAD_DIM
    QR, KVR = w_uq.shape[1], w_ukv.shape[1]
    F = ffn1_w_down.shape[1] * N_CHIPS
    cx, cy, cc = _place()
    c_idx = jnp.reshape(cc, (1,)).astype(jnp.int32)
    j_idx = jnp.reshape(2 * cx + cy, (1,)).astype(jnp.int32)

    shards = [_halves(W[n][0].astype(BF16)) for n in big]
    me_chip = 2 * cx + cy
    landings = [lax.dynamic_update_slice(lax.empty((N_CHIPS,) + s.shape, BF16), s[None], (me_chip, 0, 0, 0))
                for s in shards]
    groups = [["ffn1_w_gate"], ["ffn1_w_up"], ["ffn1_w_down"], ["w_in"],
              ["w_uq", "w_ukv", "w_branch_a", "w_branch_b", "w_out"],
              ["ffn2_w_gate", "ffn2_w_up", "ffn2_w_down"], ["w_pl", "w_pl_gate"]]
    group_idx = [[big.index(n) for n in members] for members in groups]
    group_sems, shards_thru, landings_thru = gather_start(shards, landings, group_idx)
    gathered = {}

    def arrive(n, after):
        g = [n in members for members in groups].index(True)
        idx = group_idx[g]
        got = gather_wait(group_sems[g], [shards_thru[i] for i in idx], [landings_thru[i] for i in idx], after,
                          str(g))
        gathered.update(zip(groups[g], gather_forward(got, str(g))))

    def stacked(n, after=None):
        if n not in gathered:
            arrive(n, after)
        g = gathered[n]
        return g.reshape(N_CHIPS, 2 * g.shape[2], g.shape[3])

    def plain(n, after=None):
        if n in col_sharded:
            st = stacked(n, after)
            return st.transpose(1, 0, 2).reshape(st.shape[1], N_CHIPS * st.shape[2])
        if n not in gathered:
            arrive(n, after)
        g = gathered[n]
        return g.reshape(N_CHIPS * 2 * g.shape[2], g.shape[3])

    n_front = 3 * NAW + QR + KVR
    off_ga = n_front
    off_kr = n_front + 2 * D

    pos = jnp.arange(S, dtype=F32)
    inv_freq = 1.0 / (ROPE_THETA ** (jnp.arange(0, MLA_ROPE, 2, dtype=F32) / MLA_ROPE))
    ang = pos[:, None] * inv_freq[None, :]
    zpad = jnp.zeros((S, LANES - MLA_ROPE), F32)
    cos_t = jnp.concatenate([jnp.cos(ang), jnp.cos(ang), zpad], axis=1)
    sin_t = jnp.concatenate([-jnp.sin(ang), jnp.sin(ang), zpad], axis=1)

    def ffn_fwd(h, norm_g, tag, pre):
        n = norm_fwd(h, norm_g, name=f"{tag}_norm")
        g = mm(n, stacked(pre + "_w_gate", n), name=f"{tag}_gate", b_stack=True)
        u = mm(n, stacked(pre + "_w_up", g), name=f"{tag}_up", b_stack=True)
        a = swiglu_fwd(g, u, name=f"{tag}_act")
        h_out = mm(a, plain(pre + "_w_down", a), name=f"{tag}_down", res=h, alpha=0.5)
        return h_out, (n, g, u, a)

    def ffn_bwd(h, norm_g, saved, dh, dh_half, tag, pre, last):
        n, g, u, a = saved
        G[pre + "_w_down"] = mm(a, dh_half, name=f"{tag}_dw_down", ta=True, out_dtype=BF16)
        da = mm(dh_half, plain(pre + "_w_down"), name=f"{tag}_da", tb=True)
        dg, du = swiglu_bwd(g, u, da, name=f"{tag}_dact")
        G[pre + "_w_gate"] = mm(n, dg, name=f"{tag}_dw_gate", ta=True, out_dtype=BF16, out_stack=True)
        G[pre + "_w_up"] = mm(n, du, name=f"{tag}_dw_up", ta=True, out_dtype=BF16, out_stack=True)
        reduce_start([pre + "_w_down", pre + "_w_gate", pre + "_w_up"], tag)
        dn = mm(dg, stacked(pre + "_w_gate"), name=f"{tag}_dn_gate", tb=True, b_stack=True)
        dn = mm(du, stacked(pre + "_w_up"), name=f"{tag}_dn_up", tb=True, b_stack=True, res=dn)
        return norm_bwd(h, norm_g, dn, name=f"{tag}_dnorm", res=dh, bf16_alpha=None if last else 1.0)

    h1, ffn1_saved = ffn_fwd(xs, ffn1_norm, "ffn1", "ffn1")
    u_mix = norm_fwd(h1, mix_norm, name="mix_norm")
    win = plain("w_in", u_mix)
    win_l = jnp.concatenate([win[:, :n_front], win[:, n_front + MLA_ROPE:], win[:, n_front:n_front + MLA_ROPE],
                             jnp.zeros((D, LANES - MLA_ROPE), BF16)], axis=1)
    z = mm(u_mix, win_l, name="mix_in")
    bias = na_bias(na_rpb[0])
    o_a = na_fwd(z, bias, NH, S)
    c_q = norm_fwd((z, QR, 3 * NAW // QR), q_a_norm, name="q_a_norm")
    c_kv = norm_fwd((z, KVR, (3 * NAW + QR) // KVR), kv_a_norm, name="kv_a_norm")
    wuq = plain("w_uq", c_kv).reshape(QR, MH, MLA_QK)
    wuq_n = wuq[:, :, :MLA_NOPE].reshape(QR, MH * MLA_NOPE)
    wuq_r = jnp.pad(wuq[:, :, MLA_NOPE:], ((0, 0), (0, 0), (0, LANES - MLA_ROPE))).reshape(QR, MH * LANES)
    wukv = plain("w_ukv").reshape(KVR, MH, 2, HEAD_DIM)
    wuk = wukv[:, :, 0].reshape(KVR, MH * HEAD_DIM)
    wuv = wukv[:, :, 1].reshape(KVR, MH * HEAD_DIM)
    q_n = mm(c_q, wuq_n, name="mla_q_nope", out_dtype=BF16)
    q_r = rope(mm(c_q, wuq_r, name="mla_q_rope"), cos_t, sin_t, name="rope_q", out_dtype=BF16)
    k_n = mm(c_kv, wuk, name="mla_k_nope", out_dtype=BF16)
    v_m = mm(c_kv, wuv, name="mla_v", out_dtype=BF16)
    k_r = rope((z, LANES, off_kr // LANES), cos_t, sin_t, name="rope_k", out_dtype=BF16)
    o_b, lse = mla_fwd(q_n, q_r, k_n, v_m, k_r, MH, S)
    y_a = mm(o_a, stacked("w_branch_a"), name="branch_a", b_stack=True)
    y_b = mm(o_b, stacked("w_branch_b"), name="branch_b", b_stack=True)
    z_ga, z_gb = (z, D, off_ga // D), (z, D, off_ga // D + 1)
    merged = rowwise(lambda ga, gb, ya, yb: _sig(ga) * ya + _sig(gb) * yb, [z_ga, z_gb, y_a, y_b], [], [(D, BF16)],
                     tm=256, name="merge")[0]
    h2 = mm(merged, plain("w_out"), name="mix_out", res=h1)
    h3, ffn2_saved = ffn_fwd(h2, ffn2_norm, "ffn2", "ffn2")
    n4 = norm_fwd(h3, pl_norm, name="pl_norm")
    pg_pre = mm(n4, plain("w_pl_gate", n4), name="pl_gate")
    pe = mm(ps, stacked("w_pl"), name="pl_embed", b_stack=True)

    def tail(h3b, pgb, peb, tb_, fg):
        pg = _sig(pgb)
        h4 = h3b + pg * peb
        r = _rstd(h4)
        xh = h4 * r
        err = xh * fg - tb_
        loss_rows = jnp.mean(err * err, axis=-1, keepdims=True)
        dy = err * (1.0 / D)
        dxh = dy * fg
        dh4 = r * (dxh - xh * jnp.mean(dxh * xh, axis=-1, keepdims=True))
        loss_part = jnp.broadcast_to(0.5 * jnp.sum(loss_rows, axis=0, keepdims=True), (1, LANES))
        return (dh4, dh4 * peb * pg * (1.0 - pg), dh4 * pg, loss_part, jnp.sum(dy * xh, axis=0, keepdims=True))

    dh4, dpg_pre, dpe, loss_part, g_final = rowwise(
        tail, [h3, pg_pre, pe, tgt], [final_norm.reshape(1, D)], [(D, F32), (D, BF16), (D, BF16)],
        accs=[(1, LANES), (1, D)], tm=128, name="loss_tail")
    loss = lax.psum(loss_part[0, 0], ("x", "y", "c"))

    G = {}
    pending = []

    def four(g):
        if g.ndim == 2:
            return g.reshape(N_CHIPS, 2, g.shape[0] // (2 * N_CHIPS), g.shape[1])
        return g.reshape(N_CHIPS, 2, g.shape[1] // 2, g.shape[2])

    def reduce_start(names, tag):
        g4 = [four(G[n]) for n in names]
        sums = [pair_sum(a, b, c_idx, n) for n, a, b in zip(names, g4, pair_exchange(g4, tag))]
        lands = [lax.empty((N_CHIPS - 1,) + s_.shape[1:], BF16) for s_ in sums]
        pending.append((names, tag) + scatter_start(sums, lands, tag))

    def reduce_finish(entry, after):
        names, tag, sems, sums, lands = entry
        sums, got = scatter_wait(sems, sums, lands, after, tag)
        halves = [chip_sum(a, b, j_idx, n) for n, a, b in zip(names, sums, got)]
        out = None
        for n, own, sib in zip(names, halves, half_exchange(halves, tag)):
            shp = W[n].shape
            two_d = lambda a_: a_.reshape(shp[1], shp[2])
            out = adamw_shard(two_d(W[n]), own, sib, two_d(Mo[n]), two_d(Vo[n]), c_idx, name="adamw_" + n)
            grads[n], delta[n], new_m[n], new_v[n] = [o.reshape(shp) for o in out]
        return out[0]

    G["w_pl"] = mm(ps, dpe, name="pl_dw_embed", ta=True, out_dtype=BF16, out_stack=True)
    G["w_pl_gate"] = mm(n4, dpg_pre, name="pl_dw_gate", ta=True, out_dtype=BF16)
    dn4 = mm(dpg_pre, plain("w_pl_gate"), name="pl_dn", tb=True)
    dh3, dh3_half, g_pl = norm_bwd(h3, pl_norm, dn4, name="pl_dnorm", res=dh4, bf16_alpha=0.5)
    reduce_start(["w_pl", "w_pl_gate"], "pl")
    dh2, dh2_b, g_ffn2 = ffn_bwd(h2, ffn2_norm, ffn2_saved, dh3, dh3_half, "ffn2", "ffn2", last=False)

    G["w_out"] = mm(merged, dh2_b, name="mix_dw_out", ta=True, out_dtype=BF16)
    dmerged = mm(dh2_b, plain("w_out"), name="mix_dmerged", tb=True)

    def merge_bwd(ga, gb, ya, yb, dm):
        sa, sb = _sig(ga), _sig(gb)
        return dm * sa, dm * sb, dm * ya * sa * (1.0 - sa), dm * yb * sb * (1.0 - sb)

    dy_a, dy_b, dga, dgb = rowwise(merge_bwd, [z_ga, z_gb, y_a, y_b, dmerged], [], [(D, BF16)] * 4, tm=256,
                                   name="merge_bwd")
    G["w_branch_a"] = mm(o_a, dy_a, name="branch_a_dw", ta=True, out_dtype=BF16, out_stack=True)
    G["w_branch_b"] = mm(o_b, dy_b, name="branch_b_dw", ta=True, out_dtype=BF16, out_stack=True)
    do_a = mm(dy_a, stacked("w_branch_a"), name="branch_a_dx", tb=True, b_stack=True)
    do_b = mm(dy_b, stacked("w_branch_b"), name="branch_b_dx", tb=True, b_stack=True)
    dq_na, dk_na, dv_na, dbias = na_bwd(z, bias, do_a, NH, S)
    g_rpb = na_rpb_grad(dbias)
    dq_n, dq_rr, dk_n, dv_m, dk_rr = mla_bwd(q_n, q_r, k_n, v_m, k_r, lse, do_b, MH, S)
    dq_r = rope(dq_rr, cos_t, -sin_t, name="rope_q_bwd", out_dtype=BF16)
    dk_r = rope(dk_rr, cos_t, -sin_t, name="rope_k_bwd", out_dtype=BF16)
    gw_uq_n = mm(c_q, dq_n, name="mla_dw_q_nope", ta=True, out_dtype=BF16)
    gw_uq_r = mm(c_q, dq_r, name="mla_dw_q_rope", ta=True, out_dtype=BF16)
    dc_q = mm(dq_n, wuq_n, name="mla_dcq_nope", tb=True)
    dc_q = mm(dq_r, wuq_r, name="mla_dcq_rope", tb=True, res=dc_q)
    gw_uk = mm(c_kv, dk_n, name="mla_dw_k", ta=True, out_dtype=BF16)
    gw_uv = mm(c_kv, dv_m, name="mla_dw_v", ta=True, out_dtype=BF16)
    dc_kv = mm(dk_n, wuk, name="mla_dckv_k", tb=True)
    dc_kv = mm(dv_m, wuv, name="mla_dckv_v", tb=True, res=dc_kv)
    dq_lat, g_qa = norm_bwd((z, QR, 3 * NAW // QR), q_a_norm, dc_q, name="q_a_dnorm", want_f32=False, bf16_alpha=1.0)
    dkv_lat, g_kva = norm_bwd((z, KVR, (3 * NAW + QR) // KVR), kv_a_norm, dc_kv, name="kv_a_dnorm", want_f32=False,
                              bf16_alpha=1.0)
    dz = jnp.concatenate([dq_na, dk_na.astype(BF16), dv_na.astype(BF16), dq_lat, dkv_lat, dga, dgb, dk_r], axis=1)
    gw_in_l = mm(u_mix, dz, name="mix_dw_in", ta=True, out_dtype=BF16)

    def to_stack(g2d):
        k, n = g2d.shape
        return g2d.reshape(k, N_CHIPS, n // N_CHIPS).transpose(1, 0, 2)

    gw_uq = jnp.concatenate([gw_uq_n.reshape(QR, MH, MLA_NOPE), gw_uq_r.reshape(QR, MH, LANES)[:, :, :MLA_ROPE]],
                            axis=2).reshape(QR, MH * MLA_QK)
    G["w_uq"] = to_stack(gw_uq)
    gw_ukv = jnp.stack([gw_uk.reshape(KVR, MH, HEAD_DIM), gw_uv.reshape(KVR, MH, HEAD_DIM)], axis=2)
    G["w_ukv"] = to_stack(gw_ukv.reshape(KVR, MH * 2 * HEAD_DIM))
    reduce_start(["w_out", "w_branch_a", "w_branch_b", "w_uq", "w_ukv"], "mix")
    gw_in = jnp.concatenate([gw_in_l[:, :n_front], gw_in_l[:, off_kr:off_kr + MLA_ROPE], gw_in_l[:, n_front:off_kr]],
                            axis=1)
    G["w_in"] = to_stack(gw_in)
    reduce_start(["w_in"], "win")
    du_mix = mm(dz, win_l, name="mix_du", tb=True)
    dh1, dh1_half, g_mix = norm_bwd(h1, mix_norm, du_mix, name="mix_dnorm", res=dh2, bf16_alpha=0.5)
    grad_x, g_ffn1 = ffn_bwd(xs, ffn1_norm, ffn1_saved, dh1, dh1_half, "ffn1", "ffn1", last=True)

    small_g = {"ffn1_norm": g_ffn1, "mix_norm": g_mix, "q_a_norm": g_qa, "kv_a_norm": g_kva, "na_rpb": g_rpb,
               "ffn2_norm": g_ffn2, "pl_norm": g_pl, "final_norm": g_final}
    sizes = [int(np.prod(W[n].shape)) for n in small]
    total = sum(sizes)
    padded = -(-total // (8 * LANES)) * (8 * LANES)

    def pack(parts):
        flat = jnp.concatenate([jnp.reshape(parts[n], (-1,)).astype(F32) for n in small]
                               + [jnp.zeros((padded - total,), F32)])
        return flat.reshape(padded // LANES, LANES)

    def unpack(a):
        flat, out, o = a.reshape(-1), {}, 0
        for n, sz in zip(small, sizes):
            out[n] = flat[o:o + sz].reshape(W[n].shape)
            o += sz
        return out

    g_small = sum_devices(gather_small(pack(small_g)))
    d_small, m_small, v_small = adamw(pack(W), g_small, pack(Mo), pack(Vo), name="adamw_small")
    grads = unpack(g_small)
    delta, new_m, new_v = unpack(d_small), unpack(m_small), unpack(v_small)

    after = grad_x
    for entry in pending:
        after = reduce_finish(entry, after)

    return (loss, grad_x[None], *[grads[n] for n in order], *[delta[n] for n in order],
            *[new_m[n] for n in order], *[new_v[n] for n in order])
```

```python
import functools

import numpy as np
import jax
import jax.numpy as jnp
from jax import lax
from jax.experimental import pallas as pl
from jax.experimental.pallas import tpu as pltpu

F32 = jnp.float32
BF16 = jnp.bfloat16

VMEM_LIMIT_V7X = 56 * 1024 * 1024
VMEM_BUDGET_V7X = 40 * 1024 * 1024
LANES = 128

GRID_W = 64
NA_WIN_ROWS = 8
NA_WIN_COLS = 16
HEAD_DIM = 128
MLA_NOPE = 128
MLA_ROPE = 64
MLA_QK = MLA_NOPE + MLA_ROPE
ROPE_THETA = 10000.0
NORM_EPS = 1e-6
NEG_INF = -1e30
N_CHIPS = 4

ADAM_LR = 0.001
ADAM_B1 = 0.9
ADAM_B2 = 0.999
ADAM_EPS = 1e-08
ADAM_WD = 0.01
ADAM_STEP = 10

MESH = pl.DeviceIdType.MESH
ANY = pl.BlockSpec(memory_space=pl.ANY)


def _params(sem=None):
    return pltpu.CompilerParams(dimension_semantics=sem, vmem_limit_bytes=VMEM_LIMIT_V7X)


def _pick(n, target, align):
    best = None
    t = align
    while t <= min(n, target):
        if n % t == 0:
            best = t
        t += align
    return n if best is None else best


def mm(a, b, *, name, ta=False, tb=False, out_dtype=F32, res=None, alpha=1.0, b_stack=False, out_stack=False,
       exact=False, after=()):
    K, M = (a.shape if ta else a.shape[::-1])
    if b_stack:
        nst = b.shape[0]
        if tb:
            N, kb = b.shape[1], b.shape[2]
            Kb, nb = nst * kb, None
        else:
            Kb, nb = b.shape[1], b.shape[2]
            N = nst * nb
    else:
        N, Kb = (b.shape if tb else b.shape[::-1])
    assert K == Kb, (a.shape, b.shape, ta, tb)
    if out_stack:
        assert N % N_CHIPS == 0
    n_unit = N // N_CHIPS if out_stack else (nb if (b_stack and not tb) else N)
    k_unit = kb if (b_stack and tb) else K
    tn = _pick(n_unit, 512, LANES) if n_unit % 512 == 0 or n_unit <= 512 else _pick(n_unit, 1536, LANES)
    tk = _pick(k_unit, 2048, LANES)
    tm = _pick(M, 1024, LANES if ta else 16)
    isz = lambda t: jnp.dtype(t.dtype).itemsize
    osz = jnp.dtype(out_dtype).itemsize

    def vmem(tm_):
        return (2 * tm_ * tk * isz(a) + 2 * tk * tn * isz(b) + 2 * tm_ * tn * osz + tm_ * tn * 4
                + (2 * tm_ * tn * isz(res) if res is not None else 0))

    while vmem(tm) > VMEM_BUDGET_V7X and tm % 2 == 0 and (tm // 2) % (LANES if ta else 16) == 0:
        tm //= 2
    nk = K // tk
    gm, gn = M // tm, N // tn

    a_spec = pl.BlockSpec((tk, tm), lambda i, j, k: (k, i)) if ta else pl.BlockSpec((tm, tk), lambda i, j, k: (i, k))
    if b_stack and not tb:
        q = nb // tn
        b_spec = pl.BlockSpec((None, tk, tn), lambda i, j, k: (j // q, k, j % q))
    elif b_stack and tb:
        q = kb // tk
        b_spec = pl.BlockSpec((None, tn, tk), lambda i, j, k: (k // q, j, k % q))
    elif tb:
        b_spec = pl.BlockSpec((tn, tk), lambda i, j, k: (j, k))
    else:
        b_spec = pl.BlockSpec((tk, tn), lambda i, j, k: (k, j))
    if out_stack:
        qo = (N // N_CHIPS) // tn
        o_spec = pl.BlockSpec((None, tm, tn), lambda i, j, k: (j // qo, i, j % qo))
        o_shape = jax.ShapeDtypeStruct((N_CHIPS, M, N // N_CHIPS), out_dtype)
    else:
        o_spec = pl.BlockSpec((tm, tn), lambda i, j, k: (i, j))
        o_shape = jax.ShapeDtypeStruct((M, N), out_dtype)
    dims = (((0 if ta else 1,), (1 if tb else 0,)), ((), ()))
    has_res = res is not None

    def body(*refs):
        a_ref, b_ref = refs[:2]
        r_ref = refs[2] if has_res else None
        o_ref, acc_ref = refs[-2:]
        k = pl.program_id(2)
        if exact:
            part = lax.dot_general(a_ref[...], b_ref[...], dims, preferred_element_type=F32,
                                   precision=lax.Precision.HIGHEST)
        else:
            part = lax.dot_general(a_ref[...].astype(BF16), b_ref[...].astype(BF16), dims,
                                   preferred_element_type=F32)

        def finish(total):
            if alpha != 1.0:
                total = total * alpha
            if has_res:
                total = total + r_ref[...].astype(F32)
            o_ref[...] = total.astype(out_dtype)

        if nk == 1:
            finish(part)
        else:
            @pl.when(k == 0)
            def _():
                acc_ref[...] = part

            @pl.when(jnp.logical_and(k > 0, k < nk - 1))
            def _():
                acc_ref[...] += part

            @pl.when(k == nk - 1)
            def _():
                finish(acc_ref[...] + part)

    in_specs = [a_spec, b_spec]
    args = [a, b]
    if has_res:
        in_specs.append(pl.BlockSpec((tm, tn), lambda i, j, k: (i, j)))
        args.append(res)
    in_specs += [ANY] * len(after)
    args += list(after)
    return pl.pallas_call(
        body, name=name, grid=(gm, gn, nk), in_specs=in_specs, out_specs=o_spec, out_shape=o_shape,
        scratch_shapes=[pltpu.VMEM((tm, tn) if nk > 1 else (8, LANES), F32)],
        compiler_params=_params(("parallel", "parallel", "arbitrary")),
    )(*args)


def rowwise(fn, rows, consts, outs, accs=(), *, tm, name, tn=None):
    rows = [r if isinstance(r, tuple) else (r, r.shape[1], 0) for r in rows]
    S = rows[0][0].shape[0]
    tm = _pick(S, tm, 16)
    nrow, ncon, nout = len(rows), len(consts), len(outs)
    if tn is None:
        grid = (S // tm,)
        in_specs = [pl.BlockSpec((tm, w), functools.partial(lambda i, cb: (i, cb), cb=cb)) for _, w, cb in rows]
        in_specs += [pl.BlockSpec(c.shape, lambda i: (0, 0)) for c in consts]
        out_specs = [pl.BlockSpec((tm, n), lambda i: (i, 0)) for n, _ in outs]
        out_specs += [pl.BlockSpec(s, lambda i: (0, 0)) for s in accs]
        sem = ("arbitrary",)
    else:
        assert not accs
        N = rows[0][1]
        grid = (S // tm, N // tn)
        in_specs = [pl.BlockSpec((tm, tn), lambda i, j: (i, j)) for _ in rows]
        in_specs += [pl.BlockSpec(c.shape, lambda i, j: (0, 0)) for c in consts]
        out_specs = [pl.BlockSpec((tm, tn), lambda i, j: (i, j)) for _ in outs]
        sem = ("parallel", "parallel")
    out_shape = [jax.ShapeDtypeStruct((S, n), dt) for n, dt in outs]
    out_shape += [jax.ShapeDtypeStruct(s, F32) for s in accs]

    def body(*refs):
        vals = fn(*[r[...] for r in refs[:nrow + ncon]])
        if not isinstance(vals, (tuple, list)):
            vals = (vals,)
        o_refs = refs[nrow + ncon:]
        for o_ref, v in zip(o_refs[:nout], vals[:nout]):
            o_ref[...] = v.astype(o_ref.dtype)
        if accs:
            first = pl.program_id(0) == 0

            def accumulate(a_ref, v):
                @pl.when(first)
                def _():
                    a_ref[...] = v

                @pl.when(jnp.logical_not(first))
                def _():
                    a_ref[...] += v

            for a_ref, v in zip(o_refs[nout:], vals[nout:]):
                accumulate(a_ref, v.astype(F32))

    res = pl.pallas_call(
        body, name=name, grid=grid, in_specs=in_specs, out_specs=out_specs, out_shape=out_shape,
        compiler_params=_params(sem),
    )(*[r[0] for r in rows], *consts)
    return res


def _rstd(x):
    return lax.rsqrt(jnp.mean(x * x, axis=-1, keepdims=True) + NORM_EPS)


def norm_fwd(x, g, *, name, tm=256):
    w = x[1] if isinstance(x, tuple) else x.shape[1]

    def fn(xb, gb):
        return (xb * _rstd(xb)) * gb

    return rowwise(fn, [x], [g], [(w, BF16)], tm=tm, name=name)[0]


def norm_bwd(x, g, dn, *, name, res=None, want_f32=True, bf16_alpha=None, tm=256):
    w = x[1] if isinstance(x, tuple) else x.shape[1]
    has_res = res is not None

    def fn(*blocks):
        if has_res:
            xb, dnb, rb, gb = blocks
        else:
            xb, dnb, gb = blocks
        r = _rstd(xb)
        xh = xb * r
        dxh = dnb * gb
        dx = r * (dxh - xh * jnp.mean(dxh * xh, axis=-1, keepdims=True))
        if has_res:
            dx = dx + rb
        out = []
        if want_f32:
            out.append(dx)
        if bf16_alpha is not None:
            out.append(dx * bf16_alpha if bf16_alpha != 1.0 else dx)
        out.append(jnp.sum(dnb * xh, axis=0, keepdims=True))
        return tuple(out)

    outs = ([(w, F32)] if want_f32 else []) + ([(w, BF16)] if bf16_alpha is not None else [])
    rows = [x, dn] + ([res] if has_res else [])
    return rowwise(fn, rows, [g], outs, accs=[(1, w)], tm=tm, name=name)


def _sig(x):
    return jax.nn.sigmoid(x)


def swiglu_fwd(g, u, *, name):
    return rowwise(lambda gb, ub: gb * _sig(gb) * ub, [g, u], [], [(g.shape[1], BF16)], tm=256, name=name,
                   tn=_pick(g.shape[1], 1536, LANES))[0]


def swiglu_bwd(g, u, da, *, name):
    def fn(gb, ub, dab):
        s = _sig(gb)
        return dab * ub * (s + gb * s * (1.0 - s)), dab * (gb * s)

    n = g.shape[1]
    return rowwise(fn, [g, u, da], [], [(n, BF16), (n, BF16)], tm=256, name=name, tn=_pick(n, 1536, LANES))


def rope(x, cos, sin_signed, *, name, out_dtype):
    w = x[1] if isinstance(x, tuple) else x.shape[1]
    half = MLA_ROPE // 2

    def fn(xb, cb, sb):
        lane = lax.broadcasted_iota(jnp.int32, cb.shape, 1)
        outs = []
        for hb in range(w // LANES):
            blk = xb[:, hb * LANES:(hb + 1) * LANES]
            partner = jnp.where(lane < half, pltpu.roll(blk, LANES - half, 1), pltpu.roll(blk, half, 1))
            outs.append(blk * cb + partner * sb)
        return outs[0] if len(outs) == 1 else jnp.concatenate(outs, axis=1)

    return rowwise(fn, [x, cos, sin_signed], [], [(w, out_dtype)], tm=256, name=name)[0]


def _na_tables():
    cols = np.arange(GRID_W)
    kw = NA_WIN_COLS
    dc = np.clip(cols[None, :] - cols[:, None], -(kw - 1), kw - 1) + (kw - 1)
    onehot = np.zeros((LANES, GRID_W * GRID_W), np.float32)
    onehot[dc.reshape(-1), np.arange(GRID_W * GRID_W)] = 1.0
    col_start = np.clip(cols - kw // 2, 0, GRID_W - kw)
    mask = (cols[None, :] >= col_start[:, None]) & (cols[None, :] < col_start[:, None] + kw)
    return onehot, np.where(mask, 0.0, NEG_INF).astype(np.float32)


def na_bias(rpb, after=()):
    H = rpb.shape[0]
    nr, kh = 2 * NA_WIN_ROWS - 1, NA_WIN_ROWS
    onehot, maskb = _na_tables()
    rp = jnp.pad(rpb.reshape(H * nr, 2 * NA_WIN_COLS - 1), ((0, 0), (0, LANES - (2 * NA_WIN_COLS - 1))))
    t1 = mm(rp, jnp.asarray(onehot), name="na_bias_table", exact=True, after=after).reshape(H, nr, GRID_W, GRID_W)
    t1 = t1 + jnp.asarray(maskb)[None, None]
    per_t = [jnp.stack([t1[:, i - t + kh - 1] for i in range(kh)], axis=2) for t in range(kh)]
    return jnp.stack(per_t, axis=1).reshape(H, kh, GRID_W, kh * GRID_W)


def na_rpb_grad(db):
    H = db.shape[0]
    nr, kh = 2 * NA_WIN_ROWS - 1, NA_WIN_ROWS
    onehot, _ = _na_tables()
    db = db.reshape(H, kh, GRID_W, kh, GRID_W)
    per_dr = []
    for dri in range(nr):
        terms = [db[:, t, :, dri - (kh - 1) + t, :] for t in range(kh) if 0 <= dri - (kh - 1) + t < kh]
        per_dr.append(functools.reduce(jnp.add, terms))
    dt1 = jnp.stack(per_dr, axis=1).reshape(H * nr, GRID_W * GRID_W)
    g = mm(dt1, jnp.asarray(onehot), name="na_rpb_grad", tb=True, exact=True)
    return g[:, :2 * NA_WIN_COLS - 1].reshape(H, nr, 2 * NA_WIN_COLS - 1)


def _na_first_row(r, rows):
    return jnp.clip(r - NA_WIN_ROWS // 2, 0, rows - NA_WIN_ROWS)


def _na_scores(q_ref, k_ref, b_ref, start):
    q = q_ref[...].astype(BF16)
    k = k_ref[pl.ds(start, NA_WIN_ROWS * GRID_W), :].astype(BF16)
    s = lax.dot_general(q, k, (((1,), (1,)), ((), ())), preferred_element_type=F32)
    s = s * (HEAD_DIM ** -0.5) + b_ref[...]
    m = jnp.max(s, axis=-1, keepdims=True)
    e = jnp.exp(s - m)
    return q, k, e / jnp.sum(e, axis=-1, keepdims=True)


def na_fwd(z, bias, H, S):
    rows = S // GRID_W
    nkeys = NA_WIN_ROWS * GRID_W

    def body(q_ref, k_ref, v_ref, b_ref, o_ref):
        r = pl.program_id(1)
        start = pl.multiple_of(_na_first_row(r, rows) * GRID_W, GRID_W)
        _, _, p = _na_scores(q_ref, k_ref, b_ref, start)
        v = v_ref[pl.ds(start, nkeys), :].astype(BF16)
        o_ref[...] = jnp.dot(p.astype(BF16), v, preferred_element_type=F32).astype(o_ref.dtype)

    return pl.pallas_call(
        body, name="na_fwd", grid=(H, rows),
        in_specs=[pl.BlockSpec((GRID_W, HEAD_DIM), lambda h, r: (r, h)),
                  pl.BlockSpec((S, HEAD_DIM), lambda h, r: (0, H + h)),
                  pl.BlockSpec((S, HEAD_DIM), lambda h, r: (0, 2 * H + h)),
                  pl.BlockSpec((None, None, GRID_W, nkeys), lambda h, r: (h, r - _na_first_row(r, rows), 0, 0))],
        out_specs=pl.BlockSpec((GRID_W, HEAD_DIM), lambda h, r: (r, h)),
        out_shape=jax.ShapeDtypeStruct((S, H * HEAD_DIM), BF16),
        compiler_params=_params(("parallel", "arbitrary")),
    )(z, z, z, bias)


def na_bwd(z, bias, do, H, S):
    rows = S // GRID_W
    nkeys = NA_WIN_ROWS * GRID_W
    tn_dims = (((0,), (0,)), ((), ()))

    def body(q_ref, k_ref, v_ref, b_ref, do_ref, dq_ref, dk_ref, dv_ref, db_ref):
        r = pl.program_id(1)
        start = pl.multiple_of(_na_first_row(r, rows) * GRID_W, GRID_W)
        q, k, p = _na_scores(q_ref, k_ref, b_ref, start)
        v = v_ref[pl.ds(start, nkeys), :].astype(BF16)
        dob = do_ref[...].astype(BF16)
        dp = lax.dot_general(dob, v, (((1,), (1,)), ((), ())), preferred_element_type=F32)
        ds = p * (dp - jnp.sum(dp * p, axis=-1, keepdims=True))
        dsb = (ds * (HEAD_DIM ** -0.5)).astype(BF16)
        dq_ref[...] = jnp.dot(dsb, k, preferred_element_type=F32).astype(dq_ref.dtype)

        @pl.when(r == 0)
        def _():
            dk_ref[...] = jnp.zeros_like(dk_ref)
            dv_ref[...] = jnp.zeros_like(dv_ref)

        dk_ref[pl.ds(start, nkeys), :] += lax.dot_general(dsb, q, tn_dims, preferred_element_type=F32)
        dv_ref[pl.ds(start, nkeys), :] += lax.dot_general(p.astype(BF16), dob, tn_dims, preferred_element_type=F32)

        fresh = jnp.logical_or(r <= NA_WIN_ROWS // 2, r > rows - NA_WIN_ROWS // 2)

        @pl.when(fresh)
        def _():
            db_ref[...] = ds

        @pl.when(jnp.logical_not(fresh))
        def _():
            db_ref[...] += ds

    W = H * HEAD_DIM
    return pl.pallas_call(
        body, name="na_bwd", grid=(H, rows),
        in_specs=[pl.BlockSpec((GRID_W, HEAD_DIM), lambda h, r: (r, h)),
                  pl.BlockSpec((S, HEAD_DIM), lambda h, r: (0, H + h)),
                  pl.BlockSpec((S, HEAD_DIM), lambda h, r: (0, 2 * H + h)),
                  pl.BlockSpec((None, None, GRID_W, nkeys), lambda h, r: (h, r - _na_first_row(r, rows), 0, 0)),
                  pl.BlockSpec((GRID_W, HEAD_DIM), lambda h, r: (r, h))],
        out_specs=[pl.BlockSpec((GRID_W, HEAD_DIM), lambda h, r: (r, h)),
                   pl.BlockSpec((S, HEAD_DIM), lambda h, r: (0, h)),
                   pl.BlockSpec((S, HEAD_DIM), lambda h, r: (0, h)),
                   pl.BlockSpec((None, None, GRID_W, nkeys), lambda h, r: (h, r - _na_first_row(r, rows), 0, 0))],
        out_shape=[jax.ShapeDtypeStruct((S, W), BF16), jax.ShapeDtypeStruct((S, W), F32),
                   jax.ShapeDtypeStruct((S, W), F32), jax.ShapeDtypeStruct((H, NA_WIN_ROWS, GRID_W, nkeys), F32)],
        compiler_params=_params(("arbitrary", "arbitrary")),
    )(z, z, z, bias, do)


def _mla_scores(qn_ref, qr_ref, kn_ref, kr_ref):
    nt = (((1,), (1,)), ((), ()))
    s = lax.dot_general(qn_ref[...], kn_ref[...], nt, preferred_element_type=F32)
    s = s + lax.dot_general(qr_ref[...], kr_ref[...], nt, preferred_element_type=F32)
    return s * (MLA_QK ** -0.5)


def mla_fwd(qn, qr, kn, v, kr, H, S):
    tq = _pick(S, 256, 16)

    def body(qn_ref, qr_ref, kn_ref, v_ref, kr_ref, o_ref, lse_ref):
        s = _mla_scores(qn_ref, qr_ref, kn_ref, kr_ref)
        m = jnp.max(s, axis=-1, keepdims=True)
        e = jnp.exp(s - m)
        l = jnp.sum(e, axis=-1, keepdims=True)
        o_ref[...] = jnp.dot((e / l).astype(BF16), v_ref[...], preferred_element_type=F32).astype(o_ref.dtype)
        lse_ref[...] = jnp.broadcast_to(m + jnp.log(l), lse_ref.shape)

    qspec = pl.BlockSpec((tq, HEAD_DIM), lambda h, i: (i, h))
    kspec = pl.BlockSpec((S, HEAD_DIM), lambda h, i: (0, h))
    return pl.pallas_call(
        body, name="mla_fwd", grid=(H, S // tq),
        in_specs=[qspec, qspec, kspec, kspec, pl.BlockSpec((S, LANES), lambda h, i: (0, 0))],
        out_specs=[qspec, qspec],
        out_shape=[jax.ShapeDtypeStruct((S, H * HEAD_DIM), BF16), jax.ShapeDtypeStruct((S, H * LANES), F32)],
        compiler_params=_params(("parallel", "arbitrary")),
    )(qn, qr, kn, v, kr)


def mla_bwd(qn, qr, kn, v, kr, lse, do, H, S):
    tq = _pick(S, 256, 16)
    nt = (((1,), (1,)), ((), ()))
    tn_dims = (((0,), (0,)), ((), ()))

    def body(qn_ref, qr_ref, kn_ref, v_ref, kr_ref, lse_ref, do_ref, dqn_ref, dqr_ref, dkn_ref, dv_ref, dkr_ref):
        h, i = pl.program_id(0), pl.program_id(1)
        s = _mla_scores(qn_ref, qr_ref, kn_ref, kr_ref)
        p = jnp.exp(s - lse_ref[:, 0:1])
        dob = do_ref[...].astype(BF16)
        dp = lax.dot_general(dob, v_ref[...], nt, preferred_element_type=F32)
        ds = p * (dp - jnp.sum(dp * p, axis=-1, keepdims=True))
        dsb = (ds * (MLA_QK ** -0.5)).astype(BF16)
        dqn_ref[...] = jnp.dot(dsb, kn_ref[...], preferred_element_type=F32).astype(dqn_ref.dtype)
        dqr_ref[...] = jnp.dot(dsb, kr_ref[...], preferred_element_type=F32).astype(dqr_ref.dtype)

        @pl.when(i == 0)
        def _():
            dkn_ref[...] = jnp.zeros_like(dkn_ref)
            dv_ref[...] = jnp.zeros_like(dv_ref)

        @pl.when(jnp.logical_and(i == 0, h == 0))
        def _():
            dkr_ref[...] = jnp.zeros_like(dkr_ref)

        dkn_ref[...] += lax.dot_general(dsb, qn_ref[...], tn_dims, preferred_element_type=F32)
        dkr_ref[...] += lax.dot_general(dsb, qr_ref[...], tn_dims, preferred_element_type=F32)
        dv_ref[...] += lax.dot_general(p.astype(BF16), dob, tn_dims, preferred_element_type=F32)

    qspec = pl.BlockSpec((tq, HEAD_DIM), lambda h, i: (i, h))
    kspec = pl.BlockSpec((S, HEAD_DIM), lambda h, i: (0, h))
    rspec = pl.BlockSpec((S, LANES), lambda h, i: (0, 0))
    W = H * HEAD_DIM
    return pl.pallas_call(
        body, name="mla_bwd", grid=(H, S // tq),
        in_specs=[qspec, qspec, kspec, kspec, rspec, qspec, qspec],
        out_specs=[qspec, qspec, kspec, kspec, rspec],
        out_shape=[jax.ShapeDtypeStruct((S, W), BF16), jax.ShapeDtypeStruct((S, W), F32),
                   jax.ShapeDtypeStruct((S, W), F32), jax.ShapeDtypeStruct((S, W), F32),
                   jax.ShapeDtypeStruct((S, LANES), F32)],
        compiler_params=_params(("arbitrary", "arbitrary")),
    )(qn, qr, kn, v, kr, lse, do)


def _place():
    return lax.axis_index("x"), lax.axis_index("y"), lax.axis_index("c")


def _other_chips(x, y):
    return [(1 - x, y), (x, 1 - y), (1 - x, 1 - y)]


def _remote(src, dst, send_sem, recv_sem, to):
    return pltpu.make_async_remote_copy(src_ref=src, dst_ref=dst, send_sem=send_sem, recv_sem=recv_sem,
                                        device_id=to, device_id_type=MESH)


HBM = pl.BlockSpec(memory_space=pltpu.HBM)
SEM = pl.BlockSpec(memory_space=pltpu.SEMAPHORE)
EFFECT = pltpu.SideEffectType.DATAFLOW_SIDE_EFFECTING


def _in_hbm(a):
    return pltpu.with_memory_space_constraint(a, pltpu.HBM)


TOKEN = jax.ShapeDtypeStruct((8, LANES), F32)
IN_VMEM = pl.BlockSpec(memory_space=pltpu.VMEM)


def gather_start(shards, landings, after, tag):
    n = len(shards)

    def body(*refs):
        ins, lands = refs[:n], refs[n:2 * n]
        send, recv = refs[2 * n + len(after)], refs[2 * n + len(after) + 1]
        token = refs[-1]
        x, y, c = _place()
        me = 2 * x + y
        for w in range(n):
            for k, (px, py) in enumerate(_other_chips(x, y)):
                _remote(ins[w].at[c], lands[w].at[me, c], send.at[3 * w + k], recv.at[3 * w + k], (px, py, c)).start()
        token[...] = jnp.zeros_like(token)

    bufs = list(shards) + list(landings)
    outs = pl.pallas_call(
        body, name="gather_start_" + tag,
        out_shape=(pltpu.SemaphoreType.DMA((3 * n,)),) * 2 + tuple(pltpu.HBM(b.shape, b.dtype) for b in bufs) + (TOKEN,),
        in_specs=[HBM] * (2 * n) + [ANY] * len(after), out_specs=tuple([SEM, SEM] + [HBM] * (2 * n) + [IN_VMEM]),
        input_output_aliases={i: 2 + i for i in range(2 * n)},
        compiler_params=pltpu.CompilerParams(has_side_effects=EFFECT),
    )(*[_in_hbm(b) for b in bufs], *after)
    return (outs[0], outs[1]), outs[2:2 + n], outs[2 + n:2 + 2 * n], outs[-1]


def gather_wait(sems, shards, landings, after, tag):
    n = len(shards)
    send, recv = sems

    def body(*refs):
        ins, lands = refs[:n], refs[n:2 * n]
        send_sem, recv_sem = refs[2 * n], refs[2 * n + 1]
        x, y, c = _place()
        me = 2 * x + y
        for w in range(n):
            for k, (px, py) in enumerate(_other_chips(x, y)):
                cp = _remote(ins[w].at[c], lands[w].at[2 * px + py, c], send_sem.at[3 * w + k], recv_sem.at[3 * w + k],
                             (px, py, c))
                cp.wait_send()
                cp.wait_recv()

    bufs = list(shards) + list(landings)
    outs = pl.pallas_call(
        body, name="gather_wait_" + tag, out_shape=tuple(pltpu.HBM(b.shape, b.dtype) for b in bufs),
        in_specs=[HBM] * (2 * n) + [SEM, SEM] + [ANY] * len(after), out_specs=tuple([HBM] * (2 * n)),
        input_output_aliases={i: i for i in range(2 * n)},
        compiler_params=pltpu.CompilerParams(has_side_effects=EFFECT),
    )(*bufs, send, recv, *after)
    return outs[:n], outs[n:]


def gather_forward(landings, tag):
    n = len(landings)

    def body(*refs):
        ins, outs = refs[:n], refs[n:2 * n]
        send, recv = refs[2 * n:]
        x, y, c = _place()
        sibling = (x, y, 1 - c)
        cps = []
        for w in range(n):
            for k, (px, py) in enumerate(_other_chips(x, y)):
                j = 2 * px + py
                cp = _remote(ins[w].at[j, c], outs[w].at[j, c], send.at[3 * w + k], recv.at[3 * w + k], sibling)
                cp.start()
                cps.append(cp)
        for w in range(n):
            for k, (px, py) in enumerate(_other_chips(x, y)):
                blk = outs[w].at[2 * px + py, 1 - c]
                _remote(blk, blk, send.at[3 * w + k], recv.at[3 * w + k], sibling).wait_recv()
        for cp in cps:
            cp.wait_send()

    return pl.pallas_call(
        body, name="gather_forward_" + tag, in_specs=[ANY] * n, out_specs=[ANY] * n,
        out_shape=[jax.ShapeDtypeStruct(a.shape, a.dtype) for a in landings],
        input_output_aliases={i: i for i in range(n)},
        scratch_shapes=[pltpu.SemaphoreType.DMA((3 * n,)), pltpu.SemaphoreType.DMA((3 * n,))],
    )(*landings)


def pair_exchange(grads, tag):
    n = len(grads)

    def body(*refs):
        ins, outs = refs[:n], refs[n:2 * n]
        send, recv = refs[2 * n:]
        x, y, c = _place()
        cps = []
        for w in range(n):
            cp = _remote(ins[w].at[:, 1 - c], outs[w], send.at[w], recv.at[w], (x, y, 1 - c))
            cp.start()
            cps.append(cp)
        for cp in cps:
            cp.wait()

    return pl.pallas_call(
        body, name="grad_pair_exchange_" + tag, in_specs=[ANY] * n, out_specs=[ANY] * n,
        out_shape=[jax.ShapeDtypeStruct((g.shape[0],) + g.shape[2:], g.dtype) for g in grads],
        scratch_shapes=[pltpu.SemaphoreType.DMA((n,)), pltpu.SemaphoreType.DMA((n,))],
    )(*grads)


def scatter_start(sums, landings, tag):
    n = len(sums)

    def body(*refs):
        ins, lands = refs[:n], refs[n:2 * n]
        send, recv = refs[2 * n], refs[2 * n + 1]
        token = refs[-1]
        x, y, c = _place()
        for w in range(n):
            for k, (px, py) in enumerate(_other_chips(x, y)):
                _remote(ins[w].at[2 * px + py], lands[w].at[k], send.at[3 * w + k], recv.at[3 * w + k], (px, py, c)).start()
        token[...] = jnp.zeros_like(token)

    bufs = list(sums) + list(landings)
    outs = pl.pallas_call(
        body, name="scatter_start_" + tag,
        out_shape=(pltpu.SemaphoreType.DMA((3 * n,)),) * 2 + tuple(pltpu.HBM(b.shape, b.dtype) for b in bufs) + (TOKEN,),
        in_specs=[HBM] * (2 * n), out_specs=tuple([SEM, SEM] + [HBM] * (2 * n) + [IN_VMEM]),
        input_output_aliases={i: 2 + i for i in range(2 * n)},
        compiler_params=pltpu.CompilerParams(has_side_effects=EFFECT),
    )(*[_in_hbm(b) for b in bufs])
    return (outs[0], outs[1]), outs[2:2 + n], outs[2 + n:2 + 2 * n], outs[-1]


def scatter_wait(sems, sums, landings, after, tag):
    n = len(sums)

    def body(*refs):
        ins, lands = refs[:n], refs[n:2 * n]
        send, recv = refs[2 * n], refs[2 * n + 1]
        x, y, c = _place()
        for w in range(n):
            for k, (px, py) in enumerate(_other_chips(x, y)):
                cp = _remote(ins[w].at[2 * px + py], lands[w].at[k], send.at[3 * w + k], recv.at[3 * w + k], (px, py, c))
                cp.wait_send()
                cp.wait_recv()

    bufs = list(sums) + list(landings)
    outs = pl.pallas_call(
        body, name="scatter_wait_" + tag, out_shape=tuple(pltpu.HBM(b.shape, b.dtype) for b in bufs),
        in_specs=[HBM] * (2 * n) + [SEM, SEM] + [ANY] * len(after), out_specs=tuple([HBM] * (2 * n)),
        input_output_aliases={i: i for i in range(2 * n)},
        compiler_params=pltpu.CompilerParams(has_side_effects=EFFECT),
    )(*bufs, sems[0], sems[1], *after)
    return outs[:n], outs[n:]


def half_exchange(halves, tag):
    n = len(halves)

    def body(*refs):
        ins, outs = refs[:n], refs[n:2 * n]
        send, recv = refs[2 * n:]
        x, y, c = _place()
        cps = []
        for w in range(n):
            cp = _remote(ins[w], outs[w], send.at[w], recv.at[w], (x, y, 1 - c))
            cp.start()
            cps.append(cp)
        for cp in cps:
            cp.wait()

    return pl.pallas_call(
        body, name="grad_half_exchange_" + tag, in_specs=[ANY] * n, out_specs=[ANY] * n,
        out_shape=[jax.ShapeDtypeStruct(h.shape, h.dtype) for h in halves],
        scratch_shapes=[pltpu.SemaphoreType.DMA((n,)), pltpu.SemaphoreType.DMA((n,))],
    )(*halves)


def gather_small(v):
    def body(v_ref, o_ref, send, recv, local):
        x, y, c = _place()
        me = 4 * x + 2 * y + c
        own = pltpu.make_async_copy(v_ref, o_ref.at[me], local)
        own.start()
        cps = []
        for k in range(1, 8):
            fx, fy, fc = (k >> 2) & 1, (k >> 1) & 1, k & 1
            to = (x ^ fx if fx else x, y ^ fy if fy else y, c ^ fc if fc else c)
            cp = _remote(v_ref, o_ref.at[me], send.at[k - 1], recv.at[k - 1], to)
            cp.start()
            cps.append(cp)
        for k in range(1, 8):
            fx, fy, fc = (k >> 2) & 1, (k >> 1) & 1, k & 1
            px, py, pc = (x ^ fx if fx else x, y ^ fy if fy else y, c ^ fc if fc else c)
            cps[k - 1].wait_send()
            _remote(v_ref, o_ref.at[4 * px + 2 * py + pc], send.at[k - 1], recv.at[k - 1], (px, py, pc)).wait_recv()
        own.wait()

    return pl.pallas_call(
        body, name="gather_small_grads", in_specs=[ANY], out_specs=ANY,
        out_shape=jax.ShapeDtypeStruct((8,) + v.shape, v.dtype),
        scratch_shapes=[pltpu.SemaphoreType.DMA((7,)), pltpu.SemaphoreType.DMA((7,)), pltpu.SemaphoreType.DMA],
    )(v)


def _row_tile(rows, cols, nbuf_bytes):
    tm = _pick(rows, 512, 16)
    while tm * cols * nbuf_bytes * 2 > VMEM_BUDGET_V7X and tm % 32 == 0:
        tm //= 2
    return tm


def pair_sum(g, r, c_idx, tag):
    _, _, rows, cols = g.shape
    tm = _row_tile(rows, cols, 2 + 2 + 2)
    nb = rows // tm

    def body(c_ref, g_ref, r_ref, o_ref):
        o_ref[...] = (g_ref[...].astype(F32) + r_ref[...].astype(F32)).astype(o_ref.dtype)

    gs = pltpu.PrefetchScalarGridSpec(
        num_scalar_prefetch=1, grid=(N_CHIPS, nb),
        in_specs=[pl.BlockSpec((None, None, tm, cols), lambda j, i, c_ref: (j, c_ref[0], i, 0)),
                  pl.BlockSpec((None, tm, cols), lambda j, i, c_ref: (j, i, 0))],
        out_specs=pl.BlockSpec((None, tm, cols), lambda j, i, c_ref: (j, i, 0)))
    return pl.pallas_call(body, name="grad_pair_sum_" + tag, grid_spec=gs,
                          out_shape=jax.ShapeDtypeStruct(r.shape, BF16),
                          compiler_params=_params(("arbitrary", "arbitrary")))(c_idx, g, r)


def chip_sum(s, r, j_idx, tag):
    _, rows, cols = s.shape
    tm = _row_tile(rows, cols, 2 + 3 * 2 + 4)
    nb = rows // tm

    def body(j_ref, s_ref, r_ref, o_ref):
        t = s_ref[...].astype(F32)
        for k in range(3):
            t = t + r_ref[k].astype(F32)
        o_ref[...] = t

    gs = pltpu.PrefetchScalarGridSpec(
        num_scalar_prefetch=1, grid=(nb,),
        in_specs=[pl.BlockSpec((None, tm, cols), lambda i, j_ref: (j_ref[0], i, 0)),
                  pl.BlockSpec((3, tm, cols), lambda i, j_ref: (0, i, 0))],
        out_specs=pl.BlockSpec((tm, cols), lambda i, j_ref: (i, 0)))
    return pl.pallas_call(body, name="grad_chip_sum_" + tag, grid_spec=gs,
                          out_shape=jax.ShapeDtypeStruct((rows, cols), F32),
                          compiler_params=_params(("arbitrary",)))(j_idx, s, r)


def adamw(w, g, m, v, *, name):
    rows, cols = w.shape
    tm = _row_tile(rows, cols, 7 * 4)

    return rowwise(_adamw_math, [w, g, m, v], [], [(cols, F32)] * 3, tm=tm, name=name)


def _adamw_math(wb, gb, mb, vb):
    m2 = ADAM_B1 * mb + (1.0 - ADAM_B1) * gb
    v2 = ADAM_B2 * vb + (1.0 - ADAM_B2) * (gb * gb)
    m_hat = m2 / (1.0 - ADAM_B1 ** ADAM_STEP)
    v_hat = v2 / (1.0 - ADAM_B2 ** ADAM_STEP)
    delta = -ADAM_LR * (m_hat / (jnp.sqrt(v_hat) + ADAM_EPS) + ADAM_WD * wb)
    return delta, m2, v2


def adamw_shard(w, g_own, g_sib, m, v, c_idx, *, name):
    rows, cols = g_own.shape
    tm = _row_tile(rows, cols, 9 * 4)
    nb = rows // tm

    def body(c_ref, w_ref, go_ref, gs_ref, m_ref, v_ref, g_out, d_out, m_out, v_out):
        gb = jnp.where(pl.program_id(0) == c_ref[0], go_ref[...], gs_ref[...])
        delta, m2, v2 = _adamw_math(w_ref[...], gb, m_ref[...], v_ref[...])
        g_out[...] = gb
        d_out[...] = delta
        m_out[...] = m2
        v_out[...] = v2

    full = pl.BlockSpec((tm, cols), lambda h, i, c_ref: (h * nb + i, 0))
    half = pl.BlockSpec((tm, cols), lambda h, i, c_ref: (i, 0))
    gs = pltpu.PrefetchScalarGridSpec(num_scalar_prefetch=1, grid=(2, nb), in_specs=[full, half, half, full, full],
                                      out_specs=[full] * 4)
    return pl.pallas_call(body, name=name, grid_spec=gs, out_shape=[jax.ShapeDtypeStruct(w.shape, F32)] * 4,
                          compiler_params=_params(("arbitrary", "arbitrary")))(c_idx, w, g_own, g_sib, m, v)


def sum_devices(a):
    def body(a_ref, o_ref):
        t = a_ref[0]
        for k in range(1, 8):
            t = t + a_ref[k]
        o_ref[...] = t

    return pl.pallas_call(body, name="sum_small_grads", out_shape=jax.ShapeDtypeStruct(a.shape[1:], a.dtype))(a)


def _halves(w2d):
    r, c = w2d.shape
    return w2d.reshape(2, r // 2, c)


def kernel(x, p, ffn1_norm, ffn1_w_gate, ffn1_w_up, ffn1_w_down, mix_norm, w_in, q_a_norm, w_uq, kv_a_norm, w_ukv, na_rpb, w_branch_a, w_branch_b, w_out, ffn2_norm, ffn2_w_gate, ffn2_w_up, ffn2_w_down, pl_norm, w_pl, w_pl_gate, final_norm, loss_target, m_ffn1_norm, m_ffn1_w_gate, m_ffn1_w_up, m_ffn1_w_down, m_mix_norm, m_w_in, m_q_a_norm, m_w_uq, m_kv_a_norm, m_w_ukv, m_na_rpb, m_w_branch_a, m_w_branch_b, m_w_out, m_ffn2_norm, m_ffn2_w_gate, m_ffn2_w_up, m_ffn2_w_down, m_pl_norm, m_w_pl, m_w_pl_gate, m_final_norm, v_ffn1_norm, v_ffn1_w_gate, v_ffn1_w_up, v_ffn1_w_down, v_mix_norm, v_w_in, v_q_a_norm, v_w_uq, v_kv_a_norm, v_w_ukv, v_na_rpb, v_w_branch_a, v_w_branch_b, v_w_out, v_ffn2_norm, v_ffn2_w_gate, v_ffn2_w_up, v_ffn2_w_down, v_pl_norm, v_w_pl, v_w_pl_gate, v_final_norm):
    big = ["ffn1_w_gate", "ffn1_w_up", "ffn1_w_down", "w_in", "w_uq", "w_ukv", "w_branch_a", "w_branch_b", "w_out",
           "ffn2_w_gate", "ffn2_w_up", "ffn2_w_down", "w_pl", "w_pl_gate"]
    col_sharded = {"ffn1_w_gate", "ffn1_w_up", "w_in", "w_uq", "w_ukv", "w_branch_a", "w_branch_b", "ffn2_w_gate",
                   "ffn2_w_up", "w_pl"}
    small = ["ffn1_norm", "mix_norm", "q_a_norm", "kv_a_norm", "na_rpb", "ffn2_norm", "pl_norm", "final_norm"]
    order = ["ffn1_norm", "ffn1_w_gate", "ffn1_w_up", "ffn1_w_down", "mix_norm", "w_in", "q_a_norm", "w_uq",
             "kv_a_norm", "w_ukv", "na_rpb", "w_branch_a", "w_branch_b", "w_out", "ffn2_norm", "ffn2_w_gate",
             "ffn2_w_up", "ffn2_w_down", "pl_norm", "w_pl", "w_pl_gate", "final_norm"]
    env = dict(locals())
    W = {n: env[n] for n in order}
    Mo = {n: env["m_" + n] for n in order}
    Vo = {n: env["v_" + n] for n in order}

    xs = x[0]
    S, D = xs.shape
    tgt = loss_target[0]
    ps = p[0, 0]
    NAW = w_branch_a.shape[1]
    MLAW = w_branch_b.shape[1]
    NH, MH = NAW // HEAD_DIM, MLAW // HEAD_DIM
    QR, KVR = w_uq.shape[1], w_ukv.shape[1]
    F = ffn1_w_down.shape[1] * N_CHIPS
    cx, cy, cc = _place()
    c_idx = jnp.reshape(cc, (1,)).astype(jnp.int32)
    j_idx = jnp.reshape(2 * cx + cy, (1,)).astype(jnp.int32)

    me_chip = 2 * cx + cy
    groups = [["ffn1_w_gate"], ["ffn1_w_up"], ["ffn1_w_down"], ["w_in"],
              ["w_uq", "w_ukv", "w_branch_a", "w_branch_b", "w_out"],
              ["ffn2_w_gate", "ffn2_w_up", "ffn2_w_down"], ["w_pl", "w_pl_gate"]]
    started, tokens = [], []
    for g, members in enumerate(groups):
        shards = [_halves(W[n][0].astype(BF16)) for n in members]
        landings = [lax.empty((N_CHIPS,) + s.shape, BF16) for s in shards]
        sems, shards_thru, landings_thru, token = gather_start(shards, landings, tokens[-1:], str(g))
        started.append((sems, shards_thru, landings_thru))
        tokens.append(token)
    gathered = {}

    def arrive(n, after):
        g = [n in members for members in groups].index(True)
        sems, shards_thru, landings_thru = started[g]
        after = list(after) if isinstance(after, (list, tuple)) else [after]
        shards_out, landed = gather_wait(sems, shards_thru, landings_thru, after, str(g))
        for name, full, own in zip(groups[g], gather_forward(landed, str(g)), shards_out):
            gathered[name] = lax.dynamic_update_slice(full, own[None], (me_chip, 0, 0, 0))

    def stacked(n, after=None):
        if n not in gathered:
            arrive(n, after)
        g = gathered[n]
        return g.reshape(N_CHIPS, 2 * g.shape[2], g.shape[3])

    def plain(n, after=None):
        if n in col_sharded:
            st = stacked(n, after)
            return st.transpose(1, 0, 2).reshape(st.shape[1], N_CHIPS * st.shape[2])
        if n not in gathered:
            arrive(n, after)
        g = gathered[n]
        return g.reshape(N_CHIPS * 2 * g.shape[2], g.shape[3])

    n_front = 3 * NAW + QR + KVR
    off_ga = n_front
    off_kr = n_front + 2 * D

    pos = jnp.arange(S, dtype=F32)
    inv_freq = 1.0 / (ROPE_THETA ** (jnp.arange(0, MLA_ROPE, 2, dtype=F32) / MLA_ROPE))
    ang = pos[:, None] * inv_freq[None, :]
    zpad = jnp.zeros((S, LANES - MLA_ROPE), F32)
    cos_t = jnp.concatenate([jnp.cos(ang), jnp.cos(ang), zpad], axis=1)
    sin_t = jnp.concatenate([-jnp.sin(ang), jnp.sin(ang), zpad], axis=1)

    def ffn_fwd(h, norm_g, tag, pre, after=()):
        n = norm_fwd(h, norm_g, name=f"{tag}_norm")
        g = mm(n, stacked(pre + "_w_gate", [n, *after]), name=f"{tag}_gate", b_stack=True)
        u = mm(n, stacked(pre + "_w_up", g), name=f"{tag}_up", b_stack=True)
        a = swiglu_fwd(g, u, name=f"{tag}_act")
        h_out = mm(a, plain(pre + "_w_down", a), name=f"{tag}_down", res=h, alpha=0.5)
        return h_out, (n, g, u, a)

    def ffn_bwd(h, norm_g, saved, dh, dh_half, tag, pre, last, after=()):
        n, g, u, a = saved
        G[pre + "_w_down"] = mm(a, dh_half, name=f"{tag}_dw_down", ta=True, out_dtype=BF16)
        da = mm(dh_half, plain(pre + "_w_down"), name=f"{tag}_da", tb=True, after=after)
        dg, du = swiglu_bwd(g, u, da, name=f"{tag}_dact")
        G[pre + "_w_gate"] = mm(n, dg, name=f"{tag}_dw_gate", ta=True, out_dtype=BF16, out_stack=True)
        G[pre + "_w_up"] = mm(n, du, name=f"{tag}_dw_up", ta=True, out_dtype=BF16, out_stack=True)
        token = reduce_start([pre + "_w_down", pre + "_w_gate", pre + "_w_up"], tag)
        dn = mm(dg, stacked(pre + "_w_gate"), name=f"{tag}_dn_gate", tb=True, b_stack=True, after=[token])
        dn = mm(du, stacked(pre + "_w_up"), name=f"{tag}_dn_up", tb=True, b_stack=True, res=dn)
        return norm_bwd(h, norm_g, dn, name=f"{tag}_dnorm", res=dh, bf16_alpha=None if last else 1.0)

    bias = na_bias(na_rpb[0], after=tokens[-1:])
    h1, ffn1_saved = ffn_fwd(xs, ffn1_norm, "ffn1", "ffn1", after=[bias, tokens[-1]])
    u_mix = norm_fwd(h1, mix_norm, name="mix_norm")
    win = plain("w_in", u_mix)
    win_l = jnp.concatenate([win[:, :n_front], win[:, n_front + MLA_ROPE:], win[:, n_front:n_front + MLA_ROPE],
                             jnp.zeros((D, LANES - MLA_ROPE), BF16)], axis=1)
    z = mm(u_mix, win_l, name="mix_in")
    o_a = na_fwd(z, bias, NH, S)
    c_q = norm_fwd((z, QR, 3 * NAW // QR), q_a_norm, name="q_a_norm")
    c_kv = norm_fwd((z, KVR, (3 * NAW + QR) // KVR), kv_a_norm, name="kv_a_norm")
    wuq = plain("w_uq", c_kv).reshape(QR, MH, MLA_QK)
    wuq_n = wuq[:, :, :MLA_NOPE].reshape(QR, MH * MLA_NOPE)
    wuq_r = jnp.pad(wuq[:, :, MLA_NOPE:], ((0, 0), (0, 0), (0, LANES - MLA_ROPE))).reshape(QR, MH * LANES)
    wukv = plain("w_ukv").reshape(KVR, MH, 2, HEAD_DIM)
    wuk = wukv[:, :, 0].reshape(KVR, MH * HEAD_DIM)
    wuv = wukv[:, :, 1].reshape(KVR, MH * HEAD_DIM)
    q_n = mm(c_q, wuq_n, name="mla_q_nope", out_dtype=BF16)
    q_r = rope(mm(c_q, wuq_r, name="mla_q_rope"), cos_t, sin_t, name="rope_q", out_dtype=BF16)
    k_n = mm(c_kv, wuk, name="mla_k_nope", out_dtype=BF16)
    v_m = mm(c_kv, wuv, name="mla_v", out_dtype=BF16)
    k_r = rope((z, LANES, off_kr // LANES), cos_t, sin_t, name="rope_k", out_dtype=BF16)
    o_b, lse = mla_fwd(q_n, q_r, k_n, v_m, k_r, MH, S)
    y_a = mm(o_a, stacked("w_branch_a"), name="branch_a", b_stack=True)
    y_b = mm(o_b, stacked("w_branch_b"), name="branch_b", b_stack=True)
    z_ga, z_gb = (z, D, off_ga // D), (z, D, off_ga // D + 1)
    merged = rowwise(lambda ga, gb, ya, yb: _sig(ga) * ya + _sig(gb) * yb, [z_ga, z_gb, y_a, y_b], [], [(D, BF16)],
                     tm=256, name="merge")[0]
    h2 = mm(merged, plain("w_out"), name="mix_out", res=h1)
    h3, ffn2_saved = ffn_fwd(h2, ffn2_norm, "ffn2", "ffn2")
    n4 = norm_fwd(h3, pl_norm, name="pl_norm")
    pg_pre = mm(n4, plain("w_pl_gate", n4), name="pl_gate")
    pe = mm(ps, stacked("w_pl"), name="pl_embed", b_stack=True)

    def tail(h3b, pgb, peb, tb_, fg):
        pg = _sig(pgb)
        h4 = h3b + pg * peb
        r = _rstd(h4)
        xh = h4 * r
        err = xh * fg - tb_
        loss_rows = jnp.mean(err * err, axis=-1, keepdims=True)
        dy = err * (1.0 / D)
        dxh = dy * fg
        dh4 = r * (dxh - xh * jnp.mean(dxh * xh, axis=-1, keepdims=True))
        loss_part = jnp.broadcast_to(0.5 * jnp.sum(loss_rows, axis=0, keepdims=True), (1, LANES))
        return (dh4, dh4 * peb * pg * (1.0 - pg), dh4 * pg, loss_part, jnp.sum(dy * xh, axis=0, keepdims=True))

    dh4, dpg_pre, dpe, loss_part, g_final = rowwise(
        tail, [h3, pg_pre, pe, tgt], [final_norm.reshape(1, D)], [(D, F32), (D, BF16), (D, BF16)],
        accs=[(1, LANES), (1, D)], tm=128, name="loss_tail")
    loss = lax.psum(loss_part[0, 0], ("x", "y", "c"))

    G = {}
    pending = []

    def four(g):
        if g.ndim == 2:
            return g.reshape(N_CHIPS, 2, g.shape[0] // (2 * N_CHIPS), g.shape[1])
        return g.reshape(N_CHIPS, 2, g.shape[1] // 2, g.shape[2])

    def reduce_start(names, tag):
        g4 = [four(G[n]) for n in names]
        sums = [pair_sum(a, b, c_idx, n) for n, a, b in zip(names, g4, pair_exchange(g4, tag))]
        lands = [lax.empty((N_CHIPS - 1,) + s_.shape[1:], BF16) for s_ in sums]
        sems, sums, lands, token = scatter_start(sums, lands, tag)
        pending.append((names, tag, sems, sums, lands))
        return token

    def reduce_finish(entry, after):
        names, tag, sems, sums, lands = entry
        sums, got = scatter_wait(sems, sums, lands, after, tag)
        halves = [chip_sum(a, b, j_idx, n) for n, a, b in zip(names, sums, got)]
        done = []
        for n, own, sib in zip(names, halves, half_exchange(halves, tag)):
            shp = W[n].shape
            two_d = lambda a_: a_.reshape(shp[1], shp[2])
            out = adamw_shard(two_d(W[n]), own, sib, two_d(Mo[n]), two_d(Vo[n]), c_idx, name="adamw_" + n)
            grads[n], delta[n], new_m[n], new_v[n] = [o.reshape(shp) for o in out]
            done.append(out[0])
        return done

    G["w_pl"] = mm(ps, dpe, name="pl_dw_embed", ta=True, out_dtype=BF16, out_stack=True)
    G["w_pl_gate"] = mm(n4, dpg_pre, name="pl_dw_gate", ta=True, out_dtype=BF16)
    dn4 = mm(dpg_pre, plain("w_pl_gate"), name="pl_dn", tb=True)
    dh3, dh3_half, g_pl = norm_bwd(h3, pl_norm, dn4, name="pl_dnorm", res=dh4, bf16_alpha=0.5)
    token = reduce_start(["w_pl", "w_pl_gate"], "pl")
    dh2, dh2_b, g_ffn2 = ffn_bwd(h2, ffn2_norm, ffn2_saved, dh3, dh3_half, "ffn2", "ffn2", last=False, after=[token])

    G["w_out"] = mm(merged, dh2_b, name="mix_dw_out", ta=True, out_dtype=BF16)
    dmerged = mm(dh2_b, plain("w_out"), name="mix_dmerged", tb=True)

    def merge_bwd(ga, gb, ya, yb, dm):
        sa, sb = _sig(ga), _sig(gb)
        return dm * sa, dm * sb, dm * ya * sa * (1.0 - sa), dm * yb * sb * (1.0 - sb)

    dy_a, dy_b, dga, dgb = rowwise(merge_bwd, [z_ga, z_gb, y_a, y_b, dmerged], [], [(D, BF16)] * 4, tm=256,
                                   name="merge_bwd")
    G["w_branch_a"] = mm(o_a, dy_a, name="branch_a_dw", ta=True, out_dtype=BF16, out_stack=True)
    G["w_branch_b"] = mm(o_b, dy_b, name="branch_b_dw", ta=True, out_dtype=BF16, out_stack=True)
    do_a = mm(dy_a, stacked("w_branch_a"), name="branch_a_dx", tb=True, b_stack=True)
    do_b = mm(dy_b, stacked("w_branch_b"), name="branch_b_dx", tb=True, b_stack=True)
    dq_na, dk_na, dv_na, dbias = na_bwd(z, bias, do_a, NH, S)
    g_rpb = na_rpb_grad(dbias)
    dq_n, dq_rr, dk_n, dv_m, dk_rr = mla_bwd(q_n, q_r, k_n, v_m, k_r, lse, do_b, MH, S)
    dq_r = rope(dq_rr, cos_t, -sin_t, name="rope_q_bwd", out_dtype=BF16)
    dk_r = rope(dk_rr, cos_t, -sin_t, name="rope_k_bwd", out_dtype=BF16)
    gw_uq_n = mm(c_q, dq_n, name="mla_dw_q_nope", ta=True, out_dtype=BF16)
    gw_uq_r = mm(c_q, dq_r, name="mla_dw_q_rope", ta=True, out_dtype=BF16)
    dc_q = mm(dq_n, wuq_n, name="mla_dcq_nope", tb=True)
    dc_q = mm(dq_r, wuq_r, name="mla_dcq_rope", tb=True, res=dc_q)
    gw_uk = mm(c_kv, dk_n, name="mla_dw_k", ta=True, out_dtype=BF16)
    gw_uv = mm(c_kv, dv_m, name="mla_dw_v", ta=True, out_dtype=BF16)
    dc_kv = mm(dk_n, wuk, name="mla_dckv_k", tb=True)
    dc_kv = mm(dv_m, wuv, name="mla_dckv_v", tb=True, res=dc_kv)
    dq_lat, g_qa = norm_bwd((z, QR, 3 * NAW // QR), q_a_norm, dc_q, name="q_a_dnorm", want_f32=False, bf16_alpha=1.0)
    dkv_lat, g_kva = norm_bwd((z, KVR, (3 * NAW + QR) // KVR), kv_a_norm, dc_kv, name="kv_a_dnorm", want_f32=False,
                              bf16_alpha=1.0)
    dz = jnp.concatenate([dq_na, dk_na.astype(BF16), dv_na.astype(BF16), dq_lat, dkv_lat, dga, dgb, dk_r], axis=1)
    gw_in_l = mm(u_mix, dz, name="mix_dw_in", ta=True, out_dtype=BF16)

    def to_stack(g2d):
        k, n = g2d.shape
        return g2d.reshape(k, N_CHIPS, n // N_CHIPS).transpose(1, 0, 2)

    gw_uq = jnp.concatenate([gw_uq_n.reshape(QR, MH, MLA_NOPE), gw_uq_r.reshape(QR, MH, LANES)[:, :, :MLA_ROPE]],
                            axis=2).reshape(QR, MH * MLA_QK)
    G["w_uq"] = to_stack(gw_uq)
    gw_ukv = jnp.stack([gw_uk.reshape(KVR, MH, HEAD_DIM), gw_uv.reshape(KVR, MH, HEAD_DIM)], axis=2)
    G["w_ukv"] = to_stack(gw_ukv.reshape(KVR, MH * 2 * HEAD_DIM))
    token_mix = reduce_start(["w_out", "w_branch_a", "w_branch_b", "w_uq", "w_ukv"], "mix")
    gw_in = jnp.concatenate([gw_in_l[:, :n_front], gw_in_l[:, off_kr:off_kr + MLA_ROPE], gw_in_l[:, n_front:off_kr]],
                            axis=1)
    G["w_in"] = to_stack(gw_in)
    token_win = reduce_start(["w_in"], "win")
    du_mix = mm(dz, win_l, name="mix_du", tb=True, after=[token_mix, token_win])
    dh1, dh1_half, g_mix = norm_bwd(h1, mix_norm, du_mix, name="mix_dnorm", res=dh2, bf16_alpha=0.5)
    grad_x, g_ffn1 = ffn_bwd(xs, ffn1_norm, ffn1_saved, dh1, dh1_half, "ffn1", "ffn1", last=True)

    small_g = {"ffn1_norm": g_ffn1, "mix_norm": g_mix, "q_a_norm": g_qa, "kv_a_norm": g_kva, "na_rpb": g_rpb,
               "ffn2_norm": g_ffn2, "pl_norm": g_pl, "final_norm": g_final}
    sizes = [int(np.prod(W[n].shape)) for n in small]
    total = sum(sizes)
    padded = -(-total // (8 * LANES)) * (8 * LANES)

    def pack(parts):
        flat = jnp.concatenate([jnp.reshape(parts[n], (-1,)).astype(F32) for n in small]
                               + [jnp.zeros((padded - total,), F32)])
        return flat.reshape(padded // LANES, LANES)

    def unpack(a):
        flat, out, o = a.reshape(-1), {}, 0
        for n, sz in zip(small, sizes):
            out[n] = flat[o:o + sz].reshape(W[n].shape)
            o += sz
        return out

    g_small = sum_devices(gather_small(pack(small_g)))
    d_small, m_small, v_small = adamw(pack(W), g_small, pack(Mo), pack(Vo), name="adamw_small")
    grads = unpack(g_small)
    delta, new_m, new_v = unpack(d_small), unpack(m_small), unpack(v_small)

    after = [grad_x]
    for entry in pending:
        after = reduce_finish(entry, after)

    return (loss, grad_x[None], *[grads[n] for n in order], *[delta[n] for n in order],
            *[new_m[n] for n in order], *[new_v[n] for n in order])
```

```python
import functools

import numpy as np
import jax
import jax.numpy as jnp
from jax import lax
from jax.experimental import pallas as pl
from jax.experimental.pallas import tpu as pltpu

F32 = jnp.float32
BF16 = jnp.bfloat16

VMEM_LIMIT_V7X = 56 * 1024 * 1024
VMEM_BUDGET_V7X = 40 * 1024 * 1024
LANES = 128

GRID_W = 64
NA_WIN_ROWS = 8
NA_WIN_COLS = 16
HEAD_DIM = 128
MLA_NOPE = 128
MLA_ROPE = 64
MLA_QK = MLA_NOPE + MLA_ROPE
ROPE_THETA = 10000.0
NORM_EPS = 1e-6
NEG_INF = -1e30
N_CHIPS = 4

ADAM_LR = 0.001
ADAM_B1 = 0.9
ADAM_B2 = 0.999
ADAM_EPS = 1e-08
ADAM_WD = 0.01
ADAM_STEP = 10

MESH = pl.DeviceIdType.MESH
ANY = pl.BlockSpec(memory_space=pl.ANY)


def _params(sem=None):
    return pltpu.CompilerParams(dimension_semantics=sem, vmem_limit_bytes=VMEM_LIMIT_V7X)


def _pick(n, target, align):
    best = None
    t = align
    while t <= min(n, target):
        if n % t == 0:
            best = t
        t += align
    return n if best is None else best


def mm(a, b, *, name, ta=False, tb=False, out_dtype=F32, res=None, alpha=1.0, b_stack=False, out_stack=False,
       exact=False, after=()):
    K, M = (a.shape if ta else a.shape[::-1])
    if b_stack:
        nst = b.shape[0]
        if tb:
            N, kb = b.shape[1], b.shape[2]
            Kb, nb = nst * kb, None
        else:
            Kb, nb = b.shape[1], b.shape[2]
            N = nst * nb
    else:
        N, Kb = (b.shape if tb else b.shape[::-1])
    assert K == Kb, (a.shape, b.shape, ta, tb)
    if out_stack:
        assert N % N_CHIPS == 0
    n_unit = N // N_CHIPS if out_stack else (nb if (b_stack and not tb) else N)
    k_unit = kb if (b_stack and tb) else K
    tn = _pick(n_unit, 512, LANES) if n_unit % 512 == 0 or n_unit <= 512 else _pick(n_unit, 1536, LANES)
    tk = _pick(k_unit, 2048, LANES)
    tm = _pick(M, 1024, LANES if ta else 16)
    isz = lambda t: jnp.dtype(t.dtype).itemsize
    osz = jnp.dtype(out_dtype).itemsize

    def vmem(tm_):
        return (2 * tm_ * tk * isz(a) + 2 * tk * tn * isz(b) + 2 * tm_ * tn * osz + tm_ * tn * 4
                + (2 * tm_ * tn * isz(res) if res is not None else 0))

    while vmem(tm) > VMEM_BUDGET_V7X and tm % 2 == 0 and (tm // 2) % (LANES if ta else 16) == 0:
        tm //= 2
    nk = K // tk
    gm, gn = M // tm, N // tn

    a_spec = pl.BlockSpec((tk, tm), lambda i, j, k: (k, i)) if ta else pl.BlockSpec((tm, tk), lambda i, j, k: (i, k))
    if b_stack and not tb:
        q = nb // tn
        b_spec = pl.BlockSpec((None, tk, tn), lambda i, j, k: (j // q, k, j % q))
    elif b_stack and tb:
        q = kb // tk
        b_spec = pl.BlockSpec((None, tn, tk), lambda i, j, k: (k // q, j, k % q))
    elif tb:
        b_spec = pl.BlockSpec((tn, tk), lambda i, j, k: (j, k))
    else:
        b_spec = pl.BlockSpec((tk, tn), lambda i, j, k: (k, j))
    if out_stack:
        qo = (N // N_CHIPS) // tn
        o_spec = pl.BlockSpec((None, tm, tn), lambda i, j, k: (j // qo, i, j % qo))
        o_shape = jax.ShapeDtypeStruct((N_CHIPS, M, N // N_CHIPS), out_dtype)
    else:
        o_spec = pl.BlockSpec((tm, tn), lambda i, j, k: (i, j))
        o_shape = jax.ShapeDtypeStruct((M, N), out_dtype)
    dims = (((0 if ta else 1,), (1 if tb else 0,)), ((), ()))
    has_res = res is not None

    def body(*refs):
        a_ref, b_ref = refs[:2]
        r_ref = refs[2] if has_res else None
        o_ref, acc_ref = refs[-2:]
        k = pl.program_id(2)
        if exact:
            part = lax.dot_general(a_ref[...], b_ref[...], dims, preferred_element_type=F32,
                                   precision=lax.Precision.HIGHEST)
        else:
            part = lax.dot_general(a_ref[...].astype(BF16), b_ref[...].astype(BF16), dims,
                                   preferred_element_type=F32)

        def finish(total):
            if alpha != 1.0:
                total = total * alpha
            if has_res:
                total = total + r_ref[...].astype(F32)
            o_ref[...] = total.astype(out_dtype)

        if nk == 1:
            finish(part)
        else:
            @pl.when(k == 0)
            def _():
                acc_ref[...] = part

            @pl.when(jnp.logical_and(k > 0, k < nk - 1))
            def _():
                acc_ref[...] += part

            @pl.when(k == nk - 1)
            def _():
                finish(acc_ref[...] + part)

    in_specs = [a_spec, b_spec]
    args = [a, b]
    if has_res:
        in_specs.append(pl.BlockSpec((tm, tn), lambda i, j, k: (i, j)))
        args.append(res)
    in_specs += [ANY] * len(after)
    args += list(after)
    return pl.pallas_call(
        body, name=name, grid=(gm, gn, nk), in_specs=in_specs, out_specs=o_spec, out_shape=o_shape,
        scratch_shapes=[pltpu.VMEM((tm, tn) if nk > 1 else (8, LANES), F32)],
        compiler_params=_params(("parallel", "parallel", "arbitrary")),
    )(*args)


def rowwise(fn, rows, consts, outs, accs=(), *, tm, name, tn=None, into=None):
    rows = [r if isinstance(r, tuple) else (r, r.shape[1], 0) for r in rows]
    S = rows[0][0].shape[0]
    tm = _pick(S, tm, 16)
    nrow, ncon, nout = len(rows), len(consts), len(outs)
    outs = [o if len(o) == 3 else (o[0], o[1], None) for o in outs]
    if tn is None:
        grid = (S // tm,)
        in_specs = [pl.BlockSpec((tm, w), functools.partial(lambda i, cb: (i, cb), cb=cb)) for _, w, cb in rows]
        in_specs += [pl.BlockSpec(c.shape, lambda i: (0, 0)) for c in consts]
        out_specs = [pl.BlockSpec((tm, n), functools.partial(lambda i, cb: (i, cb), cb=cb or 0)) for n, _, cb in outs]
        out_specs += [pl.BlockSpec(s, lambda i: (0, 0)) for s in accs]
        sem = ("arbitrary",)
    else:
        assert not accs
        N = rows[0][1]
        grid = (S // tm, N // tn)
        in_specs = [pl.BlockSpec((tm, tn), lambda i, j: (i, j)) for _ in rows]
        in_specs += [pl.BlockSpec(c.shape, lambda i, j: (0, 0)) for c in consts]
        out_specs = [pl.BlockSpec((tm, tn), lambda i, j: (i, j)) for _ in outs]
        sem = ("parallel", "parallel")
    out_shape = [jax.ShapeDtypeStruct((S, n if cb is None else into[1]), dt) for n, dt, cb in outs]
    out_shape += [jax.ShapeDtypeStruct(s, F32) for s in accs]
    extra, aliases = [], {}
    if into is not None and into[0] is not None:
        extra = [into[0]]
        aliases = {nrow + ncon: [cb is not None for _, _, cb in outs].index(True)}

    def body(*refs):
        vals = fn(*[r[...] for r in refs[:nrow + ncon]])
        if not isinstance(vals, (tuple, list)):
            vals = (vals,)
        o_refs = refs[nrow + ncon + len(extra):]
        for o_ref, v in zip(o_refs[:nout], vals[:nout]):
            o_ref[...] = v.astype(o_ref.dtype)
        if accs:
            first = pl.program_id(0) == 0

            def accumulate(a_ref, v):
                @pl.when(first)
                def _():
                    a_ref[...] = v

                @pl.when(jnp.logical_not(first))
                def _():
                    a_ref[...] += v

            for a_ref, v in zip(o_refs[nout:], vals[nout:]):
                accumulate(a_ref, v.astype(F32))

    return pl.pallas_call(
        body, name=name, grid=grid, in_specs=in_specs + [ANY] * len(extra), out_specs=out_specs, out_shape=out_shape,
        input_output_aliases=aliases, compiler_params=_params(sem),
    )(*[r[0] for r in rows], *consts, *extra)


def _rstd(x):
    return lax.rsqrt(jnp.mean(x * x, axis=-1, keepdims=True) + NORM_EPS)


def norm_fwd(x, g, *, name, tm=256):
    w = x[1] if isinstance(x, tuple) else x.shape[1]

    def fn(xb, gb):
        return (xb * _rstd(xb)) * gb

    return rowwise(fn, [x], [g], [(w, BF16)], tm=tm, name=name)[0]


def norm_bwd(x, g, dn, *, name, res=None, want_f32=True, bf16_alpha=None, tm=256, into=None, cb=None):
    w = x[1] if isinstance(x, tuple) else x.shape[1]
    has_res = res is not None

    def fn(*blocks):
        if has_res:
            xb, dnb, rb, gb = blocks
        else:
            xb, dnb, gb = blocks
        r = _rstd(xb)
        xh = xb * r
        dxh = dnb * gb
        dx = r * (dxh - xh * jnp.mean(dxh * xh, axis=-1, keepdims=True))
        if has_res:
            dx = dx + rb
        out = []
        if want_f32:
            out.append(dx)
        if bf16_alpha is not None:
            out.append(dx * bf16_alpha if bf16_alpha != 1.0 else dx)
        out.append(jnp.sum(dnb * xh, axis=0, keepdims=True))
        return tuple(out)

    outs = ([(w, F32)] if want_f32 else []) + ([(w, BF16, cb)] if bf16_alpha is not None else [])
    rows = [x, dn] + ([res] if has_res else [])
    return rowwise(fn, rows, [g], outs, accs=[(1, w)], tm=tm, name=name, into=into)


def _sig(x):
    return jax.nn.sigmoid(x)


def swiglu_fwd(g, u, *, name):
    return rowwise(lambda gb, ub: gb * _sig(gb) * ub, [g, u], [], [(g.shape[1], BF16)], tm=256, name=name,
                   tn=_pick(g.shape[1], 1536, LANES))[0]


def swiglu_bwd(g, u, da, *, name):
    def fn(gb, ub, dab):
        s = _sig(gb)
        return dab * ub * (s + gb * s * (1.0 - s)), dab * (gb * s)

    n = g.shape[1]
    return rowwise(fn, [g, u, da], [], [(n, BF16), (n, BF16)], tm=256, name=name, tn=_pick(n, 1536, LANES))


def rope(x, cos, sin_signed, *, name, out_dtype, into=None, cb=None, zero_cols=0):
    w = x[1] if isinstance(x, tuple) else x.shape[1]
    half = MLA_ROPE // 2

    def fn(xb, cb, sb):
        lane = lax.broadcasted_iota(jnp.int32, cb.shape, 1)
        outs = []
        for hb in range(w // LANES):
            blk = xb[:, hb * LANES:(hb + 1) * LANES]
            partner = jnp.where(lane < half, pltpu.roll(blk, LANES - half, 1), pltpu.roll(blk, half, 1))
            outs.append(blk * cb + partner * sb)
        if zero_cols:
            outs.append(jnp.zeros((xb.shape[0], zero_cols), xb.dtype))
        return outs[0] if len(outs) == 1 else jnp.concatenate(outs, axis=1)

    return rowwise(fn, [x, cos, sin_signed], [], [(w + zero_cols, out_dtype, cb)], tm=256, name=name, into=into)[0]


def _na_tables():
    cols = np.arange(GRID_W)
    kw = NA_WIN_COLS
    dc = np.clip(cols[None, :] - cols[:, None], -(kw - 1), kw - 1) + (kw - 1)
    onehot = np.zeros((LANES, GRID_W * GRID_W), np.float32)
    onehot[dc.reshape(-1), np.arange(GRID_W * GRID_W)] = 1.0
    col_start = np.clip(cols - kw // 2, 0, GRID_W - kw)
    mask = (cols[None, :] >= col_start[:, None]) & (cols[None, :] < col_start[:, None] + kw)
    return onehot, np.where(mask, 0.0, NEG_INF).astype(np.float32)


def na_bias(rpb, after=()):
    H = rpb.shape[0]
    nr, kh = 2 * NA_WIN_ROWS - 1, NA_WIN_ROWS
    onehot, maskb = _na_tables()
    rp = jnp.pad(rpb.reshape(H * nr, 2 * NA_WIN_COLS - 1), ((0, 0), (0, LANES - (2 * NA_WIN_COLS - 1))))
    t1 = mm(rp, jnp.asarray(onehot), name="na_bias_table", exact=True, after=after).reshape(H, nr, GRID_W, GRID_W)
    t1 = t1 + jnp.asarray(maskb)[None, None]
    per_t = [jnp.stack([t1[:, i - t + kh - 1] for i in range(kh)], axis=2) for t in range(kh)]
    return jnp.stack(per_t, axis=1).reshape(H, kh, GRID_W, kh * GRID_W)


def na_rpb_grad(db):
    H = db.shape[0]
    nr, kh = 2 * NA_WIN_ROWS - 1, NA_WIN_ROWS
    onehot, _ = _na_tables()
    db = db.reshape(H, kh, GRID_W, kh, GRID_W)
    per_dr = []
    for dri in range(nr):
        terms = [db[:, t, :, dri - (kh - 1) + t, :] for t in range(kh) if 0 <= dri - (kh - 1) + t < kh]
        per_dr.append(functools.reduce(jnp.add, terms))
    dt1 = jnp.stack(per_dr, axis=1).reshape(H * nr, GRID_W * GRID_W)
    g = mm(dt1, jnp.asarray(onehot), name="na_rpb_grad", tb=True, exact=True)
    return g[:, :2 * NA_WIN_COLS - 1].reshape(H, nr, 2 * NA_WIN_COLS - 1)


def _na_first_row(r, rows):
    return jnp.clip(r - NA_WIN_ROWS // 2, 0, rows - NA_WIN_ROWS)


def _na_scores(q_ref, k_ref, b_ref, start):
    q = q_ref[...].astype(BF16)
    k = k_ref[pl.ds(start, NA_WIN_ROWS * GRID_W), :].astype(BF16)
    s = lax.dot_general(q, k, (((1,), (1,)), ((), ())), preferred_element_type=F32)
    s = s * (HEAD_DIM ** -0.5) + b_ref[...]
    m = jnp.max(s, axis=-1, keepdims=True)
    e = jnp.exp(s - m)
    return q, k, e / jnp.sum(e, axis=-1, keepdims=True)


def na_fwd(z, bias, H, S):
    rows = S // GRID_W
    nkeys = NA_WIN_ROWS * GRID_W

    def body(q_ref, k_ref, v_ref, b_ref, o_ref):
        r = pl.program_id(1)
        start = pl.multiple_of(_na_first_row(r, rows) * GRID_W, GRID_W)
        _, _, p = _na_scores(q_ref, k_ref, b_ref, start)
        v = v_ref[pl.ds(start, nkeys), :].astype(BF16)
        o_ref[...] = jnp.dot(p.astype(BF16), v, preferred_element_type=F32).astype(o_ref.dtype)

    return pl.pallas_call(
        body, name="na_fwd", grid=(H, rows),
        in_specs=[pl.BlockSpec((GRID_W, HEAD_DIM), lambda h, r: (r, h)),
                  pl.BlockSpec((S, HEAD_DIM), lambda h, r: (0, H + h)),
                  pl.BlockSpec((S, HEAD_DIM), lambda h, r: (0, 2 * H + h)),
                  pl.BlockSpec((None, None, GRID_W, nkeys), lambda h, r: (h, r - _na_first_row(r, rows), 0, 0))],
        out_specs=pl.BlockSpec((GRID_W, HEAD_DIM), lambda h, r: (r, h)),
        out_shape=jax.ShapeDtypeStruct((S, H * HEAD_DIM), BF16),
        compiler_params=_params(("parallel", "arbitrary")),
    )(z, z, z, bias)


def na_bwd(z, bias, do, H, S):
    rows = S // GRID_W
    nkeys = NA_WIN_ROWS * GRID_W
    tn_dims = (((0,), (0,)), ((), ()))

    def body(q_ref, k_ref, v_ref, b_ref, do_ref, dq_ref, dk_ref, dv_ref, db_ref, dk_acc, dv_acc):
        r = pl.program_id(1)
        start = pl.multiple_of(_na_first_row(r, rows) * GRID_W, GRID_W)
        q, k, p = _na_scores(q_ref, k_ref, b_ref, start)
        v = v_ref[pl.ds(start, nkeys), :].astype(BF16)
        dob = do_ref[...].astype(BF16)
        dp = lax.dot_general(dob, v, (((1,), (1,)), ((), ())), preferred_element_type=F32)
        ds = p * (dp - jnp.sum(dp * p, axis=-1, keepdims=True))
        dsb = (ds * (HEAD_DIM ** -0.5)).astype(BF16)
        dq_ref[...] = jnp.dot(dsb, k, preferred_element_type=F32).astype(dq_ref.dtype)

        @pl.when(r == 0)
        def _():
            dk_acc[...] = jnp.zeros_like(dk_acc)
            dv_acc[...] = jnp.zeros_like(dv_acc)

        dk_acc[pl.ds(start, nkeys), :] += lax.dot_general(dsb, q, tn_dims, preferred_element_type=F32)
        dv_acc[pl.ds(start, nkeys), :] += lax.dot_general(p.astype(BF16), dob, tn_dims, preferred_element_type=F32)

        @pl.when(r == rows - 1)
        def _():
            dk_ref[...] = dk_acc[...].astype(dk_ref.dtype)
            dv_ref[...] = dv_acc[...].astype(dv_ref.dtype)

        fresh = jnp.logical_or(r <= NA_WIN_ROWS // 2, r > rows - NA_WIN_ROWS // 2)

        @pl.when(fresh)
        def _():
            db_ref[...] = ds

        @pl.when(jnp.logical_not(fresh))
        def _():
            db_ref[...] += ds

    W = H * HEAD_DIM
    return pl.pallas_call(
        body, name="na_bwd", grid=(H, rows),
        in_specs=[pl.BlockSpec((GRID_W, HEAD_DIM), lambda h, r: (r, h)),
                  pl.BlockSpec((S, HEAD_DIM), lambda h, r: (0, H + h)),
                  pl.BlockSpec((S, HEAD_DIM), lambda h, r: (0, 2 * H + h)),
                  pl.BlockSpec((None, None, GRID_W, nkeys), lambda h, r: (h, r - _na_first_row(r, rows), 0, 0)),
                  pl.BlockSpec((GRID_W, HEAD_DIM), lambda h, r: (r, h))],
        out_specs=[pl.BlockSpec((GRID_W, HEAD_DIM), lambda h, r: (r, h)),
                   pl.BlockSpec((S, HEAD_DIM), lambda h, r: (0, h)),
                   pl.BlockSpec((S, HEAD_DIM), lambda h, r: (0, h)),
                   pl.BlockSpec((None, None, GRID_W, nkeys), lambda h, r: (h, r - _na_first_row(r, rows), 0, 0))],
        out_shape=[jax.ShapeDtypeStruct((S, W), BF16)] * 3 + [jax.ShapeDtypeStruct((H, NA_WIN_ROWS, GRID_W, nkeys), F32)],
        scratch_shapes=[pltpu.VMEM((S, HEAD_DIM), F32)] * 2,
        compiler_params=_params(("arbitrary", "arbitrary")),
    )(z, z, z, bias, do)


def _mla_scores(qn_ref, qr_ref, kn_ref, kr_ref):
    nt = (((1,), (1,)), ((), ()))
    s = lax.dot_general(qn_ref[...], kn_ref[...], nt, preferred_element_type=F32)
    s = s + lax.dot_general(qr_ref[...], kr_ref[...], nt, preferred_element_type=F32)
    return s * (MLA_QK ** -0.5)


def mla_fwd(qn, qr, kn, v, kr, H, S):
    tq = _pick(S, 256, 16)

    def body(qn_ref, qr_ref, kn_ref, v_ref, kr_ref, o_ref, lse_ref):
        s = _mla_scores(qn_ref, qr_ref, kn_ref, kr_ref)
        m = jnp.max(s, axis=-1, keepdims=True)
        e = jnp.exp(s - m)
        l = jnp.sum(e, axis=-1, keepdims=True)
        o_ref[...] = jnp.dot((e / l).astype(BF16), v_ref[...], preferred_element_type=F32).astype(o_ref.dtype)
        lse_ref[...] = jnp.broadcast_to(m + jnp.log(l), lse_ref.shape)

    qspec = pl.BlockSpec((tq, HEAD_DIM), lambda h, i: (i, h))
    kspec = pl.BlockSpec((S, HEAD_DIM), lambda h, i: (0, h))
    return pl.pallas_call(
        body, name="mla_fwd", grid=(H, S // tq),
        in_specs=[qspec, qspec, kspec, kspec, pl.BlockSpec((S, LANES), lambda h, i: (0, 0))],
        out_specs=[qspec, qspec],
        out_shape=[jax.ShapeDtypeStruct((S, H * HEAD_DIM), BF16), jax.ShapeDtypeStruct((S, H * LANES), F32)],
        compiler_params=_params(("parallel", "arbitrary")),
    )(qn, qr, kn, v, kr)


def mla_bwd(qn, qr, kn, v, kr, lse, do, H, S):
    tq = _pick(S, 256, 16)
    nt = (((1,), (1,)), ((), ()))
    tn_dims = (((0,), (0,)), ((), ()))

    def body(qn_ref, qr_ref, kn_ref, v_ref, kr_ref, lse_ref, do_ref, dqn_ref, dqr_ref, dkn_ref, dv_ref, dkr_ref):
        h, i = pl.program_id(0), pl.program_id(1)
        s = _mla_scores(qn_ref, qr_ref, kn_ref, kr_ref)
        p = jnp.exp(s - lse_ref[:, 0:1])
        dob = do_ref[...].astype(BF16)
        dp = lax.dot_general(dob, v_ref[...], nt, preferred_element_type=F32)
        ds = p * (dp - jnp.sum(dp * p, axis=-1, keepdims=True))
        dsb = (ds * (MLA_QK ** -0.5)).astype(BF16)
        dqn_ref[...] = jnp.dot(dsb, kn_ref[...], preferred_element_type=F32).astype(dqn_ref.dtype)
        dqr_ref[...] = jnp.dot(dsb, kr_ref[...], preferred_element_type=F32).astype(dqr_ref.dtype)

        @pl.when(i == 0)
        def _():
            dkn_ref[...] = jnp.zeros_like(dkn_ref)
            dv_ref[...] = jnp.zeros_like(dv_ref)

        @pl.when(jnp.logical_and(i == 0, h == 0))
        def _():
            dkr_ref[...] = jnp.zeros_like(dkr_ref)

        dkn_ref[...] += lax.dot_general(dsb, qn_ref[...], tn_dims, preferred_element_type=F32)
        dkr_ref[...] += lax.dot_general(dsb, qr_ref[...], tn_dims, preferred_element_type=F32)
        dv_ref[...] += lax.dot_general(p.astype(BF16), dob, tn_dims, preferred_element_type=F32)

    qspec = pl.BlockSpec((tq, HEAD_DIM), lambda h, i: (i, h))
    kspec = pl.BlockSpec((S, HEAD_DIM), lambda h, i: (0, h))
    rspec = pl.BlockSpec((S, LANES), lambda h, i: (0, 0))
    W = H * HEAD_DIM
    return pl.pallas_call(
        body, name="mla_bwd", grid=(H, S // tq),
        in_specs=[qspec, qspec, kspec, kspec, rspec, qspec, qspec],
        out_specs=[qspec, qspec, kspec, kspec, rspec],
        out_shape=[jax.ShapeDtypeStruct((S, W), BF16), jax.ShapeDtypeStruct((S, W), F32),
                   jax.ShapeDtypeStruct((S, W), F32), jax.ShapeDtypeStruct((S, W), F32),
                   jax.ShapeDtypeStruct((S, LANES), F32)],
        compiler_params=_params(("arbitrary", "arbitrary")),
    )(qn, qr, kn, v, kr, lse, do)


def _place():
    return lax.axis_index("x"), lax.axis_index("y"), lax.axis_index("c")


def _other_chips(x, y):
    return [(1 - x, y), (x, 1 - y), (1 - x, 1 - y)]


def _remote(src, dst, send_sem, recv_sem, to):
    return pltpu.make_async_remote_copy(src_ref=src, dst_ref=dst, send_sem=send_sem, recv_sem=recv_sem,
                                        device_id=to, device_id_type=MESH)


HBM = pl.BlockSpec(memory_space=pltpu.HBM)
SEM = pl.BlockSpec(memory_space=pltpu.SEMAPHORE)
EFFECT = pltpu.SideEffectType.DATAFLOW_SIDE_EFFECTING


def _in_hbm(a):
    return pltpu.with_memory_space_constraint(a, pltpu.HBM)


TOKEN = jax.ShapeDtypeStruct((8, LANES), F32)
IN_VMEM = pl.BlockSpec(memory_space=pltpu.VMEM)


def gather_start(shards, landings, after, tag):
    n = len(shards)

    def body(*refs):
        ins, lands = refs[:n], refs[n:2 * n]
        send, recv = refs[2 * n + len(after)], refs[2 * n + len(after) + 1]
        token = refs[-1]
        x, y, c = _place()
        me = 2 * x + y
        for w in range(n):
            for k, (px, py) in enumerate(_other_chips(x, y)):
                _remote(ins[w].at[c], lands[w].at[me, c], send.at[3 * w + k], recv.at[3 * w + k], (px, py, c)).start()
        token[...] = jnp.zeros_like(token)

    bufs = list(shards) + list(landings)
    outs = pl.pallas_call(
        body, name="gather_start_" + tag,
        out_shape=(pltpu.SemaphoreType.DMA((3 * n,)),) * 2 + tuple(pltpu.HBM(b.shape, b.dtype) for b in bufs) + (TOKEN,),
        in_specs=[HBM] * (2 * n) + [ANY] * len(after), out_specs=tuple([SEM, SEM] + [HBM] * (2 * n) + [IN_VMEM]),
        input_output_aliases={i: 2 + i for i in range(2 * n)},
        compiler_params=pltpu.CompilerParams(has_side_effects=EFFECT),
    )(*[_in_hbm(b) for b in bufs], *after)
    return (outs[0], outs[1]), outs[2:2 + n], outs[2 + n:2 + 2 * n], outs[-1]


def gather_wait(sems, shards, landings, after, tag):
    n = len(shards)
    send, recv = sems

    def body(*refs):
        ins, lands = refs[:n], refs[n:2 * n]
        send_sem, recv_sem = refs[2 * n], refs[2 * n + 1]
        x, y, c = _place()
        me = 2 * x + y
        for w in range(n):
            for k, (px, py) in enumerate(_other_chips(x, y)):
                cp = _remote(ins[w].at[c], lands[w].at[2 * px + py, c], send_sem.at[3 * w + k], recv_sem.at[3 * w + k],
                             (px, py, c))
                cp.wait_send()
                cp.wait_recv()

    bufs = list(shards) + list(landings)
    outs = pl.pallas_call(
        body, name="gather_wait_" + tag, out_shape=tuple(pltpu.HBM(b.shape, b.dtype) for b in bufs),
        in_specs=[HBM] * (2 * n) + [SEM, SEM] + [ANY] * len(after), out_specs=tuple([HBM] * (2 * n)),
        input_output_aliases={i: i for i in range(2 * n)},
        compiler_params=pltpu.CompilerParams(has_side_effects=EFFECT),
    )(*bufs, send, recv, *after)
    return outs[:n], outs[n:]


def gather_forward(landings, tag):
    n = len(landings)

    def body(*refs):
        ins, outs = refs[:n], refs[n:2 * n]
        send, recv = refs[2 * n:]
        x, y, c = _place()
        sibling = (x, y, 1 - c)
        cps = []
        for w in range(n):
            for k, (px, py) in enumerate(_other_chips(x, y)):
                j = 2 * px + py
                cp = _remote(ins[w].at[j, c], outs[w].at[j, c], send.at[3 * w + k], recv.at[3 * w + k], sibling)
                cp.start()
                cps.append(cp)
        for w in range(n):
            for k, (px, py) in enumerate(_other_chips(x, y)):
                blk = outs[w].at[2 * px + py, 1 - c]
                _remote(blk, blk, send.at[3 * w + k], recv.at[3 * w + k], sibling).wait_recv()
        for cp in cps:
            cp.wait_send()

    return pl.pallas_call(
        body, name="gather_forward_" + tag, in_specs=[ANY] * n, out_specs=[ANY] * n,
        out_shape=[jax.ShapeDtypeStruct(a.shape, a.dtype) for a in landings],
        input_output_aliases={i: i for i in range(n)},
        scratch_shapes=[pltpu.SemaphoreType.DMA((3 * n,)), pltpu.SemaphoreType.DMA((3 * n,))],
    )(*landings)


def pair_exchange(grads, tag):
    n = len(grads)

    def body(*refs):
        ins, outs = refs[:n], refs[n:2 * n]
        send, recv = refs[2 * n:]
        x, y, c = _place()
        cps = []
        for w in range(n):
            cp = _remote(ins[w].at[:, 1 - c], outs[w], send.at[w], recv.at[w], (x, y, 1 - c))
            cp.start()
            cps.append(cp)
        for cp in cps:
            cp.wait()

    return pl.pallas_call(
        body, name="grad_pair_exchange_" + tag, in_specs=[ANY] * n, out_specs=[ANY] * n,
        out_shape=[jax.ShapeDtypeStruct((g.shape[0],) + g.shape[2:], g.dtype) for g in grads],
        scratch_shapes=[pltpu.SemaphoreType.DMA((n,)), pltpu.SemaphoreType.DMA((n,))],
    )(*grads)


def scatter_start(sums, landings, tag):
    n = len(sums)

    def body(*refs):
        ins, lands = refs[:n], refs[n:2 * n]
        send, recv = refs[2 * n], refs[2 * n + 1]
        token = refs[-1]
        x, y, c = _place()
        for w in range(n):
            for k, (px, py) in enumerate(_other_chips(x, y)):
                _remote(ins[w].at[2 * px + py], lands[w].at[k], send.at[3 * w + k], recv.at[3 * w + k], (px, py, c)).start()
        token[...] = jnp.zeros_like(token)

    bufs = list(sums) + list(landings)
    outs = pl.pallas_call(
        body, name="scatter_start_" + tag,
        out_shape=(pltpu.SemaphoreType.DMA((3 * n,)),) * 2 + tuple(pltpu.HBM(b.shape, b.dtype) for b in bufs) + (TOKEN,),
        in_specs=[HBM] * (2 * n), out_specs=tuple([SEM, SEM] + [HBM] * (2 * n) + [IN_VMEM]),
        input_output_aliases={i: 2 + i for i in range(2 * n)},
        compiler_params=pltpu.CompilerParams(has_side_effects=EFFECT),
    )(*[_in_hbm(b) for b in bufs])
    return (outs[0], outs[1]), outs[2:2 + n], outs[2 + n:2 + 2 * n], outs[-1]


def scatter_wait(sems, sums, landings, after, tag):
    n = len(sums)

    def body(*refs):
        ins, lands = refs[:n], refs[n:2 * n]
        send, recv = refs[2 * n], refs[2 * n + 1]
        x, y, c = _place()
        for w in range(n):
            for k, (px, py) in enumerate(_other_chips(x, y)):
                cp = _remote(ins[w].at[2 * px + py], lands[w].at[k], send.at[3 * w + k], recv.at[3 * w + k], (px, py, c))
                cp.wait_send()
                cp.wait_recv()

    bufs = list(sums) + list(landings)
    outs = pl.pallas_call(
        body, name="scatter_wait_" + tag, out_shape=tuple(pltpu.HBM(b.shape, b.dtype) for b in bufs),
        in_specs=[HBM] * (2 * n) + [SEM, SEM] + [ANY] * len(after), out_specs=tuple([HBM] * (2 * n)),
        input_output_aliases={i: i for i in range(2 * n)},
        compiler_params=pltpu.CompilerParams(has_side_effects=EFFECT),
    )(*bufs, sems[0], sems[1], *after)
    return outs[:n], outs[n:]


def half_exchange(halves, tag):
    n = len(halves)

    def body(*refs):
        ins, outs = refs[:n], refs[n:2 * n]
        send, recv = refs[2 * n:]
        x, y, c = _place()
        cps = []
        for w in range(n):
            cp = _remote(ins[w], outs[w], send.at[w], recv.at[w], (x, y, 1 - c))
            cp.start()
            cps.append(cp)
        for cp in cps:
            cp.wait()

    return pl.pallas_call(
        body, name="grad_half_exchange_" + tag, in_specs=[ANY] * n, out_specs=[ANY] * n,
        out_shape=[jax.ShapeDtypeStruct(h.shape, h.dtype) for h in halves],
        scratch_shapes=[pltpu.SemaphoreType.DMA((n,)), pltpu.SemaphoreType.DMA((n,))],
    )(*halves)


def gather_small(v):
    def body(v_ref, o_ref, send, recv, local):
        x, y, c = _place()
        me = 4 * x + 2 * y + c
        own = pltpu.make_async_copy(v_ref, o_ref.at[me], local)
        own.start()
        cps = []
        for k in range(1, 8):
            fx, fy, fc = (k >> 2) & 1, (k >> 1) & 1, k & 1
            to = (x ^ fx if fx else x, y ^ fy if fy else y, c ^ fc if fc else c)
            cp = _remote(v_ref, o_ref.at[me], send.at[k - 1], recv.at[k - 1], to)
            cp.start()
            cps.append(cp)
        for k in range(1, 8):
            fx, fy, fc = (k >> 2) & 1, (k >> 1) & 1, k & 1
            px, py, pc = (x ^ fx if fx else x, y ^ fy if fy else y, c ^ fc if fc else c)
            cps[k - 1].wait_send()
            _remote(v_ref, o_ref.at[4 * px + 2 * py + pc], send.at[k - 1], recv.at[k - 1], (px, py, pc)).wait_recv()
        own.wait()

    return pl.pallas_call(
        body, name="gather_small_grads", in_specs=[ANY], out_specs=ANY,
        out_shape=jax.ShapeDtypeStruct((8,) + v.shape, v.dtype),
        scratch_shapes=[pltpu.SemaphoreType.DMA((7,)), pltpu.SemaphoreType.DMA((7,)), pltpu.SemaphoreType.DMA],
    )(v)


def _row_tile(rows, cols, nbuf_bytes):
    tm = _pick(rows, 512, 16)
    while tm * cols * nbuf_bytes * 2 > VMEM_BUDGET_V7X and tm % 32 == 0:
        tm //= 2
    return tm


def pair_sum(g, r, c_idx, tag):
    _, _, rows, cols = g.shape
    tm = _row_tile(rows, cols, 2 + 2 + 2)
    nb = rows // tm

    def body(c_ref, g_ref, r_ref, o_ref):
        o_ref[...] = (g_ref[...].astype(F32) + r_ref[...].astype(F32)).astype(o_ref.dtype)

    gs = pltpu.PrefetchScalarGridSpec(
        num_scalar_prefetch=1, grid=(N_CHIPS, nb),
        in_specs=[pl.BlockSpec((None, None, tm, cols), lambda j, i, c_ref: (j, c_ref[0], i, 0)),
                  pl.BlockSpec((None, tm, cols), lambda j, i, c_ref: (j, i, 0))],
        out_specs=pl.BlockSpec((None, tm, cols), lambda j, i, c_ref: (j, i, 0)))
    return pl.pallas_call(body, name="grad_pair_sum_" + tag, grid_spec=gs,
                          out_shape=jax.ShapeDtypeStruct(r.shape, BF16),
                          compiler_params=_params(("arbitrary", "arbitrary")))(c_idx, g, r)


def chip_sum(s, r, j_idx, tag):
    _, rows, cols = s.shape
    tm = _row_tile(rows, cols, 2 + 3 * 2 + 4)
    nb = rows // tm

    def body(j_ref, s_ref, r_ref, o_ref):
        t = s_ref[...].astype(F32)
        for k in range(3):
            t = t + r_ref[k].astype(F32)
        o_ref[...] = t

    gs = pltpu.PrefetchScalarGridSpec(
        num_scalar_prefetch=1, grid=(nb,),
        in_specs=[pl.BlockSpec((None, tm, cols), lambda i, j_ref: (j_ref[0], i, 0)),
                  pl.BlockSpec((3, tm, cols), lambda i, j_ref: (0, i, 0))],
        out_specs=pl.BlockSpec((tm, cols), lambda i, j_ref: (i, 0)))
    return pl.pallas_call(body, name="grad_chip_sum_" + tag, grid_spec=gs,
                          out_shape=jax.ShapeDtypeStruct((rows, cols), F32),
                          compiler_params=_params(("arbitrary",)))(j_idx, s, r)


def adamw(w, g, m, v, *, name):
    rows, cols = w.shape
    tm = _row_tile(rows, cols, 7 * 4)

    return rowwise(_adamw_math, [w, g, m, v], [], [(cols, F32)] * 3, tm=tm, name=name)


def _adamw_math(wb, gb, mb, vb):
    m2 = ADAM_B1 * mb + (1.0 - ADAM_B1) * gb
    v2 = ADAM_B2 * vb + (1.0 - ADAM_B2) * (gb * gb)
    m_hat = m2 / (1.0 - ADAM_B1 ** ADAM_STEP)
    v_hat = v2 / (1.0 - ADAM_B2 ** ADAM_STEP)
    delta = -ADAM_LR * (m_hat / (jnp.sqrt(v_hat) + ADAM_EPS) + ADAM_WD * wb)
    return delta, m2, v2


def adamw_shard(w, g_own, g_sib, m, v, c_idx, *, name):
    rows, cols = g_own.shape
    tm = _row_tile(rows, cols, 9 * 4)
    nb = rows // tm

    def body(c_ref, w_ref, go_ref, gs_ref, m_ref, v_ref, g_out, d_out, m_out, v_out):
        gb = jnp.where(pl.program_id(0) == c_ref[0], go_ref[...], gs_ref[...])
        delta, m2, v2 = _adamw_math(w_ref[...], gb, m_ref[...], v_ref[...])
        g_out[...] = gb
        d_out[...] = delta
        m_out[...] = m2
        v_out[...] = v2

    full = pl.BlockSpec((tm, cols), lambda h, i, c_ref: (h * nb + i, 0))
    half = pl.BlockSpec((tm, cols), lambda h, i, c_ref: (i, 0))
    gs = pltpu.PrefetchScalarGridSpec(num_scalar_prefetch=1, grid=(2, nb), in_specs=[full, half, half, full, full],
                                      out_specs=[full] * 4)
    return pl.pallas_call(body, name=name, grid_spec=gs, out_shape=[jax.ShapeDtypeStruct(w.shape, F32)] * 4,
                          compiler_params=_params(("arbitrary", "arbitrary")))(c_idx, w, g_own, g_sib, m, v)


def sum_devices(a):
    def body(a_ref, o_ref):
        t = a_ref[0]
        for k in range(1, 8):
            t = t + a_ref[k]
        o_ref[...] = t

    return pl.pallas_call(body, name="sum_small_grads", out_shape=jax.ShapeDtypeStruct(a.shape[1:], a.dtype))(a)


def _halves(w2d):
    r, c = w2d.shape
    return w2d.reshape(2, r // 2, c)


def kernel(x, p, ffn1_norm, ffn1_w_gate, ffn1_w_up, ffn1_w_down, mix_norm, w_in, q_a_norm, w_uq, kv_a_norm, w_ukv, na_rpb, w_branch_a, w_branch_b, w_out, ffn2_norm, ffn2_w_gate, ffn2_w_up, ffn2_w_down, pl_norm, w_pl, w_pl_gate, final_norm, loss_target, m_ffn1_norm, m_ffn1_w_gate, m_ffn1_w_up, m_ffn1_w_down, m_mix_norm, m_w_in, m_q_a_norm, m_w_uq, m_kv_a_norm, m_w_ukv, m_na_rpb, m_w_branch_a, m_w_branch_b, m_w_out, m_ffn2_norm, m_ffn2_w_gate, m_ffn2_w_up, m_ffn2_w_down, m_pl_norm, m_w_pl, m_w_pl_gate, m_final_norm, v_ffn1_norm, v_ffn1_w_gate, v_ffn1_w_up, v_ffn1_w_down, v_mix_norm, v_w_in, v_q_a_norm, v_w_uq, v_kv_a_norm, v_w_ukv, v_na_rpb, v_w_branch_a, v_w_branch_b, v_w_out, v_ffn2_norm, v_ffn2_w_gate, v_ffn2_w_up, v_ffn2_w_down, v_pl_norm, v_w_pl, v_w_pl_gate, v_final_norm):
    big = ["ffn1_w_gate", "ffn1_w_up", "ffn1_w_down", "w_in", "w_uq", "w_ukv", "w_branch_a", "w_branch_b", "w_out",
           "ffn2_w_gate", "ffn2_w_up", "ffn2_w_down", "w_pl", "w_pl_gate"]
    col_sharded = {"ffn1_w_gate", "ffn1_w_up", "w_in", "w_uq", "w_ukv", "w_branch_a", "w_branch_b", "ffn2_w_gate",
                   "ffn2_w_up", "w_pl"}
    small = ["ffn1_norm", "mix_norm", "q_a_norm", "kv_a_norm", "na_rpb", "ffn2_norm", "pl_norm", "final_norm"]
    order = ["ffn1_norm", "ffn1_w_gate", "ffn1_w_up", "ffn1_w_down", "mix_norm", "w_in", "q_a_norm", "w_uq",
             "kv_a_norm", "w_ukv", "na_rpb", "w_branch_a", "w_branch_b", "w_out", "ffn2_norm", "ffn2_w_gate",
             "ffn2_w_up", "ffn2_w_down", "pl_norm", "w_pl", "w_pl_gate", "final_norm"]
    env = dict(locals())
    W = {n: env[n] for n in order}
    Mo = {n: env["m_" + n] for n in order}
    Vo = {n: env["v_" + n] for n in order}

    xs = x[0]
    S, D = xs.shape
    tgt = loss_target[0]
    ps = p[0, 0]
    NAW = w_branch_a.shape[1]
    MLAW = w_branch_b.shape[1]
    NH, MH = NAW // HEAD_DIM, MLAW // HEAD_DIM
    QR, KVR = w_uq.shape[1], w_ukv.shape[1]
    F = ffn1_w_down.shape[1] * N_CHIPS
    cx, cy, cc = _place()
    c_idx = jnp.reshape(cc, (1,)).astype(jnp.int32)
    j_idx = jnp.reshape(2 * cx + cy, (1,)).astype(jnp.int32)

    me_chip = 2 * cx + cy
    groups = [["ffn1_w_gate"], ["ffn1_w_up"], ["ffn1_w_down"], ["w_in"],
              ["w_uq", "w_ukv", "w_branch_a", "w_branch_b", "w_out"],
              ["ffn2_w_gate", "ffn2_w_up", "ffn2_w_down"], ["w_pl", "w_pl_gate"]]
    started, tokens = [], []
    for g, members in enumerate(groups):
        shards = [_halves(W[n][0].astype(BF16)) for n in members]
        landings = [lax.empty((N_CHIPS,) + s.shape, BF16) for s in shards]
        sems, shards_thru, landings_thru, token = gather_start(shards, landings, tokens[-1:], str(g))
        started.append((sems, shards_thru, landings_thru))
        tokens.append(token)
    gathered = {}

    def arrive(n, after):
        g = [n in members for members in groups].index(True)
        sems, shards_thru, landings_thru = started[g]
        after = list(after) if isinstance(after, (list, tuple)) else [after]
        shards_out, landed = gather_wait(sems, shards_thru, landings_thru, after, str(g))
        for name, full, own in zip(groups[g], gather_forward(landed, str(g)), shards_out):
            gathered[name] = lax.dynamic_update_slice(full, own[None], (me_chip, 0, 0, 0))

    def stacked(n, after=None):
        if n not in gathered:
            arrive(n, after)
        g = gathered[n]
        return g.reshape(N_CHIPS, 2 * g.shape[2], g.shape[3])

    def plain(n, after=None):
        if n in col_sharded:
            st = stacked(n, after)
            return st.transpose(1, 0, 2).reshape(st.shape[1], N_CHIPS * st.shape[2])
        if n not in gathered:
            arrive(n, after)
        g = gathered[n]
        return g.reshape(N_CHIPS * 2 * g.shape[2], g.shape[3])

    n_na = 3 * NAW
    n_front = n_na + QR + KVR
    n_in = n_front + MLA_ROPE + 2 * D
    off_ql, off_kvl, off_kr = 2 * D, 2 * D + QR, 2 * D + QR + KVR
    kr_w = 2 * LANES
    rest_w = off_kr + kr_w
    rest_ranges = [(n_front + MLA_ROPE, n_in), (n_na, n_front), (n_front, n_front + MLA_ROPE)]

    def shard_cols(st, lo, hi):
        nb, parts = st.shape[2], []
        while lo < hi:
            j = lo // nb
            end = min(hi, (j + 1) * nb)
            parts.append(st[j][:, lo - j * nb:end - j * nb])
            lo = end
        return parts

    def w_in_shards(g_na, g_rest):
        pieces = [(0, n_na, g_na, 0)]
        o = 0
        for lo, hi in rest_ranges:
            pieces.append((lo, hi, g_rest, o))
            o += hi - lo
        nb, shards = n_in // N_CHIPS, []
        for j in range(N_CHIPS):
            parts = []
            for lo, hi, src, o in sorted(pieces):
                a, b = max(lo, j * nb), min(hi, (j + 1) * nb)
                if a < b:
                    parts.append(src[:, o + a - lo:o + b - lo])
            shards.append(jnp.concatenate(parts, axis=1))
        return jnp.stack(shards)

    pos = jnp.arange(S, dtype=F32)
    inv_freq = 1.0 / (ROPE_THETA ** (jnp.arange(0, MLA_ROPE, 2, dtype=F32) / MLA_ROPE))
    ang = pos[:, None] * inv_freq[None, :]
    zpad = jnp.zeros((S, LANES - MLA_ROPE), F32)
    cos_t = jnp.concatenate([jnp.cos(ang), jnp.cos(ang), zpad], axis=1)
    sin_t = jnp.concatenate([-jnp.sin(ang), jnp.sin(ang), zpad], axis=1)

    def ffn_fwd(h, norm_g, tag, pre, after=()):
        n = norm_fwd(h, norm_g, name=f"{tag}_norm")
        g = mm(n, stacked(pre + "_w_gate", [n, *after]), name=f"{tag}_gate", b_stack=True)
        u = mm(n, stacked(pre + "_w_up", g), name=f"{tag}_up", b_stack=True)
        a = swiglu_fwd(g, u, name=f"{tag}_act")
        h_out = mm(a, plain(pre + "_w_down", a), name=f"{tag}_down", res=h, alpha=0.5)
        return h_out, (n, g, u, a)

    def ffn_bwd(h, norm_g, saved, dh, dh_half, tag, pre, last, after=()):
        n, g, u, a = saved
        G[pre + "_w_down"] = mm(a, dh_half, name=f"{tag}_dw_down", ta=True, out_dtype=BF16)
        da = mm(dh_half, plain(pre + "_w_down"), name=f"{tag}_da", tb=True, after=after)
        dg, du = swiglu_bwd(g, u, da, name=f"{tag}_dact")
        G[pre + "_w_gate"] = mm(n, dg, name=f"{tag}_dw_gate", ta=True, out_dtype=BF16, out_stack=True)
        G[pre + "_w_up"] = mm(n, du, name=f"{tag}_dw_up", ta=True, out_dtype=BF16, out_stack=True)
        token = reduce_start([pre + "_w_down", pre + "_w_gate", pre + "_w_up"], tag)
        dn = mm(dg, stacked(pre + "_w_gate"), name=f"{tag}_dn_gate", tb=True, b_stack=True, after=[token])
        dn = mm(du, stacked(pre + "_w_up"), name=f"{tag}_dn_up", tb=True, b_stack=True, res=dn)
        return norm_bwd(h, norm_g, dn, name=f"{tag}_dnorm", res=dh, bf16_alpha=None if last else 1.0)

    bias = na_bias(na_rpb[0], after=tokens[-1:])
    h1, ffn1_saved = ffn_fwd(xs, ffn1_norm, "ffn1", "ffn1", after=[bias, tokens[-1]])
    u_mix = norm_fwd(h1, mix_norm, name="mix_norm")
    win_st = stacked("w_in", u_mix)
    w_na = jnp.concatenate(shard_cols(win_st, 0, n_na), axis=1)
    w_rest = jnp.concatenate([p_ for lo, hi in rest_ranges for p_ in shard_cols(win_st, lo, hi)]
                             + [jnp.zeros((D, kr_w - MLA_ROPE), BF16)], axis=1)
    z_na = mm(u_mix, w_na, name="mix_in_na", out_dtype=BF16)
    z = mm(u_mix, w_rest, name="mix_in_rest")
    o_a = na_fwd(z_na, bias, NH, S)
    c_q = norm_fwd((z, QR, off_ql // QR), q_a_norm, name="q_a_norm")
    c_kv = norm_fwd((z, KVR, off_kvl // KVR), kv_a_norm, name="kv_a_norm")
    wuq = plain("w_uq", c_kv).reshape(QR, MH, MLA_QK)
    wuq_n = wuq[:, :, :MLA_NOPE].reshape(QR, MH * MLA_NOPE)
    wuq_r = jnp.pad(wuq[:, :, MLA_NOPE:], ((0, 0), (0, 0), (0, LANES - MLA_ROPE))).reshape(QR, MH * LANES)
    wukv = plain("w_ukv").reshape(KVR, MH, 2, HEAD_DIM)
    wuk = wukv[:, :, 0].reshape(KVR, MH * HEAD_DIM)
    wuv = wukv[:, :, 1].reshape(KVR, MH * HEAD_DIM)
    q_n = mm(c_q, wuq_n, name="mla_q_nope", out_dtype=BF16)
    q_r = rope(mm(c_q, wuq_r, name="mla_q_rope"), cos_t, sin_t, name="rope_q", out_dtype=BF16)
    k_n = mm(c_kv, wuk, name="mla_k_nope", out_dtype=BF16)
    v_m = mm(c_kv, wuv, name="mla_v", out_dtype=BF16)
    k_r = rope((z, LANES, off_kr // LANES), cos_t, sin_t, name="rope_k", out_dtype=BF16)
    o_b, lse = mla_fwd(q_n, q_r, k_n, v_m, k_r, MH, S)
    y_a = mm(o_a, stacked("w_branch_a"), name="branch_a", b_stack=True)
    y_b = mm(o_b, stacked("w_branch_b"), name="branch_b", b_stack=True)
    z_ga, z_gb = (z, D, 0), (z, D, 1)
    merged = rowwise(lambda ga, gb, ya, yb: _sig(ga) * ya + _sig(gb) * yb, [z_ga, z_gb, y_a, y_b], [], [(D, BF16)],
                     tm=256, name="merge")[0]
    h2 = mm(merged, plain("w_out"), name="mix_out", res=h1)
    h3, ffn2_saved = ffn_fwd(h2, ffn2_norm, "ffn2", "ffn2")
    n4 = norm_fwd(h3, pl_norm, name="pl_norm")
    pg_pre = mm(n4, plain("w_pl_gate", n4), name="pl_gate")
    pe = mm(ps, stacked("w_pl"), name="pl_embed", b_stack=True)

    def tail(h3b, pgb, peb, tb_, fg):
        pg = _sig(pgb)
        h4 = h3b + pg * peb
        r = _rstd(h4)
        xh = h4 * r
        err = xh * fg - tb_
        loss_rows = jnp.mean(err * err, axis=-1, keepdims=True)
        dy = err * (1.0 / D)
        dxh = dy * fg
        dh4 = r * (dxh - xh * jnp.mean(dxh * xh, axis=-1, keepdims=True))
        loss_part = jnp.broadcast_to(0.5 * jnp.sum(loss_rows, axis=0, keepdims=True), (1, LANES))
        return (dh4, dh4 * peb * pg * (1.0 - pg), dh4 * pg, loss_part, jnp.sum(dy * xh, axis=0, keepdims=True))

    dh4, dpg_pre, dpe, loss_part, g_final = rowwise(
        tail, [h3, pg_pre, pe, tgt], [final_norm.reshape(1, D)], [(D, F32), (D, BF16), (D, BF16)],
        accs=[(1, LANES), (1, D)], tm=128, name="loss_tail")
    loss = lax.psum(loss_part[0, 0], ("x", "y", "c"))

    G = {}
    pending = []

    def four(g):
        if g.ndim == 2:
            return g.reshape(N_CHIPS, 2, g.shape[0] // (2 * N_CHIPS), g.shape[1])
        return g.reshape(N_CHIPS, 2, g.shape[1] // 2, g.shape[2])

    def reduce_start(names, tag):
        g4 = [four(G[n]) for n in names]
        sums = [pair_sum(a, b, c_idx, n) for n, a, b in zip(names, g4, pair_exchange(g4, tag))]
        lands = [lax.empty((N_CHIPS - 1,) + s_.shape[1:], BF16) for s_ in sums]
        sems, sums, lands, token = scatter_start(sums, lands, tag)
        pending.append((names, tag, sems, sums, lands))
        return token

    def reduce_finish(entry, after):
        names, tag, sems, sums, lands = entry
        sums, got = scatter_wait(sems, sums, lands, after, tag)
        halves = [chip_sum(a, b, j_idx, n) for n, a, b in zip(names, sums, got)]
        done = []
        for n, own, sib in zip(names, halves, half_exchange(halves, tag)):
            shp = W[n].shape
            two_d = lambda a_: a_.reshape(shp[1], shp[2])
            out = adamw_shard(two_d(W[n]), own, sib, two_d(Mo[n]), two_d(Vo[n]), c_idx, name="adamw_" + n)
            grads[n], delta[n], new_m[n], new_v[n] = [o.reshape(shp) for o in out]
            done.append(out[0])
        return done

    G["w_pl"] = mm(ps, dpe, name="pl_dw_embed", ta=True, out_dtype=BF16, out_stack=True)
    G["w_pl_gate"] = mm(n4, dpg_pre, name="pl_dw_gate", ta=True, out_dtype=BF16)
    dn4 = mm(dpg_pre, plain("w_pl_gate"), name="pl_dn", tb=True)
    dh3, dh3_half, g_pl = norm_bwd(h3, pl_norm, dn4, name="pl_dnorm", res=dh4, bf16_alpha=0.5)
    token = reduce_start(["w_pl", "w_pl_gate"], "pl")
    dh2, dh2_b, g_ffn2 = ffn_bwd(h2, ffn2_norm, ffn2_saved, dh3, dh3_half, "ffn2", "ffn2", last=False, after=[token])

    G["w_out"] = mm(merged, dh2_b, name="mix_dw_out", ta=True, out_dtype=BF16)
    dmerged = mm(dh2_b, plain("w_out"), name="mix_dmerged", tb=True)

    def merge_bwd(ga, gb, ya, yb, dm):
        sa, sb = _sig(ga), _sig(gb)
        dgates = jnp.concatenate([dm * ya * sa * (1.0 - sa), dm * yb * sb * (1.0 - sb)], axis=1)
        return dm * sa, dm * sb, dgates

    dy_a, dy_b, dz_rest = rowwise(merge_bwd, [z_ga, z_gb, y_a, y_b, dmerged], [],
                                  [(D, BF16), (D, BF16), (2 * D, BF16, 0)], tm=256, name="merge_bwd",
                                  into=(None, rest_w))
    G["w_branch_a"] = mm(o_a, dy_a, name="branch_a_dw", ta=True, out_dtype=BF16, out_stack=True)
    G["w_branch_b"] = mm(o_b, dy_b, name="branch_b_dw", ta=True, out_dtype=BF16, out_stack=True)
    do_a = mm(dy_a, stacked("w_branch_a"), name="branch_a_dx", tb=True, b_stack=True)
    do_b = mm(dy_b, stacked("w_branch_b"), name="branch_b_dx", tb=True, b_stack=True)
    dq_na, dk_na, dv_na, dbias = na_bwd(z_na, bias, do_a, NH, S)
    g_rpb = na_rpb_grad(dbias)
    dq_n, dq_rr, dk_n, dv_m, dk_rr = mla_bwd(q_n, q_r, k_n, v_m, k_r, lse, do_b, MH, S)
    dq_r = rope(dq_rr, cos_t, -sin_t, name="rope_q_bwd", out_dtype=BF16)
    dz_rest = rope(dk_rr, cos_t, -sin_t, name="rope_k_bwd", out_dtype=BF16, into=(dz_rest, rest_w),
                   cb=off_kr // kr_w, zero_cols=kr_w - LANES)
    gw_uq_n = mm(c_q, dq_n, name="mla_dw_q_nope", ta=True, out_dtype=BF16)
    gw_uq_r = mm(c_q, dq_r, name="mla_dw_q_rope", ta=True, out_dtype=BF16)
    dc_q = mm(dq_n, wuq_n, name="mla_dcq_nope", tb=True)
    dc_q = mm(dq_r, wuq_r, name="mla_dcq_rope", tb=True, res=dc_q)
    gw_uk = mm(c_kv, dk_n, name="mla_dw_k", ta=True, out_dtype=BF16)
    gw_uv = mm(c_kv, dv_m, name="mla_dw_v", ta=True, out_dtype=BF16)
    dc_kv = mm(dk_n, wuk, name="mla_dckv_k", tb=True)
    dc_kv = mm(dv_m, wuv, name="mla_dckv_v", tb=True, res=dc_kv)
    dz_rest, g_qa = norm_bwd((z, QR, off_ql // QR), q_a_norm, dc_q, name="q_a_dnorm", want_f32=False, bf16_alpha=1.0,
                             into=(dz_rest, rest_w), cb=off_ql // QR)
    dz_rest, g_kva = norm_bwd((z, KVR, off_kvl // KVR), kv_a_norm, dc_kv, name="kv_a_dnorm", want_f32=False,
                              bf16_alpha=1.0, into=(dz_rest, rest_w), cb=off_kvl // KVR)
    dz_na = jnp.concatenate([dq_na, dk_na, dv_na], axis=1)
    gw_na = mm(u_mix, dz_na, name="mix_dw_in_na", ta=True, out_dtype=BF16)
    gw_rest = mm(u_mix, dz_rest, name="mix_dw_in_rest", ta=True, out_dtype=BF16)

    def to_stack(g2d):
        k, n = g2d.shape
        return g2d.reshape(k, N_CHIPS, n // N_CHIPS).transpose(1, 0, 2)

    gw_uq = jnp.concatenate([gw_uq_n.reshape(QR, MH, MLA_NOPE), gw_uq_r.reshape(QR, MH, LANES)[:, :, :MLA_ROPE]],
                            axis=2).reshape(QR, MH * MLA_QK)
    G["w_uq"] = to_stack(gw_uq)
    gw_ukv = jnp.stack([gw_uk.reshape(KVR, MH, HEAD_DIM), gw_uv.reshape(KVR, MH, HEAD_DIM)], axis=2)
    G["w_ukv"] = to_stack(gw_ukv.reshape(KVR, MH * 2 * HEAD_DIM))
    token_mix = reduce_start(["w_out", "w_branch_a", "w_branch_b", "w_uq", "w_ukv"], "mix")
    G["w_in"] = w_in_shards(gw_na, gw_rest)
    token_win = reduce_start(["w_in"], "win")
    du_mix = mm(dz_na, w_na, name="mix_du_na", tb=True, after=[token_mix, token_win])
    du_mix = mm(dz_rest, w_rest, name="mix_du_rest", tb=True, res=du_mix)
    dh1, dh1_half, g_mix = norm_bwd(h1, mix_norm, du_mix, name="mix_dnorm", res=dh2, bf16_alpha=0.5)
    grad_x, g_ffn1 = ffn_bwd(xs, ffn1_norm, ffn1_saved, dh1, dh1_half, "ffn1", "ffn1", last=True)

    small_g = {"ffn1_norm": g_ffn1, "mix_norm": g_mix, "q_a_norm": g_qa, "kv_a_norm": g_kva, "na_rpb": g_rpb,
               "ffn2_norm": g_ffn2, "pl_norm": g_pl, "final_norm": g_final}
    sizes = [int(np.prod(W[n].shape)) for n in small]
    total = sum(sizes)
    padded = -(-total // (8 * LANES)) * (8 * LANES)

    def pack(parts):
        flat = jnp.concatenate([jnp.reshape(parts[n], (-1,)).astype(F32) for n in small]
                               + [jnp.zeros((padded - total,), F32)])
        return flat.reshape(padded // LANES, LANES)

    def unpack(a):
        flat, out, o = a.reshape(-1), {}, 0
        for n, sz in zip(small, sizes):
            out[n] = flat[o:o + sz].reshape(W[n].shape)
            o += sz
        return out

    g_small = sum_devices(gather_small(pack(small_g)))
    d_small, m_small, v_small = adamw(pack(W), g_small, pack(Mo), pack(Vo), name="adamw_small")
    grads = unpack(g_small)
    delta, new_m, new_v = unpack(d_small), unpack(m_small), unpack(v_small)

    after = [grad_x]
    for entry in pending:
        after = reduce_finish(entry, after)

    return (loss, grad_x[None], *[grads[n] for n in order], *[delta[n] for n in order],
            *[new_m[n] for n in order], *[new_v[n] for n in order])
```

```python
import functools

import numpy as np
import jax
import jax.numpy as jnp
from jax import lax
from jax.experimental import pallas as pl
from jax.experimental.pallas import tpu as pltpu

F32 = jnp.float32
BF16 = jnp.bfloat16

VMEM_LIMIT_V7X = 56 * 1024 * 1024
VMEM_BUDGET_V7X = 40 * 1024 * 1024
LANES = 128

GRID_W = 64
NA_WIN_ROWS = 8
NA_WIN_COLS = 16
HEAD_DIM = 128
MLA_NOPE = 128
MLA_ROPE = 64
MLA_QK = MLA_NOPE + MLA_ROPE
ROPE_THETA = 10000.0
NORM_EPS = 1e-6
NEG_INF = -1e30
N_CHIPS = 4

ADAM_LR = 0.001
ADAM_B1 = 0.9
ADAM_B2 = 0.999
ADAM_EPS = 1e-08
ADAM_WD = 0.01
ADAM_STEP = 10

MESH = pl.DeviceIdType.MESH
ANY = pl.BlockSpec(memory_space=pl.ANY)


def _params(sem=None):
    return pltpu.CompilerParams(dimension_semantics=sem, vmem_limit_bytes=VMEM_LIMIT_V7X)


def _pick(n, target, align):
    best = None
    t = align
    while t <= min(n, target):
        if n % t == 0:
            best = t
        t += align
    return n if best is None else best


def mm(a, b, *, name, ta=False, tb=False, out_dtype=F32, res=None, alpha=1.0, b_stack=False, out_stack=False,
       exact=False, after=()):
    K, M = (a.shape if ta else a.shape[::-1])
    if b_stack:
        nst = b.shape[0]
        if tb:
            N, kb = b.shape[1], b.shape[2]
            Kb, nb = nst * kb, None
        else:
            Kb, nb = b.shape[1], b.shape[2]
            N = nst * nb
    else:
        N, Kb = (b.shape if tb else b.shape[::-1])
    assert K == Kb, (a.shape, b.shape, ta, tb)
    if out_stack:
        assert N % N_CHIPS == 0
    n_unit = N // N_CHIPS if out_stack else (nb if (b_stack and not tb) else N)
    k_unit = kb if (b_stack and tb) else K
    tn = _pick(n_unit, 512, LANES) if n_unit % 512 == 0 or n_unit <= 512 else _pick(n_unit, 1536, LANES)
    tk = _pick(k_unit, 2048, LANES)
    tm = _pick(M, 1024, LANES if ta else 16)
    isz = lambda t: jnp.dtype(t.dtype).itemsize
    osz = jnp.dtype(out_dtype).itemsize

    def vmem(tm_):
        return (2 * tm_ * tk * isz(a) + 2 * tk * tn * isz(b) + 2 * tm_ * tn * osz + tm_ * tn * 4
                + (2 * tm_ * tn * isz(res) if res is not None else 0))

    while vmem(tm) > VMEM_BUDGET_V7X and tm % 2 == 0 and (tm // 2) % (LANES if ta else 16) == 0:
        tm //= 2
    nk = K // tk
    gm, gn = M // tm, N // tn

    a_spec = pl.BlockSpec((tk, tm), lambda i, j, k: (k, i)) if ta else pl.BlockSpec((tm, tk), lambda i, j, k: (i, k))
    if b_stack and not tb:
        q = nb // tn
        b_spec = pl.BlockSpec((None, tk, tn), lambda i, j, k: (j // q, k, j % q))
    elif b_stack and tb:
        q = kb // tk
        b_spec = pl.BlockSpec((None, tn, tk), lambda i, j, k: (k // q, j, k % q))
    elif tb:
        b_spec = pl.BlockSpec((tn, tk), lambda i, j, k: (j, k))
    else:
        b_spec = pl.BlockSpec((tk, tn), lambda i, j, k: (k, j))
    if out_stack:
        qo = (N // N_CHIPS) // tn
        o_spec = pl.BlockSpec((None, tm, tn), lambda i, j, k: (j // qo, i, j % qo))
        o_shape = jax.ShapeDtypeStruct((N_CHIPS, M, N // N_CHIPS), out_dtype)
    else:
        o_spec = pl.BlockSpec((tm, tn), lambda i, j, k: (i, j))
        o_shape = jax.ShapeDtypeStruct((M, N), out_dtype)
    dims = (((0 if ta else 1,), (1 if tb else 0,)), ((), ()))
    has_res = res is not None

    def body(*refs):
        a_ref, b_ref = refs[:2]
        r_ref = refs[2] if has_res else None
        o_ref, acc_ref = refs[-2:]
        k = pl.program_id(2)
        if exact:
            part = lax.dot_general(a_ref[...], b_ref[...], dims, preferred_element_type=F32,
                                   precision=lax.Precision.HIGHEST)
        else:
            part = lax.dot_general(a_ref[...].astype(BF16), b_ref[...].astype(BF16), dims,
                                   preferred_element_type=F32)

        def finish(total):
            if alpha != 1.0:
                total = total * alpha
            if has_res:
                total = total + r_ref[...].astype(F32)
            o_ref[...] = total.astype(out_dtype)

        if nk == 1:
            finish(part)
        else:
            @pl.when(k == 0)
            def _():
                acc_ref[...] = part

            @pl.when(jnp.logical_and(k > 0, k < nk - 1))
            def _():
                acc_ref[...] += part

            @pl.when(k == nk - 1)
            def _():
                finish(acc_ref[...] + part)

    in_specs = [a_spec, b_spec]
    args = [a, b]
    if has_res:
        in_specs.append(pl.BlockSpec((tm, tn), lambda i, j, k: (i, j)))
        args.append(res)
    in_specs += [ANY] * len(after)
    args += list(after)
    return pl.pallas_call(
        body, name=name, grid=(gm, gn, nk), in_specs=in_specs, out_specs=o_spec, out_shape=o_shape,
        scratch_shapes=[pltpu.VMEM((tm, tn) if nk > 1 else (8, LANES), F32)],
        compiler_params=_params(("parallel", "parallel", "arbitrary")),
    )(*args)


def rowwise(fn, rows, consts, outs, accs=(), *, tm, name, tn=None, into=None):
    rows = [r if isinstance(r, tuple) else (r, r.shape[1], 0) for r in rows]
    S = rows[0][0].shape[0]
    tm = _pick(S, tm, 16)
    nrow, ncon, nout = len(rows), len(consts), len(outs)
    outs = [o if len(o) == 3 else (o[0], o[1], None) for o in outs]
    if tn is None:
        grid = (S // tm,)
        in_specs = [pl.BlockSpec((tm, w), functools.partial(lambda i, cb: (i, cb), cb=cb)) for _, w, cb in rows]
        in_specs += [pl.BlockSpec(c.shape, lambda i: (0, 0)) for c in consts]
        out_specs = [pl.BlockSpec((tm, n), functools.partial(lambda i, cb: (i, cb), cb=cb or 0)) for n, _, cb in outs]
        out_specs += [pl.BlockSpec(s, lambda i: (0, 0)) for s in accs]
        sem = ("arbitrary",)
    else:
        assert not accs
        N = rows[0][1]
        grid = (S // tm, N // tn)
        in_specs = [pl.BlockSpec((tm, tn), lambda i, j: (i, j)) for _ in rows]
        in_specs += [pl.BlockSpec(c.shape, lambda i, j: (0, 0)) for c in consts]
        out_specs = [pl.BlockSpec((tm, tn), lambda i, j: (i, j)) for _ in outs]
        sem = ("parallel", "parallel")
    out_shape = [jax.ShapeDtypeStruct((S, n if cb is None else into[1]), dt) for n, dt, cb in outs]
    out_shape += [jax.ShapeDtypeStruct(s, F32) for s in accs]
    extra, aliases = [], {}
    if into is not None and into[0] is not None:
        extra = [into[0]]
        aliases = {nrow + ncon: [cb is not None for _, _, cb in outs].index(True)}

    def body(*refs):
        vals = fn(*[r[...] for r in refs[:nrow + ncon]])
        if not isinstance(vals, (tuple, list)):
            vals = (vals,)
        o_refs = refs[nrow + ncon + len(extra):]
        for o_ref, v in zip(o_refs[:nout], vals[:nout]):
            o_ref[...] = v.astype(o_ref.dtype)
        if accs:
            first = pl.program_id(0) == 0

            def accumulate(a_ref, v):
                @pl.when(first)
                def _():
                    a_ref[...] = v

                @pl.when(jnp.logical_not(first))
                def _():
                    a_ref[...] += v

            for a_ref, v in zip(o_refs[nout:], vals[nout:]):
                accumulate(a_ref, v.astype(F32))

    return pl.pallas_call(
        body, name=name, grid=grid, in_specs=in_specs + [ANY] * len(extra), out_specs=out_specs, out_shape=out_shape,
        input_output_aliases=aliases, compiler_params=_params(sem),
    )(*[r[0] for r in rows], *consts, *extra)


def _rstd(x):
    return lax.rsqrt(jnp.mean(x * x, axis=-1, keepdims=True) + NORM_EPS)


def norm_fwd(x, g, *, name, tm=256):
    w = x[1] if isinstance(x, tuple) else x.shape[1]

    def fn(xb, gb):
        return (xb * _rstd(xb)) * gb

    return rowwise(fn, [x], [g], [(w, BF16)], tm=tm, name=name)[0]


def norm_bwd(x, g, dn, *, name, res=None, want_f32=True, bf16_alpha=None, tm=256, into=None, cb=None):
    w = x[1] if isinstance(x, tuple) else x.shape[1]
    has_res = res is not None

    def fn(*blocks):
        if has_res:
            xb, dnb, rb, gb = blocks
        else:
            xb, dnb, gb = blocks
        r = _rstd(xb)
        xh = xb * r
        dxh = dnb * gb
        dx = r * (dxh - xh * jnp.mean(dxh * xh, axis=-1, keepdims=True))
        if has_res:
            dx = dx + rb
        out = []
        if want_f32:
            out.append(dx)
        if bf16_alpha is not None:
            out.append(dx * bf16_alpha if bf16_alpha != 1.0 else dx)
        out.append(jnp.sum(dnb * xh, axis=0, keepdims=True))
        return tuple(out)

    outs = ([(w, F32)] if want_f32 else []) + ([(w, BF16, cb)] if bf16_alpha is not None else [])
    rows = [x, dn] + ([res] if has_res else [])
    return rowwise(fn, rows, [g], outs, accs=[(1, w)], tm=tm, name=name, into=into)


def _sig(x):
    return jax.nn.sigmoid(x)


def swiglu_fwd(g, u, *, name):
    return rowwise(lambda gb, ub: gb * _sig(gb) * ub, [g, u], [], [(g.shape[1], BF16)], tm=256, name=name,
                   tn=_pick(g.shape[1], 1536, LANES))[0]


def swiglu_bwd(g, u, da, *, name):
    def fn(gb, ub, dab):
        s = _sig(gb)
        return dab * ub * (s + gb * s * (1.0 - s)), dab * (gb * s)

    n = g.shape[1]
    return rowwise(fn, [g, u, da], [], [(n, BF16), (n, BF16)], tm=256, name=name, tn=_pick(n, 1536, LANES))


def rope(x, cos, sin_signed, *, name, out_dtype, into=None, cb=None, zero_cols=0):
    w = x[1] if isinstance(x, tuple) else x.shape[1]
    half = MLA_ROPE // 2

    def fn(xb, cb, sb):
        lane = lax.broadcasted_iota(jnp.int32, cb.shape, 1)
        outs = []
        for hb in range(w // LANES):
            blk = xb[:, hb * LANES:(hb + 1) * LANES]
            partner = jnp.where(lane < half, pltpu.roll(blk, LANES - half, 1), pltpu.roll(blk, half, 1))
            outs.append(blk * cb + partner * sb)
        if zero_cols:
            outs.append(jnp.zeros((xb.shape[0], zero_cols), xb.dtype))
        return outs[0] if len(outs) == 1 else jnp.concatenate(outs, axis=1)

    return rowwise(fn, [x, cos, sin_signed], [], [(w + zero_cols, out_dtype, cb)], tm=256, name=name, into=into)[0]


def _na_tables():
    cols = np.arange(GRID_W)
    kw = NA_WIN_COLS
    dc = np.clip(cols[None, :] - cols[:, None], -(kw - 1), kw - 1) + (kw - 1)
    onehot = np.zeros((LANES, GRID_W * GRID_W), np.float32)
    onehot[dc.reshape(-1), np.arange(GRID_W * GRID_W)] = 1.0
    col_start = np.clip(cols - kw // 2, 0, GRID_W - kw)
    mask = (cols[None, :] >= col_start[:, None]) & (cols[None, :] < col_start[:, None] + kw)
    return onehot, np.where(mask, 0.0, NEG_INF).astype(np.float32)


def na_bias(rpb, after=()):
    H = rpb.shape[0]
    nr, kh = 2 * NA_WIN_ROWS - 1, NA_WIN_ROWS
    onehot, maskb = _na_tables()
    rp = jnp.pad(rpb.reshape(H * nr, 2 * NA_WIN_COLS - 1), ((0, 0), (0, LANES - (2 * NA_WIN_COLS - 1))))
    t1 = mm(rp, jnp.asarray(onehot), name="na_bias_table", exact=True, after=after).reshape(H, nr, GRID_W, GRID_W)
    t1 = t1 + jnp.asarray(maskb)[None, None]
    per_t = [jnp.stack([t1[:, i - t + kh - 1] for i in range(kh)], axis=2) for t in range(kh)]
    return jnp.stack(per_t, axis=1).reshape(H, kh, GRID_W, kh * GRID_W)


def na_rpb_grad(db):
    H = db.shape[0]
    nr, kh = 2 * NA_WIN_ROWS - 1, NA_WIN_ROWS
    onehot, _ = _na_tables()
    db = db.reshape(H, kh, GRID_W, kh, GRID_W)
    per_dr = []
    for dri in range(nr):
        terms = [db[:, t, :, dri - (kh - 1) + t, :] for t in range(kh) if 0 <= dri - (kh - 1) + t < kh]
        per_dr.append(functools.reduce(jnp.add, terms))
    dt1 = jnp.stack(per_dr, axis=1).reshape(H * nr, GRID_W * GRID_W)
    g = mm(dt1, jnp.asarray(onehot), name="na_rpb_grad", tb=True, exact=True)
    return g[:, :2 * NA_WIN_COLS - 1].reshape(H, nr, 2 * NA_WIN_COLS - 1)


def _na_first_row(r, rows):
    return jnp.clip(r - NA_WIN_ROWS // 2, 0, rows - NA_WIN_ROWS)


NA_ROWS_PER_STEP = 8


def _na_probs(q, k_ref, b_ref, r, rows):
    first = _na_first_row(r, rows)
    start = pl.multiple_of(first * GRID_W, GRID_W)
    k = k_ref[pl.ds(start, NA_WIN_ROWS * GRID_W), :]
    s = lax.dot_general(q, k, (((1,), (1,)), ((), ())), preferred_element_type=F32)
    s = s * (HEAD_DIM ** -0.5) + b_ref[r - first]
    m = jnp.max(s, axis=-1, keepdims=True)
    e = jnp.exp(s - m)
    return k, e / jnp.sum(e, axis=-1, keepdims=True), start, r - first


def na_fwd(z, bias, H, S):
    rows = S // GRID_W
    nkeys = NA_WIN_ROWS * GRID_W
    rb = _pick(rows, NA_ROWS_PER_STEP, 1)

    def body(q_ref, k_ref, v_ref, b_ref, o_ref):
        for j in range(rb):
            r = pl.program_id(1) * rb + j
            rows_j = pl.ds(j * GRID_W, GRID_W)
            _, p, start, _ = _na_probs(q_ref[rows_j, :], k_ref, b_ref, r, rows)
            v = v_ref[pl.ds(start, nkeys), :]
            o_ref[rows_j, :] = jnp.dot(p.astype(BF16), v, preferred_element_type=F32).astype(o_ref.dtype)

    return pl.pallas_call(
        body, name="na_fwd", grid=(H, rows // rb),
        in_specs=[pl.BlockSpec((rb * GRID_W, HEAD_DIM), lambda h, i: (i, h)),
                  pl.BlockSpec((S, HEAD_DIM), lambda h, i: (0, H + h)),
                  pl.BlockSpec((S, HEAD_DIM), lambda h, i: (0, 2 * H + h)),
                  pl.BlockSpec((None, NA_WIN_ROWS, GRID_W, nkeys), lambda h, i: (h, 0, 0, 0))],
        out_specs=pl.BlockSpec((rb * GRID_W, HEAD_DIM), lambda h, i: (i, h)),
        out_shape=jax.ShapeDtypeStruct((S, H * HEAD_DIM), BF16),
        compiler_params=_params(("parallel", "arbitrary")),
    )(z, z, z, bias)


def na_bwd(z, bias, do, H, S):
    rows = S // GRID_W
    nkeys = NA_WIN_ROWS * GRID_W
    rb = _pick(rows, NA_ROWS_PER_STEP, 1)
    tn_dims = (((0,), (0,)), ((), ()))

    def body(q_ref, k_ref, v_ref, b_ref, do_ref, dq_ref, dk_ref, dv_ref, db_ref, dk_acc, dv_acc):
        i = pl.program_id(1)

        @pl.when(i == 0)
        def _():
            dk_acc[...] = jnp.zeros_like(dk_acc)
            dv_acc[...] = jnp.zeros_like(dv_acc)
            db_ref[...] = jnp.zeros_like(db_ref)

        for j in range(rb):
            rows_j = pl.ds(j * GRID_W, GRID_W)
            q = q_ref[rows_j, :]
            k, p, start, t = _na_probs(q, k_ref, b_ref, i * rb + j, rows)
            keys = pl.ds(start, nkeys)
            dob = do_ref[rows_j, :].astype(BF16)
            dp = lax.dot_general(dob, v_ref[keys, :], (((1,), (1,)), ((), ())), preferred_element_type=F32)
            ds = p * (dp - jnp.sum(dp * p, axis=-1, keepdims=True))
            dsb = (ds * (HEAD_DIM ** -0.5)).astype(BF16)
            dq_ref[rows_j, :] = jnp.dot(dsb, k, preferred_element_type=F32).astype(dq_ref.dtype)
            dk_acc[keys, :] += lax.dot_general(dsb, q, tn_dims, preferred_element_type=F32)
            dv_acc[keys, :] += lax.dot_general(p.astype(BF16), dob, tn_dims, preferred_element_type=F32)
            db_ref[t] += ds

        @pl.when(i == rows // rb - 1)
        def _():
            dk_ref[...] = dk_acc[...].astype(dk_ref.dtype)
            dv_ref[...] = dv_acc[...].astype(dv_ref.dtype)

    W = H * HEAD_DIM
    qspec = pl.BlockSpec((rb * GRID_W, HEAD_DIM), lambda h, i: (i, h))
    bspec = pl.BlockSpec((None, NA_WIN_ROWS, GRID_W, nkeys), lambda h, i: (h, 0, 0, 0))
    return pl.pallas_call(
        body, name="na_bwd", grid=(H, rows // rb),
        in_specs=[qspec, pl.BlockSpec((S, HEAD_DIM), lambda h, i: (0, H + h)),
                  pl.BlockSpec((S, HEAD_DIM), lambda h, i: (0, 2 * H + h)), bspec, qspec],
        out_specs=[qspec, pl.BlockSpec((S, HEAD_DIM), lambda h, i: (0, h)),
                   pl.BlockSpec((S, HEAD_DIM), lambda h, i: (0, h)), bspec],
        out_shape=[jax.ShapeDtypeStruct((S, W), BF16)] * 3 + [jax.ShapeDtypeStruct((H, NA_WIN_ROWS, GRID_W, nkeys), F32)],
        scratch_shapes=[pltpu.VMEM((S, HEAD_DIM), F32)] * 2,
        compiler_params=_params(("arbitrary", "arbitrary")),
    )(z, z, z, bias, do)


def _mla_keys(kn_ref, kr_ref, kcat):
    @pl.when(pl.program_id(1) == 0)
    def _():
        kcat[:, :HEAD_DIM] = kn_ref[...]
        kcat[:, HEAD_DIM:] = kr_ref[...]


def _mla_scores(qn_ref, qr_ref, kcat):
    qcat = jnp.concatenate([qn_ref[...], qr_ref[...]], axis=1)
    s = lax.dot_general(qcat, kcat[...], (((1,), (1,)), ((), ())), preferred_element_type=F32)
    return qcat, s * (MLA_QK ** -0.5)


def mla_fwd(qn, qr, kn, v, kr, H, S):
    tq = _pick(S, 256, 16)

    def body(qn_ref, qr_ref, kn_ref, v_ref, kr_ref, o_ref, lse_ref, kcat):
        _mla_keys(kn_ref, kr_ref, kcat)
        _, s = _mla_scores(qn_ref, qr_ref, kcat)
        m = jnp.max(s, axis=-1, keepdims=True)
        e = jnp.exp(s - m)
        l = jnp.sum(e, axis=-1, keepdims=True)
        o_ref[...] = jnp.dot((e / l).astype(BF16), v_ref[...], preferred_element_type=F32).astype(o_ref.dtype)
        lse_ref[...] = jnp.broadcast_to(m + jnp.log(l), lse_ref.shape)

    qspec = pl.BlockSpec((tq, HEAD_DIM), lambda h, i: (i, h))
    kspec = pl.BlockSpec((S, HEAD_DIM), lambda h, i: (0, h))
    return pl.pallas_call(
        body, name="mla_fwd", grid=(H, S // tq),
        in_specs=[qspec, qspec, kspec, kspec, pl.BlockSpec((S, LANES), lambda h, i: (0, 0))],
        out_specs=[qspec, qspec],
        out_shape=[jax.ShapeDtypeStruct((S, H * HEAD_DIM), BF16), jax.ShapeDtypeStruct((S, H * LANES), F32)],
        scratch_shapes=[pltpu.VMEM((S, 2 * HEAD_DIM), BF16)],
        compiler_params=_params(("parallel", "arbitrary")),
    )(qn, qr, kn, v, kr)


def mla_bwd(qn, qr, kn, v, kr, lse, do, H, S):
    tq = _pick(S, 256, 16)
    nt = (((1,), (1,)), ((), ()))
    tn_dims = (((0,), (0,)), ((), ()))

    def body(qn_ref, qr_ref, kn_ref, v_ref, kr_ref, lse_ref, do_ref, dqn_ref, dqr_ref, dkn_ref, dv_ref, dkr_ref, kcat):
        h, i = pl.program_id(0), pl.program_id(1)
        _mla_keys(kn_ref, kr_ref, kcat)
        qcat, s = _mla_scores(qn_ref, qr_ref, kcat)
        p = jnp.exp(s - lse_ref[:, 0:1])
        dob = do_ref[...].astype(BF16)
        dp = lax.dot_general(dob, v_ref[...], nt, preferred_element_type=F32)
        ds = p * (dp - jnp.sum(dp * p, axis=-1, keepdims=True))
        dsb = (ds * (MLA_QK ** -0.5)).astype(BF16)
        dq = jnp.dot(dsb, kcat[...], preferred_element_type=F32)
        dqn_ref[...] = dq[:, :HEAD_DIM].astype(dqn_ref.dtype)
        dqr_ref[...] = dq[:, HEAD_DIM:].astype(dqr_ref.dtype)

        @pl.when(i == 0)
        def _():
            dkn_ref[...] = jnp.zeros_like(dkn_ref)
            dv_ref[...] = jnp.zeros_like(dv_ref)

        @pl.when(jnp.logical_and(i == 0, h == 0))
        def _():
            dkr_ref[...] = jnp.zeros_like(dkr_ref)

        dk = lax.dot_general(dsb, qcat, tn_dims, preferred_element_type=F32)
        dkn_ref[...] += dk[:, :HEAD_DIM]
        dkr_ref[...] += dk[:, HEAD_DIM:]
        dv_ref[...] += lax.dot_general(p.astype(BF16), dob, tn_dims, preferred_element_type=F32)

    qspec = pl.BlockSpec((tq, HEAD_DIM), lambda h, i: (i, h))
    kspec = pl.BlockSpec((S, HEAD_DIM), lambda h, i: (0, h))
    rspec = pl.BlockSpec((S, LANES), lambda h, i: (0, 0))
    W = H * HEAD_DIM
    return pl.pallas_call(
        body, name="mla_bwd", grid=(H, S // tq),
        in_specs=[qspec, qspec, kspec, kspec, rspec, qspec, qspec],
        out_specs=[qspec, qspec, kspec, kspec, rspec],
        out_shape=[jax.ShapeDtypeStruct((S, W), BF16), jax.ShapeDtypeStruct((S, W), F32),
                   jax.ShapeDtypeStruct((S, W), F32), jax.ShapeDtypeStruct((S, W), F32),
                   jax.ShapeDtypeStruct((S, LANES), F32)],
        scratch_shapes=[pltpu.VMEM((S, 2 * HEAD_DIM), BF16)],
        compiler_params=_params(("arbitrary", "arbitrary")),
    )(qn, qr, kn, v, kr, lse, do)


def _place():
    return lax.axis_index("x"), lax.axis_index("y"), lax.axis_index("c")


def _other_chips(x, y):
    return [(1 - x, y), (x, 1 - y), (1 - x, 1 - y)]


def _remote(src, dst, send_sem, recv_sem, to):
    return pltpu.make_async_remote_copy(src_ref=src, dst_ref=dst, send_sem=send_sem, recv_sem=recv_sem,
                                        device_id=to, device_id_type=MESH)


HBM = pl.BlockSpec(memory_space=pltpu.HBM)
SEM = pl.BlockSpec(memory_space=pltpu.SEMAPHORE)
EFFECT = pltpu.SideEffectType.DATAFLOW_SIDE_EFFECTING


def _in_hbm(a):
    return pltpu.with_memory_space_constraint(a, pltpu.HBM)


TOKEN = jax.ShapeDtypeStruct((8, LANES), F32)
IN_VMEM = pl.BlockSpec(memory_space=pltpu.VMEM)


def gather_start(shards, landings, after, tag):
    n = len(shards)

    def body(*refs):
        ins, lands = refs[:n], refs[n:2 * n]
        send, recv = refs[2 * n + len(after)], refs[2 * n + len(after) + 1]
        token = refs[-1]
        x, y, c = _place()
        me = 2 * x + y
        for w in range(n):
            for k, (px, py) in enumerate(_other_chips(x, y)):
                _remote(ins[w].at[c], lands[w].at[me, c], send.at[3 * w + k], recv.at[3 * w + k], (px, py, c)).start()
        token[...] = jnp.zeros_like(token)

    bufs = list(shards) + list(landings)
    outs = pl.pallas_call(
        body, name="gather_start_" + tag,
        out_shape=(pltpu.SemaphoreType.DMA((3 * n,)),) * 2 + tuple(pltpu.HBM(b.shape, b.dtype) for b in bufs) + (TOKEN,),
        in_specs=[HBM] * (2 * n) + [ANY] * len(after), out_specs=tuple([SEM, SEM] + [HBM] * (2 * n) + [IN_VMEM]),
        input_output_aliases={i: 2 + i for i in range(2 * n)},
        compiler_params=pltpu.CompilerParams(has_side_effects=EFFECT),
    )(*[_in_hbm(b) for b in bufs], *after)
    return (outs[0], outs[1]), outs[2:2 + n], outs[2 + n:2 + 2 * n], outs[-1]


def gather_wait(sems, shards, landings, after, tag):
    n = len(shards)
    send, recv = sems

    def body(*refs):
        ins, lands = refs[:n], refs[n:2 * n]
        send_sem, recv_sem = refs[2 * n], refs[2 * n + 1]
        x, y, c = _place()
        me = 2 * x + y
        for w in range(n):
            for k, (px, py) in enumerate(_other_chips(x, y)):
                cp = _remote(ins[w].at[c], lands[w].at[2 * px + py, c], send_sem.at[3 * w + k], recv_sem.at[3 * w + k],
                             (px, py, c))
                cp.wait_send()
                cp.wait_recv()

    bufs = list(shards) + list(landings)
    outs = pl.pallas_call(
        body, name="gather_wait_" + tag, out_shape=tuple(pltpu.HBM(b.shape, b.dtype) for b in bufs),
        in_specs=[HBM] * (2 * n) + [SEM, SEM] + [ANY] * len(after), out_specs=tuple([HBM] * (2 * n)),
        input_output_aliases={i: i for i in range(2 * n)},
        compiler_params=pltpu.CompilerParams(has_side_effects=EFFECT),
    )(*bufs, send, recv, *after)
    return outs[:n], outs[n:]


def gather_forward(landings, tag):
    n = len(landings)

    def body(*refs):
        ins, outs = refs[:n], refs[n:2 * n]
        send, recv = refs[2 * n:]
        x, y, c = _place()
        sibling = (x, y, 1 - c)
        cps = []
        for w in range(n):
            for k, (px, py) in enumerate(_other_chips(x, y)):
                j = 2 * px + py
                cp = _remote(ins[w].at[j, c], outs[w].at[j, c], send.at[3 * w + k], recv.at[3 * w + k], sibling)
                cp.start()
                cps.append(cp)
        for w in range(n):
            for k, (px, py) in enumerate(_other_chips(x, y)):
                blk = outs[w].at[2 * px + py, 1 - c]
                _remote(blk, blk, send.at[3 * w + k], recv.at[3 * w + k], sibling).wait_recv()
        for cp in cps:
            cp.wait_send()

    return pl.pallas_call(
        body, name="gather_forward_" + tag, in_specs=[ANY] * n, out_specs=[ANY] * n,
        out_shape=[jax.ShapeDtypeStruct(a.shape, a.dtype) for a in landings],
        input_output_aliases={i: i for i in range(n)},
        scratch_shapes=[pltpu.SemaphoreType.DMA((3 * n,)), pltpu.SemaphoreType.DMA((3 * n,))],
    )(*landings)


def pair_exchange(grads, tag):
    n = len(grads)

    def body(*refs):
        ins, outs = refs[:n], refs[n:2 * n]
        send, recv = refs[2 * n:]
        x, y, c = _place()
        cps = []
        for w in range(n):
            cp = _remote(ins[w].at[:, 1 - c], outs[w], send.at[w], recv.at[w], (x, y, 1 - c))
            cp.start()
            cps.append(cp)
        for cp in cps:
            cp.wait()

    return pl.pallas_call(
        body, name="grad_pair_exchange_" + tag, in_specs=[ANY] * n, out_specs=[ANY] * n,
        out_shape=[jax.ShapeDtypeStruct((g.shape[0],) + g.shape[2:], g.dtype) for g in grads],
        scratch_shapes=[pltpu.SemaphoreType.DMA((n,)), pltpu.SemaphoreType.DMA((n,))],
    )(*grads)


def scatter_start(sums, landings, tag):
    n = len(sums)

    def body(*refs):
        ins, lands = refs[:n], refs[n:2 * n]
        send, recv = refs[2 * n], refs[2 * n + 1]
        token = refs[-1]
        x, y, c = _place()
        for w in range(n):
            for k, (px, py) in enumerate(_other_chips(x, y)):
                _remote(ins[w].at[2 * px + py], lands[w].at[k], send.at[3 * w + k], recv.at[3 * w + k], (px, py, c)).start()
        token[...] = jnp.zeros_like(token)

    bufs = list(sums) + list(landings)
    outs = pl.pallas_call(
        body, name="scatter_start_" + tag,
        out_shape=(pltpu.SemaphoreType.DMA((3 * n,)),) * 2 + tuple(pltpu.HBM(b.shape, b.dtype) for b in bufs) + (TOKEN,),
        in_specs=[HBM] * (2 * n), out_specs=tuple([SEM, SEM] + [HBM] * (2 * n) + [IN_VMEM]),
        input_output_aliases={i: 2 + i for i in range(2 * n)},
        compiler_params=pltpu.CompilerParams(has_side_effects=EFFECT),
    )(*[_in_hbm(b) for b in bufs])
    return (outs[0], outs[1]), outs[2:2 + n], outs[2 + n:2 + 2 * n], outs[-1]


def scatter_wait(sems, sums, landings, after, tag):
    n = len(sums)

    def body(*refs):
        ins, lands = refs[:n], refs[n:2 * n]
        send, recv = refs[2 * n], refs[2 * n + 1]
        x, y, c = _place()
        for w in range(n):
            for k, (px, py) in enumerate(_other_chips(x, y)):
                cp = _remote(ins[w].at[2 * px + py], lands[w].at[k], send.at[3 * w + k], recv.at[3 * w + k], (px, py, c))
                cp.wait_send()
                cp.wait_recv()

    bufs = list(sums) + list(landings)
    outs = pl.pallas_call(
        body, name="scatter_wait_" + tag, out_shape=tuple(pltpu.HBM(b.shape, b.dtype) for b in bufs),
        in_specs=[HBM] * (2 * n) + [SEM, SEM] + [ANY] * len(after), out_specs=tuple([HBM] * (2 * n)),
        input_output_aliases={i: i for i in range(2 * n)},
        compiler_params=pltpu.CompilerParams(has_side_effects=EFFECT),
    )(*bufs, sems[0], sems[1], *after)
    return outs[:n], outs[n:]


def half_exchange(halves, tag):
    n = len(halves)

    def body(*refs):
        ins, outs = refs[:n], refs[n:2 * n]
        send, recv = refs[2 * n:]
        x, y, c = _place()
        cps = []
        for w in range(n):
            cp = _remote(ins[w], outs[w], send.at[w], recv.at[w], (x, y, 1 - c))
            cp.start()
            cps.append(cp)
        for cp in cps:
            cp.wait()

    return pl.pallas_call(
        body, name="grad_half_exchange_" + tag, in_specs=[ANY] * n, out_specs=[ANY] * n,
        out_shape=[jax.ShapeDtypeStruct(h.shape, h.dtype) for h in halves],
        scratch_shapes=[pltpu.SemaphoreType.DMA((n,)), pltpu.SemaphoreType.DMA((n,))],
    )(*halves)


def gather_small(v):
    def body(v_ref, o_ref, send, recv, local):
        x, y, c = _place()
        me = 4 * x + 2 * y + c
        own = pltpu.make_async_copy(v_ref, o_ref.at[me], local)
        own.start()
        cps = []
        for k in range(1, 8):
            fx, fy, fc = (k >> 2) & 1, (k >> 1) & 1, k & 1
            to = (x ^ fx if fx else x, y ^ fy if fy else y, c ^ fc if fc else c)
            cp = _remote(v_ref, o_ref.at[me], send.at[k - 1], recv.at[k - 1], to)
            cp.start()
            cps.append(cp)
        for k in range(1, 8):
            fx, fy, fc = (k >> 2) & 1, (k >> 1) & 1, k & 1
            px, py, pc = (x ^ fx if fx else x, y ^ fy if fy else y, c ^ fc if fc else c)
            cps[k - 1].wait_send()
            _remote(v_ref, o_ref.at[4 * px + 2 * py + pc], send.at[k - 1], recv.at[k - 1], (px, py, pc)).wait_recv()
        own.wait()

    return pl.pallas_call(
        body, name="gather_small_grads", in_specs=[ANY], out_specs=ANY,
        out_shape=jax.ShapeDtypeStruct((8,) + v.shape, v.dtype),
        scratch_shapes=[pltpu.SemaphoreType.DMA((7,)), pltpu.SemaphoreType.DMA((7,)), pltpu.SemaphoreType.DMA],
    )(v)


def _row_tile(rows, cols, nbuf_bytes):
    tm = _pick(rows, 512, 16)
    while tm * cols * nbuf_bytes * 2 > VMEM_BUDGET_V7X and tm % 32 == 0:
        tm //= 2
    return tm


def pair_sum(g, r, c_idx, tag):
    _, _, rows, cols = g.shape
    tm = _row_tile(rows, cols, 2 + 2 + 2)
    nb = rows // tm

    def body(c_ref, g_ref, r_ref, o_ref):
        o_ref[...] = (g_ref[...].astype(F32) + r_ref[...].astype(F32)).astype(o_ref.dtype)

    gs = pltpu.PrefetchScalarGridSpec(
        num_scalar_prefetch=1, grid=(N_CHIPS, nb),
        in_specs=[pl.BlockSpec((None, None, tm, cols), lambda j, i, c_ref: (j, c_ref[0], i, 0)),
                  pl.BlockSpec((None, tm, cols), lambda j, i, c_ref: (j, i, 0))],
        out_specs=pl.BlockSpec((None, tm, cols), lambda j, i, c_ref: (j, i, 0)))
    return pl.pallas_call(body, name="grad_pair_sum_" + tag, grid_spec=gs,
                          out_shape=jax.ShapeDtypeStruct(r.shape, BF16),
                          compiler_params=_params(("arbitrary", "arbitrary")))(c_idx, g, r)


def chip_sum(s, r, j_idx, tag):
    _, rows, cols = s.shape
    tm = _row_tile(rows, cols, 2 + 3 * 2 + 4)
    nb = rows // tm

    def body(j_ref, s_ref, r_ref, o_ref):
        t = s_ref[...].astype(F32)
        for k in range(3):
            t = t + r_ref[k].astype(F32)
        o_ref[...] = t

    gs = pltpu.PrefetchScalarGridSpec(
        num_scalar_prefetch=1, grid=(nb,),
        in_specs=[pl.BlockSpec((None, tm, cols), lambda i, j_ref: (j_ref[0], i, 0)),
                  pl.BlockSpec((3, tm, cols), lambda i, j_ref: (0, i, 0))],
        out_specs=pl.BlockSpec((tm, cols), lambda i, j_ref: (i, 0)))
    return pl.pallas_call(body, name="grad_chip_sum_" + tag, grid_spec=gs,
                          out_shape=jax.ShapeDtypeStruct((rows, cols), F32),
                          compiler_params=_params(("arbitrary",)))(j_idx, s, r)


def adamw(w, g, m, v, *, name):
    rows, cols = w.shape
    tm = _row_tile(rows, cols, 7 * 4)

    return rowwise(_adamw_math, [w, g, m, v], [], [(cols, F32)] * 3, tm=tm, name=name)


def _adamw_math(wb, gb, mb, vb):
    m2 = ADAM_B1 * mb + (1.0 - ADAM_B1) * gb
    v2 = ADAM_B2 * vb + (1.0 - ADAM_B2) * (gb * gb)
    m_hat = m2 / (1.0 - ADAM_B1 ** ADAM_STEP)
    v_hat = v2 / (1.0 - ADAM_B2 ** ADAM_STEP)
    delta = -ADAM_LR * (m_hat / (jnp.sqrt(v_hat) + ADAM_EPS) + ADAM_WD * wb)
    return delta, m2, v2


def adamw_shard(w, g_own, g_sib, m, v, c_idx, *, name):
    rows, cols = g_own.shape
    tm = _row_tile(rows, cols, 9 * 4)
    nb = rows // tm

    def body(c_ref, w_ref, go_ref, gs_ref, m_ref, v_ref, g_out, d_out, m_out, v_out):
        gb = jnp.where(pl.program_id(0) == c_ref[0], go_ref[...], gs_ref[...])
        delta, m2, v2 = _adamw_math(w_ref[...], gb, m_ref[...], v_ref[...])
        g_out[...] = gb
        d_out[...] = delta
        m_out[...] = m2
        v_out[...] = v2

    full = pl.BlockSpec((tm, cols), lambda h, i, c_ref: (h * nb + i, 0))
    half = pl.BlockSpec((tm, cols), lambda h, i, c_ref: (i, 0))
    gs = pltpu.PrefetchScalarGridSpec(num_scalar_prefetch=1, grid=(2, nb), in_specs=[full, half, half, full, full],
                                      out_specs=[full] * 4)
    return pl.pallas_call(body, name=name, grid_spec=gs, out_shape=[jax.ShapeDtypeStruct(w.shape, F32)] * 4,
                          compiler_params=_params(("arbitrary", "arbitrary")))(c_idx, w, g_own, g_sib, m, v)


def sum_devices(a):
    def body(a_ref, o_ref):
        t = a_ref[0]
        for k in range(1, 8):
            t = t + a_ref[k]
        o_ref[...] = t

    return pl.pallas_call(body, name="sum_small_grads", out_shape=jax.ShapeDtypeStruct(a.shape[1:], a.dtype))(a)


def _halves(w2d):
    r, c = w2d.shape
    return w2d.reshape(2, r // 2, c)


def kernel(x, p, ffn1_norm, ffn1_w_gate, ffn1_w_up, ffn1_w_down, mix_norm, w_in, q_a_norm, w_uq, kv_a_norm, w_ukv, na_rpb, w_branch_a, w_branch_b, w_out, ffn2_norm, ffn2_w_gate, ffn2_w_up, ffn2_w_down, pl_norm, w_pl, w_pl_gate, final_norm, loss_target, m_ffn1_norm, m_ffn1_w_gate, m_ffn1_w_up, m_ffn1_w_down, m_mix_norm, m_w_in, m_q_a_norm, m_w_uq, m_kv_a_norm, m_w_ukv, m_na_rpb, m_w_branch_a, m_w_branch_b, m_w_out, m_ffn2_norm, m_ffn2_w_gate, m_ffn2_w_up, m_ffn2_w_down, m_pl_norm, m_w_pl, m_w_pl_gate, m_final_norm, v_ffn1_norm, v_ffn1_w_gate, v_ffn1_w_up, v_ffn1_w_down, v_mix_norm, v_w_in, v_q_a_norm, v_w_uq, v_kv_a_norm, v_w_ukv, v_na_rpb, v_w_branch_a, v_w_branch_b, v_w_out, v_ffn2_norm, v_ffn2_w_gate, v_ffn2_w_up, v_ffn2_w_down, v_pl_norm, v_w_pl, v_w_pl_gate, v_final_norm):
    big = ["ffn1_w_gate", "ffn1_w_up", "ffn1_w_down", "w_in", "w_uq", "w_ukv", "w_branch_a", "w_branch_b", "w_out",
           "ffn2_w_gate", "ffn2_w_up", "ffn2_w_down", "w_pl", "w_pl_gate"]
    col_sharded = {"ffn1_w_gate", "ffn1_w_up", "w_in", "w_uq", "w_ukv", "w_branch_a", "w_branch_b", "ffn2_w_gate",
                   "ffn2_w_up", "w_pl"}
    small = ["ffn1_norm", "mix_norm", "q_a_norm", "kv_a_norm", "na_rpb", "ffn2_norm", "pl_norm", "final_norm"]
    order = ["ffn1_norm", "ffn1_w_gate", "ffn1_w_up", "ffn1_w_down", "mix_norm", "w_in", "q_a_norm", "w_uq",
             "kv_a_norm", "w_ukv", "na_rpb", "w_branch_a", "w_branch_b", "w_out", "ffn2_norm", "ffn2_w_gate",
             "ffn2_w_up", "ffn2_w_down", "pl_norm", "w_pl", "w_pl_gate", "final_norm"]
    env = dict(locals())
    W = {n: env[n] for n in order}
    Mo = {n: env["m_" + n] for n in order}
    Vo = {n: env["v_" + n] for n in order}

    xs = x[0]
    S, D = xs.shape
    tgt = loss_target[0]
    ps = p[0, 0]
    NAW = w_branch_a.shape[1]
    MLAW = w_branch_b.shape[1]
    NH, MH = NAW // HEAD_DIM, MLAW // HEAD_DIM
    QR, KVR = w_uq.shape[1], w_ukv.shape[1]
    F = ffn1_w_down.shape[1] * N_CHIPS
    cx, cy, cc = _place()
    c_idx = jnp.reshape(cc, (1,)).astype(jnp.int32)
    j_idx = jnp.reshape(2 * cx + cy, (1,)).astype(jnp.int32)

    me_chip = 2 * cx + cy
    groups = [["ffn1_w_gate"], ["ffn1_w_up"], ["ffn1_w_down"], ["w_in"],
              ["w_uq", "w_ukv", "w_branch_a", "w_branch_b", "w_out"],
              ["ffn2_w_gate", "ffn2_w_up", "ffn2_w_down"], ["w_pl", "w_pl_gate"]]
    started, tokens = [], []
    for g, members in enumerate(groups):
        shards = [_halves(W[n][0].astype(BF16)) for n in members]
        landings = [lax.empty((N_CHIPS,) + s.shape, BF16) for s in shards]
        sems, shards_thru, landings_thru, token = gather_start(shards, landings, tokens[-1:], str(g))
        started.append((sems, shards_thru, landings_thru))
        tokens.append(token)
    gathered = {}

    def arrive(n, after):
        g = [n in members for members in groups].index(True)
        sems, shards_thru, landings_thru = started[g]
        after = list(after) if isinstance(after, (list, tuple)) else [after]
        shards_out, landed = gather_wait(sems, shards_thru, landings_thru, after, str(g))
        for name, full, own in zip(groups[g], gather_forward(landed, str(g)), shards_out):
            gathered[name] = lax.dynamic_update_slice(full, own[None], (me_chip, 0, 0, 0))

    def stacked(n, after=None):
        if n not in gathered:
            arrive(n, after)
        g = gathered[n]
        return g.reshape(N_CHIPS, 2 * g.shape[2], g.shape[3])

    def plain(n, after=None):
        if n in col_sharded:
            st = stacked(n, after)
            return st.transpose(1, 0, 2).reshape(st.shape[1], N_CHIPS * st.shape[2])
        if n not in gathered:
            arrive(n, after)
        g = gathered[n]
        return g.reshape(N_CHIPS * 2 * g.shape[2], g.shape[3])

    n_na = 3 * NAW
    n_front = n_na + QR + KVR
    n_in = n_front + MLA_ROPE + 2 * D
    off_ql, off_kvl, off_kr = 2 * D, 2 * D + QR, 2 * D + QR + KVR
    kr_w = 2 * LANES
    rest_w = off_kr + kr_w
    rest_ranges = [(n_front + MLA_ROPE, n_in), (n_na, n_front), (n_front, n_front + MLA_ROPE)]

    def shard_cols(st, lo, hi):
        nb, parts = st.shape[2], []
        while lo < hi:
            j = lo // nb
            end = min(hi, (j + 1) * nb)
            parts.append(st[j][:, lo - j * nb:end - j * nb])
            lo = end
        return parts

    def w_in_shards(g_na, g_rest):
        pieces = [(0, n_na, g_na, 0)]
        o = 0
        for lo, hi in rest_ranges:
            pieces.append((lo, hi, g_rest, o))
            o += hi - lo
        nb, shards = n_in // N_CHIPS, []
        for j in range(N_CHIPS):
            parts = []
            for lo, hi, src, o in sorted(pieces):
                a, b = max(lo, j * nb), min(hi, (j + 1) * nb)
                if a < b:
                    parts.append(src[:, o + a - lo:o + b - lo])
            shards.append(jnp.concatenate(parts, axis=1))
        return jnp.stack(shards)

    pos = jnp.arange(S, dtype=F32)
    inv_freq = 1.0 / (ROPE_THETA ** (jnp.arange(0, MLA_ROPE, 2, dtype=F32) / MLA_ROPE))
    ang = pos[:, None] * inv_freq[None, :]
    zpad = jnp.zeros((S, LANES - MLA_ROPE), F32)
    cos_t = jnp.concatenate([jnp.cos(ang), jnp.cos(ang), zpad], axis=1)
    sin_t = jnp.concatenate([-jnp.sin(ang), jnp.sin(ang), zpad], axis=1)

    def ffn_fwd(h, norm_g, tag, pre, after=()):
        n = norm_fwd(h, norm_g, name=f"{tag}_norm")
        g = mm(n, stacked(pre + "_w_gate", [n, *after]), name=f"{tag}_gate", b_stack=True)
        u = mm(n, stacked(pre + "_w_up", g), name=f"{tag}_up", b_stack=True)
        a = swiglu_fwd(g, u, name=f"{tag}_act")
        h_out = mm(a, plain(pre + "_w_down", a), name=f"{tag}_down", res=h, alpha=0.5)
        return h_out, (n, g, u, a)

    def ffn_bwd(h, norm_g, saved, dh, dh_half, tag, pre, last, after=()):
        n, g, u, a = saved
        G[pre + "_w_down"] = mm(a, dh_half, name=f"{tag}_dw_down", ta=True, out_dtype=BF16)
        da = mm(dh_half, plain(pre + "_w_down"), name=f"{tag}_da", tb=True, after=after)
        dg, du = swiglu_bwd(g, u, da, name=f"{tag}_dact")
        G[pre + "_w_gate"] = mm(n, dg, name=f"{tag}_dw_gate", ta=True, out_dtype=BF16, out_stack=True)
        G[pre + "_w_up"] = mm(n, du, name=f"{tag}_dw_up", ta=True, out_dtype=BF16, out_stack=True)
        token = reduce_start([pre + "_w_down", pre + "_w_gate", pre + "_w_up"], tag)
        dn = mm(dg, stacked(pre + "_w_gate"), name=f"{tag}_dn_gate", tb=True, b_stack=True, after=[token])
        dn = mm(du, stacked(pre + "_w_up"), name=f"{tag}_dn_up", tb=True, b_stack=True, res=dn)
        return norm_bwd(h, norm_g, dn, name=f"{tag}_dnorm", res=dh, bf16_alpha=None if last else 1.0)

    bias = na_bias(na_rpb[0], after=tokens[-1:])
    h1, ffn1_saved = ffn_fwd(xs, ffn1_norm, "ffn1", "ffn1", after=[bias, tokens[-1]])
    u_mix = norm_fwd(h1, mix_norm, name="mix_norm")
    win_st = stacked("w_in", u_mix)
    w_na = jnp.concatenate(shard_cols(win_st, 0, n_na), axis=1)
    w_rest = jnp.concatenate([p_ for lo, hi in rest_ranges for p_ in shard_cols(win_st, lo, hi)]
                             + [jnp.zeros((D, kr_w - MLA_ROPE), BF16)], axis=1)
    z_na = mm(u_mix, w_na, name="mix_in_na", out_dtype=BF16)
    z = mm(u_mix, w_rest, name="mix_in_rest")
    o_a = na_fwd(z_na, bias, NH, S)
    c_q = norm_fwd((z, QR, off_ql // QR), q_a_norm, name="q_a_norm")
    c_kv = norm_fwd((z, KVR, off_kvl // KVR), kv_a_norm, name="kv_a_norm")
    wuq = plain("w_uq", c_kv).reshape(QR, MH, MLA_QK)
    wuq_n = wuq[:, :, :MLA_NOPE].reshape(QR, MH * MLA_NOPE)
    wuq_r = jnp.pad(wuq[:, :, MLA_NOPE:], ((0, 0), (0, 0), (0, LANES - MLA_ROPE))).reshape(QR, MH * LANES)
    wukv = plain("w_ukv").reshape(KVR, MH, 2, HEAD_DIM)
    wuk = wukv[:, :, 0].reshape(KVR, MH * HEAD_DIM)
    wuv = wukv[:, :, 1].reshape(KVR, MH * HEAD_DIM)
    q_n = mm(c_q, wuq_n, name="mla_q_nope", out_dtype=BF16)
    q_r = rope(mm(c_q, wuq_r, name="mla_q_rope"), cos_t, sin_t, name="rope_q", out_dtype=BF16)
    k_n = mm(c_kv, wuk, name="mla_k_nope", out_dtype=BF16)
    v_m = mm(c_kv, wuv, name="mla_v", out_dtype=BF16)
    k_r = rope((z, LANES, off_kr // LANES), cos_t, sin_t, name="rope_k", out_dtype=BF16)
    o_b, lse = mla_fwd(q_n, q_r, k_n, v_m, k_r, MH, S)
    y_a = mm(o_a, stacked("w_branch_a"), name="branch_a", b_stack=True)
    y_b = mm(o_b, stacked("w_branch_b"), name="branch_b", b_stack=True)
    z_ga, z_gb = (z, D, 0), (z, D, 1)
    merged = rowwise(lambda ga, gb, ya, yb: _sig(ga) * ya + _sig(gb) * yb, [z_ga, z_gb, y_a, y_b], [], [(D, BF16)],
                     tm=256, name="merge")[0]
    h2 = mm(merged, plain("w_out"), name="mix_out", res=h1)
    h3, ffn2_saved = ffn_fwd(h2, ffn2_norm, "ffn2", "ffn2")
    n4 = norm_fwd(h3, pl_norm, name="pl_norm")
    pg_pre = mm(n4, plain("w_pl_gate", n4), name="pl_gate")
    pe = mm(ps, stacked("w_pl"), name="pl_embed", b_stack=True)

    def tail(h3b, pgb, peb, tb_, fg):
        pg = _sig(pgb)
        h4 = h3b + pg * peb
        r = _rstd(h4)
        xh = h4 * r
        err = xh * fg - tb_
        loss_rows = jnp.mean(err * err, axis=-1, keepdims=True)
        dy = err * (1.0 / D)
        dxh = dy * fg
        dh4 = r * (dxh - xh * jnp.mean(dxh * xh, axis=-1, keepdims=True))
        loss_part = jnp.broadcast_to(0.5 * jnp.sum(loss_rows, axis=0, keepdims=True), (1, LANES))
        return (dh4, dh4 * peb * pg * (1.0 - pg), dh4 * pg, loss_part, jnp.sum(dy * xh, axis=0, keepdims=True))

    dh4, dpg_pre, dpe, loss_part, g_final = rowwise(
        tail, [h3, pg_pre, pe, tgt], [final_norm.reshape(1, D)], [(D, F32), (D, BF16), (D, BF16)],
        accs=[(1, LANES), (1, D)], tm=128, name="loss_tail")
    loss = lax.psum(loss_part[0, 0], ("x", "y", "c"))

    G = {}
    pending = []

    def four(g):
        if g.ndim == 2:
            return g.reshape(N_CHIPS, 2, g.shape[0] // (2 * N_CHIPS), g.shape[1])
        return g.reshape(N_CHIPS, 2, g.shape[1] // 2, g.shape[2])

    def reduce_start(names, tag):
        g4 = [four(G[n]) for n in names]
        sums = [pair_sum(a, b, c_idx, n) for n, a, b in zip(names, g4, pair_exchange(g4, tag))]
        lands = [lax.empty((N_CHIPS - 1,) + s_.shape[1:], BF16) for s_ in sums]
        sems, sums, lands, token = scatter_start(sums, lands, tag)
        pending.append((names, tag, sems, sums, lands))
        return token

    def reduce_finish(entry, after):
        names, tag, sems, sums, lands = entry
        sums, got = scatter_wait(sems, sums, lands, after, tag)
        halves = [chip_sum(a, b, j_idx, n) for n, a, b in zip(names, sums, got)]
        done = []
        for n, own, sib in zip(names, halves, half_exchange(halves, tag)):
            shp = W[n].shape
            two_d = lambda a_: a_.reshape(shp[1], shp[2])
            out = adamw_shard(two_d(W[n]), own, sib, two_d(Mo[n]), two_d(Vo[n]), c_idx, name="adamw_" + n)
            grads[n], delta[n], new_m[n], new_v[n] = [o.reshape(shp) for o in out]
            done.append(out[0])
        return done

    G["w_pl"] = mm(ps, dpe, name="pl_dw_embed", ta=True, out_dtype=BF16, out_stack=True)
    G["w_pl_gate"] = mm(n4, dpg_pre, name="pl_dw_gate", ta=True, out_dtype=BF16)
    dn4 = mm(dpg_pre, plain("w_pl_gate"), name="pl_dn", tb=True)
    dh3, dh3_half, g_pl = norm_bwd(h3, pl_norm, dn4, name="pl_dnorm", res=dh4, bf16_alpha=0.5)
    token = reduce_start(["w_pl", "w_pl_gate"], "pl")
    dh2, dh2_b, g_ffn2 = ffn_bwd(h2, ffn2_norm, ffn2_saved, dh3, dh3_half, "ffn2", "ffn2", last=False, after=[token])

    G["w_out"] = mm(merged, dh2_b, name="mix_dw_out", ta=True, out_dtype=BF16)
    dmerged = mm(dh2_b, plain("w_out"), name="mix_dmerged", tb=True)

    def merge_bwd(ga, gb, ya, yb, dm):
        sa, sb = _sig(ga), _sig(gb)
        dgates = jnp.concatenate([dm * ya * sa * (1.0 - sa), dm * yb * sb * (1.0 - sb)], axis=1)
        return dm * sa, dm * sb, dgates

    dy_a, dy_b, dz_rest = rowwise(merge_bwd, [z_ga, z_gb, y_a, y_b, dmerged], [],
                                  [(D, BF16), (D, BF16), (2 * D, BF16, 0)], tm=256, name="merge_bwd",
                                  into=(None, rest_w))
    G["w_branch_a"] = mm(o_a, dy_a, name="branch_a_dw", ta=True, out_dtype=BF16, out_stack=True)
    G["w_branch_b"] = mm(o_b, dy_b, name="branch_b_dw", ta=True, out_dtype=BF16, out_stack=True)
    do_a = mm(dy_a, stacked("w_branch_a"), name="branch_a_dx", tb=True, b_stack=True)
    do_b = mm(dy_b, stacked("w_branch_b"), name="branch_b_dx", tb=True, b_stack=True)
    dq_na, dk_na, dv_na, dbias = na_bwd(z_na, bias, do_a, NH, S)
    g_rpb = na_rpb_grad(dbias)
    dq_n, dq_rr, dk_n, dv_m, dk_rr = mla_bwd(q_n, q_r, k_n, v_m, k_r, lse, do_b, MH, S)
    dq_r = rope(dq_rr, cos_t, -sin_t, name="rope_q_bwd", out_dtype=BF16)
    dz_rest = rope(dk_rr, cos_t, -sin_t, name="rope_k_bwd", out_dtype=BF16, into=(dz_rest, rest_w),
                   cb=off_kr // kr_w, zero_cols=kr_w - LANES)
    gw_uq_n = mm(c_q, dq_n, name="mla_dw_q_nope", ta=True, out_dtype=BF16)
    gw_uq_r = mm(c_q, dq_r, name="mla_dw_q_rope", ta=True, out_dtype=BF16)
    dc_q = mm(dq_n, wuq_n, name="mla_dcq_nope", tb=True)
    dc_q = mm(dq_r, wuq_r, name="mla_dcq_rope", tb=True, res=dc_q)
    gw_uk = mm(c_kv, dk_n, name="mla_dw_k", ta=True, out_dtype=BF16)
    gw_uv = mm(c_kv, dv_m, name="mla_dw_v", ta=True, out_dtype=BF16)
    dc_kv = mm(dk_n, wuk, name="mla_dckv_k", tb=True)
    dc_kv = mm(dv_m, wuv, name="mla_dckv_v", tb=True, res=dc_kv)
    dz_rest, g_qa = norm_bwd((z, QR, off_ql // QR), q_a_norm, dc_q, name="q_a_dnorm", want_f32=False, bf16_alpha=1.0,
                             into=(dz_rest, rest_w), cb=off_ql // QR)
    dz_rest, g_kva = norm_bwd((z, KVR, off_kvl // KVR), kv_a_norm, dc_kv, name="kv_a_dnorm", want_f32=False,
                              bf16_alpha=1.0, into=(dz_rest, rest_w), cb=off_kvl // KVR)
    dz_na = jnp.concatenate([dq_na, dk_na, dv_na], axis=1)
    gw_na = mm(u_mix, dz_na, name="mix_dw_in_na", ta=True, out_dtype=BF16)
    gw_rest = mm(u_mix, dz_rest, name="mix_dw_in_rest", ta=True, out_dtype=BF16)

    def to_stack(g2d):
        k, n = g2d.shape
        return g2d.reshape(k, N_CHIPS, n // N_CHIPS).transpose(1, 0, 2)

    gw_uq = jnp.concatenate([gw_uq_n.reshape(QR, MH, MLA_NOPE), gw_uq_r.reshape(QR, MH, LANES)[:, :, :MLA_ROPE]],
                            axis=2).reshape(QR, MH * MLA_QK)
    G["w_uq"] = to_stack(gw_uq)
    gw_ukv = jnp.stack([gw_uk.reshape(KVR, MH, HEAD_DIM), gw_uv.reshape(KVR, MH, HEAD_DIM)], axis=2)
    G["w_ukv"] = to_stack(gw_ukv.reshape(KVR, MH * 2 * HEAD_DIM))
    token_mix = reduce_start(["w_out", "w_branch_a", "w_branch_b", "w_uq", "w_ukv"], "mix")
    G["w_in"] = w_in_shards(gw_na, gw_rest)
    token_win = reduce_start(["w_in"], "win")
    du_mix = mm(dz_na, w_na, name="mix_du_na", tb=True, after=[token_mix, token_win])
    du_mix = mm(dz_rest, w_rest, name="mix_du_rest", tb=True, res=du_mix)
    dh1, dh1_half, g_mix = norm_bwd(h1, mix_norm, du_mix, name="mix_dnorm", res=dh2, bf16_alpha=0.5)
    grad_x, g_ffn1 = ffn_bwd(xs, ffn1_norm, ffn1_saved, dh1, dh1_half, "ffn1", "ffn1", last=True)

    small_g = {"ffn1_norm": g_ffn1, "mix_norm": g_mix, "q_a_norm": g_qa, "kv_a_norm": g_kva, "na_rpb": g_rpb,
               "ffn2_norm": g_ffn2, "pl_norm": g_pl, "final_norm": g_final}
    sizes = [int(np.prod(W[n].shape)) for n in small]
    total = sum(sizes)
    padded = -(-total // (8 * LANES)) * (8 * LANES)

    def pack(parts):
        flat = jnp.concatenate([jnp.reshape(parts[n], (-1,)).astype(F32) for n in small]
                               + [jnp.zeros((padded - total,), F32)])
        return flat.reshape(padded // LANES, LANES)

    def unpack(a):
        flat, out, o = a.reshape(-1), {}, 0
        for n, sz in zip(small, sizes):
            out[n] = flat[o:o + sz].reshape(W[n].shape)
            o += sz
        return out

    g_small = sum_devices(gather_small(pack(small_g)))
    d_small, m_small, v_small = adamw(pack(W), g_small, pack(Mo), pack(Vo), name="adamw_small")
    grads = unpack(g_small)
    delta, new_m, new_v = unpack(d_small), unpack(m_small), unpack(v_small)

    after = [grad_x]
    for entry in pending:
        after = reduce_finish(entry, after)

    return (loss, grad_x[None], *[grads[n] for n in order], *[delta[n] for n in order],
            *[new_m[n] for n in order], *[new_v[n] for n in order])
```

```python
import functools

import numpy as np
import jax
import jax.numpy as jnp
from jax import lax
from jax.experimental import pallas as pl
from jax.experimental.pallas import tpu as pltpu

F32 = jnp.float32
BF16 = jnp.bfloat16

VMEM_LIMIT_V7X = 56 * 1024 * 1024
VMEM_BUDGET_V7X = 40 * 1024 * 1024
LANES = 128

GRID_W = 64
NA_WIN_ROWS = 8
NA_WIN_COLS = 16
HEAD_DIM = 128
MLA_NOPE = 128
MLA_ROPE = 64
MLA_QK = MLA_NOPE + MLA_ROPE
ROPE_THETA = 10000.0
NORM_EPS = 1e-6
NEG_INF = -1e30
N_CHIPS = 4

ADAM_LR = 0.001
ADAM_B1 = 0.9
ADAM_B2 = 0.999
ADAM_EPS = 1e-08
ADAM_WD = 0.01
ADAM_STEP = 10

MESH = pl.DeviceIdType.MESH
ANY = pl.BlockSpec(memory_space=pl.ANY)


def _params(sem=None):
    return pltpu.CompilerParams(dimension_semantics=sem, vmem_limit_bytes=VMEM_LIMIT_V7X)


def _pick(n, target, align):
    best = None
    t = align
    while t <= min(n, target):
        if n % t == 0:
            best = t
        t += align
    return n if best is None else best


def mm(a, b, *, name, ta=False, tb=False, out_dtype=F32, res=None, alpha=1.0, b_stack=False, out_stack=False,
       exact=False, after=()):
    K, M = (a.shape if ta else a.shape[::-1])
    nst = kb = nb = None
    if b_stack:
        nst = b.shape[0]
        if tb:
            N, kb = b.shape[1], b.shape[2]
            Kb = nst * kb
        else:
            Kb, nb = b.shape[1], b.shape[2]
            N = nst * nb
    else:
        N, Kb = (b.shape if tb else b.shape[::-1])
    assert K == Kb, (a.shape, b.shape, ta, tb)
    if out_stack:
        assert N % N_CHIPS == 0
    n_unit = N // N_CHIPS if out_stack else (nb if nb is not None else N)
    tn = _pick(n_unit, 512, LANES) if n_unit % 512 == 0 or n_unit <= 512 else _pick(n_unit, 1536, LANES)
    m_align = LANES if ta else 16
    tm = _pick(M, 1024, m_align)
    isz = lambda t: jnp.dtype(t.dtype).itemsize
    osz = jnp.dtype(out_dtype).itemsize

    def vmem(tm_, tn_):
        return (2 * tm_ * K * isz(a) + 2 * K * tn_ * isz(b) + 2 * tm_ * tn_ * osz + tm_ * tn_ * 4
                + (tm_ * K * 2 if ta else 0) + (2 * tm_ * tn_ * isz(res) if res is not None else 0))

    while vmem(tm, tn) > VMEM_BUDGET_V7X and tm % 2 == 0 and (tm // 2) % m_align == 0:
        tm //= 2
    while vmem(tm, tn) > VMEM_BUDGET_V7X and tn % 2 == 0 and (tn // 2) % LANES == 0 and n_unit % (tn // 2) == 0:
        tn //= 2
    assert vmem(tm, tn) <= VMEM_BUDGET_V7X, (name, tm, tn, K)

    a_spec = pl.BlockSpec((K, tm), lambda i, j: (0, i)) if ta else pl.BlockSpec((tm, K), lambda i, j: (i, 0))
    if b_stack and not tb:
        q = nb // tn
        b_spec = pl.BlockSpec((None, K, tn), lambda i, j: (j // q, 0, j % q))
    elif b_stack and tb:
        b_spec = pl.BlockSpec((nst, tn, kb), lambda i, j: (0, j, 0))
    elif tb:
        b_spec = pl.BlockSpec((tn, K), lambda i, j: (j, 0))
    else:
        b_spec = pl.BlockSpec((K, tn), lambda i, j: (0, j))
    if out_stack:
        qo = (N // N_CHIPS) // tn
        o_spec = pl.BlockSpec((None, tm, tn), lambda i, j: (j // qo, i, j % qo))
        o_shape = jax.ShapeDtypeStruct((N_CHIPS, M, N // N_CHIPS), out_dtype)
    else:
        o_spec = pl.BlockSpec((tm, tn), lambda i, j: (i, j))
        o_shape = jax.ShapeDtypeStruct((M, N), out_dtype)
    has_res = res is not None
    nn = (((1,), (0,)), ((), ()))
    nt = (((1,), (1,)), ((), ()))

    def body(*refs):
        a_ref, b_ref = refs[:2]
        r_ref = refs[2] if has_res else None
        o_ref = refs[3 + len(after)] if has_res else refs[2 + len(after)]
        if ta:
            at_ref = refs[-1]

            @pl.when(pl.program_id(1) == 0)
            def _():
                at_ref[...] = a_ref[...].astype(BF16).T

            lhs = at_ref[...]
        elif exact:
            lhs = a_ref[...]
        else:
            lhs = a_ref[...].astype(BF16)
        if exact:
            total = lax.dot_general(lhs, b_ref[...], nt if tb else nn, preferred_element_type=F32,
                                    precision=lax.Precision.HIGHEST)
        elif b_stack and tb:
            total = None
            for s in range(nst):
                part = lax.dot_general(lhs[:, s * kb:(s + 1) * kb], b_ref[s].astype(BF16), nt,
                                       preferred_element_type=F32)
                total = part if total is None else total + part
        else:
            total = lax.dot_general(lhs, b_ref[...].astype(BF16), nt if tb else nn, preferred_element_type=F32)
        if alpha != 1.0:
            total = total * alpha
        if has_res:
            total = total + r_ref[...].astype(F32)
        o_ref[...] = total.astype(out_dtype)

    in_specs = [a_spec, b_spec]
    args = [a, b]
    if has_res:
        in_specs.append(pl.BlockSpec((tm, tn), lambda i, j: (i, j)))
        args.append(res)
    in_specs += [ANY] * len(after)
    args += list(after)
    return pl.pallas_call(
        body, name=name, grid=(M // tm, N // tn), in_specs=in_specs, out_specs=o_spec, out_shape=o_shape,
        scratch_shapes=[pltpu.VMEM((tm, K), BF16)] if ta else [],
        compiler_params=_params(("parallel", "arbitrary")),
    )(*args)


def rowwise(fn, rows, consts, outs, accs=(), *, tm, name, tn=None, into=None):
    rows = [r if isinstance(r, tuple) else (r, r.shape[1], 0) for r in rows]
    S = rows[0][0].shape[0]
    tm = _pick(S, tm, 16)
    nrow, ncon, nout = len(rows), len(consts), len(outs)
    outs = [o if len(o) == 3 else (o[0], o[1], None) for o in outs]
    if tn is None:
        grid = (S // tm,)
        in_specs = [pl.BlockSpec((tm, w), functools.partial(lambda i, cb: (i, cb), cb=cb)) for _, w, cb in rows]
        in_specs += [pl.BlockSpec(c.shape, lambda i: (0, 0)) for c in consts]
        out_specs = [pl.BlockSpec((tm, n), functools.partial(lambda i, cb: (i, cb), cb=cb or 0)) for n, _, cb in outs]
        out_specs += [pl.BlockSpec(s, lambda i: (0, 0)) for s in accs]
        sem = ("arbitrary",)
    else:
        assert not accs
        N = rows[0][1]
        grid = (S // tm, N // tn)
        in_specs = [pl.BlockSpec((tm, tn), lambda i, j: (i, j)) for _ in rows]
        in_specs += [pl.BlockSpec(c.shape, lambda i, j: (0, 0)) for c in consts]
        out_specs = [pl.BlockSpec((tm, tn), lambda i, j: (i, j)) for _ in outs]
        sem = ("parallel", "parallel")
    out_shape = [jax.ShapeDtypeStruct((S, n if cb is None else into[1]), dt) for n, dt, cb in outs]
    out_shape += [jax.ShapeDtypeStruct(s, F32) for s in accs]
    extra, aliases = [], {}
    if into is not None and into[0] is not None:
        extra = [into[0]]
        aliases = {nrow + ncon: [cb is not None for _, _, cb in outs].index(True)}

    def body(*refs):
        vals = fn(*[r[...] for r in refs[:nrow + ncon]])
        if not isinstance(vals, (tuple, list)):
            vals = (vals,)
        o_refs = refs[nrow + ncon + len(extra):]
        for o_ref, v in zip(o_refs[:nout], vals[:nout]):
            o_ref[...] = v.astype(o_ref.dtype)
        if accs:
            first = pl.program_id(0) == 0

            def accumulate(a_ref, v):
                @pl.when(first)
                def _():
                    a_ref[...] = v

                @pl.when(jnp.logical_not(first))
                def _():
                    a_ref[...] += v

            for a_ref, v in zip(o_refs[nout:], vals[nout:]):
                accumulate(a_ref, v.astype(F32))

    return pl.pallas_call(
        body, name=name, grid=grid, in_specs=in_specs + [ANY] * len(extra), out_specs=out_specs, out_shape=out_shape,
        input_output_aliases=aliases, compiler_params=_params(sem),
    )(*[r[0] for r in rows], *consts, *extra)


def _rstd(x):
    return lax.rsqrt(jnp.mean(x * x, axis=-1, keepdims=True) + NORM_EPS)


def norm_fwd(x, g, *, name, tm=256):
    w = x[1] if isinstance(x, tuple) else x.shape[1]

    def fn(xb, gb):
        return (xb * _rstd(xb)) * gb

    return rowwise(fn, [x], [g], [(w, BF16)], tm=tm, name=name)[0]


def norm_bwd(x, g, dn, *, name, res=None, want_f32=True, bf16_alpha=None, tm=256, into=None, cb=None):
    w = x[1] if isinstance(x, tuple) else x.shape[1]
    has_res = res is not None

    def fn(*blocks):
        if has_res:
            xb, dnb, rb, gb = blocks
        else:
            xb, dnb, gb = blocks
        r = _rstd(xb)
        xh = xb * r
        dxh = dnb * gb
        dx = r * (dxh - xh * jnp.mean(dxh * xh, axis=-1, keepdims=True))
        if has_res:
            dx = dx + rb
        out = []
        if want_f32:
            out.append(dx)
        if bf16_alpha is not None:
            out.append(dx * bf16_alpha if bf16_alpha != 1.0 else dx)
        out.append(jnp.sum(dnb * xh, axis=0, keepdims=True))
        return tuple(out)

    outs = ([(w, F32)] if want_f32 else []) + ([(w, BF16, cb)] if bf16_alpha is not None else [])
    rows = [x, dn] + ([res] if has_res else [])
    return rowwise(fn, rows, [g], outs, accs=[(1, w)], tm=tm, name=name, into=into)


def _sig(x):
    return jax.nn.sigmoid(x)


def swiglu_fwd(g, u, *, name):
    return rowwise(lambda gb, ub: gb * _sig(gb) * ub, [g, u], [], [(g.shape[1], BF16)], tm=256, name=name,
                   tn=_pick(g.shape[1], 1536, LANES))[0]


def swiglu_bwd(g, u, da, *, name):
    def fn(gb, ub, dab):
        s = _sig(gb)
        return dab * ub * (s + gb * s * (1.0 - s)), dab * (gb * s)

    n = g.shape[1]
    return rowwise(fn, [g, u, da], [], [(n, BF16), (n, BF16)], tm=256, name=name, tn=_pick(n, 1536, LANES))


def rope(x, cos, sin_signed, *, name, out_dtype, into=None, cb=None, zero_cols=0):
    w = x[1] if isinstance(x, tuple) else x.shape[1]
    half = MLA_ROPE // 2

    def fn(xb, cb, sb):
        lane = lax.broadcasted_iota(jnp.int32, cb.shape, 1)
        outs = []
        for hb in range(w // LANES):
            blk = xb[:, hb * LANES:(hb + 1) * LANES]
            partner = jnp.where(lane < half, pltpu.roll(blk, LANES - half, 1), pltpu.roll(blk, half, 1))
            outs.append(blk * cb + partner * sb)
        if zero_cols:
            outs.append(jnp.zeros((xb.shape[0], zero_cols), xb.dtype))
        return outs[0] if len(outs) == 1 else jnp.concatenate(outs, axis=1)

    return rowwise(fn, [x, cos, sin_signed], [], [(w + zero_cols, out_dtype, cb)], tm=256, name=name, into=into)[0]


def _na_tables():
    cols = np.arange(GRID_W)
    kw = NA_WIN_COLS
    dc = np.clip(cols[None, :] - cols[:, None], -(kw - 1), kw - 1) + (kw - 1)
    onehot = np.zeros((LANES, GRID_W * GRID_W), np.float32)
    onehot[dc.reshape(-1), np.arange(GRID_W * GRID_W)] = 1.0
    col_start = np.clip(cols - kw // 2, 0, GRID_W - kw)
    mask = (cols[None, :] >= col_start[:, None]) & (cols[None, :] < col_start[:, None] + kw)
    return onehot, np.where(mask, 0.0, NEG_INF).astype(np.float32)


def na_bias(rpb, after=()):
    H = rpb.shape[0]
    nr, kh = 2 * NA_WIN_ROWS - 1, NA_WIN_ROWS
    onehot, maskb = _na_tables()
    rp = jnp.pad(rpb.reshape(H * nr, 2 * NA_WIN_COLS - 1), ((0, 0), (0, LANES - (2 * NA_WIN_COLS - 1))))
    t1 = mm(rp, jnp.asarray(onehot), name="na_bias_table", exact=True, after=after).reshape(H, nr, GRID_W, GRID_W)
    t1 = t1 + jnp.asarray(maskb)[None, None]
    per_t = [jnp.stack([t1[:, i - t + kh - 1] for i in range(kh)], axis=2) for t in range(kh)]
    return jnp.stack(per_t, axis=1).reshape(H, kh, GRID_W, kh * GRID_W)


def na_rpb_grad(db):
    H = db.shape[0]
    nr, kh = 2 * NA_WIN_ROWS - 1, NA_WIN_ROWS
    onehot, _ = _na_tables()
    db = db.reshape(H, kh, GRID_W, kh, GRID_W)
    per_dr = []
    for dri in range(nr):
        terms = [db[:, t, :, dri - (kh - 1) + t, :] for t in range(kh) if 0 <= dri - (kh - 1) + t < kh]
        per_dr.append(functools.reduce(jnp.add, terms))
    dt1 = jnp.stack(per_dr, axis=1).reshape(H * nr, GRID_W * GRID_W)
    g = mm(dt1, jnp.asarray(onehot), name="na_rpb_grad", tb=True, exact=True)
    return g[:, :2 * NA_WIN_COLS - 1].reshape(H, nr, 2 * NA_WIN_COLS - 1)


def _na_first_row(r, rows):
    return jnp.clip(r - NA_WIN_ROWS // 2, 0, rows - NA_WIN_ROWS)


NA_ROWS_PER_STEP = 8


def _na_probs(q, k_ref, b_ref, r, rows):
    first = _na_first_row(r, rows)
    start = pl.multiple_of(first * GRID_W, GRID_W)
    k = k_ref[pl.ds(start, NA_WIN_ROWS * GRID_W), :]
    s = lax.dot_general(q, k, (((1,), (1,)), ((), ())), preferred_element_type=F32)
    s = s * (HEAD_DIM ** -0.5) + b_ref[r - first]
    m = jnp.max(s, axis=-1, keepdims=True)
    e = jnp.exp(s - m)
    return k, e / jnp.sum(e, axis=-1, keepdims=True), start, r - first


def na_fwd(z, bias, H, S):
    rows = S // GRID_W
    nkeys = NA_WIN_ROWS * GRID_W
    rb = _pick(rows, NA_ROWS_PER_STEP, 1)

    def body(q_ref, k_ref, v_ref, b_ref, o_ref):
        for j in range(rb):
            r = pl.program_id(1) * rb + j
            rows_j = pl.ds(j * GRID_W, GRID_W)
            _, p, start, _ = _na_probs(q_ref[rows_j, :], k_ref, b_ref, r, rows)
            v = v_ref[pl.ds(start, nkeys), :]
            o_ref[rows_j, :] = jnp.dot(p.astype(BF16), v, preferred_element_type=F32).astype(o_ref.dtype)

    return pl.pallas_call(
        body, name="na_fwd", grid=(H, rows // rb),
        in_specs=[pl.BlockSpec((rb * GRID_W, HEAD_DIM), lambda h, i: (i, h)),
                  pl.BlockSpec((S, HEAD_DIM), lambda h, i: (0, H + h)),
                  pl.BlockSpec((S, HEAD_DIM), lambda h, i: (0, 2 * H + h)),
                  pl.BlockSpec((None, NA_WIN_ROWS, GRID_W, nkeys), lambda h, i: (h, 0, 0, 0))],
        out_specs=pl.BlockSpec((rb * GRID_W, HEAD_DIM), lambda h, i: (i, h)),
        out_shape=jax.ShapeDtypeStruct((S, H * HEAD_DIM), BF16),
        compiler_params=_params(("parallel", "arbitrary")),
    )(z, z, z, bias)


def na_bwd(z, bias, do, H, S):
    rows = S // GRID_W
    nkeys = NA_WIN_ROWS * GRID_W
    rb = _pick(rows, NA_ROWS_PER_STEP, 1)
    tn_dims = (((0,), (0,)), ((), ()))

    def body(q_ref, k_ref, v_ref, b_ref, do_ref, dq_ref, dk_ref, dv_ref, db_ref, dk_acc, dv_acc):
        i = pl.program_id(1)

        @pl.when(i == 0)
        def _():
            dk_acc[...] = jnp.zeros_like(dk_acc)
            dv_acc[...] = jnp.zeros_like(dv_acc)
            db_ref[...] = jnp.zeros_like(db_ref)

        for j in range(rb):
            rows_j = pl.ds(j * GRID_W, GRID_W)
            q = q_ref[rows_j, :]
            k, p, start, t = _na_probs(q, k_ref, b_ref, i * rb + j, rows)
            keys = pl.ds(start, nkeys)
            dob = do_ref[rows_j, :].astype(BF16)
            dp = lax.dot_general(dob, v_ref[keys, :], (((1,), (1,)), ((), ())), preferred_element_type=F32)
            ds = p * (dp - jnp.sum(dp * p, axis=-1, keepdims=True))
            dsb = (ds * (HEAD_DIM ** -0.5)).astype(BF16)
            dq_ref[rows_j, :] = jnp.dot(dsb, k, preferred_element_type=F32).astype(dq_ref.dtype)
            dk_acc[keys, :] += lax.dot_general(dsb, q, tn_dims, preferred_element_type=F32)
            dv_acc[keys, :] += lax.dot_general(p.astype(BF16), dob, tn_dims, preferred_element_type=F32)
            db_ref[t] += ds

        @pl.when(i == rows // rb - 1)
        def _():
            dk_ref[...] = dk_acc[...].astype(dk_ref.dtype)
            dv_ref[...] = dv_acc[...].astype(dv_ref.dtype)

    W = H * HEAD_DIM
    qspec = pl.BlockSpec((rb * GRID_W, HEAD_DIM), lambda h, i: (i, h))
    bspec = pl.BlockSpec((None, NA_WIN_ROWS, GRID_W, nkeys), lambda h, i: (h, 0, 0, 0))
    return pl.pallas_call(
        body, name="na_bwd", grid=(H, rows // rb),
        in_specs=[qspec, pl.BlockSpec((S, HEAD_DIM), lambda h, i: (0, H + h)),
                  pl.BlockSpec((S, HEAD_DIM), lambda h, i: (0, 2 * H + h)), bspec, qspec],
        out_specs=[qspec, pl.BlockSpec((S, HEAD_DIM), lambda h, i: (0, h)),
                   pl.BlockSpec((S, HEAD_DIM), lambda h, i: (0, h)), bspec],
        out_shape=[jax.ShapeDtypeStruct((S, W), BF16)] * 3 + [jax.ShapeDtypeStruct((H, NA_WIN_ROWS, GRID_W, nkeys), F32)],
        scratch_shapes=[pltpu.VMEM((S, HEAD_DIM), F32)] * 2,
        compiler_params=_params(("arbitrary", "arbitrary")),
    )(z, z, z, bias, do)


def _mla_keys(kn_ref, kr_ref, kcat):
    @pl.when(pl.program_id(1) == 0)
    def _():
        kcat[:, :HEAD_DIM] = kn_ref[...]
        kcat[:, HEAD_DIM:] = kr_ref[...]


MLA_LOG2_SCALE = (MLA_QK ** -0.5) * 1.4426950408889634


def _mla_scores(qn_ref, qr_ref, kcat):
    qcat = jnp.concatenate([qn_ref[...], qr_ref[...]], axis=1)
    return qcat, lax.dot_general(qcat, kcat[...], (((1,), (1,)), ((), ())), preferred_element_type=F32)


def mla_fwd(qn, qr, kn, v, kr, H, S):
    tq = _pick(S, 256, 16)

    def body(qn_ref, qr_ref, kn_ref, v_ref, kr_ref, o_ref, lse_ref, kcat):
        _mla_keys(kn_ref, kr_ref, kcat)
        _, s = _mla_scores(qn_ref, qr_ref, kcat)
        m = jnp.max(s, axis=-1, keepdims=True)
        e = jnp.exp2((s - m) * MLA_LOG2_SCALE)
        l = jnp.sum(e, axis=-1, keepdims=True)
        o = jnp.dot(e.astype(BF16), v_ref[...], preferred_element_type=F32)
        o_ref[...] = (o / l).astype(o_ref.dtype)
        lse_ref[...] = jnp.broadcast_to(m * MLA_LOG2_SCALE + jnp.log2(l), lse_ref.shape)

    qspec = pl.BlockSpec((tq, HEAD_DIM), lambda h, i: (i, h))
    kspec = pl.BlockSpec((S, HEAD_DIM), lambda h, i: (0, h))
    return pl.pallas_call(
        body, name="mla_fwd", grid=(H, S // tq),
        in_specs=[qspec, qspec, kspec, kspec, pl.BlockSpec((S, LANES), lambda h, i: (0, 0))],
        out_specs=[qspec, qspec],
        out_shape=[jax.ShapeDtypeStruct((S, H * HEAD_DIM), BF16), jax.ShapeDtypeStruct((S, H * LANES), F32)],
        scratch_shapes=[pltpu.VMEM((S, 2 * HEAD_DIM), BF16)],
        compiler_params=_params(("parallel", "arbitrary")),
    )(qn, qr, kn, v, kr)


def mla_bwd(qn, qr, kn, v, kr, lse, do, H, S):
    tq = _pick(S, 256, 16)
    nt = (((1,), (1,)), ((), ()))
    tn_dims = (((0,), (0,)), ((), ()))

    def body(qn_ref, qr_ref, kn_ref, v_ref, kr_ref, lse_ref, do_ref, dqn_ref, dqr_ref, dkn_ref, dv_ref, dkr_ref, kcat):
        h, i = pl.program_id(0), pl.program_id(1)
        _mla_keys(kn_ref, kr_ref, kcat)
        qcat, s = _mla_scores(qn_ref, qr_ref, kcat)
        p = jnp.exp2(s * MLA_LOG2_SCALE - lse_ref[:, 0:1])
        dob = do_ref[...].astype(BF16)
        dp = lax.dot_general(dob, v_ref[...], nt, preferred_element_type=F32)
        ds = p * (dp - jnp.sum(dp * p, axis=-1, keepdims=True))
        dsb = (ds * (MLA_QK ** -0.5)).astype(BF16)
        dq = jnp.dot(dsb, kcat[...], preferred_element_type=F32)
        dqn_ref[...] = dq[:, :HEAD_DIM].astype(dqn_ref.dtype)
        dqr_ref[...] = dq[:, HEAD_DIM:].astype(dqr_ref.dtype)

        @pl.when(i == 0)
        def _():
            dkn_ref[...] = jnp.zeros_like(dkn_ref)
            dv_ref[...] = jnp.zeros_like(dv_ref)

        @pl.when(jnp.logical_and(i == 0, h == 0))
        def _():
            dkr_ref[...] = jnp.zeros_like(dkr_ref)

        dk = lax.dot_general(dsb, qcat, tn_dims, preferred_element_type=F32)
        dkn_ref[...] += dk[:, :HEAD_DIM]
        dkr_ref[...] += dk[:, HEAD_DIM:]
        dv_ref[...] += lax.dot_general(p.astype(BF16), dob, tn_dims, preferred_element_type=F32)

    qspec = pl.BlockSpec((tq, HEAD_DIM), lambda h, i: (i, h))
    kspec = pl.BlockSpec((S, HEAD_DIM), lambda h, i: (0, h))
    rspec = pl.BlockSpec((S, LANES), lambda h, i: (0, 0))
    W = H * HEAD_DIM
    return pl.pallas_call(
        body, name="mla_bwd", grid=(H, S // tq),
        in_specs=[qspec, qspec, kspec, kspec, rspec, qspec, qspec],
        out_specs=[qspec, qspec, kspec, kspec, rspec],
        out_shape=[jax.ShapeDtypeStruct((S, W), BF16), jax.ShapeDtypeStruct((S, W), F32),
                   jax.ShapeDtypeStruct((S, W), F32), jax.ShapeDtypeStruct((S, W), F32),
                   jax.ShapeDtypeStruct((S, LANES), F32)],
        scratch_shapes=[pltpu.VMEM((S, 2 * HEAD_DIM), BF16)],
        compiler_params=_params(("arbitrary", "arbitrary")),
    )(qn, qr, kn, v, kr, lse, do)


def _place():
    return lax.axis_index("x"), lax.axis_index("y"), lax.axis_index("c")


def _other_chips(x, y):
    return [(1 - x, y), (x, 1 - y), (1 - x, 1 - y)]


def _remote(src, dst, send_sem, recv_sem, to):
    return pltpu.make_async_remote_copy(src_ref=src, dst_ref=dst, send_sem=send_sem, recv_sem=recv_sem,
                                        device_id=to, device_id_type=MESH)


HBM = pl.BlockSpec(memory_space=pltpu.HBM)
SEM = pl.BlockSpec(memory_space=pltpu.SEMAPHORE)
EFFECT = pltpu.SideEffectType.DATAFLOW_SIDE_EFFECTING


def _in_hbm(a):
    return pltpu.with_memory_space_constraint(a, pltpu.HBM)


TOKEN = jax.ShapeDtypeStruct((8, LANES), F32)
IN_VMEM = pl.BlockSpec(memory_space=pltpu.VMEM)


def gather_start(shards, landings, after, tag):
    n = len(shards)

    def body(*refs):
        ins, lands = refs[:n], refs[n:2 * n]
        send, recv = refs[2 * n + len(after)], refs[2 * n + len(after) + 1]
        token = refs[-1]
        x, y, c = _place()
        me = 2 * x + y
        for w in range(n):
            for k, (px, py) in enumerate(_other_chips(x, y)):
                _remote(ins[w].at[c], lands[w].at[me, c], send.at[3 * w + k], recv.at[3 * w + k], (px, py, c)).start()
        token[...] = jnp.zeros_like(token)

    bufs = list(shards) + list(landings)
    outs = pl.pallas_call(
        body, name="gather_start_" + tag,
        out_shape=(pltpu.SemaphoreType.DMA((3 * n,)),) * 2 + tuple(pltpu.HBM(b.shape, b.dtype) for b in bufs) + (TOKEN,),
        in_specs=[HBM] * (2 * n) + [ANY] * len(after), out_specs=tuple([SEM, SEM] + [HBM] * (2 * n) + [IN_VMEM]),
        input_output_aliases={i: 2 + i for i in range(2 * n)},
        compiler_params=pltpu.CompilerParams(has_side_effects=EFFECT),
    )(*[_in_hbm(b) for b in bufs], *after)
    return (outs[0], outs[1]), outs[2:2 + n], outs[2 + n:2 + 2 * n], outs[-1]


def gather_wait(sems, shards, landings, after, tag):
    n = len(shards)
    send, recv = sems

    def body(*refs):
        ins, lands = refs[:n], refs[n:2 * n]
        send_sem, recv_sem = refs[2 * n], refs[2 * n + 1]
        x, y, c = _place()
        me = 2 * x + y
        for w in range(n):
            for k, (px, py) in enumerate(_other_chips(x, y)):
                cp = _remote(ins[w].at[c], lands[w].at[2 * px + py, c], send_sem.at[3 * w + k], recv_sem.at[3 * w + k],
                             (px, py, c))
                cp.wait_send()
                cp.wait_recv()

    bufs = list(shards) + list(landings)
    outs = pl.pallas_call(
        body, name="gather_wait_" + tag, out_shape=tuple(pltpu.HBM(b.shape, b.dtype) for b in bufs),
        in_specs=[HBM] * (2 * n) + [SEM, SEM] + [ANY] * len(after), out_specs=tuple([HBM] * (2 * n)),
        input_output_aliases={i: i for i in range(2 * n)},
        compiler_params=pltpu.CompilerParams(has_side_effects=EFFECT),
    )(*bufs, send, recv, *after)
    return outs[:n], outs[n:]


def gather_forward(landings, tag):
    n = len(landings)

    def body(*refs):
        ins, outs = refs[:n], refs[n:2 * n]
        send, recv = refs[2 * n:]
        x, y, c = _place()
        sibling = (x, y, 1 - c)
        cps = []
        for w in range(n):
            for k, (px, py) in enumerate(_other_chips(x, y)):
                j = 2 * px + py
                cp = _remote(ins[w].at[j, c], outs[w].at[j, c], send.at[3 * w + k], recv.at[3 * w + k], sibling)
                cp.start()
                cps.append(cp)
        for w in range(n):
            for k, (px, py) in enumerate(_other_chips(x, y)):
                blk = outs[w].at[2 * px + py, 1 - c]
                _remote(blk, blk, send.at[3 * w + k], recv.at[3 * w + k], sibling).wait_recv()
        for cp in cps:
            cp.wait_send()

    return pl.pallas_call(
        body, name="gather_forward_" + tag, in_specs=[ANY] * n, out_specs=[ANY] * n,
        out_shape=[jax.ShapeDtypeStruct(a.shape, a.dtype) for a in landings],
        input_output_aliases={i: i for i in range(n)},
        scratch_shapes=[pltpu.SemaphoreType.DMA((3 * n,)), pltpu.SemaphoreType.DMA((3 * n,))],
    )(*landings)


def pair_exchange(grads, tag):
    n = len(grads)

    def body(*refs):
        ins, outs = refs[:n], refs[n:2 * n]
        send, recv = refs[2 * n:]
        x, y, c = _place()
        cps = []
        for w in range(n):
            cp = _remote(ins[w].at[:, 1 - c], outs[w], send.at[w], recv.at[w], (x, y, 1 - c))
            cp.start()
            cps.append(cp)
        for cp in cps:
            cp.wait()

    return pl.pallas_call(
        body, name="grad_pair_exchange_" + tag, in_specs=[ANY] * n, out_specs=[ANY] * n,
        out_shape=[jax.ShapeDtypeStruct((g.shape[0],) + g.shape[2:], g.dtype) for g in grads],
        scratch_shapes=[pltpu.SemaphoreType.DMA((n,)), pltpu.SemaphoreType.DMA((n,))],
    )(*grads)


def scatter_start(sums, landings, tag):
    n = len(sums)

    def body(*refs):
        ins, lands = refs[:n], refs[n:2 * n]
        send, recv = refs[2 * n], refs[2 * n + 1]
        token = refs[-1]
        x, y, c = _place()
        for w in range(n):
            for k, (px, py) in enumerate(_other_chips(x, y)):
                _remote(ins[w].at[2 * px + py], lands[w].at[k], send.at[3 * w + k], recv.at[3 * w + k], (px, py, c)).start()
        token[...] = jnp.zeros_like(token)

    bufs = list(sums) + list(landings)
    outs = pl.pallas_call(
        body, name="scatter_start_" + tag,
        out_shape=(pltpu.SemaphoreType.DMA((3 * n,)),) * 2 + tuple(pltpu.HBM(b.shape, b.dtype) for b in bufs) + (TOKEN,),
        in_specs=[HBM] * (2 * n), out_specs=tuple([SEM, SEM] + [HBM] * (2 * n) + [IN_VMEM]),
        input_output_aliases={i: 2 + i for i in range(2 * n)},
        compiler_params=pltpu.CompilerParams(has_side_effects=EFFECT),
    )(*[_in_hbm(b) for b in bufs])
    return (outs[0], outs[1]), outs[2:2 + n], outs[2 + n:2 + 2 * n], outs[-1]


def scatter_wait(sems, sums, landings, after, tag):
    n = len(sums)

    def body(*refs):
        ins, lands = refs[:n], refs[n:2 * n]
        send, recv = refs[2 * n], refs[2 * n + 1]
        x, y, c = _place()
        for w in range(n):
            for k, (px, py) in enumerate(_other_chips(x, y)):
                cp = _remote(ins[w].at[2 * px + py], lands[w].at[k], send.at[3 * w + k], recv.at[3 * w + k], (px, py, c))
                cp.wait_send()
                cp.wait_recv()

    bufs = list(sums) + list(landings)
    outs = pl.pallas_call(
        body, name="scatter_wait_" + tag, out_shape=tuple(pltpu.HBM(b.shape, b.dtype) for b in bufs),
        in_specs=[HBM] * (2 * n) + [SEM, SEM] + [ANY] * len(after), out_specs=tuple([HBM] * (2 * n)),
        input_output_aliases={i: i for i in range(2 * n)},
        compiler_params=pltpu.CompilerParams(has_side_effects=EFFECT),
    )(*bufs, sems[0], sems[1], *after)
    return outs[:n], outs[n:]


def half_exchange(halves, tag):
    n = len(halves)

    def body(*refs):
        ins, outs = refs[:n], refs[n:2 * n]
        send, recv = refs[2 * n:]
        x, y, c = _place()
        cps = []
        for w in range(n):
            cp = _remote(ins[w], outs[w], send.at[w], recv.at[w], (x, y, 1 - c))
            cp.start()
            cps.append(cp)
        for cp in cps:
            cp.wait()

    return pl.pallas_call(
        body, name="grad_half_exchange_" + tag, in_specs=[ANY] * n, out_specs=[ANY] * n,
        out_shape=[jax.ShapeDtypeStruct(h.shape, h.dtype) for h in halves],
        scratch_shapes=[pltpu.SemaphoreType.DMA((n,)), pltpu.SemaphoreType.DMA((n,))],
    )(*halves)


def gather_small(v):
    def body(v_ref, o_ref, send, recv, local):
        x, y, c = _place()
        me = 4 * x + 2 * y + c
        own = pltpu.make_async_copy(v_ref, o_ref.at[me], local)
        own.start()
        cps = []
        for k in range(1, 8):
            fx, fy, fc = (k >> 2) & 1, (k >> 1) & 1, k & 1
            to = (x ^ fx if fx else x, y ^ fy if fy else y, c ^ fc if fc else c)
            cp = _remote(v_ref, o_ref.at[me], send.at[k - 1], recv.at[k - 1], to)
            cp.start()
            cps.append(cp)
        for k in range(1, 8):
            fx, fy, fc = (k >> 2) & 1, (k >> 1) & 1, k & 1
            px, py, pc = (x ^ fx if fx else x, y ^ fy if fy else y, c ^ fc if fc else c)
            cps[k - 1].wait_send()
            _remote(v_ref, o_ref.at[4 * px + 2 * py + pc], send.at[k - 1], recv.at[k - 1], (px, py, pc)).wait_recv()
        own.wait()

    return pl.pallas_call(
        body, name="gather_small_grads", in_specs=[ANY], out_specs=ANY,
        out_shape=jax.ShapeDtypeStruct((8,) + v.shape, v.dtype),
        scratch_shapes=[pltpu.SemaphoreType.DMA((7,)), pltpu.SemaphoreType.DMA((7,)), pltpu.SemaphoreType.DMA],
    )(v)


def _row_tile(rows, cols, nbuf_bytes):
    tm = _pick(rows, 512, 16)
    while tm * cols * nbuf_bytes * 2 > VMEM_BUDGET_V7X and tm % 32 == 0:
        tm //= 2
    return tm


def pair_sum(g, r, c_idx, tag):
    _, _, rows, cols = g.shape
    tm = _row_tile(rows, cols, 2 + 2 + 2)
    nb = rows // tm

    def body(c_ref, g_ref, r_ref, o_ref):
        o_ref[...] = (g_ref[...].astype(F32) + r_ref[...].astype(F32)).astype(o_ref.dtype)

    gs = pltpu.PrefetchScalarGridSpec(
        num_scalar_prefetch=1, grid=(N_CHIPS, nb),
        in_specs=[pl.BlockSpec((None, None, tm, cols), lambda j, i, c_ref: (j, c_ref[0], i, 0)),
                  pl.BlockSpec((None, tm, cols), lambda j, i, c_ref: (j, i, 0))],
        out_specs=pl.BlockSpec((None, tm, cols), lambda j, i, c_ref: (j, i, 0)))
    return pl.pallas_call(body, name="grad_pair_sum_" + tag, grid_spec=gs,
                          out_shape=jax.ShapeDtypeStruct(r.shape, BF16),
                          compiler_params=_params(("arbitrary", "arbitrary")))(c_idx, g, r)


def chip_sum(s, r, j_idx, tag):
    _, rows, cols = s.shape
    tm = _row_tile(rows, cols, 2 + 3 * 2 + 4)
    nb = rows // tm

    def body(j_ref, s_ref, r_ref, o_ref):
        t = s_ref[...].astype(F32)
        for k in range(3):
            t = t + r_ref[k].astype(F32)
        o_ref[...] = t

    gs = pltpu.PrefetchScalarGridSpec(
        num_scalar_prefetch=1, grid=(nb,),
        in_specs=[pl.BlockSpec((None, tm, cols), lambda i, j_ref: (j_ref[0], i, 0)),
                  pl.BlockSpec((3, tm, cols), lambda i, j_ref: (0, i, 0))],
        out_specs=pl.BlockSpec((tm, cols), lambda i, j_ref: (i, 0)))
    return pl.pallas_call(body, name="grad_chip_sum_" + tag, grid_spec=gs,
                          out_shape=jax.ShapeDtypeStruct((rows, cols), F32),
                          compiler_params=_params(("arbitrary",)))(j_idx, s, r)


def adamw(w, g, m, v, *, name):
    rows, cols = w.shape
    tm = _row_tile(rows, cols, 7 * 4)

    return rowwise(_adamw_math, [w, g, m, v], [], [(cols, F32)] * 3, tm=tm, name=name)


def _adamw_math(wb, gb, mb, vb):
    m2 = ADAM_B1 * mb + (1.0 - ADAM_B1) * gb
    v2 = ADAM_B2 * vb + (1.0 - ADAM_B2) * (gb * gb)
    m_hat = m2 / (1.0 - ADAM_B1 ** ADAM_STEP)
    v_hat = v2 / (1.0 - ADAM_B2 ** ADAM_STEP)
    delta = -ADAM_LR * (m_hat / (jnp.sqrt(v_hat) + ADAM_EPS) + ADAM_WD * wb)
    return delta, m2, v2


def adamw_shard(w, g_own, g_sib, m, v, c_idx, *, name):
    rows, cols = g_own.shape
    tm = _row_tile(rows, cols, 9 * 4)
    nb = rows // tm

    def body(c_ref, w_ref, go_ref, gs_ref, m_ref, v_ref, g_out, d_out, m_out, v_out):
        gb = jnp.where(pl.program_id(0) == c_ref[0], go_ref[...], gs_ref[...])
        delta, m2, v2 = _adamw_math(w_ref[...], gb, m_ref[...], v_ref[...])
        g_out[...] = gb
        d_out[...] = delta
        m_out[...] = m2
        v_out[...] = v2

    full = pl.BlockSpec((tm, cols), lambda h, i, c_ref: (h * nb + i, 0))
    own = pl.BlockSpec((tm, cols), lambda h, i, c_ref: (jnp.where(h == c_ref[0], i, 0), 0))
    sib = pl.BlockSpec((tm, cols), lambda h, i, c_ref: (jnp.where(h == c_ref[0], 0, i), 0))
    gs = pltpu.PrefetchScalarGridSpec(num_scalar_prefetch=1, grid=(2, nb), in_specs=[full, own, sib, full, full],
                                      out_specs=[full] * 4)
    return pl.pallas_call(body, name=name, grid_spec=gs, out_shape=[jax.ShapeDtypeStruct(w.shape, F32)] * 4,
                          compiler_params=_params(("arbitrary", "arbitrary")))(c_idx, w, g_own, g_sib, m, v)


def sum_devices(a):
    def body(a_ref, o_ref):
        t = a_ref[0]
        for k in range(1, 8):
            t = t + a_ref[k]
        o_ref[...] = t

    return pl.pallas_call(body, name="sum_small_grads", out_shape=jax.ShapeDtypeStruct(a.shape[1:], a.dtype))(a)


def _halves(w2d):
    r, c = w2d.shape
    return w2d.reshape(2, r // 2, c)


def kernel(x, p, ffn1_norm, ffn1_w_gate, ffn1_w_up, ffn1_w_down, mix_norm, w_in, q_a_norm, w_uq, kv_a_norm, w_ukv, na_rpb, w_branch_a, w_branch_b, w_out, ffn2_norm, ffn2_w_gate, ffn2_w_up, ffn2_w_down, pl_norm, w_pl, w_pl_gate, final_norm, loss_target, m_ffn1_norm, m_ffn1_w_gate, m_ffn1_w_up, m_ffn1_w_down, m_mix_norm, m_w_in, m_q_a_norm, m_w_uq, m_kv_a_norm, m_w_ukv, m_na_rpb, m_w_branch_a, m_w_branch_b, m_w_out, m_ffn2_norm, m_ffn2_w_gate, m_ffn2_w_up, m_ffn2_w_down, m_pl_norm, m_w_pl, m_w_pl_gate, m_final_norm, v_ffn1_norm, v_ffn1_w_gate, v_ffn1_w_up, v_ffn1_w_down, v_mix_norm, v_w_in, v_q_a_norm, v_w_uq, v_kv_a_norm, v_w_ukv, v_na_rpb, v_w_branch_a, v_w_branch_b, v_w_out, v_ffn2_norm, v_ffn2_w_gate, v_ffn2_w_up, v_ffn2_w_down, v_pl_norm, v_w_pl, v_w_pl_gate, v_final_norm):
    big = ["ffn1_w_gate", "ffn1_w_up", "ffn1_w_down", "w_in", "w_uq", "w_ukv", "w_branch_a", "w_branch_b", "w_out",
           "ffn2_w_gate", "ffn2_w_up", "ffn2_w_down", "w_pl", "w_pl_gate"]
    col_sharded = {"ffn1_w_gate", "ffn1_w_up", "w_in", "w_uq", "w_ukv", "w_branch_a", "w_branch_b", "ffn2_w_gate",
                   "ffn2_w_up", "w_pl"}
    small = ["ffn1_norm", "mix_norm", "q_a_norm", "kv_a_norm", "na_rpb", "ffn2_norm", "pl_norm", "final_norm"]
    order = ["ffn1_norm", "ffn1_w_gate", "ffn1_w_up", "ffn1_w_down", "mix_norm", "w_in", "q_a_norm", "w_uq",
             "kv_a_norm", "w_ukv", "na_rpb", "w_branch_a", "w_branch_b", "w_out", "ffn2_norm", "ffn2_w_gate",
             "ffn2_w_up", "ffn2_w_down", "pl_norm", "w_pl", "w_pl_gate", "final_norm"]
    env = dict(locals())
    W = {n: env[n] for n in order}
    Mo = {n: env["m_" + n] for n in order}
    Vo = {n: env["v_" + n] for n in order}

    xs = x[0]
    S, D = xs.shape
    tgt = loss_target[0]
    ps = p[0, 0]
    NAW = w_branch_a.shape[1]
    MLAW = w_branch_b.shape[1]
    NH, MH = NAW // HEAD_DIM, MLAW // HEAD_DIM
    QR, KVR = w_uq.shape[1], w_ukv.shape[1]
    F = ffn1_w_down.shape[1] * N_CHIPS
    cx, cy, cc = _place()
    c_idx = jnp.reshape(cc, (1,)).astype(jnp.int32)
    j_idx = jnp.reshape(2 * cx + cy, (1,)).astype(jnp.int32)

    me_chip = 2 * cx + cy
    groups = [["ffn1_w_gate"], ["ffn1_w_up"], ["ffn1_w_down"], ["w_in"],
              ["w_uq", "w_ukv", "w_branch_a", "w_branch_b", "w_out"],
              ["ffn2_w_gate", "ffn2_w_up", "ffn2_w_down"], ["w_pl", "w_pl_gate"]]
    started, tokens = [], []
    for g, members in enumerate(groups):
        shards = [_halves(W[n][0].astype(BF16)) for n in members]
        landings = [lax.empty((N_CHIPS,) + s.shape, BF16) for s in shards]
        sems, shards_thru, landings_thru, token = gather_start(shards, landings, tokens[-1:], str(g))
        started.append((sems, shards_thru, landings_thru))
        tokens.append(token)
    gathered = {}

    def arrive(n, after):
        g = [n in members for members in groups].index(True)
        sems, shards_thru, landings_thru = started[g]
        after = list(after) if isinstance(after, (list, tuple)) else [after]
        shards_out, landed = gather_wait(sems, shards_thru, landings_thru, after, str(g))
        for name, full, own in zip(groups[g], gather_forward(landed, str(g)), shards_out):
            gathered[name] = lax.dynamic_update_slice(full, own[None], (me_chip, 0, 0, 0))

    def stacked(n, after=None):
        if n not in gathered:
            arrive(n, after)
        g = gathered[n]
        return g.reshape(N_CHIPS, 2 * g.shape[2], g.shape[3])

    def plain(n, after=None):
        if n in col_sharded:
            st = stacked(n, after)
            return st.transpose(1, 0, 2).reshape(st.shape[1], N_CHIPS * st.shape[2])
        if n not in gathered:
            arrive(n, after)
        g = gathered[n]
        return g.reshape(N_CHIPS * 2 * g.shape[2], g.shape[3])

    n_na = 3 * NAW
    n_front = n_na + QR + KVR
    n_in = n_front + MLA_ROPE + 2 * D
    off_ql, off_kvl, off_kr = 2 * D, 2 * D + QR, 2 * D + QR + KVR
    kr_w = 2 * LANES
    rest_w = off_kr + kr_w
    rest_ranges = [(n_front + MLA_ROPE, n_in), (n_na, n_front), (n_front, n_front + MLA_ROPE)]

    def shard_cols(st, lo, hi):
        nb, parts = st.shape[2], []
        while lo < hi:
            j = lo // nb
            end = min(hi, (j + 1) * nb)
            parts.append(st[j][:, lo - j * nb:end - j * nb])
            lo = end
        return parts

    def w_in_shards(g_na, g_rest):
        pieces = [(0, n_na, g_na, 0)]
        o = 0
        for lo, hi in rest_ranges:
            pieces.append((lo, hi, g_rest, o))
            o += hi - lo
        nb, shards = n_in // N_CHIPS, []
        for j in range(N_CHIPS):
            parts = []
            for lo, hi, src, o in sorted(pieces):
                a, b = max(lo, j * nb), min(hi, (j + 1) * nb)
                if a < b:
                    parts.append(src[:, o + a - lo:o + b - lo])
            shards.append(jnp.concatenate(parts, axis=1))
        return jnp.stack(shards)

    pos = jnp.arange(S, dtype=F32)
    inv_freq = 1.0 / (ROPE_THETA ** (jnp.arange(0, MLA_ROPE, 2, dtype=F32) / MLA_ROPE))
    ang = pos[:, None] * inv_freq[None, :]
    zpad = jnp.zeros((S, LANES - MLA_ROPE), F32)
    cos_t = jnp.concatenate([jnp.cos(ang), jnp.cos(ang), zpad], axis=1)
    sin_t = jnp.concatenate([-jnp.sin(ang), jnp.sin(ang), zpad], axis=1)

    def ffn_fwd(h, norm_g, tag, pre, after=()):
        n = norm_fwd(h, norm_g, name=f"{tag}_norm")
        g = mm(n, stacked(pre + "_w_gate", [n, *after]), name=f"{tag}_gate", b_stack=True)
        u = mm(n, stacked(pre + "_w_up", g), name=f"{tag}_up", b_stack=True)
        a = swiglu_fwd(g, u, name=f"{tag}_act")
        h_out = mm(a, plain(pre + "_w_down", a), name=f"{tag}_down", res=h, alpha=0.5)
        return h_out, (n, g, u, a)

    def ffn_bwd(h, norm_g, saved, dh, dh_half, tag, pre, last, after=()):
        n, g, u, a = saved
        G[pre + "_w_down"] = mm(a, dh_half, name=f"{tag}_dw_down", ta=True, out_dtype=BF16)
        da = mm(dh_half, plain(pre + "_w_down"), name=f"{tag}_da", tb=True, after=after)
        dg, du = swiglu_bwd(g, u, da, name=f"{tag}_dact")
        G[pre + "_w_gate"] = mm(n, dg, name=f"{tag}_dw_gate", ta=True, out_dtype=BF16, out_stack=True)
        G[pre + "_w_up"] = mm(n, du, name=f"{tag}_dw_up", ta=True, out_dtype=BF16, out_stack=True)
        token = reduce_start([pre + "_w_down", pre + "_w_gate", pre + "_w_up"], tag)
        dn = mm(dg, stacked(pre + "_w_gate"), name=f"{tag}_dn_gate", tb=True, b_stack=True, after=[token])
        dn = mm(du, stacked(pre + "_w_up"), name=f"{tag}_dn_up", tb=True, b_stack=True, res=dn)
        return norm_bwd(h, norm_g, dn, name=f"{tag}_dnorm", res=dh, bf16_alpha=None if last else 1.0)

    bias = na_bias(na_rpb[0], after=tokens[-1:])
    h1, ffn1_saved = ffn_fwd(xs, ffn1_norm, "ffn1", "ffn1", after=[bias, tokens[-1]])
    u_mix = norm_fwd(h1, mix_norm, name="mix_norm")
    win_st = stacked("w_in", u_mix)
    w_na = jnp.concatenate(shard_cols(win_st, 0, n_na), axis=1)
    w_rest = jnp.concatenate([p_ for lo, hi in rest_ranges for p_ in shard_cols(win_st, lo, hi)]
                             + [jnp.zeros((D, kr_w - MLA_ROPE), BF16)], axis=1)
    z_na = mm(u_mix, w_na, name="mix_in_na", out_dtype=BF16)
    z = mm(u_mix, w_rest, name="mix_in_rest")
    o_a = na_fwd(z_na, bias, NH, S)
    c_q = norm_fwd((z, QR, off_ql // QR), q_a_norm, name="q_a_norm")
    c_kv = norm_fwd((z, KVR, off_kvl // KVR), kv_a_norm, name="kv_a_norm")
    wuq = plain("w_uq", c_kv).reshape(QR, MH, MLA_QK)
    wuq_n = wuq[:, :, :MLA_NOPE].reshape(QR, MH * MLA_NOPE)
    wuq_r = jnp.pad(wuq[:, :, MLA_NOPE:], ((0, 0), (0, 0), (0, LANES - MLA_ROPE))).reshape(QR, MH * LANES)
    wukv = plain("w_ukv").reshape(KVR, MH, 2, HEAD_DIM)
    wuk = wukv[:, :, 0].reshape(KVR, MH * HEAD_DIM)
    wuv = wukv[:, :, 1].reshape(KVR, MH * HEAD_DIM)
    q_n = mm(c_q, wuq_n, name="mla_q_nope", out_dtype=BF16)
    q_r = rope(mm(c_q, wuq_r, name="mla_q_rope"), cos_t, sin_t, name="rope_q", out_dtype=BF16)
    k_n = mm(c_kv, wuk, name="mla_k_nope", out_dtype=BF16)
    v_m = mm(c_kv, wuv, name="mla_v", out_dtype=BF16)
    k_r = rope((z, LANES, off_kr // LANES), cos_t, sin_t, name="rope_k", out_dtype=BF16)
    o_b, lse = mla_fwd(q_n, q_r, k_n, v_m, k_r, MH, S)
    y_a = mm(o_a, stacked("w_branch_a"), name="branch_a", b_stack=True)
    y_b = mm(o_b, stacked("w_branch_b"), name="branch_b", b_stack=True)
    z_ga, z_gb = (z, D, 0), (z, D, 1)
    merged = rowwise(lambda ga, gb, ya, yb: _sig(ga) * ya + _sig(gb) * yb, [z_ga, z_gb, y_a, y_b], [], [(D, BF16)],
                     tm=256, name="merge")[0]
    h2 = mm(merged, plain("w_out"), name="mix_out", res=h1)
    h3, ffn2_saved = ffn_fwd(h2, ffn2_norm, "ffn2", "ffn2")
    n4 = norm_fwd(h3, pl_norm, name="pl_norm")
    pg_pre = mm(n4, plain("w_pl_gate", n4), name="pl_gate")
    pe = mm(ps, stacked("w_pl"), name="pl_embed", b_stack=True)

    def tail(h3b, pgb, peb, tb_, fg):
        pg = _sig(pgb)
        h4 = h3b + pg * peb
        r = _rstd(h4)
        xh = h4 * r
        err = xh * fg - tb_
        loss_rows = jnp.mean(err * err, axis=-1, keepdims=True)
        dy = err * (1.0 / D)
        dxh = dy * fg
        dh4 = r * (dxh - xh * jnp.mean(dxh * xh, axis=-1, keepdims=True))
        loss_part = jnp.broadcast_to(0.5 * jnp.sum(loss_rows, axis=0, keepdims=True), (1, LANES))
        return (dh4, dh4 * peb * pg * (1.0 - pg), dh4 * pg, loss_part, jnp.sum(dy * xh, axis=0, keepdims=True))

    dh4, dpg_pre, dpe, loss_part, g_final = rowwise(
        tail, [h3, pg_pre, pe, tgt], [final_norm.reshape(1, D)], [(D, F32), (D, BF16), (D, BF16)],
        accs=[(1, LANES), (1, D)], tm=128, name="loss_tail")
    loss = lax.psum(loss_part[0, 0], ("x", "y", "c"))

    G = {}
    pending = []

    def four(g):
        if g.ndim == 2:
            return g.reshape(N_CHIPS, 2, g.shape[0] // (2 * N_CHIPS), g.shape[1])
        return g.reshape(N_CHIPS, 2, g.shape[1] // 2, g.shape[2])

    def reduce_start(names, tag):
        g4 = [four(G[n]) for n in names]
        sums = [pair_sum(a, b, c_idx, n) for n, a, b in zip(names, g4, pair_exchange(g4, tag))]
        lands = [lax.empty((N_CHIPS - 1,) + s_.shape[1:], BF16) for s_ in sums]
        sems, sums, lands, token = scatter_start(sums, lands, tag)
        pending.append((names, tag, sems, sums, lands))
        return token

    def reduce_finish(entry, after):
        names, tag, sems, sums, lands = entry
        sums, got = scatter_wait(sems, sums, lands, after, tag)
        halves = [chip_sum(a, b, j_idx, n) for n, a, b in zip(names, sums, got)]
        done = []
        for n, own, sib in zip(names, halves, half_exchange(halves, tag)):
            shp = W[n].shape
            two_d = lambda a_: a_.reshape(shp[1], shp[2])
            out = adamw_shard(two_d(W[n]), own, sib, two_d(Mo[n]), two_d(Vo[n]), c_idx, name="adamw_" + n)
            grads[n], delta[n], new_m[n], new_v[n] = [o.reshape(shp) for o in out]
            done.append(out[0])
        return done

    G["w_pl"] = mm(ps, dpe, name="pl_dw_embed", ta=True, out_dtype=BF16, out_stack=True)
    G["w_pl_gate"] = mm(n4, dpg_pre, name="pl_dw_gate", ta=True, out_dtype=BF16)
    dn4 = mm(dpg_pre, plain("w_pl_gate"), name="pl_dn", tb=True)
    dh3, dh3_half, g_pl = norm_bwd(h3, pl_norm, dn4, name="pl_dnorm", res=dh4, bf16_alpha=0.5)
    token = reduce_start(["w_pl", "w_pl_gate"], "pl")
    dh2, dh2_b, g_ffn2 = ffn_bwd(h2, ffn2_norm, ffn2_saved, dh3, dh3_half, "ffn2", "ffn2", last=False, after=[token])

    G["w_out"] = mm(merged, dh2_b, name="mix_dw_out", ta=True, out_dtype=BF16)
    dmerged = mm(dh2_b, plain("w_out"), name="mix_dmerged", tb=True)

    def merge_bwd(ga, gb, ya, yb, dm):
        sa, sb = _sig(ga), _sig(gb)
        dgates = jnp.concatenate([dm * ya * sa * (1.0 - sa), dm * yb * sb * (1.0 - sb)], axis=1)
        return dm * sa, dm * sb, dgates

    dy_a, dy_b, dz_rest = rowwise(merge_bwd, [z_ga, z_gb, y_a, y_b, dmerged], [],
                                  [(D, BF16), (D, BF16), (2 * D, BF16, 0)], tm=256, name="merge_bwd",
                                  into=(None, rest_w))
    G["w_branch_a"] = mm(o_a, dy_a, name="branch_a_dw", ta=True, out_dtype=BF16, out_stack=True)
    G["w_branch_b"] = mm(o_b, dy_b, name="branch_b_dw", ta=True, out_dtype=BF16, out_stack=True)
    do_a = mm(dy_a, stacked("w_branch_a"), name="branch_a_dx", tb=True, b_stack=True)
    do_b = mm(dy_b, stacked("w_branch_b"), name="branch_b_dx", tb=True, b_stack=True)
    dq_na, dk_na, dv_na, dbias = na_bwd(z_na, bias, do_a, NH, S)
    g_rpb = na_rpb_grad(dbias)
    dq_n, dq_rr, dk_n, dv_m, dk_rr = mla_bwd(q_n, q_r, k_n, v_m, k_r, lse, do_b, MH, S)
    dq_r = rope(dq_rr, cos_t, -sin_t, name="rope_q_bwd", out_dtype=BF16)
    dz_rest = rope(dk_rr, cos_t, -sin_t, name="rope_k_bwd", out_dtype=BF16, into=(dz_rest, rest_w),
                   cb=off_kr // kr_w, zero_cols=kr_w - LANES)
    gw_uq_n = mm(c_q, dq_n, name="mla_dw_q_nope", ta=True, out_dtype=BF16)
    gw_uq_r = mm(c_q, dq_r, name="mla_dw_q_rope", ta=True, out_dtype=BF16)
    dc_q = mm(dq_n, wuq_n, name="mla_dcq_nope", tb=True)
    dc_q = mm(dq_r, wuq_r, name="mla_dcq_rope", tb=True, res=dc_q)
    gw_uk = mm(c_kv, dk_n, name="mla_dw_k", ta=True, out_dtype=BF16)
    gw_uv = mm(c_kv, dv_m, name="mla_dw_v", ta=True, out_dtype=BF16)
    dc_kv = mm(dk_n, wuk, name="mla_dckv_k", tb=True)
    dc_kv = mm(dv_m, wuv, name="mla_dckv_v", tb=True, res=dc_kv)
    dz_rest, g_qa = norm_bwd((z, QR, off_ql // QR), q_a_norm, dc_q, name="q_a_dnorm", want_f32=False, bf16_alpha=1.0,
                             into=(dz_rest, rest_w), cb=off_ql // QR)
    dz_rest, g_kva = norm_bwd((z, KVR, off_kvl // KVR), kv_a_norm, dc_kv, name="kv_a_dnorm", want_f32=False,
                              bf16_alpha=1.0, into=(dz_rest, rest_w), cb=off_kvl // KVR)
    dz_na = jnp.concatenate([dq_na, dk_na, dv_na], axis=1)
    gw_na = mm(u_mix, dz_na, name="mix_dw_in_na", ta=True, out_dtype=BF16)
    gw_rest = mm(u_mix, dz_rest, name="mix_dw_in_rest", ta=True, out_dtype=BF16)

    def to_stack(g2d):
        k, n = g2d.shape
        return g2d.reshape(k, N_CHIPS, n // N_CHIPS).transpose(1, 0, 2)

    gw_uq = jnp.concatenate([gw_uq_n.reshape(QR, MH, MLA_NOPE), gw_uq_r.reshape(QR, MH, LANES)[:, :, :MLA_ROPE]],
                            axis=2).reshape(QR, MH * MLA_QK)
    G["w_uq"] = to_stack(gw_uq)
    gw_ukv = jnp.stack([gw_uk.reshape(KVR, MH, HEAD_DIM), gw_uv.reshape(KVR, MH, HEAD_DIM)], axis=2)
    G["w_ukv"] = to_stack(gw_ukv.reshape(KVR, MH * 2 * HEAD_DIM))
    token_mix = reduce_start(["w_out", "w_branch_a", "w_branch_b", "w_uq", "w_ukv"], "mix")
    G["w_in"] = w_in_shards(gw_na, gw_rest)
    token_win = reduce_start(["w_in"], "win")
    du_mix = mm(dz_na, w_na, name="mix_du_na", tb=True, after=[token_mix, token_win])
    du_mix = mm(dz_rest, w_rest, name="mix_du_rest", tb=True, res=du_mix)
    dh1, dh1_half, g_mix = norm_bwd(h1, mix_norm, du_mix, name="mix_dnorm", res=dh2, bf16_alpha=0.5)
    grad_x, g_ffn1 = ffn_bwd(xs, ffn1_norm, ffn1_saved, dh1, dh1_half, "ffn1", "ffn1", last=True)

    small_g = {"ffn1_norm": g_ffn1, "mix_norm": g_mix, "q_a_norm": g_qa, "kv_a_norm": g_kva, "na_rpb": g_rpb,
               "ffn2_norm": g_ffn2, "pl_norm": g_pl, "final_norm": g_final}
    sizes = [int(np.prod(W[n].shape)) for n in small]
    total = sum(sizes)
    padded = -(-total // (8 * LANES)) * (8 * LANES)

    def pack(parts):
        flat = jnp.concatenate([jnp.reshape(parts[n], (-1,)).astype(F32) for n in small]
                               + [jnp.zeros((padded - total,), F32)])
        return flat.reshape(padded // LANES, LANES)

    def unpack(a):
        flat, out, o = a.reshape(-1), {}, 0
        for n, sz in zip(small, sizes):
            out[n] = flat[o:o + sz].reshape(W[n].shape)
            o += sz
        return out

    g_small = sum_devices(gather_small(pack(small_g)))
    d_small, m_small, v_small = adamw(pack(W), g_small, pack(Mo), pack(Vo), name="adamw_small")
    grads = unpack(g_small)
    delta, new_m, new_v = unpack(d_small), unpack(m_small), unpack(v_small)

    after = [grad_x]
    for entry in pending:
        after = reduce_finish(entry, after)

    return (loss, grad_x[None], *[grads[n] for n in order], *[delta[n] for n in order],
            *[new_m[n] for n in order], *[new_v[n] for n in order])
```

```python
import functools

import numpy as np
import jax
import jax.numpy as jnp
from jax import lax
from jax.experimental import pallas as pl
from jax.experimental.pallas import tpu as pltpu

F32 = jnp.float32
BF16 = jnp.bfloat16

VMEM_LIMIT_V7X = 56 * 1024 * 1024
VMEM_BUDGET_V7X = 40 * 1024 * 1024
LANES = 128

GRID_W = 64
NA_WIN_ROWS = 8
NA_WIN_COLS = 16
HEAD_DIM = 128
MLA_NOPE = 128
MLA_ROPE = 64
MLA_QK = MLA_NOPE + MLA_ROPE
ROPE_THETA = 10000.0
NORM_EPS = 1e-6
NEG_INF = -1e30
N_CHIPS = 4

ADAM_LR = 0.001
ADAM_B1 = 0.9
ADAM_B2 = 0.999
ADAM_EPS = 1e-08
ADAM_WD = 0.01
ADAM_STEP = 10

MESH = pl.DeviceIdType.MESH
ANY = pl.BlockSpec(memory_space=pl.ANY)


def _params(sem=None):
    return pltpu.CompilerParams(dimension_semantics=sem, vmem_limit_bytes=VMEM_LIMIT_V7X)


def _pick(n, target, align):
    best = None
    t = align
    while t <= min(n, target):
        if n % t == 0:
            best = t
        t += align
    return n if best is None else best


def mm(a, b, *, name, ta=False, tb=False, out_dtype=F32, res=None, alpha=1.0, b_stack=False, out_stack=False,
       exact=False, after=()):
    K, M = (a.shape if ta else a.shape[::-1])
    nst = kb = nb = None
    if b_stack:
        nst = b.shape[0]
        if tb:
            N, kb = b.shape[1], b.shape[2]
            Kb = nst * kb
        else:
            Kb, nb = b.shape[1], b.shape[2]
            N = nst * nb
    else:
        N, Kb = (b.shape if tb else b.shape[::-1])
    assert K == Kb, (a.shape, b.shape, ta, tb)
    if out_stack:
        assert N % N_CHIPS == 0
    n_unit = N // N_CHIPS if out_stack else (nb if nb is not None else N)
    tn = _pick(n_unit, 512, LANES) if n_unit % 512 == 0 or n_unit <= 512 else _pick(n_unit, 1536, LANES)
    if ta and n_unit == N and 4 * K * N * jnp.dtype(b.dtype).itemsize <= VMEM_BUDGET_V7X:
        tn = N
    m_align = LANES if ta else 16
    tm = _pick(M, 1024, m_align)
    isz = lambda t: jnp.dtype(t.dtype).itemsize
    osz = jnp.dtype(out_dtype).itemsize

    def vmem(tm_, tn_):
        return (2 * tm_ * K * isz(a) + 2 * K * tn_ * isz(b) + 2 * tm_ * tn_ * osz + tm_ * tn_ * 4
                + (tm_ * K * 2 if ta else 0) + (2 * tm_ * tn_ * isz(res) if res is not None else 0))

    while vmem(tm, tn) > VMEM_BUDGET_V7X and tm % 2 == 0 and (tm // 2) % m_align == 0:
        tm //= 2
    while vmem(tm, tn) > VMEM_BUDGET_V7X and tn % 2 == 0 and (tn // 2) % LANES == 0 and n_unit % (tn // 2) == 0:
        tn //= 2
    assert vmem(tm, tn) <= VMEM_BUDGET_V7X, (name, tm, tn, K)

    a_spec = pl.BlockSpec((K, tm), lambda i, j: (0, i)) if ta else pl.BlockSpec((tm, K), lambda i, j: (i, 0))
    if b_stack and not tb:
        q = nb // tn
        b_spec = pl.BlockSpec((None, K, tn), lambda i, j: (j // q, 0, j % q))
    elif b_stack and tb:
        b_spec = pl.BlockSpec((nst, tn, kb), lambda i, j: (0, j, 0))
    elif tb:
        b_spec = pl.BlockSpec((tn, K), lambda i, j: (j, 0))
    else:
        b_spec = pl.BlockSpec((K, tn), lambda i, j: (0, j))
    if out_stack:
        qo = (N // N_CHIPS) // tn
        o_spec = pl.BlockSpec((None, tm, tn), lambda i, j: (j // qo, i, j % qo))
        o_shape = jax.ShapeDtypeStruct((N_CHIPS, M, N // N_CHIPS), out_dtype)
    else:
        o_spec = pl.BlockSpec((tm, tn), lambda i, j: (i, j))
        o_shape = jax.ShapeDtypeStruct((M, N), out_dtype)
    has_res = res is not None
    nn = (((1,), (0,)), ((), ()))
    nt = (((1,), (1,)), ((), ()))

    def body(*refs):
        a_ref, b_ref = refs[:2]
        r_ref = refs[2] if has_res else None
        o_ref = refs[3 + len(after)] if has_res else refs[2 + len(after)]
        if ta:
            at_ref = refs[-1]

            @pl.when(pl.program_id(1) == 0)
            def _():
                at_ref[...] = a_ref[...].astype(BF16).T

            lhs = at_ref[...]
        elif exact:
            lhs = a_ref[...]
        else:
            lhs = a_ref[...].astype(BF16)
        if exact:
            total = lax.dot_general(lhs, b_ref[...], nt if tb else nn, preferred_element_type=F32,
                                    precision=lax.Precision.HIGHEST)
        elif b_stack and tb:
            total = None
            for s in range(nst):
                part = lax.dot_general(lhs[:, s * kb:(s + 1) * kb], b_ref[s].astype(BF16), nt,
                                       preferred_element_type=F32)
                total = part if total is None else total + part
        else:
            total = lax.dot_general(lhs, b_ref[...].astype(BF16), nt if tb else nn, preferred_element_type=F32)
        if alpha != 1.0:
            total = total * alpha
        if has_res:
            total = total + r_ref[...].astype(F32)
        o_ref[...] = total.astype(out_dtype)

    in_specs = [a_spec, b_spec]
    args = [a, b]
    if has_res:
        in_specs.append(pl.BlockSpec((tm, tn), lambda i, j: (i, j)))
        args.append(res)
    in_specs += [ANY] * len(after)
    args += list(after)
    return pl.pallas_call(
        body, name=name, grid=(M // tm, N // tn), in_specs=in_specs, out_specs=o_spec, out_shape=o_shape,
        scratch_shapes=[pltpu.VMEM((tm, K), BF16)] if ta else [],
        compiler_params=_params(("parallel", "arbitrary")),
    )(*args)


def rowwise(fn, rows, consts, outs, accs=(), *, tm, name, tn=None, into=None):
    rows = [r if isinstance(r, tuple) else (r, r.shape[1], 0) for r in rows]
    S = rows[0][0].shape[0]
    tm = _pick(S, tm, 16)
    nrow, ncon, nout = len(rows), len(consts), len(outs)
    outs = [o if len(o) == 3 else (o[0], o[1], None) for o in outs]
    if tn is None:
        grid = (S // tm,)
        in_specs = [pl.BlockSpec((tm, w), functools.partial(lambda i, cb: (i, cb), cb=cb)) for _, w, cb in rows]
        in_specs += [pl.BlockSpec(c.shape, lambda i: (0, 0)) for c in consts]
        out_specs = [pl.BlockSpec((tm, n), functools.partial(lambda i, cb: (i, cb), cb=cb or 0)) for n, _, cb in outs]
        out_specs += [pl.BlockSpec(s, lambda i: (0, 0)) for s in accs]
        sem = ("arbitrary",)
    else:
        assert not accs
        N = rows[0][1]
        grid = (S // tm, N // tn)
        in_specs = [pl.BlockSpec((tm, tn), lambda i, j: (i, j)) for _ in rows]
        in_specs += [pl.BlockSpec(c.shape, lambda i, j: (0, 0)) for c in consts]
        out_specs = [pl.BlockSpec((tm, tn), lambda i, j: (i, j)) for _ in outs]
        sem = ("parallel", "parallel")
    out_shape = [jax.ShapeDtypeStruct((S, n if cb is None else into[1]), dt) for n, dt, cb in outs]
    out_shape += [jax.ShapeDtypeStruct(s, F32) for s in accs]
    extra, aliases = [], {}
    if into is not None and into[0] is not None:
        extra = [into[0]]
        aliases = {nrow + ncon: [cb is not None for _, _, cb in outs].index(True)}

    def body(*refs):
        vals = fn(*[r[...] for r in refs[:nrow + ncon]])
        if not isinstance(vals, (tuple, list)):
            vals = (vals,)
        o_refs = refs[nrow + ncon + len(extra):]
        for o_ref, v in zip(o_refs[:nout], vals[:nout]):
            o_ref[...] = v.astype(o_ref.dtype)
        if accs:
            first = pl.program_id(0) == 0

            def accumulate(a_ref, v):
                @pl.when(first)
                def _():
                    a_ref[...] = v

                @pl.when(jnp.logical_not(first))
                def _():
                    a_ref[...] += v

            for a_ref, v in zip(o_refs[nout:], vals[nout:]):
                accumulate(a_ref, v.astype(F32))

    return pl.pallas_call(
        body, name=name, grid=grid, in_specs=in_specs + [ANY] * len(extra), out_specs=out_specs, out_shape=out_shape,
        input_output_aliases=aliases, compiler_params=_params(sem),
    )(*[r[0] for r in rows], *consts, *extra)


def _rstd(x):
    return lax.rsqrt(jnp.mean(x * x, axis=-1, keepdims=True) + NORM_EPS)


def norm_fwd(x, g, *, name, tm=256):
    w = x[1] if isinstance(x, tuple) else x.shape[1]

    def fn(xb, gb):
        return (xb * _rstd(xb)) * gb

    return rowwise(fn, [x], [g], [(w, BF16)], tm=tm, name=name)[0]


def norm_bwd(x, g, dn, *, name, res=None, want_f32=True, bf16_alpha=None, tm=256, into=None, cb=None):
    w = x[1] if isinstance(x, tuple) else x.shape[1]
    has_res = res is not None

    def fn(*blocks):
        if has_res:
            xb, dnb, rb, gb = blocks
        else:
            xb, dnb, gb = blocks
        r = _rstd(xb)
        xh = xb * r
        dxh = dnb * gb
        dx = r * (dxh - xh * jnp.mean(dxh * xh, axis=-1, keepdims=True))
        if has_res:
            dx = dx + rb
        out = []
        if want_f32:
            out.append(dx)
        if bf16_alpha is not None:
            out.append(dx * bf16_alpha if bf16_alpha != 1.0 else dx)
        out.append(jnp.sum(dnb * xh, axis=0, keepdims=True))
        return tuple(out)

    outs = ([(w, F32)] if want_f32 else []) + ([(w, BF16, cb)] if bf16_alpha is not None else [])
    rows = [x, dn] + ([res] if has_res else [])
    return rowwise(fn, rows, [g], outs, accs=[(1, w)], tm=tm, name=name, into=into)


def _sig(x):
    return jax.nn.sigmoid(x)


def swiglu_fwd(g, u, *, name):
    return rowwise(lambda gb, ub: gb * _sig(gb) * ub, [g, u], [], [(g.shape[1], BF16)], tm=256, name=name,
                   tn=_pick(g.shape[1], 1536, LANES))[0]


def swiglu_bwd(g, u, da, *, name):
    def fn(gb, ub, dab):
        s = _sig(gb)
        return dab * ub * (s + gb * s * (1.0 - s)), dab * (gb * s)

    n = g.shape[1]
    return rowwise(fn, [g, u, da], [], [(n, BF16), (n, BF16)], tm=256, name=name, tn=_pick(n, 1536, LANES))


def rope(x, cos, sin_signed, *, name, out_dtype, into=None, cb=None, zero_cols=0):
    w = x[1] if isinstance(x, tuple) else x.shape[1]
    half = MLA_ROPE // 2

    def fn(xb, cb, sb):
        lane = lax.broadcasted_iota(jnp.int32, cb.shape, 1)
        outs = []
        for hb in range(w // LANES):
            blk = xb[:, hb * LANES:(hb + 1) * LANES]
            partner = jnp.where(lane < half, pltpu.roll(blk, LANES - half, 1), pltpu.roll(blk, half, 1))
            outs.append(blk * cb + partner * sb)
        if zero_cols:
            outs.append(jnp.zeros((xb.shape[0], zero_cols), xb.dtype))
        return outs[0] if len(outs) == 1 else jnp.concatenate(outs, axis=1)

    return rowwise(fn, [x, cos, sin_signed], [], [(w + zero_cols, out_dtype, cb)], tm=256, name=name, into=into)[0]


def _na_tables():
    cols = np.arange(GRID_W)
    kw = NA_WIN_COLS
    dc = np.clip(cols[None, :] - cols[:, None], -(kw - 1), kw - 1) + (kw - 1)
    onehot = np.zeros((LANES, GRID_W * GRID_W), np.float32)
    onehot[dc.reshape(-1), np.arange(GRID_W * GRID_W)] = 1.0
    col_start = np.clip(cols - kw // 2, 0, GRID_W - kw)
    mask = (cols[None, :] >= col_start[:, None]) & (cols[None, :] < col_start[:, None] + kw)
    return onehot, np.where(mask, 0.0, NEG_INF).astype(np.float32)


def na_bias(rpb, after=()):
    H = rpb.shape[0]
    nr, kh = 2 * NA_WIN_ROWS - 1, NA_WIN_ROWS
    onehot, maskb = _na_tables()
    rp = jnp.pad(rpb.reshape(H * nr, 2 * NA_WIN_COLS - 1), ((0, 0), (0, LANES - (2 * NA_WIN_COLS - 1))))
    t1 = mm(rp, jnp.asarray(onehot), name="na_bias_table", exact=True, after=after).reshape(H, nr, GRID_W, GRID_W)
    t1 = t1 + jnp.asarray(maskb)[None, None]
    per_t = [jnp.stack([t1[:, i - t + kh - 1] for i in range(kh)], axis=2) for t in range(kh)]
    return jnp.stack(per_t, axis=1).reshape(H, kh, GRID_W, kh * GRID_W)


def na_rpb_grad(db):
    H = db.shape[0]
    nr, kh = 2 * NA_WIN_ROWS - 1, NA_WIN_ROWS
    onehot, _ = _na_tables()
    db = db.reshape(H, kh, GRID_W, kh, GRID_W)
    per_dr = []
    for dri in range(nr):
        terms = [db[:, t, :, dri - (kh - 1) + t, :] for t in range(kh) if 0 <= dri - (kh - 1) + t < kh]
        per_dr.append(functools.reduce(jnp.add, terms))
    dt1 = jnp.stack(per_dr, axis=1).reshape(H * nr, GRID_W * GRID_W)
    g = mm(dt1, jnp.asarray(onehot), name="na_rpb_grad", tb=True, exact=True)
    return g[:, :2 * NA_WIN_COLS - 1].reshape(H, nr, 2 * NA_WIN_COLS - 1)


def _na_first_row(r, rows):
    return jnp.clip(r - NA_WIN_ROWS // 2, 0, rows - NA_WIN_ROWS)


NA_ROWS_PER_STEP = 8


def _na_probs(q, k_ref, b_ref, r, rows):
    first = _na_first_row(r, rows)
    start = pl.multiple_of(first * GRID_W, GRID_W)
    k = k_ref[pl.ds(start, NA_WIN_ROWS * GRID_W), :]
    s = lax.dot_general(q, k, (((1,), (1,)), ((), ())), preferred_element_type=F32)
    s = s * (HEAD_DIM ** -0.5) + b_ref[r - first]
    m = jnp.max(s, axis=-1, keepdims=True)
    e = jnp.exp(s - m)
    return k, e / jnp.sum(e, axis=-1, keepdims=True), start, r - first


def na_fwd(z, bias, H, S):
    rows = S // GRID_W
    nkeys = NA_WIN_ROWS * GRID_W
    rb = _pick(rows, NA_ROWS_PER_STEP, 1)

    def body(q_ref, k_ref, v_ref, b_ref, o_ref):
        for j in range(rb):
            r = pl.program_id(1) * rb + j
            rows_j = pl.ds(j * GRID_W, GRID_W)
            _, p, start, _ = _na_probs(q_ref[rows_j, :], k_ref, b_ref, r, rows)
            v = v_ref[pl.ds(start, nkeys), :]
            o_ref[rows_j, :] = jnp.dot(p.astype(BF16), v, preferred_element_type=F32).astype(o_ref.dtype)

    return pl.pallas_call(
        body, name="na_fwd", grid=(H, rows // rb),
        in_specs=[pl.BlockSpec((rb * GRID_W, HEAD_DIM), lambda h, i: (i, h)),
                  pl.BlockSpec((S, HEAD_DIM), lambda h, i: (0, H + h)),
                  pl.BlockSpec((S, HEAD_DIM), lambda h, i: (0, 2 * H + h)),
                  pl.BlockSpec((None, NA_WIN_ROWS, GRID_W, nkeys), lambda h, i: (h, 0, 0, 0))],
        out_specs=pl.BlockSpec((rb * GRID_W, HEAD_DIM), lambda h, i: (i, h)),
        out_shape=jax.ShapeDtypeStruct((S, H * HEAD_DIM), BF16),
        compiler_params=_params(("parallel", "arbitrary")),
    )(z, z, z, bias)


def na_bwd(z, bias, do, H, S):
    rows = S // GRID_W
    nkeys = NA_WIN_ROWS * GRID_W
    rb = _pick(rows, NA_ROWS_PER_STEP, 1)
    tn_dims = (((0,), (0,)), ((), ()))

    def body(q_ref, k_ref, v_ref, b_ref, do_ref, dq_ref, dk_ref, dv_ref, db_ref, dk_acc, dv_acc):
        i = pl.program_id(1)

        @pl.when(i == 0)
        def _():
            dk_acc[...] = jnp.zeros_like(dk_acc)
            dv_acc[...] = jnp.zeros_like(dv_acc)
            db_ref[...] = jnp.zeros_like(db_ref)

        for j in range(rb):
            rows_j = pl.ds(j * GRID_W, GRID_W)
            q = q_ref[rows_j, :]
            k, p, start, t = _na_probs(q, k_ref, b_ref, i * rb + j, rows)
            keys = pl.ds(start, nkeys)
            dob = do_ref[rows_j, :].astype(BF16)
            dp = lax.dot_general(dob, v_ref[keys, :], (((1,), (1,)), ((), ())), preferred_element_type=F32)
            ds = p * (dp - jnp.sum(dp * p, axis=-1, keepdims=True))
            dsb = (ds * (HEAD_DIM ** -0.5)).astype(BF16)
            dq_ref[rows_j, :] = jnp.dot(dsb, k, preferred_element_type=F32).astype(dq_ref.dtype)
            dk_acc[keys, :] += lax.dot_general(dsb, q, tn_dims, preferred_element_type=F32)
            dv_acc[keys, :] += lax.dot_general(p.astype(BF16), dob, tn_dims, preferred_element_type=F32)
            db_ref[t] += ds

        @pl.when(i == rows // rb - 1)
        def _():
            dk_ref[...] = dk_acc[...].astype(dk_ref.dtype)
            dv_ref[...] = dv_acc[...].astype(dv_ref.dtype)

    W = H * HEAD_DIM
    qspec = pl.BlockSpec((rb * GRID_W, HEAD_DIM), lambda h, i: (i, h))
    bspec = pl.BlockSpec((None, NA_WIN_ROWS, GRID_W, nkeys), lambda h, i: (h, 0, 0, 0))
    return pl.pallas_call(
        body, name="na_bwd", grid=(H, rows // rb),
        in_specs=[qspec, pl.BlockSpec((S, HEAD_DIM), lambda h, i: (0, H + h)),
                  pl.BlockSpec((S, HEAD_DIM), lambda h, i: (0, 2 * H + h)), bspec, qspec],
        out_specs=[qspec, pl.BlockSpec((S, HEAD_DIM), lambda h, i: (0, h)),
                   pl.BlockSpec((S, HEAD_DIM), lambda h, i: (0, h)), bspec],
        out_shape=[jax.ShapeDtypeStruct((S, W), BF16)] * 3 + [jax.ShapeDtypeStruct((H, NA_WIN_ROWS, GRID_W, nkeys), F32)],
        scratch_shapes=[pltpu.VMEM((S, HEAD_DIM), F32)] * 2,
        compiler_params=_params(("arbitrary", "arbitrary")),
    )(z, z, z, bias, do)


def _mla_keys(kn_ref, kr_ref, kcat):
    @pl.when(pl.program_id(1) == 0)
    def _():
        kcat[:, :HEAD_DIM] = kn_ref[...]
        kcat[:, HEAD_DIM:] = kr_ref[...]


MLA_LOG2_SCALE = (MLA_QK ** -0.5) * 1.4426950408889634


def _mla_scores(qn_ref, qr_ref, kcat):
    qcat = jnp.concatenate([qn_ref[...], qr_ref[...]], axis=1)
    return qcat, lax.dot_general(qcat, kcat[...], (((1,), (1,)), ((), ())), preferred_element_type=F32)


def mla_fwd(qn, qr, kn, v, kr, H, S):
    tq = _pick(S, 256, 16)

    def body(qn_ref, qr_ref, kn_ref, v_ref, kr_ref, o_ref, lse_ref, kcat):
        _mla_keys(kn_ref, kr_ref, kcat)
        _, s = _mla_scores(qn_ref, qr_ref, kcat)
        m = jnp.max(s, axis=-1, keepdims=True)
        e = jnp.exp2((s - m) * MLA_LOG2_SCALE)
        l = jnp.sum(e, axis=-1, keepdims=True)
        o = jnp.dot(e.astype(BF16), v_ref[...], preferred_element_type=F32)
        o_ref[...] = (o / l).astype(o_ref.dtype)
        lse_ref[...] = jnp.broadcast_to(m * MLA_LOG2_SCALE + jnp.log2(l), lse_ref.shape)

    qspec = pl.BlockSpec((tq, HEAD_DIM), lambda h, i: (i, h))
    kspec = pl.BlockSpec((S, HEAD_DIM), lambda h, i: (0, h))
    return pl.pallas_call(
        body, name="mla_fwd", grid=(H, S // tq),
        in_specs=[qspec, qspec, kspec, kspec, pl.BlockSpec((S, LANES), lambda h, i: (0, 0))],
        out_specs=[qspec, qspec],
        out_shape=[jax.ShapeDtypeStruct((S, H * HEAD_DIM), BF16), jax.ShapeDtypeStruct((S, H * LANES), F32)],
        scratch_shapes=[pltpu.VMEM((S, 2 * HEAD_DIM), BF16)],
        compiler_params=_params(("parallel", "arbitrary")),
    )(qn, qr, kn, v, kr)


def mla_bwd(qn, qr, kn, v, kr, lse, do, H, S):
    tq = _pick(S, 256, 16)
    nt = (((1,), (1,)), ((), ()))
    tn_dims = (((0,), (0,)), ((), ()))

    def body(qn_ref, qr_ref, kn_ref, v_ref, kr_ref, lse_ref, do_ref, dqn_ref, dqr_ref, dkn_ref, dv_ref, dkr_ref, kcat):
        h, i = pl.program_id(0), pl.program_id(1)
        _mla_keys(kn_ref, kr_ref, kcat)
        qcat, s = _mla_scores(qn_ref, qr_ref, kcat)
        p = jnp.exp2(s * MLA_LOG2_SCALE - lse_ref[:, 0:1])
        dob = do_ref[...].astype(BF16)
        dp = lax.dot_general(dob, v_ref[...], nt, preferred_element_type=F32)
        ds = p * (dp - jnp.sum(dp * p, axis=-1, keepdims=True))
        dsb = (ds * (MLA_QK ** -0.5)).astype(BF16)
        dq = jnp.dot(dsb, kcat[...], preferred_element_type=F32)
        dqn_ref[...] = dq[:, :HEAD_DIM].astype(dqn_ref.dtype)
        dqr_ref[...] = dq[:, HEAD_DIM:].astype(dqr_ref.dtype)

        @pl.when(i == 0)
        def _():
            dkn_ref[...] = jnp.zeros_like(dkn_ref)
            dv_ref[...] = jnp.zeros_like(dv_ref)

        @pl.when(jnp.logical_and(i == 0, h == 0))
        def _():
            dkr_ref[...] = jnp.zeros_like(dkr_ref)

        dk = lax.dot_general(dsb, qcat, tn_dims, preferred_element_type=F32)
        dkn_ref[...] += dk[:, :HEAD_DIM]
        dkr_ref[...] += dk[:, HEAD_DIM:]
        dv_ref[...] += lax.dot_general(p.astype(BF16), dob, tn_dims, preferred_element_type=F32)

    qspec = pl.BlockSpec((tq, HEAD_DIM), lambda h, i: (i, h))
    kspec = pl.BlockSpec((S, HEAD_DIM), lambda h, i: (0, h))
    rspec = pl.BlockSpec((S, LANES), lambda h, i: (0, 0))
    W = H * HEAD_DIM
    return pl.pallas_call(
        body, name="mla_bwd", grid=(H, S // tq),
        in_specs=[qspec, qspec, kspec, kspec, rspec, qspec, qspec],
        out_specs=[qspec, qspec, kspec, kspec, rspec],
        out_shape=[jax.ShapeDtypeStruct((S, W), BF16), jax.ShapeDtypeStruct((S, W), F32),
                   jax.ShapeDtypeStruct((S, W), F32), jax.ShapeDtypeStruct((S, W), F32),
                   jax.ShapeDtypeStruct((S, LANES), F32)],
        scratch_shapes=[pltpu.VMEM((S, 2 * HEAD_DIM), BF16)],
        compiler_params=_params(("arbitrary", "arbitrary")),
    )(qn, qr, kn, v, kr, lse, do)


def _place():
    return lax.axis_index("x"), lax.axis_index("y"), lax.axis_index("c")


def _other_chips(x, y):
    return [(1 - x, y), (x, 1 - y), (1 - x, 1 - y)]


def _remote(src, dst, send_sem, recv_sem, to):
    return pltpu.make_async_remote_copy(src_ref=src, dst_ref=dst, send_sem=send_sem, recv_sem=recv_sem,
                                        device_id=to, device_id_type=MESH)


HBM = pl.BlockSpec(memory_space=pltpu.HBM)
SEM = pl.BlockSpec(memory_space=pltpu.SEMAPHORE)
EFFECT = pltpu.SideEffectType.DATAFLOW_SIDE_EFFECTING


def _in_hbm(a):
    return pltpu.with_memory_space_constraint(a, pltpu.HBM)


TOKEN = jax.ShapeDtypeStruct((8, LANES), F32)
IN_VMEM = pl.BlockSpec(memory_space=pltpu.VMEM)


def gather_start(shards, landings, after, tag):
    n = len(shards)

    def body(*refs):
        ins, lands = refs[:n], refs[n:2 * n]
        send, recv = refs[2 * n + len(after)], refs[2 * n + len(after) + 1]
        token = refs[-1]
        x, y, c = _place()
        me = 2 * x + y
        for w in range(n):
            for k, (px, py) in enumerate(_other_chips(x, y)):
                _remote(ins[w].at[c], lands[w].at[me, c], send.at[3 * w + k], recv.at[3 * w + k], (px, py, c)).start()
        token[...] = jnp.zeros_like(token)

    bufs = list(shards) + list(landings)
    outs = pl.pallas_call(
        body, name="gather_start_" + tag,
        out_shape=(pltpu.SemaphoreType.DMA((3 * n,)),) * 2 + tuple(pltpu.HBM(b.shape, b.dtype) for b in bufs) + (TOKEN,),
        in_specs=[HBM] * (2 * n) + [ANY] * len(after), out_specs=tuple([SEM, SEM] + [HBM] * (2 * n) + [IN_VMEM]),
        input_output_aliases={i: 2 + i for i in range(2 * n)},
        compiler_params=pltpu.CompilerParams(has_side_effects=EFFECT),
    )(*[_in_hbm(b) for b in bufs], *after)
    return (outs[0], outs[1]), outs[2:2 + n], outs[2 + n:2 + 2 * n], outs[-1]


def gather_wait(sems, shards, landings, after, tag):
    n = len(shards)
    send, recv = sems

    def body(*refs):
        ins, lands = refs[:n], refs[n:2 * n]
        send_sem, recv_sem = refs[2 * n], refs[2 * n + 1]
        x, y, c = _place()
        me = 2 * x + y
        for w in range(n):
            for k, (px, py) in enumerate(_other_chips(x, y)):
                cp = _remote(ins[w].at[c], lands[w].at[2 * px + py, c], send_sem.at[3 * w + k], recv_sem.at[3 * w + k],
                             (px, py, c))
                cp.wait_send()
                cp.wait_recv()

    bufs = list(shards) + list(landings)
    outs = pl.pallas_call(
        body, name="gather_wait_" + tag, out_shape=tuple(pltpu.HBM(b.shape, b.dtype) for b in bufs),
        in_specs=[HBM] * (2 * n) + [SEM, SEM] + [ANY] * len(after), out_specs=tuple([HBM] * (2 * n)),
        input_output_aliases={i: i for i in range(2 * n)},
        compiler_params=pltpu.CompilerParams(has_side_effects=EFFECT),
    )(*bufs, send, recv, *after)
    return outs[:n], outs[n:]


def gather_forward(landings, tag):
    n = len(landings)

    def body(*refs):
        ins, outs = refs[:n], refs[n:2 * n]
        send, recv = refs[2 * n:]
        x, y, c = _place()
        sibling = (x, y, 1 - c)
        cps = []
        for w in range(n):
            for k, (px, py) in enumerate(_other_chips(x, y)):
                j = 2 * px + py
                cp = _remote(ins[w].at[j, c], outs[w].at[j, c], send.at[3 * w + k], recv.at[3 * w + k], sibling)
                cp.start()
                cps.append(cp)
        for w in range(n):
            for k, (px, py) in enumerate(_other_chips(x, y)):
                blk = outs[w].at[2 * px + py, 1 - c]
                _remote(blk, blk, send.at[3 * w + k], recv.at[3 * w + k], sibling).wait_recv()
        for cp in cps:
            cp.wait_send()

    return pl.pallas_call(
        body, name="gather_forward_" + tag, in_specs=[ANY] * n, out_specs=[ANY] * n,
        out_shape=[jax.ShapeDtypeStruct(a.shape, a.dtype) for a in landings],
        input_output_aliases={i: i for i in range(n)},
        scratch_shapes=[pltpu.SemaphoreType.DMA((3 * n,)), pltpu.SemaphoreType.DMA((3 * n,))],
    )(*landings)


def pair_start(grads, landings, tag):
    n = len(grads)

    def body(*refs):
        ins, lands = refs[:n], refs[n:2 * n]
        send, recv = refs[2 * n], refs[2 * n + 1]
        token = refs[-1]
        x, y, c = _place()
        for w in range(n):
            _remote(ins[w].at[:, 1 - c], lands[w], send.at[w], recv.at[w], (x, y, 1 - c)).start()
        token[...] = jnp.zeros_like(token)

    bufs = list(grads) + list(landings)
    outs = pl.pallas_call(
        body, name="pair_start_" + tag,
        out_shape=(pltpu.SemaphoreType.DMA((n,)),) * 2 + tuple(pltpu.HBM(b.shape, b.dtype) for b in bufs) + (TOKEN,),
        in_specs=[HBM] * (2 * n), out_specs=tuple([SEM, SEM] + [HBM] * (2 * n) + [IN_VMEM]),
        input_output_aliases={i: 2 + i for i in range(2 * n)},
        compiler_params=pltpu.CompilerParams(has_side_effects=EFFECT),
    )(*[_in_hbm(b) for b in bufs])
    return (outs[0], outs[1]), outs[2:2 + n], outs[2 + n:2 + 2 * n], outs[-1]


def pair_wait(sems, grads, landings, after, tag):
    n = len(grads)

    def body(*refs):
        ins, lands = refs[:n], refs[n:2 * n]
        send, recv = refs[2 * n], refs[2 * n + 1]
        x, y, c = _place()
        for w in range(n):
            cp = _remote(ins[w].at[:, 1 - c], lands[w], send.at[w], recv.at[w], (x, y, 1 - c))
            cp.wait_send()
            cp.wait_recv()

    bufs = list(grads) + list(landings)
    outs = pl.pallas_call(
        body, name="pair_wait_" + tag, out_shape=tuple(pltpu.HBM(b.shape, b.dtype) for b in bufs),
        in_specs=[HBM] * (2 * n) + [SEM, SEM] + [ANY] * len(after), out_specs=tuple([HBM] * (2 * n)),
        input_output_aliases={i: i for i in range(2 * n)},
        compiler_params=pltpu.CompilerParams(has_side_effects=EFFECT),
    )(*bufs, sems[0], sems[1], *after)
    return outs[:n], outs[n:]


def scatter_start(sums, landings, tag):
    n = len(sums)

    def body(*refs):
        ins, lands = refs[:n], refs[n:2 * n]
        send, recv = refs[2 * n], refs[2 * n + 1]
        token = refs[-1]
        x, y, c = _place()
        for w in range(n):
            for k, (px, py) in enumerate(_other_chips(x, y)):
                _remote(ins[w].at[2 * px + py], lands[w].at[k], send.at[3 * w + k], recv.at[3 * w + k], (px, py, c)).start()
        token[...] = jnp.zeros_like(token)

    bufs = list(sums) + list(landings)
    outs = pl.pallas_call(
        body, name="scatter_start_" + tag,
        out_shape=(pltpu.SemaphoreType.DMA((3 * n,)),) * 2 + tuple(pltpu.HBM(b.shape, b.dtype) for b in bufs) + (TOKEN,),
        in_specs=[HBM] * (2 * n), out_specs=tuple([SEM, SEM] + [HBM] * (2 * n) + [IN_VMEM]),
        input_output_aliases={i: 2 + i for i in range(2 * n)},
        compiler_params=pltpu.CompilerParams(has_side_effects=EFFECT),
    )(*[_in_hbm(b) for b in bufs])
    return (outs[0], outs[1]), outs[2:2 + n], outs[2 + n:2 + 2 * n], outs[-1]


def scatter_wait(sems, sums, landings, after, tag):
    n = len(sums)

    def body(*refs):
        ins, lands = refs[:n], refs[n:2 * n]
        send, recv = refs[2 * n], refs[2 * n + 1]
        x, y, c = _place()
        for w in range(n):
            for k, (px, py) in enumerate(_other_chips(x, y)):
                cp = _remote(ins[w].at[2 * px + py], lands[w].at[k], send.at[3 * w + k], recv.at[3 * w + k], (px, py, c))
                cp.wait_send()
                cp.wait_recv()

    bufs = list(sums) + list(landings)
    outs = pl.pallas_call(
        body, name="scatter_wait_" + tag, out_shape=tuple(pltpu.HBM(b.shape, b.dtype) for b in bufs),
        in_specs=[HBM] * (2 * n) + [SEM, SEM] + [ANY] * len(after), out_specs=tuple([HBM] * (2 * n)),
        input_output_aliases={i: i for i in range(2 * n)},
        compiler_params=pltpu.CompilerParams(has_side_effects=EFFECT),
    )(*bufs, sems[0], sems[1], *after)
    return outs[:n], outs[n:]


def half_exchange(halves, tag):
    n = len(halves)

    def body(*refs):
        ins, outs = refs[:n], refs[n:2 * n]
        send, recv = refs[2 * n:]
        x, y, c = _place()
        cps = []
        for w in range(n):
            cp = _remote(ins[w], outs[w], send.at[w], recv.at[w], (x, y, 1 - c))
            cp.start()
            cps.append(cp)
        for cp in cps:
            cp.wait()

    return pl.pallas_call(
        body, name="grad_half_exchange_" + tag, in_specs=[ANY] * n, out_specs=[ANY] * n,
        out_shape=[jax.ShapeDtypeStruct(h.shape, h.dtype) for h in halves],
        scratch_shapes=[pltpu.SemaphoreType.DMA((n,)), pltpu.SemaphoreType.DMA((n,))],
    )(*halves)


def gather_small(v):
    def body(v_ref, o_ref, send, recv, local):
        x, y, c = _place()
        me = 4 * x + 2 * y + c
        own = pltpu.make_async_copy(v_ref, o_ref.at[me], local)
        own.start()
        cps = []
        for k in range(1, 8):
            fx, fy, fc = (k >> 2) & 1, (k >> 1) & 1, k & 1
            to = (x ^ fx if fx else x, y ^ fy if fy else y, c ^ fc if fc else c)
            cp = _remote(v_ref, o_ref.at[me], send.at[k - 1], recv.at[k - 1], to)
            cp.start()
            cps.append(cp)
        for k in range(1, 8):
            fx, fy, fc = (k >> 2) & 1, (k >> 1) & 1, k & 1
            px, py, pc = (x ^ fx if fx else x, y ^ fy if fy else y, c ^ fc if fc else c)
            cps[k - 1].wait_send()
            _remote(v_ref, o_ref.at[4 * px + 2 * py + pc], send.at[k - 1], recv.at[k - 1], (px, py, pc)).wait_recv()
        own.wait()

    return pl.pallas_call(
        body, name="gather_small_grads", in_specs=[ANY], out_specs=ANY,
        out_shape=jax.ShapeDtypeStruct((8,) + v.shape, v.dtype),
        scratch_shapes=[pltpu.SemaphoreType.DMA((7,)), pltpu.SemaphoreType.DMA((7,)), pltpu.SemaphoreType.DMA],
    )(v)


def _row_tile(rows, cols, nbuf_bytes):
    tm = _pick(rows, 512, 16)
    while tm * cols * nbuf_bytes * 2 > VMEM_BUDGET_V7X and tm % 32 == 0:
        tm //= 2
    return tm


def pair_sum(g, r, c_idx, tag):
    _, _, rows, cols = g.shape
    tm = _row_tile(rows, cols, 2 + 2 + 2)
    nb = rows // tm

    def body(c_ref, g_ref, r_ref, o_ref):
        o_ref[...] = (g_ref[...].astype(F32) + r_ref[...].astype(F32)).astype(o_ref.dtype)

    gs = pltpu.PrefetchScalarGridSpec(
        num_scalar_prefetch=1, grid=(N_CHIPS, nb),
        in_specs=[pl.BlockSpec((None, None, tm, cols), lambda j, i, c_ref: (j, c_ref[0], i, 0)),
                  pl.BlockSpec((None, tm, cols), lambda j, i, c_ref: (j, i, 0))],
        out_specs=pl.BlockSpec((None, tm, cols), lambda j, i, c_ref: (j, i, 0)))
    return pl.pallas_call(body, name="grad_pair_sum_" + tag, grid_spec=gs,
                          out_shape=jax.ShapeDtypeStruct(r.shape, BF16),
                          compiler_params=_params(("arbitrary", "arbitrary")))(c_idx, g, r)


def chip_sum(s, r, j_idx, tag):
    _, rows, cols = s.shape
    tm = _row_tile(rows, cols, 2 + 3 * 2 + 4)
    nb = rows // tm

    def body(j_ref, s_ref, r_ref, o_ref):
        t = s_ref[...].astype(F32)
        for k in range(3):
            t = t + r_ref[k].astype(F32)
        o_ref[...] = t

    gs = pltpu.PrefetchScalarGridSpec(
        num_scalar_prefetch=1, grid=(nb,),
        in_specs=[pl.BlockSpec((None, tm, cols), lambda i, j_ref: (j_ref[0], i, 0)),
                  pl.BlockSpec((3, tm, cols), lambda i, j_ref: (0, i, 0))],
        out_specs=pl.BlockSpec((tm, cols), lambda i, j_ref: (i, 0)))
    return pl.pallas_call(body, name="grad_chip_sum_" + tag, grid_spec=gs,
                          out_shape=jax.ShapeDtypeStruct((rows, cols), F32),
                          compiler_params=_params(("arbitrary",)))(j_idx, s, r)


def adamw(w, g, m, v, *, name):
    rows, cols = w.shape
    tm = _row_tile(rows, cols, 7 * 4)

    return rowwise(_adamw_math, [w, g, m, v], [], [(cols, F32)] * 3, tm=tm, name=name)


def _adamw_math(wb, gb, mb, vb):
    m2 = ADAM_B1 * mb + (1.0 - ADAM_B1) * gb
    v2 = ADAM_B2 * vb + (1.0 - ADAM_B2) * (gb * gb)
    m_hat = m2 / (1.0 - ADAM_B1 ** ADAM_STEP)
    v_hat = v2 / (1.0 - ADAM_B2 ** ADAM_STEP)
    delta = -ADAM_LR * (m_hat / (jnp.sqrt(v_hat) + ADAM_EPS) + ADAM_WD * wb)
    return delta, m2, v2


def adamw_shard(w, g_own, g_sib, m, v, c_idx, *, name):
    rows, cols = g_own.shape
    tm = _row_tile(rows, cols, 9 * 4)
    nb = rows // tm

    def body(c_ref, w_ref, go_ref, gs_ref, m_ref, v_ref, g_out, d_out, m_out, v_out):
        gb = jnp.where(pl.program_id(0) == c_ref[0], go_ref[...], gs_ref[...])
        delta, m2, v2 = _adamw_math(w_ref[...], gb, m_ref[...], v_ref[...])
        g_out[...] = gb
        d_out[...] = delta
        m_out[...] = m2
        v_out[...] = v2

    full = pl.BlockSpec((tm, cols), lambda h, i, c_ref: (h * nb + i, 0))
    own = pl.BlockSpec((tm, cols), lambda h, i, c_ref: (jnp.where(h == c_ref[0], i, 0), 0))
    sib = pl.BlockSpec((tm, cols), lambda h, i, c_ref: (jnp.where(h == c_ref[0], 0, i), 0))
    gs = pltpu.PrefetchScalarGridSpec(num_scalar_prefetch=1, grid=(2, nb), in_specs=[full, own, sib, full, full],
                                      out_specs=[full] * 4)
    return pl.pallas_call(body, name=name, grid_spec=gs, out_shape=[jax.ShapeDtypeStruct(w.shape, F32)] * 4,
                          compiler_params=_params(("arbitrary", "arbitrary")))(c_idx, w, g_own, g_sib, m, v)


def sum_devices(a):
    def body(a_ref, o_ref):
        t = a_ref[0]
        for k in range(1, 8):
            t = t + a_ref[k]
        o_ref[...] = t

    return pl.pallas_call(body, name="sum_small_grads", out_shape=jax.ShapeDtypeStruct(a.shape[1:], a.dtype))(a)


def _halves(w2d):
    r, c = w2d.shape
    return w2d.reshape(2, r // 2, c)


def kernel(x, p, ffn1_norm, ffn1_w_gate, ffn1_w_up, ffn1_w_down, mix_norm, w_in, q_a_norm, w_uq, kv_a_norm, w_ukv, na_rpb, w_branch_a, w_branch_b, w_out, ffn2_norm, ffn2_w_gate, ffn2_w_up, ffn2_w_down, pl_norm, w_pl, w_pl_gate, final_norm, loss_target, m_ffn1_norm, m_ffn1_w_gate, m_ffn1_w_up, m_ffn1_w_down, m_mix_norm, m_w_in, m_q_a_norm, m_w_uq, m_kv_a_norm, m_w_ukv, m_na_rpb, m_w_branch_a, m_w_branch_b, m_w_out, m_ffn2_norm, m_ffn2_w_gate, m_ffn2_w_up, m_ffn2_w_down, m_pl_norm, m_w_pl, m_w_pl_gate, m_final_norm, v_ffn1_norm, v_ffn1_w_gate, v_ffn1_w_up, v_ffn1_w_down, v_mix_norm, v_w_in, v_q_a_norm, v_w_uq, v_kv_a_norm, v_w_ukv, v_na_rpb, v_w_branch_a, v_w_branch_b, v_w_out, v_ffn2_norm, v_ffn2_w_gate, v_ffn2_w_up, v_ffn2_w_down, v_pl_norm, v_w_pl, v_w_pl_gate, v_final_norm):
    big = ["ffn1_w_gate", "ffn1_w_up", "ffn1_w_down", "w_in", "w_uq", "w_ukv", "w_branch_a", "w_branch_b", "w_out",
           "ffn2_w_gate", "ffn2_w_up", "ffn2_w_down", "w_pl", "w_pl_gate"]
    col_sharded = {"ffn1_w_gate", "ffn1_w_up", "w_in", "w_uq", "w_ukv", "w_branch_a", "w_branch_b", "ffn2_w_gate",
                   "ffn2_w_up", "w_pl"}
    small = ["ffn1_norm", "mix_norm", "q_a_norm", "kv_a_norm", "na_rpb", "ffn2_norm", "pl_norm", "final_norm"]
    order = ["ffn1_norm", "ffn1_w_gate", "ffn1_w_up", "ffn1_w_down", "mix_norm", "w_in", "q_a_norm", "w_uq",
             "kv_a_norm", "w_ukv", "na_rpb", "w_branch_a", "w_branch_b", "w_out", "ffn2_norm", "ffn2_w_gate",
             "ffn2_w_up", "ffn2_w_down", "pl_norm", "w_pl", "w_pl_gate", "final_norm"]
    env = dict(locals())
    W = {n: env[n] for n in order}
    Mo = {n: env["m_" + n] for n in order}
    Vo = {n: env["v_" + n] for n in order}

    xs = x[0]
    S, D = xs.shape
    tgt = loss_target[0]
    ps = p[0, 0]
    NAW = w_branch_a.shape[1]
    MLAW = w_branch_b.shape[1]
    NH, MH = NAW // HEAD_DIM, MLAW // HEAD_DIM
    QR, KVR = w_uq.shape[1], w_ukv.shape[1]
    F = ffn1_w_down.shape[1] * N_CHIPS
    cx, cy, cc = _place()
    c_idx = jnp.reshape(cc, (1,)).astype(jnp.int32)
    j_idx = jnp.reshape(2 * cx + cy, (1,)).astype(jnp.int32)

    me_chip = 2 * cx + cy
    groups = [["ffn1_w_gate"], ["ffn1_w_up"], ["ffn1_w_down"], ["w_in"],
              ["w_uq", "w_ukv", "w_branch_a", "w_branch_b", "w_out"],
              ["ffn2_w_gate", "ffn2_w_up", "ffn2_w_down"], ["w_pl", "w_pl_gate"]]
    started, tokens = [], []
    for g, members in enumerate(groups):
        shards = [_halves(W[n][0].astype(BF16)) for n in members]
        landings = [lax.empty((N_CHIPS,) + s.shape, BF16) for s in shards]
        sems, shards_thru, landings_thru, token = gather_start(shards, landings, tokens[-1:], str(g))
        started.append((sems, shards_thru, landings_thru))
        tokens.append(token)
    gathered = {}

    def arrive(n, after):
        g = [n in members for members in groups].index(True)
        sems, shards_thru, landings_thru = started[g]
        after = list(after) if isinstance(after, (list, tuple)) else [after]
        shards_out, landed = gather_wait(sems, shards_thru, landings_thru, after, str(g))
        for name, full, own in zip(groups[g], gather_forward(landed, str(g)), shards_out):
            gathered[name] = lax.dynamic_update_slice(full, own[None], (me_chip, 0, 0, 0))

    def stacked(n, after=None):
        if n not in gathered:
            arrive(n, after)
        g = gathered[n]
        return g.reshape(N_CHIPS, 2 * g.shape[2], g.shape[3])

    def plain(n, after=None):
        if n in col_sharded:
            st = stacked(n, after)
            return st.transpose(1, 0, 2).reshape(st.shape[1], N_CHIPS * st.shape[2])
        if n not in gathered:
            arrive(n, after)
        g = gathered[n]
        return g.reshape(N_CHIPS * 2 * g.shape[2], g.shape[3])

    n_na = 3 * NAW
    n_front = n_na + QR + KVR
    n_in = n_front + MLA_ROPE + 2 * D
    off_ql, off_kvl, off_kr = 2 * D, 2 * D + QR, 2 * D + QR + KVR
    kr_w = 2 * LANES
    rest_w = off_kr + kr_w
    rest_ranges = [(n_front + MLA_ROPE, n_in), (n_na, n_front), (n_front, n_front + MLA_ROPE)]

    def shard_cols(st, lo, hi):
        nb, parts = st.shape[2], []
        while lo < hi:
            j = lo // nb
            end = min(hi, (j + 1) * nb)
            parts.append(st[j][:, lo - j * nb:end - j * nb])
            lo = end
        return parts

    def w_in_shards(g_na, g_rest):
        pieces = [(0, n_na, g_na, 0)]
        o = 0
        for lo, hi in rest_ranges:
            pieces.append((lo, hi, g_rest, o))
            o += hi - lo
        nb, shards = n_in // N_CHIPS, []
        for j in range(N_CHIPS):
            parts = []
            for lo, hi, src, o in sorted(pieces):
                a, b = max(lo, j * nb), min(hi, (j + 1) * nb)
                if a < b:
                    parts.append(src[:, o + a - lo:o + b - lo])
            shards.append(jnp.concatenate(parts, axis=1))
        return jnp.stack(shards)

    pos = jnp.arange(S, dtype=F32)
    inv_freq = 1.0 / (ROPE_THETA ** (jnp.arange(0, MLA_ROPE, 2, dtype=F32) / MLA_ROPE))
    ang = pos[:, None] * inv_freq[None, :]
    zpad = jnp.zeros((S, LANES - MLA_ROPE), F32)
    cos_t = jnp.concatenate([jnp.cos(ang), jnp.cos(ang), zpad], axis=1)
    sin_t = jnp.concatenate([-jnp.sin(ang), jnp.sin(ang), zpad], axis=1)

    def ffn_fwd(h, norm_g, tag, pre, after=()):
        n = norm_fwd(h, norm_g, name=f"{tag}_norm")
        g = mm(n, stacked(pre + "_w_gate", [n, *after]), name=f"{tag}_gate", b_stack=True)
        u = mm(n, stacked(pre + "_w_up", g), name=f"{tag}_up", b_stack=True)
        a = swiglu_fwd(g, u, name=f"{tag}_act")
        h_out = mm(a, plain(pre + "_w_down", a), name=f"{tag}_down", res=h, alpha=0.5)
        return h_out, (n, g, u, a)

    def ffn_bwd(h, norm_g, saved, dh, dh_half, tag, pre, last, after=()):
        n, g, u, a = saved
        G[pre + "_w_down"] = mm(a, dh_half, name=f"{tag}_dw_down", ta=True, out_dtype=BF16, after=after)
        begun_d, token_d = pair_begin([pre + "_w_down"], tag + "_d")
        da = mm(dh_half, plain(pre + "_w_down"), name=f"{tag}_da", tb=True, after=[token_d])
        dg, du = swiglu_bwd(g, u, da, name=f"{tag}_dact")
        G[pre + "_w_gate"] = mm(n, dg, name=f"{tag}_dw_gate", ta=True, out_dtype=BF16, out_stack=True)
        G[pre + "_w_up"] = mm(n, du, name=f"{tag}_dw_up", ta=True, out_dtype=BF16, out_stack=True)
        begun_gu, token_gu = pair_begin([pre + "_w_gate", pre + "_w_up"], tag + "_gu")
        dn = mm(dg, stacked(pre + "_w_gate"), name=f"{tag}_dn_gate", tb=True, b_stack=True, after=[token_gu])
        token = reduce_go([begun_d, begun_gu], tag, [dn])
        dn = mm(du, stacked(pre + "_w_up"), name=f"{tag}_dn_up", tb=True, b_stack=True, res=dn, after=[token])
        return norm_bwd(h, norm_g, dn, name=f"{tag}_dnorm", res=dh, bf16_alpha=None if last else 1.0)

    bias = na_bias(na_rpb[0], after=tokens[-1:])
    h1, ffn1_saved = ffn_fwd(xs, ffn1_norm, "ffn1", "ffn1", after=[bias, tokens[-1]])
    u_mix = norm_fwd(h1, mix_norm, name="mix_norm")
    win_st = stacked("w_in", u_mix)
    w_na = jnp.concatenate(shard_cols(win_st, 0, n_na), axis=1)
    w_rest = jnp.concatenate([p_ for lo, hi in rest_ranges for p_ in shard_cols(win_st, lo, hi)]
                             + [jnp.zeros((D, kr_w - MLA_ROPE), BF16)], axis=1)
    z_na = mm(u_mix, w_na, name="mix_in_na", out_dtype=BF16)
    z = mm(u_mix, w_rest, name="mix_in_rest")
    o_a = na_fwd(z_na, bias, NH, S)
    c_q = norm_fwd((z, QR, off_ql // QR), q_a_norm, name="q_a_norm")
    c_kv = norm_fwd((z, KVR, off_kvl // KVR), kv_a_norm, name="kv_a_norm")
    wuq = plain("w_uq", c_kv).reshape(QR, MH, MLA_QK)
    wuq_n = wuq[:, :, :MLA_NOPE].reshape(QR, MH * MLA_NOPE)
    wuq_r = jnp.pad(wuq[:, :, MLA_NOPE:], ((0, 0), (0, 0), (0, LANES - MLA_ROPE))).reshape(QR, MH * LANES)
    wukv = plain("w_ukv").reshape(KVR, MH, 2, HEAD_DIM)
    wuk = wukv[:, :, 0].reshape(KVR, MH * HEAD_DIM)
    wuv = wukv[:, :, 1].reshape(KVR, MH * HEAD_DIM)
    q_n = mm(c_q, wuq_n, name="mla_q_nope", out_dtype=BF16)
    q_r = rope(mm(c_q, wuq_r, name="mla_q_rope"), cos_t, sin_t, name="rope_q", out_dtype=BF16)
    k_n = mm(c_kv, wuk, name="mla_k_nope", out_dtype=BF16)
    v_m = mm(c_kv, wuv, name="mla_v", out_dtype=BF16)
    k_r = rope((z, LANES, off_kr // LANES), cos_t, sin_t, name="rope_k", out_dtype=BF16)
    o_b, lse = mla_fwd(q_n, q_r, k_n, v_m, k_r, MH, S)
    y_a = mm(o_a, stacked("w_branch_a"), name="branch_a", b_stack=True)
    y_b = mm(o_b, stacked("w_branch_b"), name="branch_b", b_stack=True)
    z_ga, z_gb = (z, D, 0), (z, D, 1)
    merged = rowwise(lambda ga, gb, ya, yb: _sig(ga) * ya + _sig(gb) * yb, [z_ga, z_gb, y_a, y_b], [], [(D, BF16)],
                     tm=256, name="merge")[0]
    h2 = mm(merged, plain("w_out"), name="mix_out", res=h1)
    h3, ffn2_saved = ffn_fwd(h2, ffn2_norm, "ffn2", "ffn2")
    n4 = norm_fwd(h3, pl_norm, name="pl_norm")
    pg_pre = mm(n4, plain("w_pl_gate", n4), name="pl_gate")
    pe = mm(ps, stacked("w_pl"), name="pl_embed", b_stack=True)

    def tail(h3b, pgb, peb, tb_, fg):
        pg = _sig(pgb)
        h4 = h3b + pg * peb
        r = _rstd(h4)
        xh = h4 * r
        err = xh * fg - tb_
        loss_rows = jnp.mean(err * err, axis=-1, keepdims=True)
        dy = err * (1.0 / D)
        dxh = dy * fg
        dh4 = r * (dxh - xh * jnp.mean(dxh * xh, axis=-1, keepdims=True))
        loss_part = jnp.broadcast_to(0.5 * jnp.sum(loss_rows, axis=0, keepdims=True), (1, LANES))
        return (dh4, dh4 * peb * pg * (1.0 - pg), dh4 * pg, loss_part, jnp.sum(dy * xh, axis=0, keepdims=True))

    dh4, dpg_pre, dpe, loss_part, g_final = rowwise(
        tail, [h3, pg_pre, pe, tgt], [final_norm.reshape(1, D)], [(D, F32), (D, BF16), (D, BF16)],
        accs=[(1, LANES), (1, D)], tm=128, name="loss_tail")
    loss = lax.psum(loss_part[0, 0], ("x", "y", "c"))

    G = {}
    pending = []

    def four(g):
        if g.ndim == 2:
            return g.reshape(N_CHIPS, 2, g.shape[0] // (2 * N_CHIPS), g.shape[1])
        return g.reshape(N_CHIPS, 2, g.shape[1] // 2, g.shape[2])

    def pair_begin(names, tag):
        g4 = [four(G[n]) for n in names]
        lands = [lax.empty((N_CHIPS,) + g.shape[2:], BF16) for g in g4]
        sems, g4, lands, token = pair_start(g4, lands, tag)
        return (names, tag, sems, g4, lands), token

    def reduce_go(begun, tag, after):
        names, sums = [], []
        for b_names, b_tag, sems, g4, lands in begun:
            g4, got = pair_wait(sems, g4, lands, list(after), b_tag)
            sums += [pair_sum(a_, r_, c_idx, n) for n, a_, r_ in zip(b_names, g4, got)]
            names += b_names
        lands = [lax.empty((N_CHIPS - 1,) + s_.shape[1:], BF16) for s_ in sums]
        sems, sums, lands, token = scatter_start(sums, lands, tag)
        pending.append((names, tag, sems, sums, lands))
        return token

    def reduce_finish(entry, after):
        names, tag, sems, sums, lands = entry
        sums, got = scatter_wait(sems, sums, lands, after, tag)
        halves = [chip_sum(a, b, j_idx, n) for n, a, b in zip(names, sums, got)]
        done = []
        for n, own, sib in zip(names, halves, half_exchange(halves, tag)):
            shp = W[n].shape
            two_d = lambda a_: a_.reshape(shp[1], shp[2])
            out = adamw_shard(two_d(W[n]), own, sib, two_d(Mo[n]), two_d(Vo[n]), c_idx, name="adamw_" + n)
            grads[n], delta[n], new_m[n], new_v[n] = [o.reshape(shp) for o in out]
            done.append(out[0])
        return done

    G["w_pl"] = mm(ps, dpe, name="pl_dw_embed", ta=True, out_dtype=BF16, out_stack=True)
    G["w_pl_gate"] = mm(n4, dpg_pre, name="pl_dw_gate", ta=True, out_dtype=BF16)
    begun_pl, token = pair_begin(["w_pl", "w_pl_gate"], "pl")
    dn4 = mm(dpg_pre, plain("w_pl_gate"), name="pl_dn", tb=True, after=[token])
    dh3, dh3_half, g_pl = norm_bwd(h3, pl_norm, dn4, name="pl_dnorm", res=dh4, bf16_alpha=0.5)
    token = reduce_go([begun_pl], "pl", [dh3])
    dh2, dh2_b, g_ffn2 = ffn_bwd(h2, ffn2_norm, ffn2_saved, dh3, dh3_half, "ffn2", "ffn2", last=False, after=[token])

    G["w_out"] = mm(merged, dh2_b, name="mix_dw_out", ta=True, out_dtype=BF16)
    dmerged = mm(dh2_b, plain("w_out"), name="mix_dmerged", tb=True)

    def merge_bwd(ga, gb, ya, yb, dm):
        sa, sb = _sig(ga), _sig(gb)
        dgates = jnp.concatenate([dm * ya * sa * (1.0 - sa), dm * yb * sb * (1.0 - sb)], axis=1)
        return dm * sa, dm * sb, dgates

    dy_a, dy_b, dz_rest = rowwise(merge_bwd, [z_ga, z_gb, y_a, y_b, dmerged], [],
                                  [(D, BF16), (D, BF16), (2 * D, BF16, 0)], tm=256, name="merge_bwd",
                                  into=(None, rest_w))
    G["w_branch_a"] = mm(o_a, dy_a, name="branch_a_dw", ta=True, out_dtype=BF16, out_stack=True)
    G["w_branch_b"] = mm(o_b, dy_b, name="branch_b_dw", ta=True, out_dtype=BF16, out_stack=True)
    do_a = mm(dy_a, stacked("w_branch_a"), name="branch_a_dx", tb=True, b_stack=True)
    do_b = mm(dy_b, stacked("w_branch_b"), name="branch_b_dx", tb=True, b_stack=True)
    dq_na, dk_na, dv_na, dbias = na_bwd(z_na, bias, do_a, NH, S)
    g_rpb = na_rpb_grad(dbias)
    dq_n, dq_rr, dk_n, dv_m, dk_rr = mla_bwd(q_n, q_r, k_n, v_m, k_r, lse, do_b, MH, S)
    dq_r = rope(dq_rr, cos_t, -sin_t, name="rope_q_bwd", out_dtype=BF16)
    dz_rest = rope(dk_rr, cos_t, -sin_t, name="rope_k_bwd", out_dtype=BF16, into=(dz_rest, rest_w),
                   cb=off_kr // kr_w, zero_cols=kr_w - LANES)
    gw_uq_n = mm(c_q, dq_n, name="mla_dw_q_nope", ta=True, out_dtype=BF16)
    gw_uq_r = mm(c_q, dq_r, name="mla_dw_q_rope", ta=True, out_dtype=BF16)
    dc_q = mm(dq_n, wuq_n, name="mla_dcq_nope", tb=True)
    dc_q = mm(dq_r, wuq_r, name="mla_dcq_rope", tb=True, res=dc_q)
    gw_uk = mm(c_kv, dk_n, name="mla_dw_k", ta=True, out_dtype=BF16)
    gw_uv = mm(c_kv, dv_m, name="mla_dw_v", ta=True, out_dtype=BF16)
    dc_kv = mm(dk_n, wuk, name="mla_dckv_k", tb=True)
    dc_kv = mm(dv_m, wuv, name="mla_dckv_v", tb=True, res=dc_kv)
    dz_rest, g_qa = norm_bwd((z, QR, off_ql // QR), q_a_norm, dc_q, name="q_a_dnorm", want_f32=False, bf16_alpha=1.0,
                             into=(dz_rest, rest_w), cb=off_ql // QR)
    dz_rest, g_kva = norm_bwd((z, KVR, off_kvl // KVR), kv_a_norm, dc_kv, name="kv_a_dnorm", want_f32=False,
                              bf16_alpha=1.0, into=(dz_rest, rest_w), cb=off_kvl // KVR)
    dz_na = jnp.concatenate([dq_na, dk_na, dv_na], axis=1)

    def to_stack(g2d):
        k, n = g2d.shape
        return g2d.reshape(k, N_CHIPS, n // N_CHIPS).transpose(1, 0, 2)

    gw_uq = jnp.concatenate([gw_uq_n.reshape(QR, MH, MLA_NOPE), gw_uq_r.reshape(QR, MH, LANES)[:, :, :MLA_ROPE]],
                            axis=2).reshape(QR, MH * MLA_QK)
    G["w_uq"] = to_stack(gw_uq)
    gw_ukv = jnp.stack([gw_uk.reshape(KVR, MH, HEAD_DIM), gw_uv.reshape(KVR, MH, HEAD_DIM)], axis=2)
    G["w_ukv"] = to_stack(gw_ukv.reshape(KVR, MH * 2 * HEAD_DIM))
    begun_mix, token = pair_begin(["w_out", "w_branch_a", "w_branch_b", "w_uq", "w_ukv"], "mix")
    gw_na = mm(u_mix, dz_na, name="mix_dw_in_na", ta=True, out_dtype=BF16, after=[token])
    gw_rest = mm(u_mix, dz_rest, name="mix_dw_in_rest", ta=True, out_dtype=BF16)
    G["w_in"] = w_in_shards(gw_na, gw_rest)
    begun_win, token_win = pair_begin(["w_in"], "win")
    token_mix = reduce_go([begun_mix], "mix", [gw_rest, token_win])
    du_mix = mm(dz_na, w_na, name="mix_du_na", tb=True, after=[token_mix])
    token_win = reduce_go([begun_win], "win", [du_mix])
    du_mix = mm(dz_rest, w_rest, name="mix_du_rest", tb=True, res=du_mix, after=[token_win])
    dh1, dh1_half, g_mix = norm_bwd(h1, mix_norm, du_mix, name="mix_dnorm", res=dh2, bf16_alpha=0.5)
    grad_x, g_ffn1 = ffn_bwd(xs, ffn1_norm, ffn1_saved, dh1, dh1_half, "ffn1", "ffn1", last=True)

    small_g = {"ffn1_norm": g_ffn1, "mix_norm": g_mix, "q_a_norm": g_qa, "kv_a_norm": g_kva, "na_rpb": g_rpb,
               "ffn2_norm": g_ffn2, "pl_norm": g_pl, "final_norm": g_final}
    sizes = [int(np.prod(W[n].shape)) for n in small]
    total = sum(sizes)
    padded = -(-total // (8 * LANES)) * (8 * LANES)

    def pack(parts):
        flat = jnp.concatenate([jnp.reshape(parts[n], (-1,)).astype(F32) for n in small]
                               + [jnp.zeros((padded - total,), F32)])
        return flat.reshape(padded // LANES, LANES)

    def unpack(a):
        flat, out, o = a.reshape(-1), {}, 0
        for n, sz in zip(small, sizes):
            out[n] = flat[o:o + sz].reshape(W[n].shape)
            o += sz
        return out

    g_small = sum_devices(gather_small(pack(small_g)))
    d_small, m_small, v_small = adamw(pack(W), g_small, pack(Mo), pack(Vo), name="adamw_small")
    grads = unpack(g_small)
    delta, new_m, new_v = unpack(d_small), unpack(m_small), unpack(v_small)

    after = [grad_x]
    for entry in pending:
        after = reduce_finish(entry, after)

    return (loss, grad_x[None], *[grads[n] for n in order], *[delta[n] for n in order],
            *[new_m[n] for n in order], *[new_v[n] for n in order])
```

```python
import functools

import numpy as np
import jax
import jax.numpy as jnp
from jax import lax
from jax.experimental import pallas as pl
from jax.experimental.pallas import tpu as pltpu

F32 = jnp.float32
BF16 = jnp.bfloat16

VMEM_LIMIT_V7X = 56 * 1024 * 1024
VMEM_BUDGET_V7X = 40 * 1024 * 1024
LANES = 128

GRID_W = 64
NA_WIN_ROWS = 8
NA_WIN_COLS = 16
HEAD_DIM = 128
MLA_NOPE = 128
MLA_ROPE = 64
MLA_QK = MLA_NOPE + MLA_ROPE
ROPE_THETA = 10000.0
NORM_EPS = 1e-6
NEG_INF = -1e30
N_CHIPS = 4

ADAM_LR = 0.001
ADAM_B1 = 0.9
ADAM_B2 = 0.999
ADAM_EPS = 1e-08
ADAM_WD = 0.01
ADAM_STEP = 10

MESH = pl.DeviceIdType.MESH
ANY = pl.BlockSpec(memory_space=pl.ANY)


def _params(sem=None):
    return pltpu.CompilerParams(dimension_semantics=sem, vmem_limit_bytes=VMEM_LIMIT_V7X)


def _pick(n, target, align):
    best = None
    t = align
    while t <= min(n, target):
        if n % t == 0:
            best = t
        t += align
    return n if best is None else best


def mm(a, b, *, name, ta=False, tb=False, out_dtype=F32, res=None, alpha=1.0, b_stack=False, out_stack=False,
       exact=False, after=(), epilogue=None, epi_in=(), epi_out=()):
    K, M = (a.shape if ta else a.shape[::-1])
    nst = kb = nb = None
    if b_stack:
        nst = b.shape[0]
        if tb:
            N, kb = b.shape[1], b.shape[2]
            Kb = nst * kb
        else:
            Kb, nb = b.shape[1], b.shape[2]
            N = nst * nb
    else:
        N, Kb = (b.shape if tb else b.shape[::-1])
    assert K == Kb, (a.shape, b.shape, ta, tb)
    if out_stack:
        assert N % N_CHIPS == 0
    n_unit = N // N_CHIPS if out_stack else (nb if nb is not None else N)
    tn = _pick(n_unit, 512, LANES) if n_unit % 512 == 0 or n_unit <= 512 else _pick(n_unit, 1536, LANES)
    if ta and n_unit == N and 4 * K * N * jnp.dtype(b.dtype).itemsize <= VMEM_BUDGET_V7X:
        tn = N
    m_align = LANES if ta else 16
    tm = _pick(M, 1024, m_align)
    isz = lambda t: jnp.dtype(t.dtype).itemsize
    out_dtypes = list(epi_out) if epilogue is not None else [out_dtype]
    osz = sum(jnp.dtype(t).itemsize for t in out_dtypes) + sum(isz(e) for e in epi_in)

    def vmem(tm_, tn_):
        return (2 * tm_ * K * isz(a) + 2 * K * tn_ * isz(b) + 2 * tm_ * tn_ * osz + tm_ * tn_ * 4
                + (tm_ * K * 2 if ta else 0) + (2 * tm_ * tn_ * isz(res) if res is not None else 0))

    while vmem(tm, tn) > VMEM_BUDGET_V7X and tm % 2 == 0 and (tm // 2) % m_align == 0:
        tm //= 2
    while vmem(tm, tn) > VMEM_BUDGET_V7X and tn % 2 == 0 and (tn // 2) % LANES == 0 and n_unit % (tn // 2) == 0:
        tn //= 2
    assert vmem(tm, tn) <= VMEM_BUDGET_V7X, (name, tm, tn, K)

    a_spec = pl.BlockSpec((K, tm), lambda i, j: (0, i)) if ta else pl.BlockSpec((tm, K), lambda i, j: (i, 0))
    if b_stack and not tb:
        q = nb // tn
        b_spec = pl.BlockSpec((None, K, tn), lambda i, j: (j // q, 0, j % q))
    elif b_stack and tb:
        b_spec = pl.BlockSpec((nst, tn, kb), lambda i, j: (0, j, 0))
    elif tb:
        b_spec = pl.BlockSpec((tn, K), lambda i, j: (j, 0))
    else:
        b_spec = pl.BlockSpec((K, tn), lambda i, j: (0, j))
    if out_stack:
        qo = (N // N_CHIPS) // tn
        o_spec = pl.BlockSpec((None, tm, tn), lambda i, j: (j // qo, i, j % qo))
        o_shapes = [jax.ShapeDtypeStruct((N_CHIPS, M, N // N_CHIPS), t) for t in out_dtypes]
    else:
        o_spec = pl.BlockSpec((tm, tn), lambda i, j: (i, j))
        o_shapes = [jax.ShapeDtypeStruct((M, N), t) for t in out_dtypes]
    has_res = res is not None
    n_in = 2 + has_res + len(epi_in) + len(after)
    nn = (((1,), (0,)), ((), ()))
    nt = (((1,), (1,)), ((), ()))

    def body(*refs):
        a_ref, b_ref = refs[:2]
        r_ref = refs[2] if has_res else None
        e_refs = refs[2 + has_res:2 + has_res + len(epi_in)]
        o_refs = refs[n_in:n_in + len(out_dtypes)]
        if ta:
            at_ref = refs[-1]

            @pl.when(pl.program_id(1) == 0)
            def _():
                at_ref[...] = a_ref[...].astype(BF16).T

            lhs = at_ref[...]
        elif exact:
            lhs = a_ref[...]
        else:
            lhs = a_ref[...].astype(BF16)
        if exact:
            total = lax.dot_general(lhs, b_ref[...], nt if tb else nn, preferred_element_type=F32,
                                    precision=lax.Precision.HIGHEST)
        elif b_stack and tb:
            total = None
            for s in range(nst):
                part = lax.dot_general(lhs[:, s * kb:(s + 1) * kb], b_ref[s].astype(BF16), nt,
                                       preferred_element_type=F32)
                total = part if total is None else total + part
        else:
            total = lax.dot_general(lhs, b_ref[...].astype(BF16), nt if tb else nn, preferred_element_type=F32)
        if alpha != 1.0:
            total = total * alpha
        if has_res:
            total = total + r_ref[...].astype(F32)
        vals = epilogue(total, *[e[...] for e in e_refs]) if epilogue is not None else (total,)
        for o_ref, v in zip(o_refs, vals):
            o_ref[...] = v.astype(o_ref.dtype)

    tile = pl.BlockSpec((tm, tn), lambda i, j: (i, j))
    in_specs = [a_spec, b_spec] + [tile] * (has_res + len(epi_in)) + [ANY] * len(after)
    args = [a, b] + ([res] if has_res else []) + list(epi_in) + list(after)
    outs = pl.pallas_call(
        body, name=name, grid=(M // tm, N // tn), in_specs=in_specs, out_specs=[o_spec] * len(out_dtypes),
        out_shape=o_shapes, scratch_shapes=[pltpu.VMEM((tm, K), BF16)] if ta else [],
        compiler_params=_params(("parallel", "arbitrary")),
    )(*args)
    return outs if epilogue is not None else outs[0]


def rowwise(fn, rows, consts, outs, accs=(), *, tm, name, tn=None, into=None):
    rows = [r if isinstance(r, tuple) else (r, r.shape[1], 0) for r in rows]
    S = rows[0][0].shape[0]
    tm = _pick(S, tm, 16)
    nrow, ncon, nout = len(rows), len(consts), len(outs)
    outs = [o if len(o) == 3 else (o[0], o[1], None) for o in outs]
    if tn is None:
        grid = (S // tm,)
        in_specs = [pl.BlockSpec((tm, w), functools.partial(lambda i, cb: (i, cb), cb=cb)) for _, w, cb in rows]
        in_specs += [pl.BlockSpec(c.shape, lambda i: (0, 0)) for c in consts]
        out_specs = [pl.BlockSpec((tm, n), functools.partial(lambda i, cb: (i, cb), cb=cb or 0)) for n, _, cb in outs]
        out_specs += [pl.BlockSpec(s, lambda i: (0, 0)) for s in accs]
        sem = ("arbitrary",)
    else:
        assert not accs
        N = rows[0][1]
        grid = (S // tm, N // tn)
        in_specs = [pl.BlockSpec((tm, tn), lambda i, j: (i, j)) for _ in rows]
        in_specs += [pl.BlockSpec(c.shape, lambda i, j: (0, 0)) for c in consts]
        out_specs = [pl.BlockSpec((tm, tn), lambda i, j: (i, j)) for _ in outs]
        sem = ("parallel", "parallel")
    out_shape = [jax.ShapeDtypeStruct((S, n if cb is None else into[1]), dt) for n, dt, cb in outs]
    out_shape += [jax.ShapeDtypeStruct(s, F32) for s in accs]
    extra, aliases = [], {}
    if into is not None and into[0] is not None:
        extra = [into[0]]
        aliases = {nrow + ncon: [cb is not None for _, _, cb in outs].index(True)}

    def body(*refs):
        vals = fn(*[r[...] for r in refs[:nrow + ncon]])
        if not isinstance(vals, (tuple, list)):
            vals = (vals,)
        o_refs = refs[nrow + ncon + len(extra):]
        for o_ref, v in zip(o_refs[:nout], vals[:nout]):
            o_ref[...] = v.astype(o_ref.dtype)
        if accs:
            first = pl.program_id(0) == 0

            def accumulate(a_ref, v):
                @pl.when(first)
                def _():
                    a_ref[...] = v

                @pl.when(jnp.logical_not(first))
                def _():
                    a_ref[...] += v

            for a_ref, v in zip(o_refs[nout:], vals[nout:]):
                accumulate(a_ref, v.astype(F32))

    return pl.pallas_call(
        body, name=name, grid=grid, in_specs=in_specs + [ANY] * len(extra), out_specs=out_specs, out_shape=out_shape,
        input_output_aliases=aliases, compiler_params=_params(sem),
    )(*[r[0] for r in rows], *consts, *extra)


def _rstd(x):
    return lax.rsqrt(jnp.mean(x * x, axis=-1, keepdims=True) + NORM_EPS)


def norm_fwd(x, g, *, name, tm=256):
    w = x[1] if isinstance(x, tuple) else x.shape[1]

    def fn(xb, gb):
        return (xb * _rstd(xb)) * gb

    return rowwise(fn, [x], [g], [(w, BF16)], tm=tm, name=name)[0]


def norm_bwd(x, g, dn, *, name, res=None, want_f32=True, bf16_alpha=None, tm=256, into=None, cb=None):
    w = x[1] if isinstance(x, tuple) else x.shape[1]
    has_res = res is not None

    def fn(*blocks):
        if has_res:
            xb, dnb, rb, gb = blocks
        else:
            xb, dnb, gb = blocks
        r = _rstd(xb)
        xh = xb * r
        dxh = dnb * gb
        dx = r * (dxh - xh * jnp.mean(dxh * xh, axis=-1, keepdims=True))
        if has_res:
            dx = dx + rb
        out = []
        if want_f32:
            out.append(dx)
        if bf16_alpha is not None:
            out.append(dx * bf16_alpha if bf16_alpha != 1.0 else dx)
        out.append(jnp.sum(dnb * xh, axis=0, keepdims=True))
        return tuple(out)

    outs = ([(w, F32)] if want_f32 else []) + ([(w, BF16, cb)] if bf16_alpha is not None else [])
    rows = [x, dn] + ([res] if has_res else [])
    return rowwise(fn, rows, [g], outs, accs=[(1, w)], tm=tm, name=name, into=into)


def _sig(x):
    return jax.nn.sigmoid(x)


def swiglu_tile(ub, gb):
    return ub, gb * _sig(gb) * ub


def swiglu_bwd_tile(dab, gb, ub):
    sg = _sig(gb)
    return dab * ub * (sg + gb * sg * (1.0 - sg)), dab * (gb * sg)


def rope(x, cos, sin_signed, *, name, out_dtype, into=None, cb=None, zero_cols=0):
    w = x[1] if isinstance(x, tuple) else x.shape[1]
    half = MLA_ROPE // 2

    def fn(xb, cb, sb):
        lane = lax.broadcasted_iota(jnp.int32, cb.shape, 1)
        outs = []
        for hb in range(w // LANES):
            blk = xb[:, hb * LANES:(hb + 1) * LANES]
            partner = jnp.where(lane < half, pltpu.roll(blk, LANES - half, 1), pltpu.roll(blk, half, 1))
            outs.append(blk * cb + partner * sb)
        if zero_cols:
            outs.append(jnp.zeros((xb.shape[0], zero_cols), xb.dtype))
        return outs[0] if len(outs) == 1 else jnp.concatenate(outs, axis=1)

    return rowwise(fn, [x, cos, sin_signed], [], [(w + zero_cols, out_dtype, cb)], tm=256, name=name, into=into)[0]


def _na_tables():
    cols = np.arange(GRID_W)
    kw = NA_WIN_COLS
    dc = np.clip(cols[None, :] - cols[:, None], -(kw - 1), kw - 1) + (kw - 1)
    onehot = np.zeros((LANES, GRID_W * GRID_W), np.float32)
    onehot[dc.reshape(-1), np.arange(GRID_W * GRID_W)] = 1.0
    col_start = np.clip(cols - kw // 2, 0, GRID_W - kw)
    mask = (cols[None, :] >= col_start[:, None]) & (cols[None, :] < col_start[:, None] + kw)
    return onehot, np.where(mask, 0.0, NEG_INF).astype(np.float32)


def na_bias(rpb, after=()):
    H = rpb.shape[0]
    nr, kh = 2 * NA_WIN_ROWS - 1, NA_WIN_ROWS
    onehot, maskb = _na_tables()
    rp = jnp.pad(rpb.reshape(H * nr, 2 * NA_WIN_COLS - 1), ((0, 0), (0, LANES - (2 * NA_WIN_COLS - 1))))
    t1 = mm(rp, jnp.asarray(onehot), name="na_bias_table", exact=True, after=after).reshape(H, nr, GRID_W, GRID_W)
    t1 = t1 + jnp.asarray(maskb)[None, None]
    per_t = [jnp.stack([t1[:, i - t + kh - 1] for i in range(kh)], axis=2) for t in range(kh)]
    return jnp.stack(per_t, axis=1).reshape(H, kh, GRID_W, kh * GRID_W)


def na_rpb_grad(db):
    H = db.shape[0]
    nr, kh = 2 * NA_WIN_ROWS - 1, NA_WIN_ROWS
    onehot, _ = _na_tables()
    db = db.reshape(H, kh, GRID_W, kh, GRID_W)
    per_dr = []
    for dri in range(nr):
        terms = [db[:, t, :, dri - (kh - 1) + t, :] for t in range(kh) if 0 <= dri - (kh - 1) + t < kh]
        per_dr.append(functools.reduce(jnp.add, terms))
    dt1 = jnp.stack(per_dr, axis=1).reshape(H * nr, GRID_W * GRID_W)
    g = mm(dt1, jnp.asarray(onehot), name="na_rpb_grad", tb=True, exact=True)
    return g[:, :2 * NA_WIN_COLS - 1].reshape(H, nr, 2 * NA_WIN_COLS - 1)


def _na_first_row(r, rows):
    return jnp.clip(r - NA_WIN_ROWS // 2, 0, rows - NA_WIN_ROWS)


NA_ROWS_PER_STEP = 8


def _na_probs(q, k_ref, b_ref, r, rows):
    first = _na_first_row(r, rows)
    start = pl.multiple_of(first * GRID_W, GRID_W)
    k = k_ref[pl.ds(start, NA_WIN_ROWS * GRID_W), :]
    s = lax.dot_general(q, k, (((1,), (1,)), ((), ())), preferred_element_type=F32)
    s = s * (HEAD_DIM ** -0.5) + b_ref[r - first]
    m = jnp.max(s, axis=-1, keepdims=True)
    e = jnp.exp(s - m)
    return k, e / jnp.sum(e, axis=-1, keepdims=True), start, r - first


def na_fwd(z, bias, H, S):
    rows = S // GRID_W
    nkeys = NA_WIN_ROWS * GRID_W
    rb = _pick(rows, NA_ROWS_PER_STEP, 1)

    def body(q_ref, k_ref, v_ref, b_ref, o_ref):
        for j in range(rb):
            r = pl.program_id(1) * rb + j
            rows_j = pl.ds(j * GRID_W, GRID_W)
            _, p, start, _ = _na_probs(q_ref[rows_j, :], k_ref, b_ref, r, rows)
            v = v_ref[pl.ds(start, nkeys), :]
            o_ref[rows_j, :] = jnp.dot(p.astype(BF16), v, preferred_element_type=F32).astype(o_ref.dtype)

    return pl.pallas_call(
        body, name="na_fwd", grid=(H, rows // rb),
        in_specs=[pl.BlockSpec((rb * GRID_W, HEAD_DIM), lambda h, i: (i, h)),
                  pl.BlockSpec((S, HEAD_DIM), lambda h, i: (0, H + h)),
                  pl.BlockSpec((S, HEAD_DIM), lambda h, i: (0, 2 * H + h)),
                  pl.BlockSpec((None, NA_WIN_ROWS, GRID_W, nkeys), lambda h, i: (h, 0, 0, 0))],
        out_specs=pl.BlockSpec((rb * GRID_W, HEAD_DIM), lambda h, i: (i, h)),
        out_shape=jax.ShapeDtypeStruct((S, H * HEAD_DIM), BF16),
        compiler_params=_params(("parallel", "arbitrary")),
    )(z, z, z, bias)


def na_bwd(z, bias, do, H, S):
    rows = S // GRID_W
    nkeys = NA_WIN_ROWS * GRID_W
    rb = _pick(rows, NA_ROWS_PER_STEP, 1)
    tn_dims = (((0,), (0,)), ((), ()))

    def body(q_ref, k_ref, v_ref, b_ref, do_ref, dq_ref, dk_ref, dv_ref, db_ref, dk_acc, dv_acc):
        i = pl.program_id(1)

        @pl.when(i == 0)
        def _():
            dk_acc[...] = jnp.zeros_like(dk_acc)
            dv_acc[...] = jnp.zeros_like(dv_acc)
            db_ref[...] = jnp.zeros_like(db_ref)

        for j in range(rb):
            rows_j = pl.ds(j * GRID_W, GRID_W)
            q = q_ref[rows_j, :]
            k, p, start, t = _na_probs(q, k_ref, b_ref, i * rb + j, rows)
            keys = pl.ds(start, nkeys)
            dob = do_ref[rows_j, :].astype(BF16)
            dp = lax.dot_general(dob, v_ref[keys, :], (((1,), (1,)), ((), ())), preferred_element_type=F32)
            ds = p * (dp - jnp.sum(dp * p, axis=-1, keepdims=True))
            dsb = (ds * (HEAD_DIM ** -0.5)).astype(BF16)
            dq_ref[rows_j, :] = jnp.dot(dsb, k, preferred_element_type=F32).astype(dq_ref.dtype)
            dk_acc[keys, :] += lax.dot_general(dsb, q, tn_dims, preferred_element_type=F32)
            dv_acc[keys, :] += lax.dot_general(p.astype(BF16), dob, tn_dims, preferred_element_type=F32)
            db_ref[t] += ds

        @pl.when(i == rows // rb - 1)
        def _():
            dk_ref[...] = dk_acc[...].astype(dk_ref.dtype)
            dv_ref[...] = dv_acc[...].astype(dv_ref.dtype)

    W = H * HEAD_DIM
    qspec = pl.BlockSpec((rb * GRID_W, HEAD_DIM), lambda h, i: (i, h))
    bspec = pl.BlockSpec((None, NA_WIN_ROWS, GRID_W, nkeys), lambda h, i: (h, 0, 0, 0))
    return pl.pallas_call(
        body, name="na_bwd", grid=(H, rows // rb),
        in_specs=[qspec, pl.BlockSpec((S, HEAD_DIM), lambda h, i: (0, H + h)),
                  pl.BlockSpec((S, HEAD_DIM), lambda h, i: (0, 2 * H + h)), bspec, qspec],
        out_specs=[qspec, pl.BlockSpec((S, HEAD_DIM), lambda h, i: (0, h)),
                   pl.BlockSpec((S, HEAD_DIM), lambda h, i: (0, h)), bspec],
        out_shape=[jax.ShapeDtypeStruct((S, W), BF16)] * 3 + [jax.ShapeDtypeStruct((H, NA_WIN_ROWS, GRID_W, nkeys), F32)],
        scratch_shapes=[pltpu.VMEM((S, HEAD_DIM), F32)] * 2,
        compiler_params=_params(("arbitrary", "arbitrary")),
    )(z, z, z, bias, do)


def _mla_keys(kn_ref, kr_ref, kcat):
    @pl.when(pl.program_id(1) == 0)
    def _():
        kcat[:, :HEAD_DIM] = kn_ref[...]
        kcat[:, HEAD_DIM:] = kr_ref[...]


MLA_LOG2_SCALE = (MLA_QK ** -0.5) * 1.4426950408889634


def _mla_scores(qn_ref, qr_ref, kcat):
    qcat = jnp.concatenate([qn_ref[...], qr_ref[...]], axis=1)
    return qcat, lax.dot_general(qcat, kcat[...], (((1,), (1,)), ((), ())), preferred_element_type=F32)


def mla_fwd(qn, qr, kn, v, kr, H, S):
    tq = _pick(S, 256, 16)

    def body(qn_ref, qr_ref, kn_ref, v_ref, kr_ref, o_ref, lse_ref, kcat):
        _mla_keys(kn_ref, kr_ref, kcat)
        _, s = _mla_scores(qn_ref, qr_ref, kcat)
        m = jnp.max(s, axis=-1, keepdims=True)
        e = jnp.exp2((s - m) * MLA_LOG2_SCALE)
        l = jnp.sum(e, axis=-1, keepdims=True)
        o = jnp.dot(e.astype(BF16), v_ref[...], preferred_element_type=F32)
        o_ref[...] = (o / l).astype(o_ref.dtype)
        lse_ref[...] = jnp.broadcast_to(m * MLA_LOG2_SCALE + jnp.log2(l), lse_ref.shape)

    qspec = pl.BlockSpec((tq, HEAD_DIM), lambda h, i: (i, h))
    kspec = pl.BlockSpec((S, HEAD_DIM), lambda h, i: (0, h))
    return pl.pallas_call(
        body, name="mla_fwd", grid=(H, S // tq),
        in_specs=[qspec, qspec, kspec, kspec, pl.BlockSpec((S, LANES), lambda h, i: (0, 0))],
        out_specs=[qspec, qspec],
        out_shape=[jax.ShapeDtypeStruct((S, H * HEAD_DIM), BF16), jax.ShapeDtypeStruct((S, H * LANES), F32)],
        scratch_shapes=[pltpu.VMEM((S, 2 * HEAD_DIM), BF16)],
        compiler_params=_params(("parallel", "arbitrary")),
    )(qn, qr, kn, v, kr)


def mla_bwd(qn, qr, kn, v, kr, lse, do, H, S):
    tq = _pick(S, 256, 16)
    nt = (((1,), (1,)), ((), ()))
    tn_dims = (((0,), (0,)), ((), ()))

    def body(qn_ref, qr_ref, kn_ref, v_ref, kr_ref, lse_ref, do_ref, dqn_ref, dqr_ref, dkn_ref, dv_ref, dkr_ref, kcat):
        h, i = pl.program_id(0), pl.program_id(1)
        _mla_keys(kn_ref, kr_ref, kcat)
        qcat, s = _mla_scores(qn_ref, qr_ref, kcat)
        p = jnp.exp2(s * MLA_LOG2_SCALE - lse_ref[:, 0:1])
        dob = do_ref[...].astype(BF16)
        dp = lax.dot_general(dob, v_ref[...], nt, preferred_element_type=F32)
        ds = p * (dp - jnp.sum(dp * p, axis=-1, keepdims=True))
        dsb = (ds * (MLA_QK ** -0.5)).astype(BF16)
        dq = jnp.dot(dsb, kcat[...], preferred_element_type=F32)
        dqn_ref[...] = dq[:, :HEAD_DIM].astype(dqn_ref.dtype)
        dqr_ref[...] = dq[:, HEAD_DIM:].astype(dqr_ref.dtype)

        @pl.when(i == 0)
        def _():
            dkn_ref[...] = jnp.zeros_like(dkn_ref)
            dv_ref[...] = jnp.zeros_like(dv_ref)

        @pl.when(jnp.logical_and(i == 0, h == 0))
        def _():
            dkr_ref[...] = jnp.zeros_like(dkr_ref)

        dk = lax.dot_general(dsb, qcat, tn_dims, preferred_element_type=F32)
        dkn_ref[...] += dk[:, :HEAD_DIM]
        dkr_ref[...] += dk[:, HEAD_DIM:]
        dv_ref[...] += lax.dot_general(p.astype(BF16), dob, tn_dims, preferred_element_type=F32)

    qspec = pl.BlockSpec((tq, HEAD_DIM), lambda h, i: (i, h))
    kspec = pl.BlockSpec((S, HEAD_DIM), lambda h, i: (0, h))
    rspec = pl.BlockSpec((S, LANES), lambda h, i: (0, 0))
    W = H * HEAD_DIM
    return pl.pallas_call(
        body, name="mla_bwd", grid=(H, S // tq),
        in_specs=[qspec, qspec, kspec, kspec, rspec, qspec, qspec],
        out_specs=[qspec, qspec, kspec, kspec, rspec],
        out_shape=[jax.ShapeDtypeStruct((S, W), BF16), jax.ShapeDtypeStruct((S, W), F32),
                   jax.ShapeDtypeStruct((S, W), F32), jax.ShapeDtypeStruct((S, W), F32),
                   jax.ShapeDtypeStruct((S, LANES), F32)],
        scratch_shapes=[pltpu.VMEM((S, 2 * HEAD_DIM), BF16)],
        compiler_params=_params(("arbitrary", "arbitrary")),
    )(qn, qr, kn, v, kr, lse, do)


def _place():
    return lax.axis_index("x"), lax.axis_index("y"), lax.axis_index("c")


def _other_chips(x, y):
    return [(1 - x, y), (x, 1 - y), (1 - x, 1 - y)]


def _remote(src, dst, send_sem, recv_sem, to):
    return pltpu.make_async_remote_copy(src_ref=src, dst_ref=dst, send_sem=send_sem, recv_sem=recv_sem,
                                        device_id=to, device_id_type=MESH)


HBM = pl.BlockSpec(memory_space=pltpu.HBM)
SEM = pl.BlockSpec(memory_space=pltpu.SEMAPHORE)
EFFECT = pltpu.SideEffectType.DATAFLOW_SIDE_EFFECTING


def _in_hbm(a):
    return pltpu.with_memory_space_constraint(a, pltpu.HBM)


TOKEN = jax.ShapeDtypeStruct((8, LANES), F32)
IN_VMEM = pl.BlockSpec(memory_space=pltpu.VMEM)


def gather_start(shards, landings, after, tag):
    n = len(shards)

    def body(*refs):
        ins, lands = refs[:n], refs[n:2 * n]
        send, recv = refs[2 * n + len(after)], refs[2 * n + len(after) + 1]
        token = refs[-1]
        x, y, c = _place()
        me = 2 * x + y
        for w in range(n):
            for k, (px, py) in enumerate(_other_chips(x, y)):
                _remote(ins[w].at[c], lands[w].at[me, c], send.at[3 * w + k], recv.at[3 * w + k], (px, py, c)).start()
        token[...] = jnp.zeros_like(token)

    bufs = list(shards) + list(landings)
    outs = pl.pallas_call(
        body, name="gather_start_" + tag,
        out_shape=(pltpu.SemaphoreType.DMA((3 * n,)),) * 2 + tuple(pltpu.HBM(b.shape, b.dtype) for b in bufs) + (TOKEN,),
        in_specs=[HBM] * (2 * n) + [ANY] * len(after), out_specs=tuple([SEM, SEM] + [HBM] * (2 * n) + [IN_VMEM]),
        input_output_aliases={i: 2 + i for i in range(2 * n)},
        compiler_params=pltpu.CompilerParams(has_side_effects=EFFECT),
    )(*[_in_hbm(b) for b in bufs], *after)
    return (outs[0], outs[1]), outs[2:2 + n], outs[2 + n:2 + 2 * n], outs[-1]


def gather_wait(sems, shards, landings, after, tag):
    n = len(shards)
    send, recv = sems

    def body(*refs):
        ins, lands = refs[:n], refs[n:2 * n]
        send_sem, recv_sem = refs[2 * n], refs[2 * n + 1]
        x, y, c = _place()
        me = 2 * x + y
        for w in range(n):
            for k, (px, py) in enumerate(_other_chips(x, y)):
                cp = _remote(ins[w].at[c], lands[w].at[2 * px + py, c], send_sem.at[3 * w + k], recv_sem.at[3 * w + k],
                             (px, py, c))
                cp.wait_send()
                cp.wait_recv()

    bufs = list(shards) + list(landings)
    outs = pl.pallas_call(
        body, name="gather_wait_" + tag, out_shape=tuple(pltpu.HBM(b.shape, b.dtype) for b in bufs),
        in_specs=[HBM] * (2 * n) + [SEM, SEM] + [ANY] * len(after), out_specs=tuple([HBM] * (2 * n)),
        input_output_aliases={i: i for i in range(2 * n)},
        compiler_params=pltpu.CompilerParams(has_side_effects=EFFECT),
    )(*bufs, send, recv, *after)
    return outs[:n], outs[n:]


def gather_forward(landings, tag):
    n = len(landings)

    def body(*refs):
        ins, outs = refs[:n], refs[n:2 * n]
        send, recv = refs[2 * n:]
        x, y, c = _place()
        sibling = (x, y, 1 - c)
        cps = []
        for w in range(n):
            for k, (px, py) in enumerate(_other_chips(x, y)):
                j = 2 * px + py
                cp = _remote(ins[w].at[j, c], outs[w].at[j, c], send.at[3 * w + k], recv.at[3 * w + k], sibling)
                cp.start()
                cps.append(cp)
        for w in range(n):
            for k, (px, py) in enumerate(_other_chips(x, y)):
                blk = outs[w].at[2 * px + py, 1 - c]
                _remote(blk, blk, send.at[3 * w + k], recv.at[3 * w + k], sibling).wait_recv()
        for cp in cps:
            cp.wait_send()

    return pl.pallas_call(
        body, name="gather_forward_" + tag, in_specs=[ANY] * n, out_specs=[ANY] * n,
        out_shape=[jax.ShapeDtypeStruct(a.shape, a.dtype) for a in landings],
        input_output_aliases={i: i for i in range(n)},
        scratch_shapes=[pltpu.SemaphoreType.DMA((3 * n,)), pltpu.SemaphoreType.DMA((3 * n,))],
    )(*landings)


def pair_start(grads, landings, tag):
    n = len(grads)

    def body(*refs):
        ins, lands = refs[:n], refs[n:2 * n]
        send, recv = refs[2 * n], refs[2 * n + 1]
        token = refs[-1]
        x, y, c = _place()
        for w in range(n):
            _remote(ins[w].at[:, 1 - c], lands[w], send.at[w], recv.at[w], (x, y, 1 - c)).start()
        token[...] = jnp.zeros_like(token)

    bufs = list(grads) + list(landings)
    outs = pl.pallas_call(
        body, name="pair_start_" + tag,
        out_shape=(pltpu.SemaphoreType.DMA((n,)),) * 2 + tuple(pltpu.HBM(b.shape, b.dtype) for b in bufs) + (TOKEN,),
        in_specs=[HBM] * (2 * n), out_specs=tuple([SEM, SEM] + [HBM] * (2 * n) + [IN_VMEM]),
        input_output_aliases={i: 2 + i for i in range(2 * n)},
        compiler_params=pltpu.CompilerParams(has_side_effects=EFFECT),
    )(*[_in_hbm(b) for b in bufs])
    return (outs[0], outs[1]), outs[2:2 + n], outs[2 + n:2 + 2 * n], outs[-1]


def pair_wait(sems, grads, landings, after, tag):
    n = len(grads)

    def body(*refs):
        ins, lands = refs[:n], refs[n:2 * n]
        send, recv = refs[2 * n], refs[2 * n + 1]
        x, y, c = _place()
        for w in range(n):
            cp = _remote(ins[w].at[:, 1 - c], lands[w], send.at[w], recv.at[w], (x, y, 1 - c))
            cp.wait_send()
            cp.wait_recv()

    bufs = list(grads) + list(landings)
    outs = pl.pallas_call(
        body, name="pair_wait_" + tag, out_shape=tuple(pltpu.HBM(b.shape, b.dtype) for b in bufs),
        in_specs=[HBM] * (2 * n) + [SEM, SEM] + [ANY] * len(after), out_specs=tuple([HBM] * (2 * n)),
        input_output_aliases={i: i for i in range(2 * n)},
        compiler_params=pltpu.CompilerParams(has_side_effects=EFFECT),
    )(*bufs, sems[0], sems[1], *after)
    return outs[:n], outs[n:]


def scatter_start(sums, landings, tag):
    n = len(sums)

    def body(*refs):
        ins, lands = refs[:n], refs[n:2 * n]
        send, recv = refs[2 * n], refs[2 * n + 1]
        token = refs[-1]
        x, y, c = _place()
        for w in range(n):
            for k, (px, py) in enumerate(_other_chips(x, y)):
                _remote(ins[w].at[2 * px + py], lands[w].at[k], send.at[3 * w + k], recv.at[3 * w + k], (px, py, c)).start()
        token[...] = jnp.zeros_like(token)

    bufs = list(sums) + list(landings)
    outs = pl.pallas_call(
        body, name="scatter_start_" + tag,
        out_shape=(pltpu.SemaphoreType.DMA((3 * n,)),) * 2 + tuple(pltpu.HBM(b.shape, b.dtype) for b in bufs) + (TOKEN,),
        in_specs=[HBM] * (2 * n), out_specs=tuple([SEM, SEM] + [HBM] * (2 * n) + [IN_VMEM]),
        input_output_aliases={i: 2 + i for i in range(2 * n)},
        compiler_params=pltpu.CompilerParams(has_side_effects=EFFECT),
    )(*[_in_hbm(b) for b in bufs])
    return (outs[0], outs[1]), outs[2:2 + n], outs[2 + n:2 + 2 * n], outs[-1]


def scatter_wait(sems, sums, landings, after, tag):
    n = len(sums)

    def body(*refs):
        ins, lands = refs[:n], refs[n:2 * n]
        send, recv = refs[2 * n], refs[2 * n + 1]
        x, y, c = _place()
        for w in range(n):
            for k, (px, py) in enumerate(_other_chips(x, y)):
                cp = _remote(ins[w].at[2 * px + py], lands[w].at[k], send.at[3 * w + k], recv.at[3 * w + k], (px, py, c))
                cp.wait_send()
                cp.wait_recv()

    bufs = list(sums) + list(landings)
    outs = pl.pallas_call(
        body, name="scatter_wait_" + tag, out_shape=tuple(pltpu.HBM(b.shape, b.dtype) for b in bufs),
        in_specs=[HBM] * (2 * n) + [SEM, SEM] + [ANY] * len(after), out_specs=tuple([HBM] * (2 * n)),
        input_output_aliases={i: i for i in range(2 * n)},
        compiler_params=pltpu.CompilerParams(has_side_effects=EFFECT),
    )(*bufs, sems[0], sems[1], *after)
    return outs[:n], outs[n:]


def half_exchange(halves, tag):
    n = len(halves)

    def body(*refs):
        ins, outs = refs[:n], refs[n:2 * n]
        send, recv = refs[2 * n:]
        x, y, c = _place()
        cps = []
        for w in range(n):
            cp = _remote(ins[w], outs[w], send.at[w], recv.at[w], (x, y, 1 - c))
            cp.start()
            cps.append(cp)
        for cp in cps:
            cp.wait()

    return pl.pallas_call(
        body, name="grad_half_exchange_" + tag, in_specs=[ANY] * n, out_specs=[ANY] * n,
        out_shape=[jax.ShapeDtypeStruct(h.shape, h.dtype) for h in halves],
        scratch_shapes=[pltpu.SemaphoreType.DMA((n,)), pltpu.SemaphoreType.DMA((n,))],
    )(*halves)


def gather_small(v):
    def body(v_ref, o_ref, send, recv, local):
        x, y, c = _place()
        me = 4 * x + 2 * y + c
        own = pltpu.make_async_copy(v_ref, o_ref.at[me], local)
        own.start()
        cps = []
        for k in range(1, 8):
            fx, fy, fc = (k >> 2) & 1, (k >> 1) & 1, k & 1
            to = (x ^ fx if fx else x, y ^ fy if fy else y, c ^ fc if fc else c)
            cp = _remote(v_ref, o_ref.at[me], send.at[k - 1], recv.at[k - 1], to)
            cp.start()
            cps.append(cp)
        for k in range(1, 8):
            fx, fy, fc = (k >> 2) & 1, (k >> 1) & 1, k & 1
            px, py, pc = (x ^ fx if fx else x, y ^ fy if fy else y, c ^ fc if fc else c)
            cps[k - 1].wait_send()
            _remote(v_ref, o_ref.at[4 * px + 2 * py + pc], send.at[k - 1], recv.at[k - 1], (px, py, pc)).wait_recv()
        own.wait()

    return pl.pallas_call(
        body, name="gather_small_grads", in_specs=[ANY], out_specs=ANY,
        out_shape=jax.ShapeDtypeStruct((8,) + v.shape, v.dtype),
        scratch_shapes=[pltpu.SemaphoreType.DMA((7,)), pltpu.SemaphoreType.DMA((7,)), pltpu.SemaphoreType.DMA],
    )(v)


def _row_tile(rows, cols, nbuf_bytes):
    tm = _pick(rows, 512, 16)
    while tm * cols * nbuf_bytes * 2 > VMEM_BUDGET_V7X and tm % 32 == 0:
        tm //= 2
    return tm


def pair_sum(g, r, c_idx, tag):
    _, _, rows, cols = g.shape
    tm = _row_tile(rows, cols, 2 + 2 + 2)
    nb = rows // tm

    def body(c_ref, g_ref, r_ref, o_ref):
        o_ref[...] = (g_ref[...].astype(F32) + r_ref[...].astype(F32)).astype(o_ref.dtype)

    gs = pltpu.PrefetchScalarGridSpec(
        num_scalar_prefetch=1, grid=(N_CHIPS, nb),
        in_specs=[pl.BlockSpec((None, None, tm, cols), lambda j, i, c_ref: (j, c_ref[0], i, 0)),
                  pl.BlockSpec((None, tm, cols), lambda j, i, c_ref: (j, i, 0))],
        out_specs=pl.BlockSpec((None, tm, cols), lambda j, i, c_ref: (j, i, 0)))
    return pl.pallas_call(body, name="grad_pair_sum_" + tag, grid_spec=gs,
                          out_shape=jax.ShapeDtypeStruct(r.shape, BF16),
                          compiler_params=_params(("arbitrary", "arbitrary")))(c_idx, g, r)


def chip_sum(s, r, j_idx, tag):
    _, rows, cols = s.shape
    tm = _row_tile(rows, cols, 2 + 3 * 2 + 4)
    nb = rows // tm

    def body(j_ref, s_ref, r_ref, o_ref):
        t = s_ref[...].astype(F32)
        for k in range(3):
            t = t + r_ref[k].astype(F32)
        o_ref[...] = t

    gs = pltpu.PrefetchScalarGridSpec(
        num_scalar_prefetch=1, grid=(nb,),
        in_specs=[pl.BlockSpec((None, tm, cols), lambda i, j_ref: (j_ref[0], i, 0)),
                  pl.BlockSpec((3, tm, cols), lambda i, j_ref: (0, i, 0))],
        out_specs=pl.BlockSpec((tm, cols), lambda i, j_ref: (i, 0)))
    return pl.pallas_call(body, name="grad_chip_sum_" + tag, grid_spec=gs,
                          out_shape=jax.ShapeDtypeStruct((rows, cols), F32),
                          compiler_params=_params(("arbitrary",)))(j_idx, s, r)


def adamw(w, g, m, v, *, name):
    rows, cols = w.shape
    tm = _row_tile(rows, cols, 7 * 4)

    return rowwise(_adamw_math, [w, g, m, v], [], [(cols, F32)] * 3, tm=tm, name=name)


def _adamw_math(wb, gb, mb, vb):
    m2 = ADAM_B1 * mb + (1.0 - ADAM_B1) * gb
    v2 = ADAM_B2 * vb + (1.0 - ADAM_B2) * (gb * gb)
    m_hat = m2 / (1.0 - ADAM_B1 ** ADAM_STEP)
    v_hat = v2 / (1.0 - ADAM_B2 ** ADAM_STEP)
    delta = -ADAM_LR * (m_hat / (jnp.sqrt(v_hat) + ADAM_EPS) + ADAM_WD * wb)
    return delta, m2, v2


def adamw_shard(w, g_own, g_sib, m, v, c_idx, *, name):
    rows, cols = g_own.shape
    tm = _row_tile(rows, cols, 9 * 4)
    nb = rows // tm

    def body(c_ref, w_ref, go_ref, gs_ref, m_ref, v_ref, g_out, d_out, m_out, v_out):
        gb = jnp.where(pl.program_id(0) == c_ref[0], go_ref[...], gs_ref[...])
        delta, m2, v2 = _adamw_math(w_ref[...], gb, m_ref[...], v_ref[...])
        g_out[...] = gb
        d_out[...] = delta
        m_out[...] = m2
        v_out[...] = v2

    full = pl.BlockSpec((None, tm, cols), lambda h, i, c_ref: (0, h * nb + i, 0))
    own = pl.BlockSpec((tm, cols), lambda h, i, c_ref: (jnp.where(h == c_ref[0], i, 0), 0))
    sib = pl.BlockSpec((tm, cols), lambda h, i, c_ref: (jnp.where(h == c_ref[0], 0, i), 0))
    gs = pltpu.PrefetchScalarGridSpec(num_scalar_prefetch=1, grid=(2, nb), in_specs=[full, own, sib, full, full],
                                      out_specs=[full] * 4)
    return pl.pallas_call(body, name=name, grid_spec=gs, out_shape=[jax.ShapeDtypeStruct(w.shape, F32)] * 4,
                          compiler_params=_params(("arbitrary", "arbitrary")))(c_idx, w, g_own, g_sib, m, v)


def sum_devices(a):
    def body(a_ref, o_ref):
        t = a_ref[0]
        for k in range(1, 8):
            t = t + a_ref[k]
        o_ref[...] = t

    return pl.pallas_call(body, name="sum_small_grads", out_shape=jax.ShapeDtypeStruct(a.shape[1:], a.dtype))(a)


def _halves(w2d):
    r, c = w2d.shape
    return w2d.reshape(2, r // 2, c)


def kernel(x, p, ffn1_norm, ffn1_w_gate, ffn1_w_up, ffn1_w_down, mix_norm, w_in, q_a_norm, w_uq, kv_a_norm, w_ukv, na_rpb, w_branch_a, w_branch_b, w_out, ffn2_norm, ffn2_w_gate, ffn2_w_up, ffn2_w_down, pl_norm, w_pl, w_pl_gate, final_norm, loss_target, m_ffn1_norm, m_ffn1_w_gate, m_ffn1_w_up, m_ffn1_w_down, m_mix_norm, m_w_in, m_q_a_norm, m_w_uq, m_kv_a_norm, m_w_ukv, m_na_rpb, m_w_branch_a, m_w_branch_b, m_w_out, m_ffn2_norm, m_ffn2_w_gate, m_ffn2_w_up, m_ffn2_w_down, m_pl_norm, m_w_pl, m_w_pl_gate, m_final_norm, v_ffn1_norm, v_ffn1_w_gate, v_ffn1_w_up, v_ffn1_w_down, v_mix_norm, v_w_in, v_q_a_norm, v_w_uq, v_kv_a_norm, v_w_ukv, v_na_rpb, v_w_branch_a, v_w_branch_b, v_w_out, v_ffn2_norm, v_ffn2_w_gate, v_ffn2_w_up, v_ffn2_w_down, v_pl_norm, v_w_pl, v_w_pl_gate, v_final_norm):
    big = ["ffn1_w_gate", "ffn1_w_up", "ffn1_w_down", "w_in", "w_uq", "w_ukv", "w_branch_a", "w_branch_b", "w_out",
           "ffn2_w_gate", "ffn2_w_up", "ffn2_w_down", "w_pl", "w_pl_gate"]
    col_sharded = {"ffn1_w_gate", "ffn1_w_up", "w_in", "w_uq", "w_ukv", "w_branch_a", "w_branch_b", "ffn2_w_gate",
                   "ffn2_w_up", "w_pl"}
    small = ["ffn1_norm", "mix_norm", "q_a_norm", "kv_a_norm", "na_rpb", "ffn2_norm", "pl_norm", "final_norm"]
    order = ["ffn1_norm", "ffn1_w_gate", "ffn1_w_up", "ffn1_w_down", "mix_norm", "w_in", "q_a_norm", "w_uq",
             "kv_a_norm", "w_ukv", "na_rpb", "w_branch_a", "w_branch_b", "w_out", "ffn2_norm", "ffn2_w_gate",
             "ffn2_w_up", "ffn2_w_down", "pl_norm", "w_pl", "w_pl_gate", "final_norm"]
    env = dict(locals())
    W = {n: env[n] for n in order}
    Mo = {n: env["m_" + n] for n in order}
    Vo = {n: env["v_" + n] for n in order}

    xs = x[0]
    S, D = xs.shape
    tgt = loss_target[0]
    ps = p[0, 0]
    NAW = w_branch_a.shape[1]
    MLAW = w_branch_b.shape[1]
    NH, MH = NAW // HEAD_DIM, MLAW // HEAD_DIM
    QR, KVR = w_uq.shape[1], w_ukv.shape[1]
    F = ffn1_w_down.shape[1] * N_CHIPS
    cx, cy, cc = _place()
    c_idx = jnp.reshape(cc, (1,)).astype(jnp.int32)
    j_idx = jnp.reshape(2 * cx + cy, (1,)).astype(jnp.int32)

    me_chip = 2 * cx + cy
    groups = [["ffn1_w_gate"], ["ffn1_w_up"], ["ffn1_w_down"], ["w_in"],
              ["w_uq", "w_ukv", "w_branch_a", "w_branch_b", "w_out"],
              ["ffn2_w_gate", "ffn2_w_up", "ffn2_w_down"], ["w_pl", "w_pl_gate"]]
    started, tokens = [], []
    for g, members in enumerate(groups):
        shards = [_halves(W[n][0].astype(BF16)) for n in members]
        landings = [lax.empty((N_CHIPS,) + s.shape, BF16) for s in shards]
        sems, shards_thru, landings_thru, token = gather_start(shards, landings, tokens[-1:], str(g))
        started.append((sems, shards_thru, landings_thru))
        tokens.append(token)
    gathered = {}

    def arrive(n, after):
        g = [n in members for members in groups].index(True)
        sems, shards_thru, landings_thru = started[g]
        after = list(after) if isinstance(after, (list, tuple)) else [after]
        shards_out, landed = gather_wait(sems, shards_thru, landings_thru, after, str(g))
        for name, full, own in zip(groups[g], gather_forward(landed, str(g)), shards_out):
            gathered[name] = lax.dynamic_update_slice(full, own[None], (me_chip, 0, 0, 0))

    def stacked(n, after=None):
        if n not in gathered:
            arrive(n, after)
        g = gathered[n]
        return g.reshape(N_CHIPS, 2 * g.shape[2], g.shape[3])

    def plain(n, after=None):
        if n in col_sharded:
            st = stacked(n, after)
            return st.transpose(1, 0, 2).reshape(st.shape[1], N_CHIPS * st.shape[2])
        if n not in gathered:
            arrive(n, after)
        g = gathered[n]
        return g.reshape(N_CHIPS * 2 * g.shape[2], g.shape[3])

    n_na = 3 * NAW
    n_front = n_na + QR + KVR
    n_in = n_front + MLA_ROPE + 2 * D
    off_ql, off_kvl, off_kr = 2 * D, 2 * D + QR, 2 * D + QR + KVR
    kr_w = 2 * LANES
    rest_w = off_kr + kr_w
    rest_ranges = [(n_front + MLA_ROPE, n_in), (n_na, n_front), (n_front, n_front + MLA_ROPE)]

    def shard_cols(st, lo, hi):
        nb, parts = st.shape[2], []
        while lo < hi:
            j = lo // nb
            end = min(hi, (j + 1) * nb)
            parts.append(st[j][:, lo - j * nb:end - j * nb])
            lo = end
        return parts

    def w_in_shards(g_na, g_rest):
        pieces = [(0, n_na, g_na, 0)]
        o = 0
        for lo, hi in rest_ranges:
            pieces.append((lo, hi, g_rest, o))
            o += hi - lo
        nb, shards = n_in // N_CHIPS, []
        for j in range(N_CHIPS):
            parts = []
            for lo, hi, src, o in sorted(pieces):
                a, b = max(lo, j * nb), min(hi, (j + 1) * nb)
                if a < b:
                    parts.append(src[:, o + a - lo:o + b - lo])
            shards.append(jnp.concatenate(parts, axis=1))
        return jnp.stack(shards)

    pos = jnp.arange(S, dtype=F32)
    inv_freq = 1.0 / (ROPE_THETA ** (jnp.arange(0, MLA_ROPE, 2, dtype=F32) / MLA_ROPE))
    ang = pos[:, None] * inv_freq[None, :]
    zpad = jnp.zeros((S, LANES - MLA_ROPE), F32)
    cos_t = jnp.concatenate([jnp.cos(ang), jnp.cos(ang), zpad], axis=1)
    sin_t = jnp.concatenate([-jnp.sin(ang), jnp.sin(ang), zpad], axis=1)

    def ffn_fwd(h, norm_g, tag, pre, after=()):
        n = norm_fwd(h, norm_g, name=f"{tag}_norm")
        g = mm(n, stacked(pre + "_w_gate", [n, *after]), name=f"{tag}_gate", b_stack=True)
        u, a = mm(n, stacked(pre + "_w_up", g), name=f"{tag}_up", b_stack=True, epilogue=swiglu_tile, epi_in=[g],
                  epi_out=[F32, BF16])
        h_out = mm(a, plain(pre + "_w_down", a), name=f"{tag}_down", res=h, alpha=0.5)
        return h_out, (n, g, u, a)

    def ffn_bwd(h, norm_g, saved, dh, dh_half, tag, pre, last, after=()):
        n, g, u, a = saved
        G[pre + "_w_down"] = mm(a, dh_half, name=f"{tag}_dw_down", ta=True, out_dtype=BF16, after=after)
        begun_d, token_d = pair_begin([pre + "_w_down"], tag + "_d")
        dg, du = mm(dh_half, plain(pre + "_w_down"), name=f"{tag}_da", tb=True, after=[token_d],
                    epilogue=swiglu_bwd_tile, epi_in=[g, u], epi_out=[BF16, BF16])
        G[pre + "_w_gate"] = mm(n, dg, name=f"{tag}_dw_gate", ta=True, out_dtype=BF16, out_stack=True)
        G[pre + "_w_up"] = mm(n, du, name=f"{tag}_dw_up", ta=True, out_dtype=BF16, out_stack=True)
        begun_gu, token_gu = pair_begin([pre + "_w_gate", pre + "_w_up"], tag + "_gu")
        dn = mm(dg, stacked(pre + "_w_gate"), name=f"{tag}_dn_gate", tb=True, b_stack=True, after=[token_gu])
        token = reduce_go([begun_d, begun_gu], tag, [dn])
        dn = mm(du, stacked(pre + "_w_up"), name=f"{tag}_dn_up", tb=True, b_stack=True, res=dn, after=[token])
        return norm_bwd(h, norm_g, dn, name=f"{tag}_dnorm", res=dh, bf16_alpha=None if last else 1.0)

    bias = na_bias(na_rpb[0], after=tokens[-1:])
    h1, ffn1_saved = ffn_fwd(xs, ffn1_norm, "ffn1", "ffn1", after=[bias, tokens[-1]])
    u_mix = norm_fwd(h1, mix_norm, name="mix_norm")
    win_st = stacked("w_in", u_mix)
    w_na = jnp.concatenate(shard_cols(win_st, 0, n_na), axis=1)
    w_rest = jnp.concatenate([p_ for lo, hi in rest_ranges for p_ in shard_cols(win_st, lo, hi)]
                             + [jnp.zeros((D, kr_w - MLA_ROPE), BF16)], axis=1)
    z_na = mm(u_mix, w_na, name="mix_in_na", out_dtype=BF16)
    z = mm(u_mix, w_rest, name="mix_in_rest")
    o_a = na_fwd(z_na, bias, NH, S)
    c_q = norm_fwd((z, QR, off_ql // QR), q_a_norm, name="q_a_norm")
    c_kv = norm_fwd((z, KVR, off_kvl // KVR), kv_a_norm, name="kv_a_norm")
    wuq = plain("w_uq", c_kv).reshape(QR, MH, MLA_QK)
    wuq_n = wuq[:, :, :MLA_NOPE].reshape(QR, MH * MLA_NOPE)
    wuq_r = jnp.pad(wuq[:, :, MLA_NOPE:], ((0, 0), (0, 0), (0, LANES - MLA_ROPE))).reshape(QR, MH * LANES)
    wukv = plain("w_ukv").reshape(KVR, MH, 2, HEAD_DIM)
    wuk = wukv[:, :, 0].reshape(KVR, MH * HEAD_DIM)
    wuv = wukv[:, :, 1].reshape(KVR, MH * HEAD_DIM)
    q_n = mm(c_q, wuq_n, name="mla_q_nope", out_dtype=BF16)
    q_r = rope(mm(c_q, wuq_r, name="mla_q_rope"), cos_t, sin_t, name="rope_q", out_dtype=BF16)
    k_n = mm(c_kv, wuk, name="mla_k_nope", out_dtype=BF16)
    v_m = mm(c_kv, wuv, name="mla_v", out_dtype=BF16)
    k_r = rope((z, LANES, off_kr // LANES), cos_t, sin_t, name="rope_k", out_dtype=BF16)
    o_b, lse = mla_fwd(q_n, q_r, k_n, v_m, k_r, MH, S)
    y_a = mm(o_a, stacked("w_branch_a"), name="branch_a", b_stack=True)
    y_b = mm(o_b, stacked("w_branch_b"), name="branch_b", b_stack=True)
    z_ga, z_gb = (z, D, 0), (z, D, 1)
    merged = rowwise(lambda ga, gb, ya, yb: _sig(ga) * ya + _sig(gb) * yb, [z_ga, z_gb, y_a, y_b], [], [(D, BF16)],
                     tm=256, name="merge")[0]
    h2 = mm(merged, plain("w_out"), name="mix_out", res=h1)
    h3, ffn2_saved = ffn_fwd(h2, ffn2_norm, "ffn2", "ffn2")
    n4 = norm_fwd(h3, pl_norm, name="pl_norm")
    pg_pre = mm(n4, plain("w_pl_gate", n4), name="pl_gate")
    pe = mm(ps, stacked("w_pl"), name="pl_embed", b_stack=True)

    def tail(h3b, pgb, peb, tb_, fg):
        pg = _sig(pgb)
        h4 = h3b + pg * peb
        r = _rstd(h4)
        xh = h4 * r
        err = xh * fg - tb_
        loss_rows = jnp.mean(err * err, axis=-1, keepdims=True)
        dy = err * (1.0 / D)
        dxh = dy * fg
        dh4 = r * (dxh - xh * jnp.mean(dxh * xh, axis=-1, keepdims=True))
        loss_part = jnp.broadcast_to(0.5 * jnp.sum(loss_rows, axis=0, keepdims=True), (1, LANES))
        return (dh4, dh4 * peb * pg * (1.0 - pg), dh4 * pg, loss_part, jnp.sum(dy * xh, axis=0, keepdims=True))

    dh4, dpg_pre, dpe, loss_part, g_final = rowwise(
        tail, [h3, pg_pre, pe, tgt], [final_norm.reshape(1, D)], [(D, F32), (D, BF16), (D, BF16)],
        accs=[(1, LANES), (1, D)], tm=128, name="loss_tail")
    loss = lax.psum(loss_part[0, 0], ("x", "y", "c"))

    G = {}
    pending = []

    def four(g):
        if g.ndim == 2:
            return g.reshape(N_CHIPS, 2, g.shape[0] // (2 * N_CHIPS), g.shape[1])
        return g.reshape(N_CHIPS, 2, g.shape[1] // 2, g.shape[2])

    def pair_begin(names, tag):
        g4 = [four(G[n]) for n in names]
        lands = [lax.empty((N_CHIPS,) + g.shape[2:], BF16) for g in g4]
        sems, g4, lands, token = pair_start(g4, lands, tag)
        return (names, tag, sems, g4, lands), token

    def reduce_go(begun, tag, after):
        names, sums = [], []
        for b_names, b_tag, sems, g4, lands in begun:
            g4, got = pair_wait(sems, g4, lands, list(after), b_tag)
            sums += [pair_sum(a_, r_, c_idx, n) for n, a_, r_ in zip(b_names, g4, got)]
            names += b_names
        lands = [lax.empty((N_CHIPS - 1,) + s_.shape[1:], BF16) for s_ in sums]
        sems, sums, lands, token = scatter_start(sums, lands, tag)
        pending.append((names, tag, sems, sums, lands))
        return token

    def reduce_finish(entry, after):
        names, tag, sems, sums, lands = entry
        sums, got = scatter_wait(sems, sums, lands, after, tag)
        halves = [chip_sum(a, b, j_idx, n) for n, a, b in zip(names, sums, got)]
        done = []
        for n, own, sib in zip(names, halves, half_exchange(halves, tag)):
            grads[n], delta[n], new_m[n], new_v[n] = adamw_shard(W[n], own, sib, Mo[n], Vo[n], c_idx, name="adamw_" + n)
            done.append(grads[n])
        return done

    G["w_pl"] = mm(ps, dpe, name="pl_dw_embed", ta=True, out_dtype=BF16, out_stack=True)
    G["w_pl_gate"] = mm(n4, dpg_pre, name="pl_dw_gate", ta=True, out_dtype=BF16)
    begun_pl, token = pair_begin(["w_pl", "w_pl_gate"], "pl")
    dn4 = mm(dpg_pre, plain("w_pl_gate"), name="pl_dn", tb=True, after=[token])
    dh3, dh3_half, g_pl = norm_bwd(h3, pl_norm, dn4, name="pl_dnorm", res=dh4, bf16_alpha=0.5)
    token = reduce_go([begun_pl], "pl", [dh3])
    dh2, dh2_b, g_ffn2 = ffn_bwd(h2, ffn2_norm, ffn2_saved, dh3, dh3_half, "ffn2", "ffn2", last=False, after=[token])

    G["w_out"] = mm(merged, dh2_b, name="mix_dw_out", ta=True, out_dtype=BF16)
    dmerged = mm(dh2_b, plain("w_out"), name="mix_dmerged", tb=True)

    def merge_bwd(ga, gb, ya, yb, dm):
        sa, sb = _sig(ga), _sig(gb)
        dgates = jnp.concatenate([dm * ya * sa * (1.0 - sa), dm * yb * sb * (1.0 - sb)], axis=1)
        return dm * sa, dm * sb, dgates

    dy_a, dy_b, dz_rest = rowwise(merge_bwd, [z_ga, z_gb, y_a, y_b, dmerged], [],
                                  [(D, BF16), (D, BF16), (2 * D, BF16, 0)], tm=256, name="merge_bwd",
                                  into=(None, rest_w))
    G["w_branch_a"] = mm(o_a, dy_a, name="branch_a_dw", ta=True, out_dtype=BF16, out_stack=True)
    G["w_branch_b"] = mm(o_b, dy_b, name="branch_b_dw", ta=True, out_dtype=BF16, out_stack=True)
    do_a = mm(dy_a, stacked("w_branch_a"), name="branch_a_dx", tb=True, b_stack=True)
    do_b = mm(dy_b, stacked("w_branch_b"), name="branch_b_dx", tb=True, b_stack=True)
    dq_na, dk_na, dv_na, dbias = na_bwd(z_na, bias, do_a, NH, S)
    g_rpb = na_rpb_grad(dbias)
    dq_n, dq_rr, dk_n, dv_m, dk_rr = mla_bwd(q_n, q_r, k_n, v_m, k_r, lse, do_b, MH, S)
    dq_r = rope(dq_rr, cos_t, -sin_t, name="rope_q_bwd", out_dtype=BF16)
    dz_rest = rope(dk_rr, cos_t, -sin_t, name="rope_k_bwd", out_dtype=BF16, into=(dz_rest, rest_w),
                   cb=off_kr // kr_w, zero_cols=kr_w - LANES)
    gw_uq_n = mm(c_q, dq_n, name="mla_dw_q_nope", ta=True, out_dtype=BF16)
    gw_uq_r = mm(c_q, dq_r, name="mla_dw_q_rope", ta=True, out_dtype=BF16)
    dc_q = mm(dq_n, wuq_n, name="mla_dcq_nope", tb=True)
    dc_q = mm(dq_r, wuq_r, name="mla_dcq_rope", tb=True, res=dc_q)
    gw_uk = mm(c_kv, dk_n, name="mla_dw_k", ta=True, out_dtype=BF16)
    gw_uv = mm(c_kv, dv_m, name="mla_dw_v", ta=True, out_dtype=BF16)
    dc_kv = mm(dk_n, wuk, name="mla_dckv_k", tb=True)
    dc_kv = mm(dv_m, wuv, name="mla_dckv_v", tb=True, res=dc_kv)
    dz_rest, g_qa = norm_bwd((z, QR, off_ql // QR), q_a_norm, dc_q, name="q_a_dnorm", want_f32=False, bf16_alpha=1.0,
                             into=(dz_rest, rest_w), cb=off_ql // QR)
    dz_rest, g_kva = norm_bwd((z, KVR, off_kvl // KVR), kv_a_norm, dc_kv, name="kv_a_dnorm", want_f32=False,
                              bf16_alpha=1.0, into=(dz_rest, rest_w), cb=off_kvl // KVR)
    dz_na = jnp.concatenate([dq_na, dk_na, dv_na], axis=1)

    def to_stack(g2d):
        k, n = g2d.shape
        return g2d.reshape(k, N_CHIPS, n // N_CHIPS).transpose(1, 0, 2)

    gw_uq = jnp.concatenate([gw_uq_n.reshape(QR, MH, MLA_NOPE), gw_uq_r.reshape(QR, MH, LANES)[:, :, :MLA_ROPE]],
                            axis=2).reshape(QR, MH * MLA_QK)
    G["w_uq"] = to_stack(gw_uq)
    gw_ukv = jnp.stack([gw_uk.reshape(KVR, MH, HEAD_DIM), gw_uv.reshape(KVR, MH, HEAD_DIM)], axis=2)
    G["w_ukv"] = to_stack(gw_ukv.reshape(KVR, MH * 2 * HEAD_DIM))
    begun_mix, token = pair_begin(["w_out", "w_branch_a", "w_branch_b", "w_uq", "w_ukv"], "mix")
    gw_na = mm(u_mix, dz_na, name="mix_dw_in_na", ta=True, out_dtype=BF16, after=[token])
    gw_rest = mm(u_mix, dz_rest, name="mix_dw_in_rest", ta=True, out_dtype=BF16)
    G["w_in"] = w_in_shards(gw_na, gw_rest)
    begun_win, token_win = pair_begin(["w_in"], "win")
    token_mix = reduce_go([begun_mix], "mix", [gw_rest, token_win])
    du_mix = mm(dz_na, w_na, name="mix_du_na", tb=True, after=[token_mix])
    token_win = reduce_go([begun_win], "win", [du_mix])
    du_mix = mm(dz_rest, w_rest, name="mix_du_rest", tb=True, res=du_mix, after=[token_win])
    dh1, dh1_half, g_mix = norm_bwd(h1, mix_norm, du_mix, name="mix_dnorm", res=dh2, bf16_alpha=0.5)
    grad_x, g_ffn1 = ffn_bwd(xs, ffn1_norm, ffn1_saved, dh1, dh1_half, "ffn1", "ffn1", last=True)

    small_g = {"ffn1_norm": g_ffn1, "mix_norm": g_mix, "q_a_norm": g_qa, "kv_a_norm": g_kva, "na_rpb": g_rpb,
               "ffn2_norm": g_ffn2, "pl_norm": g_pl, "final_norm": g_final}
    sizes = [int(np.prod(W[n].shape)) for n in small]
    total = sum(sizes)
    padded = -(-total // (8 * LANES)) * (8 * LANES)

    def pack(parts):
        flat = jnp.concatenate([jnp.reshape(parts[n], (-1,)).astype(F32) for n in small]
                               + [jnp.zeros((padded - total,), F32)])
        return flat.reshape(padded // LANES, LANES)

    def unpack(a):
        flat, out, o = a.reshape(-1), {}, 0
        for n, sz in zip(small, sizes):
            out[n] = flat[o:o + sz].reshape(W[n].shape)
            o += sz
        return out

    g_small = sum_devices(gather_small(pack(small_g)))
    d_small, m_small, v_small = adamw(pack(W), g_small, pack(Mo), pack(Vo), name="adamw_small")
    grads = unpack(g_small)
    delta, new_m, new_v = unpack(d_small), unpack(m_small), unpack(v_small)

    after = [grad_x]
    for entry in pending:
        after = reduce_finish(entry, after)

    return (loss, grad_x[None], *[grads[n] for n in order], *[delta[n] for n in order],
            *[new_m[n] for n in order], *[new_v[n] for n in order])
```

```python
import functools

import numpy as np
import jax
import jax.numpy as jnp
from jax import lax
from jax.experimental import pallas as pl
from jax.experimental.pallas import tpu as pltpu

F32 = jnp.float32
BF16 = jnp.bfloat16

VMEM_LIMIT_V7X = 56 * 1024 * 1024
VMEM_BUDGET_V7X = 40 * 1024 * 1024
LANES = 128

GRID_W = 64
NA_WIN_ROWS = 8
NA_WIN_COLS = 16
HEAD_DIM = 128
MLA_NOPE = 128
MLA_ROPE = 64
MLA_QK = MLA_NOPE + MLA_ROPE
ROPE_THETA = 10000.0
NORM_EPS = 1e-6
NEG_INF = -1e30
N_CHIPS = 4

ADAM_LR = 0.001
ADAM_B1 = 0.9
ADAM_B2 = 0.999
ADAM_EPS = 1e-08
ADAM_WD = 0.01
ADAM_STEP = 10

MESH = pl.DeviceIdType.MESH
ANY = pl.BlockSpec(memory_space=pl.ANY)


def _params(sem=None):
    return pltpu.CompilerParams(dimension_semantics=sem, vmem_limit_bytes=VMEM_LIMIT_V7X)


def _pick(n, target, align):
    best = None
    t = align
    while t <= min(n, target):
        if n % t == 0:
            best = t
        t += align
    return n if best is None else best


def mm(a, b, *, name, ta=False, tb=False, out_dtype=F32, res=None, alpha=1.0, b_stack=False, out_stack=False,
       exact=False, after=(), epilogue=None, epi_in=(), epi_out=()):
    K, M = (a.shape if ta else a.shape[::-1])
    nst = kb = nb = None
    if b_stack:
        nst = b.shape[0]
        if tb:
            N, kb = b.shape[1], b.shape[2]
            Kb = nst * kb
        else:
            Kb, nb = b.shape[1], b.shape[2]
            N = nst * nb
    else:
        N, Kb = (b.shape if tb else b.shape[::-1])
    assert K == Kb, (a.shape, b.shape, ta, tb)
    if out_stack:
        assert N % N_CHIPS == 0
    n_unit = N // N_CHIPS if out_stack else (nb if nb is not None else N)
    tn = _pick(n_unit, 512, LANES) if n_unit % 512 == 0 or n_unit <= 512 else _pick(n_unit, 1536, LANES)
    if ta and n_unit == N and 4 * K * N * jnp.dtype(b.dtype).itemsize <= VMEM_BUDGET_V7X:
        tn = N
    m_align = LANES if ta else 16
    tm = _pick(M, 1024, m_align)
    isz = lambda t: jnp.dtype(t.dtype).itemsize
    out_dtypes = list(epi_out) if epilogue is not None else [out_dtype]
    osz = sum(jnp.dtype(t).itemsize for t in out_dtypes) + sum(isz(e) for e in epi_in)

    def vmem(tm_, tn_):
        return (2 * tm_ * K * isz(a) + 2 * K * tn_ * isz(b) + 2 * tm_ * tn_ * osz + tm_ * tn_ * 4
                + (tm_ * K * 2 if ta else 0) + (2 * tm_ * tn_ * isz(res) if res is not None else 0))

    while vmem(tm, tn) > VMEM_BUDGET_V7X and tm % 2 == 0 and (tm // 2) % m_align == 0:
        tm //= 2
    while vmem(tm, tn) > VMEM_BUDGET_V7X and tn % 2 == 0 and (tn // 2) % LANES == 0 and n_unit % (tn // 2) == 0:
        tn //= 2
    assert vmem(tm, tn) <= VMEM_BUDGET_V7X, (name, tm, tn, K)

    a_spec = pl.BlockSpec((K, tm), lambda i, j: (0, i)) if ta else pl.BlockSpec((tm, K), lambda i, j: (i, 0))
    if b_stack and not tb:
        q = nb // tn
        b_spec = pl.BlockSpec((None, K, tn), lambda i, j: (j // q, 0, j % q))
    elif b_stack and tb:
        b_spec = pl.BlockSpec((nst, tn, kb), lambda i, j: (0, j, 0))
    elif tb:
        b_spec = pl.BlockSpec((tn, K), lambda i, j: (j, 0))
    else:
        b_spec = pl.BlockSpec((K, tn), lambda i, j: (0, j))
    if out_stack:
        qo = (N // N_CHIPS) // tn
        o_spec = pl.BlockSpec((None, tm, tn), lambda i, j: (j // qo, i, j % qo))
        o_shapes = [jax.ShapeDtypeStruct((N_CHIPS, M, N // N_CHIPS), t) for t in out_dtypes]
    else:
        o_spec = pl.BlockSpec((tm, tn), lambda i, j: (i, j))
        o_shapes = [jax.ShapeDtypeStruct((M, N), t) for t in out_dtypes]
    has_res = res is not None
    n_in = 2 + has_res + len(epi_in) + len(after)
    nn = (((1,), (0,)), ((), ()))
    nt = (((1,), (1,)), ((), ()))

    def body(*refs):
        a_ref, b_ref = refs[:2]
        r_ref = refs[2] if has_res else None
        e_refs = refs[2 + has_res:2 + has_res + len(epi_in)]
        o_refs = refs[n_in:n_in + len(out_dtypes)]
        if ta:
            at_ref = refs[-1]

            @pl.when(pl.program_id(1) == 0)
            def _():
                at_ref[...] = a_ref[...].astype(BF16).T

            lhs = at_ref[...]
        elif exact:
            lhs = a_ref[...]
        else:
            lhs = a_ref[...].astype(BF16)
        if exact:
            total = lax.dot_general(lhs, b_ref[...], nt if tb else nn, preferred_element_type=F32,
                                    precision=lax.Precision.HIGHEST)
        elif b_stack and tb:
            total = None
            for s in range(nst):
                part = lax.dot_general(lhs[:, s * kb:(s + 1) * kb], b_ref[s].astype(BF16), nt,
                                       preferred_element_type=F32)
                total = part if total is None else total + part
        else:
            total = lax.dot_general(lhs, b_ref[...].astype(BF16), nt if tb else nn, preferred_element_type=F32)
        if alpha != 1.0:
            total = total * alpha
        if has_res:
            total = total + r_ref[...].astype(F32)
        vals = epilogue(total, *[e[...] for e in e_refs]) if epilogue is not None else (total,)
        for o_ref, v in zip(o_refs, vals):
            o_ref[...] = v.astype(o_ref.dtype)

    tile = pl.BlockSpec((tm, tn), lambda i, j: (i, j))
    in_specs = [a_spec, b_spec] + [tile] * (has_res + len(epi_in)) + [ANY] * len(after)
    args = [a, b] + ([res] if has_res else []) + list(epi_in) + list(after)
    outs = pl.pallas_call(
        body, name=name, grid=(M // tm, N // tn), in_specs=in_specs, out_specs=[o_spec] * len(out_dtypes),
        out_shape=o_shapes, scratch_shapes=[pltpu.VMEM((tm, K), BF16)] if ta else [],
        compiler_params=_params(("parallel", "arbitrary")),
    )(*args)
    return outs if epilogue is not None else outs[0]


def rowwise(fn, rows, consts, outs, accs=(), *, tm, name, tn=None, into=None):
    rows = [r if isinstance(r, tuple) else (r, r.shape[1], 0) for r in rows]
    S = rows[0][0].shape[0]
    tm = _pick(S, tm, 16)
    nrow, ncon, nout = len(rows), len(consts), len(outs)
    outs = [o if len(o) == 3 else (o[0], o[1], None) for o in outs]
    if tn is None:
        grid = (S // tm,)
        in_specs = [pl.BlockSpec((tm, w), functools.partial(lambda i, cb: (i, cb), cb=cb)) for _, w, cb in rows]
        in_specs += [pl.BlockSpec(c.shape, lambda i: (0, 0)) for c in consts]
        out_specs = [pl.BlockSpec((tm, n), functools.partial(lambda i, cb: (i, cb), cb=cb or 0)) for n, _, cb in outs]
        out_specs += [pl.BlockSpec(s, lambda i: (0, 0)) for s in accs]
        sem = ("arbitrary",)
    else:
        assert not accs
        N = rows[0][1]
        grid = (S // tm, N // tn)
        in_specs = [pl.BlockSpec((tm, tn), lambda i, j: (i, j)) for _ in rows]
        in_specs += [pl.BlockSpec(c.shape, lambda i, j: (0, 0)) for c in consts]
        out_specs = [pl.BlockSpec((tm, tn), lambda i, j: (i, j)) for _ in outs]
        sem = ("parallel", "parallel")
    out_shape = [jax.ShapeDtypeStruct((S, n if cb is None else into[1]), dt) for n, dt, cb in outs]
    out_shape += [jax.ShapeDtypeStruct(s, F32) for s in accs]
    extra, aliases = [], {}
    if into is not None and into[0] is not None:
        extra = [into[0]]
        aliases = {nrow + ncon: [cb is not None for _, _, cb in outs].index(True)}

    def body(*refs):
        vals = fn(*[r[...] for r in refs[:nrow + ncon]])
        if not isinstance(vals, (tuple, list)):
            vals = (vals,)
        o_refs = refs[nrow + ncon + len(extra):]
        for o_ref, v in zip(o_refs[:nout], vals[:nout]):
            o_ref[...] = v.astype(o_ref.dtype)
        if accs:
            first = pl.program_id(0) == 0

            def accumulate(a_ref, v):
                @pl.when(first)
                def _():
                    a_ref[...] = v

                @pl.when(jnp.logical_not(first))
                def _():
                    a_ref[...] += v

            for a_ref, v in zip(o_refs[nout:], vals[nout:]):
                accumulate(a_ref, v.astype(F32))

    return pl.pallas_call(
        body, name=name, grid=grid, in_specs=in_specs + [ANY] * len(extra), out_specs=out_specs, out_shape=out_shape,
        input_output_aliases=aliases, compiler_params=_params(sem),
    )(*[r[0] for r in rows], *consts, *extra)


def _rstd(x):
    return lax.rsqrt(jnp.mean(x * x, axis=-1, keepdims=True) + NORM_EPS)


def norm_fwd(x, g, *, name, tm=256):
    w = x[1] if isinstance(x, tuple) else x.shape[1]

    def fn(xb, gb):
        return (xb * _rstd(xb)) * gb

    return rowwise(fn, [x], [g], [(w, BF16)], tm=tm, name=name)[0]


def norm_bwd(x, g, dn, *, name, res=None, want_f32=True, bf16_alpha=None, tm=256, into=None, cb=None):
    w = x[1] if isinstance(x, tuple) else x.shape[1]
    has_res = res is not None

    def fn(*blocks):
        if has_res:
            xb, dnb, rb, gb = blocks
        else:
            xb, dnb, gb = blocks
        r = _rstd(xb)
        xh = xb * r
        dxh = dnb * gb
        dx = r * (dxh - xh * jnp.mean(dxh * xh, axis=-1, keepdims=True))
        if has_res:
            dx = dx + rb
        out = []
        if want_f32:
            out.append(dx)
        if bf16_alpha is not None:
            out.append(dx * bf16_alpha if bf16_alpha != 1.0 else dx)
        out.append(jnp.sum(dnb * xh, axis=0, keepdims=True))
        return tuple(out)

    outs = ([(w, F32)] if want_f32 else []) + ([(w, BF16, cb)] if bf16_alpha is not None else [])
    rows = [x, dn] + ([res] if has_res else [])
    return rowwise(fn, rows, [g], outs, accs=[(1, w)], tm=tm, name=name, into=into)


def _sig(x):
    return jax.nn.sigmoid(x)


def swiglu_tile(ub, gb):
    return ub, gb * _sig(gb) * ub


def swiglu_bwd_tile(dab, gb, ub):
    sg = _sig(gb)
    return dab * ub * (sg + gb * sg * (1.0 - sg)), dab * (gb * sg)


def rope(x, cos, sin_signed, *, name, out_dtype, into=None, cb=None, zero_cols=0):
    w = x[1] if isinstance(x, tuple) else x.shape[1]
    half = MLA_ROPE // 2

    def fn(xb, cb, sb):
        lane = lax.broadcasted_iota(jnp.int32, cb.shape, 1)
        outs = []
        for hb in range(w // LANES):
            blk = xb[:, hb * LANES:(hb + 1) * LANES]
            partner = jnp.where(lane < half, pltpu.roll(blk, LANES - half, 1), pltpu.roll(blk, half, 1))
            outs.append(blk * cb + partner * sb)
        if zero_cols:
            outs.append(jnp.zeros((xb.shape[0], zero_cols), xb.dtype))
        return outs[0] if len(outs) == 1 else jnp.concatenate(outs, axis=1)

    return rowwise(fn, [x, cos, sin_signed], [], [(w + zero_cols, out_dtype, cb)], tm=256, name=name, into=into)[0]


def _na_tables():
    cols = np.arange(GRID_W)
    kw = NA_WIN_COLS
    dc = np.clip(cols[None, :] - cols[:, None], -(kw - 1), kw - 1) + (kw - 1)
    onehot = np.zeros((LANES, GRID_W * GRID_W), np.float32)
    onehot[dc.reshape(-1), np.arange(GRID_W * GRID_W)] = 1.0
    col_start = np.clip(cols - kw // 2, 0, GRID_W - kw)
    mask = (cols[None, :] >= col_start[:, None]) & (cols[None, :] < col_start[:, None] + kw)
    return onehot, np.where(mask, 0.0, NEG_INF).astype(np.float32)


def na_bias(rpb, after=()):
    H = rpb.shape[0]
    nr, kh = 2 * NA_WIN_ROWS - 1, NA_WIN_ROWS
    onehot, maskb = _na_tables()
    rp = jnp.pad(rpb.reshape(H * nr, 2 * NA_WIN_COLS - 1), ((0, 0), (0, LANES - (2 * NA_WIN_COLS - 1))))
    t1 = mm(rp, jnp.asarray(onehot), name="na_bias_table", exact=True, after=after).reshape(H, nr, GRID_W, GRID_W)
    t1 = t1 + jnp.asarray(maskb)[None, None]
    per_t = [jnp.stack([t1[:, i - t + kh - 1] for i in range(kh)], axis=2) for t in range(kh)]
    return jnp.stack(per_t, axis=1).reshape(H, kh, GRID_W, kh * GRID_W)


def na_rpb_grad(db):
    H = db.shape[0]
    nr, kh = 2 * NA_WIN_ROWS - 1, NA_WIN_ROWS
    onehot, _ = _na_tables()
    db = db.reshape(H, kh, GRID_W, kh, GRID_W)
    per_dr = []
    for dri in range(nr):
        terms = [db[:, t, :, dri - (kh - 1) + t, :] for t in range(kh) if 0 <= dri - (kh - 1) + t < kh]
        per_dr.append(functools.reduce(jnp.add, terms))
    dt1 = jnp.stack(per_dr, axis=1).reshape(H * nr, GRID_W * GRID_W)
    g = mm(dt1, jnp.asarray(onehot), name="na_rpb_grad", tb=True, exact=True)
    return g[:, :2 * NA_WIN_COLS - 1].reshape(H, nr, 2 * NA_WIN_COLS - 1)


def _na_first_row(r, rows):
    return jnp.clip(r - NA_WIN_ROWS // 2, 0, rows - NA_WIN_ROWS)


NA_ROWS_PER_STEP = 8


def _na_probs(q, k_ref, b_ref, r, rows):
    first = _na_first_row(r, rows)
    start = pl.multiple_of(first * GRID_W, GRID_W)
    k = k_ref[pl.ds(start, NA_WIN_ROWS * GRID_W), :]
    s = lax.dot_general(q, k, (((1,), (1,)), ((), ())), preferred_element_type=F32)
    s = s * (HEAD_DIM ** -0.5) + b_ref[r - first]
    m = jnp.max(s, axis=-1, keepdims=True)
    e = jnp.exp(s - m)
    return k, e / jnp.sum(e, axis=-1, keepdims=True), start, r - first


def na_fwd(z, bias, H, S):
    rows = S // GRID_W
    nkeys = NA_WIN_ROWS * GRID_W
    rb = _pick(rows, NA_ROWS_PER_STEP, 1)

    def body(q_ref, k_ref, v_ref, b_ref, o_ref):
        for j in range(rb):
            r = pl.program_id(1) * rb + j
            rows_j = pl.ds(j * GRID_W, GRID_W)
            _, p, start, _ = _na_probs(q_ref[rows_j, :], k_ref, b_ref, r, rows)
            v = v_ref[pl.ds(start, nkeys), :]
            o_ref[rows_j, :] = jnp.dot(p.astype(BF16), v, preferred_element_type=F32).astype(o_ref.dtype)

    return pl.pallas_call(
        body, name="na_fwd", grid=(H, rows // rb),
        in_specs=[pl.BlockSpec((rb * GRID_W, HEAD_DIM), lambda h, i: (i, h)),
                  pl.BlockSpec((S, HEAD_DIM), lambda h, i: (0, H + h)),
                  pl.BlockSpec((S, HEAD_DIM), lambda h, i: (0, 2 * H + h)),
                  pl.BlockSpec((None, NA_WIN_ROWS, GRID_W, nkeys), lambda h, i: (h, 0, 0, 0))],
        out_specs=pl.BlockSpec((rb * GRID_W, HEAD_DIM), lambda h, i: (i, h)),
        out_shape=jax.ShapeDtypeStruct((S, H * HEAD_DIM), BF16),
        compiler_params=_params(("parallel", "arbitrary")),
    )(z, z, z, bias)


def na_bwd(z, bias, do, H, S):
    rows = S // GRID_W
    nkeys = NA_WIN_ROWS * GRID_W
    rb = _pick(rows, NA_ROWS_PER_STEP, 1)
    tn_dims = (((0,), (0,)), ((), ()))

    def body(q_ref, k_ref, v_ref, b_ref, do_ref, dq_ref, dk_ref, dv_ref, db_ref, dk_acc, dv_acc):
        i = pl.program_id(1)

        @pl.when(i == 0)
        def _():
            dk_acc[...] = jnp.zeros_like(dk_acc)
            dv_acc[...] = jnp.zeros_like(dv_acc)
            db_ref[...] = jnp.zeros_like(db_ref)

        for j in range(rb):
            rows_j = pl.ds(j * GRID_W, GRID_W)
            q = q_ref[rows_j, :]
            k, p, start, t = _na_probs(q, k_ref, b_ref, i * rb + j, rows)
            keys = pl.ds(start, nkeys)
            dob = do_ref[rows_j, :].astype(BF16)
            dp = lax.dot_general(dob, v_ref[keys, :], (((1,), (1,)), ((), ())), preferred_element_type=F32)
            ds = p * (dp - jnp.sum(dp * p, axis=-1, keepdims=True))
            dsb = (ds * (HEAD_DIM ** -0.5)).astype(BF16)
            dq_ref[rows_j, :] = jnp.dot(dsb, k, preferred_element_type=F32).astype(dq_ref.dtype)
            dk_acc[keys, :] += lax.dot_general(dsb, q, tn_dims, preferred_element_type=F32)
            dv_acc[keys, :] += lax.dot_general(p.astype(BF16), dob, tn_dims, preferred_element_type=F32)
            db_ref[t] += ds

        @pl.when(i == rows // rb - 1)
        def _():
            dk_ref[...] = dk_acc[...].astype(dk_ref.dtype)
            dv_ref[...] = dv_acc[...].astype(dv_ref.dtype)

    W = H * HEAD_DIM
    qspec = pl.BlockSpec((rb * GRID_W, HEAD_DIM), lambda h, i: (i, h))
    bspec = pl.BlockSpec((None, NA_WIN_ROWS, GRID_W, nkeys), lambda h, i: (h, 0, 0, 0))
    return pl.pallas_call(
        body, name="na_bwd", grid=(H, rows // rb),
        in_specs=[qspec, pl.BlockSpec((S, HEAD_DIM), lambda h, i: (0, H + h)),
                  pl.BlockSpec((S, HEAD_DIM), lambda h, i: (0, 2 * H + h)), bspec, qspec],
        out_specs=[qspec, pl.BlockSpec((S, HEAD_DIM), lambda h, i: (0, h)),
                   pl.BlockSpec((S, HEAD_DIM), lambda h, i: (0, h)), bspec],
        out_shape=[jax.ShapeDtypeStruct((S, W), BF16)] * 3 + [jax.ShapeDtypeStruct((H, NA_WIN_ROWS, GRID_W, nkeys), F32)],
        scratch_shapes=[pltpu.VMEM((S, HEAD_DIM), F32)] * 2,
        compiler_params=_params(("arbitrary", "arbitrary")),
    )(z, z, z, bias, do)


def _mla_keys(kn_ref, kr_ref, kcat):
    @pl.when(pl.program_id(1) == 0)
    def _():
        kcat[:, :HEAD_DIM] = kn_ref[...]
        kcat[:, HEAD_DIM:] = kr_ref[...]


MLA_LOG2_SCALE = (MLA_QK ** -0.5) * 1.4426950408889634


def _mla_scores(qn_ref, qr_ref, kcat):
    qcat = jnp.concatenate([qn_ref[...], qr_ref[...]], axis=1)
    return qcat, lax.dot_general(qcat, kcat[...], (((1,), (1,)), ((), ())), preferred_element_type=F32)


def mla_fwd(qn, qr, kn, v, kr, H, S):
    tq = _pick(S, 256, 16)

    def body(qn_ref, qr_ref, kn_ref, v_ref, kr_ref, o_ref, lse_ref, kcat):
        _mla_keys(kn_ref, kr_ref, kcat)
        _, s = _mla_scores(qn_ref, qr_ref, kcat)
        m = jnp.max(s, axis=-1, keepdims=True)
        e = jnp.exp2((s - m) * MLA_LOG2_SCALE)
        l = jnp.sum(e, axis=-1, keepdims=True)
        o = jnp.dot(e.astype(BF16), v_ref[...], preferred_element_type=F32)
        o_ref[...] = (o / l).astype(o_ref.dtype)
        lse_ref[...] = jnp.broadcast_to(m * MLA_LOG2_SCALE + jnp.log2(l), lse_ref.shape)

    qspec = pl.BlockSpec((tq, HEAD_DIM), lambda h, i: (i, h))
    kspec = pl.BlockSpec((S, HEAD_DIM), lambda h, i: (0, h))
    return pl.pallas_call(
        body, name="mla_fwd", grid=(H, S // tq),
        in_specs=[qspec, qspec, kspec, kspec, pl.BlockSpec((S, LANES), lambda h, i: (0, 0))],
        out_specs=[qspec, qspec],
        out_shape=[jax.ShapeDtypeStruct((S, H * HEAD_DIM), BF16), jax.ShapeDtypeStruct((S, H * LANES), F32)],
        scratch_shapes=[pltpu.VMEM((S, 2 * HEAD_DIM), BF16)],
        compiler_params=_params(("parallel", "arbitrary")),
    )(qn, qr, kn, v, kr)


def mla_bwd(qn, qr, kn, v, kr, lse, do, H, S):
    tq = _pick(S, 256, 16)
    nt = (((1,), (1,)), ((), ()))
    tn_dims = (((0,), (0,)), ((), ()))

    def body(qn_ref, qr_ref, kn_ref, v_ref, kr_ref, lse_ref, do_ref, dqn_ref, dqr_ref, dkn_ref, dv_ref, dkr_ref, kcat):
        h, i = pl.program_id(0), pl.program_id(1)
        _mla_keys(kn_ref, kr_ref, kcat)
        qcat, s = _mla_scores(qn_ref, qr_ref, kcat)
        p = jnp.exp2(s * MLA_LOG2_SCALE - lse_ref[:, 0:1])
        dob = do_ref[...].astype(BF16)
        dp = lax.dot_general(dob, v_ref[...], nt, preferred_element_type=F32)
        ds = p * (dp - jnp.sum(dp * p, axis=-1, keepdims=True))
        dsb = (ds * (MLA_QK ** -0.5)).astype(BF16)
        dq = jnp.dot(dsb, kcat[...], preferred_element_type=F32)
        dqn_ref[...] = dq[:, :HEAD_DIM].astype(dqn_ref.dtype)
        dqr_ref[...] = dq[:, HEAD_DIM:].astype(dqr_ref.dtype)

        @pl.when(i == 0)
        def _():
            dkn_ref[...] = jnp.zeros_like(dkn_ref)
            dv_ref[...] = jnp.zeros_like(dv_ref)

        @pl.when(jnp.logical_and(i == 0, h == 0))
        def _():
            dkr_ref[...] = jnp.zeros_like(dkr_ref)

        dk = lax.dot_general(dsb, qcat, tn_dims, preferred_element_type=F32)
        dkn_ref[...] += dk[:, :HEAD_DIM]
        dkr_ref[...] += dk[:, HEAD_DIM:]
        dv_ref[...] += lax.dot_general(p.astype(BF16), dob, tn_dims, preferred_element_type=F32)

    qspec = pl.BlockSpec((tq, HEAD_DIM), lambda h, i: (i, h))
    kspec = pl.BlockSpec((S, HEAD_DIM), lambda h, i: (0, h))
    rspec = pl.BlockSpec((S, LANES), lambda h, i: (0, 0))
    W = H * HEAD_DIM
    return pl.pallas_call(
        body, name="mla_bwd", grid=(H, S // tq),
        in_specs=[qspec, qspec, kspec, kspec, rspec, qspec, qspec],
        out_specs=[qspec, qspec, kspec, kspec, rspec],
        out_shape=[jax.ShapeDtypeStruct((S, W), BF16), jax.ShapeDtypeStruct((S, W), F32),
                   jax.ShapeDtypeStruct((S, W), F32), jax.ShapeDtypeStruct((S, W), F32),
                   jax.ShapeDtypeStruct((S, LANES), F32)],
        scratch_shapes=[pltpu.VMEM((S, 2 * HEAD_DIM), BF16)],
        compiler_params=_params(("arbitrary", "arbitrary")),
    )(qn, qr, kn, v, kr, lse, do)


def _place():
    return lax.axis_index("x"), lax.axis_index("y"), lax.axis_index("c")


def _other_chips(x, y):
    return [(1 - x, y), (x, 1 - y), (1 - x, 1 - y)]


def _remote(src, dst, send_sem, recv_sem, to):
    return pltpu.make_async_remote_copy(src_ref=src, dst_ref=dst, send_sem=send_sem, recv_sem=recv_sem,
                                        device_id=to, device_id_type=MESH)


HBM = pl.BlockSpec(memory_space=pltpu.HBM)
SEM = pl.BlockSpec(memory_space=pltpu.SEMAPHORE)
EFFECT = pltpu.SideEffectType.DATAFLOW_SIDE_EFFECTING


def _in_hbm(a):
    return pltpu.with_memory_space_constraint(a, pltpu.HBM)


TOKEN = jax.ShapeDtypeStruct((8, LANES), F32)
IN_VMEM = pl.BlockSpec(memory_space=pltpu.VMEM)


def gather_start(shards, landings, after, tag):
    n = len(shards)

    def body(*refs):
        ins, lands = refs[:n], refs[n:2 * n]
        send, recv = refs[2 * n + len(after)], refs[2 * n + len(after) + 1]
        token = refs[-1]
        x, y, c = _place()
        me = 2 * x + y
        for w in range(n):
            for k, (px, py) in enumerate(_other_chips(x, y)):
                _remote(ins[w].at[c], lands[w].at[me, c], send.at[3 * w + k], recv.at[3 * w + k], (px, py, c)).start()
        token[...] = jnp.zeros_like(token)

    bufs = list(shards) + list(landings)
    outs = pl.pallas_call(
        body, name="gather_start_" + tag,
        out_shape=(pltpu.SemaphoreType.DMA((3 * n,)),) * 2 + tuple(pltpu.HBM(b.shape, b.dtype) for b in bufs) + (TOKEN,),
        in_specs=[HBM] * (2 * n) + [ANY] * len(after), out_specs=tuple([SEM, SEM] + [HBM] * (2 * n) + [IN_VMEM]),
        input_output_aliases={i: 2 + i for i in range(2 * n)},
        compiler_params=pltpu.CompilerParams(has_side_effects=EFFECT),
    )(*[_in_hbm(b) for b in bufs], *after)
    return (outs[0], outs[1]), outs[2:2 + n], outs[2 + n:2 + 2 * n], outs[-1]


def gather_wait(sems, shards, landings, after, tag):
    n = len(shards)
    send, recv = sems

    def body(*refs):
        ins, lands = refs[:n], refs[n:2 * n]
        send_sem, recv_sem = refs[2 * n], refs[2 * n + 1]
        x, y, c = _place()
        me = 2 * x + y
        for w in range(n):
            for k, (px, py) in enumerate(_other_chips(x, y)):
                cp = _remote(ins[w].at[c], lands[w].at[2 * px + py, c], send_sem.at[3 * w + k], recv_sem.at[3 * w + k],
                             (px, py, c))
                cp.wait_send()
                cp.wait_recv()

    bufs = list(shards) + list(landings)
    outs = pl.pallas_call(
        body, name="gather_wait_" + tag, out_shape=tuple(pltpu.HBM(b.shape, b.dtype) for b in bufs),
        in_specs=[HBM] * (2 * n) + [SEM, SEM] + [ANY] * len(after), out_specs=tuple([HBM] * (2 * n)),
        input_output_aliases={i: i for i in range(2 * n)},
        compiler_params=pltpu.CompilerParams(has_side_effects=EFFECT),
    )(*bufs, send, recv, *after)
    return outs[:n], outs[n:]


def gather_forward(landings, tag):
    n = len(landings)

    def body(*refs):
        ins, outs = refs[:n], refs[n:2 * n]
        send, recv = refs[2 * n:]
        x, y, c = _place()
        sibling = (x, y, 1 - c)
        cps = []
        for w in range(n):
            for k, (px, py) in enumerate(_other_chips(x, y)):
                j = 2 * px + py
                cp = _remote(ins[w].at[j, c], outs[w].at[j, c], send.at[3 * w + k], recv.at[3 * w + k], sibling)
                cp.start()
                cps.append(cp)
        for w in range(n):
            for k, (px, py) in enumerate(_other_chips(x, y)):
                blk = outs[w].at[2 * px + py, 1 - c]
                _remote(blk, blk, send.at[3 * w + k], recv.at[3 * w + k], sibling).wait_recv()
        for cp in cps:
            cp.wait_send()

    return pl.pallas_call(
        body, name="gather_forward_" + tag, in_specs=[ANY] * n, out_specs=[ANY] * n,
        out_shape=[jax.ShapeDtypeStruct(a.shape, a.dtype) for a in landings],
        input_output_aliases={i: i for i in range(n)},
        scratch_shapes=[pltpu.SemaphoreType.DMA((3 * n,)), pltpu.SemaphoreType.DMA((3 * n,))],
    )(*landings)


def pair_start(grads, landings, tag):
    n = len(grads)

    def body(*refs):
        ins, lands = refs[:n], refs[n:2 * n]
        send, recv = refs[2 * n], refs[2 * n + 1]
        token = refs[-1]
        x, y, c = _place()
        for w in range(n):
            _remote(ins[w].at[:, 1 - c], lands[w], send.at[w], recv.at[w], (x, y, 1 - c)).start()
        token[...] = jnp.zeros_like(token)

    bufs = list(grads) + list(landings)
    outs = pl.pallas_call(
        body, name="pair_start_" + tag,
        out_shape=(pltpu.SemaphoreType.DMA((n,)),) * 2 + tuple(pltpu.HBM(b.shape, b.dtype) for b in bufs) + (TOKEN,),
        in_specs=[HBM] * (2 * n), out_specs=tuple([SEM, SEM] + [HBM] * (2 * n) + [IN_VMEM]),
        input_output_aliases={i: 2 + i for i in range(2 * n)},
        compiler_params=pltpu.CompilerParams(has_side_effects=EFFECT),
    )(*[_in_hbm(b) for b in bufs])
    return (outs[0], outs[1]), outs[2:2 + n], outs[2 + n:2 + 2 * n], outs[-1]


def pair_wait(sems, grads, landings, after, tag):
    n = len(grads)

    def body(*refs):
        ins, lands = refs[:n], refs[n:2 * n]
        send, recv = refs[2 * n], refs[2 * n + 1]
        x, y, c = _place()
        for w in range(n):
            cp = _remote(ins[w].at[:, 1 - c], lands[w], send.at[w], recv.at[w], (x, y, 1 - c))
            cp.wait_send()
            cp.wait_recv()

    bufs = list(grads) + list(landings)
    outs = pl.pallas_call(
        body, name="pair_wait_" + tag, out_shape=tuple(pltpu.HBM(b.shape, b.dtype) for b in bufs),
        in_specs=[HBM] * (2 * n) + [SEM, SEM] + [ANY] * len(after), out_specs=tuple([HBM] * (2 * n)),
        input_output_aliases={i: i for i in range(2 * n)},
        compiler_params=pltpu.CompilerParams(has_side_effects=EFFECT),
    )(*bufs, sems[0], sems[1], *after)
    return outs[:n], outs[n:]


def scatter_start(sums, landings, tag):
    n = len(sums)

    def body(*refs):
        ins, lands = refs[:n], refs[n:2 * n]
        send, recv = refs[2 * n], refs[2 * n + 1]
        token = refs[-1]
        x, y, c = _place()
        for w in range(n):
            for k, (px, py) in enumerate(_other_chips(x, y)):
                _remote(ins[w].at[2 * px + py], lands[w].at[k], send.at[3 * w + k], recv.at[3 * w + k], (px, py, c)).start()
        token[...] = jnp.zeros_like(token)

    bufs = list(sums) + list(landings)
    outs = pl.pallas_call(
        body, name="scatter_start_" + tag,
        out_shape=(pltpu.SemaphoreType.DMA((3 * n,)),) * 2 + tuple(pltpu.HBM(b.shape, b.dtype) for b in bufs) + (TOKEN,),
        in_specs=[HBM] * (2 * n), out_specs=tuple([SEM, SEM] + [HBM] * (2 * n) + [IN_VMEM]),
        input_output_aliases={i: 2 + i for i in range(2 * n)},
        compiler_params=pltpu.CompilerParams(has_side_effects=EFFECT),
    )(*[_in_hbm(b) for b in bufs])
    return (outs[0], outs[1]), outs[2:2 + n], outs[2 + n:2 + 2 * n], outs[-1]


def scatter_wait(sems, sums, landings, after, tag):
    n = len(sums)

    def body(*refs):
        ins, lands = refs[:n], refs[n:2 * n]
        send, recv = refs[2 * n], refs[2 * n + 1]
        x, y, c = _place()
        for w in range(n):
            for k, (px, py) in enumerate(_other_chips(x, y)):
                cp = _remote(ins[w].at[2 * px + py], lands[w].at[k], send.at[3 * w + k], recv.at[3 * w + k], (px, py, c))
                cp.wait_send()
                cp.wait_recv()

    bufs = list(sums) + list(landings)
    outs = pl.pallas_call(
        body, name="scatter_wait_" + tag, out_shape=tuple(pltpu.HBM(b.shape, b.dtype) for b in bufs),
        in_specs=[HBM] * (2 * n) + [SEM, SEM] + [ANY] * len(after), out_specs=tuple([HBM] * (2 * n)),
        input_output_aliases={i: i for i in range(2 * n)},
        compiler_params=pltpu.CompilerParams(has_side_effects=EFFECT),
    )(*bufs, sems[0], sems[1], *after)
    return outs[:n], outs[n:]


def half_exchange(halves, tag):
    n = len(halves)

    def body(*refs):
        ins, outs = refs[:n], refs[n:2 * n]
        send, recv = refs[2 * n:]
        x, y, c = _place()
        cps = []
        for w in range(n):
            cp = _remote(ins[w], outs[w], send.at[w], recv.at[w], (x, y, 1 - c))
            cp.start()
            cps.append(cp)
        for cp in cps:
            cp.wait()

    return pl.pallas_call(
        body, name="grad_half_exchange_" + tag, in_specs=[ANY] * n, out_specs=[ANY] * n,
        out_shape=[jax.ShapeDtypeStruct(h.shape, h.dtype) for h in halves],
        scratch_shapes=[pltpu.SemaphoreType.DMA((n,)), pltpu.SemaphoreType.DMA((n,))],
    )(*halves)


def gather_small(v, after=()):
    def body(*refs):
        v_ref = refs[0]
        o_ref, send, recv, local = refs[1 + len(after):]
        x, y, c = _place()
        me = 4 * x + 2 * y + c
        own = pltpu.make_async_copy(v_ref, o_ref.at[me], local)
        own.start()
        cps = []
        for k in range(1, 8):
            fx, fy, fc = (k >> 2) & 1, (k >> 1) & 1, k & 1
            to = (x ^ fx if fx else x, y ^ fy if fy else y, c ^ fc if fc else c)
            cp = _remote(v_ref, o_ref.at[me], send.at[k - 1], recv.at[k - 1], to)
            cp.start()
            cps.append(cp)
        for k in range(1, 8):
            fx, fy, fc = (k >> 2) & 1, (k >> 1) & 1, k & 1
            px, py, pc = (x ^ fx if fx else x, y ^ fy if fy else y, c ^ fc if fc else c)
            cps[k - 1].wait_send()
            _remote(v_ref, o_ref.at[4 * px + 2 * py + pc], send.at[k - 1], recv.at[k - 1], (px, py, pc)).wait_recv()
        own.wait()

    return pl.pallas_call(
        body, name="gather_small_grads", in_specs=[ANY] * (1 + len(after)), out_specs=ANY,
        out_shape=jax.ShapeDtypeStruct((8,) + v.shape, v.dtype),
        scratch_shapes=[pltpu.SemaphoreType.DMA((7,)), pltpu.SemaphoreType.DMA((7,)), pltpu.SemaphoreType.DMA],
    )(v, *after)


def _row_tile(rows, cols, nbuf_bytes):
    tm = _pick(rows, 512, 16)
    while tm * cols * nbuf_bytes * 2 > VMEM_BUDGET_V7X and tm % 32 == 0:
        tm //= 2
    return tm


def pair_sum(g, r, c_idx, tag):
    _, _, rows, cols = g.shape
    tm = _row_tile(rows, cols, 2 + 2 + 2)
    nb = rows // tm

    def body(c_ref, g_ref, r_ref, o_ref):
        o_ref[...] = (g_ref[...].astype(F32) + r_ref[...].astype(F32)).astype(o_ref.dtype)

    gs = pltpu.PrefetchScalarGridSpec(
        num_scalar_prefetch=1, grid=(N_CHIPS, nb),
        in_specs=[pl.BlockSpec((None, None, tm, cols), lambda j, i, c_ref: (j, c_ref[0], i, 0)),
                  pl.BlockSpec((None, tm, cols), lambda j, i, c_ref: (j, i, 0))],
        out_specs=pl.BlockSpec((None, tm, cols), lambda j, i, c_ref: (j, i, 0)))
    return pl.pallas_call(body, name="grad_pair_sum_" + tag, grid_spec=gs,
                          out_shape=jax.ShapeDtypeStruct(r.shape, BF16),
                          compiler_params=_params(("arbitrary", "arbitrary")))(c_idx, g, r)


def chip_sum(s, r, j_idx, tag):
    _, rows, cols = s.shape
    tm = _row_tile(rows, cols, 2 + 3 * 2 + 4)
    nb = rows // tm

    def body(j_ref, s_ref, r_ref, o_ref):
        t = s_ref[...].astype(F32)
        for k in range(3):
            t = t + r_ref[k].astype(F32)
        o_ref[...] = t

    gs = pltpu.PrefetchScalarGridSpec(
        num_scalar_prefetch=1, grid=(nb,),
        in_specs=[pl.BlockSpec((None, tm, cols), lambda i, j_ref: (j_ref[0], i, 0)),
                  pl.BlockSpec((3, tm, cols), lambda i, j_ref: (0, i, 0))],
        out_specs=pl.BlockSpec((tm, cols), lambda i, j_ref: (i, 0)))
    return pl.pallas_call(body, name="grad_chip_sum_" + tag, grid_spec=gs,
                          out_shape=jax.ShapeDtypeStruct((rows, cols), F32),
                          compiler_params=_params(("arbitrary",)))(j_idx, s, r)


def adamw(w, g, m, v, *, name):
    rows, cols = w.shape
    tm = _row_tile(rows, cols, 7 * 4)

    return rowwise(_adamw_math, [w, g, m, v], [], [(cols, F32)] * 3, tm=tm, name=name)


def _adamw_math(wb, gb, mb, vb):
    m2 = ADAM_B1 * mb + (1.0 - ADAM_B1) * gb
    v2 = ADAM_B2 * vb + (1.0 - ADAM_B2) * (gb * gb)
    m_hat = m2 / (1.0 - ADAM_B1 ** ADAM_STEP)
    v_hat = v2 / (1.0 - ADAM_B2 ** ADAM_STEP)
    delta = -ADAM_LR * (m_hat / (jnp.sqrt(v_hat) + ADAM_EPS) + ADAM_WD * wb)
    return delta, m2, v2


def adamw_shard(w, g_own, g_sib, m, v, c_idx, *, name):
    rows, cols = g_own.shape
    tm = _row_tile(rows, cols, 9 * 4)
    nb = rows // tm

    def body(c_ref, w_ref, go_ref, gs_ref, m_ref, v_ref, g_out, d_out, m_out, v_out):
        gb = jnp.where(pl.program_id(0) == c_ref[0], go_ref[...], gs_ref[...])
        delta, m2, v2 = _adamw_math(w_ref[...], gb, m_ref[...], v_ref[...])
        g_out[...] = gb
        d_out[...] = delta
        m_out[...] = m2
        v_out[...] = v2

    full = pl.BlockSpec((tm, cols), lambda h, i, c_ref: (h * nb + i, 0))
    own = pl.BlockSpec((tm, cols), lambda h, i, c_ref: (jnp.where(h == c_ref[0], i, 0), 0))
    sib = pl.BlockSpec((tm, cols), lambda h, i, c_ref: (jnp.where(h == c_ref[0], 0, i), 0))
    gs = pltpu.PrefetchScalarGridSpec(num_scalar_prefetch=1, grid=(2, nb), in_specs=[full, own, sib, full, full],
                                      out_specs=[full] * 4)
    return pl.pallas_call(body, name=name, grid_spec=gs, out_shape=[jax.ShapeDtypeStruct(w.shape, F32)] * 4,
                          compiler_params=_params(("arbitrary", "arbitrary")))(c_idx, w, g_own, g_sib, m, v)


def sum_devices(a):
    def body(a_ref, o_ref):
        t = a_ref[0]
        for k in range(1, 8):
            t = t + a_ref[k]
        o_ref[...] = t

    return pl.pallas_call(body, name="sum_small_grads", out_shape=jax.ShapeDtypeStruct(a.shape[1:], a.dtype))(a)


def _halves(w2d):
    r, c = w2d.shape
    return w2d.reshape(2, r // 2, c)


def kernel(x, p, ffn1_norm, ffn1_w_gate, ffn1_w_up, ffn1_w_down, mix_norm, w_in, q_a_norm, w_uq, kv_a_norm, w_ukv, na_rpb, w_branch_a, w_branch_b, w_out, ffn2_norm, ffn2_w_gate, ffn2_w_up, ffn2_w_down, pl_norm, w_pl, w_pl_gate, final_norm, loss_target, m_ffn1_norm, m_ffn1_w_gate, m_ffn1_w_up, m_ffn1_w_down, m_mix_norm, m_w_in, m_q_a_norm, m_w_uq, m_kv_a_norm, m_w_ukv, m_na_rpb, m_w_branch_a, m_w_branch_b, m_w_out, m_ffn2_norm, m_ffn2_w_gate, m_ffn2_w_up, m_ffn2_w_down, m_pl_norm, m_w_pl, m_w_pl_gate, m_final_norm, v_ffn1_norm, v_ffn1_w_gate, v_ffn1_w_up, v_ffn1_w_down, v_mix_norm, v_w_in, v_q_a_norm, v_w_uq, v_kv_a_norm, v_w_ukv, v_na_rpb, v_w_branch_a, v_w_branch_b, v_w_out, v_ffn2_norm, v_ffn2_w_gate, v_ffn2_w_up, v_ffn2_w_down, v_pl_norm, v_w_pl, v_w_pl_gate, v_final_norm):
    big = ["ffn1_w_gate", "ffn1_w_up", "ffn1_w_down", "w_in", "w_uq", "w_ukv", "w_branch_a", "w_branch_b", "w_out",
           "ffn2_w_gate", "ffn2_w_up", "ffn2_w_down", "w_pl", "w_pl_gate"]
    col_sharded = {"ffn1_w_gate", "ffn1_w_up", "w_in", "w_uq", "w_ukv", "w_branch_a", "w_branch_b", "ffn2_w_gate",
                   "ffn2_w_up", "w_pl"}
    small = ["ffn1_norm", "mix_norm", "q_a_norm", "kv_a_norm", "na_rpb", "ffn2_norm", "pl_norm", "final_norm"]
    order = ["ffn1_norm", "ffn1_w_gate", "ffn1_w_up", "ffn1_w_down", "mix_norm", "w_in", "q_a_norm", "w_uq",
             "kv_a_norm", "w_ukv", "na_rpb", "w_branch_a", "w_branch_b", "w_out", "ffn2_norm", "ffn2_w_gate",
             "ffn2_w_up", "ffn2_w_down", "pl_norm", "w_pl", "w_pl_gate", "final_norm"]
    env = dict(locals())
    W = {n: env[n] for n in order}
    Mo = {n: env["m_" + n] for n in order}
    Vo = {n: env["v_" + n] for n in order}

    xs = x[0]
    S, D = xs.shape
    tgt = loss_target[0]
    ps = p[0, 0]
    NAW = w_branch_a.shape[1]
    MLAW = w_branch_b.shape[1]
    NH, MH = NAW // HEAD_DIM, MLAW // HEAD_DIM
    QR, KVR = w_uq.shape[1], w_ukv.shape[1]
    F = ffn1_w_down.shape[1] * N_CHIPS
    cx, cy, cc = _place()
    c_idx = jnp.reshape(cc, (1,)).astype(jnp.int32)
    j_idx = jnp.reshape(2 * cx + cy, (1,)).astype(jnp.int32)

    me_chip = 2 * cx + cy
    groups = [["ffn1_w_gate"], ["ffn1_w_up"], ["ffn1_w_down"], ["w_in"],
              ["w_uq", "w_ukv", "w_branch_a", "w_branch_b", "w_out"],
              ["ffn2_w_gate", "ffn2_w_up", "ffn2_w_down"], ["w_pl", "w_pl_gate"]]
    started, tokens = [], []
    for g, members in enumerate(groups):
        shards = [_halves(W[n][0].astype(BF16)) for n in members]
        landings = [lax.empty((N_CHIPS,) + s.shape, BF16) for s in shards]
        sems, shards_thru, landings_thru, token = gather_start(shards, landings, tokens[-1:], str(g))
        started.append((sems, shards_thru, landings_thru))
        tokens.append(token)
    gathered = {}

    def arrive(n, after):
        g = [n in members for members in groups].index(True)
        sems, shards_thru, landings_thru = started[g]
        after = list(after) if isinstance(after, (list, tuple)) else [after]
        shards_out, landed = gather_wait(sems, shards_thru, landings_thru, after, str(g))
        for name, full, own in zip(groups[g], gather_forward(landed, str(g)), shards_out):
            gathered[name] = lax.dynamic_update_slice(full, own[None], (me_chip, 0, 0, 0))

    def stacked(n, after=None):
        if n not in gathered:
            arrive(n, after)
        g = gathered[n]
        return g.reshape(N_CHIPS, 2 * g.shape[2], g.shape[3])

    def plain(n, after=None):
        if n in col_sharded:
            st = stacked(n, after)
            return st.transpose(1, 0, 2).reshape(st.shape[1], N_CHIPS * st.shape[2])
        if n not in gathered:
            arrive(n, after)
        g = gathered[n]
        return g.reshape(N_CHIPS * 2 * g.shape[2], g.shape[3])

    n_na = 3 * NAW
    n_front = n_na + QR + KVR
    n_in = n_front + MLA_ROPE + 2 * D
    off_ql, off_kvl, off_kr = 2 * D, 2 * D + QR, 2 * D + QR + KVR
    kr_w = 2 * LANES
    rest_w = off_kr + kr_w
    rest_ranges = [(n_front + MLA_ROPE, n_in), (n_na, n_front), (n_front, n_front + MLA_ROPE)]

    def shard_cols(st, lo, hi):
        nb, parts = st.shape[2], []
        while lo < hi:
            j = lo // nb
            end = min(hi, (j + 1) * nb)
            parts.append(st[j][:, lo - j * nb:end - j * nb])
            lo = end
        return parts

    def w_in_shards(g_na, g_rest):
        pieces = [(0, n_na, g_na, 0)]
        o = 0
        for lo, hi in rest_ranges:
            pieces.append((lo, hi, g_rest, o))
            o += hi - lo
        nb, shards = n_in // N_CHIPS, []
        for j in range(N_CHIPS):
            parts = []
            for lo, hi, src, o in sorted(pieces):
                a, b = max(lo, j * nb), min(hi, (j + 1) * nb)
                if a < b:
                    parts.append(src[:, o + a - lo:o + b - lo])
            shards.append(jnp.concatenate(parts, axis=1))
        return jnp.stack(shards)

    pos = jnp.arange(S, dtype=F32)
    inv_freq = 1.0 / (ROPE_THETA ** (jnp.arange(0, MLA_ROPE, 2, dtype=F32) / MLA_ROPE))
    ang = pos[:, None] * inv_freq[None, :]
    zpad = jnp.zeros((S, LANES - MLA_ROPE), F32)
    cos_t = jnp.concatenate([jnp.cos(ang), jnp.cos(ang), zpad], axis=1)
    sin_t = jnp.concatenate([-jnp.sin(ang), jnp.sin(ang), zpad], axis=1)

    def ffn_fwd(h, norm_g, tag, pre, after=()):
        n = norm_fwd(h, norm_g, name=f"{tag}_norm")
        g = mm(n, stacked(pre + "_w_gate", [n, *after]), name=f"{tag}_gate", b_stack=True)
        u, a = mm(n, stacked(pre + "_w_up", g), name=f"{tag}_up", b_stack=True, epilogue=swiglu_tile, epi_in=[g],
                  epi_out=[F32, BF16])
        h_out = mm(a, plain(pre + "_w_down", a), name=f"{tag}_down", res=h, alpha=0.5)
        return h_out, (n, g, u, a)

    def ffn_bwd(h, norm_g, saved, dh, dh_half, tag, pre, last, after=()):
        n, g, u, a = saved
        G[pre + "_w_down"] = mm(a, dh_half, name=f"{tag}_dw_down", ta=True, out_dtype=BF16, after=after)
        begun_d, token_d = pair_begin([pre + "_w_down"], tag + "_d")
        dg, du = mm(dh_half, plain(pre + "_w_down"), name=f"{tag}_da", tb=True, after=[token_d],
                    epilogue=swiglu_bwd_tile, epi_in=[g, u], epi_out=[BF16, BF16])
        G[pre + "_w_gate"] = mm(n, dg, name=f"{tag}_dw_gate", ta=True, out_dtype=BF16, out_stack=True)
        G[pre + "_w_up"] = mm(n, du, name=f"{tag}_dw_up", ta=True, out_dtype=BF16, out_stack=True)
        begun_gu, token_gu = pair_begin([pre + "_w_gate", pre + "_w_up"], tag + "_gu")
        dn = mm(dg, stacked(pre + "_w_gate"), name=f"{tag}_dn_gate", tb=True, b_stack=True, after=[token_gu])
        token = reduce_go([begun_d, begun_gu], tag, [dn])
        dn = mm(du, stacked(pre + "_w_up"), name=f"{tag}_dn_up", tb=True, b_stack=True, res=dn, after=[token])
        return norm_bwd(h, norm_g, dn, name=f"{tag}_dnorm", res=dh, bf16_alpha=None if last else 1.0)

    bias = na_bias(na_rpb[0], after=tokens[-1:])
    h1, ffn1_saved = ffn_fwd(xs, ffn1_norm, "ffn1", "ffn1", after=[bias, tokens[-1]])
    u_mix = norm_fwd(h1, mix_norm, name="mix_norm")
    win_st = stacked("w_in", u_mix)
    w_na = jnp.concatenate(shard_cols(win_st, 0, n_na), axis=1)
    w_rest = jnp.concatenate([p_ for lo, hi in rest_ranges for p_ in shard_cols(win_st, lo, hi)]
                             + [jnp.zeros((D, kr_w - MLA_ROPE), BF16)], axis=1)
    z_na = mm(u_mix, w_na, name="mix_in_na", out_dtype=BF16)
    z = mm(u_mix, w_rest, name="mix_in_rest")
    o_a = na_fwd(z_na, bias, NH, S)
    c_q = norm_fwd((z, QR, off_ql // QR), q_a_norm, name="q_a_norm")
    c_kv = norm_fwd((z, KVR, off_kvl // KVR), kv_a_norm, name="kv_a_norm")
    wuq = plain("w_uq", c_kv).reshape(QR, MH, MLA_QK)
    wuq_n = wuq[:, :, :MLA_NOPE].reshape(QR, MH * MLA_NOPE)
    wuq_r = jnp.pad(wuq[:, :, MLA_NOPE:], ((0, 0), (0, 0), (0, LANES - MLA_ROPE))).reshape(QR, MH * LANES)
    wukv = plain("w_ukv").reshape(KVR, MH, 2, HEAD_DIM)
    wuk = wukv[:, :, 0].reshape(KVR, MH * HEAD_DIM)
    wuv = wukv[:, :, 1].reshape(KVR, MH * HEAD_DIM)
    q_n = mm(c_q, wuq_n, name="mla_q_nope", out_dtype=BF16)
    q_r = rope(mm(c_q, wuq_r, name="mla_q_rope"), cos_t, sin_t, name="rope_q", out_dtype=BF16)
    k_n = mm(c_kv, wuk, name="mla_k_nope", out_dtype=BF16)
    v_m = mm(c_kv, wuv, name="mla_v", out_dtype=BF16)
    k_r = rope((z, LANES, off_kr // LANES), cos_t, sin_t, name="rope_k", out_dtype=BF16)
    o_b, lse = mla_fwd(q_n, q_r, k_n, v_m, k_r, MH, S)
    y_a = mm(o_a, stacked("w_branch_a"), name="branch_a", b_stack=True)
    y_b = mm(o_b, stacked("w_branch_b"), name="branch_b", b_stack=True)
    z_ga, z_gb = (z, D, 0), (z, D, 1)
    merged = rowwise(lambda ga, gb, ya, yb: _sig(ga) * ya + _sig(gb) * yb, [z_ga, z_gb, y_a, y_b], [], [(D, BF16)],
                     tm=256, name="merge")[0]
    h2 = mm(merged, plain("w_out"), name="mix_out", res=h1)
    h3, ffn2_saved = ffn_fwd(h2, ffn2_norm, "ffn2", "ffn2")
    n4 = norm_fwd(h3, pl_norm, name="pl_norm")
    pg_pre = mm(n4, plain("w_pl_gate", n4), name="pl_gate")
    pe = mm(ps, stacked("w_pl"), name="pl_embed", b_stack=True)

    def tail(h3b, pgb, peb, tb_, fg):
        pg = _sig(pgb)
        h4 = h3b + pg * peb
        r = _rstd(h4)
        xh = h4 * r
        err = xh * fg - tb_
        loss_rows = jnp.mean(err * err, axis=-1, keepdims=True)
        dy = err * (1.0 / D)
        dxh = dy * fg
        dh4 = r * (dxh - xh * jnp.mean(dxh * xh, axis=-1, keepdims=True))
        loss_part = jnp.broadcast_to(0.5 * jnp.sum(loss_rows, axis=0, keepdims=True), (1, LANES))
        return (dh4, dh4 * peb * pg * (1.0 - pg), dh4 * pg, loss_part, jnp.sum(dy * xh, axis=0, keepdims=True))

    dh4, dpg_pre, dpe, loss_part, g_final = rowwise(
        tail, [h3, pg_pre, pe, tgt], [final_norm.reshape(1, D)], [(D, F32), (D, BF16), (D, BF16)],
        accs=[(1, LANES), (1, D)], tm=128, name="loss_tail")
    loss = lax.psum(loss_part[0, 0], ("x", "y", "c"))

    G = {}
    pending = []

    def four(g):
        if g.ndim == 2:
            return g.reshape(N_CHIPS, 2, g.shape[0] // (2 * N_CHIPS), g.shape[1])
        return g.reshape(N_CHIPS, 2, g.shape[1] // 2, g.shape[2])

    def pair_begin(names, tag):
        g4 = [four(G[n]) for n in names]
        lands = [lax.empty((N_CHIPS,) + g.shape[2:], BF16) for g in g4]
        sems, g4, lands, token = pair_start(g4, lands, tag)
        return (names, tag, sems, g4, lands), token

    def reduce_go(begun, tag, after):
        names, sums = [], []
        for b_names, b_tag, sems, g4, lands in begun:
            g4, got = pair_wait(sems, g4, lands, list(after), b_tag)
            sums += [pair_sum(a_, r_, c_idx, n) for n, a_, r_ in zip(b_names, g4, got)]
            names += b_names
        lands = [lax.empty((N_CHIPS - 1,) + s_.shape[1:], BF16) for s_ in sums]
        sems, sums, lands, token = scatter_start(sums, lands, tag)
        pending.append((names, tag, sems, sums, lands))
        return token

    def reduce_finish(entry, after):
        names, tag, sems, sums, lands = entry
        sums, got = scatter_wait(sems, sums, lands, after, tag)
        halves = [chip_sum(a, b, j_idx, n) for n, a, b in zip(names, sums, got)]
        done = []
        for n, own, sib in zip(names, halves, half_exchange(halves, tag)):
            shp = W[n].shape
            two_d = lambda a_: a_.reshape(shp[1], shp[2])
            out = adamw_shard(two_d(W[n]), own, sib, two_d(Mo[n]), two_d(Vo[n]), c_idx, name="adamw_" + n)
            grads[n], delta[n], new_m[n], new_v[n] = [o.reshape(shp) for o in out]
            done.append(out[0])
        return done

    G["w_pl"] = mm(ps, dpe, name="pl_dw_embed", ta=True, out_dtype=BF16, out_stack=True)
    G["w_pl_gate"] = mm(n4, dpg_pre, name="pl_dw_gate", ta=True, out_dtype=BF16)
    begun_pl, token = pair_begin(["w_pl", "w_pl_gate"], "pl")
    dn4 = mm(dpg_pre, plain("w_pl_gate"), name="pl_dn", tb=True, after=[token])
    dh3, dh3_half, g_pl = norm_bwd(h3, pl_norm, dn4, name="pl_dnorm", res=dh4, bf16_alpha=0.5)
    token = reduce_go([begun_pl], "pl", [dh3])
    dh2, dh2_b, g_ffn2 = ffn_bwd(h2, ffn2_norm, ffn2_saved, dh3, dh3_half, "ffn2", "ffn2", last=False, after=[token])

    G["w_out"] = mm(merged, dh2_b, name="mix_dw_out", ta=True, out_dtype=BF16)
    dmerged = mm(dh2_b, plain("w_out"), name="mix_dmerged", tb=True)

    def merge_bwd(ga, gb, ya, yb, dm):
        sa, sb = _sig(ga), _sig(gb)
        dgates = jnp.concatenate([dm * ya * sa * (1.0 - sa), dm * yb * sb * (1.0 - sb)], axis=1)
        return dm * sa, dm * sb, dgates

    dy_a, dy_b, dz_rest = rowwise(merge_bwd, [z_ga, z_gb, y_a, y_b, dmerged], [],
                                  [(D, BF16), (D, BF16), (2 * D, BF16, 0)], tm=256, name="merge_bwd",
                                  into=(None, rest_w))
    G["w_branch_a"] = mm(o_a, dy_a, name="branch_a_dw", ta=True, out_dtype=BF16, out_stack=True)
    G["w_branch_b"] = mm(o_b, dy_b, name="branch_b_dw", ta=True, out_dtype=BF16, out_stack=True)
    do_a = mm(dy_a, stacked("w_branch_a"), name="branch_a_dx", tb=True, b_stack=True)
    do_b = mm(dy_b, stacked("w_branch_b"), name="branch_b_dx", tb=True, b_stack=True)
    dq_na, dk_na, dv_na, dbias = na_bwd(z_na, bias, do_a, NH, S)
    g_rpb = na_rpb_grad(dbias)
    dq_n, dq_rr, dk_n, dv_m, dk_rr = mla_bwd(q_n, q_r, k_n, v_m, k_r, lse, do_b, MH, S)
    dq_r = rope(dq_rr, cos_t, -sin_t, name="rope_q_bwd", out_dtype=BF16)
    dz_rest = rope(dk_rr, cos_t, -sin_t, name="rope_k_bwd", out_dtype=BF16, into=(dz_rest, rest_w),
                   cb=off_kr // kr_w, zero_cols=kr_w - LANES)
    gw_uq_n = mm(c_q, dq_n, name="mla_dw_q_nope", ta=True, out_dtype=BF16)
    gw_uq_r = mm(c_q, dq_r, name="mla_dw_q_rope", ta=True, out_dtype=BF16)
    dc_q = mm(dq_n, wuq_n, name="mla_dcq_nope", tb=True)
    dc_q = mm(dq_r, wuq_r, name="mla_dcq_rope", tb=True, res=dc_q)
    gw_uk = mm(c_kv, dk_n, name="mla_dw_k", ta=True, out_dtype=BF16)
    gw_uv = mm(c_kv, dv_m, name="mla_dw_v", ta=True, out_dtype=BF16)
    dc_kv = mm(dk_n, wuk, name="mla_dckv_k", tb=True)
    dc_kv = mm(dv_m, wuv, name="mla_dckv_v", tb=True, res=dc_kv)
    dz_rest, g_qa = norm_bwd((z, QR, off_ql // QR), q_a_norm, dc_q, name="q_a_dnorm", want_f32=False, bf16_alpha=1.0,
                             into=(dz_rest, rest_w), cb=off_ql // QR)
    dz_rest, g_kva = norm_bwd((z, KVR, off_kvl // KVR), kv_a_norm, dc_kv, name="kv_a_dnorm", want_f32=False,
                              bf16_alpha=1.0, into=(dz_rest, rest_w), cb=off_kvl // KVR)
    dz_na = jnp.concatenate([dq_na, dk_na, dv_na], axis=1)

    def to_stack(g2d):
        k, n = g2d.shape
        return g2d.reshape(k, N_CHIPS, n // N_CHIPS).transpose(1, 0, 2)

    gw_uq = jnp.concatenate([gw_uq_n.reshape(QR, MH, MLA_NOPE), gw_uq_r.reshape(QR, MH, LANES)[:, :, :MLA_ROPE]],
                            axis=2).reshape(QR, MH * MLA_QK)
    G["w_uq"] = to_stack(gw_uq)
    gw_ukv = jnp.stack([gw_uk.reshape(KVR, MH, HEAD_DIM), gw_uv.reshape(KVR, MH, HEAD_DIM)], axis=2)
    G["w_ukv"] = to_stack(gw_ukv.reshape(KVR, MH * 2 * HEAD_DIM))
    begun_mix, token = pair_begin(["w_out", "w_branch_a", "w_branch_b", "w_uq", "w_ukv"], "mix")
    gw_na = mm(u_mix, dz_na, name="mix_dw_in_na", ta=True, out_dtype=BF16, after=[token])
    gw_rest = mm(u_mix, dz_rest, name="mix_dw_in_rest", ta=True, out_dtype=BF16)
    G["w_in"] = w_in_shards(gw_na, gw_rest)
    begun_win, token_win = pair_begin(["w_in"], "win")
    token_mix = reduce_go([begun_mix], "mix", [gw_rest, token_win])
    du_mix = mm(dz_na, w_na, name="mix_du_na", tb=True, after=[token_mix])
    token_win = reduce_go([begun_win], "win", [du_mix])
    du_mix = mm(dz_rest, w_rest, name="mix_du_rest", tb=True, res=du_mix, after=[token_win])
    dh1, dh1_half, g_mix = norm_bwd(h1, mix_norm, du_mix, name="mix_dnorm", res=dh2, bf16_alpha=0.5)
    grad_x, g_ffn1 = ffn_bwd(xs, ffn1_norm, ffn1_saved, dh1, dh1_half, "ffn1", "ffn1", last=True)

    small_g = {"ffn1_norm": g_ffn1, "mix_norm": g_mix, "q_a_norm": g_qa, "kv_a_norm": g_kva, "na_rpb": g_rpb,
               "ffn2_norm": g_ffn2, "pl_norm": g_pl, "final_norm": g_final}
    sizes = [int(np.prod(W[n].shape)) for n in small]
    total = sum(sizes)
    padded = -(-total // (8 * LANES)) * (8 * LANES)

    def pack(parts):
        flat = jnp.concatenate([jnp.reshape(parts[n], (-1,)).astype(F32) for n in small]
                               + [jnp.zeros((padded - total,), F32)])
        return flat.reshape(padded // LANES, LANES)

    def unpack(a):
        flat, out, o = a.reshape(-1), {}, 0
        for n, sz in zip(small, sizes):
            out[n] = flat[o:o + sz].reshape(W[n].shape)
            o += sz
        return out

    grads, delta, new_m, new_v = {}, {}, {}, {}
    after = [grad_x]
    for entry in pending:
        after = reduce_finish(entry, after)

    g_small = sum_devices(gather_small(pack(small_g), after))
    d_small, m_small, v_small = adamw(pack(W), g_small, pack(Mo), pack(Vo), name="adamw_small")
    for full, part in ((grads, g_small), (delta, d_small), (new_m, m_small), (new_v, v_small)):
        full.update(unpack(part))

    return (loss, grad_x[None], *[grads[n] for n in order], *[delta[n] for n in order],
            *[new_m[n] for n in order], *[new_v[n] for n in order])
```

```python
import functools

import numpy as np
import jax
import jax.numpy as jnp
from jax import lax
from jax.experimental import pallas as pl
from jax.experimental.pallas import tpu as pltpu

F32 = jnp.float32
BF16 = jnp.bfloat16

VMEM_LIMIT_V7X = 56 * 1024 * 1024
VMEM_BUDGET_V7X = 40 * 1024 * 1024
LANES = 128

GRID_W = 64
NA_WIN_ROWS = 8
NA_WIN_COLS = 16
HEAD_DIM = 128
MLA_NOPE = 128
MLA_ROPE = 64
MLA_QK = MLA_NOPE + MLA_ROPE
ROPE_THETA = 10000.0
NORM_EPS = 1e-6
NEG_INF = -1e30
N_CHIPS = 4

ADAM_LR = 0.001
ADAM_B1 = 0.9
ADAM_B2 = 0.999
ADAM_EPS = 1e-08
ADAM_WD = 0.01
ADAM_STEP = 10

MESH = pl.DeviceIdType.MESH
ANY = pl.BlockSpec(memory_space=pl.ANY)


def _params(sem=None):
    return pltpu.CompilerParams(dimension_semantics=sem, vmem_limit_bytes=VMEM_LIMIT_V7X)


def _pick(n, target, align):
    best = None
    t = align
    while t <= min(n, target):
        if n % t == 0:
            best = t
        t += align
    return n if best is None else best


def mm(a, b, *, name, ta=False, tb=False, out_dtype=F32, res=None, alpha=1.0, b_stack=False, out_stack=False,
       exact=False, after=(), epilogue=None, epi_in=(), epi_out=()):
    K, M = (a.shape if ta else a.shape[::-1])
    nst = kb = nb = None
    if b_stack:
        nst = b.shape[0]
        if tb:
            N, kb = b.shape[1], b.shape[2]
            Kb = nst * kb
        else:
            Kb, nb = b.shape[1], b.shape[2]
            N = nst * nb
    else:
        N, Kb = (b.shape if tb else b.shape[::-1])
    assert K == Kb, (a.shape, b.shape, ta, tb)
    if out_stack:
        assert N % N_CHIPS == 0
    n_unit = N // N_CHIPS if out_stack else (nb if nb is not None else N)
    tn = _pick(n_unit, 512, LANES) if n_unit % 512 == 0 or n_unit <= 512 else _pick(n_unit, 1536, LANES)
    if ta and n_unit == N and 4 * K * N * jnp.dtype(b.dtype).itemsize <= VMEM_BUDGET_V7X:
        tn = N
    m_align = LANES if ta else 16
    tm = _pick(M, 1024, m_align)
    isz = lambda t: jnp.dtype(t.dtype).itemsize
    out_dtypes = list(epi_out) if epilogue is not None else [out_dtype]
    osz = sum(jnp.dtype(t).itemsize for t in out_dtypes) + sum(isz(e) for e in epi_in)

    def vmem(tm_, tn_):
        return (2 * tm_ * K * isz(a) + 2 * K * tn_ * isz(b) + 2 * tm_ * tn_ * osz + tm_ * tn_ * 4
                + (tm_ * K * 2 if ta else 0) + (2 * tm_ * tn_ * isz(res) if res is not None else 0))

    while vmem(tm, tn) > VMEM_BUDGET_V7X and tm % 2 == 0 and (tm // 2) % m_align == 0:
        tm //= 2
    while vmem(tm, tn) > VMEM_BUDGET_V7X and tn % 2 == 0 and (tn // 2) % LANES == 0 and n_unit % (tn // 2) == 0:
        tn //= 2
    assert vmem(tm, tn) <= VMEM_BUDGET_V7X, (name, tm, tn, K)

    a_spec = pl.BlockSpec((K, tm), lambda i, j: (0, i)) if ta else pl.BlockSpec((tm, K), lambda i, j: (i, 0))
    if b_stack and not tb:
        q = nb // tn
        b_spec = pl.BlockSpec((None, K, tn), lambda i, j: (j // q, 0, j % q))
    elif b_stack and tb:
        b_spec = pl.BlockSpec((nst, tn, kb), lambda i, j: (0, j, 0))
    elif tb:
        b_spec = pl.BlockSpec((tn, K), lambda i, j: (j, 0))
    else:
        b_spec = pl.BlockSpec((K, tn), lambda i, j: (0, j))
    if out_stack:
        qo = (N // N_CHIPS) // tn
        o_spec = pl.BlockSpec((None, tm, tn), lambda i, j: (j // qo, i, j % qo))
        o_shapes = [jax.ShapeDtypeStruct((N_CHIPS, M, N // N_CHIPS), t) for t in out_dtypes]
    else:
        o_spec = pl.BlockSpec((tm, tn), lambda i, j: (i, j))
        o_shapes = [jax.ShapeDtypeStruct((M, N), t) for t in out_dtypes]
    has_res = res is not None
    n_in = 2 + has_res + len(epi_in) + len(after)
    nn = (((1,), (0,)), ((), ()))
    nt = (((1,), (1,)), ((), ()))

    def body(*refs):
        a_ref, b_ref = refs[:2]
        r_ref = refs[2] if has_res else None
        e_refs = refs[2 + has_res:2 + has_res + len(epi_in)]
        o_refs = refs[n_in:n_in + len(out_dtypes)]
        if ta:
            at_ref = refs[-1]

            @pl.when(pl.program_id(1) == 0)
            def _():
                at_ref[...] = a_ref[...].astype(BF16).T

            lhs = at_ref[...]
        elif exact:
            lhs = a_ref[...]
        else:
            lhs = a_ref[...].astype(BF16)
        if exact:
            total = lax.dot_general(lhs, b_ref[...], nt if tb else nn, preferred_element_type=F32,
                                    precision=lax.Precision.HIGHEST)
        elif b_stack and tb:
            total = None
            for s in range(nst):
                part = lax.dot_general(lhs[:, s * kb:(s + 1) * kb], b_ref[s].astype(BF16), nt,
                                       preferred_element_type=F32)
                total = part if total is None else total + part
        else:
            total = lax.dot_general(lhs, b_ref[...].astype(BF16), nt if tb else nn, preferred_element_type=F32)
        if alpha != 1.0:
            total = total * alpha
        if has_res:
            total = total + r_ref[...].astype(F32)
        vals = epilogue(total, *[e[...] for e in e_refs]) if epilogue is not None else (total,)
        for o_ref, v in zip(o_refs, vals):
            o_ref[...] = v.astype(o_ref.dtype)

    tile = pl.BlockSpec((tm, tn), lambda i, j: (i, j))
    in_specs = [a_spec, b_spec] + [tile] * (has_res + len(epi_in)) + [ANY] * len(after)
    args = [a, b] + ([res] if has_res else []) + list(epi_in) + list(after)
    outs = pl.pallas_call(
        body, name=name, grid=(M // tm, N // tn), in_specs=in_specs, out_specs=[o_spec] * len(out_dtypes),
        out_shape=o_shapes, scratch_shapes=[pltpu.VMEM((tm, K), BF16)] if ta else [],
        compiler_params=_params(("parallel", "arbitrary")),
    )(*args)
    return outs if epilogue is not None else outs[0]


def rowwise(fn, rows, consts, outs, accs=(), *, tm, name, tn=None, into=None):
    rows = [r if isinstance(r, tuple) else (r, r.shape[1], 0) for r in rows]
    S = rows[0][0].shape[0]
    tm = _pick(S, tm, 16)
    nrow, ncon, nout = len(rows), len(consts), len(outs)
    outs = [o if len(o) == 3 else (o[0], o[1], None) for o in outs]
    if tn is None:
        grid = (S // tm,)
        in_specs = [pl.BlockSpec((tm, w), functools.partial(lambda i, cb: (i, cb), cb=cb)) for _, w, cb in rows]
        in_specs += [pl.BlockSpec(c.shape, lambda i: (0, 0)) for c in consts]
        out_specs = [pl.BlockSpec((tm, n), functools.partial(lambda i, cb: (i, cb), cb=cb or 0)) for n, _, cb in outs]
        out_specs += [pl.BlockSpec(s, lambda i: (0, 0)) for s in accs]
        sem = ("arbitrary",)
    else:
        assert not accs
        N = rows[0][1]
        grid = (S // tm, N // tn)
        in_specs = [pl.BlockSpec((tm, tn), lambda i, j: (i, j)) for _ in rows]
        in_specs += [pl.BlockSpec(c.shape, lambda i, j: (0, 0)) for c in consts]
        out_specs = [pl.BlockSpec((tm, tn), lambda i, j: (i, j)) for _ in outs]
        sem = ("parallel", "parallel")
    out_shape = [jax.ShapeDtypeStruct((S, n if cb is None else into[1]), dt) for n, dt, cb in outs]
    out_shape += [jax.ShapeDtypeStruct(s, F32) for s in accs]
    extra, aliases = [], {}
    if into is not None and into[0] is not None:
        extra = [into[0]]
        aliases = {nrow + ncon: [cb is not None for _, _, cb in outs].index(True)}

    def body(*refs):
        vals = fn(*[r[...] for r in refs[:nrow + ncon]])
        if not isinstance(vals, (tuple, list)):
            vals = (vals,)
        o_refs = refs[nrow + ncon + len(extra):]
        for o_ref, v in zip(o_refs[:nout], vals[:nout]):
            o_ref[...] = v.astype(o_ref.dtype)
        if accs:
            first = pl.program_id(0) == 0

            def accumulate(a_ref, v):
                @pl.when(first)
                def _():
                    a_ref[...] = v

                @pl.when(jnp.logical_not(first))
                def _():
                    a_ref[...] += v

            for a_ref, v in zip(o_refs[nout:], vals[nout:]):
                accumulate(a_ref, v.astype(F32))

    return pl.pallas_call(
        body, name=name, grid=grid, in_specs=in_specs + [ANY] * len(extra), out_specs=out_specs, out_shape=out_shape,
        input_output_aliases=aliases, compiler_params=_params(sem),
    )(*[r[0] for r in rows], *consts, *extra)


def _rstd(x):
    return lax.rsqrt(jnp.mean(x * x, axis=-1, keepdims=True) + NORM_EPS)


def norm_fwd(x, g, *, name, tm=256):
    w = x[1] if isinstance(x, tuple) else x.shape[1]

    def fn(xb, gb):
        return (xb * _rstd(xb)) * gb

    return rowwise(fn, [x], [g], [(w, BF16)], tm=tm, name=name)[0]


def norm_bwd(x, g, dn, *, name, res=None, want_f32=True, bf16_alpha=None, tm=256, into=None, cb=None):
    w = x[1] if isinstance(x, tuple) else x.shape[1]
    has_res = res is not None

    def fn(*blocks):
        if has_res:
            xb, dnb, rb, gb = blocks
        else:
            xb, dnb, gb = blocks
        r = _rstd(xb)
        xh = xb * r
        dxh = dnb * gb
        dx = r * (dxh - xh * jnp.mean(dxh * xh, axis=-1, keepdims=True))
        if has_res:
            dx = dx + rb
        out = []
        if want_f32:
            out.append(dx)
        if bf16_alpha is not None:
            out.append(dx * bf16_alpha if bf16_alpha != 1.0 else dx)
        out.append(jnp.sum(dnb * xh, axis=0, keepdims=True))
        return tuple(out)

    outs = ([(w, F32)] if want_f32 else []) + ([(w, BF16, cb)] if bf16_alpha is not None else [])
    rows = [x, dn] + ([res] if has_res else [])
    return rowwise(fn, rows, [g], outs, accs=[(1, w)], tm=tm, name=name, into=into)


def _sig(x):
    return jax.nn.sigmoid(x)


def swiglu_tile(ub, gb):
    return ub, gb * _sig(gb) * ub


def swiglu_bwd_tile(dab, gb, ub):
    sg = _sig(gb)
    return dab * ub * (sg + gb * sg * (1.0 - sg)), dab * (gb * sg)


def rope(x, cos, sin_signed, *, name, out_dtype, into=None, cb=None, zero_cols=0):
    w = x[1] if isinstance(x, tuple) else x.shape[1]
    half = MLA_ROPE // 2

    def fn(xb, cb, sb):
        lane = lax.broadcasted_iota(jnp.int32, cb.shape, 1)
        outs = []
        for hb in range(w // LANES):
            blk = xb[:, hb * LANES:(hb + 1) * LANES]
            partner = jnp.where(lane < half, pltpu.roll(blk, LANES - half, 1), pltpu.roll(blk, half, 1))
            outs.append(blk * cb + partner * sb)
        if zero_cols:
            outs.append(jnp.zeros((xb.shape[0], zero_cols), xb.dtype))
        return outs[0] if len(outs) == 1 else jnp.concatenate(outs, axis=1)

    return rowwise(fn, [x, cos, sin_signed], [], [(w + zero_cols, out_dtype, cb)], tm=256, name=name, into=into)[0]


def _na_tables():
    cols = np.arange(GRID_W)
    kw = NA_WIN_COLS
    dc = np.clip(cols[None, :] - cols[:, None], -(kw - 1), kw - 1) + (kw - 1)
    onehot = np.zeros((LANES, GRID_W * GRID_W), np.float32)
    onehot[dc.reshape(-1), np.arange(GRID_W * GRID_W)] = 1.0
    col_start = np.clip(cols - kw // 2, 0, GRID_W - kw)
    mask = (cols[None, :] >= col_start[:, None]) & (cols[None, :] < col_start[:, None] + kw)
    return onehot, np.where(mask, 0.0, NEG_INF).astype(np.float32)


def na_bias(rpb, after=()):
    H = rpb.shape[0]
    nr, kh = 2 * NA_WIN_ROWS - 1, NA_WIN_ROWS
    onehot, maskb = _na_tables()
    rp = jnp.pad(rpb.reshape(H * nr, 2 * NA_WIN_COLS - 1), ((0, 0), (0, LANES - (2 * NA_WIN_COLS - 1))))
    t1 = mm(rp, jnp.asarray(onehot), name="na_bias_table", exact=True, after=after).reshape(H, nr, GRID_W, GRID_W)
    t1 = t1 + jnp.asarray(maskb)[None, None]
    per_t = [jnp.stack([t1[:, i - t + kh - 1] for i in range(kh)], axis=2) for t in range(kh)]
    return jnp.stack(per_t, axis=1).reshape(H, kh, GRID_W, kh * GRID_W)


def na_rpb_grad(db):
    H = db.shape[0]
    nr, kh = 2 * NA_WIN_ROWS - 1, NA_WIN_ROWS
    onehot, _ = _na_tables()
    db = db.reshape(H, kh, GRID_W, kh, GRID_W)
    per_dr = []
    for dri in range(nr):
        terms = [db[:, t, :, dri - (kh - 1) + t, :] for t in range(kh) if 0 <= dri - (kh - 1) + t < kh]
        per_dr.append(functools.reduce(jnp.add, terms))
    dt1 = jnp.stack(per_dr, axis=1).reshape(H * nr, GRID_W * GRID_W)
    g = mm(dt1, jnp.asarray(onehot), name="na_rpb_grad", tb=True, exact=True)
    return g[:, :2 * NA_WIN_COLS - 1].reshape(H, nr, 2 * NA_WIN_COLS - 1)


def _na_first_row(r, rows):
    return jnp.clip(r - NA_WIN_ROWS // 2, 0, rows - NA_WIN_ROWS)


NA_ROWS_PER_STEP = 8


def _na_probs(q, k_ref, b_ref, r, rows):
    first = _na_first_row(r, rows)
    start = pl.multiple_of(first * GRID_W, GRID_W)
    k = k_ref[pl.ds(start, NA_WIN_ROWS * GRID_W), :]
    s = lax.dot_general(q, k, (((1,), (1,)), ((), ())), preferred_element_type=F32)
    s = s * (HEAD_DIM ** -0.5) + b_ref[r - first]
    m = jnp.max(s, axis=-1, keepdims=True)
    e = jnp.exp(s - m)
    return k, e / jnp.sum(e, axis=-1, keepdims=True), start, r - first


def na_fwd(z, bias, H, S):
    rows = S // GRID_W
    nkeys = NA_WIN_ROWS * GRID_W
    rb = _pick(rows, NA_ROWS_PER_STEP, 1)

    def body(q_ref, k_ref, v_ref, b_ref, o_ref):
        for j in range(rb):
            r = pl.program_id(1) * rb + j
            rows_j = pl.ds(j * GRID_W, GRID_W)
            _, p, start, _ = _na_probs(q_ref[rows_j, :], k_ref, b_ref, r, rows)
            v = v_ref[pl.ds(start, nkeys), :]
            o_ref[rows_j, :] = jnp.dot(p.astype(BF16), v, preferred_element_type=F32).astype(o_ref.dtype)

    return pl.pallas_call(
        body, name="na_fwd", grid=(H, rows // rb),
        in_specs=[pl.BlockSpec((rb * GRID_W, HEAD_DIM), lambda h, i: (i, h)),
                  pl.BlockSpec((S, HEAD_DIM), lambda h, i: (0, H + h)),
                  pl.BlockSpec((S, HEAD_DIM), lambda h, i: (0, 2 * H + h)),
                  pl.BlockSpec((None, NA_WIN_ROWS, GRID_W, nkeys), lambda h, i: (h, 0, 0, 0))],
        out_specs=pl.BlockSpec((rb * GRID_W, HEAD_DIM), lambda h, i: (i, h)),
        out_shape=jax.ShapeDtypeStruct((S, H * HEAD_DIM), BF16),
        compiler_params=_params(("parallel", "arbitrary")),
    )(z, z, z, bias)


def na_bwd(z, bias, do, H, S):
    rows = S // GRID_W
    nkeys = NA_WIN_ROWS * GRID_W
    rb = _pick(rows, NA_ROWS_PER_STEP, 1)
    tn_dims = (((0,), (0,)), ((), ()))

    def body(q_ref, k_ref, v_ref, b_ref, do_ref, dq_ref, dk_ref, dv_ref, db_ref, dk_acc, dv_acc):
        i = pl.program_id(1)

        @pl.when(i == 0)
        def _():
            dk_acc[...] = jnp.zeros_like(dk_acc)
            dv_acc[...] = jnp.zeros_like(dv_acc)
            db_ref[...] = jnp.zeros_like(db_ref)

        for j in range(rb):
            rows_j = pl.ds(j * GRID_W, GRID_W)
            q = q_ref[rows_j, :]
            k, p, start, t = _na_probs(q, k_ref, b_ref, i * rb + j, rows)
            keys = pl.ds(start, nkeys)
            dob = do_ref[rows_j, :].astype(BF16)
            dp = lax.dot_general(dob, v_ref[keys, :], (((1,), (1,)), ((), ())), preferred_element_type=F32)
            ds = p * (dp - jnp.sum(dp * p, axis=-1, keepdims=True))
            dsb = (ds * (HEAD_DIM ** -0.5)).astype(BF16)
            dq_ref[rows_j, :] = jnp.dot(dsb, k, preferred_element_type=F32).astype(dq_ref.dtype)
            dk_acc[keys, :] += lax.dot_general(dsb, q, tn_dims, preferred_element_type=F32)
            dv_acc[keys, :] += lax.dot_general(p.astype(BF16), dob, tn_dims, preferred_element_type=F32)
            db_ref[t] += ds

        @pl.when(i == rows // rb - 1)
        def _():
            dk_ref[...] = dk_acc[...].astype(dk_ref.dtype)
            dv_ref[...] = dv_acc[...].astype(dv_ref.dtype)

    W = H * HEAD_DIM
    qspec = pl.BlockSpec((rb * GRID_W, HEAD_DIM), lambda h, i: (i, h))
    bspec = pl.BlockSpec((None, NA_WIN_ROWS, GRID_W, nkeys), lambda h, i: (h, 0, 0, 0))
    return pl.pallas_call(
        body, name="na_bwd", grid=(H, rows // rb),
        in_specs=[qspec, pl.BlockSpec((S, HEAD_DIM), lambda h, i: (0, H + h)),
                  pl.BlockSpec((S, HEAD_DIM), lambda h, i: (0, 2 * H + h)), bspec, qspec],
        out_specs=[qspec, pl.BlockSpec((S, HEAD_DIM), lambda h, i: (0, h)),
                   pl.BlockSpec((S, HEAD_DIM), lambda h, i: (0, h)), bspec],
        out_shape=[jax.ShapeDtypeStruct((S, W), BF16)] * 3 + [jax.ShapeDtypeStruct((H, NA_WIN_ROWS, GRID_W, nkeys), F32)],
        scratch_shapes=[pltpu.VMEM((S, HEAD_DIM), F32)] * 2,
        compiler_params=_params(("arbitrary", "arbitrary")),
    )(z, z, z, bias, do)


def _mla_keys(kn_ref, kr_ref, kcat):
    @pl.when(pl.program_id(1) == 0)
    def _():
        kcat[:, :HEAD_DIM] = kn_ref[...]
        kcat[:, HEAD_DIM:] = kr_ref[...]


MLA_LOG2_SCALE = (MLA_QK ** -0.5) * 1.4426950408889634


def _mla_scores(qn_ref, qr_ref, kcat):
    qcat = jnp.concatenate([qn_ref[...], qr_ref[...]], axis=1)
    return qcat, lax.dot_general(qcat, kcat[...], (((1,), (1,)), ((), ())), preferred_element_type=F32)


def mla_fwd(qn, qr, kn, v, kr, H, S):
    tq = _pick(S, 256, 16)

    def body(qn_ref, qr_ref, kn_ref, v_ref, kr_ref, o_ref, lse_ref, kcat):
        _mla_keys(kn_ref, kr_ref, kcat)
        _, s = _mla_scores(qn_ref, qr_ref, kcat)
        m = jnp.max(s, axis=-1, keepdims=True)
        e = jnp.exp2((s - m) * MLA_LOG2_SCALE)
        l = jnp.sum(e, axis=-1, keepdims=True)
        o = jnp.dot(e.astype(BF16), v_ref[...], preferred_element_type=F32)
        o_ref[...] = (o / l).astype(o_ref.dtype)
        lse_ref[...] = jnp.broadcast_to(m * MLA_LOG2_SCALE + jnp.log2(l), lse_ref.shape)

    qspec = pl.BlockSpec((tq, HEAD_DIM), lambda h, i: (i, h))
    kspec = pl.BlockSpec((S, HEAD_DIM), lambda h, i: (0, h))
    return pl.pallas_call(
        body, name="mla_fwd", grid=(H, S // tq),
        in_specs=[qspec, qspec, kspec, kspec, pl.BlockSpec((S, LANES), lambda h, i: (0, 0))],
        out_specs=[qspec, qspec],
        out_shape=[jax.ShapeDtypeStruct((S, H * HEAD_DIM), BF16), jax.ShapeDtypeStruct((S, H * LANES), F32)],
        scratch_shapes=[pltpu.VMEM((S, 2 * HEAD_DIM), BF16)],
        compiler_params=_params(("parallel", "arbitrary")),
    )(qn, qr, kn, v, kr)


def mla_bwd(qn, qr, kn, v, kr, lse, do, H, S):
    tq = _pick(S, 256, 16)
    nt = (((1,), (1,)), ((), ()))
    tn_dims = (((0,), (0,)), ((), ()))

    def body(qn_ref, qr_ref, kn_ref, v_ref, kr_ref, lse_ref, do_ref, dqn_ref, dqr_ref, dkn_ref, dv_ref, dkr_ref, kcat):
        h, i = pl.program_id(0), pl.program_id(1)
        _mla_keys(kn_ref, kr_ref, kcat)
        qcat, s = _mla_scores(qn_ref, qr_ref, kcat)
        p = jnp.exp2(s * MLA_LOG2_SCALE - lse_ref[:, 0:1])
        dob = do_ref[...].astype(BF16)
        dp = lax.dot_general(dob, v_ref[...], nt, preferred_element_type=F32)
        ds = p * (dp - jnp.sum(dp * p, axis=-1, keepdims=True))
        dsb = (ds * (MLA_QK ** -0.5)).astype(BF16)
        dq = jnp.dot(dsb, kcat[...], preferred_element_type=F32)
        dqn_ref[...] = dq[:, :HEAD_DIM].astype(dqn_ref.dtype)
        dqr_ref[...] = dq[:, HEAD_DIM:].astype(dqr_ref.dtype)

        @pl.when(i == 0)
        def _():
            dkn_ref[...] = jnp.zeros_like(dkn_ref)
            dv_ref[...] = jnp.zeros_like(dv_ref)

        @pl.when(jnp.logical_and(i == 0, h == 0))
        def _():
            dkr_ref[...] = jnp.zeros_like(dkr_ref)

        dk = lax.dot_general(dsb, qcat, tn_dims, preferred_element_type=F32)
        dkn_ref[...] += dk[:, :HEAD_DIM]
        dkr_ref[...] += dk[:, HEAD_DIM:]
        dv_ref[...] += lax.dot_general(p.astype(BF16), dob, tn_dims, preferred_element_type=F32)

    qspec = pl.BlockSpec((tq, HEAD_DIM), lambda h, i: (i, h))
    kspec = pl.BlockSpec((S, HEAD_DIM), lambda h, i: (0, h))
    rspec = pl.BlockSpec((S, LANES), lambda h, i: (0, 0))
    W = H * HEAD_DIM
    return pl.pallas_call(
        body, name="mla_bwd", grid=(H, S // tq),
        in_specs=[qspec, qspec, kspec, kspec, rspec, qspec, qspec],
        out_specs=[qspec, qspec, kspec, kspec, rspec],
        out_shape=[jax.ShapeDtypeStruct((S, W), BF16), jax.ShapeDtypeStruct((S, W), F32),
                   jax.ShapeDtypeStruct((S, W), F32), jax.ShapeDtypeStruct((S, W), F32),
                   jax.ShapeDtypeStruct((S, LANES), F32)],
        scratch_shapes=[pltpu.VMEM((S, 2 * HEAD_DIM), BF16)],
        compiler_params=_params(("arbitrary", "arbitrary")),
    )(qn, qr, kn, v, kr, lse, do)


def _place():
    return lax.axis_index("x"), lax.axis_index("y"), lax.axis_index("c")


def _other_chips(x, y):
    return [(1 - x, y), (x, 1 - y), (1 - x, 1 - y)]


def _remote(src, dst, send_sem, recv_sem, to):
    return pltpu.make_async_remote_copy(src_ref=src, dst_ref=dst, send_sem=send_sem, recv_sem=recv_sem,
                                        device_id=to, device_id_type=MESH)


HBM = pl.BlockSpec(memory_space=pltpu.HBM)
SEM = pl.BlockSpec(memory_space=pltpu.SEMAPHORE)
EFFECT = pltpu.SideEffectType.DATAFLOW_SIDE_EFFECTING


def _in_hbm(a):
    return pltpu.with_memory_space_constraint(a, pltpu.HBM)


TOKEN = jax.ShapeDtypeStruct((8, LANES), F32)
IN_VMEM = pl.BlockSpec(memory_space=pltpu.VMEM)


def cast_own_block(w, me_idx, after, name):
    _, k, n = w.shape
    rows = k // 2
    tm = _row_tile(rows, n, 4 + 2)
    nb = rows // tm

    def body(me_ref, w_ref, *rest):
        rest[len(after)][...] = w_ref[...].astype(BF16)

    gs = pltpu.PrefetchScalarGridSpec(
        num_scalar_prefetch=1, grid=(2, nb),
        in_specs=[pl.BlockSpec((None, tm, n), lambda h, i, me_ref: (0, h * nb + i, 0))] + [ANY] * len(after),
        out_specs=pl.BlockSpec((None, None, tm, n), lambda h, i, me_ref: (me_ref[0], h, i, 0)))
    return pl.pallas_call(body, name=name, grid_spec=gs, out_shape=jax.ShapeDtypeStruct((N_CHIPS, 2, rows, n), BF16),
                          compiler_params=_params(("arbitrary", "arbitrary")))(me_idx, w, *after)


def gather_start(landings, after, tag):
    n = len(landings)

    def body(*refs):
        lands = refs[:n]
        send, recv = refs[n + len(after)], refs[n + len(after) + 1]
        token = refs[-1]
        x, y, c = _place()
        me = 2 * x + y
        for w in range(n):
            for k, (px, py) in enumerate(_other_chips(x, y)):
                blk = lands[w].at[me, c]
                _remote(blk, blk, send.at[3 * w + k], recv.at[3 * w + k], (px, py, c)).start()
        token[...] = jnp.zeros_like(token)

    outs = pl.pallas_call(
        body, name="gather_start_" + tag,
        out_shape=(pltpu.SemaphoreType.DMA((3 * n,)),) * 2 + tuple(pltpu.HBM(b.shape, b.dtype) for b in landings)
        + (TOKEN,),
        in_specs=[HBM] * n + [ANY] * len(after), out_specs=tuple([SEM, SEM] + [HBM] * n + [IN_VMEM]),
        input_output_aliases={i: 2 + i for i in range(n)},
        compiler_params=pltpu.CompilerParams(has_side_effects=EFFECT),
    )(*[_in_hbm(b) for b in landings], *after)
    return (outs[0], outs[1]), outs[2:2 + n], outs[-1]


def gather_wait(sems, landings, after, tag):
    n = len(landings)
    send, recv = sems

    def body(*refs):
        lands = refs[:n]
        send_sem, recv_sem = refs[n], refs[n + 1]
        x, y, c = _place()
        me = 2 * x + y
        for w in range(n):
            for k, (px, py) in enumerate(_other_chips(x, y)):
                cp = _remote(lands[w].at[me, c], lands[w].at[2 * px + py, c], send_sem.at[3 * w + k],
                             recv_sem.at[3 * w + k], (px, py, c))
                cp.wait_send()
                cp.wait_recv()

    return pl.pallas_call(
        body, name="gather_wait_" + tag, out_shape=tuple(pltpu.HBM(b.shape, b.dtype) for b in landings),
        in_specs=[HBM] * n + [SEM, SEM] + [ANY] * len(after), out_specs=tuple([HBM] * n),
        input_output_aliases={i: i for i in range(n)},
        compiler_params=pltpu.CompilerParams(has_side_effects=EFFECT),
    )(*landings, send, recv, *after)


def gather_forward(landings, tag):
    n = len(landings)

    def body(*refs):
        ins, outs = refs[:n], refs[n:2 * n]
        send, recv = refs[2 * n:]
        x, y, c = _place()
        sibling = (x, y, 1 - c)
        cps = []
        for w in range(n):
            for k, (px, py) in enumerate(_other_chips(x, y)):
                j = 2 * px + py
                cp = _remote(ins[w].at[j, c], outs[w].at[j, c], send.at[3 * w + k], recv.at[3 * w + k], sibling)
                cp.start()
                cps.append(cp)
        for w in range(n):
            for k, (px, py) in enumerate(_other_chips(x, y)):
                blk = outs[w].at[2 * px + py, 1 - c]
                _remote(blk, blk, send.at[3 * w + k], recv.at[3 * w + k], sibling).wait_recv()
        for cp in cps:
            cp.wait_send()

    return pl.pallas_call(
        body, name="gather_forward_" + tag, in_specs=[ANY] * n, out_specs=[ANY] * n,
        out_shape=[jax.ShapeDtypeStruct(a.shape, a.dtype) for a in landings],
        input_output_aliases={i: i for i in range(n)},
        scratch_shapes=[pltpu.SemaphoreType.DMA((3 * n,)), pltpu.SemaphoreType.DMA((3 * n,))],
    )(*landings)


def pair_start(grads, landings, tag):
    n = len(grads)

    def body(*refs):
        ins, lands = refs[:n], refs[n:2 * n]
        send, recv = refs[2 * n], refs[2 * n + 1]
        token = refs[-1]
        x, y, c = _place()
        for w in range(n):
            _remote(ins[w].at[:, 1 - c], lands[w], send.at[w], recv.at[w], (x, y, 1 - c)).start()
        token[...] = jnp.zeros_like(token)

    bufs = list(grads) + list(landings)
    outs = pl.pallas_call(
        body, name="pair_start_" + tag,
        out_shape=(pltpu.SemaphoreType.DMA((n,)),) * 2 + tuple(pltpu.HBM(b.shape, b.dtype) for b in bufs) + (TOKEN,),
        in_specs=[HBM] * (2 * n), out_specs=tuple([SEM, SEM] + [HBM] * (2 * n) + [IN_VMEM]),
        input_output_aliases={i: 2 + i for i in range(2 * n)},
        compiler_params=pltpu.CompilerParams(has_side_effects=EFFECT),
    )(*[_in_hbm(b) for b in bufs])
    return (outs[0], outs[1]), outs[2:2 + n], outs[2 + n:2 + 2 * n], outs[-1]


def pair_wait(sems, grads, landings, after, tag):
    n = len(grads)

    def body(*refs):
        ins, lands = refs[:n], refs[n:2 * n]
        send, recv = refs[2 * n], refs[2 * n + 1]
        x, y, c = _place()
        for w in range(n):
            cp = _remote(ins[w].at[:, 1 - c], lands[w], send.at[w], recv.at[w], (x, y, 1 - c))
            cp.wait_send()
            cp.wait_recv()

    bufs = list(grads) + list(landings)
    outs = pl.pallas_call(
        body, name="pair_wait_" + tag, out_shape=tuple(pltpu.HBM(b.shape, b.dtype) for b in bufs),
        in_specs=[HBM] * (2 * n) + [SEM, SEM] + [ANY] * len(after), out_specs=tuple([HBM] * (2 * n)),
        input_output_aliases={i: i for i in range(2 * n)},
        compiler_params=pltpu.CompilerParams(has_side_effects=EFFECT),
    )(*bufs, sems[0], sems[1], *after)
    return outs[:n], outs[n:]


def scatter_start(sums, landings, tag):
    n = len(sums)

    def body(*refs):
        ins, lands = refs[:n], refs[n:2 * n]
        send, recv = refs[2 * n], refs[2 * n + 1]
        token = refs[-1]
        x, y, c = _place()
        for w in range(n):
            for k, (px, py) in enumerate(_other_chips(x, y)):
                _remote(ins[w].at[2 * px + py], lands[w].at[k], send.at[3 * w + k], recv.at[3 * w + k], (px, py, c)).start()
        token[...] = jnp.zeros_like(token)

    bufs = list(sums) + list(landings)
    outs = pl.pallas_call(
        body, name="scatter_start_" + tag,
        out_shape=(pltpu.SemaphoreType.DMA((3 * n,)),) * 2 + tuple(pltpu.HBM(b.shape, b.dtype) for b in bufs) + (TOKEN,),
        in_specs=[HBM] * (2 * n), out_specs=tuple([SEM, SEM] + [HBM] * (2 * n) + [IN_VMEM]),
        input_output_aliases={i: 2 + i for i in range(2 * n)},
        compiler_params=pltpu.CompilerParams(has_side_effects=EFFECT),
    )(*[_in_hbm(b) for b in bufs])
    return (outs[0], outs[1]), outs[2:2 + n], outs[2 + n:2 + 2 * n], outs[-1]


def scatter_wait(sems, sums, landings, after, tag):
    n = len(sums)

    def body(*refs):
        ins, lands = refs[:n], refs[n:2 * n]
        send, recv = refs[2 * n], refs[2 * n + 1]
        x, y, c = _place()
        for w in range(n):
            for k, (px, py) in enumerate(_other_chips(x, y)):
                cp = _remote(ins[w].at[2 * px + py], lands[w].at[k], send.at[3 * w + k], recv.at[3 * w + k], (px, py, c))
                cp.wait_send()
                cp.wait_recv()

    bufs = list(sums) + list(landings)
    outs = pl.pallas_call(
        body, name="scatter_wait_" + tag, out_shape=tuple(pltpu.HBM(b.shape, b.dtype) for b in bufs),
        in_specs=[HBM] * (2 * n) + [SEM, SEM] + [ANY] * len(after), out_specs=tuple([HBM] * (2 * n)),
        input_output_aliases={i: i for i in range(2 * n)},
        compiler_params=pltpu.CompilerParams(has_side_effects=EFFECT),
    )(*bufs, sems[0], sems[1], *after)
    return outs[:n], outs[n:]


def half_exchange(halves, tag):
    n = len(halves)

    def body(*refs):
        ins, outs = refs[:n], refs[n:2 * n]
        send, recv = refs[2 * n:]
        x, y, c = _place()
        cps = []
        for w in range(n):
            cp = _remote(ins[w], outs[w], send.at[w], recv.at[w], (x, y, 1 - c))
            cp.start()
            cps.append(cp)
        for cp in cps:
            cp.wait()

    return pl.pallas_call(
        body, name="grad_half_exchange_" + tag, in_specs=[ANY] * n, out_specs=[ANY] * n,
        out_shape=[jax.ShapeDtypeStruct(h.shape, h.dtype) for h in halves],
        scratch_shapes=[pltpu.SemaphoreType.DMA((n,)), pltpu.SemaphoreType.DMA((n,))],
    )(*halves)


def gather_small(v, after=()):
    def body(*refs):
        v_ref = refs[0]
        o_ref, send, recv, local = refs[1 + len(after):]
        x, y, c = _place()
        me = 4 * x + 2 * y + c
        own = pltpu.make_async_copy(v_ref, o_ref.at[me], local)
        own.start()
        cps = []
        for k in range(1, 8):
            fx, fy, fc = (k >> 2) & 1, (k >> 1) & 1, k & 1
            to = (x ^ fx if fx else x, y ^ fy if fy else y, c ^ fc if fc else c)
            cp = _remote(v_ref, o_ref.at[me], send.at[k - 1], recv.at[k - 1], to)
            cp.start()
            cps.append(cp)
        for k in range(1, 8):
            fx, fy, fc = (k >> 2) & 1, (k >> 1) & 1, k & 1
            px, py, pc = (x ^ fx if fx else x, y ^ fy if fy else y, c ^ fc if fc else c)
            cps[k - 1].wait_send()
            _remote(v_ref, o_ref.at[4 * px + 2 * py + pc], send.at[k - 1], recv.at[k - 1], (px, py, pc)).wait_recv()
        own.wait()

    return pl.pallas_call(
        body, name="gather_small_grads", in_specs=[ANY] * (1 + len(after)), out_specs=ANY,
        out_shape=jax.ShapeDtypeStruct((8,) + v.shape, v.dtype),
        scratch_shapes=[pltpu.SemaphoreType.DMA((7,)), pltpu.SemaphoreType.DMA((7,)), pltpu.SemaphoreType.DMA],
    )(v, *after)


def _row_tile(rows, cols, nbuf_bytes):
    tm = _pick(rows, 512, 16)
    while tm * cols * nbuf_bytes * 2 > VMEM_BUDGET_V7X and tm % 32 == 0:
        tm //= 2
    return tm


def pair_sum(g, r, c_idx, tag):
    _, _, rows, cols = g.shape
    tm = _row_tile(rows, cols, 2 + 2 + 2)
    nb = rows // tm

    def body(c_ref, g_ref, r_ref, o_ref):
        o_ref[...] = (g_ref[...].astype(F32) + r_ref[...].astype(F32)).astype(o_ref.dtype)

    gs = pltpu.PrefetchScalarGridSpec(
        num_scalar_prefetch=1, grid=(N_CHIPS, nb),
        in_specs=[pl.BlockSpec((None, None, tm, cols), lambda j, i, c_ref: (j, c_ref[0], i, 0)),
                  pl.BlockSpec((None, tm, cols), lambda j, i, c_ref: (j, i, 0))],
        out_specs=pl.BlockSpec((None, tm, cols), lambda j, i, c_ref: (j, i, 0)))
    return pl.pallas_call(body, name="grad_pair_sum_" + tag, grid_spec=gs,
                          out_shape=jax.ShapeDtypeStruct(r.shape, BF16),
                          compiler_params=_params(("arbitrary", "arbitrary")))(c_idx, g, r)


def chip_sum(s, r, j_idx, tag):
    _, rows, cols = s.shape
    tm = _row_tile(rows, cols, 2 + 3 * 2 + 4)
    nb = rows // tm

    def body(j_ref, s_ref, r_ref, o_ref):
        t = s_ref[...].astype(F32)
        for k in range(3):
            t = t + r_ref[k].astype(F32)
        o_ref[...] = t

    gs = pltpu.PrefetchScalarGridSpec(
        num_scalar_prefetch=1, grid=(nb,),
        in_specs=[pl.BlockSpec((None, tm, cols), lambda i, j_ref: (j_ref[0], i, 0)),
                  pl.BlockSpec((3, tm, cols), lambda i, j_ref: (0, i, 0))],
        out_specs=pl.BlockSpec((tm, cols), lambda i, j_ref: (i, 0)))
    return pl.pallas_call(body, name="grad_chip_sum_" + tag, grid_spec=gs,
                          out_shape=jax.ShapeDtypeStruct((rows, cols), F32),
                          compiler_params=_params(("arbitrary",)))(j_idx, s, r)


def adamw(w, g, m, v, *, name):
    rows, cols = w.shape
    tm = _row_tile(rows, cols, 7 * 4)

    return rowwise(_adamw_math, [w, g, m, v], [], [(cols, F32)] * 3, tm=tm, name=name)


def _adamw_math(wb, gb, mb, vb):
    m2 = ADAM_B1 * mb + (1.0 - ADAM_B1) * gb
    v2 = ADAM_B2 * vb + (1.0 - ADAM_B2) * (gb * gb)
    m_hat = m2 / (1.0 - ADAM_B1 ** ADAM_STEP)
    v_hat = v2 / (1.0 - ADAM_B2 ** ADAM_STEP)
    delta = -ADAM_LR * (m_hat / (jnp.sqrt(v_hat) + ADAM_EPS) + ADAM_WD * wb)
    return delta, m2, v2


def adamw_shard(w, g_own, g_sib, m, v, c_idx, *, name):
    rows, cols = g_own.shape
    tm = _row_tile(rows, cols, 9 * 4)
    nb = rows // tm

    def body(c_ref, w_ref, go_ref, gs_ref, m_ref, v_ref, g_out, d_out, m_out, v_out):
        gb = jnp.where(pl.program_id(0) == c_ref[0], go_ref[...], gs_ref[...])
        delta, m2, v2 = _adamw_math(w_ref[...], gb, m_ref[...], v_ref[...])
        g_out[...] = gb
        d_out[...] = delta
        m_out[...] = m2
        v_out[...] = v2

    full = pl.BlockSpec((tm, cols), lambda h, i, c_ref: (h * nb + i, 0))
    own = pl.BlockSpec((tm, cols), lambda h, i, c_ref: (jnp.where(h == c_ref[0], i, 0), 0))
    sib = pl.BlockSpec((tm, cols), lambda h, i, c_ref: (jnp.where(h == c_ref[0], 0, i), 0))
    gs = pltpu.PrefetchScalarGridSpec(num_scalar_prefetch=1, grid=(2, nb), in_specs=[full, own, sib, full, full],
                                      out_specs=[full] * 4)
    return pl.pallas_call(body, name=name, grid_spec=gs, out_shape=[jax.ShapeDtypeStruct(w.shape, F32)] * 4,
                          compiler_params=_params(("arbitrary", "arbitrary")))(c_idx, w, g_own, g_sib, m, v)


def sum_devices(a):
    def body(a_ref, o_ref):
        t = a_ref[0]
        for k in range(1, 8):
            t = t + a_ref[k]
        o_ref[...] = t

    return pl.pallas_call(body, name="sum_small_grads", out_shape=jax.ShapeDtypeStruct(a.shape[1:], a.dtype))(a)


def kernel(x, p, ffn1_norm, ffn1_w_gate, ffn1_w_up, ffn1_w_down, mix_norm, w_in, q_a_norm, w_uq, kv_a_norm, w_ukv, na_rpb, w_branch_a, w_branch_b, w_out, ffn2_norm, ffn2_w_gate, ffn2_w_up, ffn2_w_down, pl_norm, w_pl, w_pl_gate, final_norm, loss_target, m_ffn1_norm, m_ffn1_w_gate, m_ffn1_w_up, m_ffn1_w_down, m_mix_norm, m_w_in, m_q_a_norm, m_w_uq, m_kv_a_norm, m_w_ukv, m_na_rpb, m_w_branch_a, m_w_branch_b, m_w_out, m_ffn2_norm, m_ffn2_w_gate, m_ffn2_w_up, m_ffn2_w_down, m_pl_norm, m_w_pl, m_w_pl_gate, m_final_norm, v_ffn1_norm, v_ffn1_w_gate, v_ffn1_w_up, v_ffn1_w_down, v_mix_norm, v_w_in, v_q_a_norm, v_w_uq, v_kv_a_norm, v_w_ukv, v_na_rpb, v_w_branch_a, v_w_branch_b, v_w_out, v_ffn2_norm, v_ffn2_w_gate, v_ffn2_w_up, v_ffn2_w_down, v_pl_norm, v_w_pl, v_w_pl_gate, v_final_norm):
    big = ["ffn1_w_gate", "ffn1_w_up", "ffn1_w_down", "w_in", "w_uq", "w_ukv", "w_branch_a", "w_branch_b", "w_out",
           "ffn2_w_gate", "ffn2_w_up", "ffn2_w_down", "w_pl", "w_pl_gate"]
    col_sharded = {"ffn1_w_gate", "ffn1_w_up", "w_in", "w_uq", "w_ukv", "w_branch_a", "w_branch_b", "ffn2_w_gate",
                   "ffn2_w_up", "w_pl"}
    small = ["ffn1_norm", "mix_norm", "q_a_norm", "kv_a_norm", "na_rpb", "ffn2_norm", "pl_norm", "final_norm"]
    order = ["ffn1_norm", "ffn1_w_gate", "ffn1_w_up", "ffn1_w_down", "mix_norm", "w_in", "q_a_norm", "w_uq",
             "kv_a_norm", "w_ukv", "na_rpb", "w_branch_a", "w_branch_b", "w_out", "ffn2_norm", "ffn2_w_gate",
             "ffn2_w_up", "ffn2_w_down", "pl_norm", "w_pl", "w_pl_gate", "final_norm"]
    env = dict(locals())
    W = {n: env[n] for n in order}
    Mo = {n: env["m_" + n] for n in order}
    Vo = {n: env["v_" + n] for n in order}

    xs = x[0]
    S, D = xs.shape
    tgt = loss_target[0]
    ps = p[0, 0]
    NAW = w_branch_a.shape[1]
    MLAW = w_branch_b.shape[1]
    NH, MH = NAW // HEAD_DIM, MLAW // HEAD_DIM
    QR, KVR = w_uq.shape[1], w_ukv.shape[1]
    F = ffn1_w_down.shape[1] * N_CHIPS
    cx, cy, cc = _place()
    c_idx = jnp.reshape(cc, (1,)).astype(jnp.int32)
    j_idx = jnp.reshape(2 * cx + cy, (1,)).astype(jnp.int32)

    groups = [["ffn1_w_gate"], ["ffn1_w_up"], ["ffn1_w_down"], ["w_in"],
              ["w_uq", "w_ukv", "w_branch_a", "w_branch_b", "w_out"],
              ["ffn2_w_gate", "ffn2_w_up", "ffn2_w_down"], ["w_pl", "w_pl_gate"]]
    started, tokens = [], []
    for g, members in enumerate(groups):
        landings = [cast_own_block(W[n], j_idx, tokens[-1:], "cast_" + n) for n in members]
        sems, landings, token = gather_start(landings, tokens[-1:], str(g))
        started.append((sems, landings))
        tokens.append(token)
    gathered = {}

    def arrive(n, after):
        g = [n in members for members in groups].index(True)
        sems, landings = started[g]
        after = list(after) if isinstance(after, (list, tuple)) else [after]
        landed = gather_wait(sems, landings, after, str(g))
        gathered.update(zip(groups[g], gather_forward(landed, str(g))))

    def stacked(n, after=None):
        if n not in gathered:
            arrive(n, after)
        g = gathered[n]
        return g.reshape(N_CHIPS, 2 * g.shape[2], g.shape[3])

    def plain(n, after=None):
        if n in col_sharded:
            st = stacked(n, after)
            return st.transpose(1, 0, 2).reshape(st.shape[1], N_CHIPS * st.shape[2])
        if n not in gathered:
            arrive(n, after)
        g = gathered[n]
        return g.reshape(N_CHIPS * 2 * g.shape[2], g.shape[3])

    n_na = 3 * NAW
    n_front = n_na + QR + KVR
    n_in = n_front + MLA_ROPE + 2 * D
    off_ql, off_kvl, off_kr = 2 * D, 2 * D + QR, 2 * D + QR + KVR
    kr_w = 2 * LANES
    rest_w = off_kr + kr_w
    rest_ranges = [(n_front + MLA_ROPE, n_in), (n_na, n_front), (n_front, n_front + MLA_ROPE)]

    def shard_cols(st, lo, hi):
        nb, parts = st.shape[2], []
        while lo < hi:
            j = lo // nb
            end = min(hi, (j + 1) * nb)
            parts.append(st[j][:, lo - j * nb:end - j * nb])
            lo = end
        return parts

    def w_in_shards(g_na, g_rest):
        pieces = [(0, n_na, g_na, 0)]
        o = 0
        for lo, hi in rest_ranges:
            pieces.append((lo, hi, g_rest, o))
            o += hi - lo
        nb, shards = n_in // N_CHIPS, []
        for j in range(N_CHIPS):
            parts = []
            for lo, hi, src, o in sorted(pieces):
                a, b = max(lo, j * nb), min(hi, (j + 1) * nb)
                if a < b:
                    parts.append(src[:, o + a - lo:o + b - lo])
            shards.append(jnp.concatenate(parts, axis=1))
        return jnp.stack(shards)

    pos = jnp.arange(S, dtype=F32)
    inv_freq = 1.0 / (ROPE_THETA ** (jnp.arange(0, MLA_ROPE, 2, dtype=F32) / MLA_ROPE))
    ang = pos[:, None] * inv_freq[None, :]
    zpad = jnp.zeros((S, LANES - MLA_ROPE), F32)
    cos_t = jnp.concatenate([jnp.cos(ang), jnp.cos(ang), zpad], axis=1)
    sin_t = jnp.concatenate([-jnp.sin(ang), jnp.sin(ang), zpad], axis=1)

    def ffn_fwd(h, norm_g, tag, pre, after=()):
        n = norm_fwd(h, norm_g, name=f"{tag}_norm")
        g = mm(n, stacked(pre + "_w_gate", [n, *after]), name=f"{tag}_gate", b_stack=True)
        u, a = mm(n, stacked(pre + "_w_up", g), name=f"{tag}_up", b_stack=True, epilogue=swiglu_tile, epi_in=[g],
                  epi_out=[F32, BF16])
        h_out = mm(a, plain(pre + "_w_down", a), name=f"{tag}_down", res=h, alpha=0.5)
        return h_out, (n, g, u, a)

    def ffn_bwd(h, norm_g, saved, dh, dh_half, tag, pre, last, after=()):
        n, g, u, a = saved
        G[pre + "_w_down"] = mm(a, dh_half, name=f"{tag}_dw_down", ta=True, out_dtype=BF16, after=after)
        begun_d, token_d = pair_begin([pre + "_w_down"], tag + "_d")
        dg, du = mm(dh_half, plain(pre + "_w_down"), name=f"{tag}_da", tb=True, after=[token_d],
                    epilogue=swiglu_bwd_tile, epi_in=[g, u], epi_out=[BF16, BF16])
        G[pre + "_w_gate"] = mm(n, dg, name=f"{tag}_dw_gate", ta=True, out_dtype=BF16, out_stack=True)
        G[pre + "_w_up"] = mm(n, du, name=f"{tag}_dw_up", ta=True, out_dtype=BF16, out_stack=True)
        begun_gu, token_gu = pair_begin([pre + "_w_gate", pre + "_w_up"], tag + "_gu")
        dn = mm(dg, stacked(pre + "_w_gate"), name=f"{tag}_dn_gate", tb=True, b_stack=True, after=[token_gu])
        token = reduce_go([begun_d, begun_gu], tag, [dn])
        dn = mm(du, stacked(pre + "_w_up"), name=f"{tag}_dn_up", tb=True, b_stack=True, res=dn, after=[token])
        return norm_bwd(h, norm_g, dn, name=f"{tag}_dnorm", res=dh, bf16_alpha=None if last else 1.0)

    bias = na_bias(na_rpb[0], after=tokens[-1:])
    h1, ffn1_saved = ffn_fwd(xs, ffn1_norm, "ffn1", "ffn1", after=[bias, tokens[-1]])
    u_mix = norm_fwd(h1, mix_norm, name="mix_norm")
    win_st = stacked("w_in", u_mix)
    w_na = jnp.concatenate(shard_cols(win_st, 0, n_na), axis=1)
    w_rest = jnp.concatenate([p_ for lo, hi in rest_ranges for p_ in shard_cols(win_st, lo, hi)]
                             + [jnp.zeros((D, kr_w - MLA_ROPE), BF16)], axis=1)
    z_na = mm(u_mix, w_na, name="mix_in_na", out_dtype=BF16)
    z = mm(u_mix, w_rest, name="mix_in_rest")
    o_a = na_fwd(z_na, bias, NH, S)
    c_q = norm_fwd((z, QR, off_ql // QR), q_a_norm, name="q_a_norm")
    c_kv = norm_fwd((z, KVR, off_kvl // KVR), kv_a_norm, name="kv_a_norm")
    wuq = plain("w_uq", c_kv).reshape(QR, MH, MLA_QK)
    wuq_n = wuq[:, :, :MLA_NOPE].reshape(QR, MH * MLA_NOPE)
    wuq_r = jnp.pad(wuq[:, :, MLA_NOPE:], ((0, 0), (0, 0), (0, LANES - MLA_ROPE))).reshape(QR, MH * LANES)
    wukv = plain("w_ukv").reshape(KVR, MH, 2, HEAD_DIM)
    wuk = wukv[:, :, 0].reshape(KVR, MH * HEAD_DIM)
    wuv = wukv[:, :, 1].reshape(KVR, MH * HEAD_DIM)
    q_n = mm(c_q, wuq_n, name="mla_q_nope", out_dtype=BF16)
    q_r = rope(mm(c_q, wuq_r, name="mla_q_rope"), cos_t, sin_t, name="rope_q", out_dtype=BF16)
    k_n = mm(c_kv, wuk, name="mla_k_nope", out_dtype=BF16)
    v_m = mm(c_kv, wuv, name="mla_v", out_dtype=BF16)
    k_r = rope((z, LANES, off_kr // LANES), cos_t, sin_t, name="rope_k", out_dtype=BF16)
    o_b, lse = mla_fwd(q_n, q_r, k_n, v_m, k_r, MH, S)
    y_a = mm(o_a, stacked("w_branch_a"), name="branch_a", b_stack=True)
    y_b = mm(o_b, stacked("w_branch_b"), name="branch_b", b_stack=True)
    z_ga, z_gb = (z, D, 0), (z, D, 1)
    merged = rowwise(lambda ga, gb, ya, yb: _sig(ga) * ya + _sig(gb) * yb, [z_ga, z_gb, y_a, y_b], [], [(D, BF16)],
                     tm=256, name="merge")[0]
    h2 = mm(merged, plain("w_out"), name="mix_out", res=h1)
    h3, ffn2_saved = ffn_fwd(h2, ffn2_norm, "ffn2", "ffn2")
    n4 = norm_fwd(h3, pl_norm, name="pl_norm")
    pg_pre = mm(n4, plain("w_pl_gate", n4), name="pl_gate")
    pe = mm(ps, stacked("w_pl"), name="pl_embed", b_stack=True)

    def tail(h3b, pgb, peb, tb_, fg):
        pg = _sig(pgb)
        h4 = h3b + pg * peb
        r = _rstd(h4)
        xh = h4 * r
        err = xh * fg - tb_
        loss_rows = jnp.mean(err * err, axis=-1, keepdims=True)
        dy = err * (1.0 / D)
        dxh = dy * fg
        dh4 = r * (dxh - xh * jnp.mean(dxh * xh, axis=-1, keepdims=True))
        loss_part = jnp.broadcast_to(0.5 * jnp.sum(loss_rows, axis=0, keepdims=True), (1, LANES))
        return (dh4, dh4 * peb * pg * (1.0 - pg), dh4 * pg, loss_part, jnp.sum(dy * xh, axis=0, keepdims=True))

    dh4, dpg_pre, dpe, loss_part, g_final = rowwise(
        tail, [h3, pg_pre, pe, tgt], [final_norm.reshape(1, D)], [(D, F32), (D, BF16), (D, BF16)],
        accs=[(1, LANES), (1, D)], tm=128, name="loss_tail")
    loss = lax.psum(loss_part[0, 0], ("x", "y", "c"))

    G = {}
    pending = []

    def four(g):
        if g.ndim == 2:
            return g.reshape(N_CHIPS, 2, g.shape[0] // (2 * N_CHIPS), g.shape[1])
        return g.reshape(N_CHIPS, 2, g.shape[1] // 2, g.shape[2])

    def pair_begin(names, tag):
        g4 = [four(G[n]) for n in names]
        lands = [lax.empty((N_CHIPS,) + g.shape[2:], BF16) for g in g4]
        sems, g4, lands, token = pair_start(g4, lands, tag)
        return (names, tag, sems, g4, lands), token

    def reduce_go(begun, tag, after):
        names, sums = [], []
        for b_names, b_tag, sems, g4, lands in begun:
            g4, got = pair_wait(sems, g4, lands, list(after), b_tag)
            sums += [pair_sum(a_, r_, c_idx, n) for n, a_, r_ in zip(b_names, g4, got)]
            names += b_names
        lands = [lax.empty((N_CHIPS - 1,) + s_.shape[1:], BF16) for s_ in sums]
        sems, sums, lands, token = scatter_start(sums, lands, tag)
        pending.append((names, tag, sems, sums, lands))
        return token

    def reduce_finish(entry, after):
        names, tag, sems, sums, lands = entry
        sums, got = scatter_wait(sems, sums, lands, after, tag)
        halves = [chip_sum(a, b, j_idx, n) for n, a, b in zip(names, sums, got)]
        done = []
        for n, own, sib in zip(names, halves, half_exchange(halves, tag)):
            shp = W[n].shape
            two_d = lambda a_: a_.reshape(shp[1], shp[2])
            out = adamw_shard(two_d(W[n]), own, sib, two_d(Mo[n]), two_d(Vo[n]), c_idx, name="adamw_" + n)
            grads[n], delta[n], new_m[n], new_v[n] = [o.reshape(shp) for o in out]
            done.append(out[0])
        return done

    G["w_pl"] = mm(ps, dpe, name="pl_dw_embed", ta=True, out_dtype=BF16, out_stack=True)
    G["w_pl_gate"] = mm(n4, dpg_pre, name="pl_dw_gate", ta=True, out_dtype=BF16)
    begun_pl, token = pair_begin(["w_pl", "w_pl_gate"], "pl")
    dn4 = mm(dpg_pre, plain("w_pl_gate"), name="pl_dn", tb=True, after=[token])
    dh3, dh3_half, g_pl = norm_bwd(h3, pl_norm, dn4, name="pl_dnorm", res=dh4, bf16_alpha=0.5)
    token = reduce_go([begun_pl], "pl", [dh3])
    dh2, dh2_b, g_ffn2 = ffn_bwd(h2, ffn2_norm, ffn2_saved, dh3, dh3_half, "ffn2", "ffn2", last=False, after=[token])

    G["w_out"] = mm(merged, dh2_b, name="mix_dw_out", ta=True, out_dtype=BF16)
    dmerged = mm(dh2_b, plain("w_out"), name="mix_dmerged", tb=True)

    def merge_bwd(ga, gb, ya, yb, dm):
        sa, sb = _sig(ga), _sig(gb)
        dgates = jnp.concatenate([dm * ya * sa * (1.0 - sa), dm * yb * sb * (1.0 - sb)], axis=1)
        return dm * sa, dm * sb, dgates

    dy_a, dy_b, dz_rest = rowwise(merge_bwd, [z_ga, z_gb, y_a, y_b, dmerged], [],
                                  [(D, BF16), (D, BF16), (2 * D, BF16, 0)], tm=256, name="merge_bwd",
                                  into=(None, rest_w))
    G["w_branch_a"] = mm(o_a, dy_a, name="branch_a_dw", ta=True, out_dtype=BF16, out_stack=True)
    G["w_branch_b"] = mm(o_b, dy_b, name="branch_b_dw", ta=True, out_dtype=BF16, out_stack=True)
    do_a = mm(dy_a, stacked("w_branch_a"), name="branch_a_dx", tb=True, b_stack=True)
    do_b = mm(dy_b, stacked("w_branch_b"), name="branch_b_dx", tb=True, b_stack=True)
    dq_na, dk_na, dv_na, dbias = na_bwd(z_na, bias, do_a, NH, S)
    g_rpb = na_rpb_grad(dbias)
    dq_n, dq_rr, dk_n, dv_m, dk_rr = mla_bwd(q_n, q_r, k_n, v_m, k_r, lse, do_b, MH, S)
    dq_r = rope(dq_rr, cos_t, -sin_t, name="rope_q_bwd", out_dtype=BF16)
    dz_rest = rope(dk_rr, cos_t, -sin_t, name="rope_k_bwd", out_dtype=BF16, into=(dz_rest, rest_w),
                   cb=off_kr // kr_w, zero_cols=kr_w - LANES)
    gw_uq_n = mm(c_q, dq_n, name="mla_dw_q_nope", ta=True, out_dtype=BF16)
    gw_uq_r = mm(c_q, dq_r, name="mla_dw_q_rope", ta=True, out_dtype=BF16)
    dc_q = mm(dq_n, wuq_n, name="mla_dcq_nope", tb=True)
    dc_q = mm(dq_r, wuq_r, name="mla_dcq_rope", tb=True, res=dc_q)
    gw_uk = mm(c_kv, dk_n, name="mla_dw_k", ta=True, out_dtype=BF16)
    gw_uv = mm(c_kv, dv_m, name="mla_dw_v", ta=True, out_dtype=BF16)
    dc_kv = mm(dk_n, wuk, name="mla_dckv_k", tb=True)
    dc_kv = mm(dv_m, wuv, name="mla_dckv_v", tb=True, res=dc_kv)
    dz_rest, g_qa = norm_bwd((z, QR, off_ql // QR), q_a_norm, dc_q, name="q_a_dnorm", want_f32=False, bf16_alpha=1.0,
                             into=(dz_rest, rest_w), cb=off_ql // QR)
    dz_rest, g_kva = norm_bwd((z, KVR, off_kvl // KVR), kv_a_norm, dc_kv, name="kv_a_dnorm", want_f32=False,
                              bf16_alpha=1.0, into=(dz_rest, rest_w), cb=off_kvl // KVR)
    dz_na = jnp.concatenate([dq_na, dk_na, dv_na], axis=1)

    def to_stack(g2d):
        k, n = g2d.shape
        return g2d.reshape(k, N_CHIPS, n // N_CHIPS).transpose(1, 0, 2)

    gw_uq = jnp.concatenate([gw_uq_n.reshape(QR, MH, MLA_NOPE), gw_uq_r.reshape(QR, MH, LANES)[:, :, :MLA_ROPE]],
                            axis=2).reshape(QR, MH * MLA_QK)
    G["w_uq"] = to_stack(gw_uq)
    gw_ukv = jnp.stack([gw_uk.reshape(KVR, MH, HEAD_DIM), gw_uv.reshape(KVR, MH, HEAD_DIM)], axis=2)
    G["w_ukv"] = to_stack(gw_ukv.reshape(KVR, MH * 2 * HEAD_DIM))
    begun_mix, token = pair_begin(["w_out", "w_branch_a", "w_branch_b", "w_uq", "w_ukv"], "mix")
    gw_na = mm(u_mix, dz_na, name="mix_dw_in_na", ta=True, out_dtype=BF16, after=[token])
    gw_rest = mm(u_mix, dz_rest, name="mix_dw_in_rest", ta=True, out_dtype=BF16)
    G["w_in"] = w_in_shards(gw_na, gw_rest)
    begun_win, token_win = pair_begin(["w_in"], "win")
    token_mix = reduce_go([begun_mix], "mix", [gw_rest, token_win])
    du_mix = mm(dz_na, w_na, name="mix_du_na", tb=True, after=[token_mix])
    token_win = reduce_go([begun_win], "win", [du_mix])
    du_mix = mm(dz_rest, w_rest, name="mix_du_rest", tb=True, res=du_mix, after=[token_win])
    dh1, dh1_half, g_mix = norm_bwd(h1, mix_norm, du_mix, name="mix_dnorm", res=dh2, bf16_alpha=0.5)
    grad_x, g_ffn1 = ffn_bwd(xs, ffn1_norm, ffn1_saved, dh1, dh1_half, "ffn1", "ffn1", last=True)

    small_g = {"ffn1_norm": g_ffn1, "mix_norm": g_mix, "q_a_norm": g_qa, "kv_a_norm": g_kva, "na_rpb": g_rpb,
               "ffn2_norm": g_ffn2, "pl_norm": g_pl, "final_norm": g_final}
    sizes = [int(np.prod(W[n].shape)) for n in small]
    total = sum(sizes)
    padded = -(-total // (8 * LANES)) * (8 * LANES)

    def pack(parts):
        flat = jnp.concatenate([jnp.reshape(parts[n], (-1,)).astype(F32) for n in small]
                               + [jnp.zeros((padded - total,), F32)])
        return flat.reshape(padded // LANES, LANES)

    def unpack(a):
        flat, out, o = a.reshape(-1), {}, 0
        for n, sz in zip(small, sizes):
            out[n] = flat[o:o + sz].reshape(W[n].shape)
            o += sz
        return out

    grads, delta, new_m, new_v = {}, {}, {}, {}
    after = [grad_x]
    for entry in pending:
        after = reduce_finish(entry, after)

    g_small = sum_devices(gather_small(pack(small_g), after))
    d_small, m_small, v_small = adamw(pack(W), g_small, pack(Mo), pack(Vo), name="adamw_small")
    for full, part in ((grads, g_small), (delta, d_small), (new_m, m_small), (new_v, v_small)):
        full.update(unpack(part))

    return (loss, grad_x[None], *[grads[n] for n in order], *[delta[n] for n in order],
            *[new_m[n] for n in order], *[new_v[n] for n in order])
```

```python
import functools

import numpy as np
import jax
import jax.numpy as jnp
from jax import lax
from jax.experimental import pallas as pl
from jax.experimental.pallas import tpu as pltpu

F32 = jnp.float32
BF16 = jnp.bfloat16

VMEM_LIMIT_V7X = 56 * 1024 * 1024
VMEM_BUDGET_V7X = 40 * 1024 * 1024
LANES = 128

GRID_W = 64
NA_WIN_ROWS = 8
NA_WIN_COLS = 16
HEAD_DIM = 128
MLA_NOPE = 128
MLA_ROPE = 64
MLA_QK = MLA_NOPE + MLA_ROPE
ROPE_THETA = 10000.0
NORM_EPS = 1e-6
NEG_INF = -1e30
N_CHIPS = 4

ADAM_LR = 0.001
ADAM_B1 = 0.9
ADAM_B2 = 0.999
ADAM_EPS = 1e-08
ADAM_WD = 0.01
ADAM_STEP = 10

MESH = pl.DeviceIdType.MESH
ANY = pl.BlockSpec(memory_space=pl.ANY)


def _params(sem=None):
    return pltpu.CompilerParams(dimension_semantics=sem, vmem_limit_bytes=VMEM_LIMIT_V7X)


def _pick(n, target, align):
    best = None
    t = align
    while t <= min(n, target):
        if n % t == 0:
            best = t
        t += align
    return n if best is None else best


def mm(a, b, *, name, ta=False, tb=False, out_dtype=F32, res=None, alpha=1.0, b_stack=False, out_stack=False,
       exact=False, after=(), epilogue=None, epi_in=(), epi_out=()):
    K, M = (a.shape if ta else a.shape[::-1])
    nst = kb = nb = None
    if b_stack:
        nst = b.shape[0]
        if tb:
            N, kb = b.shape[1], b.shape[2]
            Kb = nst * kb
        else:
            Kb, nb = b.shape[1], b.shape[2]
            N = nst * nb
    else:
        N, Kb = (b.shape if tb else b.shape[::-1])
    assert K == Kb, (a.shape, b.shape, ta, tb)
    if out_stack:
        assert N % N_CHIPS == 0
    n_unit = N // N_CHIPS if out_stack else (nb if nb is not None else N)
    tn = _pick(n_unit, 512, LANES) if n_unit % 512 == 0 or n_unit <= 512 else _pick(n_unit, 1536, LANES)
    if ta and n_unit == N and 4 * K * N * jnp.dtype(b.dtype).itemsize <= VMEM_BUDGET_V7X:
        tn = N
    m_align = LANES if ta else 16
    tm = _pick(M, 1024, m_align)
    isz = lambda t: jnp.dtype(t.dtype).itemsize
    out_dtypes = list(epi_out) if epilogue is not None else [out_dtype]
    osz = sum(jnp.dtype(t).itemsize for t in out_dtypes) + sum(isz(e) for e in epi_in)

    def vmem(tm_, tn_):
        return (2 * tm_ * K * isz(a) + 2 * K * tn_ * isz(b) + 2 * tm_ * tn_ * osz + tm_ * tn_ * 4
                + (tm_ * K * 2 if ta else 0) + (2 * tm_ * tn_ * isz(res) if res is not None else 0))

    while vmem(tm, tn) > VMEM_BUDGET_V7X and tm % 2 == 0 and (tm // 2) % m_align == 0:
        tm //= 2
    while vmem(tm, tn) > VMEM_BUDGET_V7X and tn % 2 == 0 and (tn // 2) % LANES == 0 and n_unit % (tn // 2) == 0:
        tn //= 2
    assert vmem(tm, tn) <= VMEM_BUDGET_V7X, (name, tm, tn, K)

    a_spec = pl.BlockSpec((K, tm), lambda i, j: (0, i)) if ta else pl.BlockSpec((tm, K), lambda i, j: (i, 0))
    if b_stack and not tb:
        q = nb // tn
        b_spec = pl.BlockSpec((None, K, tn), lambda i, j: (j // q, 0, j % q))
    elif b_stack and tb:
        b_spec = pl.BlockSpec((nst, tn, kb), lambda i, j: (0, j, 0))
    elif tb:
        b_spec = pl.BlockSpec((tn, K), lambda i, j: (j, 0))
    else:
        b_spec = pl.BlockSpec((K, tn), lambda i, j: (0, j))
    if out_stack:
        qo = (N // N_CHIPS) // tn
        o_spec = pl.BlockSpec((None, tm, tn), lambda i, j: (j // qo, i, j % qo))
        o_shapes = [jax.ShapeDtypeStruct((N_CHIPS, M, N // N_CHIPS), t) for t in out_dtypes]
    else:
        o_spec = pl.BlockSpec((tm, tn), lambda i, j: (i, j))
        o_shapes = [jax.ShapeDtypeStruct((M, N), t) for t in out_dtypes]
    has_res = res is not None
    n_in = 2 + has_res + len(epi_in) + len(after)
    nn = (((1,), (0,)), ((), ()))
    nt = (((1,), (1,)), ((), ()))

    def body(*refs):
        a_ref, b_ref = refs[:2]
        r_ref = refs[2] if has_res else None
        e_refs = refs[2 + has_res:2 + has_res + len(epi_in)]
        o_refs = refs[n_in:n_in + len(out_dtypes)]
        if ta:
            at_ref = refs[-1]

            @pl.when(pl.program_id(1) == 0)
            def _():
                at_ref[...] = a_ref[...].astype(BF16).T

            lhs = at_ref[...]
        elif exact:
            lhs = a_ref[...]
        else:
            lhs = a_ref[...].astype(BF16)
        if exact:
            total = lax.dot_general(lhs, b_ref[...], nt if tb else nn, preferred_element_type=F32,
                                    precision=lax.Precision.HIGHEST)
        elif b_stack and tb:
            total = None
            for s in range(nst):
                part = lax.dot_general(lhs[:, s * kb:(s + 1) * kb], b_ref[s].astype(BF16), nt,
                                       preferred_element_type=F32)
                total = part if total is None else total + part
        else:
            total = lax.dot_general(lhs, b_ref[...].astype(BF16), nt if tb else nn, preferred_element_type=F32)
        if alpha != 1.0:
            total = total * alpha
        if has_res:
            total = total + r_ref[...].astype(F32)
        vals = epilogue(total, *[e[...] for e in e_refs]) if epilogue is not None else (total,)
        for o_ref, v in zip(o_refs, vals):
            o_ref[...] = v.astype(o_ref.dtype)

    tile = pl.BlockSpec((tm, tn), lambda i, j: (i, j))
    in_specs = [a_spec, b_spec] + [tile] * (has_res + len(epi_in)) + [ANY] * len(after)
    args = [a, b] + ([res] if has_res else []) + list(epi_in) + list(after)
    outs = pl.pallas_call(
        body, name=name, grid=(M // tm, N // tn), in_specs=in_specs, out_specs=[o_spec] * len(out_dtypes),
        out_shape=o_shapes, scratch_shapes=[pltpu.VMEM((tm, K), BF16)] if ta else [],
        compiler_params=_params(("parallel", "arbitrary")),
    )(*args)
    return outs if epilogue is not None else outs[0]


def rowwise(fn, rows, consts, outs, accs=(), *, tm, name, tn=None, into=None):
    rows = [r if isinstance(r, tuple) else (r, r.shape[1], 0) for r in rows]
    S = rows[0][0].shape[0]
    tm = _pick(S, tm, 16)
    nrow, ncon, nout = len(rows), len(consts), len(outs)
    outs = [o if len(o) == 3 else (o[0], o[1], None) for o in outs]
    if tn is None:
        grid = (S // tm,)
        in_specs = [pl.BlockSpec((tm, w), functools.partial(lambda i, cb: (i, cb), cb=cb)) for _, w, cb in rows]
        in_specs += [pl.BlockSpec(c.shape, lambda i: (0, 0)) for c in consts]
        out_specs = [pl.BlockSpec((tm, n), functools.partial(lambda i, cb: (i, cb), cb=cb or 0)) for n, _, cb in outs]
        out_specs += [pl.BlockSpec(s, lambda i: (0, 0)) for s in accs]
        sem = ("arbitrary",)
    else:
        assert not accs
        N = rows[0][1]
        grid = (S // tm, N // tn)
        in_specs = [pl.BlockSpec((tm, tn), lambda i, j: (i, j)) for _ in rows]
        in_specs += [pl.BlockSpec(c.shape, lambda i, j: (0, 0)) for c in consts]
        out_specs = [pl.BlockSpec((tm, tn), lambda i, j: (i, j)) for _ in outs]
        sem = ("parallel", "parallel")
    out_shape = [jax.ShapeDtypeStruct((S, n if cb is None else into[1]), dt) for n, dt, cb in outs]
    out_shape += [jax.ShapeDtypeStruct(s, F32) for s in accs]
    extra, aliases = [], {}
    if into is not None and into[0] is not None:
        extra = [into[0]]
        aliases = {nrow + ncon: [cb is not None for _, _, cb in outs].index(True)}

    def body(*refs):
        vals = fn(*[r[...] for r in refs[:nrow + ncon]])
        if not isinstance(vals, (tuple, list)):
            vals = (vals,)
        o_refs = refs[nrow + ncon + len(extra):]
        for o_ref, v in zip(o_refs[:nout], vals[:nout]):
            o_ref[...] = v.astype(o_ref.dtype)
        if accs:
            first = pl.program_id(0) == 0

            def accumulate(a_ref, v):
                @pl.when(first)
                def _():
                    a_ref[...] = v

                @pl.when(jnp.logical_not(first))
                def _():
                    a_ref[...] += v

            for a_ref, v in zip(o_refs[nout:], vals[nout:]):
                accumulate(a_ref, v.astype(F32))

    return pl.pallas_call(
        body, name=name, grid=grid, in_specs=in_specs + [ANY] * len(extra), out_specs=out_specs, out_shape=out_shape,
        input_output_aliases=aliases, compiler_params=_params(sem),
    )(*[r[0] for r in rows], *consts, *extra)


def _rstd(x):
    return lax.rsqrt(jnp.mean(x * x, axis=-1, keepdims=True) + NORM_EPS)


def norm_fwd(x, g, *, name, tm=256):
    w = x[1] if isinstance(x, tuple) else x.shape[1]

    def fn(xb, gb):
        return (xb * _rstd(xb)) * gb

    return rowwise(fn, [x], [g], [(w, BF16)], tm=tm, name=name)[0]


def norm_bwd(x, g, dn, *, name, res=None, want_f32=True, bf16_alpha=None, tm=256, into=None, cb=None):
    w = x[1] if isinstance(x, tuple) else x.shape[1]
    has_res = res is not None

    def fn(*blocks):
        if has_res:
            xb, dnb, rb, gb = blocks
        else:
            xb, dnb, gb = blocks
        r = _rstd(xb)
        xh = xb * r
        dxh = dnb * gb
        dx = r * (dxh - xh * jnp.mean(dxh * xh, axis=-1, keepdims=True))
        if has_res:
            dx = dx + rb
        out = []
        if want_f32:
            out.append(dx)
        if bf16_alpha is not None:
            out.append(dx * bf16_alpha if bf16_alpha != 1.0 else dx)
        out.append(jnp.sum(dnb * xh, axis=0, keepdims=True))
        return tuple(out)

    outs = ([(w, F32)] if want_f32 else []) + ([(w, BF16, cb)] if bf16_alpha is not None else [])
    rows = [x, dn] + ([res] if has_res else [])
    return rowwise(fn, rows, [g], outs, accs=[(1, w)], tm=tm, name=name, into=into)


def _sig(x):
    return jax.nn.sigmoid(x)


def swiglu_tile(ub, gb):
    return ub, gb * _sig(gb) * ub


def swiglu_bwd_tile(dab, gb, ub):
    sg = _sig(gb)
    return dab * ub * (sg + gb * sg * (1.0 - sg)), dab * (gb * sg)


def rope(x, cos, sin_signed, *, name, out_dtype, into=None, cb=None, zero_cols=0):
    w = x[1] if isinstance(x, tuple) else x.shape[1]
    half = MLA_ROPE // 2

    def fn(xb, cb, sb):
        lane = lax.broadcasted_iota(jnp.int32, cb.shape, 1)
        outs = []
        for hb in range(w // LANES):
            blk = xb[:, hb * LANES:(hb + 1) * LANES]
            partner = jnp.where(lane < half, pltpu.roll(blk, LANES - half, 1), pltpu.roll(blk, half, 1))
            outs.append(blk * cb + partner * sb)
        if zero_cols:
            outs.append(jnp.zeros((xb.shape[0], zero_cols), xb.dtype))
        return outs[0] if len(outs) == 1 else jnp.concatenate(outs, axis=1)

    return rowwise(fn, [x, cos, sin_signed], [], [(w + zero_cols, out_dtype, cb)], tm=256, name=name, into=into)[0]


def _na_tables():
    cols = np.arange(GRID_W)
    kw = NA_WIN_COLS
    dc = np.clip(cols[None, :] - cols[:, None], -(kw - 1), kw - 1) + (kw - 1)
    onehot = np.zeros((LANES, GRID_W * GRID_W), np.float32)
    onehot[dc.reshape(-1), np.arange(GRID_W * GRID_W)] = 1.0
    col_start = np.clip(cols - kw // 2, 0, GRID_W - kw)
    mask = (cols[None, :] >= col_start[:, None]) & (cols[None, :] < col_start[:, None] + kw)
    return onehot, np.where(mask, 0.0, NEG_INF).astype(np.float32)


def na_bias(rpb, after=()):
    H = rpb.shape[0]
    nr, kh = 2 * NA_WIN_ROWS - 1, NA_WIN_ROWS
    onehot, maskb = _na_tables()
    rp = jnp.pad(rpb.reshape(H * nr, 2 * NA_WIN_COLS - 1), ((0, 0), (0, LANES - (2 * NA_WIN_COLS - 1))))
    t1 = mm(rp, jnp.asarray(onehot), name="na_bias_table", exact=True, after=after).reshape(H, nr, GRID_W, GRID_W)
    t1 = t1 + jnp.asarray(maskb)[None, None]
    per_t = [jnp.stack([t1[:, i - t + kh - 1] for i in range(kh)], axis=2) for t in range(kh)]
    return jnp.stack(per_t, axis=1).reshape(H, kh, GRID_W, kh * GRID_W)


def na_rpb_grad(db):
    H = db.shape[0]
    nr, kh = 2 * NA_WIN_ROWS - 1, NA_WIN_ROWS
    onehot, _ = _na_tables()
    shifted = [jnp.pad(db[:, t], ((0, 0), (0, 0), ((kh - 1 - t) * GRID_W, t * GRID_W))) for t in range(kh)]
    dw1 = functools.reduce(jnp.add, shifted)
    dt1 = dw1.reshape(H, GRID_W, nr, GRID_W).transpose(0, 2, 1, 3).reshape(H * nr, GRID_W * GRID_W)
    g = mm(dt1, jnp.asarray(onehot), name="na_rpb_grad", tb=True, exact=True)
    return g[:, :2 * NA_WIN_COLS - 1].reshape(H, nr, 2 * NA_WIN_COLS - 1)


def _na_first_row(r, rows):
    return jnp.clip(r - NA_WIN_ROWS // 2, 0, rows - NA_WIN_ROWS)


NA_ROWS_PER_STEP = 8


def _na_probs(q, k_ref, b_ref, r, rows):
    first = _na_first_row(r, rows)
    start = pl.multiple_of(first * GRID_W, GRID_W)
    k = k_ref[pl.ds(start, NA_WIN_ROWS * GRID_W), :]
    s = lax.dot_general(q, k, (((1,), (1,)), ((), ())), preferred_element_type=F32)
    s = s * (HEAD_DIM ** -0.5) + b_ref[r - first]
    m = jnp.max(s, axis=-1, keepdims=True)
    e = jnp.exp(s - m)
    return k, e / jnp.sum(e, axis=-1, keepdims=True), start, r - first


def na_fwd(z, bias, H, S):
    rows = S // GRID_W
    nkeys = NA_WIN_ROWS * GRID_W
    rb = _pick(rows, NA_ROWS_PER_STEP, 1)

    def body(q_ref, k_ref, v_ref, b_ref, o_ref):
        for j in range(rb):
            r = pl.program_id(1) * rb + j
            rows_j = pl.ds(j * GRID_W, GRID_W)
            _, p, start, _ = _na_probs(q_ref[rows_j, :], k_ref, b_ref, r, rows)
            v = v_ref[pl.ds(start, nkeys), :]
            o_ref[rows_j, :] = jnp.dot(p.astype(BF16), v, preferred_element_type=F32).astype(o_ref.dtype)

    return pl.pallas_call(
        body, name="na_fwd", grid=(H, rows // rb),
        in_specs=[pl.BlockSpec((rb * GRID_W, HEAD_DIM), lambda h, i: (i, h)),
                  pl.BlockSpec((S, HEAD_DIM), lambda h, i: (0, H + h)),
                  pl.BlockSpec((S, HEAD_DIM), lambda h, i: (0, 2 * H + h)),
                  pl.BlockSpec((None, NA_WIN_ROWS, GRID_W, nkeys), lambda h, i: (h, 0, 0, 0))],
        out_specs=pl.BlockSpec((rb * GRID_W, HEAD_DIM), lambda h, i: (i, h)),
        out_shape=jax.ShapeDtypeStruct((S, H * HEAD_DIM), BF16),
        compiler_params=_params(("parallel", "arbitrary")),
    )(z, z, z, bias)


def na_bwd(z, bias, do, H, S):
    rows = S // GRID_W
    nkeys = NA_WIN_ROWS * GRID_W
    rb = _pick(rows, NA_ROWS_PER_STEP, 1)
    tn_dims = (((0,), (0,)), ((), ()))

    def body(q_ref, k_ref, v_ref, b_ref, do_ref, dq_ref, dk_ref, dv_ref, db_ref, dk_acc, dv_acc):
        i = pl.program_id(1)

        @pl.when(i == 0)
        def _():
            dk_acc[...] = jnp.zeros_like(dk_acc)
            dv_acc[...] = jnp.zeros_like(dv_acc)
            db_ref[...] = jnp.zeros_like(db_ref)

        for j in range(rb):
            rows_j = pl.ds(j * GRID_W, GRID_W)
            q = q_ref[rows_j, :]
            k, p, start, t = _na_probs(q, k_ref, b_ref, i * rb + j, rows)
            keys = pl.ds(start, nkeys)
            dob = do_ref[rows_j, :].astype(BF16)
            dp = lax.dot_general(dob, v_ref[keys, :], (((1,), (1,)), ((), ())), preferred_element_type=F32)
            ds = p * (dp - jnp.sum(dp * p, axis=-1, keepdims=True))
            dsb = (ds * (HEAD_DIM ** -0.5)).astype(BF16)
            dq_ref[rows_j, :] = jnp.dot(dsb, k, preferred_element_type=F32).astype(dq_ref.dtype)
            dk_acc[keys, :] += lax.dot_general(dsb, q, tn_dims, preferred_element_type=F32)
            dv_acc[keys, :] += lax.dot_general(p.astype(BF16), dob, tn_dims, preferred_element_type=F32)
            db_ref[t] += ds

        @pl.when(i == rows // rb - 1)
        def _():
            dk_ref[...] = dk_acc[...].astype(dk_ref.dtype)
            dv_ref[...] = dv_acc[...].astype(dv_ref.dtype)

    W = H * HEAD_DIM
    qspec = pl.BlockSpec((rb * GRID_W, HEAD_DIM), lambda h, i: (i, h))
    bspec = pl.BlockSpec((None, NA_WIN_ROWS, GRID_W, nkeys), lambda h, i: (h, 0, 0, 0))
    return pl.pallas_call(
        body, name="na_bwd", grid=(H, rows // rb),
        in_specs=[qspec, pl.BlockSpec((S, HEAD_DIM), lambda h, i: (0, H + h)),
                  pl.BlockSpec((S, HEAD_DIM), lambda h, i: (0, 2 * H + h)), bspec, qspec],
        out_specs=[qspec, pl.BlockSpec((S, HEAD_DIM), lambda h, i: (0, h)),
                   pl.BlockSpec((S, HEAD_DIM), lambda h, i: (0, h)), bspec],
        out_shape=[jax.ShapeDtypeStruct((S, W), BF16)] * 3 + [jax.ShapeDtypeStruct((H, NA_WIN_ROWS, GRID_W, nkeys), F32)],
        scratch_shapes=[pltpu.VMEM((S, HEAD_DIM), F32)] * 2,
        compiler_params=_params(("arbitrary", "arbitrary")),
    )(z, z, z, bias, do)


def _mla_keys(kn_ref, kr_ref, kcat):
    @pl.when(pl.program_id(1) == 0)
    def _():
        kcat[:, :HEAD_DIM] = kn_ref[...]
        kcat[:, HEAD_DIM:] = kr_ref[...]


MLA_LOG2_SCALE = (MLA_QK ** -0.5) * 1.4426950408889634


def _mla_scores(qn_ref, qr_ref, kcat):
    qcat = jnp.concatenate([qn_ref[...], qr_ref[...]], axis=1)
    return qcat, lax.dot_general(qcat, kcat[...], (((1,), (1,)), ((), ())), preferred_element_type=F32)


def mla_fwd(qn, qr, kn, v, kr, H, S):
    tq = _pick(S, 256, 16)

    def body(qn_ref, qr_ref, kn_ref, v_ref, kr_ref, o_ref, lse_ref, kcat):
        _mla_keys(kn_ref, kr_ref, kcat)
        _, s = _mla_scores(qn_ref, qr_ref, kcat)
        m = jnp.max(s, axis=-1, keepdims=True)
        e = jnp.exp2((s - m) * MLA_LOG2_SCALE)
        l = jnp.sum(e, axis=-1, keepdims=True)
        o = jnp.dot(e.astype(BF16), v_ref[...], preferred_element_type=F32)
        o_ref[...] = (o / l).astype(o_ref.dtype)
        lse_ref[...] = jnp.broadcast_to(m * MLA_LOG2_SCALE + jnp.log2(l), lse_ref.shape)

    qspec = pl.BlockSpec((tq, HEAD_DIM), lambda h, i: (i, h))
    kspec = pl.BlockSpec((S, HEAD_DIM), lambda h, i: (0, h))
    return pl.pallas_call(
        body, name="mla_fwd", grid=(H, S // tq),
        in_specs=[qspec, qspec, kspec, kspec, pl.BlockSpec((S, LANES), lambda h, i: (0, 0))],
        out_specs=[qspec, qspec],
        out_shape=[jax.ShapeDtypeStruct((S, H * HEAD_DIM), BF16), jax.ShapeDtypeStruct((S, H * LANES), F32)],
        scratch_shapes=[pltpu.VMEM((S, 2 * HEAD_DIM), BF16)],
        compiler_params=_params(("parallel", "arbitrary")),
    )(qn, qr, kn, v, kr)


def mla_bwd(qn, qr, kn, v, kr, lse, do, H, S):
    tq = _pick(S, 256, 16)
    nt = (((1,), (1,)), ((), ()))
    tn_dims = (((0,), (0,)), ((), ()))

    def body(qn_ref, qr_ref, kn_ref, v_ref, kr_ref, lse_ref, do_ref, dqn_ref, dqr_ref, dkn_ref, dv_ref, dkr_ref, kcat):
        h, i = pl.program_id(0), pl.program_id(1)
        _mla_keys(kn_ref, kr_ref, kcat)
        qcat, s = _mla_scores(qn_ref, qr_ref, kcat)
        p = jnp.exp2(s * MLA_LOG2_SCALE - lse_ref[:, 0:1])
        dob = do_ref[...].astype(BF16)
        dp = lax.dot_general(dob, v_ref[...], nt, preferred_element_type=F32)
        ds = p * (dp - jnp.sum(dp * p, axis=-1, keepdims=True))
        dsb = (ds * (MLA_QK ** -0.5)).astype(BF16)
        dq = jnp.dot(dsb, kcat[...], preferred_element_type=F32)
        dqn_ref[...] = dq[:, :HEAD_DIM].astype(dqn_ref.dtype)
        dqr_ref[...] = dq[:, HEAD_DIM:].astype(dqr_ref.dtype)

        @pl.when(i == 0)
        def _():
            dkn_ref[...] = jnp.zeros_like(dkn_ref)
            dv_ref[...] = jnp.zeros_like(dv_ref)

        @pl.when(jnp.logical_and(i == 0, h == 0))
        def _():
            dkr_ref[...] = jnp.zeros_like(dkr_ref)

        dk = lax.dot_general(dsb, qcat, tn_dims, preferred_element_type=F32)
        dkn_ref[...] += dk[:, :HEAD_DIM]
        dkr_ref[...] += dk[:, HEAD_DIM:]
        dv_ref[...] += lax.dot_general(p.astype(BF16), dob, tn_dims, preferred_element_type=F32)

    qspec = pl.BlockSpec((tq, HEAD_DIM), lambda h, i: (i, h))
    kspec = pl.BlockSpec((S, HEAD_DIM), lambda h, i: (0, h))
    rspec = pl.BlockSpec((S, LANES), lambda h, i: (0, 0))
    W = H * HEAD_DIM
    return pl.pallas_call(
        body, name="mla_bwd", grid=(H, S // tq),
        in_specs=[qspec, qspec, kspec, kspec, rspec, qspec, qspec],
        out_specs=[qspec, qspec, kspec, kspec, rspec],
        out_shape=[jax.ShapeDtypeStruct((S, W), BF16), jax.ShapeDtypeStruct((S, W), F32),
                   jax.ShapeDtypeStruct((S, W), F32), jax.ShapeDtypeStruct((S, W), F32),
                   jax.ShapeDtypeStruct((S, LANES), F32)],
        scratch_shapes=[pltpu.VMEM((S, 2 * HEAD_DIM), BF16)],
        compiler_params=_params(("arbitrary", "arbitrary")),
    )(qn, qr, kn, v, kr, lse, do)


def _place():
    return lax.axis_index("x"), lax.axis_index("y"), lax.axis_index("c")


def _other_chips(x, y):
    return [(1 - x, y), (x, 1 - y), (1 - x, 1 - y)]


def _remote(src, dst, send_sem, recv_sem, to):
    return pltpu.make_async_remote_copy(src_ref=src, dst_ref=dst, send_sem=send_sem, recv_sem=recv_sem,
                                        device_id=to, device_id_type=MESH)


HBM = pl.BlockSpec(memory_space=pltpu.HBM)
SEM = pl.BlockSpec(memory_space=pltpu.SEMAPHORE)
EFFECT = pltpu.SideEffectType.DATAFLOW_SIDE_EFFECTING


def _in_hbm(a):
    return pltpu.with_memory_space_constraint(a, pltpu.HBM)


TOKEN = jax.ShapeDtypeStruct((8, LANES), F32)
IN_VMEM = pl.BlockSpec(memory_space=pltpu.VMEM)


def cast_own_block(w, me_idx, after, name):
    _, k, n = w.shape
    rows = k // 2
    tm = _row_tile(rows, n, 4 + 2)
    nb = rows // tm

    def body(me_ref, w_ref, *rest):
        rest[len(after)][...] = w_ref[...].astype(BF16)

    gs = pltpu.PrefetchScalarGridSpec(
        num_scalar_prefetch=1, grid=(2, nb),
        in_specs=[pl.BlockSpec((None, tm, n), lambda h, i, me_ref: (0, h * nb + i, 0))] + [ANY] * len(after),
        out_specs=pl.BlockSpec((None, None, tm, n), lambda h, i, me_ref: (me_ref[0], h, i, 0)))
    return pl.pallas_call(body, name=name, grid_spec=gs, out_shape=jax.ShapeDtypeStruct((N_CHIPS, 2, rows, n), BF16),
                          compiler_params=_params(("arbitrary", "arbitrary")))(me_idx, w, *after)


def gather_start(landings, after, tag):
    n = len(landings)

    def body(*refs):
        lands = refs[:n]
        send, recv = refs[n + len(after)], refs[n + len(after) + 1]
        token = refs[-1]
        x, y, c = _place()
        me = 2 * x + y
        for w in range(n):
            for k, (px, py) in enumerate(_other_chips(x, y)):
                blk = lands[w].at[me, c]
                _remote(blk, blk, send.at[3 * w + k], recv.at[3 * w + k], (px, py, c)).start()
        token[...] = jnp.zeros_like(token)

    outs = pl.pallas_call(
        body, name="gather_start_" + tag,
        out_shape=(pltpu.SemaphoreType.DMA((3 * n,)),) * 2 + tuple(pltpu.HBM(b.shape, b.dtype) for b in landings)
        + (TOKEN,),
        in_specs=[HBM] * n + [ANY] * len(after), out_specs=tuple([SEM, SEM] + [HBM] * n + [IN_VMEM]),
        input_output_aliases={i: 2 + i for i in range(n)},
        compiler_params=pltpu.CompilerParams(has_side_effects=EFFECT),
    )(*[_in_hbm(b) for b in landings], *after)
    return (outs[0], outs[1]), outs[2:2 + n], outs[-1]


def gather_wait(sems, landings, after, tag):
    n = len(landings)
    send, recv = sems

    def body(*refs):
        lands = refs[:n]
        send_sem, recv_sem = refs[n], refs[n + 1]
        x, y, c = _place()
        me = 2 * x + y
        for w in range(n):
            for k, (px, py) in enumerate(_other_chips(x, y)):
                cp = _remote(lands[w].at[me, c], lands[w].at[2 * px + py, c], send_sem.at[3 * w + k],
                             recv_sem.at[3 * w + k], (px, py, c))
                cp.wait_send()
                cp.wait_recv()

    return pl.pallas_call(
        body, name="gather_wait_" + tag, out_shape=tuple(pltpu.HBM(b.shape, b.dtype) for b in landings),
        in_specs=[HBM] * n + [SEM, SEM] + [ANY] * len(after), out_specs=tuple([HBM] * n),
        input_output_aliases={i: i for i in range(n)},
        compiler_params=pltpu.CompilerParams(has_side_effects=EFFECT),
    )(*landings, send, recv, *after)


def gather_forward(landings, tag):
    n = len(landings)

    def body(*refs):
        ins, outs = refs[:n], refs[n:2 * n]
        send, recv = refs[2 * n:]
        x, y, c = _place()
        sibling = (x, y, 1 - c)
        cps = []
        for w in range(n):
            for k, (px, py) in enumerate(_other_chips(x, y)):
                j = 2 * px + py
                cp = _remote(ins[w].at[j, c], outs[w].at[j, c], send.at[3 * w + k], recv.at[3 * w + k], sibling)
                cp.start()
                cps.append(cp)
        for w in range(n):
            for k, (px, py) in enumerate(_other_chips(x, y)):
                blk = outs[w].at[2 * px + py, 1 - c]
                _remote(blk, blk, send.at[3 * w + k], recv.at[3 * w + k], sibling).wait_recv()
        for cp in cps:
            cp.wait_send()

    return pl.pallas_call(
        body, name="gather_forward_" + tag, in_specs=[ANY] * n, out_specs=[ANY] * n,
        out_shape=[jax.ShapeDtypeStruct(a.shape, a.dtype) for a in landings],
        input_output_aliases={i: i for i in range(n)},
        scratch_shapes=[pltpu.SemaphoreType.DMA((3 * n,)), pltpu.SemaphoreType.DMA((3 * n,))],
    )(*landings)


def pair_start(grads, landings, tag):
    n = len(grads)

    def body(*refs):
        ins, lands = refs[:n], refs[n:2 * n]
        send, recv = refs[2 * n], refs[2 * n + 1]
        token = refs[-1]
        x, y, c = _place()
        for w in range(n):
            _remote(ins[w].at[:, 1 - c], lands[w], send.at[w], recv.at[w], (x, y, 1 - c)).start()
        token[...] = jnp.zeros_like(token)

    bufs = list(grads) + list(landings)
    outs = pl.pallas_call(
        body, name="pair_start_" + tag,
        out_shape=(pltpu.SemaphoreType.DMA((n,)),) * 2 + tuple(pltpu.HBM(b.shape, b.dtype) for b in bufs) + (TOKEN,),
        in_specs=[HBM] * (2 * n), out_specs=tuple([SEM, SEM] + [HBM] * (2 * n) + [IN_VMEM]),
        input_output_aliases={i: 2 + i for i in range(2 * n)},
        compiler_params=pltpu.CompilerParams(has_side_effects=EFFECT),
    )(*[_in_hbm(b) for b in bufs])
    return (outs[0], outs[1]), outs[2:2 + n], outs[2 + n:2 + 2 * n], outs[-1]


def pair_wait(sems, grads, landings, after, tag):
    n = len(grads)

    def body(*refs):
        ins, lands = refs[:n], refs[n:2 * n]
        send, recv = refs[2 * n], refs[2 * n + 1]
        x, y, c = _place()
        for w in range(n):
            cp = _remote(ins[w].at[:, 1 - c], lands[w], send.at[w], recv.at[w], (x, y, 1 - c))
            cp.wait_send()
            cp.wait_recv()

    bufs = list(grads) + list(landings)
    outs = pl.pallas_call(
        body, name="pair_wait_" + tag, out_shape=tuple(pltpu.HBM(b.shape, b.dtype) for b in bufs),
        in_specs=[HBM] * (2 * n) + [SEM, SEM] + [ANY] * len(after), out_specs=tuple([HBM] * (2 * n)),
        input_output_aliases={i: i for i in range(2 * n)},
        compiler_params=pltpu.CompilerParams(has_side_effects=EFFECT),
    )(*bufs, sems[0], sems[1], *after)
    return outs[:n], outs[n:]


def scatter_start(sums, landings, tag):
    n = len(sums)

    def body(*refs):
        ins, lands = refs[:n], refs[n:2 * n]
        send, recv = refs[2 * n], refs[2 * n + 1]
        token = refs[-1]
        x, y, c = _place()
        for w in range(n):
            for k, (px, py) in enumerate(_other_chips(x, y)):
                _remote(ins[w].at[2 * px + py], lands[w].at[k], send.at[3 * w + k], recv.at[3 * w + k], (px, py, c)).start()
        token[...] = jnp.zeros_like(token)

    bufs = list(sums) + list(landings)
    outs = pl.pallas_call(
        body, name="scatter_start_" + tag,
        out_shape=(pltpu.SemaphoreType.DMA((3 * n,)),) * 2 + tuple(pltpu.HBM(b.shape, b.dtype) for b in bufs) + (TOKEN,),
        in_specs=[HBM] * (2 * n), out_specs=tuple([SEM, SEM] + [HBM] * (2 * n) + [IN_VMEM]),
        input_output_aliases={i: 2 + i for i in range(2 * n)},
        compiler_params=pltpu.CompilerParams(has_side_effects=EFFECT),
    )(*[_in_hbm(b) for b in bufs])
    return (outs[0], outs[1]), outs[2:2 + n], outs[2 + n:2 + 2 * n], outs[-1]


def scatter_wait(sems, sums, landings, after, tag):
    n = len(sums)

    def body(*refs):
        ins, lands = refs[:n], refs[n:2 * n]
        send, recv = refs[2 * n], refs[2 * n + 1]
        x, y, c = _place()
        for w in range(n):
            for k, (px, py) in enumerate(_other_chips(x, y)):
                cp = _remote(ins[w].at[2 * px + py], lands[w].at[k], send.at[3 * w + k], recv.at[3 * w + k], (px, py, c))
                cp.wait_send()
                cp.wait_recv()

    bufs = list(sums) + list(landings)
    outs = pl.pallas_call(
        body, name="scatter_wait_" + tag, out_shape=tuple(pltpu.HBM(b.shape, b.dtype) for b in bufs),
        in_specs=[HBM] * (2 * n) + [SEM, SEM] + [ANY] * len(after), out_specs=tuple([HBM] * (2 * n)),
        input_output_aliases={i: i for i in range(2 * n)},
        compiler_params=pltpu.CompilerParams(has_side_effects=EFFECT),
    )(*bufs, sems[0], sems[1], *after)
    return outs[:n], outs[n:]


def half_exchange(halves, tag):
    n = len(halves)

    def body(*refs):
        ins, outs = refs[:n], refs[n:2 * n]
        send, recv = refs[2 * n:]
        x, y, c = _place()
        cps = []
        for w in range(n):
            cp = _remote(ins[w], outs[w], send.at[w], recv.at[w], (x, y, 1 - c))
            cp.start()
            cps.append(cp)
        for cp in cps:
            cp.wait()

    return pl.pallas_call(
        body, name="grad_half_exchange_" + tag, in_specs=[ANY] * n, out_specs=[ANY] * n,
        out_shape=[jax.ShapeDtypeStruct(h.shape, h.dtype) for h in halves],
        scratch_shapes=[pltpu.SemaphoreType.DMA((n,)), pltpu.SemaphoreType.DMA((n,))],
    )(*halves)


def gather_small(v, after=()):
    def body(*refs):
        v_ref = refs[0]
        o_ref, send, recv, local = refs[1 + len(after):]
        x, y, c = _place()
        me = 4 * x + 2 * y + c
        own = pltpu.make_async_copy(v_ref, o_ref.at[me], local)
        own.start()
        cps = []
        for k in range(1, 8):
            fx, fy, fc = (k >> 2) & 1, (k >> 1) & 1, k & 1
            to = (x ^ fx if fx else x, y ^ fy if fy else y, c ^ fc if fc else c)
            cp = _remote(v_ref, o_ref.at[me], send.at[k - 1], recv.at[k - 1], to)
            cp.start()
            cps.append(cp)
        for k in range(1, 8):
            fx, fy, fc = (k >> 2) & 1, (k >> 1) & 1, k & 1
            px, py, pc = (x ^ fx if fx else x, y ^ fy if fy else y, c ^ fc if fc else c)
            cps[k - 1].wait_send()
            _remote(v_ref, o_ref.at[4 * px + 2 * py + pc], send.at[k - 1], recv.at[k - 1], (px, py, pc)).wait_recv()
        own.wait()

    return pl.pallas_call(
        body, name="gather_small_grads", in_specs=[ANY] * (1 + len(after)), out_specs=ANY,
        out_shape=jax.ShapeDtypeStruct((8,) + v.shape, v.dtype),
        scratch_shapes=[pltpu.SemaphoreType.DMA((7,)), pltpu.SemaphoreType.DMA((7,)), pltpu.SemaphoreType.DMA],
    )(v, *after)


def _row_tile(rows, cols, nbuf_bytes):
    tm = _pick(rows, 512, 16)
    while tm * cols * nbuf_bytes * 2 > VMEM_BUDGET_V7X and tm % 32 == 0:
        tm //= 2
    return tm


def pair_sum(g, r, c_idx, tag):
    _, _, rows, cols = g.shape
    tm = _row_tile(rows, cols, 2 + 2 + 2)
    nb = rows // tm

    def body(c_ref, g_ref, r_ref, o_ref):
        o_ref[...] = (g_ref[...].astype(F32) + r_ref[...].astype(F32)).astype(o_ref.dtype)

    gs = pltpu.PrefetchScalarGridSpec(
        num_scalar_prefetch=1, grid=(N_CHIPS, nb),
        in_specs=[pl.BlockSpec((None, None, tm, cols), lambda j, i, c_ref: (j, c_ref[0], i, 0)),
                  pl.BlockSpec((None, tm, cols), lambda j, i, c_ref: (j, i, 0))],
        out_specs=pl.BlockSpec((None, tm, cols), lambda j, i, c_ref: (j, i, 0)))
    return pl.pallas_call(body, name="grad_pair_sum_" + tag, grid_spec=gs,
                          out_shape=jax.ShapeDtypeStruct(r.shape, BF16),
                          compiler_params=_params(("arbitrary", "arbitrary")))(c_idx, g, r)


def chip_sum(s, r, j_idx, tag):
    _, rows, cols = s.shape
    tm = _row_tile(rows, cols, 2 + 3 * 2 + 4)
    nb = rows // tm

    def body(j_ref, s_ref, r_ref, o_ref):
        t = s_ref[...].astype(F32)
        for k in range(3):
            t = t + r_ref[k].astype(F32)
        o_ref[...] = t

    gs = pltpu.PrefetchScalarGridSpec(
        num_scalar_prefetch=1, grid=(nb,),
        in_specs=[pl.BlockSpec((None, tm, cols), lambda i, j_ref: (j_ref[0], i, 0)),
                  pl.BlockSpec((3, tm, cols), lambda i, j_ref: (0, i, 0))],
        out_specs=pl.BlockSpec((tm, cols), lambda i, j_ref: (i, 0)))
    return pl.pallas_call(body, name="grad_chip_sum_" + tag, grid_spec=gs,
                          out_shape=jax.ShapeDtypeStruct((rows, cols), F32),
                          compiler_params=_params(("arbitrary",)))(j_idx, s, r)


def adamw(w, g, m, v, *, name):
    rows, cols = w.shape
    tm = _row_tile(rows, cols, 7 * 4)

    return rowwise(_adamw_math, [w, g, m, v], [], [(cols, F32)] * 3, tm=tm, name=name)


def _adamw_math(wb, gb, mb, vb):
    m2 = ADAM_B1 * mb + (1.0 - ADAM_B1) * gb
    v2 = ADAM_B2 * vb + (1.0 - ADAM_B2) * (gb * gb)
    m_hat = m2 / (1.0 - ADAM_B1 ** ADAM_STEP)
    v_hat = v2 / (1.0 - ADAM_B2 ** ADAM_STEP)
    delta = -ADAM_LR * (m_hat / (jnp.sqrt(v_hat) + ADAM_EPS) + ADAM_WD * wb)
    return delta, m2, v2


def adamw_shard(w, g_own, g_sib, m, v, c_idx, *, name):
    rows, cols = g_own.shape
    tm = _row_tile(rows, cols, 9 * 4)
    nb = rows // tm

    def body(c_ref, w_ref, go_ref, gs_ref, m_ref, v_ref, g_out, d_out, m_out, v_out):
        gb = jnp.where(pl.program_id(0) == c_ref[0], go_ref[...], gs_ref[...])
        delta, m2, v2 = _adamw_math(w_ref[...], gb, m_ref[...], v_ref[...])
        g_out[...] = gb
        d_out[...] = delta
        m_out[...] = m2
        v_out[...] = v2

    full = pl.BlockSpec((tm, cols), lambda h, i, c_ref: (h * nb + i, 0))
    own = pl.BlockSpec((tm, cols), lambda h, i, c_ref: (jnp.where(h == c_ref[0], i, 0), 0))
    sib = pl.BlockSpec((tm, cols), lambda h, i, c_ref: (jnp.where(h == c_ref[0], 0, i), 0))
    gs = pltpu.PrefetchScalarGridSpec(num_scalar_prefetch=1, grid=(2, nb), in_specs=[full, own, sib, full, full],
                                      out_specs=[full] * 4)
    return pl.pallas_call(body, name=name, grid_spec=gs, out_shape=[jax.ShapeDtypeStruct(w.shape, F32)] * 4,
                          compiler_params=_params(("arbitrary", "arbitrary")))(c_idx, w, g_own, g_sib, m, v)


def sum_devices(a):
    def body(a_ref, o_ref):
        t = a_ref[0]
        for k in range(1, 8):
            t = t + a_ref[k]
        o_ref[...] = t

    return pl.pallas_call(body, name="sum_small_grads", out_shape=jax.ShapeDtypeStruct(a.shape[1:], a.dtype))(a)


def kernel(x, p, ffn1_norm, ffn1_w_gate, ffn1_w_up, ffn1_w_down, mix_norm, w_in, q_a_norm, w_uq, kv_a_norm, w_ukv, na_rpb, w_branch_a, w_branch_b, w_out, ffn2_norm, ffn2_w_gate, ffn2_w_up, ffn2_w_down, pl_norm, w_pl, w_pl_gate, final_norm, loss_target, m_ffn1_norm, m_ffn1_w_gate, m_ffn1_w_up, m_ffn1_w_down, m_mix_norm, m_w_in, m_q_a_norm, m_w_uq, m_kv_a_norm, m_w_ukv, m_na_rpb, m_w_branch_a, m_w_branch_b, m_w_out, m_ffn2_norm, m_ffn2_w_gate, m_ffn2_w_up, m_ffn2_w_down, m_pl_norm, m_w_pl, m_w_pl_gate, m_final_norm, v_ffn1_norm, v_ffn1_w_gate, v_ffn1_w_up, v_ffn1_w_down, v_mix_norm, v_w_in, v_q_a_norm, v_w_uq, v_kv_a_norm, v_w_ukv, v_na_rpb, v_w_branch_a, v_w_branch_b, v_w_out, v_ffn2_norm, v_ffn2_w_gate, v_ffn2_w_up, v_ffn2_w_down, v_pl_norm, v_w_pl, v_w_pl_gate, v_final_norm):
    big = ["ffn1_w_gate", "ffn1_w_up", "ffn1_w_down", "w_in", "w_uq", "w_ukv", "w_branch_a", "w_branch_b", "w_out",
           "ffn2_w_gate", "ffn2_w_up", "ffn2_w_down", "w_pl", "w_pl_gate"]
    col_sharded = {"ffn1_w_gate", "ffn1_w_up", "w_in", "w_uq", "w_ukv", "w_branch_a", "w_branch_b", "ffn2_w_gate",
                   "ffn2_w_up", "w_pl"}
    small = ["ffn1_norm", "mix_norm", "q_a_norm", "kv_a_norm", "na_rpb", "ffn2_norm", "pl_norm", "final_norm"]
    order = ["ffn1_norm", "ffn1_w_gate", "ffn1_w_up", "ffn1_w_down", "mix_norm", "w_in", "q_a_norm", "w_uq",
             "kv_a_norm", "w_ukv", "na_rpb", "w_branch_a", "w_branch_b", "w_out", "ffn2_norm", "ffn2_w_gate",
             "ffn2_w_up", "ffn2_w_down", "pl_norm", "w_pl", "w_pl_gate", "final_norm"]
    env = dict(locals())
    W = {n: env[n] for n in order}
    Mo = {n: env["m_" + n] for n in order}
    Vo = {n: env["v_" + n] for n in order}

    xs = x[0]
    S, D = xs.shape
    tgt = loss_target[0]
    ps = p[0, 0]
    NAW = w_branch_a.shape[1]
    MLAW = w_branch_b.shape[1]
    NH, MH = NAW // HEAD_DIM, MLAW // HEAD_DIM
    QR, KVR = w_uq.shape[1], w_ukv.shape[1]
    F = ffn1_w_down.shape[1] * N_CHIPS
    cx, cy, cc = _place()
    c_idx = jnp.reshape(cc, (1,)).astype(jnp.int32)
    j_idx = jnp.reshape(2 * cx + cy, (1,)).astype(jnp.int32)

    groups = [["ffn1_w_gate"], ["ffn1_w_up"], ["ffn1_w_down"], ["w_in"],
              ["w_uq", "w_ukv", "w_branch_a", "w_branch_b", "w_out"],
              ["ffn2_w_gate", "ffn2_w_up", "ffn2_w_down"], ["w_pl", "w_pl_gate"]]
    started, tokens = [], []
    for g, members in enumerate(groups):
        landings = [cast_own_block(W[n], j_idx, tokens[-1:], "cast_" + n) for n in members]
        sems, landings, token = gather_start(landings, tokens[-1:], str(g))
        started.append((sems, landings))
        tokens.append(token)
    gathered = {}

    def arrive(n, after):
        g = [n in members for members in groups].index(True)
        sems, landings = started[g]
        after = list(after) if isinstance(after, (list, tuple)) else [after]
        landed = gather_wait(sems, landings, after, str(g))
        gathered.update(zip(groups[g], gather_forward(landed, str(g))))

    def stacked(n, after=None):
        if n not in gathered:
            arrive(n, after)
        g = gathered[n]
        return g.reshape(N_CHIPS, 2 * g.shape[2], g.shape[3])

    def plain(n, after=None):
        if n in col_sharded:
            st = stacked(n, after)
            return st.transpose(1, 0, 2).reshape(st.shape[1], N_CHIPS * st.shape[2])
        if n not in gathered:
            arrive(n, after)
        g = gathered[n]
        return g.reshape(N_CHIPS * 2 * g.shape[2], g.shape[3])

    n_na = 3 * NAW
    n_front = n_na + QR + KVR
    n_in = n_front + MLA_ROPE + 2 * D
    off_ql, off_kvl, off_kr = 2 * D, 2 * D + QR, 2 * D + QR + KVR
    kr_w = 2 * LANES
    rest_w = off_kr + kr_w
    rest_ranges = [(n_front + MLA_ROPE, n_in), (n_na, n_front), (n_front, n_front + MLA_ROPE)]

    def shard_cols(st, lo, hi):
        nb, parts = st.shape[2], []
        while lo < hi:
            j = lo // nb
            end = min(hi, (j + 1) * nb)
            parts.append(st[j][:, lo - j * nb:end - j * nb])
            lo = end
        return parts

    def w_in_shards(g_na, g_rest):
        pieces = [(0, n_na, g_na, 0)]
        o = 0
        for lo, hi in rest_ranges:
            pieces.append((lo, hi, g_rest, o))
            o += hi - lo
        nb, shards = n_in // N_CHIPS, []
        for j in range(N_CHIPS):
            parts = []
            for lo, hi, src, o in sorted(pieces):
                a, b = max(lo, j * nb), min(hi, (j + 1) * nb)
                if a < b:
                    parts.append(src[:, o + a - lo:o + b - lo])
            shards.append(jnp.concatenate(parts, axis=1))
        return jnp.stack(shards)

    pos = jnp.arange(S, dtype=F32)
    inv_freq = 1.0 / (ROPE_THETA ** (jnp.arange(0, MLA_ROPE, 2, dtype=F32) / MLA_ROPE))
    ang = pos[:, None] * inv_freq[None, :]
    zpad = jnp.zeros((S, LANES - MLA_ROPE), F32)
    cos_t = jnp.concatenate([jnp.cos(ang), jnp.cos(ang), zpad], axis=1)
    sin_t = jnp.concatenate([-jnp.sin(ang), jnp.sin(ang), zpad], axis=1)

    def ffn_fwd(h, norm_g, tag, pre, after=()):
        n = norm_fwd(h, norm_g, name=f"{tag}_norm")
        g = mm(n, stacked(pre + "_w_gate", [n, *after]), name=f"{tag}_gate", b_stack=True)
        u, a = mm(n, stacked(pre + "_w_up", g), name=f"{tag}_up", b_stack=True, epilogue=swiglu_tile, epi_in=[g],
                  epi_out=[F32, BF16])
        h_out = mm(a, plain(pre + "_w_down", a), name=f"{tag}_down", res=h, alpha=0.5)
        return h_out, (n, g, u, a)

    def ffn_bwd(h, norm_g, saved, dh, dh_half, tag, pre, last, after=()):
        n, g, u, a = saved
        G[pre + "_w_down"] = mm(a, dh_half, name=f"{tag}_dw_down", ta=True, out_dtype=BF16, after=after)
        begun_d, token_d = pair_begin([pre + "_w_down"], tag + "_d")
        dg, du = mm(dh_half, plain(pre + "_w_down"), name=f"{tag}_da", tb=True, after=[token_d],
                    epilogue=swiglu_bwd_tile, epi_in=[g, u], epi_out=[BF16, BF16])
        G[pre + "_w_gate"] = mm(n, dg, name=f"{tag}_dw_gate", ta=True, out_dtype=BF16, out_stack=True)
        G[pre + "_w_up"] = mm(n, du, name=f"{tag}_dw_up", ta=True, out_dtype=BF16, out_stack=True)
        begun_gu, token_gu = pair_begin([pre + "_w_gate", pre + "_w_up"], tag + "_gu")
        dn = mm(dg, stacked(pre + "_w_gate"), name=f"{tag}_dn_gate", tb=True, b_stack=True, after=[token_gu])
        token = reduce_go([begun_d, begun_gu], tag, [dn])
        dn = mm(du, stacked(pre + "_w_up"), name=f"{tag}_dn_up", tb=True, b_stack=True, res=dn, after=[token])
        return norm_bwd(h, norm_g, dn, name=f"{tag}_dnorm", res=dh, bf16_alpha=None if last else 1.0)

    bias = na_bias(na_rpb[0], after=tokens[-1:])
    h1, ffn1_saved = ffn_fwd(xs, ffn1_norm, "ffn1", "ffn1", after=[bias, tokens[-1]])
    u_mix = norm_fwd(h1, mix_norm, name="mix_norm")
    win_st = stacked("w_in", u_mix)
    w_na = jnp.concatenate(shard_cols(win_st, 0, n_na), axis=1)
    w_rest = jnp.concatenate([p_ for lo, hi in rest_ranges for p_ in shard_cols(win_st, lo, hi)]
                             + [jnp.zeros((D, kr_w - MLA_ROPE), BF16)], axis=1)
    z_na = mm(u_mix, w_na, name="mix_in_na", out_dtype=BF16)
    z = mm(u_mix, w_rest, name="mix_in_rest")
    o_a = na_fwd(z_na, bias, NH, S)
    c_q = norm_fwd((z, QR, off_ql // QR), q_a_norm, name="q_a_norm")
    c_kv = norm_fwd((z, KVR, off_kvl // KVR), kv_a_norm, name="kv_a_norm")
    wuq = plain("w_uq", c_kv).reshape(QR, MH, MLA_QK)
    wuq_n = wuq[:, :, :MLA_NOPE].reshape(QR, MH * MLA_NOPE)
    wuq_r = jnp.pad(wuq[:, :, MLA_NOPE:], ((0, 0), (0, 0), (0, LANES - MLA_ROPE))).reshape(QR, MH * LANES)
    wukv = plain("w_ukv").reshape(KVR, MH, 2, HEAD_DIM)
    wuk = wukv[:, :, 0].reshape(KVR, MH * HEAD_DIM)
    wuv = wukv[:, :, 1].reshape(KVR, MH * HEAD_DIM)
    q_n = mm(c_q, wuq_n, name="mla_q_nope", out_dtype=BF16)
    q_r = rope(mm(c_q, wuq_r, name="mla_q_rope"), cos_t, sin_t, name="rope_q", out_dtype=BF16)
    k_n = mm(c_kv, wuk, name="mla_k_nope", out_dtype=BF16)
    v_m = mm(c_kv, wuv, name="mla_v", out_dtype=BF16)
    k_r = rope((z, LANES, off_kr // LANES), cos_t, sin_t, name="rope_k", out_dtype=BF16)
    o_b, lse = mla_fwd(q_n, q_r, k_n, v_m, k_r, MH, S)
    y_a = mm(o_a, stacked("w_branch_a"), name="branch_a", b_stack=True)
    y_b = mm(o_b, stacked("w_branch_b"), name="branch_b", b_stack=True)
    z_ga, z_gb = (z, D, 0), (z, D, 1)
    merged = rowwise(lambda ga, gb, ya, yb: _sig(ga) * ya + _sig(gb) * yb, [z_ga, z_gb, y_a, y_b], [], [(D, BF16)],
                     tm=256, name="merge")[0]
    h2 = mm(merged, plain("w_out"), name="mix_out", res=h1)
    h3, ffn2_saved = ffn_fwd(h2, ffn2_norm, "ffn2", "ffn2")
    n4 = norm_fwd(h3, pl_norm, name="pl_norm")
    pg_pre = mm(n4, plain("w_pl_gate", n4), name="pl_gate")
    pe = mm(ps, stacked("w_pl"), name="pl_embed", b_stack=True)

    def tail(h3b, pgb, peb, tb_, fg):
        pg = _sig(pgb)
        h4 = h3b + pg * peb
        r = _rstd(h4)
        xh = h4 * r
        err = xh * fg - tb_
        loss_rows = jnp.mean(err * err, axis=-1, keepdims=True)
        dy = err * (1.0 / D)
        dxh = dy * fg
        dh4 = r * (dxh - xh * jnp.mean(dxh * xh, axis=-1, keepdims=True))
        loss_part = jnp.broadcast_to(0.5 * jnp.sum(loss_rows, axis=0, keepdims=True), (1, LANES))
        return (dh4, dh4 * peb * pg * (1.0 - pg), dh4 * pg, loss_part, jnp.sum(dy * xh, axis=0, keepdims=True))

    dh4, dpg_pre, dpe, loss_part, g_final = rowwise(
        tail, [h3, pg_pre, pe, tgt], [final_norm.reshape(1, D)], [(D, F32), (D, BF16), (D, BF16)],
        accs=[(1, LANES), (1, D)], tm=128, name="loss_tail")
    loss = lax.psum(loss_part[0, 0], ("x", "y", "c"))

    G = {}
    pending = []

    def four(g):
        if g.ndim == 2:
            return g.reshape(N_CHIPS, 2, g.shape[0] // (2 * N_CHIPS), g.shape[1])
        return g.reshape(N_CHIPS, 2, g.shape[1] // 2, g.shape[2])

    def pair_begin(names, tag):
        g4 = [four(G[n]) for n in names]
        lands = [lax.empty((N_CHIPS,) + g.shape[2:], BF16) for g in g4]
        sems, g4, lands, token = pair_start(g4, lands, tag)
        return (names, tag, sems, g4, lands), token

    def reduce_go(begun, tag, after):
        names, sums = [], []
        for b_names, b_tag, sems, g4, lands in begun:
            g4, got = pair_wait(sems, g4, lands, list(after), b_tag)
            sums += [pair_sum(a_, r_, c_idx, n) for n, a_, r_ in zip(b_names, g4, got)]
            names += b_names
        lands = [lax.empty((N_CHIPS - 1,) + s_.shape[1:], BF16) for s_ in sums]
        sems, sums, lands, token = scatter_start(sums, lands, tag)
        pending.append((names, tag, sems, sums, lands))
        return token

    def reduce_finish(entry, after):
        names, tag, sems, sums, lands = entry
        sums, got = scatter_wait(sems, sums, lands, after, tag)
        halves = [chip_sum(a, b, j_idx, n) for n, a, b in zip(names, sums, got)]
        done = []
        for n, own, sib in zip(names, halves, half_exchange(halves, tag)):
            shp = W[n].shape
            two_d = lambda a_: a_.reshape(shp[1], shp[2])
            out = adamw_shard(two_d(W[n]), own, sib, two_d(Mo[n]), two_d(Vo[n]), c_idx, name="adamw_" + n)
            grads[n], delta[n], new_m[n], new_v[n] = [o.reshape(shp) for o in out]
            done.append(out[0])
        return done

    G["w_pl"] = mm(ps, dpe, name="pl_dw_embed", ta=True, out_dtype=BF16, out_stack=True)
    G["w_pl_gate"] = mm(n4, dpg_pre, name="pl_dw_gate", ta=True, out_dtype=BF16)
    begun_pl, token = pair_begin(["w_pl", "w_pl_gate"], "pl")
    dn4 = mm(dpg_pre, plain("w_pl_gate"), name="pl_dn", tb=True, after=[token])
    dh3, dh3_half, g_pl = norm_bwd(h3, pl_norm, dn4, name="pl_dnorm", res=dh4, bf16_alpha=0.5)
    token = reduce_go([begun_pl], "pl", [dh3])
    dh2, dh2_b, g_ffn2 = ffn_bwd(h2, ffn2_norm, ffn2_saved, dh3, dh3_half, "ffn2", "ffn2", last=False, after=[token])

    G["w_out"] = mm(merged, dh2_b, name="mix_dw_out", ta=True, out_dtype=BF16)
    dmerged = mm(dh2_b, plain("w_out"), name="mix_dmerged", tb=True)

    def merge_bwd(ga, gb, ya, yb, dm):
        sa, sb = _sig(ga), _sig(gb)
        dgates = jnp.concatenate([dm * ya * sa * (1.0 - sa), dm * yb * sb * (1.0 - sb)], axis=1)
        return dm * sa, dm * sb, dgates

    dy_a, dy_b, dz_rest = rowwise(merge_bwd, [z_ga, z_gb, y_a, y_b, dmerged], [],
                                  [(D, BF16), (D, BF16), (2 * D, BF16, 0)], tm=256, name="merge_bwd",
                                  into=(None, rest_w))
    G["w_branch_a"] = mm(o_a, dy_a, name="branch_a_dw", ta=True, out_dtype=BF16, out_stack=True)
    G["w_branch_b"] = mm(o_b, dy_b, name="branch_b_dw", ta=True, out_dtype=BF16, out_stack=True)
    do_a = mm(dy_a, stacked("w_branch_a"), name="branch_a_dx", tb=True, b_stack=True)
    do_b = mm(dy_b, stacked("w_branch_b"), name="branch_b_dx", tb=True, b_stack=True)
    dq_na, dk_na, dv_na, dbias = na_bwd(z_na, bias, do_a, NH, S)
    g_rpb = na_rpb_grad(dbias)
    dq_n, dq_rr, dk_n, dv_m, dk_rr = mla_bwd(q_n, q_r, k_n, v_m, k_r, lse, do_b, MH, S)
    dq_r = rope(dq_rr, cos_t, -sin_t, name="rope_q_bwd", out_dtype=BF16)
    dz_rest = rope(dk_rr, cos_t, -sin_t, name="rope_k_bwd", out_dtype=BF16, into=(dz_rest, rest_w),
                   cb=off_kr // kr_w, zero_cols=kr_w - LANES)
    gw_uq_n = mm(c_q, dq_n, name="mla_dw_q_nope", ta=True, out_dtype=BF16)
    gw_uq_r = mm(c_q, dq_r, name="mla_dw_q_rope", ta=True, out_dtype=BF16)
    dc_q = mm(dq_n, wuq_n, name="mla_dcq_nope", tb=True)
    dc_q = mm(dq_r, wuq_r, name="mla_dcq_rope", tb=True, res=dc_q)
    gw_uk = mm(c_kv, dk_n, name="mla_dw_k", ta=True, out_dtype=BF16)
    gw_uv = mm(c_kv, dv_m, name="mla_dw_v", ta=True, out_dtype=BF16)
    dc_kv = mm(dk_n, wuk, name="mla_dckv_k", tb=True)
    dc_kv = mm(dv_m, wuv, name="mla_dckv_v", tb=True, res=dc_kv)
    dz_rest, g_qa = norm_bwd((z, QR, off_ql // QR), q_a_norm, dc_q, name="q_a_dnorm", want_f32=False, bf16_alpha=1.0,
                             into=(dz_rest, rest_w), cb=off_ql // QR)
    dz_rest, g_kva = norm_bwd((z, KVR, off_kvl // KVR), kv_a_norm, dc_kv, name="kv_a_dnorm", want_f32=False,
                              bf16_alpha=1.0, into=(dz_rest, rest_w), cb=off_kvl // KVR)
    dz_na = jnp.concatenate([dq_na, dk_na, dv_na], axis=1)

    def to_stack(g2d):
        k, n = g2d.shape
        return g2d.reshape(k, N_CHIPS, n // N_CHIPS).transpose(1, 0, 2)

    gw_uq = jnp.concatenate([gw_uq_n.reshape(QR, MH, MLA_NOPE), gw_uq_r.reshape(QR, MH, LANES)[:, :, :MLA_ROPE]],
                            axis=2).reshape(QR, MH * MLA_QK)
    G["w_uq"] = to_stack(gw_uq)
    gw_ukv = jnp.stack([gw_uk.reshape(KVR, MH, HEAD_DIM), gw_uv.reshape(KVR, MH, HEAD_DIM)], axis=2)
    G["w_ukv"] = to_stack(gw_ukv.reshape(KVR, MH * 2 * HEAD_DIM))
    begun_mix, token = pair_begin(["w_out", "w_branch_a", "w_branch_b", "w_uq", "w_ukv"], "mix")
    gw_na = mm(u_mix, dz_na, name="mix_dw_in_na", ta=True, out_dtype=BF16, after=[token])
    gw_rest = mm(u_mix, dz_rest, name="mix_dw_in_rest", ta=True, out_dtype=BF16)
    G["w_in"] = w_in_shards(gw_na, gw_rest)
    begun_win, token_win = pair_begin(["w_in"], "win")
    token_mix = reduce_go([begun_mix], "mix", [gw_rest, token_win])
    du_mix = mm(dz_na, w_na, name="mix_du_na", tb=True, after=[token_mix])
    token_win = reduce_go([begun_win], "win", [du_mix])
    du_mix = mm(dz_rest, w_rest, name="mix_du_rest", tb=True, res=du_mix, after=[token_win])
    dh1, dh1_half, g_mix = norm_bwd(h1, mix_norm, du_mix, name="mix_dnorm", res=dh2, bf16_alpha=0.5)
    grad_x, g_ffn1 = ffn_bwd(xs, ffn1_norm, ffn1_saved, dh1, dh1_half, "ffn1", "ffn1", last=True)

    small_g = {"ffn1_norm": g_ffn1, "mix_norm": g_mix, "q_a_norm": g_qa, "kv_a_norm": g_kva, "na_rpb": g_rpb,
               "ffn2_norm": g_ffn2, "pl_norm": g_pl, "final_norm": g_final}
    sizes = [int(np.prod(W[n].shape)) for n in small]
    total = sum(sizes)
    padded = -(-total // (8 * LANES)) * (8 * LANES)

    def pack(parts):
        flat = jnp.concatenate([jnp.reshape(parts[n], (-1,)).astype(F32) for n in small]
                               + [jnp.zeros((padded - total,), F32)])
        return flat.reshape(padded // LANES, LANES)

    def unpack(a):
        flat, out, o = a.reshape(-1), {}, 0
        for n, sz in zip(small, sizes):
            out[n] = flat[o:o + sz].reshape(W[n].shape)
            o += sz
        return out

    grads, delta, new_m, new_v = {}, {}, {}, {}
    after = [grad_x]
    for entry in pending:
        after = reduce_finish(entry, after)

    g_small = sum_devices(gather_small(pack(small_g), after))
    d_small, m_small, v_small = adamw(pack(W), g_small, pack(Mo), pack(Vo), name="adamw_small")
    for full, part in ((grads, g_small), (delta, d_small), (new_m, m_small), (new_v, v_small)):
        full.update(unpack(part))

    return (loss, grad_x[None], *[grads[n] for n in order], *[delta[n] for n in order],
            *[new_m[n] for n in order], *[new_v[n] for n in order])
```

```python
import functools

import numpy as np
import jax
import jax.numpy as jnp
from jax import lax
from jax.experimental import pallas as pl
from jax.experimental.pallas import tpu as pltpu

F32 = jnp.float32
BF16 = jnp.bfloat16

VMEM_LIMIT_V7X = 56 * 1024 * 1024
VMEM_BUDGET_V7X = 40 * 1024 * 1024
LANES = 128

GRID_W = 64
NA_WIN_ROWS = 8
NA_WIN_COLS = 16
HEAD_DIM = 128
MLA_NOPE = 128
MLA_ROPE = 64
MLA_QK = MLA_NOPE + MLA_ROPE
ROPE_THETA = 10000.0
NORM_EPS = 1e-6
NEG_INF = -1e30
N_CHIPS = 4

ADAM_LR = 0.001
ADAM_B1 = 0.9
ADAM_B2 = 0.999
ADAM_EPS = 1e-08
ADAM_WD = 0.01
ADAM_STEP = 10

MESH = pl.DeviceIdType.MESH
ANY = pl.BlockSpec(memory_space=pl.ANY)


def _params(sem=None):
    return pltpu.CompilerParams(dimension_semantics=sem, vmem_limit_bytes=VMEM_LIMIT_V7X)


def _pick(n, target, align):
    best = None
    t = align
    while t <= min(n, target):
        if n % t == 0:
            best = t
        t += align
    return n if best is None else best


def mm(a, b, *, name, ta=False, tb=False, out_dtype=F32, res=None, alpha=1.0, b_stack=False, out_stack=False,
       exact=False, after=(), epilogue=None, epi_in=(), epi_out=()):
    K, M = (a.shape if ta else a.shape[::-1])
    nst = kb = nb = None
    if b_stack:
        nst = b.shape[0]
        if tb:
            N, kb = b.shape[1], b.shape[2]
            Kb = nst * kb
        else:
            Kb, nb = b.shape[1], b.shape[2]
            N = nst * nb
    else:
        N, Kb = (b.shape if tb else b.shape[::-1])
    assert K == Kb, (a.shape, b.shape, ta, tb)
    if out_stack:
        assert N % N_CHIPS == 0
    n_unit = N // N_CHIPS if out_stack else (nb if nb is not None else N)
    tn = _pick(n_unit, 512, LANES) if n_unit % 512 == 0 or n_unit <= 512 else _pick(n_unit, 1536, LANES)
    if ta and n_unit == N and 4 * K * N * jnp.dtype(b.dtype).itemsize <= VMEM_BUDGET_V7X:
        tn = N
    m_align = LANES if ta else 16
    tm = _pick(M, 1024, m_align)
    isz = lambda t: jnp.dtype(t.dtype).itemsize
    out_dtypes = list(epi_out) if epilogue is not None else [out_dtype]
    osz = sum(jnp.dtype(t).itemsize for t in out_dtypes) + sum(isz(e) for e in epi_in)

    def vmem(tm_, tn_):
        return (2 * tm_ * K * isz(a) + 2 * K * tn_ * isz(b) + 2 * tm_ * tn_ * osz + tm_ * tn_ * 4
                + (tm_ * K * 2 if ta else 0) + (2 * tm_ * tn_ * isz(res) if res is not None else 0))

    while vmem(tm, tn) > VMEM_BUDGET_V7X and tm % 2 == 0 and (tm // 2) % m_align == 0:
        tm //= 2
    while vmem(tm, tn) > VMEM_BUDGET_V7X and tn % 2 == 0 and (tn // 2) % LANES == 0 and n_unit % (tn // 2) == 0:
        tn //= 2
    assert vmem(tm, tn) <= VMEM_BUDGET_V7X, (name, tm, tn, K)

    a_spec = pl.BlockSpec((K, tm), lambda i, j: (0, i)) if ta else pl.BlockSpec((tm, K), lambda i, j: (i, 0))
    if b_stack and not tb:
        q = nb // tn
        b_spec = pl.BlockSpec((None, K, tn), lambda i, j: (j // q, 0, j % q))
    elif b_stack and tb:
        b_spec = pl.BlockSpec((nst, tn, kb), lambda i, j: (0, j, 0))
    elif tb:
        b_spec = pl.BlockSpec((tn, K), lambda i, j: (j, 0))
    else:
        b_spec = pl.BlockSpec((K, tn), lambda i, j: (0, j))
    if out_stack:
        qo = (N // N_CHIPS) // tn
        o_spec = pl.BlockSpec((None, tm, tn), lambda i, j: (j // qo, i, j % qo))
        o_shapes = [jax.ShapeDtypeStruct((N_CHIPS, M, N // N_CHIPS), t) for t in out_dtypes]
    else:
        o_spec = pl.BlockSpec((tm, tn), lambda i, j: (i, j))
        o_shapes = [jax.ShapeDtypeStruct((M, N), t) for t in out_dtypes]
    has_res = res is not None
    n_in = 2 + has_res + len(epi_in) + len(after)
    nn = (((1,), (0,)), ((), ()))
    nt = (((1,), (1,)), ((), ()))

    def body(*refs):
        a_ref, b_ref = refs[:2]
        r_ref = refs[2] if has_res else None
        e_refs = refs[2 + has_res:2 + has_res + len(epi_in)]
        o_refs = refs[n_in:n_in + len(out_dtypes)]
        if ta:
            at_ref = refs[-1]

            @pl.when(pl.program_id(1) == 0)
            def _():
                at_ref[...] = a_ref[...].astype(BF16).T

            lhs = at_ref[...]
        elif exact:
            lhs = a_ref[...]
        else:
            lhs = a_ref[...].astype(BF16)
        if exact:
            total = lax.dot_general(lhs, b_ref[...], nt if tb else nn, preferred_element_type=F32,
                                    precision=lax.Precision.HIGHEST)
        elif b_stack and tb:
            total = None
            for s in range(nst):
                part = lax.dot_general(lhs[:, s * kb:(s + 1) * kb], b_ref[s].astype(BF16), nt,
                                       preferred_element_type=F32)
                total = part if total is None else total + part
        else:
            total = lax.dot_general(lhs, b_ref[...].astype(BF16), nt if tb else nn, preferred_element_type=F32)
        if alpha != 1.0:
            total = total * alpha
        if has_res:
            total = total + r_ref[...].astype(F32)
        vals = epilogue(total, *[e[...] for e in e_refs]) if epilogue is not None else (total,)
        for o_ref, v in zip(o_refs, vals):
            o_ref[...] = v.astype(o_ref.dtype)

    tile = pl.BlockSpec((tm, tn), lambda i, j: (i, j))
    in_specs = [a_spec, b_spec] + [tile] * (has_res + len(epi_in)) + [ANY] * len(after)
    args = [a, b] + ([res] if has_res else []) + list(epi_in) + list(after)
    outs = pl.pallas_call(
        body, name=name, grid=(M // tm, N // tn), in_specs=in_specs, out_specs=[o_spec] * len(out_dtypes),
        out_shape=o_shapes, scratch_shapes=[pltpu.VMEM((tm, K), BF16)] if ta else [],
        compiler_params=_params(("parallel", "arbitrary")),
    )(*args)
    return outs if epilogue is not None else outs[0]


def rowwise(fn, rows, consts, outs, accs=(), *, tm, name, tn=None, into=None):
    rows = [r if isinstance(r, tuple) else (r, r.shape[1], 0) for r in rows]
    S = rows[0][0].shape[0]
    tm = _pick(S, tm, 16)
    nrow, ncon, nout = len(rows), len(consts), len(outs)
    outs = [o if len(o) == 3 else (o[0], o[1], None) for o in outs]
    if tn is None:
        grid = (S // tm,)
        in_specs = [pl.BlockSpec((tm, w), functools.partial(lambda i, cb: (i, cb), cb=cb)) for _, w, cb in rows]
        in_specs += [pl.BlockSpec(c.shape, lambda i: (0, 0)) for c in consts]
        out_specs = [pl.BlockSpec((tm, n), functools.partial(lambda i, cb: (i, cb), cb=cb or 0)) for n, _, cb in outs]
        out_specs += [pl.BlockSpec(s, lambda i: (0, 0)) for s in accs]
        sem = ("arbitrary",)
    else:
        assert not accs
        N = rows[0][1]
        grid = (S // tm, N // tn)
        in_specs = [pl.BlockSpec((tm, tn), lambda i, j: (i, j)) for _ in rows]
        in_specs += [pl.BlockSpec(c.shape, lambda i, j: (0, 0)) for c in consts]
        out_specs = [pl.BlockSpec((tm, tn), lambda i, j: (i, j)) for _ in outs]
        sem = ("parallel", "parallel")
    out_shape = [jax.ShapeDtypeStruct((S, n if cb is None else into[1]), dt) for n, dt, cb in outs]
    out_shape += [jax.ShapeDtypeStruct(s, F32) for s in accs]
    extra, aliases = [], {}
    if into is not None and into[0] is not None:
        extra = [into[0]]
        aliases = {nrow + ncon: [cb is not None for _, _, cb in outs].index(True)}

    def body(*refs):
        vals = fn(*[r[...] for r in refs[:nrow + ncon]])
        if not isinstance(vals, (tuple, list)):
            vals = (vals,)
        o_refs = refs[nrow + ncon + len(extra):]
        for o_ref, v in zip(o_refs[:nout], vals[:nout]):
            o_ref[...] = v.astype(o_ref.dtype)
        if accs:
            first = pl.program_id(0) == 0

            def accumulate(a_ref, v):
                @pl.when(first)
                def _():
                    a_ref[...] = v

                @pl.when(jnp.logical_not(first))
                def _():
                    a_ref[...] += v

            for a_ref, v in zip(o_refs[nout:], vals[nout:]):
                accumulate(a_ref, v.astype(F32))

    return pl.pallas_call(
        body, name=name, grid=grid, in_specs=in_specs + [ANY] * len(extra), out_specs=out_specs, out_shape=out_shape,
        input_output_aliases=aliases, compiler_params=_params(sem),
    )(*[r[0] for r in rows], *consts, *extra)


def _rstd(x):
    return lax.rsqrt(jnp.mean(x * x, axis=-1, keepdims=True) + NORM_EPS)


def norm_fwd(x, g, *, name, tm=256):
    w = x[1] if isinstance(x, tuple) else x.shape[1]

    def fn(xb, gb):
        return (xb * _rstd(xb)) * gb

    return rowwise(fn, [x], [g], [(w, BF16)], tm=tm, name=name)[0]


def norm_bwd(x, g, dn, *, name, res=None, want_f32=True, bf16_alpha=None, tm=256, into=None, cb=None):
    w = x[1] if isinstance(x, tuple) else x.shape[1]
    has_res = res is not None

    def fn(*blocks):
        if has_res:
            xb, dnb, rb, gb = blocks
        else:
            xb, dnb, gb = blocks
        r = _rstd(xb)
        xh = xb * r
        dxh = dnb * gb
        dx = r * (dxh - xh * jnp.mean(dxh * xh, axis=-1, keepdims=True))
        if has_res:
            dx = dx + rb
        out = []
        if want_f32:
            out.append(dx)
        if bf16_alpha is not None:
            out.append(dx * bf16_alpha if bf16_alpha != 1.0 else dx)
        out.append(jnp.sum(dnb * xh, axis=0, keepdims=True))
        return tuple(out)

    outs = ([(w, F32)] if want_f32 else []) + ([(w, BF16, cb)] if bf16_alpha is not None else [])
    rows = [x, dn] + ([res] if has_res else [])
    return rowwise(fn, rows, [g], outs, accs=[(1, w)], tm=tm, name=name, into=into)


def _sig(x):
    return jax.nn.sigmoid(x)


def swiglu_tile(ub, gb):
    return ub, gb * _sig(gb) * ub


def swiglu_bwd_tile(dab, gb, ub):
    sg = _sig(gb)
    return dab * ub * (sg + gb * sg * (1.0 - sg)), dab * (gb * sg)


def rope(x, cos, sin_signed, *, name, out_dtype, into=None, cb=None, zero_cols=0):
    w = x[1] if isinstance(x, tuple) else x.shape[1]
    half = MLA_ROPE // 2

    def fn(xb, cb, sb):
        lane = lax.broadcasted_iota(jnp.int32, cb.shape, 1)
        outs = []
        for hb in range(w // LANES):
            blk = xb[:, hb * LANES:(hb + 1) * LANES]
            partner = jnp.where(lane < half, pltpu.roll(blk, LANES - half, 1), pltpu.roll(blk, half, 1))
            outs.append(blk * cb + partner * sb)
        if zero_cols:
            outs.append(jnp.zeros((xb.shape[0], zero_cols), xb.dtype))
        return outs[0] if len(outs) == 1 else jnp.concatenate(outs, axis=1)

    return rowwise(fn, [x, cos, sin_signed], [], [(w + zero_cols, out_dtype, cb)], tm=256, name=name, into=into)[0]


def _na_tables():
    cols = np.arange(GRID_W)
    kw = NA_WIN_COLS
    dc = np.clip(cols[None, :] - cols[:, None], -(kw - 1), kw - 1) + (kw - 1)
    onehot = np.zeros((LANES, GRID_W * GRID_W), np.float32)
    onehot[dc.reshape(-1), np.arange(GRID_W * GRID_W)] = 1.0
    col_start = np.clip(cols - kw // 2, 0, GRID_W - kw)
    mask = (cols[None, :] >= col_start[:, None]) & (cols[None, :] < col_start[:, None] + kw)
    return onehot, np.where(mask, 0.0, NEG_INF).astype(np.float32)


def na_bias(rpb, after=()):
    H = rpb.shape[0]
    nr, kh = 2 * NA_WIN_ROWS - 1, NA_WIN_ROWS
    onehot, maskb = _na_tables()
    rp = jnp.pad(rpb.reshape(H * nr, 2 * NA_WIN_COLS - 1), ((0, 0), (0, LANES - (2 * NA_WIN_COLS - 1))))
    t1 = mm(rp, jnp.asarray(onehot), name="na_bias_table", exact=True, after=after).reshape(H, nr, GRID_W, GRID_W)
    t1 = t1 + jnp.asarray(maskb)[None, None]
    per_t = [jnp.stack([t1[:, i - t + kh - 1] for i in range(kh)], axis=2) for t in range(kh)]
    return jnp.stack(per_t, axis=1).reshape(H, kh, GRID_W, kh * GRID_W)


def na_rpb_grad(db):
    H = db.shape[0]
    nr, kh = 2 * NA_WIN_ROWS - 1, NA_WIN_ROWS
    onehot, _ = _na_tables()
    shifted = [jnp.pad(db[:, t], ((0, 0), (0, 0), ((kh - 1 - t) * GRID_W, t * GRID_W))) for t in range(kh)]
    dw1 = functools.reduce(jnp.add, shifted)
    dt1 = dw1.reshape(H, GRID_W, nr, GRID_W).transpose(0, 2, 1, 3).reshape(H * nr, GRID_W * GRID_W)
    g = mm(dt1, jnp.asarray(onehot), name="na_rpb_grad", tb=True, exact=True)
    return g[:, :2 * NA_WIN_COLS - 1].reshape(H, nr, 2 * NA_WIN_COLS - 1)


def _na_first_row(r, rows):
    return jnp.clip(r - NA_WIN_ROWS // 2, 0, rows - NA_WIN_ROWS)


NA_ROWS_PER_STEP = 16


def _na_probs(q, k_ref, b_ref, r, rows):
    first = _na_first_row(r, rows)
    start = pl.multiple_of(first * GRID_W, GRID_W)
    k = k_ref[pl.ds(start, NA_WIN_ROWS * GRID_W), :]
    s = lax.dot_general(q, k, (((1,), (1,)), ((), ())), preferred_element_type=F32)
    s = s * (HEAD_DIM ** -0.5) + b_ref[r - first]
    m = jnp.max(s, axis=-1, keepdims=True)
    e = jnp.exp(s - m)
    return k, e / jnp.sum(e, axis=-1, keepdims=True), start, r - first


def na_fwd(z, bias, H, S):
    rows = S // GRID_W
    nkeys = NA_WIN_ROWS * GRID_W
    rb = _pick(rows, NA_ROWS_PER_STEP, 1)

    def body(q_ref, k_ref, v_ref, b_ref, o_ref):
        for j in range(rb):
            r = pl.program_id(1) * rb + j
            rows_j = pl.ds(j * GRID_W, GRID_W)
            _, p, start, _ = _na_probs(q_ref[rows_j, :], k_ref, b_ref, r, rows)
            v = v_ref[pl.ds(start, nkeys), :]
            o_ref[rows_j, :] = jnp.dot(p.astype(BF16), v, preferred_element_type=F32).astype(o_ref.dtype)

    return pl.pallas_call(
        body, name="na_fwd", grid=(H, rows // rb),
        in_specs=[pl.BlockSpec((rb * GRID_W, HEAD_DIM), lambda h, i: (i, h)),
                  pl.BlockSpec((S, HEAD_DIM), lambda h, i: (0, H + h)),
                  pl.BlockSpec((S, HEAD_DIM), lambda h, i: (0, 2 * H + h)),
                  pl.BlockSpec((None, NA_WIN_ROWS, GRID_W, nkeys), lambda h, i: (h, 0, 0, 0))],
        out_specs=pl.BlockSpec((rb * GRID_W, HEAD_DIM), lambda h, i: (i, h)),
        out_shape=jax.ShapeDtypeStruct((S, H * HEAD_DIM), BF16),
        compiler_params=_params(("parallel", "arbitrary")),
    )(z, z, z, bias)


def na_bwd(z, bias, do, H, S):
    rows = S // GRID_W
    nkeys = NA_WIN_ROWS * GRID_W
    rb = _pick(rows, NA_ROWS_PER_STEP, 1)
    tn_dims = (((0,), (0,)), ((), ()))

    def body(q_ref, k_ref, v_ref, b_ref, do_ref, dq_ref, dk_ref, dv_ref, db_ref, dk_acc, dv_acc):
        i = pl.program_id(1)

        @pl.when(i == 0)
        def _():
            dk_acc[...] = jnp.zeros_like(dk_acc)
            dv_acc[...] = jnp.zeros_like(dv_acc)
            db_ref[...] = jnp.zeros_like(db_ref)

        for j in range(rb):
            rows_j = pl.ds(j * GRID_W, GRID_W)
            q = q_ref[rows_j, :]
            k, p, start, t = _na_probs(q, k_ref, b_ref, i * rb + j, rows)
            keys = pl.ds(start, nkeys)
            dob = do_ref[rows_j, :].astype(BF16)
            dp = lax.dot_general(dob, v_ref[keys, :], (((1,), (1,)), ((), ())), preferred_element_type=F32)
            ds = p * (dp - jnp.sum(dp * p, axis=-1, keepdims=True))
            dsb = (ds * (HEAD_DIM ** -0.5)).astype(BF16)
            dq_ref[rows_j, :] = jnp.dot(dsb, k, preferred_element_type=F32).astype(dq_ref.dtype)
            dk_acc[keys, :] += lax.dot_general(dsb, q, tn_dims, preferred_element_type=F32)
            dv_acc[keys, :] += lax.dot_general(p.astype(BF16), dob, tn_dims, preferred_element_type=F32)
            db_ref[t] += ds

        @pl.when(i == rows // rb - 1)
        def _():
            dk_ref[...] = dk_acc[...].astype(dk_ref.dtype)
            dv_ref[...] = dv_acc[...].astype(dv_ref.dtype)

    W = H * HEAD_DIM
    qspec = pl.BlockSpec((rb * GRID_W, HEAD_DIM), lambda h, i: (i, h))
    bspec = pl.BlockSpec((None, NA_WIN_ROWS, GRID_W, nkeys), lambda h, i: (h, 0, 0, 0))
    return pl.pallas_call(
        body, name="na_bwd", grid=(H, rows // rb),
        in_specs=[qspec, pl.BlockSpec((S, HEAD_DIM), lambda h, i: (0, H + h)),
                  pl.BlockSpec((S, HEAD_DIM), lambda h, i: (0, 2 * H + h)), bspec, qspec],
        out_specs=[qspec, pl.BlockSpec((S, HEAD_DIM), lambda h, i: (0, h)),
                   pl.BlockSpec((S, HEAD_DIM), lambda h, i: (0, h)), bspec],
        out_shape=[jax.ShapeDtypeStruct((S, W), BF16)] * 3 + [jax.ShapeDtypeStruct((H, NA_WIN_ROWS, GRID_W, nkeys), F32)],
        scratch_shapes=[pltpu.VMEM((S, HEAD_DIM), F32)] * 2,
        compiler_params=_params(("arbitrary", "arbitrary")),
    )(z, z, z, bias, do)


def _mla_keys(kn_ref, kr_ref, kcat):
    @pl.when(pl.program_id(1) == 0)
    def _():
        kcat[:, :HEAD_DIM] = kn_ref[...]
        kcat[:, HEAD_DIM:] = kr_ref[...]


MLA_LOG2_SCALE = (MLA_QK ** -0.5) * 1.4426950408889634


def _mla_scores(qn_ref, qr_ref, kcat):
    qcat = jnp.concatenate([qn_ref[...], qr_ref[...]], axis=1)
    return qcat, lax.dot_general(qcat, kcat[...], (((1,), (1,)), ((), ())), preferred_element_type=F32)


def mla_fwd(qn, qr, kn, v, kr, H, S):
    tq = _pick(S, 256, 16)

    def body(qn_ref, qr_ref, kn_ref, v_ref, kr_ref, o_ref, lse_ref, kcat):
        _mla_keys(kn_ref, kr_ref, kcat)
        _, s = _mla_scores(qn_ref, qr_ref, kcat)
        m = jnp.max(s, axis=-1, keepdims=True)
        e = jnp.exp2((s - m) * MLA_LOG2_SCALE)
        l = jnp.sum(e, axis=-1, keepdims=True)
        o = jnp.dot(e.astype(BF16), v_ref[...], preferred_element_type=F32)
        o_ref[...] = (o / l).astype(o_ref.dtype)
        lse_ref[...] = jnp.broadcast_to(m * MLA_LOG2_SCALE + jnp.log2(l), lse_ref.shape)

    qspec = pl.BlockSpec((tq, HEAD_DIM), lambda h, i: (i, h))
    kspec = pl.BlockSpec((S, HEAD_DIM), lambda h, i: (0, h))
    return pl.pallas_call(
        body, name="mla_fwd", grid=(H, S // tq),
        in_specs=[qspec, qspec, kspec, kspec, pl.BlockSpec((S, LANES), lambda h, i: (0, 0))],
        out_specs=[qspec, qspec],
        out_shape=[jax.ShapeDtypeStruct((S, H * HEAD_DIM), BF16), jax.ShapeDtypeStruct((S, H * LANES), F32)],
        scratch_shapes=[pltpu.VMEM((S, 2 * HEAD_DIM), BF16)],
        compiler_params=_params(("parallel", "arbitrary")),
    )(qn, qr, kn, v, kr)


def mla_bwd(qn, qr, kn, v, kr, lse, do, H, S):
    tq = _pick(S, 512, 16)
    nt = (((1,), (1,)), ((), ()))
    tn_dims = (((0,), (0,)), ((), ()))

    def body(qn_ref, qr_ref, kn_ref, v_ref, kr_ref, lse_ref, do_ref, dqn_ref, dqr_ref, dkn_ref, dv_ref, dkr_ref, kcat):
        h, i = pl.program_id(0), pl.program_id(1)
        _mla_keys(kn_ref, kr_ref, kcat)
        qcat, s = _mla_scores(qn_ref, qr_ref, kcat)
        p = jnp.exp2(s * MLA_LOG2_SCALE - lse_ref[:, 0:1])
        dob = do_ref[...].astype(BF16)
        dp = lax.dot_general(dob, v_ref[...], nt, preferred_element_type=F32)
        ds = p * (dp - jnp.sum(dp * p, axis=-1, keepdims=True))
        dsb = (ds * (MLA_QK ** -0.5)).astype(BF16)
        dq = jnp.dot(dsb, kcat[...], preferred_element_type=F32)
        dqn_ref[...] = dq[:, :HEAD_DIM].astype(dqn_ref.dtype)
        dqr_ref[...] = dq[:, HEAD_DIM:].astype(dqr_ref.dtype)

        @pl.when(i == 0)
        def _():
            dkn_ref[...] = jnp.zeros_like(dkn_ref)
            dv_ref[...] = jnp.zeros_like(dv_ref)

        @pl.when(jnp.logical_and(i == 0, h == 0))
        def _():
            dkr_ref[...] = jnp.zeros_like(dkr_ref)

        dk = lax.dot_general(dsb, qcat, tn_dims, preferred_element_type=F32)
        dkn_ref[...] += dk[:, :HEAD_DIM]
        dkr_ref[...] += dk[:, HEAD_DIM:]
        dv_ref[...] += lax.dot_general(p.astype(BF16), dob, tn_dims, preferred_element_type=F32)

    qspec = pl.BlockSpec((tq, HEAD_DIM), lambda h, i: (i, h))
    kspec = pl.BlockSpec((S, HEAD_DIM), lambda h, i: (0, h))
    rspec = pl.BlockSpec((S, LANES), lambda h, i: (0, 0))
    W = H * HEAD_DIM
    return pl.pallas_call(
        body, name="mla_bwd", grid=(H, S // tq),
        in_specs=[qspec, qspec, kspec, kspec, rspec, qspec, qspec],
        out_specs=[qspec, qspec, kspec, kspec, rspec],
        out_shape=[jax.ShapeDtypeStruct((S, W), BF16), jax.ShapeDtypeStruct((S, W), F32),
                   jax.ShapeDtypeStruct((S, W), F32), jax.ShapeDtypeStruct((S, W), F32),
                   jax.ShapeDtypeStruct((S, LANES), F32)],
        scratch_shapes=[pltpu.VMEM((S, 2 * HEAD_DIM), BF16)],
        compiler_params=_params(("arbitrary", "arbitrary")),
    )(qn, qr, kn, v, kr, lse, do)


def _place():
    return lax.axis_index("x"), lax.axis_index("y"), lax.axis_index("c")


def _other_chips(x, y):
    return [(1 - x, y), (x, 1 - y), (1 - x, 1 - y)]


def _remote(src, dst, send_sem, recv_sem, to):
    return pltpu.make_async_remote_copy(src_ref=src, dst_ref=dst, send_sem=send_sem, recv_sem=recv_sem,
                                        device_id=to, device_id_type=MESH)


HBM = pl.BlockSpec(memory_space=pltpu.HBM)
SEM = pl.BlockSpec(memory_space=pltpu.SEMAPHORE)
EFFECT = pltpu.SideEffectType.DATAFLOW_SIDE_EFFECTING


def _in_hbm(a):
    return pltpu.with_memory_space_constraint(a, pltpu.HBM)


TOKEN = jax.ShapeDtypeStruct((8, LANES), F32)
IN_VMEM = pl.BlockSpec(memory_space=pltpu.VMEM)


def cast_own_block(w, me_idx, after, name):
    _, k, n = w.shape
    rows = k // 2
    tm = _row_tile(rows, n, 4 + 2)
    nb = rows // tm

    def body(me_ref, w_ref, *rest):
        rest[len(after)][...] = w_ref[...].astype(BF16)

    gs = pltpu.PrefetchScalarGridSpec(
        num_scalar_prefetch=1, grid=(2, nb),
        in_specs=[pl.BlockSpec((None, tm, n), lambda h, i, me_ref: (0, h * nb + i, 0))] + [ANY] * len(after),
        out_specs=pl.BlockSpec((None, None, tm, n), lambda h, i, me_ref: (me_ref[0], h, i, 0)))
    return pl.pallas_call(body, name=name, grid_spec=gs, out_shape=jax.ShapeDtypeStruct((N_CHIPS, 2, rows, n), BF16),
                          compiler_params=_params(("arbitrary", "arbitrary")))(me_idx, w, *after)


def gather_start(landings, after, tag):
    n = len(landings)

    def body(*refs):
        lands = refs[:n]
        send, recv = refs[n + len(after)], refs[n + len(after) + 1]
        token = refs[-1]
        x, y, c = _place()
        me = 2 * x + y
        for w in range(n):
            for k, (px, py) in enumerate(_other_chips(x, y)):
                blk = lands[w].at[me, c]
                _remote(blk, blk, send.at[3 * w + k], recv.at[3 * w + k], (px, py, c)).start()
        token[...] = jnp.zeros_like(token)

    outs = pl.pallas_call(
        body, name="gather_start_" + tag,
        out_shape=(pltpu.SemaphoreType.DMA((3 * n,)),) * 2 + tuple(pltpu.HBM(b.shape, b.dtype) for b in landings)
        + (TOKEN,),
        in_specs=[HBM] * n + [ANY] * len(after), out_specs=tuple([SEM, SEM] + [HBM] * n + [IN_VMEM]),
        input_output_aliases={i: 2 + i for i in range(n)},
        compiler_params=pltpu.CompilerParams(has_side_effects=EFFECT),
    )(*[_in_hbm(b) for b in landings], *after)
    return (outs[0], outs[1]), outs[2:2 + n], outs[-1]


def gather_wait(sems, landings, after, tag):
    n = len(landings)
    send, recv = sems

    def body(*refs):
        lands = refs[:n]
        send_sem, recv_sem = refs[n], refs[n + 1]
        x, y, c = _place()
        me = 2 * x + y
        for w in range(n):
            for k, (px, py) in enumerate(_other_chips(x, y)):
                cp = _remote(lands[w].at[me, c], lands[w].at[2 * px + py, c], send_sem.at[3 * w + k],
                             recv_sem.at[3 * w + k], (px, py, c))
                cp.wait_send()
                cp.wait_recv()

    return pl.pallas_call(
        body, name="gather_wait_" + tag, out_shape=tuple(pltpu.HBM(b.shape, b.dtype) for b in landings),
        in_specs=[HBM] * n + [SEM, SEM] + [ANY] * len(after), out_specs=tuple([HBM] * n),
        input_output_aliases={i: i for i in range(n)},
        compiler_params=pltpu.CompilerParams(has_side_effects=EFFECT),
    )(*landings, send, recv, *after)


def gather_forward(landings, tag):
    n = len(landings)

    def body(*refs):
        ins, outs = refs[:n], refs[n:2 * n]
        send, recv = refs[2 * n:]
        x, y, c = _place()
        sibling = (x, y, 1 - c)
        cps = []
        for w in range(n):
            for k, (px, py) in enumerate(_other_chips(x, y)):
                j = 2 * px + py
                cp = _remote(ins[w].at[j, c], outs[w].at[j, c], send.at[3 * w + k], recv.at[3 * w + k], sibling)
                cp.start()
                cps.append(cp)
        for w in range(n):
            for k, (px, py) in enumerate(_other_chips(x, y)):
                blk = outs[w].at[2 * px + py, 1 - c]
                _remote(blk, blk, send.at[3 * w + k], recv.at[3 * w + k], sibling).wait_recv()
        for cp in cps:
            cp.wait_send()

    return pl.pallas_call(
        body, name="gather_forward_" + tag, in_specs=[ANY] * n, out_specs=[ANY] * n,
        out_shape=[jax.ShapeDtypeStruct(a.shape, a.dtype) for a in landings],
        input_output_aliases={i: i for i in range(n)},
        scratch_shapes=[pltpu.SemaphoreType.DMA((3 * n,)), pltpu.SemaphoreType.DMA((3 * n,))],
    )(*landings)


def pair_start(grads, landings, tag):
    n = len(grads)

    def body(*refs):
        ins, lands = refs[:n], refs[n:2 * n]
        send, recv = refs[2 * n], refs[2 * n + 1]
        token = refs[-1]
        x, y, c = _place()
        for w in range(n):
            _remote(ins[w].at[:, 1 - c], lands[w], send.at[w], recv.at[w], (x, y, 1 - c)).start()
        token[...] = jnp.zeros_like(token)

    bufs = list(grads) + list(landings)
    outs = pl.pallas_call(
        body, name="pair_start_" + tag,
        out_shape=(pltpu.SemaphoreType.DMA((n,)),) * 2 + tuple(pltpu.HBM(b.shape, b.dtype) for b in bufs) + (TOKEN,),
        in_specs=[HBM] * (2 * n), out_specs=tuple([SEM, SEM] + [HBM] * (2 * n) + [IN_VMEM]),
        input_output_aliases={i: 2 + i for i in range(2 * n)},
        compiler_params=pltpu.CompilerParams(has_side_effects=EFFECT),
    )(*[_in_hbm(b) for b in bufs])
    return (outs[0], outs[1]), outs[2:2 + n], outs[2 + n:2 + 2 * n], outs[-1]


def pair_wait(sems, grads, landings, after, tag):
    n = len(grads)

    def body(*refs):
        ins, lands = refs[:n], refs[n:2 * n]
        send, recv = refs[2 * n], refs[2 * n + 1]
        x, y, c = _place()
        for w in range(n):
            cp = _remote(ins[w].at[:, 1 - c], lands[w], send.at[w], recv.at[w], (x, y, 1 - c))
            cp.wait_send()
            cp.wait_recv()

    bufs = list(grads) + list(landings)
    outs = pl.pallas_call(
        body, name="pair_wait_" + tag, out_shape=tuple(pltpu.HBM(b.shape, b.dtype) for b in bufs),
        in_specs=[HBM] * (2 * n) + [SEM, SEM] + [ANY] * len(after), out_specs=tuple([HBM] * (2 * n)),
        input_output_aliases={i: i for i in range(2 * n)},
        compiler_params=pltpu.CompilerParams(has_side_effects=EFFECT),
    )(*bufs, sems[0], sems[1], *after)
    return outs[:n], outs[n:]


def scatter_start(sums, landings, tag):
    n = len(sums)

    def body(*refs):
        ins, lands = refs[:n], refs[n:2 * n]
        send, recv = refs[2 * n], refs[2 * n + 1]
        token = refs[-1]
        x, y, c = _place()
        for w in range(n):
            for k, (px, py) in enumerate(_other_chips(x, y)):
                _remote(ins[w].at[2 * px + py], lands[w].at[k], send.at[3 * w + k], recv.at[3 * w + k], (px, py, c)).start()
        token[...] = jnp.zeros_like(token)

    bufs = list(sums) + list(landings)
    outs = pl.pallas_call(
        body, name="scatter_start_" + tag,
        out_shape=(pltpu.SemaphoreType.DMA((3 * n,)),) * 2 + tuple(pltpu.HBM(b.shape, b.dtype) for b in bufs) + (TOKEN,),
        in_specs=[HBM] * (2 * n), out_specs=tuple([SEM, SEM] + [HBM] * (2 * n) + [IN_VMEM]),
        input_output_aliases={i: 2 + i for i in range(2 * n)},
        compiler_params=pltpu.CompilerParams(has_side_effects=EFFECT),
    )(*[_in_hbm(b) for b in bufs])
    return (outs[0], outs[1]), outs[2:2 + n], outs[2 + n:2 + 2 * n], outs[-1]


def scatter_wait(sems, sums, landings, after, tag):
    n = len(sums)

    def body(*refs):
        ins, lands = refs[:n], refs[n:2 * n]
        send, recv = refs[2 * n], refs[2 * n + 1]
        x, y, c = _place()
        for w in range(n):
            for k, (px, py) in enumerate(_other_chips(x, y)):
                cp = _remote(ins[w].at[2 * px + py], lands[w].at[k], send.at[3 * w + k], recv.at[3 * w + k], (px, py, c))
                cp.wait_send()
                cp.wait_recv()

    bufs = list(sums) + list(landings)
    outs = pl.pallas_call(
        body, name="scatter_wait_" + tag, out_shape=tuple(pltpu.HBM(b.shape, b.dtype) for b in bufs),
        in_specs=[HBM] * (2 * n) + [SEM, SEM] + [ANY] * len(after), out_specs=tuple([HBM] * (2 * n)),
        input_output_aliases={i: i for i in range(2 * n)},
        compiler_params=pltpu.CompilerParams(has_side_effects=EFFECT),
    )(*bufs, sems[0], sems[1], *after)
    return outs[:n], outs[n:]


def half_exchange(halves, tag):
    n = len(halves)

    def body(*refs):
        ins, outs = refs[:n], refs[n:2 * n]
        send, recv = refs[2 * n:]
        x, y, c = _place()
        cps = []
        for w in range(n):
            cp = _remote(ins[w], outs[w], send.at[w], recv.at[w], (x, y, 1 - c))
            cp.start()
            cps.append(cp)
        for cp in cps:
            cp.wait()

    return pl.pallas_call(
        body, name="grad_half_exchange_" + tag, in_specs=[ANY] * n, out_specs=[ANY] * n,
        out_shape=[jax.ShapeDtypeStruct(h.shape, h.dtype) for h in halves],
        scratch_shapes=[pltpu.SemaphoreType.DMA((n,)), pltpu.SemaphoreType.DMA((n,))],
    )(*halves)


def gather_small(v, after=()):
    def body(*refs):
        v_ref = refs[0]
        o_ref, send, recv, local = refs[1 + len(after):]
        x, y, c = _place()
        me = 4 * x + 2 * y + c
        own = pltpu.make_async_copy(v_ref, o_ref.at[me], local)
        own.start()
        cps = []
        for k in range(1, 8):
            fx, fy, fc = (k >> 2) & 1, (k >> 1) & 1, k & 1
            to = (x ^ fx if fx else x, y ^ fy if fy else y, c ^ fc if fc else c)
            cp = _remote(v_ref, o_ref.at[me], send.at[k - 1], recv.at[k - 1], to)
            cp.start()
            cps.append(cp)
        for k in range(1, 8):
            fx, fy, fc = (k >> 2) & 1, (k >> 1) & 1, k & 1
            px, py, pc = (x ^ fx if fx else x, y ^ fy if fy else y, c ^ fc if fc else c)
            cps[k - 1].wait_send()
            _remote(v_ref, o_ref.at[4 * px + 2 * py + pc], send.at[k - 1], recv.at[k - 1], (px, py, pc)).wait_recv()
        own.wait()

    return pl.pallas_call(
        body, name="gather_small_grads", in_specs=[ANY] * (1 + len(after)), out_specs=ANY,
        out_shape=jax.ShapeDtypeStruct((8,) + v.shape, v.dtype),
        scratch_shapes=[pltpu.SemaphoreType.DMA((7,)), pltpu.SemaphoreType.DMA((7,)), pltpu.SemaphoreType.DMA],
    )(v, *after)


def _row_tile(rows, cols, nbuf_bytes):
    tm = _pick(rows, 512, 16)
    while tm * cols * nbuf_bytes * 2 > VMEM_BUDGET_V7X and tm % 32 == 0:
        tm //= 2
    return tm


def pair_sum(g, r, c_idx, tag):
    _, _, rows, cols = g.shape
    tm = _row_tile(rows, cols, 2 + 2 + 2)
    nb = rows // tm

    def body(c_ref, g_ref, r_ref, o_ref):
        o_ref[...] = (g_ref[...].astype(F32) + r_ref[...].astype(F32)).astype(o_ref.dtype)

    gs = pltpu.PrefetchScalarGridSpec(
        num_scalar_prefetch=1, grid=(N_CHIPS, nb),
        in_specs=[pl.BlockSpec((None, None, tm, cols), lambda j, i, c_ref: (j, c_ref[0], i, 0)),
                  pl.BlockSpec((None, tm, cols), lambda j, i, c_ref: (j, i, 0))],
        out_specs=pl.BlockSpec((None, tm, cols), lambda j, i, c_ref: (j, i, 0)))
    return pl.pallas_call(body, name="grad_pair_sum_" + tag, grid_spec=gs,
                          out_shape=jax.ShapeDtypeStruct(r.shape, BF16),
                          compiler_params=_params(("arbitrary", "arbitrary")))(c_idx, g, r)


def chip_sum(s, r, j_idx, tag):
    _, rows, cols = s.shape
    tm = _row_tile(rows, cols, 2 + 3 * 2 + 4)
    nb = rows // tm

    def body(j_ref, s_ref, r_ref, o_ref):
        t = s_ref[...].astype(F32)
        for k in range(3):
            t = t + r_ref[k].astype(F32)
        o_ref[...] = t

    gs = pltpu.PrefetchScalarGridSpec(
        num_scalar_prefetch=1, grid=(nb,),
        in_specs=[pl.BlockSpec((None, tm, cols), lambda i, j_ref: (j_ref[0], i, 0)),
                  pl.BlockSpec((3, tm, cols), lambda i, j_ref: (0, i, 0))],
        out_specs=pl.BlockSpec((tm, cols), lambda i, j_ref: (i, 0)))
    return pl.pallas_call(body, name="grad_chip_sum_" + tag, grid_spec=gs,
                          out_shape=jax.ShapeDtypeStruct((rows, cols), F32),
                          compiler_params=_params(("arbitrary",)))(j_idx, s, r)


def adamw(w, g, m, v, *, name):
    rows, cols = w.shape
    tm = _row_tile(rows, cols, 7 * 4)

    return rowwise(_adamw_math, [w, g, m, v], [], [(cols, F32)] * 3, tm=tm, name=name)


def _adamw_math(wb, gb, mb, vb):
    m2 = ADAM_B1 * mb + (1.0 - ADAM_B1) * gb
    v2 = ADAM_B2 * vb + (1.0 - ADAM_B2) * (gb * gb)
    m_hat = m2 / (1.0 - ADAM_B1 ** ADAM_STEP)
    v_hat = v2 / (1.0 - ADAM_B2 ** ADAM_STEP)
    delta = -ADAM_LR * (m_hat / (jnp.sqrt(v_hat) + ADAM_EPS) + ADAM_WD * wb)
    return delta, m2, v2


def adamw_shard(w, g_own, g_sib, m, v, c_idx, *, name):
    rows, cols = g_own.shape
    tm = _row_tile(rows, cols, 9 * 4)
    nb = rows // tm

    def body(c_ref, w_ref, go_ref, gs_ref, m_ref, v_ref, g_out, d_out, m_out, v_out):
        gb = jnp.where(pl.program_id(0) == c_ref[0], go_ref[...], gs_ref[...])
        delta, m2, v2 = _adamw_math(w_ref[...], gb, m_ref[...], v_ref[...])
        g_out[...] = gb
        d_out[...] = delta
        m_out[...] = m2
        v_out[...] = v2

    full = pl.BlockSpec((tm, cols), lambda h, i, c_ref: (h * nb + i, 0))
    own = pl.BlockSpec((tm, cols), lambda h, i, c_ref: (jnp.where(h == c_ref[0], i, 0), 0))
    sib = pl.BlockSpec((tm, cols), lambda h, i, c_ref: (jnp.where(h == c_ref[0], 0, i), 0))
    gs = pltpu.PrefetchScalarGridSpec(num_scalar_prefetch=1, grid=(2, nb), in_specs=[full, own, sib, full, full],
                                      out_specs=[full] * 4)
    return pl.pallas_call(body, name=name, grid_spec=gs, out_shape=[jax.ShapeDtypeStruct(w.shape, F32)] * 4,
                          compiler_params=_params(("arbitrary", "arbitrary")))(c_idx, w, g_own, g_sib, m, v)


def sum_devices(a):
    def body(a_ref, o_ref):
        t = a_ref[0]
        for k in range(1, 8):
            t = t + a_ref[k]
        o_ref[...] = t

    return pl.pallas_call(body, name="sum_small_grads", out_shape=jax.ShapeDtypeStruct(a.shape[1:], a.dtype))(a)


def kernel(x, p, ffn1_norm, ffn1_w_gate, ffn1_w_up, ffn1_w_down, mix_norm, w_in, q_a_norm, w_uq, kv_a_norm, w_ukv, na_rpb, w_branch_a, w_branch_b, w_out, ffn2_norm, ffn2_w_gate, ffn2_w_up, ffn2_w_down, pl_norm, w_pl, w_pl_gate, final_norm, loss_target, m_ffn1_norm, m_ffn1_w_gate, m_ffn1_w_up, m_ffn1_w_down, m_mix_norm, m_w_in, m_q_a_norm, m_w_uq, m_kv_a_norm, m_w_ukv, m_na_rpb, m_w_branch_a, m_w_branch_b, m_w_out, m_ffn2_norm, m_ffn2_w_gate, m_ffn2_w_up, m_ffn2_w_down, m_pl_norm, m_w_pl, m_w_pl_gate, m_final_norm, v_ffn1_norm, v_ffn1_w_gate, v_ffn1_w_up, v_ffn1_w_down, v_mix_norm, v_w_in, v_q_a_norm, v_w_uq, v_kv_a_norm, v_w_ukv, v_na_rpb, v_w_branch_a, v_w_branch_b, v_w_out, v_ffn2_norm, v_ffn2_w_gate, v_ffn2_w_up, v_ffn2_w_down, v_pl_norm, v_w_pl, v_w_pl_gate, v_final_norm):
    big = ["ffn1_w_gate", "ffn1_w_up", "ffn1_w_down", "w_in", "w_uq", "w_ukv", "w_branch_a", "w_branch_b", "w_out",
           "ffn2_w_gate", "ffn2_w_up", "ffn2_w_down", "w_pl", "w_pl_gate"]
    col_sharded = {"ffn1_w_gate", "ffn1_w_up", "w_in", "w_uq", "w_ukv", "w_branch_a", "w_branch_b", "ffn2_w_gate",
                   "ffn2_w_up", "w_pl"}
    small = ["ffn1_norm", "mix_norm", "q_a_norm", "kv_a_norm", "na_rpb", "ffn2_norm", "pl_norm", "final_norm"]
    order = ["ffn1_norm", "ffn1_w_gate", "ffn1_w_up", "ffn1_w_down", "mix_norm", "w_in", "q_a_norm", "w_uq",
             "kv_a_norm", "w_ukv", "na_rpb", "w_branch_a", "w_branch_b", "w_out", "ffn2_norm", "ffn2_w_gate",
             "ffn2_w_up", "ffn2_w_down", "pl_norm", "w_pl", "w_pl_gate", "final_norm"]
    env = dict(locals())
    W = {n: env[n] for n in order}
    Mo = {n: env["m_" + n] for n in order}
    Vo = {n: env["v_" + n] for n in order}

    xs = x[0]
    S, D = xs.shape
    tgt = loss_target[0]
    ps = p[0, 0]
    NAW = w_branch_a.shape[1]
    MLAW = w_branch_b.shape[1]
    NH, MH = NAW // HEAD_DIM, MLAW // HEAD_DIM
    QR, KVR = w_uq.shape[1], w_ukv.shape[1]
    F = ffn1_w_down.shape[1] * N_CHIPS
    cx, cy, cc = _place()
    c_idx = jnp.reshape(cc, (1,)).astype(jnp.int32)
    j_idx = jnp.reshape(2 * cx + cy, (1,)).astype(jnp.int32)

    groups = [["ffn1_w_gate"], ["ffn1_w_up"], ["ffn1_w_down"], ["w_in"],
              ["w_uq", "w_ukv", "w_branch_a", "w_branch_b", "w_out"],
              ["ffn2_w_gate", "ffn2_w_up", "ffn2_w_down"], ["w_pl", "w_pl_gate"]]
    started, tokens = [], []
    for g, members in enumerate(groups):
        landings = [cast_own_block(W[n], j_idx, tokens[-1:], "cast_" + n) for n in members]
        sems, landings, token = gather_start(landings, tokens[-1:], str(g))
        started.append((sems, landings))
        tokens.append(token)
    gathered = {}

    def arrive(n, after):
        g = [n in members for members in groups].index(True)
        sems, landings = started[g]
        after = list(after) if isinstance(after, (list, tuple)) else [after]
        landed = gather_wait(sems, landings, after, str(g))
        gathered.update(zip(groups[g], gather_forward(landed, str(g))))

    def stacked(n, after=None):
        if n not in gathered:
            arrive(n, after)
        g = gathered[n]
        return g.reshape(N_CHIPS, 2 * g.shape[2], g.shape[3])

    def plain(n, after=None):
        if n in col_sharded:
            st = stacked(n, after)
            return st.transpose(1, 0, 2).reshape(st.shape[1], N_CHIPS * st.shape[2])
        if n not in gathered:
            arrive(n, after)
        g = gathered[n]
        return g.reshape(N_CHIPS * 2 * g.shape[2], g.shape[3])

    n_na = 3 * NAW
    n_front = n_na + QR + KVR
    n_in = n_front + MLA_ROPE + 2 * D
    off_ql, off_kvl, off_kr = 2 * D, 2 * D + QR, 2 * D + QR + KVR
    kr_w = 2 * LANES
    rest_w = off_kr + kr_w
    rest_ranges = [(n_front + MLA_ROPE, n_in), (n_na, n_front), (n_front, n_front + MLA_ROPE)]

    def shard_cols(st, lo, hi):
        nb, parts = st.shape[2], []
        while lo < hi:
            j = lo // nb
            end = min(hi, (j + 1) * nb)
            parts.append(st[j][:, lo - j * nb:end - j * nb])
            lo = end
        return parts

    def w_in_shards(g_na, g_rest):
        pieces = [(0, n_na, g_na, 0)]
        o = 0
        for lo, hi in rest_ranges:
            pieces.append((lo, hi, g_rest, o))
            o += hi - lo
        nb, shards = n_in // N_CHIPS, []
        for j in range(N_CHIPS):
            parts = []
            for lo, hi, src, o in sorted(pieces):
                a, b = max(lo, j * nb), min(hi, (j + 1) * nb)
                if a < b:
                    parts.append(src[:, o + a - lo:o + b - lo])
            shards.append(jnp.concatenate(parts, axis=1))
        return jnp.stack(shards)

    pos = jnp.arange(S, dtype=F32)
    inv_freq = 1.0 / (ROPE_THETA ** (jnp.arange(0, MLA_ROPE, 2, dtype=F32) / MLA_ROPE))
    ang = pos[:, None] * inv_freq[None, :]
    zpad = jnp.zeros((S, LANES - MLA_ROPE), F32)
    cos_t = jnp.concatenate([jnp.cos(ang), jnp.cos(ang), zpad], axis=1)
    sin_t = jnp.concatenate([-jnp.sin(ang), jnp.sin(ang), zpad], axis=1)

    def ffn_fwd(h, norm_g, tag, pre, after=()):
        n = norm_fwd(h, norm_g, name=f"{tag}_norm")
        g = mm(n, stacked(pre + "_w_gate", [n, *after]), name=f"{tag}_gate", b_stack=True)
        u, a = mm(n, stacked(pre + "_w_up", g), name=f"{tag}_up", b_stack=True, epilogue=swiglu_tile, epi_in=[g],
                  epi_out=[F32, BF16])
        h_out = mm(a, plain(pre + "_w_down", a), name=f"{tag}_down", res=h, alpha=0.5)
        return h_out, (n, g, u, a)

    def ffn_bwd(h, norm_g, saved, dh, dh_half, tag, pre, last, after=()):
        n, g, u, a = saved
        G[pre + "_w_down"] = mm(a, dh_half, name=f"{tag}_dw_down", ta=True, out_dtype=BF16, after=after)
        begun_d, token_d = pair_begin([pre + "_w_down"], tag + "_d")
        dg, du = mm(dh_half, plain(pre + "_w_down"), name=f"{tag}_da", tb=True, after=[token_d],
                    epilogue=swiglu_bwd_tile, epi_in=[g, u], epi_out=[BF16, BF16])
        G[pre + "_w_gate"] = mm(n, dg, name=f"{tag}_dw_gate", ta=True, out_dtype=BF16, out_stack=True)
        G[pre + "_w_up"] = mm(n, du, name=f"{tag}_dw_up", ta=True, out_dtype=BF16, out_stack=True)
        begun_gu, token_gu = pair_begin([pre + "_w_gate", pre + "_w_up"], tag + "_gu")
        dn = mm(dg, stacked(pre + "_w_gate"), name=f"{tag}_dn_gate", tb=True, b_stack=True, after=[token_gu])
        token = reduce_go([begun_d, begun_gu], tag, [dn])
        dn = mm(du, stacked(pre + "_w_up"), name=f"{tag}_dn_up", tb=True, b_stack=True, res=dn, after=[token])
        return norm_bwd(h, norm_g, dn, name=f"{tag}_dnorm", res=dh, bf16_alpha=None if last else 1.0)

    bias = na_bias(na_rpb[0], after=tokens[-1:])
    h1, ffn1_saved = ffn_fwd(xs, ffn1_norm, "ffn1", "ffn1", after=[bias, tokens[-1]])
    u_mix = norm_fwd(h1, mix_norm, name="mix_norm")
    win_st = stacked("w_in", u_mix)
    w_na = jnp.concatenate(shard_cols(win_st, 0, n_na), axis=1)
    w_rest = jnp.concatenate([p_ for lo, hi in rest_ranges for p_ in shard_cols(win_st, lo, hi)]
                             + [jnp.zeros((D, kr_w - MLA_ROPE), BF16)], axis=1)
    z_na = mm(u_mix, w_na, name="mix_in_na", out_dtype=BF16)
    z = mm(u_mix, w_rest, name="mix_in_rest")
    o_a = na_fwd(z_na, bias, NH, S)
    c_q = norm_fwd((z, QR, off_ql // QR), q_a_norm, name="q_a_norm")
    c_kv = norm_fwd((z, KVR, off_kvl // KVR), kv_a_norm, name="kv_a_norm")
    wuq = plain("w_uq", c_kv).reshape(QR, MH, MLA_QK)
    wuq_n = wuq[:, :, :MLA_NOPE].reshape(QR, MH * MLA_NOPE)
    wuq_r = jnp.pad(wuq[:, :, MLA_NOPE:], ((0, 0), (0, 0), (0, LANES - MLA_ROPE))).reshape(QR, MH * LANES)
    wukv = plain("w_ukv").reshape(KVR, MH, 2, HEAD_DIM)
    wuk = wukv[:, :, 0].reshape(KVR, MH * HEAD_DIM)
    wuv = wukv[:, :, 1].reshape(KVR, MH * HEAD_DIM)
    q_n = mm(c_q, wuq_n, name="mla_q_nope", out_dtype=BF16)
    q_r = rope(mm(c_q, wuq_r, name="mla_q_rope"), cos_t, sin_t, name="rope_q", out_dtype=BF16)
    k_n = mm(c_kv, wuk, name="mla_k_nope", out_dtype=BF16)
    v_m = mm(c_kv, wuv, name="mla_v", out_dtype=BF16)
    k_r = rope((z, LANES, off_kr // LANES), cos_t, sin_t, name="rope_k", out_dtype=BF16)
    o_b, lse = mla_fwd(q_n, q_r, k_n, v_m, k_r, MH, S)
    y_a = mm(o_a, stacked("w_branch_a"), name="branch_a", b_stack=True)
    y_b = mm(o_b, stacked("w_branch_b"), name="branch_b", b_stack=True)
    z_ga, z_gb = (z, D, 0), (z, D, 1)
    merged = rowwise(lambda ga, gb, ya, yb: _sig(ga) * ya + _sig(gb) * yb, [z_ga, z_gb, y_a, y_b], [], [(D, BF16)],
                     tm=256, name="merge")[0]
    h2 = mm(merged, plain("w_out"), name="mix_out", res=h1)
    h3, ffn2_saved = ffn_fwd(h2, ffn2_norm, "ffn2", "ffn2")
    n4 = norm_fwd(h3, pl_norm, name="pl_norm")
    pg_pre = mm(n4, plain("w_pl_gate", n4), name="pl_gate")
    pe = mm(ps, stacked("w_pl"), name="pl_embed", b_stack=True)

    def tail(h3b, pgb, peb, tb_, fg):
        pg = _sig(pgb)
        h4 = h3b + pg * peb
        r = _rstd(h4)
        xh = h4 * r
        err = xh * fg - tb_
        loss_rows = jnp.mean(err * err, axis=-1, keepdims=True)
        dy = err * (1.0 / D)
        dxh = dy * fg
        dh4 = r * (dxh - xh * jnp.mean(dxh * xh, axis=-1, keepdims=True))
        loss_part = jnp.broadcast_to(0.5 * jnp.sum(loss_rows, axis=0, keepdims=True), (1, LANES))
        return (dh4, dh4 * peb * pg * (1.0 - pg), dh4 * pg, loss_part, jnp.sum(dy * xh, axis=0, keepdims=True))

    dh4, dpg_pre, dpe, loss_part, g_final = rowwise(
        tail, [h3, pg_pre, pe, tgt], [final_norm.reshape(1, D)], [(D, F32), (D, BF16), (D, BF16)],
        accs=[(1, LANES), (1, D)], tm=128, name="loss_tail")
    loss = lax.psum(loss_part[0, 0], ("x", "y", "c"))

    G = {}
    pending = []

    def four(g):
        if g.ndim == 2:
            return g.reshape(N_CHIPS, 2, g.shape[0] // (2 * N_CHIPS), g.shape[1])
        return g.reshape(N_CHIPS, 2, g.shape[1] // 2, g.shape[2])

    def pair_begin(names, tag):
        g4 = [four(G[n]) for n in names]
        lands = [lax.empty((N_CHIPS,) + g.shape[2:], BF16) for g in g4]
        sems, g4, lands, token = pair_start(g4, lands, tag)
        return (names, tag, sems, g4, lands), token

    def reduce_go(begun, tag, after):
        names, sums = [], []
        for b_names, b_tag, sems, g4, lands in begun:
            g4, got = pair_wait(sems, g4, lands, list(after), b_tag)
            sums += [pair_sum(a_, r_, c_idx, n) for n, a_, r_ in zip(b_names, g4, got)]
            names += b_names
        lands = [lax.empty((N_CHIPS - 1,) + s_.shape[1:], BF16) for s_ in sums]
        sems, sums, lands, token = scatter_start(sums, lands, tag)
        pending.append((names, tag, sems, sums, lands))
        return token

    def reduce_finish(entry, after):
        names, tag, sems, sums, lands = entry
        sums, got = scatter_wait(sems, sums, lands, after, tag)
        halves = [chip_sum(a, b, j_idx, n) for n, a, b in zip(names, sums, got)]
        done = []
        for n, own, sib in zip(names, halves, half_exchange(halves, tag)):
            shp = W[n].shape
            two_d = lambda a_: a_.reshape(shp[1], shp[2])
            out = adamw_shard(two_d(W[n]), own, sib, two_d(Mo[n]), two_d(Vo[n]), c_idx, name="adamw_" + n)
            grads[n], delta[n], new_m[n], new_v[n] = [o.reshape(shp) for o in out]
            done.append(out[0])
        return done

    G["w_pl"] = mm(ps, dpe, name="pl_dw_embed", ta=True, out_dtype=BF16, out_stack=True)
    G["w_pl_gate"] = mm(n4, dpg_pre, name="pl_dw_gate", ta=True, out_dtype=BF16)
    begun_pl, token = pair_begin(["w_pl", "w_pl_gate"], "pl")
    dn4 = mm(dpg_pre, plain("w_pl_gate"), name="pl_dn", tb=True, after=[token])
    dh3, dh3_half, g_pl = norm_bwd(h3, pl_norm, dn4, name="pl_dnorm", res=dh4, bf16_alpha=0.5)
    token = reduce_go([begun_pl], "pl", [dh3])
    dh2, dh2_b, g_ffn2 = ffn_bwd(h2, ffn2_norm, ffn2_saved, dh3, dh3_half, "ffn2", "ffn2", last=False, after=[token])

    G["w_out"] = mm(merged, dh2_b, name="mix_dw_out", ta=True, out_dtype=BF16)
    dmerged = mm(dh2_b, plain("w_out"), name="mix_dmerged", tb=True)

    def merge_bwd(ga, gb, ya, yb, dm):
        sa, sb = _sig(ga), _sig(gb)
        dgates = jnp.concatenate([dm * ya * sa * (1.0 - sa), dm * yb * sb * (1.0 - sb)], axis=1)
        return dm * sa, dm * sb, dgates

    dy_a, dy_b, dz_rest = rowwise(merge_bwd, [z_ga, z_gb, y_a, y_b, dmerged], [],
                                  [(D, BF16), (D, BF16), (2 * D, BF16, 0)], tm=256, name="merge_bwd",
                                  into=(None, rest_w))
    G["w_branch_a"] = mm(o_a, dy_a, name="branch_a_dw", ta=True, out_dtype=BF16, out_stack=True)
    G["w_branch_b"] = mm(o_b, dy_b, name="branch_b_dw", ta=True, out_dtype=BF16, out_stack=True)
    do_a = mm(dy_a, stacked("w_branch_a"), name="branch_a_dx", tb=True, b_stack=True)
    do_b = mm(dy_b, stacked("w_branch_b"), name="branch_b_dx", tb=True, b_stack=True)
    dq_na, dk_na, dv_na, dbias = na_bwd(z_na, bias, do_a, NH, S)
    g_rpb = na_rpb_grad(dbias)
    dq_n, dq_rr, dk_n, dv_m, dk_rr = mla_bwd(q_n, q_r, k_n, v_m, k_r, lse, do_b, MH, S)
    dq_r = rope(dq_rr, cos_t, -sin_t, name="rope_q_bwd", out_dtype=BF16)
    dz_rest = rope(dk_rr, cos_t, -sin_t, name="rope_k_bwd", out_dtype=BF16, into=(dz_rest, rest_w),
                   cb=off_kr // kr_w, zero_cols=kr_w - LANES)
    gw_uq_n = mm(c_q, dq_n, name="mla_dw_q_nope", ta=True, out_dtype=BF16)
    gw_uq_r = mm(c_q, dq_r, name="mla_dw_q_rope", ta=True, out_dtype=BF16)
    dc_q = mm(dq_n, wuq_n, name="mla_dcq_nope", tb=True)
    dc_q = mm(dq_r, wuq_r, name="mla_dcq_rope", tb=True, res=dc_q)
    gw_uk = mm(c_kv, dk_n, name="mla_dw_k", ta=True, out_dtype=BF16)
    gw_uv = mm(c_kv, dv_m, name="mla_dw_v", ta=True, out_dtype=BF16)
    dc_kv = mm(dk_n, wuk, name="mla_dckv_k", tb=True)
    dc_kv = mm(dv_m, wuv, name="mla_dckv_v", tb=True, res=dc_kv)
    dz_rest, g_qa = norm_bwd((z, QR, off_ql // QR), q_a_norm, dc_q, name="q_a_dnorm", want_f32=False, bf16_alpha=1.0,
                             into=(dz_rest, rest_w), cb=off_ql // QR)
    dz_rest, g_kva = norm_bwd((z, KVR, off_kvl // KVR), kv_a_norm, dc_kv, name="kv_a_dnorm", want_f32=False,
                              bf16_alpha=1.0, into=(dz_rest, rest_w), cb=off_kvl // KVR)
    dz_na = jnp.concatenate([dq_na, dk_na, dv_na], axis=1)

    def to_stack(g2d):
        k, n = g2d.shape
        return g2d.reshape(k, N_CHIPS, n // N_CHIPS).transpose(1, 0, 2)

    gw_uq = jnp.concatenate([gw_uq_n.reshape(QR, MH, MLA_NOPE), gw_uq_r.reshape(QR, MH, LANES)[:, :, :MLA_ROPE]],
                            axis=2).reshape(QR, MH * MLA_QK)
    G["w_uq"] = to_stack(gw_uq)
    gw_ukv = jnp.stack([gw_uk.reshape(KVR, MH, HEAD_DIM), gw_uv.reshape(KVR, MH, HEAD_DIM)], axis=2)
    G["w_ukv"] = to_stack(gw_ukv.reshape(KVR, MH * 2 * HEAD_DIM))
    begun_mix, token = pair_begin(["w_out", "w_branch_a", "w_branch_b", "w_uq", "w_ukv"], "mix")
    gw_na = mm(u_mix, dz_na, name="mix_dw_in_na", ta=True, out_dtype=BF16, after=[token])
    gw_rest = mm(u_mix, dz_rest, name="mix_dw_in_rest", ta=True, out_dtype=BF16)
    G["w_in"] = w_in_shards(gw_na, gw_rest)
    begun_win, token_win = pair_begin(["w_in"], "win")
    token_mix = reduce_go([begun_mix], "mix", [gw_rest, token_win])
    du_mix = mm(dz_na, w_na, name="mix_du_na", tb=True, after=[token_mix])
    token_win = reduce_go([begun_win], "win", [du_mix])
    du_mix = mm(dz_rest, w_rest, name="mix_du_rest", tb=True, res=du_mix, after=[token_win])
    dh1, dh1_half, g_mix = norm_bwd(h1, mix_norm, du_mix, name="mix_dnorm", res=dh2, bf16_alpha=0.5)
    grad_x, g_ffn1 = ffn_bwd(xs, ffn1_norm, ffn1_saved, dh1, dh1_half, "ffn1", "ffn1", last=True)

    small_g = {"ffn1_norm": g_ffn1, "mix_norm": g_mix, "q_a_norm": g_qa, "kv_a_norm": g_kva, "na_rpb": g_rpb,
               "ffn2_norm": g_ffn2, "pl_norm": g_pl, "final_norm": g_final}
    sizes = [int(np.prod(W[n].shape)) for n in small]
    total = sum(sizes)
    padded = -(-total // (8 * LANES)) * (8 * LANES)

    def pack(parts):
        flat = jnp.concatenate([jnp.reshape(parts[n], (-1,)).astype(F32) for n in small]
                               + [jnp.zeros((padded - total,), F32)])
        return flat.reshape(padded // LANES, LANES)

    def unpack(a):
        flat, out, o = a.reshape(-1), {}, 0
        for n, sz in zip(small, sizes):
            out[n] = flat[o:o + sz].reshape(W[n].shape)
            o += sz
        return out

    grads, delta, new_m, new_v = {}, {}, {}, {}
    after = [grad_x]
    for entry in pending:
        after = reduce_finish(entry, after)

    g_small = sum_devices(gather_small(pack(small_g), after))
    d_small, m_small, v_small = adamw(pack(W), g_small, pack(Mo), pack(Vo), name="adamw_small")
    for full, part in ((grads, g_small), (delta, d_small), (new_m, m_small), (new_v, v_small)):
        full.update(unpack(part))

    return (loss, grad_x[None], *[grads[n] for n in order], *[delta[n] for n in order],
            *[new_m[n] for n in order], *[new_v[n] for n in order])
```

```python
import functools

import numpy as np
import jax
import jax.numpy as jnp
from jax import lax
from jax.experimental import pallas as pl
from jax.experimental.pallas import tpu as pltpu

F32 = jnp.float32
BF16 = jnp.bfloat16

VMEM_LIMIT_V7X = 56 * 1024 * 1024
VMEM_BUDGET_V7X = 40 * 1024 * 1024
LANES = 128

GRID_W = 64
NA_WIN_ROWS = 8
NA_WIN_COLS = 16
HEAD_DIM = 128
MLA_NOPE = 128
MLA_ROPE = 64
MLA_QK = MLA_NOPE + MLA_ROPE
ROPE_THETA = 10000.0
NORM_EPS = 1e-6
NEG_INF = -1e30
N_CHIPS = 4

ADAM_LR = 0.001
ADAM_B1 = 0.9
ADAM_B2 = 0.999
ADAM_EPS = 1e-08
ADAM_WD = 0.01
ADAM_STEP = 10

MESH = pl.DeviceIdType.MESH
ANY = pl.BlockSpec(memory_space=pl.ANY)


def _params(sem=None):
    return pltpu.CompilerParams(dimension_semantics=sem, vmem_limit_bytes=VMEM_LIMIT_V7X)


def _pick(n, target, align):
    best = None
    t = align
    while t <= min(n, target):
        if n % t == 0:
            best = t
        t += align
    return n if best is None else best


def mm(a, b, *, name, ta=False, tb=False, out_dtype=F32, res=None, alpha=1.0, b_stack=False, out_stack=False,
       exact=False, after=(), epilogue=None, epi_in=(), epi_out=()):
    K, M = (a.shape if ta else a.shape[::-1])
    nst = kb = nb = None
    if b_stack:
        nst = b.shape[0]
        if tb:
            N, kb = b.shape[1], b.shape[2]
            Kb = nst * kb
        else:
            Kb, nb = b.shape[1], b.shape[2]
            N = nst * nb
    else:
        N, Kb = (b.shape if tb else b.shape[::-1])
    assert K == Kb, (a.shape, b.shape, ta, tb)
    if out_stack:
        assert N % N_CHIPS == 0
    n_unit = N // N_CHIPS if out_stack else (nb if nb is not None else N)
    tn = _pick(n_unit, 512, LANES) if n_unit % 512 == 0 or n_unit <= 512 else _pick(n_unit, 1536, LANES)
    if ta and n_unit == N and 4 * K * N * jnp.dtype(b.dtype).itemsize <= VMEM_BUDGET_V7X:
        tn = N
    m_align = LANES if ta else 16
    tm = _pick(M, 1024, m_align)
    isz = lambda t: jnp.dtype(t.dtype).itemsize
    out_dtypes = list(epi_out) if epilogue is not None else [out_dtype]
    osz = sum(jnp.dtype(t).itemsize for t in out_dtypes) + sum(isz(e) for e in epi_in)

    def vmem(tm_, tn_):
        return (2 * tm_ * K * isz(a) + 2 * K * tn_ * isz(b) + 2 * tm_ * tn_ * osz + tm_ * tn_ * 4
                + (tm_ * K * 2 if ta else 0) + (2 * tm_ * tn_ * isz(res) if res is not None else 0))

    while vmem(tm, tn) > VMEM_BUDGET_V7X and tm % 2 == 0 and (tm // 2) % m_align == 0:
        tm //= 2
    while vmem(tm, tn) > VMEM_BUDGET_V7X and tn % 2 == 0 and (tn // 2) % LANES == 0 and n_unit % (tn // 2) == 0:
        tn //= 2
    assert vmem(tm, tn) <= VMEM_BUDGET_V7X, (name, tm, tn, K)

    a_spec = pl.BlockSpec((K, tm), lambda i, j: (0, i)) if ta else pl.BlockSpec((tm, K), lambda i, j: (i, 0))
    if b_stack and not tb:
        q = nb // tn
        b_spec = pl.BlockSpec((None, K, tn), lambda i, j: (j // q, 0, j % q))
    elif b_stack and tb:
        b_spec = pl.BlockSpec((nst, tn, kb), lambda i, j: (0, j, 0))
    elif tb:
        b_spec = pl.BlockSpec((tn, K), lambda i, j: (j, 0))
    else:
        b_spec = pl.BlockSpec((K, tn), lambda i, j: (0, j))
    if out_stack:
        qo = (N // N_CHIPS) // tn
        o_spec = pl.BlockSpec((None, tm, tn), lambda i, j: (j // qo, i, j % qo))
        o_shapes = [jax.ShapeDtypeStruct((N_CHIPS, M, N // N_CHIPS), t) for t in out_dtypes]
    else:
        o_spec = pl.BlockSpec((tm, tn), lambda i, j: (i, j))
        o_shapes = [jax.ShapeDtypeStruct((M, N), t) for t in out_dtypes]
    has_res = res is not None
    n_in = 2 + has_res + len(epi_in) + len(after)
    nn = (((1,), (0,)), ((), ()))
    nt = (((1,), (1,)), ((), ()))

    def body(*refs):
        a_ref, b_ref = refs[:2]
        r_ref = refs[2] if has_res else None
        e_refs = refs[2 + has_res:2 + has_res + len(epi_in)]
        o_refs = refs[n_in:n_in + len(out_dtypes)]
        if ta:
            at_ref = refs[-1]

            @pl.when(pl.program_id(1) == 0)
            def _():
                at_ref[...] = a_ref[...].astype(BF16).T

            lhs = at_ref[...]
        elif exact:
            lhs = a_ref[...]
        else:
            lhs = a_ref[...].astype(BF16)
        if exact:
            total = lax.dot_general(lhs, b_ref[...], nt if tb else nn, preferred_element_type=F32,
                                    precision=lax.Precision.HIGHEST)
        elif b_stack and tb:
            total = None
            for s in range(nst):
                part = lax.dot_general(lhs[:, s * kb:(s + 1) * kb], b_ref[s].astype(BF16), nt,
                                       preferred_element_type=F32)
                total = part if total is None else total + part
        else:
            total = lax.dot_general(lhs, b_ref[...].astype(BF16), nt if tb else nn, preferred_element_type=F32)
        if alpha != 1.0:
            total = total * alpha
        if has_res:
            total = total + r_ref[...].astype(F32)
        vals = epilogue(total, *[e[...] for e in e_refs]) if epilogue is not None else (total,)
        for o_ref, v in zip(o_refs, vals):
            o_ref[...] = v.astype(o_ref.dtype)

    tile = pl.BlockSpec((tm, tn), lambda i, j: (i, j))
    in_specs = [a_spec, b_spec] + [tile] * (has_res + len(epi_in)) + [ANY] * len(after)
    args = [a, b] + ([res] if has_res else []) + list(epi_in) + list(after)
    outs = pl.pallas_call(
        body, name=name, grid=(M // tm, N // tn), in_specs=in_specs, out_specs=[o_spec] * len(out_dtypes),
        out_shape=o_shapes, scratch_shapes=[pltpu.VMEM((tm, K), BF16)] if ta else [],
        compiler_params=_params(("parallel", "arbitrary")),
    )(*args)
    return outs if epilogue is not None else outs[0]


def rowwise(fn, rows, consts, outs, accs=(), *, tm, name, tn=None, into=None):
    rows = [r if isinstance(r, tuple) else (r, r.shape[1], 0) for r in rows]
    S = rows[0][0].shape[0]
    tm = _pick(S, tm, 16)
    nrow, ncon, nout = len(rows), len(consts), len(outs)
    outs = [o if len(o) == 3 else (o[0], o[1], None) for o in outs]
    if tn is None:
        grid = (S // tm,)
        in_specs = [pl.BlockSpec((tm, w), functools.partial(lambda i, cb: (i, cb), cb=cb)) for _, w, cb in rows]
        in_specs += [pl.BlockSpec(c.shape, lambda i: (0, 0)) for c in consts]
        out_specs = [pl.BlockSpec((tm, n), functools.partial(lambda i, cb: (i, cb), cb=cb or 0)) for n, _, cb in outs]
        out_specs += [pl.BlockSpec(s, lambda i: (0, 0)) for s in accs]
        sem = ("arbitrary",)
    else:
        assert not accs
        N = rows[0][1]
        grid = (S // tm, N // tn)
        in_specs = [pl.BlockSpec((tm, tn), lambda i, j: (i, j)) for _ in rows]
        in_specs += [pl.BlockSpec(c.shape, lambda i, j: (0, 0)) for c in consts]
        out_specs = [pl.BlockSpec((tm, tn), lambda i, j: (i, j)) for _ in outs]
        sem = ("parallel", "parallel")
    out_shape = [jax.ShapeDtypeStruct((S, n if cb is None else into[1]), dt) for n, dt, cb in outs]
    out_shape += [jax.ShapeDtypeStruct(s, F32) for s in accs]
    extra, aliases = [], {}
    if into is not None and into[0] is not None:
        extra = [into[0]]
        aliases = {nrow + ncon: [cb is not None for _, _, cb in outs].index(True)}

    def body(*refs):
        vals = fn(*[r[...] for r in refs[:nrow + ncon]])
        if not isinstance(vals, (tuple, list)):
            vals = (vals,)
        o_refs = refs[nrow + ncon + len(extra):]
        for o_ref, v in zip(o_refs[:nout], vals[:nout]):
            o_ref[...] = v.astype(o_ref.dtype)
        if accs:
            first = pl.program_id(0) == 0

            def accumulate(a_ref, v):
                @pl.when(first)
                def _():
                    a_ref[...] = v

                @pl.when(jnp.logical_not(first))
                def _():
                    a_ref[...] += v

            for a_ref, v in zip(o_refs[nout:], vals[nout:]):
                accumulate(a_ref, v.astype(F32))

    return pl.pallas_call(
        body, name=name, grid=grid, in_specs=in_specs + [ANY] * len(extra), out_specs=out_specs, out_shape=out_shape,
        input_output_aliases=aliases, compiler_params=_params(sem),
    )(*[r[0] for r in rows], *consts, *extra)


def _rstd(x):
    return lax.rsqrt(jnp.mean(x * x, axis=-1, keepdims=True) + NORM_EPS)


def norm_fwd(x, g, *, name, tm=256):
    w = x[1] if isinstance(x, tuple) else x.shape[1]

    def fn(xb, gb):
        return (xb * _rstd(xb)) * gb

    return rowwise(fn, [x], [g], [(w, BF16)], tm=tm, name=name)[0]


def norm_bwd(x, g, dn, *, name, res=None, want_f32=True, bf16_alpha=None, tm=256, into=None, cb=None):
    w = x[1] if isinstance(x, tuple) else x.shape[1]
    has_res = res is not None

    def fn(*blocks):
        if has_res:
            xb, dnb, rb, gb = blocks
        else:
            xb, dnb, gb = blocks
        r = _rstd(xb)
        xh = xb * r
        dxh = dnb * gb
        dx = r * (dxh - xh * jnp.mean(dxh * xh, axis=-1, keepdims=True))
        if has_res:
            dx = dx + rb
        out = []
        if want_f32:
            out.append(dx)
        if bf16_alpha is not None:
            out.append(dx * bf16_alpha if bf16_alpha != 1.0 else dx)
        out.append(jnp.sum(dnb * xh, axis=0, keepdims=True))
        return tuple(out)

    outs = ([(w, F32)] if want_f32 else []) + ([(w, BF16, cb)] if bf16_alpha is not None else [])
    rows = [x, dn] + ([res] if has_res else [])
    return rowwise(fn, rows, [g], outs, accs=[(1, w)], tm=tm, name=name, into=into)


def _sig(x):
    return jax.nn.sigmoid(x)


def swiglu_tile(ub, gb):
    return ub, gb * _sig(gb) * ub


def swiglu_bwd_tile(dab, gb, ub):
    sg = _sig(gb)
    return dab * ub * (sg + gb * sg * (1.0 - sg)), dab * (gb * sg)


def rope(x, cos, sin_signed, *, name, out_dtype, into=None, cb=None, zero_cols=0):
    w = x[1] if isinstance(x, tuple) else x.shape[1]
    half = MLA_ROPE // 2

    def fn(xb, cb, sb):
        lane = lax.broadcasted_iota(jnp.int32, cb.shape, 1)
        outs = []
        for hb in range(w // LANES):
            blk = xb[:, hb * LANES:(hb + 1) * LANES]
            partner = jnp.where(lane < half, pltpu.roll(blk, LANES - half, 1), pltpu.roll(blk, half, 1))
            outs.append(blk * cb + partner * sb)
        if zero_cols:
            outs.append(jnp.zeros((xb.shape[0], zero_cols), xb.dtype))
        return outs[0] if len(outs) == 1 else jnp.concatenate(outs, axis=1)

    return rowwise(fn, [x, cos, sin_signed], [], [(w + zero_cols, out_dtype, cb)], tm=256, name=name, into=into)[0]


def _na_tables():
    cols = np.arange(GRID_W)
    kw = NA_WIN_COLS
    dc = np.clip(cols[None, :] - cols[:, None], -(kw - 1), kw - 1) + (kw - 1)
    onehot = np.zeros((LANES, GRID_W * GRID_W), np.float32)
    onehot[dc.reshape(-1), np.arange(GRID_W * GRID_W)] = 1.0
    col_start = np.clip(cols - kw // 2, 0, GRID_W - kw)
    mask = (cols[None, :] >= col_start[:, None]) & (cols[None, :] < col_start[:, None] + kw)
    return onehot, np.where(mask, 0.0, NEG_INF).astype(np.float32)


def na_bias(rpb, after=()):
    H = rpb.shape[0]
    nr, kh = 2 * NA_WIN_ROWS - 1, NA_WIN_ROWS
    onehot, maskb = _na_tables()
    rp = jnp.pad(rpb.reshape(H * nr, 2 * NA_WIN_COLS - 1), ((0, 0), (0, LANES - (2 * NA_WIN_COLS - 1))))
    t1 = mm(rp, jnp.asarray(onehot), name="na_bias_table", exact=True, after=after).reshape(H, nr, GRID_W, GRID_W)
    t1 = t1 + jnp.asarray(maskb)[None, None]
    per_t = [jnp.stack([t1[:, i - t + kh - 1] for i in range(kh)], axis=2) for t in range(kh)]
    return jnp.stack(per_t, axis=1).reshape(H, kh, GRID_W, kh * GRID_W)


def na_rpb_grad(db):
    H = db.shape[0]
    nr, kh = 2 * NA_WIN_ROWS - 1, NA_WIN_ROWS
    onehot, _ = _na_tables()
    shifted = [jnp.pad(db[:, t], ((0, 0), (0, 0), ((kh - 1 - t) * GRID_W, t * GRID_W))) for t in range(kh)]
    dw1 = functools.reduce(jnp.add, shifted)
    dt1 = dw1.reshape(H, GRID_W, nr, GRID_W).transpose(0, 2, 1, 3).reshape(H * nr, GRID_W * GRID_W)
    g = mm(dt1, jnp.asarray(onehot), name="na_rpb_grad", tb=True, exact=True)
    return g[:, :2 * NA_WIN_COLS - 1].reshape(H, nr, 2 * NA_WIN_COLS - 1)


def _na_first_row(r, rows):
    return jnp.clip(r - NA_WIN_ROWS // 2, 0, rows - NA_WIN_ROWS)


NA_ROWS_PER_STEP = 16


def _na_probs(q, k_ref, b_ref, r, rows):
    first = _na_first_row(r, rows)
    start = pl.multiple_of(first * GRID_W, GRID_W)
    k = k_ref[pl.ds(start, NA_WIN_ROWS * GRID_W), :]
    s = lax.dot_general(q, k, (((1,), (1,)), ((), ())), preferred_element_type=F32)
    s = s * (HEAD_DIM ** -0.5) + b_ref[r - first]
    m = jnp.max(s, axis=-1, keepdims=True)
    e = jnp.exp(s - m)
    return k, e / jnp.sum(e, axis=-1, keepdims=True), start, r - first


def na_fwd(z, bias, H, S):
    rows = S // GRID_W
    nkeys = NA_WIN_ROWS * GRID_W
    rb = _pick(rows, NA_ROWS_PER_STEP, 1)

    def body(q_ref, k_ref, v_ref, b_ref, o_ref):
        for j in range(rb):
            r = pl.program_id(1) * rb + j
            rows_j = pl.ds(j * GRID_W, GRID_W)
            _, p, start, _ = _na_probs(q_ref[rows_j, :], k_ref, b_ref, r, rows)
            v = v_ref[pl.ds(start, nkeys), :]
            o_ref[rows_j, :] = jnp.dot(p.astype(BF16), v, preferred_element_type=F32).astype(o_ref.dtype)

    return pl.pallas_call(
        body, name="na_fwd", grid=(H, rows // rb),
        in_specs=[pl.BlockSpec((rb * GRID_W, HEAD_DIM), lambda h, i: (i, h)),
                  pl.BlockSpec((S, HEAD_DIM), lambda h, i: (0, H + h)),
                  pl.BlockSpec((S, HEAD_DIM), lambda h, i: (0, 2 * H + h)),
                  pl.BlockSpec((None, NA_WIN_ROWS, GRID_W, nkeys), lambda h, i: (h, 0, 0, 0))],
        out_specs=pl.BlockSpec((rb * GRID_W, HEAD_DIM), lambda h, i: (i, h)),
        out_shape=jax.ShapeDtypeStruct((S, H * HEAD_DIM), BF16),
        compiler_params=_params(("parallel", "arbitrary")),
    )(z, z, z, bias)


def na_bwd(z, bias, do, H, S):
    rows = S // GRID_W
    nkeys = NA_WIN_ROWS * GRID_W
    rb = _pick(rows, NA_ROWS_PER_STEP, 1)
    tn_dims = (((0,), (0,)), ((), ()))

    def body(q_ref, k_ref, v_ref, b_ref, do_ref, dq_ref, dk_ref, dv_ref, db_ref, dk_acc, dv_acc):
        i = pl.program_id(1)

        @pl.when(i == 0)
        def _():
            dk_acc[...] = jnp.zeros_like(dk_acc)
            dv_acc[...] = jnp.zeros_like(dv_acc)
            db_ref[...] = jnp.zeros_like(db_ref)

        for j in range(rb):
            rows_j = pl.ds(j * GRID_W, GRID_W)
            q = q_ref[rows_j, :]
            k, p, start, t = _na_probs(q, k_ref, b_ref, i * rb + j, rows)
            keys = pl.ds(start, nkeys)
            dob = do_ref[rows_j, :].astype(BF16)
            dp = lax.dot_general(dob, v_ref[keys, :], (((1,), (1,)), ((), ())), preferred_element_type=F32)
            ds = p * (dp - jnp.sum(dp * p, axis=-1, keepdims=True))
            dsb = (ds * (HEAD_DIM ** -0.5)).astype(BF16)
            dq_ref[rows_j, :] = jnp.dot(dsb, k, preferred_element_type=F32).astype(dq_ref.dtype)
            dk_acc[keys, :] += lax.dot_general(dsb, q, tn_dims, preferred_element_type=F32)
            dv_acc[keys, :] += lax.dot_general(p.astype(BF16), dob, tn_dims, preferred_element_type=F32)
            db_ref[t] += ds

        @pl.when(i == rows // rb - 1)
        def _():
            dk_ref[...] = dk_acc[...].astype(dk_ref.dtype)
            dv_ref[...] = dv_acc[...].astype(dv_ref.dtype)

    W = H * HEAD_DIM
    qspec = pl.BlockSpec((rb * GRID_W, HEAD_DIM), lambda h, i: (i, h))
    bspec = pl.BlockSpec((None, NA_WIN_ROWS, GRID_W, nkeys), lambda h, i: (h, 0, 0, 0))
    return pl.pallas_call(
        body, name="na_bwd", grid=(H, rows // rb),
        in_specs=[qspec, pl.BlockSpec((S, HEAD_DIM), lambda h, i: (0, H + h)),
                  pl.BlockSpec((S, HEAD_DIM), lambda h, i: (0, 2 * H + h)), bspec, qspec],
        out_specs=[qspec, pl.BlockSpec((S, HEAD_DIM), lambda h, i: (0, h)),
                   pl.BlockSpec((S, HEAD_DIM), lambda h, i: (0, h)), bspec],
        out_shape=[jax.ShapeDtypeStruct((S, W), BF16)] * 3 + [jax.ShapeDtypeStruct((H, NA_WIN_ROWS, GRID_W, nkeys), F32)],
        scratch_shapes=[pltpu.VMEM((S, HEAD_DIM), F32)] * 2,
        compiler_params=_params(("arbitrary", "arbitrary")),
    )(z, z, z, bias, do)


def _mla_keys(kn_ref, kr_ref, kcat):
    @pl.when(pl.program_id(1) == 0)
    def _():
        kcat[:, :HEAD_DIM] = kn_ref[...]
        kcat[:, HEAD_DIM:] = kr_ref[...]


MLA_LOG2_SCALE = (MLA_QK ** -0.5) * 1.4426950408889634


def _mla_scores(qn_ref, qr_ref, kcat):
    qcat = jnp.concatenate([qn_ref[...], qr_ref[...]], axis=1)
    return qcat, lax.dot_general(qcat, kcat[...], (((1,), (1,)), ((), ())), preferred_element_type=F32)


def mla_fwd(qn, qr, kn, v, kr, H, S):
    tq = _pick(S, 256, 16)

    def body(qn_ref, qr_ref, kn_ref, v_ref, kr_ref, o_ref, lse_ref, kcat):
        _mla_keys(kn_ref, kr_ref, kcat)
        _, s = _mla_scores(qn_ref, qr_ref, kcat)
        m = jnp.max(s, axis=-1, keepdims=True)
        e = jnp.exp2((s - m) * MLA_LOG2_SCALE)
        l = jnp.sum(e, axis=-1, keepdims=True)
        o = jnp.dot(e.astype(BF16), v_ref[...], preferred_element_type=F32)
        o_ref[...] = (o / l).astype(o_ref.dtype)
        lse_ref[...] = jnp.broadcast_to(m * MLA_LOG2_SCALE + jnp.log2(l), lse_ref.shape)

    qspec = pl.BlockSpec((tq, HEAD_DIM), lambda h, i: (i, h))
    kspec = pl.BlockSpec((S, HEAD_DIM), lambda h, i: (0, h))
    return pl.pallas_call(
        body, name="mla_fwd", grid=(H, S // tq),
        in_specs=[qspec, qspec, kspec, kspec, pl.BlockSpec((S, LANES), lambda h, i: (0, 0))],
        out_specs=[qspec, qspec],
        out_shape=[jax.ShapeDtypeStruct((S, H * HEAD_DIM), BF16), jax.ShapeDtypeStruct((S, H * LANES), F32)],
        scratch_shapes=[pltpu.VMEM((S, 2 * HEAD_DIM), BF16)],
        compiler_params=_params(("parallel", "arbitrary")),
    )(qn, qr, kn, v, kr)


def mla_bwd(qn, qr, kn, v, kr, lse, do, H, S):
    tq = _pick(S, 512, 16)
    nt = (((1,), (1,)), ((), ()))
    tn_dims = (((0,), (0,)), ((), ()))

    def body(qn_ref, qr_ref, kn_ref, v_ref, kr_ref, lse_ref, do_ref, dqn_ref, dqr_ref, dkn_ref, dv_ref, dkr_ref, kcat):
        h, i = pl.program_id(0), pl.program_id(1)
        _mla_keys(kn_ref, kr_ref, kcat)
        qcat, s = _mla_scores(qn_ref, qr_ref, kcat)
        p = jnp.exp2(s * MLA_LOG2_SCALE - lse_ref[:, 0:1])
        dob = do_ref[...].astype(BF16)
        dp = lax.dot_general(dob, v_ref[...], nt, preferred_element_type=F32)
        ds = p * (dp - jnp.sum(dp * p, axis=-1, keepdims=True))
        dsb = (ds * (MLA_QK ** -0.5)).astype(BF16)
        dq = jnp.dot(dsb, kcat[...], preferred_element_type=F32)
        dqn_ref[...] = dq[:, :HEAD_DIM].astype(dqn_ref.dtype)
        dqr_ref[...] = dq[:, HEAD_DIM:].astype(dqr_ref.dtype)

        @pl.when(i == 0)
        def _():
            dkn_ref[...] = jnp.zeros_like(dkn_ref)
            dv_ref[...] = jnp.zeros_like(dv_ref)

        @pl.when(jnp.logical_and(i == 0, h == 0))
        def _():
            dkr_ref[...] = jnp.zeros_like(dkr_ref)

        dk = lax.dot_general(dsb, qcat, tn_dims, preferred_element_type=F32)
        dkn_ref[...] += dk[:, :HEAD_DIM]
        dkr_ref[...] += dk[:, HEAD_DIM:]
        dv_ref[...] += lax.dot_general(p.astype(BF16), dob, tn_dims, preferred_element_type=F32)

    qspec = pl.BlockSpec((tq, HEAD_DIM), lambda h, i: (i, h))
    kspec = pl.BlockSpec((S, HEAD_DIM), lambda h, i: (0, h))
    rspec = pl.BlockSpec((S, LANES), lambda h, i: (0, 0))
    W = H * HEAD_DIM
    return pl.pallas_call(
        body, name="mla_bwd", grid=(H, S // tq),
        in_specs=[qspec, qspec, kspec, kspec, rspec, qspec, qspec],
        out_specs=[qspec, qspec, kspec, kspec, rspec],
        out_shape=[jax.ShapeDtypeStruct((S, W), BF16), jax.ShapeDtypeStruct((S, W), F32),
                   jax.ShapeDtypeStruct((S, W), F32), jax.ShapeDtypeStruct((S, W), F32),
                   jax.ShapeDtypeStruct((S, LANES), F32)],
        scratch_shapes=[pltpu.VMEM((S, 2 * HEAD_DIM), BF16)],
        compiler_params=_params(("arbitrary", "arbitrary")),
    )(qn, qr, kn, v, kr, lse, do)


def _place():
    return lax.axis_index("x"), lax.axis_index("y"), lax.axis_index("c")


def _other_chips(x, y):
    return [(1 - x, y), (x, 1 - y), (1 - x, 1 - y)]


def _remote(src, dst, send_sem, recv_sem, to):
    return pltpu.make_async_remote_copy(src_ref=src, dst_ref=dst, send_sem=send_sem, recv_sem=recv_sem,
                                        device_id=to, device_id_type=MESH)


HBM = pl.BlockSpec(memory_space=pltpu.HBM)
SEM = pl.BlockSpec(memory_space=pltpu.SEMAPHORE)
EFFECT = pltpu.SideEffectType.DATAFLOW_SIDE_EFFECTING


def _in_hbm(a):
    return pltpu.with_memory_space_constraint(a, pltpu.HBM)


TOKEN = jax.ShapeDtypeStruct((8, LANES), F32)
IN_VMEM = pl.BlockSpec(memory_space=pltpu.VMEM)


def cast_own_block(w, me_idx, after, name):
    _, k, n = w.shape
    rows = k // 2
    tm = _row_tile(rows, n, 4 + 2)
    nb = rows // tm

    def body(me_ref, w_ref, *rest):
        rest[len(after)][...] = w_ref[...].astype(BF16)

    gs = pltpu.PrefetchScalarGridSpec(
        num_scalar_prefetch=1, grid=(2, nb),
        in_specs=[pl.BlockSpec((None, tm, n), lambda h, i, me_ref: (0, h * nb + i, 0))] + [ANY] * len(after),
        out_specs=pl.BlockSpec((None, None, tm, n), lambda h, i, me_ref: (me_ref[0], h, i, 0)))
    return pl.pallas_call(body, name=name, grid_spec=gs, out_shape=jax.ShapeDtypeStruct((N_CHIPS, 2, rows, n), BF16),
                          compiler_params=_params(("arbitrary", "arbitrary")))(me_idx, w, *after)


def gather_start(landings, after, tag):
    n = len(landings)

    def body(*refs):
        lands = refs[:n]
        send, recv = refs[n + len(after)], refs[n + len(after) + 1]
        token = refs[-1]
        x, y, c = _place()
        me = 2 * x + y
        for w in range(n):
            for k, (px, py) in enumerate(_other_chips(x, y)):
                blk = lands[w].at[me, c]
                _remote(blk, blk, send.at[3 * w + k], recv.at[3 * w + k], (px, py, c)).start()
        token[...] = jnp.zeros_like(token)

    outs = pl.pallas_call(
        body, name="gather_start_" + tag,
        out_shape=(pltpu.SemaphoreType.DMA((3 * n,)),) * 2 + tuple(pltpu.HBM(b.shape, b.dtype) for b in landings)
        + (TOKEN,),
        in_specs=[HBM] * n + [ANY] * len(after), out_specs=tuple([SEM, SEM] + [HBM] * n + [IN_VMEM]),
        input_output_aliases={i: 2 + i for i in range(n)},
        compiler_params=pltpu.CompilerParams(has_side_effects=EFFECT),
    )(*[_in_hbm(b) for b in landings], *after)
    return (outs[0], outs[1]), outs[2:2 + n], outs[-1]


def gather_wait(sems, landings, after, tag):
    n = len(landings)
    send, recv = sems

    def body(*refs):
        lands = refs[:n]
        send_sem, recv_sem = refs[n], refs[n + 1]
        x, y, c = _place()
        me = 2 * x + y
        for w in range(n):
            for k, (px, py) in enumerate(_other_chips(x, y)):
                cp = _remote(lands[w].at[me, c], lands[w].at[2 * px + py, c], send_sem.at[3 * w + k],
                             recv_sem.at[3 * w + k], (px, py, c))
                cp.wait_send()
                cp.wait_recv()

    return pl.pallas_call(
        body, name="gather_wait_" + tag, out_shape=tuple(pltpu.HBM(b.shape, b.dtype) for b in landings),
        in_specs=[HBM] * n + [SEM, SEM] + [ANY] * len(after), out_specs=tuple([HBM] * n),
        input_output_aliases={i: i for i in range(n)},
        compiler_params=pltpu.CompilerParams(has_side_effects=EFFECT),
    )(*landings, send, recv, *after)


def gather_forward(landings, tag):
    n = len(landings)

    def body(*refs):
        ins, outs = refs[:n], refs[n:2 * n]
        send, recv = refs[2 * n:]
        x, y, c = _place()
        sibling = (x, y, 1 - c)
        cps = []
        for w in range(n):
            for k, (px, py) in enumerate(_other_chips(x, y)):
                j = 2 * px + py
                cp = _remote(ins[w].at[j, c], outs[w].at[j, c], send.at[3 * w + k], recv.at[3 * w + k], sibling)
                cp.start()
                cps.append(cp)
        for w in range(n):
            for k, (px, py) in enumerate(_other_chips(x, y)):
                blk = outs[w].at[2 * px + py, 1 - c]
                _remote(blk, blk, send.at[3 * w + k], recv.at[3 * w + k], sibling).wait_recv()
        for cp in cps:
            cp.wait_send()

    return pl.pallas_call(
        body, name="gather_forward_" + tag, in_specs=[ANY] * n, out_specs=[ANY] * n,
        out_shape=[jax.ShapeDtypeStruct(a.shape, a.dtype) for a in landings],
        input_output_aliases={i: i for i in range(n)},
        scratch_shapes=[pltpu.SemaphoreType.DMA((3 * n,)), pltpu.SemaphoreType.DMA((3 * n,))],
    )(*landings)


def forward_start(landings, tag):
    n = len(landings)

    def body(*refs):
        lands = refs[:n]
        send, recv = refs[n], refs[n + 1]
        token = refs[-1]
        x, y, c = _place()
        for w in range(n):
            for k, (px, py) in enumerate(_other_chips(x, y)):
                blk = lands[w].at[2 * px + py, c]
                _remote(blk, blk, send.at[3 * w + k], recv.at[3 * w + k], (x, y, 1 - c)).start()
        token[...] = jnp.zeros_like(token)

    outs = pl.pallas_call(
        body, name="forward_start_" + tag,
        out_shape=(pltpu.SemaphoreType.DMA((3 * n,)),) * 2 + tuple(pltpu.HBM(b.shape, b.dtype) for b in landings)
        + (TOKEN,),
        in_specs=[HBM] * n, out_specs=tuple([SEM, SEM] + [HBM] * n + [IN_VMEM]),
        input_output_aliases={i: 2 + i for i in range(n)},
        compiler_params=pltpu.CompilerParams(has_side_effects=EFFECT),
    )(*[_in_hbm(b) for b in landings])
    return (outs[0], outs[1]), outs[2:2 + n], outs[-1]


def forward_wait(sems, landings, after, tag):
    n = len(landings)

    def body(*refs):
        lands = refs[:n]
        send, recv = refs[n], refs[n + 1]
        x, y, c = _place()
        for w in range(n):
            for k, (px, py) in enumerate(_other_chips(x, y)):
                j = 2 * px + py
                cp = _remote(lands[w].at[j, c], lands[w].at[j, 1 - c], send.at[3 * w + k], recv.at[3 * w + k],
                             (x, y, 1 - c))
                cp.wait_send()
                cp.wait_recv()

    return pl.pallas_call(
        body, name="forward_wait_" + tag, out_shape=tuple(pltpu.HBM(b.shape, b.dtype) for b in landings),
        in_specs=[HBM] * n + [SEM, SEM] + [ANY] * len(after), out_specs=tuple([HBM] * n),
        input_output_aliases={i: i for i in range(n)},
        compiler_params=pltpu.CompilerParams(has_side_effects=EFFECT),
    )(*landings, sems[0], sems[1], *after)


def pair_start(grads, landings, tag):
    n = len(grads)

    def body(*refs):
        ins, lands = refs[:n], refs[n:2 * n]
        send, recv = refs[2 * n], refs[2 * n + 1]
        token = refs[-1]
        x, y, c = _place()
        for w in range(n):
            _remote(ins[w].at[:, 1 - c], lands[w], send.at[w], recv.at[w], (x, y, 1 - c)).start()
        token[...] = jnp.zeros_like(token)

    bufs = list(grads) + list(landings)
    outs = pl.pallas_call(
        body, name="pair_start_" + tag,
        out_shape=(pltpu.SemaphoreType.DMA((n,)),) * 2 + tuple(pltpu.HBM(b.shape, b.dtype) for b in bufs) + (TOKEN,),
        in_specs=[HBM] * (2 * n), out_specs=tuple([SEM, SEM] + [HBM] * (2 * n) + [IN_VMEM]),
        input_output_aliases={i: 2 + i for i in range(2 * n)},
        compiler_params=pltpu.CompilerParams(has_side_effects=EFFECT),
    )(*[_in_hbm(b) for b in bufs])
    return (outs[0], outs[1]), outs[2:2 + n], outs[2 + n:2 + 2 * n], outs[-1]


def pair_wait(sems, grads, landings, after, tag):
    n = len(grads)

    def body(*refs):
        ins, lands = refs[:n], refs[n:2 * n]
        send, recv = refs[2 * n], refs[2 * n + 1]
        x, y, c = _place()
        for w in range(n):
            cp = _remote(ins[w].at[:, 1 - c], lands[w], send.at[w], recv.at[w], (x, y, 1 - c))
            cp.wait_send()
            cp.wait_recv()

    bufs = list(grads) + list(landings)
    outs = pl.pallas_call(
        body, name="pair_wait_" + tag, out_shape=tuple(pltpu.HBM(b.shape, b.dtype) for b in bufs),
        in_specs=[HBM] * (2 * n) + [SEM, SEM] + [ANY] * len(after), out_specs=tuple([HBM] * (2 * n)),
        input_output_aliases={i: i for i in range(2 * n)},
        compiler_params=pltpu.CompilerParams(has_side_effects=EFFECT),
    )(*bufs, sems[0], sems[1], *after)
    return outs[:n], outs[n:]


def scatter_start(sums, landings, tag):
    n = len(sums)

    def body(*refs):
        ins, lands = refs[:n], refs[n:2 * n]
        send, recv = refs[2 * n], refs[2 * n + 1]
        token = refs[-1]
        x, y, c = _place()
        for w in range(n):
            for k, (px, py) in enumerate(_other_chips(x, y)):
                _remote(ins[w].at[2 * px + py], lands[w].at[k], send.at[3 * w + k], recv.at[3 * w + k], (px, py, c)).start()
        token[...] = jnp.zeros_like(token)

    bufs = list(sums) + list(landings)
    outs = pl.pallas_call(
        body, name="scatter_start_" + tag,
        out_shape=(pltpu.SemaphoreType.DMA((3 * n,)),) * 2 + tuple(pltpu.HBM(b.shape, b.dtype) for b in bufs) + (TOKEN,),
        in_specs=[HBM] * (2 * n), out_specs=tuple([SEM, SEM] + [HBM] * (2 * n) + [IN_VMEM]),
        input_output_aliases={i: 2 + i for i in range(2 * n)},
        compiler_params=pltpu.CompilerParams(has_side_effects=EFFECT),
    )(*[_in_hbm(b) for b in bufs])
    return (outs[0], outs[1]), outs[2:2 + n], outs[2 + n:2 + 2 * n], outs[-1]


def scatter_wait(sems, sums, landings, after, tag):
    n = len(sums)

    def body(*refs):
        ins, lands = refs[:n], refs[n:2 * n]
        send, recv = refs[2 * n], refs[2 * n + 1]
        x, y, c = _place()
        for w in range(n):
            for k, (px, py) in enumerate(_other_chips(x, y)):
                cp = _remote(ins[w].at[2 * px + py], lands[w].at[k], send.at[3 * w + k], recv.at[3 * w + k], (px, py, c))
                cp.wait_send()
                cp.wait_recv()

    bufs = list(sums) + list(landings)
    outs = pl.pallas_call(
        body, name="scatter_wait_" + tag, out_shape=tuple(pltpu.HBM(b.shape, b.dtype) for b in bufs),
        in_specs=[HBM] * (2 * n) + [SEM, SEM] + [ANY] * len(after), out_specs=tuple([HBM] * (2 * n)),
        input_output_aliases={i: i for i in range(2 * n)},
        compiler_params=pltpu.CompilerParams(has_side_effects=EFFECT),
    )(*bufs, sems[0], sems[1], *after)
    return outs[:n], outs[n:]


def half_exchange(halves, tag):
    n = len(halves)

    def body(*refs):
        ins, outs = refs[:n], refs[n:2 * n]
        send, recv = refs[2 * n:]
        x, y, c = _place()
        cps = []
        for w in range(n):
            cp = _remote(ins[w], outs[w], send.at[w], recv.at[w], (x, y, 1 - c))
            cp.start()
            cps.append(cp)
        for cp in cps:
            cp.wait()

    return pl.pallas_call(
        body, name="grad_half_exchange_" + tag, in_specs=[ANY] * n, out_specs=[ANY] * n,
        out_shape=[jax.ShapeDtypeStruct(h.shape, h.dtype) for h in halves],
        scratch_shapes=[pltpu.SemaphoreType.DMA((n,)), pltpu.SemaphoreType.DMA((n,))],
    )(*halves)


def gather_small(v, after=()):
    def body(*refs):
        v_ref = refs[0]
        o_ref, send, recv, local = refs[1 + len(after):]
        x, y, c = _place()
        me = 4 * x + 2 * y + c
        own = pltpu.make_async_copy(v_ref, o_ref.at[me], local)
        own.start()
        cps = []
        for k in range(1, 8):
            fx, fy, fc = (k >> 2) & 1, (k >> 1) & 1, k & 1
            to = (x ^ fx if fx else x, y ^ fy if fy else y, c ^ fc if fc else c)
            cp = _remote(v_ref, o_ref.at[me], send.at[k - 1], recv.at[k - 1], to)
            cp.start()
            cps.append(cp)
        for k in range(1, 8):
            fx, fy, fc = (k >> 2) & 1, (k >> 1) & 1, k & 1
            px, py, pc = (x ^ fx if fx else x, y ^ fy if fy else y, c ^ fc if fc else c)
            cps[k - 1].wait_send()
            _remote(v_ref, o_ref.at[4 * px + 2 * py + pc], send.at[k - 1], recv.at[k - 1], (px, py, pc)).wait_recv()
        own.wait()

    return pl.pallas_call(
        body, name="gather_small_grads", in_specs=[ANY] * (1 + len(after)), out_specs=ANY,
        out_shape=jax.ShapeDtypeStruct((8,) + v.shape, v.dtype),
        scratch_shapes=[pltpu.SemaphoreType.DMA((7,)), pltpu.SemaphoreType.DMA((7,)), pltpu.SemaphoreType.DMA],
    )(v, *after)


def _row_tile(rows, cols, nbuf_bytes):
    tm = _pick(rows, 512, 16)
    while tm * cols * nbuf_bytes * 2 > VMEM_BUDGET_V7X and tm % 32 == 0:
        tm //= 2
    return tm


def pair_sum(g, r, c_idx, tag):
    _, _, rows, cols = g.shape
    tm = _row_tile(rows, cols, 2 + 2 + 2)
    nb = rows // tm

    def body(c_ref, g_ref, r_ref, o_ref):
        o_ref[...] = (g_ref[...].astype(F32) + r_ref[...].astype(F32)).astype(o_ref.dtype)

    gs = pltpu.PrefetchScalarGridSpec(
        num_scalar_prefetch=1, grid=(N_CHIPS, nb),
        in_specs=[pl.BlockSpec((None, None, tm, cols), lambda j, i, c_ref: (j, c_ref[0], i, 0)),
                  pl.BlockSpec((None, tm, cols), lambda j, i, c_ref: (j, i, 0))],
        out_specs=pl.BlockSpec((None, tm, cols), lambda j, i, c_ref: (j, i, 0)))
    return pl.pallas_call(body, name="grad_pair_sum_" + tag, grid_spec=gs,
                          out_shape=jax.ShapeDtypeStruct(r.shape, BF16),
                          compiler_params=_params(("arbitrary", "arbitrary")))(c_idx, g, r)


def chip_sum(s, r, j_idx, tag):
    _, rows, cols = s.shape
    tm = _row_tile(rows, cols, 2 + 3 * 2 + 4)
    nb = rows // tm

    def body(j_ref, s_ref, r_ref, o_ref):
        t = s_ref[...].astype(F32)
        for k in range(3):
            t = t + r_ref[k].astype(F32)
        o_ref[...] = t

    gs = pltpu.PrefetchScalarGridSpec(
        num_scalar_prefetch=1, grid=(nb,),
        in_specs=[pl.BlockSpec((None, tm, cols), lambda i, j_ref: (j_ref[0], i, 0)),
                  pl.BlockSpec((3, tm, cols), lambda i, j_ref: (0, i, 0))],
        out_specs=pl.BlockSpec((tm, cols), lambda i, j_ref: (i, 0)))
    return pl.pallas_call(body, name="grad_chip_sum_" + tag, grid_spec=gs,
                          out_shape=jax.ShapeDtypeStruct((rows, cols), F32),
                          compiler_params=_params(("arbitrary",)))(j_idx, s, r)


def adamw(w, g, m, v, *, name):
    rows, cols = w.shape
    tm = _row_tile(rows, cols, 7 * 4)

    return rowwise(_adamw_math, [w, g, m, v], [], [(cols, F32)] * 3, tm=tm, name=name)


def _adamw_math(wb, gb, mb, vb):
    m2 = ADAM_B1 * mb + (1.0 - ADAM_B1) * gb
    v2 = ADAM_B2 * vb + (1.0 - ADAM_B2) * (gb * gb)
    m_hat = m2 / (1.0 - ADAM_B1 ** ADAM_STEP)
    v_hat = v2 / (1.0 - ADAM_B2 ** ADAM_STEP)
    delta = -ADAM_LR * (m_hat / (jnp.sqrt(v_hat) + ADAM_EPS) + ADAM_WD * wb)
    return delta, m2, v2


def adamw_shard(w, g_own, g_sib, m, v, c_idx, *, name):
    rows, cols = g_own.shape
    tm = _row_tile(rows, cols, 9 * 4)
    nb = rows // tm

    def body(c_ref, w_ref, go_ref, gs_ref, m_ref, v_ref, g_out, d_out, m_out, v_out):
        gb = jnp.where(pl.program_id(0) == c_ref[0], go_ref[...], gs_ref[...])
        delta, m2, v2 = _adamw_math(w_ref[...], gb, m_ref[...], v_ref[...])
        g_out[...] = gb
        d_out[...] = delta
        m_out[...] = m2
        v_out[...] = v2

    full = pl.BlockSpec((tm, cols), lambda h, i, c_ref: (h * nb + i, 0))
    own = pl.BlockSpec((tm, cols), lambda h, i, c_ref: (jnp.where(h == c_ref[0], i, 0), 0))
    sib = pl.BlockSpec((tm, cols), lambda h, i, c_ref: (jnp.where(h == c_ref[0], 0, i), 0))
    gs = pltpu.PrefetchScalarGridSpec(num_scalar_prefetch=1, grid=(2, nb), in_specs=[full, own, sib, full, full],
                                      out_specs=[full] * 4)
    return pl.pallas_call(body, name=name, grid_spec=gs, out_shape=[jax.ShapeDtypeStruct(w.shape, F32)] * 4,
                          compiler_params=_params(("arbitrary", "arbitrary")))(c_idx, w, g_own, g_sib, m, v)


def sum_devices(a):
    def body(a_ref, o_ref):
        t = a_ref[0]
        for k in range(1, 8):
            t = t + a_ref[k]
        o_ref[...] = t

    return pl.pallas_call(body, name="sum_small_grads", out_shape=jax.ShapeDtypeStruct(a.shape[1:], a.dtype))(a)


def kernel(x, p, ffn1_norm, ffn1_w_gate, ffn1_w_up, ffn1_w_down, mix_norm, w_in, q_a_norm, w_uq, kv_a_norm, w_ukv, na_rpb, w_branch_a, w_branch_b, w_out, ffn2_norm, ffn2_w_gate, ffn2_w_up, ffn2_w_down, pl_norm, w_pl, w_pl_gate, final_norm, loss_target, m_ffn1_norm, m_ffn1_w_gate, m_ffn1_w_up, m_ffn1_w_down, m_mix_norm, m_w_in, m_q_a_norm, m_w_uq, m_kv_a_norm, m_w_ukv, m_na_rpb, m_w_branch_a, m_w_branch_b, m_w_out, m_ffn2_norm, m_ffn2_w_gate, m_ffn2_w_up, m_ffn2_w_down, m_pl_norm, m_w_pl, m_w_pl_gate, m_final_norm, v_ffn1_norm, v_ffn1_w_gate, v_ffn1_w_up, v_ffn1_w_down, v_mix_norm, v_w_in, v_q_a_norm, v_w_uq, v_kv_a_norm, v_w_ukv, v_na_rpb, v_w_branch_a, v_w_branch_b, v_w_out, v_ffn2_norm, v_ffn2_w_gate, v_ffn2_w_up, v_ffn2_w_down, v_pl_norm, v_w_pl, v_w_pl_gate, v_final_norm):
    big = ["ffn1_w_gate", "ffn1_w_up", "ffn1_w_down", "w_in", "w_uq", "w_ukv", "w_branch_a", "w_branch_b", "w_out",
           "ffn2_w_gate", "ffn2_w_up", "ffn2_w_down", "w_pl", "w_pl_gate"]
    col_sharded = {"ffn1_w_gate", "ffn1_w_up", "w_in", "w_uq", "w_ukv", "w_branch_a", "w_branch_b", "ffn2_w_gate",
                   "ffn2_w_up", "w_pl"}
    small = ["ffn1_norm", "mix_norm", "q_a_norm", "kv_a_norm", "na_rpb", "ffn2_norm", "pl_norm", "final_norm"]
    order = ["ffn1_norm", "ffn1_w_gate", "ffn1_w_up", "ffn1_w_down", "mix_norm", "w_in", "q_a_norm", "w_uq",
             "kv_a_norm", "w_ukv", "na_rpb", "w_branch_a", "w_branch_b", "w_out", "ffn2_norm", "ffn2_w_gate",
             "ffn2_w_up", "ffn2_w_down", "pl_norm", "w_pl", "w_pl_gate", "final_norm"]
    env = dict(locals())
    W = {n: env[n] for n in order}
    Mo = {n: env["m_" + n] for n in order}
    Vo = {n: env["v_" + n] for n in order}

    xs = x[0]
    S, D = xs.shape
    tgt = loss_target[0]
    ps = p[0, 0]
    NAW = w_branch_a.shape[1]
    MLAW = w_branch_b.shape[1]
    NH, MH = NAW // HEAD_DIM, MLAW // HEAD_DIM
    QR, KVR = w_uq.shape[1], w_ukv.shape[1]
    F = ffn1_w_down.shape[1] * N_CHIPS
    cx, cy, cc = _place()
    c_idx = jnp.reshape(cc, (1,)).astype(jnp.int32)
    j_idx = jnp.reshape(2 * cx + cy, (1,)).astype(jnp.int32)

    groups = [["ffn1_w_gate"], ["ffn1_w_up"], ["ffn1_w_down"], ["w_in"],
              ["w_uq", "w_ukv", "w_branch_a", "w_branch_b", "w_out"],
              ["ffn2_w_gate", "ffn2_w_up", "ffn2_w_down"], ["w_pl", "w_pl_gate"]]
    started, tokens = [], []
    for g, members in enumerate(groups):
        landings = [cast_own_block(W[n], j_idx, tokens[-1:], "cast_" + n) for n in members]
        sems, landings, token = gather_start(landings, tokens[-1:], str(g))
        started.append((sems, landings))
        tokens.append(token)
    gathered = {}

    prefetched = {}

    def prefetch(g, after):
        sems, landings = started[g]
        fsems, landed, token = forward_start(gather_wait(sems, landings, list(after), str(g)), str(g))
        prefetched[g] = (fsems, landed)
        return token

    def arrive(n, after):
        g = [n in members for members in groups].index(True)
        after = list(after) if isinstance(after, (list, tuple)) else [after]
        if g in prefetched:
            fsems, landed = prefetched[g]
            gathered.update(zip(groups[g], forward_wait(fsems, landed, after, str(g))))
            return
        sems, landings = started[g]
        landed = gather_wait(sems, landings, after, str(g))
        gathered.update(zip(groups[g], gather_forward(landed, str(g))))

    def stacked(n, after=None):
        if n not in gathered:
            arrive(n, after)
        g = gathered[n]
        return g.reshape(N_CHIPS, 2 * g.shape[2], g.shape[3])

    def plain(n, after=None):
        if n in col_sharded:
            st = stacked(n, after)
            return st.transpose(1, 0, 2).reshape(st.shape[1], N_CHIPS * st.shape[2])
        if n not in gathered:
            arrive(n, after)
        g = gathered[n]
        return g.reshape(N_CHIPS * 2 * g.shape[2], g.shape[3])

    n_na = 3 * NAW
    n_front = n_na + QR + KVR
    n_in = n_front + MLA_ROPE + 2 * D
    off_ql, off_kvl, off_kr = 2 * D, 2 * D + QR, 2 * D + QR + KVR
    kr_w = 2 * LANES
    rest_w = off_kr + kr_w
    rest_ranges = [(n_front + MLA_ROPE, n_in), (n_na, n_front), (n_front, n_front + MLA_ROPE)]

    def shard_cols(st, lo, hi):
        nb, parts = st.shape[2], []
        while lo < hi:
            j = lo // nb
            end = min(hi, (j + 1) * nb)
            parts.append(st[j][:, lo - j * nb:end - j * nb])
            lo = end
        return parts

    def w_in_shards(g_na, g_rest):
        pieces = [(0, n_na, g_na, 0)]
        o = 0
        for lo, hi in rest_ranges:
            pieces.append((lo, hi, g_rest, o))
            o += hi - lo
        nb, shards = n_in // N_CHIPS, []
        for j in range(N_CHIPS):
            parts = []
            for lo, hi, src, o in sorted(pieces):
                a, b = max(lo, j * nb), min(hi, (j + 1) * nb)
                if a < b:
                    parts.append(src[:, o + a - lo:o + b - lo])
            shards.append(jnp.concatenate(parts, axis=1))
        return jnp.stack(shards)

    pos = jnp.arange(S, dtype=F32)
    inv_freq = 1.0 / (ROPE_THETA ** (jnp.arange(0, MLA_ROPE, 2, dtype=F32) / MLA_ROPE))
    ang = pos[:, None] * inv_freq[None, :]
    zpad = jnp.zeros((S, LANES - MLA_ROPE), F32)
    cos_t = jnp.concatenate([jnp.cos(ang), jnp.cos(ang), zpad], axis=1)
    sin_t = jnp.concatenate([-jnp.sin(ang), jnp.sin(ang), zpad], axis=1)

    def ffn_fwd(h, norm_g, tag, pre, after=()):
        n = norm_fwd(h, norm_g, name=f"{tag}_norm")
        g = mm(n, stacked(pre + "_w_gate", [n, *after]), name=f"{tag}_gate", b_stack=True)
        u, a = mm(n, stacked(pre + "_w_up", g), name=f"{tag}_up", b_stack=True, epilogue=swiglu_tile, epi_in=[g],
                  epi_out=[F32, BF16])
        h_out = mm(a, plain(pre + "_w_down", a), name=f"{tag}_down", res=h, alpha=0.5)
        return h_out, (n, g, u, a)

    def ffn_bwd(h, norm_g, saved, dh, dh_half, tag, pre, last, after=()):
        n, g, u, a = saved
        G[pre + "_w_down"] = mm(a, dh_half, name=f"{tag}_dw_down", ta=True, out_dtype=BF16, after=after)
        begun_d, token_d = pair_begin([pre + "_w_down"], tag + "_d")
        dg, du = mm(dh_half, plain(pre + "_w_down"), name=f"{tag}_da", tb=True, after=[token_d],
                    epilogue=swiglu_bwd_tile, epi_in=[g, u], epi_out=[BF16, BF16])
        G[pre + "_w_gate"] = mm(n, dg, name=f"{tag}_dw_gate", ta=True, out_dtype=BF16, out_stack=True)
        G[pre + "_w_up"] = mm(n, du, name=f"{tag}_dw_up", ta=True, out_dtype=BF16, out_stack=True)
        begun_gu, token_gu = pair_begin([pre + "_w_gate", pre + "_w_up"], tag + "_gu")
        dn = mm(dg, stacked(pre + "_w_gate"), name=f"{tag}_dn_gate", tb=True, b_stack=True, after=[token_gu])
        token = reduce_go([begun_d, begun_gu], tag, [dn])
        dn = mm(du, stacked(pre + "_w_up"), name=f"{tag}_dn_up", tb=True, b_stack=True, res=dn, after=[token])
        return norm_bwd(h, norm_g, dn, name=f"{tag}_dnorm", res=dh, bf16_alpha=None if last else 1.0)

    bias = na_bias(na_rpb[0], after=tokens[-1:])
    h1, ffn1_saved = ffn_fwd(xs, ffn1_norm, "ffn1", "ffn1", after=[bias, tokens[-1]])
    u_mix = norm_fwd(h1, mix_norm, name="mix_norm")
    win_st = stacked("w_in", u_mix)
    w_na = jnp.concatenate(shard_cols(win_st, 0, n_na), axis=1)
    w_rest = jnp.concatenate([p_ for lo, hi in rest_ranges for p_ in shard_cols(win_st, lo, hi)]
                             + [jnp.zeros((D, kr_w - MLA_ROPE), BF16)], axis=1)
    z = mm(u_mix, w_rest, name="mix_in_rest")
    token = prefetch(4, [z])
    z_na = mm(u_mix, w_na, name="mix_in_na", out_dtype=BF16, after=[token])
    o_a = na_fwd(z_na, bias, NH, S)
    c_q = norm_fwd((z, QR, off_ql // QR), q_a_norm, name="q_a_norm")
    c_kv = norm_fwd((z, KVR, off_kvl // KVR), kv_a_norm, name="kv_a_norm")
    wuq = plain("w_uq", c_kv).reshape(QR, MH, MLA_QK)
    wuq_n = wuq[:, :, :MLA_NOPE].reshape(QR, MH * MLA_NOPE)
    wuq_r = jnp.pad(wuq[:, :, MLA_NOPE:], ((0, 0), (0, 0), (0, LANES - MLA_ROPE))).reshape(QR, MH * LANES)
    wukv = plain("w_ukv").reshape(KVR, MH, 2, HEAD_DIM)
    wuk = wukv[:, :, 0].reshape(KVR, MH * HEAD_DIM)
    wuv = wukv[:, :, 1].reshape(KVR, MH * HEAD_DIM)
    q_n = mm(c_q, wuq_n, name="mla_q_nope", out_dtype=BF16)
    q_r = rope(mm(c_q, wuq_r, name="mla_q_rope"), cos_t, sin_t, name="rope_q", out_dtype=BF16)
    k_n = mm(c_kv, wuk, name="mla_k_nope", out_dtype=BF16)
    v_m = mm(c_kv, wuv, name="mla_v", out_dtype=BF16)
    k_r = rope((z, LANES, off_kr // LANES), cos_t, sin_t, name="rope_k", out_dtype=BF16)
    o_b, lse = mla_fwd(q_n, q_r, k_n, v_m, k_r, MH, S)
    y_a = mm(o_a, stacked("w_branch_a"), name="branch_a", b_stack=True)
    y_b = mm(o_b, stacked("w_branch_b"), name="branch_b", b_stack=True)
    z_ga, z_gb = (z, D, 0), (z, D, 1)
    merged = rowwise(lambda ga, gb, ya, yb: _sig(ga) * ya + _sig(gb) * yb, [z_ga, z_gb, y_a, y_b], [], [(D, BF16)],
                     tm=256, name="merge")[0]
    token = prefetch(5, [merged])
    h2 = mm(merged, plain("w_out"), name="mix_out", res=h1, after=[token])
    h3, ffn2_saved = ffn_fwd(h2, ffn2_norm, "ffn2", "ffn2")
    n4 = norm_fwd(h3, pl_norm, name="pl_norm")
    pg_pre = mm(n4, plain("w_pl_gate", n4), name="pl_gate")
    pe = mm(ps, stacked("w_pl"), name="pl_embed", b_stack=True)

    def tail(h3b, pgb, peb, tb_, fg):
        pg = _sig(pgb)
        h4 = h3b + pg * peb
        r = _rstd(h4)
        xh = h4 * r
        err = xh * fg - tb_
        loss_rows = jnp.mean(err * err, axis=-1, keepdims=True)
        dy = err * (1.0 / D)
        dxh = dy * fg
        dh4 = r * (dxh - xh * jnp.mean(dxh * xh, axis=-1, keepdims=True))
        loss_part = jnp.broadcast_to(0.5 * jnp.sum(loss_rows, axis=0, keepdims=True), (1, LANES))
        return (dh4, dh4 * peb * pg * (1.0 - pg), dh4 * pg, loss_part, jnp.sum(dy * xh, axis=0, keepdims=True))

    dh4, dpg_pre, dpe, loss_part, g_final = rowwise(
        tail, [h3, pg_pre, pe, tgt], [final_norm.reshape(1, D)], [(D, F32), (D, BF16), (D, BF16)],
        accs=[(1, LANES), (1, D)], tm=128, name="loss_tail")
    loss = lax.psum(loss_part[0, 0], ("x", "y", "c"))

    G = {}
    pending = []

    def four(g):
        if g.ndim == 2:
            return g.reshape(N_CHIPS, 2, g.shape[0] // (2 * N_CHIPS), g.shape[1])
        return g.reshape(N_CHIPS, 2, g.shape[1] // 2, g.shape[2])

    def pair_begin(names, tag):
        g4 = [four(G[n]) for n in names]
        lands = [lax.empty((N_CHIPS,) + g.shape[2:], BF16) for g in g4]
        sems, g4, lands, token = pair_start(g4, lands, tag)
        return (names, tag, sems, g4, lands), token

    def reduce_go(begun, tag, after):
        names, sums = [], []
        for b_names, b_tag, sems, g4, lands in begun:
            g4, got = pair_wait(sems, g4, lands, list(after), b_tag)
            sums += [pair_sum(a_, r_, c_idx, n) for n, a_, r_ in zip(b_names, g4, got)]
            names += b_names
        lands = [lax.empty((N_CHIPS - 1,) + s_.shape[1:], BF16) for s_ in sums]
        sems, sums, lands, token = scatter_start(sums, lands, tag)
        pending.append((names, tag, sems, sums, lands))
        return token

    def reduce_finish(entry, after):
        names, tag, sems, sums, lands = entry
        sums, got = scatter_wait(sems, sums, lands, after, tag)
        halves = [chip_sum(a, b, j_idx, n) for n, a, b in zip(names, sums, got)]
        done = []
        for n, own, sib in zip(names, halves, half_exchange(halves, tag)):
            shp = W[n].shape
            two_d = lambda a_: a_.reshape(shp[1], shp[2])
            out = adamw_shard(two_d(W[n]), own, sib, two_d(Mo[n]), two_d(Vo[n]), c_idx, name="adamw_" + n)
            grads[n], delta[n], new_m[n], new_v[n] = [o.reshape(shp) for o in out]
            done.append(out[0])
        return done

    G["w_pl"] = mm(ps, dpe, name="pl_dw_embed", ta=True, out_dtype=BF16, out_stack=True)
    G["w_pl_gate"] = mm(n4, dpg_pre, name="pl_dw_gate", ta=True, out_dtype=BF16)
    begun_pl, token = pair_begin(["w_pl", "w_pl_gate"], "pl")
    dn4 = mm(dpg_pre, plain("w_pl_gate"), name="pl_dn", tb=True, after=[token])
    dh3, dh3_half, g_pl = norm_bwd(h3, pl_norm, dn4, name="pl_dnorm", res=dh4, bf16_alpha=0.5)
    token = reduce_go([begun_pl], "pl", [dh3])
    dh2, dh2_b, g_ffn2 = ffn_bwd(h2, ffn2_norm, ffn2_saved, dh3, dh3_half, "ffn2", "ffn2", last=False, after=[token])

    G["w_out"] = mm(merged, dh2_b, name="mix_dw_out", ta=True, out_dtype=BF16)
    dmerged = mm(dh2_b, plain("w_out"), name="mix_dmerged", tb=True)

    def merge_bwd(ga, gb, ya, yb, dm):
        sa, sb = _sig(ga), _sig(gb)
        dgates = jnp.concatenate([dm * ya * sa * (1.0 - sa), dm * yb * sb * (1.0 - sb)], axis=1)
        return dm * sa, dm * sb, dgates

    dy_a, dy_b, dz_rest = rowwise(merge_bwd, [z_ga, z_gb, y_a, y_b, dmerged], [],
                                  [(D, BF16), (D, BF16), (2 * D, BF16, 0)], tm=256, name="merge_bwd",
                                  into=(None, rest_w))
    G["w_branch_a"] = mm(o_a, dy_a, name="branch_a_dw", ta=True, out_dtype=BF16, out_stack=True)
    G["w_branch_b"] = mm(o_b, dy_b, name="branch_b_dw", ta=True, out_dtype=BF16, out_stack=True)
    do_a = mm(dy_a, stacked("w_branch_a"), name="branch_a_dx", tb=True, b_stack=True)
    do_b = mm(dy_b, stacked("w_branch_b"), name="branch_b_dx", tb=True, b_stack=True)
    dq_na, dk_na, dv_na, dbias = na_bwd(z_na, bias, do_a, NH, S)
    g_rpb = na_rpb_grad(dbias)
    dq_n, dq_rr, dk_n, dv_m, dk_rr = mla_bwd(q_n, q_r, k_n, v_m, k_r, lse, do_b, MH, S)
    dq_r = rope(dq_rr, cos_t, -sin_t, name="rope_q_bwd", out_dtype=BF16)
    dz_rest = rope(dk_rr, cos_t, -sin_t, name="rope_k_bwd", out_dtype=BF16, into=(dz_rest, rest_w),
                   cb=off_kr // kr_w, zero_cols=kr_w - LANES)
    gw_uq_n = mm(c_q, dq_n, name="mla_dw_q_nope", ta=True, out_dtype=BF16)
    gw_uq_r = mm(c_q, dq_r, name="mla_dw_q_rope", ta=True, out_dtype=BF16)
    dc_q = mm(dq_n, wuq_n, name="mla_dcq_nope", tb=True)
    dc_q = mm(dq_r, wuq_r, name="mla_dcq_rope", tb=True, res=dc_q)
    gw_uk = mm(c_kv, dk_n, name="mla_dw_k", ta=True, out_dtype=BF16)
    gw_uv = mm(c_kv, dv_m, name="mla_dw_v", ta=True, out_dtype=BF16)
    dc_kv = mm(dk_n, wuk, name="mla_dckv_k", tb=True)
    dc_kv = mm(dv_m, wuv, name="mla_dckv_v", tb=True, res=dc_kv)
    dz_rest, g_qa = norm_bwd((z, QR, off_ql // QR), q_a_norm, dc_q, name="q_a_dnorm", want_f32=False, bf16_alpha=1.0,
                             into=(dz_rest, rest_w), cb=off_ql // QR)
    dz_rest, g_kva = norm_bwd((z, KVR, off_kvl // KVR), kv_a_norm, dc_kv, name="kv_a_dnorm", want_f32=False,
                              bf16_alpha=1.0, into=(dz_rest, rest_w), cb=off_kvl // KVR)
    dz_na = jnp.concatenate([dq_na, dk_na, dv_na], axis=1)

    def to_stack(g2d):
        k, n = g2d.shape
        return g2d.reshape(k, N_CHIPS, n // N_CHIPS).transpose(1, 0, 2)

    gw_uq = jnp.concatenate([gw_uq_n.reshape(QR, MH, MLA_NOPE), gw_uq_r.reshape(QR, MH, LANES)[:, :, :MLA_ROPE]],
                            axis=2).reshape(QR, MH * MLA_QK)
    G["w_uq"] = to_stack(gw_uq)
    gw_ukv = jnp.stack([gw_uk.reshape(KVR, MH, HEAD_DIM), gw_uv.reshape(KVR, MH, HEAD_DIM)], axis=2)
    G["w_ukv"] = to_stack(gw_ukv.reshape(KVR, MH * 2 * HEAD_DIM))
    begun_mix, token = pair_begin(["w_out", "w_branch_a", "w_branch_b", "w_uq", "w_ukv"], "mix")
    gw_na = mm(u_mix, dz_na, name="mix_dw_in_na", ta=True, out_dtype=BF16, after=[token])
    gw_rest = mm(u_mix, dz_rest, name="mix_dw_in_rest", ta=True, out_dtype=BF16)
    G["w_in"] = w_in_shards(gw_na, gw_rest)
    begun_win, token_win = pair_begin(["w_in"], "win")
    token_mix = reduce_go([begun_mix], "mix", [gw_rest, token_win])
    du_mix = mm(dz_na, w_na, name="mix_du_na", tb=True, after=[token_mix])
    token_win = reduce_go([begun_win], "win", [du_mix])
    du_mix = mm(dz_rest, w_rest, name="mix_du_rest", tb=True, res=du_mix, after=[token_win])
    dh1, dh1_half, g_mix = norm_bwd(h1, mix_norm, du_mix, name="mix_dnorm", res=dh2, bf16_alpha=0.5)
    grad_x, g_ffn1 = ffn_bwd(xs, ffn1_norm, ffn1_saved, dh1, dh1_half, "ffn1", "ffn1", last=True)

    small_g = {"ffn1_norm": g_ffn1, "mix_norm": g_mix, "q_a_norm": g_qa, "kv_a_norm": g_kva, "na_rpb": g_rpb,
               "ffn2_norm": g_ffn2, "pl_norm": g_pl, "final_norm": g_final}
    sizes = [int(np.prod(W[n].shape)) for n in small]
    total = sum(sizes)
    padded = -(-total // (8 * LANES)) * (8 * LANES)

    def pack(parts):
        flat = jnp.concatenate([jnp.reshape(parts[n], (-1,)).astype(F32) for n in small]
                               + [jnp.zeros((padded - total,), F32)])
        return flat.reshape(padded // LANES, LANES)

    def unpack(a):
        flat, out, o = a.reshape(-1), {}, 0
        for n, sz in zip(small, sizes):
            out[n] = flat[o:o + sz].reshape(W[n].shape)
            o += sz
        return out

    grads, delta, new_m, new_v = {}, {}, {}, {}
    after = [grad_x]
    for entry in pending:
        after = reduce_finish(entry, after)

    g_small = sum_devices(gather_small(pack(small_g), after))
    d_small, m_small, v_small = adamw(pack(W), g_small, pack(Mo), pack(Vo), name="adamw_small")
    for full, part in ((grads, g_small), (delta, d_small), (new_m, m_small), (new_v, v_small)):
        full.update(unpack(part))

    return (loss, grad_x[None], *[grads[n] for n in order], *[delta[n] for n in order],
            *[new_m[n] for n in order], *[new_v[n] for n in order])
```
